```python
import jax, jax.numpy as jnp
from jax import lax
import numpy as np

D_MODEL = 2048
BATCH = 8
SEQ = 8192
DEPTH = 1

CTX_LEN = 256
GRID_W = 64
RET_HEADS = 8
RET_DK = 64
RET_DV = 128
RET_CHUNK = 128
ATT_HEADS = 16
ATT_KV_HEADS = 4
ATT_DH = 64
ATT_GROUP = ATT_HEADS // ATT_KV_HEADS
WINDOW = 128
ATT_BLOCK = 128
D_FF = -(-(8 * D_MODEL) // (3 * 256)) * 256
ROPE_BASE = 10000.0
NORM_EPS = 1e-6
PROJ_SIZES = (RET_HEADS * RET_DK, RET_HEADS * RET_DK, RET_HEADS * RET_DV, RET_HEADS * RET_DV,
              ATT_HEADS * ATT_DH, ATT_KV_HEADS * ATT_DH, ATT_KV_HEADS * ATT_DH)
D_PROJ = sum(PROJ_SIZES)
D_MIX_OUT = RET_HEADS * RET_DV + ATT_HEADS * ATT_DH

kernel_name = "hybrid_retention_window_gqa_dit_layer"


def _rmsnorm(x, g):
    xf = x.astype(jnp.float32)
    y = xf * lax.rsqrt(jnp.mean(xf * xf, axis=-1, keepdims=True) + NORM_EPS)
    return (y * g.astype(jnp.float32)).astype(x.dtype)


def _modulate(x, g, shift, scale):
    return _rmsnorm(x, g) * (1.0 + scale) + shift


def _split_proj(p):
    idx = [int(v) for v in np.cumsum(PROJ_SIZES)[:-1]]
    return jnp.split(p, idx, axis=-1)


def _heads(p, h, d):
    return p.reshape(p.shape[0], p.shape[1], h, d)


def _rope(x, pos):
    half = x.shape[-1] // 2
    inv = ROPE_BASE ** (-jnp.arange(half, dtype=jnp.float32) / half)
    ang = pos.astype(jnp.float32)[:, None] * inv[None, :]
    cos = jnp.cos(ang)[:, None, :]
    sin = jnp.sin(ang)[:, None, :]
    x1 = x[..., :half].astype(jnp.float32)
    x2 = x[..., half:].astype(jnp.float32)
    return jnp.concatenate([x1 * cos - x2 * sin, x1 * sin + x2 * cos], axis=-1).astype(x.dtype)


def _axial_rope(x, rows, cols):
    half = x.shape[-1] // 2
    return jnp.concatenate([_rope(x[..., :half], rows), _rope(x[..., half:], cols)], axis=-1)


def _ret_state(k, v, log_g):
    L = k.shape[1]
    w = jnp.exp(log_g[None, :] * (L - 1 - jnp.arange(L, dtype=jnp.float32))[:, None])
    return jnp.einsum('blhd,blhv->bhdv', k * w[None, :, :, None], v).astype(jnp.float32)


def _retention_chunkwise(q, k, v, log_g, s0, include_diag):
    b, L, h, dk = q.shape
    dv = v.shape[-1]
    C = RET_CHUNK
    n = L // C
    qc = q.reshape(b, n, C, h, dk)
    kc = k.reshape(b, n, C, h, dk)
    vc = v.reshape(b, n, C, h, dv)
    pos = jnp.arange(C, dtype=jnp.float32)
    rel = pos[:, None] - pos[None, :]
    mask = (rel >= 0) if include_diag else (rel > 0)
    decay_in = jnp.where(mask[None], jnp.exp(log_g[:, None, None] * jnp.maximum(rel, 0.0)[None]), 0.0)
    scores = jnp.einsum('bnihd,bnjhd->bnhij', qc, kc) * decay_in[None, None]
    o_in = jnp.einsum('bnhij,bnjhv->bnihv', scores, vc)
    k_w = jnp.exp(log_g[None, :] * (C - 1 - pos)[:, None])
    kv_chunk = jnp.einsum('bnjhd,bnjhv->nbhdv', kc * k_w[:, :, None], vc).astype(jnp.float32)
    g_chunk = jnp.exp(log_g * C)[None, :, None, None]

    def step(s, kv):
        return g_chunk * s + kv, s

    _, s_prev = lax.scan(step, s0.astype(jnp.float32), kv_chunk)
    q_w = jnp.exp(log_g[None, :] * (pos + 1.0)[:, None])
    o_x = jnp.einsum('bnihd,nbhdv->bnihv', qc * q_w[:, :, None], s_prev)
    return (o_in + o_x).reshape(b, L, h, dv)


def _bidir_retention(q, k, v, gate, lg_f, lg_b, s_f, s_b):
    o_f = _retention_chunkwise(q, k, v, lg_f, s_f, True)
    o_b = _retention_chunkwise(q[:, ::-1], k[:, ::-1], v[:, ::-1], lg_b, s_b, False)[:, ::-1]
    o = (o_f + o_b).astype(jnp.float32)
    o = o * lax.rsqrt(jnp.mean(o * o, axis=-1, keepdims=True) + NORM_EPS)
    o = o.reshape(o.shape[0], o.shape[1], RET_HEADS * RET_DV)
    return (o * jax.nn.silu(gate.astype(jnp.float32))).astype(gate.dtype)


def _window_attention(q, k, v, k_ctx, v_ctx, sink):
    b, L = q.shape[0], q.shape[1]
    Lc = k_ctx.shape[1]
    Bk = ATT_BLOCK
    n = L // Bk
    qb = q.reshape(b, n, Bk, ATT_KV_HEADS, ATT_GROUP, ATT_DH)
    pad = ((0, 0), (Bk, Bk), (0, 0), (0, 0))
    kp = jnp.pad(k, pad)
    vp = jnp.pad(v, pad)
    kb = jnp.concatenate([kp[:, j * Bk:j * Bk + L].reshape(b, n, Bk, ATT_KV_HEADS, ATT_DH) for j in range(3)], axis=2)
    vb = jnp.concatenate([vp[:, j * Bk:j * Bk + L].reshape(b, n, Bk, ATT_KV_HEADS, ATT_DH) for j in range(3)], axis=2)
    scale = ATT_DH ** -0.5
    s_loc = jnp.einsum('bnqkgd,bnskd->bnkgqs', qb, kb).astype(jnp.float32) * scale
    s_ctx = jnp.einsum('bnqkgd,bckd->bnkgqc', qb, k_ctx).astype(jnp.float32) * scale
    blk = jnp.arange(n)[:, None]
    qpos = (blk * Bk + jnp.arange(Bk)[None, :])[:, :, None]
    kpos = ((blk - 1) * Bk + jnp.arange(3 * Bk)[None, :])[:, None, :]
    valid = (jnp.abs(qpos - kpos) <= WINDOW) & (kpos >= 0) & (kpos < L)
    s_loc = jnp.where(valid[None, :, None, None], s_loc, -jnp.inf)
    sink_b = jnp.broadcast_to(sink.astype(jnp.float32).reshape(ATT_KV_HEADS, ATT_GROUP)[None, None, :, :, None, None],
                              s_loc.shape[:-1] + (1,))
    p = jax.nn.softmax(jnp.concatenate([s_loc, s_ctx, sink_b], axis=-1), axis=-1)
    p_loc = p[..., :3 * Bk].astype(v.dtype)
    p_ctx = p[..., 3 * Bk:3 * Bk + Lc].astype(v.dtype)
    o = jnp.einsum('bnkgqs,bnskd->bnqkgd', p_loc, vb) + jnp.einsum('bnkgqc,bckd->bnqkgd', p_ctx, v_ctx)
    return o.reshape(b, L, ATT_HEADS * ATT_DH)


def _context_attention(q, k, v, sink):
    b, Lc = q.shape[0], q.shape[1]
    qg = q.reshape(b, Lc, ATT_KV_HEADS, ATT_GROUP, ATT_DH)
    s = jnp.einsum('bqkgd,bckd->bkgqc', qg, k).astype(jnp.float32) * (ATT_DH ** -0.5)
    sink_b = jnp.broadcast_to(sink.astype(jnp.float32).reshape(ATT_KV_HEADS, ATT_GROUP)[None, :, :, None, None],
                              s.shape[:-1] + (1,))
    p = jax.nn.softmax(jnp.concatenate([s, sink_b], axis=-1), axis=-1)[..., :Lc].astype(v.dtype)
    o = jnp.einsum('bkgqc,bckd->bqkgd', p, v)
    return o.reshape(b, Lc, ATT_HEADS * ATT_DH)


def _swiglu(h, w_gate, w_up, w_down):
    return (jax.nn.silu(h @ w_gate) * (h @ w_up)) @ w_down


def _fwd_setup_inputs(seed: int = 0) -> dict:
    key = jax.random.key(seed)
    ks = jax.random.split(key, 18)

    def nrm(k, shape, scale):
        return jax.random.normal(k, shape, jnp.float32) * scale

    base_decay = np.log(-np.log1p(-2.0 ** (-5.0 - np.arange(RET_HEADS)))).astype(np.float32)
    return {
        "x": nrm(ks[0], (BATCH, SEQ, D_MODEL), 1.0),
        "c": nrm(ks[1], (BATCH, D_MODEL), 1.0),
        "ctx": nrm(ks[2], (BATCH, CTX_LEN, D_MODEL), 1.0),
        "c_ctx": nrm(ks[3], (D_MODEL,), 1.0),
        "w_mod": nrm(ks[4], (DEPTH, D_MODEL, 6 * D_MODEL), 0.5 * D_MODEL ** -0.5),
        "b_mod": nrm(ks[5], (DEPTH, 6 * D_MODEL), 0.02),
        "norm_mix": 1.0 + nrm(ks[6], (DEPTH, D_MODEL), 0.02),
        "norm_ffn": 1.0 + nrm(ks[7], (DEPTH, D_MODEL), 0.02),
        "w_in": nrm(ks[8], (DEPTH, D_MODEL, D_PROJ), D_MODEL ** -0.5),
        "ret_decay": jnp.asarray(base_decay)[None, None, :] + nrm(ks[9], (DEPTH, 2, RET_HEADS), 0.05),
        "attn_sink": nrm(ks[10], (DEPTH, ATT_HEADS), 0.5),
        "w_out": nrm(ks[11], (DEPTH, D_MIX_OUT, D_MODEL), D_MIX_OUT ** -0.5),
        "w_gate": nrm(ks[12], (DEPTH, D_MODEL, D_FF), D_MODEL ** -0.5),
        "w_up": nrm(ks[13], (DEPTH, D_MODEL, D_FF), D_MODEL ** -0.5),
        "w_down": nrm(ks[14], (DEPTH, D_FF, D_MODEL), D_FF ** -0.5),
        "norm_final": 1.0 + nrm(ks[15], (D_MODEL,), 0.02),
    }


def _fwd_reference(x, c, ctx, c_ctx, w_mod, b_mod, norm_mix, norm_ffn, w_in, ret_decay, attn_sink,
              w_out, w_gate, w_up, w_down, norm_final):
    L = x.shape[1]
    ROWS = L // GRID_W
    t = jnp.arange(L)
    rows = jnp.repeat(jnp.arange(ROWS), GRID_W)
    cols = jnp.tile(jnp.arange(GRID_W), ROWS)
    k_scale = RET_DK ** -0.5
    xc = ctx
    for l in range(DEPTH):
        last = l == DEPTH - 1
        mod = (jax.nn.silu(c) @ w_mod[l] + b_mod[l])[:, None, :]
        mod_c = (jax.nn.silu(c_ctx) @ w_mod[l] + b_mod[l])[None, None, :]
        sh_m, sc_m, gt_m, sh_f, sc_f, gt_f = jnp.split(mod, 6, axis=-1)
        sh_mc, sc_mc, gt_mc, sh_fc, sc_fc, gt_fc = jnp.split(mod_c, 6, axis=-1)
        lg_f = -jnp.exp(ret_decay[l, 0].astype(jnp.float32))
        lg_b = -jnp.exp(ret_decay[l, 1].astype(jnp.float32))

        hx = _modulate(x, norm_mix[l], sh_m, sc_m)
        hc = _modulate(xc, norm_mix[l], sh_mc, sc_mc)
        rq, rk, rv, rg, aq, ak, av = _split_proj(hx @ w_in[l])
        crq, crk, crv, crg, caq, cak, cav = _split_proj(hc @ w_in[l])

        crk = _heads(crk, RET_HEADS, RET_DK) * k_scale
        crv = _heads(crv, RET_HEADS, RET_DV)
        s_f = _ret_state(crk, crv, lg_f)
        s_b = _ret_state(crk[:, ::-1], crv[:, ::-1], lg_b)
        q_r = _rope(_heads(rq, RET_HEADS, RET_DK), t)
        k_r = _rope(_heads(rk, RET_HEADS, RET_DK), t) * k_scale
        v_r = _heads(rv, RET_HEADS, RET_DV)
        y_ret = _bidir_retention(q_r, k_r, v_r, rg, lg_f, lg_b, s_f, s_b)

        cak = _heads(cak, ATT_KV_HEADS, ATT_DH)
        cav = _heads(cav, ATT_KV_HEADS, ATT_DH)
        q_a = _axial_rope(_heads(aq, ATT_HEADS, ATT_DH), rows, cols)
        k_a = _axial_rope(_heads(ak, ATT_KV_HEADS, ATT_DH), rows, cols)
        v_a = _heads(av, ATT_KV_HEADS, ATT_DH)
        y_att = _window_attention(q_a, k_a, v_a, cak, cav, attn_sink[l])

        x = x + gt_m * (jnp.concatenate([y_ret, y_att], axis=-1) @ w_out[l])

        if not last:
            zero = jnp.zeros((xc.shape[0], RET_HEADS, RET_DK, RET_DV), jnp.float32)
            y_ret_c = _bidir_retention(_heads(crq, RET_HEADS, RET_DK), crk, crv, crg, lg_f, lg_b, zero, zero)
            y_att_c = _context_attention(_heads(caq, ATT_HEADS, ATT_DH), cak, cav, attn_sink[l])
            xc = xc + gt_mc * (jnp.concatenate([y_ret_c, y_att_c], axis=-1) @ w_out[l])
            xc = xc + gt_fc * _swiglu(_modulate(xc, norm_ffn[l], sh_fc, sc_fc), w_gate[l], w_up[l], w_down[l])

        x = x + gt_f * _swiglu(_modulate(x, norm_ffn[l], sh_f, sc_f), w_gate[l], w_up[l], w_down[l])
    return _rmsnorm(x, norm_final)


import jax as _jax
import jax.numpy as _jnp

TWIN_FORMAT = 'train_step'
FWD_PARAMS = ['x', 'c', 'ctx', 'c_ctx', 'w_mod', 'b_mod', 'norm_mix', 'norm_ffn', 'w_in', 'ret_decay', 'attn_sink', 'w_out', 'w_gate', 'w_up', 'w_down', 'norm_final']
TWIN_WEIGHTS = ['c_ctx', 'w_mod', 'b_mod', 'norm_mix', 'norm_ffn', 'w_in', 'ret_decay', 'attn_sink', 'w_out', 'w_gate', 'w_up', 'w_down', 'norm_final']
TWIN_DIFF_INPUT = 'x'
TWIN_INPUTS = ['x', 'c', 'ctx', 'c_ctx', 'w_mod', 'b_mod', 'norm_mix', 'norm_ffn', 'w_in', 'ret_decay', 'attn_sink', 'w_out', 'w_gate', 'w_up', 'w_down', 'norm_final', 'loss_target', 'm_c_ctx', 'm_w_mod', 'm_b_mod', 'm_norm_mix', 'm_norm_ffn', 'm_w_in', 'm_ret_decay', 'm_attn_sink', 'm_w_out', 'm_w_gate', 'm_w_up', 'm_w_down', 'm_norm_final', 'v_c_ctx', 'v_w_mod', 'v_b_mod', 'v_norm_mix', 'v_norm_ffn', 'v_w_in', 'v_ret_decay', 'v_attn_sink', 'v_w_out', 'v_w_gate', 'v_w_up', 'v_w_down', 'v_norm_final']
TWIN_OUTPUTS = ['loss', 'grad_x', 'grad_c_ctx', 'grad_w_mod', 'grad_b_mod', 'grad_norm_mix', 'grad_norm_ffn', 'grad_w_in', 'grad_ret_decay', 'grad_attn_sink', 'grad_w_out', 'grad_w_gate', 'grad_w_up', 'grad_w_down', 'grad_norm_final', 'delta_c_ctx', 'delta_w_mod', 'delta_b_mod', 'delta_norm_mix', 'delta_norm_ffn', 'delta_w_in', 'delta_ret_decay', 'delta_attn_sink', 'delta_w_out', 'delta_w_gate', 'delta_w_up', 'delta_w_down', 'delta_norm_final', 'new_m_c_ctx', 'new_m_w_mod', 'new_m_b_mod', 'new_m_norm_mix', 'new_m_norm_ffn', 'new_m_w_in', 'new_m_ret_decay', 'new_m_attn_sink', 'new_m_w_out', 'new_m_w_gate', 'new_m_w_up', 'new_m_w_down', 'new_m_norm_final', 'new_v_c_ctx', 'new_v_w_mod', 'new_v_b_mod', 'new_v_norm_mix', 'new_v_norm_ffn', 'new_v_w_in', 'new_v_ret_decay', 'new_v_attn_sink', 'new_v_w_out', 'new_v_w_gate', 'new_v_w_up', 'new_v_w_down', 'new_v_norm_final']
TWIN_LEAF_KINDS = {'loss': 'loss', 'grad_x': 'grad_x', 'grad_c_ctx': 'grad_w', 'grad_w_mod': 'grad_w', 'grad_b_mod': 'grad_w', 'grad_norm_mix': 'grad_w', 'grad_norm_ffn': 'grad_w', 'grad_w_in': 'grad_w', 'grad_ret_decay': 'grad_w', 'grad_attn_sink': 'grad_w', 'grad_w_out': 'grad_w', 'grad_w_gate': 'grad_w', 'grad_w_up': 'grad_w', 'grad_w_down': 'grad_w', 'grad_norm_final': 'grad_w', 'delta_c_ctx': 'delta_w', 'delta_w_mod': 'delta_w', 'delta_b_mod': 'delta_w', 'delta_norm_mix': 'delta_w', 'delta_norm_ffn': 'delta_w', 'delta_w_in': 'delta_w', 'delta_ret_decay': 'delta_w', 'delta_attn_sink': 'delta_w', 'delta_w_out': 'delta_w', 'delta_w_gate': 'delta_w', 'delta_w_up': 'delta_w', 'delta_w_down': 'delta_w', 'delta_norm_final': 'delta_w', 'new_m_c_ctx': 'new_m', 'new_m_w_mod': 'new_m', 'new_m_b_mod': 'new_m', 'new_m_norm_mix': 'new_m', 'new_m_norm_ffn': 'new_m', 'new_m_w_in': 'new_m', 'new_m_ret_decay': 'new_m', 'new_m_attn_sink': 'new_m', 'new_m_w_out': 'new_m', 'new_m_w_gate': 'new_m', 'new_m_w_up': 'new_m', 'new_m_w_down': 'new_m', 'new_m_norm_final': 'new_m', 'new_v_c_ctx': 'new_v', 'new_v_w_mod': 'new_v', 'new_v_b_mod': 'new_v', 'new_v_norm_mix': 'new_v', 'new_v_norm_ffn': 'new_v', 'new_v_w_in': 'new_v', 'new_v_ret_decay': 'new_v', 'new_v_attn_sink': 'new_v', 'new_v_w_out': 'new_v', 'new_v_w_gate': 'new_v', 'new_v_w_up': 'new_v', 'new_v_w_down': 'new_v', 'new_v_norm_final': 'new_v'}


def _forward(args):
    return _fwd_reference(*[args[k] for k in FWD_PARAMS])


def _output_shape():
    def fwd():
        inp = _fwd_setup_inputs(0)
        return _fwd_reference(*[inp[k] for k in FWD_PARAMS])
    out = _jax.eval_shape(fwd)
    return out.shape, out.dtype

N_MICROBATCH = 1
ADAM_LR = 0.001
ADAM_B1 = 0.9
ADAM_B2 = 0.999
ADAM_EPS = 1e-08
ADAM_WD = 0.01
ADAM_STEP = 10
PER_EXAMPLE_BATCH_AXIS = {'x': 0, 'c': 0, 'ctx': 0, 'loss_target': 0}
SHARED_INPUTS = []
_WEIGHT_DTYPES = {'c_ctx': _jnp.float32, 'w_mod': _jnp.float32, 'b_mod': _jnp.float32, 'norm_mix': _jnp.float32, 'norm_ffn': _jnp.float32, 'w_in': _jnp.float32, 'ret_decay': _jnp.float32, 'attn_sink': _jnp.float32, 'w_out': _jnp.float32, 'w_gate': _jnp.float32, 'w_up': _jnp.float32, 'w_down': _jnp.float32, 'norm_final': _jnp.float32}
MOMENT_SCALE = {'c_ctx': 1.561566e-02, 'w_mod': 3.444925e-02, 'b_mod': 6.020963e-02, 'norm_mix': 3.378472e-02, 'norm_ffn': 3.640764e-02, 'w_in': 2.504088e-02, 'ret_decay': 1.339847e-01, 'attn_sink': 1.001324e-04, 'w_out': 1.852226e-02, 'w_gate': 1.614262e-02, 'w_up': 1.559517e-02, 'w_down': 2.586637e-02, 'norm_final': 3.196016e+01}


def _to_microbatches(a, axis):
    t = _jnp.moveaxis(a, axis, 0)
    t = t.reshape((N_MICROBATCH, t.shape[0] // N_MICROBATCH) + t.shape[1:])
    return _jnp.moveaxis(t, 1, axis + 1)


def setup_inputs(seed: int = 0) -> dict:
    inp = _fwd_setup_inputs(seed)
    key = _jax.random.fold_in(_jax.random.key(seed), 7919)
    shape, _ = _output_shape()
    out = dict(inp)
    out["loss_target"] = _jax.random.normal(_jax.random.fold_in(key, 0), shape, _jnp.float32)
    for i, name in enumerate(TWIN_WEIGHTS):
        w = inp[name].astype(_jnp.float32)
        if MOMENT_SCALE is None:
            s = _jnp.sqrt(_jnp.mean(_jnp.square(w)) + 1e-30)
        else:
            s = MOMENT_SCALE[name]
        km, kv = _jax.random.split(_jax.random.fold_in(key, i + 1))
        out[name] = w
        out["m_" + name] = s * _jax.random.normal(km, w.shape, _jnp.float32)
        out["v_" + name] = (s * s) * _jax.random.uniform(kv, w.shape, _jnp.float32, 0.5, 1.5)
    if N_MICROBATCH > 1:
        for name, axis in PER_EXAMPLE_BATCH_AXIS.items():
            out[name] = _to_microbatches(out[name], axis)
    return {'x': out['x'], 'c': out['c'], 'ctx': out['ctx'], 'c_ctx': out['c_ctx'], 'w_mod': out['w_mod'], 'b_mod': out['b_mod'], 'norm_mix': out['norm_mix'], 'norm_ffn': out['norm_ffn'], 'w_in': out['w_in'], 'ret_decay': out['ret_decay'], 'attn_sink': out['attn_sink'], 'w_out': out['w_out'], 'w_gate': out['w_gate'], 'w_up': out['w_up'], 'w_down': out['w_down'], 'norm_final': out['norm_final'], 'loss_target': out['loss_target'], 'm_c_ctx': out['m_c_ctx'], 'm_w_mod': out['m_w_mod'], 'm_b_mod': out['m_b_mod'], 'm_norm_mix': out['m_norm_mix'], 'm_norm_ffn': out['m_norm_ffn'], 'm_w_in': out['m_w_in'], 'm_ret_decay': out['m_ret_decay'], 'm_attn_sink': out['m_attn_sink'], 'm_w_out': out['m_w_out'], 'm_w_gate': out['m_w_gate'], 'm_w_up': out['m_w_up'], 'm_w_down': out['m_w_down'], 'm_norm_final': out['m_norm_final'], 'v_c_ctx': out['v_c_ctx'], 'v_w_mod': out['v_w_mod'], 'v_b_mod': out['v_b_mod'], 'v_norm_mix': out['v_norm_mix'], 'v_norm_ffn': out['v_norm_ffn'], 'v_w_in': out['v_w_in'], 'v_ret_decay': out['v_ret_decay'], 'v_attn_sink': out['v_attn_sink'], 'v_w_out': out['v_w_out'], 'v_w_gate': out['v_w_gate'], 'v_w_up': out['v_w_up'], 'v_w_down': out['v_w_down'], 'v_norm_final': out['v_norm_final']}


def _loss(weights, diff, rest, loss_target):
    with _jax.named_scope("forward"):
        args = {**rest, TWIN_DIFF_INPUT: diff, **{k: w.astype(_WEIGHT_DTYPES[k]) for k, w in weights.items()}}
        y = _forward(args)
    with _jax.named_scope("loss_head"):
        err = _jnp.square(y.astype(_jnp.float32) - loss_target)
        return 0.5 * _jnp.sum(_jnp.mean(err, axis=-1)) if err.ndim else 0.5 * err


def _adamw(w, g, m, v):
    m = ADAM_B1 * m + (1.0 - ADAM_B1) * g
    v = ADAM_B2 * v + (1.0 - ADAM_B2) * _jnp.square(g)
    m_hat = m / (1.0 - ADAM_B1 ** ADAM_STEP)
    v_hat = v / (1.0 - ADAM_B2 ** ADAM_STEP)
    delta = -ADAM_LR * (m_hat / (_jnp.sqrt(v_hat) + ADAM_EPS) + ADAM_WD * w)
    return delta, m, v


def reference(x, c, ctx, c_ctx, w_mod, b_mod, norm_mix, norm_ffn, w_in, ret_decay, attn_sink, w_out, w_gate, w_up, w_down, norm_final, loss_target, m_c_ctx, m_w_mod, m_b_mod, m_norm_mix, m_norm_ffn, m_w_in, m_ret_decay, m_attn_sink, m_w_out, m_w_gate, m_w_up, m_w_down, m_norm_final, v_c_ctx, v_w_mod, v_b_mod, v_norm_mix, v_norm_ffn, v_w_in, v_ret_decay, v_attn_sink, v_w_out, v_w_gate, v_w_up, v_w_down, v_norm_final):
    given = dict(x=x, c=c, ctx=ctx, c_ctx=c_ctx, w_mod=w_mod, b_mod=b_mod, norm_mix=norm_mix, norm_ffn=norm_ffn, w_in=w_in, ret_decay=ret_decay, attn_sink=attn_sink, w_out=w_out, w_gate=w_gate, w_up=w_up, w_down=w_down, norm_final=norm_final, loss_target=loss_target, m_c_ctx=m_c_ctx, m_w_mod=m_w_mod, m_b_mod=m_b_mod, m_norm_mix=m_norm_mix, m_norm_ffn=m_norm_ffn, m_w_in=m_w_in, m_ret_decay=m_ret_decay, m_attn_sink=m_attn_sink, m_w_out=m_w_out, m_w_gate=m_w_gate, m_w_up=m_w_up, m_w_down=m_w_down, m_norm_final=m_norm_final, v_c_ctx=v_c_ctx, v_w_mod=v_w_mod, v_b_mod=v_b_mod, v_norm_mix=v_norm_mix, v_norm_ffn=v_norm_ffn, v_w_in=v_w_in, v_ret_decay=v_ret_decay, v_attn_sink=v_attn_sink, v_w_out=v_w_out, v_w_gate=v_w_gate, v_w_up=v_w_up, v_w_down=v_w_down, v_norm_final=v_norm_final)
    weights = {n: given[n] for n in TWIN_WEIGHTS}
    shared = {n: given[n] for n in SHARED_INPUTS}
    per_example = {n: given[n] for n in ['x', 'c', 'ctx']}
    grad_fn = _jax.value_and_grad(_loss, argnums=(0, 1))

    def one_microbatch(ex, loss_target):
        ex = dict(ex)
        diff = ex.pop(TWIN_DIFF_INPUT)
        return grad_fn(weights, diff, {**shared, **ex}, loss_target)

    if N_MICROBATCH == 1:
        loss, (grad_w, grad_x) = one_microbatch(per_example, given["loss_target"])
    else:
        def body(carry, xs):
            loss_sum, grad_sum = carry
            l_k, (gw_k, gx_k) = one_microbatch(xs[0], xs[1])
            with _jax.named_scope("update"):
                return (loss_sum + l_k, _jax.tree.map(_jnp.add, grad_sum, gw_k)), gx_k

        init = (_jnp.zeros((), _jnp.float32), _jax.tree.map(_jnp.zeros_like, weights))
        (loss, grad_w), grad_x = _jax.lax.scan(body, init, (per_example, given["loss_target"]))
    with _jax.named_scope("update"):
        delta_w, new_m, new_v = {}, {}, {}
        for n in TWIN_WEIGHTS:
            delta_w[n], new_m[n], new_v[n] = _adamw(weights[n], grad_w[n], given["m_" + n], given["v_" + n])
    return (loss, grad_x, *[grad_w[n] for n in TWIN_WEIGHTS], *[delta_w[n] for n in TWIN_WEIGHTS],
            *[new_m[n] for n in TWIN_WEIGHTS], *[new_v[n] for n in TWIN_WEIGHTS])
```

```python
import functools

import jax
import jax.numpy as jnp
from jax import lax
from jax.experimental import pallas as pl
from jax.experimental.pallas import tpu as pltpu

F32 = jnp.float32
BF16 = jnp.bfloat16

N_DEV = 8
LANES = 128
RET_HEADS = 8
RET_DK = 64
RET_DV = 128
CHUNK = 128
ATT_HEADS = 16
ATT_KV = 4
ATT_DH = 64
GRID_W = 64
ROPE_BASE = 10000.0
EPS = 1e-6
NEG = -1e30
C_RQ, C_RK, C_RV, C_RG, C_AQ, C_AK, C_AV, D_PROJ = 0, 512, 1024, 2048, 3072, 4096, 4352, 4608
K_SCALE = RET_DK ** -0.5
A_SCALE = ATT_DH ** -0.5

ADAM_LR, ADAM_B1, ADAM_B2, ADAM_EPS, ADAM_WD, ADAM_STEP = 0.001, 0.9, 0.999, 1e-08, 0.01, 10

VMEM_BIG = 52 * 1024 * 1024

NN = (((1,), (0,)), ((), ()))
NT = (((1,), (1,)), ((), ()))
TN = (((0,), (0,)), ((), ()))


def _dot(a, b, dims):
    return lax.dot_general(a, b, dims, preferred_element_type=F32)


def _cparams(sem, vmem=VMEM_BIG):
    return pltpu.CompilerParams(dimension_semantics=sem, vmem_limit_bytes=vmem)


def _pick(dim, prefs):
    for p in prefs:
        if dim % p == 0:
            return p
    return dim


def _my_id():
    return lax.axis_index("x") * 4 + lax.axis_index("y") * 2 + lax.axis_index("c")


def _sigmoid(x):
    return 1.0 / (1.0 + jnp.exp(-x))


def _peers():
    mx, my, mc = lax.axis_index("x"), lax.axis_index("y"), lax.axis_index("c")
    out = []
    for k in range(1, N_DEV):
        kx, ky, kc = (k >> 2) & 1, (k >> 1) & 1, k & 1
        px = 1 - mx if kx else mx
        py = 1 - my if ky else my
        pc = 1 - mc if kc else mc
        out.append(((px, py, pc), px * 4 + py * 2 + pc))
    return out


def _allgather(x, name):
    def body(x_ref, o_ref, ssem, rsem, lsem):
        me = _my_id()
        loc = pltpu.make_async_copy(x_ref, o_ref.at[me], lsem)
        loc.start()
        cps = []
        for k, (peer, _) in enumerate(_peers()):
            cp = pltpu.make_async_remote_copy(
                src_ref=x_ref, dst_ref=o_ref.at[me], send_sem=ssem.at[k], recv_sem=rsem.at[k],
                device_id=peer, device_id_type=pl.DeviceIdType.MESH)
            cp.start()
            cps.append(cp)
        for cp in cps:
            cp.wait_recv()
        for cp in cps:
            cp.wait_send()
        loc.wait()

    return pl.pallas_call(
        body, name=name,
        out_shape=jax.ShapeDtypeStruct((N_DEV,) + x.shape, x.dtype),
        in_specs=[pl.BlockSpec(memory_space=pl.ANY)],
        out_specs=pl.BlockSpec(memory_space=pl.ANY),
        scratch_shapes=[pltpu.SemaphoreType.DMA((N_DEV - 1,)), pltpu.SemaphoreType.DMA((N_DEV - 1,)),
                        pltpu.SemaphoreType.DMA(())],
    )(x)


def _alltoall(x, name):
    def body(x_ref, o_ref, ssem, rsem, lsem):
        me = _my_id()
        loc = pltpu.make_async_copy(x_ref.at[me], o_ref.at[me], lsem)
        loc.start()
        cps = []
        for k, (peer, pid) in enumerate(_peers()):
            cp = pltpu.make_async_remote_copy(
                src_ref=x_ref.at[pid], dst_ref=o_ref.at[me], send_sem=ssem.at[k], recv_sem=rsem.at[k],
                device_id=peer, device_id_type=pl.DeviceIdType.MESH)
            cp.start()
            cps.append(cp)
        for cp in cps:
            cp.wait_recv()
        for cp in cps:
            cp.wait_send()
        loc.wait()

    return pl.pallas_call(
        body, name=name,
        out_shape=jax.ShapeDtypeStruct(x.shape, x.dtype),
        in_specs=[pl.BlockSpec(memory_space=pl.ANY)],
        out_specs=pl.BlockSpec(memory_space=pl.ANY),
        scratch_shapes=[pltpu.SemaphoreType.DMA((N_DEV - 1,)), pltpu.SemaphoreType.DMA((N_DEV - 1,)),
                        pltpu.SemaphoreType.DMA(())],
    )(x)


def _matmul(name, pairs, n_acc, M, N, K, mode, tiles, extras, out_dtypes, epilogue, j_outer=False):
    tm, tn, tk = tiles
    gm, gn, nk = M // tm, N // tn, K // tk
    assert gm * tm == M and gn * tn == N and nk * tk == K, (name, M, N, K, tiles)
    if j_outer:
        grid = (gn, gm, nk)
        ij = lambda g0, g1: (g1, g0)
    else:
        grid = (gm, gn, nk)
        ij = lambda g0, g1: (g0, g1)

    if mode in ("nn", "nt"):
        a_spec = pl.BlockSpec((tm, tk), lambda g0, g1, k: (ij(g0, g1)[0], k))
    else:
        a_spec = pl.BlockSpec((tk, tm), lambda g0, g1, k: (k, ij(g0, g1)[0]))
    if mode == "nt":
        b_spec = pl.BlockSpec((tn, tk), lambda g0, g1, k: (ij(g0, g1)[1], k))
    else:
        b_spec = pl.BlockSpec((tk, tn), lambda g0, g1, k: (k, ij(g0, g1)[1]))
    dims = {"nn": NN, "nt": NT, "tn": TN}[mode]
    mn_spec = pl.BlockSpec((tm, tn), lambda g0, g1, k: ij(g0, g1))
    n_spec = pl.BlockSpec((1, tn), lambda g0, g1, k: (0, ij(g0, g1)[1]))

    in_specs, args = [], []
    for a, b, _ in pairs:
        in_specs += [a_spec, b_spec]
        args += [a, b]
    for arr, kind in extras:
        in_specs.append(mn_spec if kind == "mn" else n_spec)
        args.append(arr)
    n_p, n_e, n_o = len(pairs), len(extras), len(out_dtypes)

    def body(*refs):
        ab = refs[:2 * n_p]
        ex = refs[2 * n_p:2 * n_p + n_e]
        outs = refs[2 * n_p + n_e:2 * n_p + n_e + n_o]
        accs = refs[2 * n_p + n_e + n_o:]
        k = pl.program_id(2)

        def partial_sums():
            sums = [None] * n_acc
            for p, (_, _, ai) in enumerate(pairs):
                d = _dot(ab[2 * p][...], ab[2 * p + 1][...], dims)
                sums[ai] = d if sums[ai] is None else sums[ai] + d
            return sums

        def finish(acc_vals):
            res = epilogue(acc_vals, [e[...] for e in ex])
            for o, r in zip(outs, res):
                o[...] = r.astype(o.dtype)

        if nk == 1:
            finish(partial_sums())
        else:
            @pl.when(k == 0)
            def _():
                for ai, s in enumerate(partial_sums()):
                    accs[ai][...] = s

            @pl.when(k > 0)
            def _():
                for ai, s in enumerate(partial_sums()):
                    accs[ai][...] += s

            @pl.when(k == nk - 1)
            def _():
                finish([a[...] for a in accs])

    return pl.pallas_call(
        body, name=name, grid=grid,
        in_specs=in_specs,
        out_specs=[mn_spec] * n_o,
        out_shape=[jax.ShapeDtypeStruct((M, N), dt) for dt in out_dtypes],
        scratch_shapes=[pltpu.VMEM((tm, tn), F32) for _ in range(n_acc if nk > 1 else 0)],
        compiler_params=_cparams(("parallel", "parallel", "arbitrary")),
    )(*args)


def _rope_tables(L):
    t = jnp.arange(L, dtype=jnp.int32)
    lane = jnp.arange(LANES, dtype=jnp.int32)
    hl = lane % 64
    f = (hl % 32).astype(F32)
    inv = ROPE_BASE ** (-f / 32.0)
    ang = t.astype(F32)[:, None] * inv[None, :]
    sgn = jnp.where(hl < 32, -1.0, 1.0)[None, :]
    ret = jnp.stack([jnp.cos(ang), jnp.sin(ang) * sgn])
    q = hl % 32
    f2 = (q % 16).astype(F32)
    inv2 = ROPE_BASE ** (-f2 / 16.0)
    pos = jnp.where((hl < 32)[None, :], (t // GRID_W)[:, None], (t % GRID_W)[:, None]).astype(F32)
    ang2 = pos * inv2[None, :]
    sgn2 = jnp.where(q < 16, -1.0, 1.0)[None, :]
    att = jnp.stack([jnp.cos(ang2), jnp.sin(ang2) * sgn2])
    return ret.astype(F32), att.astype(F32)


def _swap(x, sh):
    lane = lax.broadcasted_iota(jnp.int32, x.shape, 1)
    ra = pltpu.roll(x, LANES - sh, 1)
    rb = pltpu.roll(x, sh, 1)
    la = pltpu.roll(lane, LANES - sh, 1)
    partner = jnp.where((lane % (2 * sh)) < sh, lane + sh, lane - sh)
    return jnp.where(la == partner, ra, rb)


def _rope(x, cos, sin, sh):
    return x * cos + _swap(x, sh) * sin


def _rope_t(d, cos, sin, sh):
    return d * cos + _swap(d * sin, sh)


def _half_mask(shape, a):
    lane = lax.broadcasted_iota(jnp.int32, shape, 1)
    return (lane < 64) if a == 0 else (lane >= 64)


def _mod_fwd(s_in, w_l, b_l):
    D, C6 = w_l.shape
    tk = _pick(D, (512, 256, 128))
    nk = D // tk

    def body(s_ref, w_ref, b_ref, o_ref):
        k = pl.program_id(0)
        s = s_ref[...]
        s = s * _sigmoid(s)
        d = jnp.dot(s, w_ref[...], preferred_element_type=F32, precision=lax.Precision.HIGHEST)

        @pl.when(k == 0)
        def _():
            o_ref[...] = d + b_ref[...]

        @pl.when(k > 0)
        def _():
            o_ref[...] += d

    return pl.pallas_call(
        body, name="mod_fwd", grid=(nk,),
        in_specs=[pl.BlockSpec((16, tk), lambda k: (0, k)), pl.BlockSpec((tk, C6), lambda k: (k, 0)),
                  pl.BlockSpec((1, C6), lambda k: (0, 0))],
        out_specs=pl.BlockSpec((16, C6), lambda k: (0, 0)),
        out_shape=jax.ShapeDtypeStruct((16, C6), F32),
        compiler_params=_cparams(("arbitrary",)),
    )(s_in, w_l, b_l)


def _mod_bwd(s_in, dm, w_l):
    D, C6 = w_l.shape
    tk = _pick(D, (512, 256, 128))
    nk = D // tk

    def body(s_ref, dm_ref, w_ref, gw_ref, gc_ref):
        s = s_ref[...]
        sg = _sigmoid(s)
        act = s * sg
        dmv = dm_ref[...]
        gw_ref[...] = lax.dot_general(act, dmv, TN, preferred_element_type=F32, precision=lax.Precision.HIGHEST)
        ds = lax.dot_general(dmv, w_ref[...], NT, preferred_element_type=F32, precision=lax.Precision.HIGHEST)
        dsil = (sg * (1.0 + s * (1.0 - sg)))[8:9, :]
        gc_ref[...] = jnp.zeros((8, tk), F32) + jnp.sum(ds[8:16, :], axis=0, keepdims=True) * dsil

    return pl.pallas_call(
        body, name="mod_bwd", grid=(nk,),
        in_specs=[pl.BlockSpec((16, tk), lambda k: (0, k)), pl.BlockSpec((16, C6), lambda k: (0, 0)),
                  pl.BlockSpec((tk, C6), lambda k: (k, 0))],
        out_specs=[pl.BlockSpec((tk, C6), lambda k: (k, 0)), pl.BlockSpec((8, tk), lambda k: (0, k))],
        out_shape=[jax.ShapeDtypeStruct((D, C6), F32), jax.ShapeDtypeStruct((8, D), F32)],
        compiler_params=_cparams(("parallel",)),
    )(s_in, dm, w_l)


def _norm_rows(x):
    r = lax.rsqrt(jnp.mean(x * x, axis=-1, keepdims=True) + EPS)
    return x * r, r


def _modulate_fwd(name, x, ctx, g, mod, modc):
    L, D = x.shape
    tr = ctx.shape[0] if ctx is not None else _pick(L, (256, 128))
    nx = L // tr
    nt = nx + (1 if ctx is not None else 0)

    def body(*refs):
        if ctx is not None:
            x_ref, c_ref, g_ref, m_ref, mc_ref, o_ref = refs
        else:
            x_ref, g_ref, m_ref, o_ref = refs
        i = pl.program_id(0)

        def run(src, m):
            n, _ = _norm_rows(src[...])
            o_ref[...] = (n * g_ref[...] * (1.0 + m[1:2, :]) + m[0:1, :]).astype(o_ref.dtype)

        if ctx is None:
            run(x_ref, m_ref)
        else:
            @pl.when(i < nx)
            def _():
                run(x_ref, m_ref)

            @pl.when(i >= nx)
            def _():
                run(c_ref, mc_ref)

    row = pl.BlockSpec((tr, D), lambda i: (jnp.minimum(i, nx - 1), 0))
    vec = pl.BlockSpec((1, D), lambda i: (0, 0))
    mv = pl.BlockSpec((8, D), lambda i: (0, 0))
    if ctx is not None:
        in_specs = [row, pl.BlockSpec((tr, D), lambda i: (0, 0)), vec, mv, mv]
        args = (x, ctx, g, mod, modc)
    else:
        in_specs = [row, vec, mv]
        args = (x, g, mod)
    return pl.pallas_call(
        body, name=name, grid=(nt,), in_specs=in_specs,
        out_specs=pl.BlockSpec((tr, D), lambda i: (i, 0)),
        out_shape=jax.ShapeDtypeStruct((nt * tr, D), BF16),
        compiler_params=_cparams(("parallel",)),
    )(*args)


def _modulate_bwd(name, x, ctx, dh, g, mod, modc, dres, fbr, gate):
    L, D = x.shape
    tr = ctx.shape[0] if ctx is not None else _pick(L, (256, 128))
    nx = L // tr
    nt = nx + (1 if ctx is not None else 0)
    has_f = fbr is not None

    def body(*refs):
        refs = list(refs)
        x_ref = refs.pop(0)
        c_ref = refs.pop(0) if ctx is not None else None
        dh_ref, g_ref, m_ref = refs.pop(0), refs.pop(0), refs.pop(0)
        mc_ref = refs.pop(0) if ctx is not None else None
        dr_ref = refs.pop(0)
        f_ref = refs.pop(0) if has_f else None
        gt_ref = refs.pop(0) if has_f else None
        dx_ref = refs.pop(0)
        df_ref = refs.pop(0) if has_f else None
        acc_ref = refs.pop(0)
        i = pl.program_id(0)

        @pl.when(i == 0)
        def _():
            acc_ref[...] = jnp.zeros_like(acc_ref)

        def sums(src, m, base, grow):
            n, r = _norm_rows(src[...])
            d = dh_ref[...].astype(F32)
            gg = g_ref[...]
            sc1 = 1.0 + m[1:2, :]
            acc_ref[base:base + 1, :] += jnp.sum(d, axis=0, keepdims=True)
            dn = d * n
            acc_ref[base + 1:base + 2, :] += jnp.sum(dn, axis=0, keepdims=True) * gg
            acc_ref[grow:grow + 1, :] += jnp.sum(dn, axis=0, keepdims=True) * sc1
            dnv = d * (gg * sc1)
            return r * (dnv - n * jnp.mean(dnv * n, axis=-1, keepdims=True))

        def x_rows():
            dx = sums(x_ref, m_ref, 0, 2) + dr_ref[...]
            dx_ref[...] = dx
            if has_f:
                acc_ref[6:7, :] += jnp.sum(dx * f_ref[...].astype(F32), axis=0, keepdims=True)
                df_ref[...] = (dx * gt_ref[...]).astype(df_ref.dtype)

        if ctx is None:
            x_rows()
        else:
            pl.when(i < nx)(x_rows)

            @pl.when(i >= nx)
            def _():
                sums(c_ref, mc_ref, 3, 2)

    row = pl.BlockSpec((tr, D), lambda i: (jnp.minimum(i, nx - 1), 0))
    vec = pl.BlockSpec((1, D), lambda i: (0, 0))
    mv = pl.BlockSpec((8, D), lambda i: (0, 0))
    in_specs, args = [row], [x]
    if ctx is not None:
        in_specs.append(pl.BlockSpec((tr, D), lambda i: (0, 0)))
        args.append(ctx)
    in_specs += [pl.BlockSpec((tr, D), lambda i: (i, 0)), vec, mv]
    args += [dh, g, mod]
    if ctx is not None:
        in_specs.append(mv)
        args.append(modc)
    in_specs.append(row)
    args.append(dres)
    out_specs = [row]
    out_shape = [jax.ShapeDtypeStruct((L, D), F32)]
    if has_f:
        in_specs += [row, vec]
        args += [fbr, gate]
        out_specs.append(row)
        out_shape.append(jax.ShapeDtypeStruct((L, D), BF16))
    out_specs.append(pl.BlockSpec((16, D), lambda i: (0, 0)))
    out_shape.append(jax.ShapeDtypeStruct((16, D), F32))
    return pl.pallas_call(
        body, name=name, grid=(nt,), in_specs=in_specs, out_specs=out_specs, out_shape=out_shape,
        compiler_params=_cparams(("arbitrary",)),
    )(*args)


def _loss_head(x2, tgt, nf, fbr, gate):
    L, D = x2.shape
    tr = _pick(L, (256, 128))

    def body(x_ref, t_ref, w_ref, f_ref, gt_ref, dx_ref, df_ref, acc_ref):
        i = pl.program_id(0)

        @pl.when(i == 0)
        def _():
            acc_ref[...] = jnp.zeros_like(acc_ref)

        n, r = _norm_rows(x_ref[...])
        w = w_ref[...]
        e = n * w - t_ref[...]
        acc_ref[0:1, :] += jnp.sum(e * e, axis=0, keepdims=True) * (0.5 / D)
        dout = e * (1.0 / D)
        acc_ref[1:2, :] += jnp.sum(dout * n, axis=0, keepdims=True)
        dn = dout * w
        dx = r * (dn - n * jnp.mean(dn * n, axis=-1, keepdims=True))
        dx_ref[...] = dx
        acc_ref[2:3, :] += jnp.sum(dx * f_ref[...].astype(F32), axis=0, keepdims=True)
        df_ref[...] = (dx * gt_ref[...]).astype(df_ref.dtype)

        @pl.when(i == pl.num_programs(0) - 1)
        def _():
            acc_ref[3:4, :] = jnp.zeros((1, D), F32) + jnp.sum(acc_ref[0:1, :])

    row = pl.BlockSpec((tr, D), lambda i: (i, 0))
    vec = pl.BlockSpec((1, D), lambda i: (0, 0))
    return pl.pallas_call(
        body, name="loss_head", grid=(L // tr,),
        in_specs=[row, row, vec, row, vec],
        out_specs=[row, row, pl.BlockSpec((8, D), lambda i: (0, 0))],
        out_shape=[jax.ShapeDtypeStruct((L, D), F32), jax.ShapeDtypeStruct((L, D), BF16),
                   jax.ShapeDtypeStruct((8, D), F32)],
        compiler_params=_cparams(("arbitrary",)),
    )(x2, tgt, nf, fbr, gate)


N_TAB = 7


def _ret_tables(rdb, Lc):
    def body(rd_ref, t_ref, c_ref):
        d = pl.program_id(0) // RET_HEADS
        fwd = d == 0
        lg = -jnp.exp(rd_ref[0])
        i = lax.broadcasted_iota(jnp.int32, (CHUNK, CHUNK), 0).astype(F32)
        j = lax.broadcasted_iota(jnp.int32, (CHUNK, CHUNK), 1).astype(F32)
        rel = jnp.where(fwd, i - j, j - i)
        mask = (rel > 0.0) | ((rel == 0.0) & fwd)
        dm = jnp.where(mask, jnp.exp(lg * jnp.maximum(rel, 0.0)), 0.0)
        t_ref[0, 0] = dm
        t_ref[0, 1] = rel * dm
        qc = jnp.where(fwd, i + 1.0, CHUNK - i)
        qw = jnp.exp(lg * qc)
        t_ref[0, 2] = qw
        t_ref[0, 3] = qw * qc
        kc = jnp.where(fwd, CHUNK - 1.0 - i, i)
        kw = jnp.exp(lg * kc)
        t_ref[0, 4] = kw
        t_ref[0, 5] = kw * kc
        t_ref[0, 6] = jnp.exp(lg * float(CHUNK)) + jnp.zeros((CHUNK, CHUNK), F32)
        m = lax.broadcasted_iota(jnp.int32, (Lc, LANES), 0).astype(F32)
        cc = jnp.where(fwd, Lc - 1.0 - m, m)
        cw = jnp.exp(lg * cc)
        c_ref[0, 0] = cw
        c_ref[0, 1] = cw * cc

    return pl.pallas_call(
        body, name="ret_tables", grid=(2 * RET_HEADS,),
        in_specs=[pl.BlockSpec((1, 1, LANES), lambda r: (r, 0, 0))],
        out_specs=[pl.BlockSpec((1, N_TAB, CHUNK, CHUNK), lambda r: (r, 0, 0, 0)),
                   pl.BlockSpec((1, 2, Lc, LANES), lambda r: (r, 0, 0, 0))],
        out_shape=[jax.ShapeDtypeStruct((2 * RET_HEADS, N_TAB, CHUNK, CHUNK), F32),
                   jax.ShapeDtypeStruct((2 * RET_HEADS, 2, Lc, LANES), F32)],
        compiler_params=_cparams(("parallel",)),
    )(rdb)


def _ret_ctx_state(P, ctab, L, Lc):
    cb = L // Lc

    def body(k_ref, v_ref, c_ref, s_ref):
        for p in range(RET_HEADS // 2):
            kp = k_ref[:, p * LANES:(p + 1) * LANES].astype(F32) * K_SCALE
            for a in range(2):
                h = 2 * p + a
                kh = jnp.where(_half_mask(kp.shape, a), kp, 0.0)
                vh = v_ref[:, h * RET_DV:(h + 1) * RET_DV]
                for d in range(2):
                    kw = (kh * c_ref[d * RET_HEADS + h, 0]).astype(BF16)
                    s_ref[d * RET_HEADS + h] = _dot(kw, vh, TN)

    return pl.pallas_call(
        body, name="ret_ctx_state", grid=(1,),
        in_specs=[pl.BlockSpec((Lc, 512), lambda i: (cb, C_RK // 512)),
                  pl.BlockSpec((Lc, 1024), lambda i: (cb, C_RV // 1024)),
                  pl.BlockSpec((2 * RET_HEADS, 2, Lc, LANES), lambda i: (0, 0, 0, 0))],
        out_specs=pl.BlockSpec((2 * RET_HEADS, LANES, RET_DV), lambda i: (0, 0, 0)),
        out_shape=jax.ShapeDtypeStruct((2 * RET_HEADS, LANES, RET_DV), F32),
        compiler_params=_cparams(("arbitrary",)),
    )(P, P, ctab)


def _ret_fwd(P, rope, tabs, s0, L):
    n = L // CHUNK

    def body(qf, kf, vf, rf, qb, kb, vb, rb, t_ref, s0_ref, of_ref, ob_ref, stf_ref, stb_ref, st):
        s = pl.program_id(0)

        @pl.when(s == 0)
        def _():
            st[...] = s0_ref[...]

        for d, (q_ref, k_ref, v_ref, r_ref, o_ref, so_ref) in enumerate(
                ((qf, kf, vf, rf, of_ref, stf_ref), (qb, kb, vb, rb, ob_ref, stb_ref))):
            cos, sin = r_ref[0], r_ref[1]
            for p in range(RET_HEADS // 2):
                qp = _rope(q_ref[:, p * LANES:(p + 1) * LANES].astype(F32), cos, sin, 32)
                kp = _rope(k_ref[:, p * LANES:(p + 1) * LANES].astype(F32), cos, sin, 32) * K_SCALE
                for a in range(2):
                    h = 2 * p + a
                    r = d * RET_HEADS + h
                    hm = _half_mask(qp.shape, a)
                    qh = jnp.where(hm, qp, 0.0)
                    kh = jnp.where(hm, kp, 0.0)
                    vh = v_ref[:, h * RET_DV:(h + 1) * RET_DV]
                    sp = st[r]
                    so_ref[0, h] = sp[a * RET_DK:(a + 1) * RET_DK, :]
                    sc = _dot(qh.astype(BF16), kh.astype(BF16), NT) * t_ref[r, 0]
                    o = _dot(sc.astype(BF16), vh, NN)
                    o += _dot((qh * t_ref[r, 2]).astype(BF16), sp.astype(BF16), NN)
                    o_ref[:, h * RET_DV:(h + 1) * RET_DV] = o
                    st[r] = t_ref[r, 6] * sp + _dot((kh * t_ref[r, 4]).astype(BF16), vh, TN)

    fw = lambda s: s
    bw = lambda s: n - 1 - s

    def specs(cm):
        return [pl.BlockSpec((CHUNK, 512), lambda s: (cm(s), C_RQ // 512)),
                pl.BlockSpec((CHUNK, 512), lambda s: (cm(s), C_RK // 512)),
                pl.BlockSpec((CHUNK, 1024), lambda s: (cm(s), C_RV // 1024)),
                pl.BlockSpec((2, CHUNK, LANES), lambda s: (0, cm(s), 0))]

    full = lambda shp: pl.BlockSpec(shp, lambda s: (0,) * len(shp))
    return pl.pallas_call(
        body, name="ret_fwd", grid=(n,),
        in_specs=specs(fw) + specs(bw) + [full((2 * RET_HEADS, N_TAB, CHUNK, CHUNK)),
                                          full((2 * RET_HEADS, LANES, RET_DV))],
        out_specs=[pl.BlockSpec((CHUNK, 1024), lambda s: (fw(s), 0)),
                   pl.BlockSpec((CHUNK, 1024), lambda s: (bw(s), 0)),
                   pl.BlockSpec((1, RET_HEADS, RET_DK, RET_DV), lambda s: (fw(s), 0, 0, 0)),
                   pl.BlockSpec((1, RET_HEADS, RET_DK, RET_DV), lambda s: (bw(s), 0, 0, 0))],
        out_shape=[jax.ShapeDtypeStruct((L, 1024), F32), jax.ShapeDtypeStruct((L, 1024), F32),
                   jax.ShapeDtypeStruct((n, RET_HEADS, RET_DK, RET_DV), F32),
                   jax.ShapeDtypeStruct((n, RET_HEADS, RET_DK, RET_DV), F32)],
        scratch_shapes=[pltpu.VMEM((2 * RET_HEADS, LANES, RET_DV), F32)],
        compiler_params=_cparams(("arbitrary",)),
    )(P, P, P, rope, P, P, P, rope, tabs, s0)


def _ret_finish_fwd(of, ob, P, L):
    tr = _pick(L, (256, 128))

    def body(f_ref, b_ref, g_ref, y_ref):
        for h in range(RET_HEADS):
            sl = slice(h * RET_DV, (h + 1) * RET_DV)
            n, _ = _norm_rows(f_ref[:, sl] + b_ref[:, sl])
            g = g_ref[:, sl].astype(F32)
            y_ref[:, sl] = (n * (g * _sigmoid(g))).astype(y_ref.dtype)

    row = pl.BlockSpec((tr, 1024), lambda i: (i, 0))
    return pl.pallas_call(
        body, name="ret_finish_fwd", grid=(L // tr,),
        in_specs=[row, row, pl.BlockSpec((tr, 1024), lambda i: (i, C_RG // 1024))],
        out_specs=row, out_shape=jax.ShapeDtypeStruct((L, 1024), BF16),
        compiler_params=_cparams(("parallel",)),
    )(of, ob, P)


def _ret_finish_bwd(of, ob, P, dY, L):
    tr = _pick(L, (256, 128))

    def body(f_ref, b_ref, g_ref, dy_ref, do_ref, dg_ref):
        for h in range(RET_HEADS):
            sl = slice(h * RET_DV, (h + 1) * RET_DV)
            n, r = _norm_rows(f_ref[:, sl] + b_ref[:, sl])
            g = g_ref[:, sl].astype(F32)
            sg = _sigmoid(g)
            dy = dy_ref[:, sl].astype(F32)
            dg_ref[:, sl] = (dy * n * (sg * (1.0 + g * (1.0 - sg)))).astype(dg_ref.dtype)
            dn = dy * (g * sg)
            do_ref[:, sl] = (r * (dn - n * jnp.mean(dn * n, axis=-1, keepdims=True))).astype(do_ref.dtype)

    row = pl.BlockSpec((tr, 1024), lambda i: (i, 0))
    return pl.pallas_call(
        body, name="ret_finish_bwd", grid=(L // tr,),
        in_specs=[row, row, pl.BlockSpec((tr, 1024), lambda i: (i, C_RG // 1024)), row],
        out_specs=[row, row],
        out_shape=[jax.ShapeDtypeStruct((L, 1024), BF16), jax.ShapeDtypeStruct((L, 1024), BF16)],
        compiler_params=_cparams(("parallel",)),
    )(of, ob, P, dY)


def _ret_bwd(P, rope, tabs, stf, stb, dO, L):
    n = L // CHUNK

    def body(qf, kf, vf, rf, gf, sf, qb, kb, vb, rb, gb, sb, t_ref,
             dqf, dkf, dvf, dqb, dkb, dvb, ds0_ref, dlg_ref, ds):
        s = pl.program_id(0)

        @pl.when(s == 0)
        def _():
            ds[...] = jnp.zeros_like(ds)
            dlg_ref[...] = jnp.zeros_like(dlg_ref)

        for d, (q_ref, k_ref, v_ref, r_ref, g_ref, s_ref, dq_ref, dk_ref, dv_ref) in enumerate(
                ((qf, kf, vf, rf, gf, sf, dqf, dkf, dvf), (qb, kb, vb, rb, gb, sb, dqb, dkb, dvb))):
            cos, sin = r_ref[0], r_ref[1]
            for p in range(RET_HEADS // 2):
                qp = _rope(q_ref[:, p * LANES:(p + 1) * LANES].astype(F32), cos, sin, 32)
                kp = _rope(k_ref[:, p * LANES:(p + 1) * LANES].astype(F32), cos, sin, 32) * K_SCALE
                dqp = jnp.zeros((CHUNK, LANES), F32)
                dkp = jnp.zeros((CHUNK, LANES), F32)
                for a in range(2):
                    h = 2 * p + a
                    r = d * RET_HEADS + h
                    hm = _half_mask(qp.shape, a)
                    qh = jnp.where(hm, qp, 0.0)
                    kh = jnp.where(hm, kp, 0.0)
                    qhb, khb = qh.astype(BF16), kh.astype(BF16)
                    vh = v_ref[:, h * RET_DV:(h + 1) * RET_DV]
                    gh = g_ref[:, h * RET_DV:(h + 1) * RET_DV]
                    zero = jnp.zeros((RET_DK, RET_DV), F32)
                    sp = s_ref[0, h]
                    sp = jnp.concatenate([sp, zero] if a == 0 else [zero, sp], axis=0)
                    dsn = ds[r]
                    dm, rm = t_ref[r, 0], t_ref[r, 1]
                    qw, qwc, kw, kwc, gch = t_ref[r, 2], t_ref[r, 3], t_ref[r, 4], t_ref[r, 5], t_ref[r, 6]
                    am = _dot(qhb, khb, NT)
                    dar = _dot(gh, vh, NT)
                    da = (dar * dm).astype(BF16)
                    xq = _dot(gh, sp.astype(BF16), NT)
                    yk = _dot(vh, dsn.astype(BF16), NT)
                    dqp += _dot(da, khb, NN) + xq * qw
                    dkp += _dot(da, qhb, TN) + yk * kw
                    dvh = _dot((am * dm).astype(BF16), gh, TN) + _dot((kh * kw).astype(BF16), dsn.astype(BF16), NN)
                    dv_ref[:, h * RET_DV:(h + 1) * RET_DV] = dvh
                    part = (jnp.sum(am * dar * rm) + jnp.sum(qh * qwc * xq) + jnp.sum(kh * kwc * yk)
                            + float(CHUNK) * jnp.sum(gch * dsn * sp))
                    dlg_ref[r:r + 1, :] += jnp.zeros((1, LANES), F32) + part
                    ds[r] = gch * dsn + _dot((qh * qw).astype(BF16), gh, TN)
                dq_ref[:, p * LANES:(p + 1) * LANES] = _rope_t(dqp, cos, sin, 32)
                dk_ref[:, p * LANES:(p + 1) * LANES] = _rope_t(dkp * K_SCALE, cos, sin, 32)

        @pl.when(s == n - 1)
        def _():
            ds0_ref[...] = ds[...]

    fw = lambda s: n - 1 - s
    bw = lambda s: s

    def specs(cm):
        return [pl.BlockSpec((CHUNK, 512), lambda s: (cm(s), C_RQ // 512)),
                pl.BlockSpec((CHUNK, 512), lambda s: (cm(s), C_RK // 512)),
                pl.BlockSpec((CHUNK, 1024), lambda s: (cm(s), C_RV // 1024)),
                pl.BlockSpec((2, CHUNK, LANES), lambda s: (0, cm(s), 0)),
                pl.BlockSpec((CHUNK, 1024), lambda s: (cm(s), 0)),
                pl.BlockSpec((1, RET_HEADS, RET_DK, RET_DV), lambda s: (cm(s), 0, 0, 0))]

    def ospecs(cm):
        return [pl.BlockSpec((CHUNK, 512), lambda s: (cm(s), 0)), pl.BlockSpec((CHUNK, 512), lambda s: (cm(s), 0)),
                pl.BlockSpec((CHUNK, 1024), lambda s: (cm(s), 0))]

    oshape = [jax.ShapeDtypeStruct((L, 512), F32), jax.ShapeDtypeStruct((L, 512), F32),
              jax.ShapeDtypeStruct((L, 1024), F32)]
    full = lambda shp: pl.BlockSpec(shp, lambda s: (0,) * len(shp))
    return pl.pallas_call(
        body, name="ret_bwd", grid=(n,),
        in_specs=specs(fw) + specs(bw) + [full((2 * RET_HEADS, N_TAB, CHUNK, CHUNK))],
        out_specs=ospecs(fw) + ospecs(bw) + [full((2 * RET_HEADS, LANES, RET_DV)), full((2 * RET_HEADS, LANES))],
        out_shape=oshape + oshape + [jax.ShapeDtypeStruct((2 * RET_HEADS, LANES, RET_DV), F32),
                                     jax.ShapeDtypeStruct((2 * RET_HEADS, LANES), F32)],
        scratch_shapes=[pltpu.VMEM((2 * RET_HEADS, LANES, RET_DV), F32)],
        compiler_params=_cparams(("arbitrary",)),
    )(P, P, P, rope, dO, stf, P, P, P, rope, dO, stb, tabs)


def _ret_ctx_bwd(P, ctab, ds0, dlg, rdb, L, Lc):
    cb = L // Lc

    def body(k_ref, v_ref, c_ref, ds_ref, dlg_ref, rd_ref, dk_ref, dv_ref, drd_ref):
        for p in range(RET_HEADS // 2):
            kp = k_ref[:, p * LANES:(p + 1) * LANES].astype(F32) * K_SCALE
            dkp = jnp.zeros((Lc, LANES), F32)
            for a in range(2):
                h = 2 * p + a
                kh = jnp.where(_half_mask(kp.shape, a), kp, 0.0)
                vh = v_ref[:, h * RET_DV:(h + 1) * RET_DV]
                dvh = jnp.zeros((Lc, RET_DV), F32)
                for d in range(2):
                    r = d * RET_HEADS + h
                    dsb = ds_ref[r].astype(BF16)
                    cw, cwc = c_ref[r, 0], c_ref[r, 1]
                    y = _dot(vh, dsb, NT)
                    dkp += y * cw
                    dvh += _dot((kh * cw).astype(BF16), dsb, NN)
                    lg = -jnp.exp(rd_ref[r])
                    drd_ref[r:r + 1, :] = (dlg_ref[r:r + 1, :] + jnp.sum(kh * cwc * y)) * lg
                dv_ref[:, h * RET_DV:(h + 1) * RET_DV] = dvh
            dk_ref[:, p * LANES:(p + 1) * LANES] = dkp * K_SCALE

    full = lambda shp: pl.BlockSpec(shp, lambda i: (0,) * len(shp))
    return pl.pallas_call(
        body, name="ret_ctx_bwd", grid=(1,),
        in_specs=[pl.BlockSpec((Lc, 512), lambda i: (cb, C_RK // 512)),
                  pl.BlockSpec((Lc, 1024), lambda i: (cb, C_RV // 1024)),
                  full((2 * RET_HEADS, 2, Lc, LANES)), full((2 * RET_HEADS, LANES, RET_DV)),
                  full((2 * RET_HEADS, LANES)), full((2 * RET_HEADS, 1, LANES))],
        out_specs=[full((Lc, 512)), full((Lc, 1024)), full((2 * RET_HEADS, LANES))],
        out_shape=[jax.ShapeDtypeStruct((Lc, 512), F32), jax.ShapeDtypeStruct((Lc, 1024), F32),
                   jax.ShapeDtypeStruct((2 * RET_HEADS, LANES), F32)],
        compiler_params=_cparams(("arbitrary",)),
    )(P, P, ctab, ds0, dlg, rdb)


BLK = 128
N_LOC = 3 * BLK


def _att_inputs(P, rope, L, Lc):
    n = L // BLK
    cb = L // Lc
    prev = lambda i: jnp.maximum(i - 1, 0)
    nxt = lambda i: jnp.minimum(i + 1, n - 1)
    specs = [pl.BlockSpec((BLK, 1024), lambda i: (i, C_AQ // 1024))]
    args = [P]
    for col in (C_AK // 256, C_AV // 256):
        for rm in (prev, lambda i: i, nxt):
            specs.append(pl.BlockSpec((BLK, 256), functools.partial(lambda i, rm, col: (rm(i), col), rm=rm, col=col)))
            args.append(P)
        specs.append(pl.BlockSpec((Lc, 256), functools.partial(lambda i, col: (cb, col), col=col)))
        args.append(P)
    for rm in (prev, lambda i: i, nxt):
        specs.append(pl.BlockSpec((2, BLK, LANES), functools.partial(lambda i, rm: (0, rm(i), 0), rm=rm)))
        args.append(rope)
    return specs, args


def _att_prep(i, n, refs, Lc):
    q_ref, kp_ref, kc_ref, kn_ref, kx_ref, vp_ref, vc_ref, vn_ref, vx_ref, rp_ref, rc_ref, rn_ref = refs
    cos = jnp.concatenate([rp_ref[0], rc_ref[0], rn_ref[0]], axis=0)
    sin = jnp.concatenate([rp_ref[1], rc_ref[1], rn_ref[1]], axis=0)
    kd, vd = [], []
    for t in range(ATT_KV // 2):
        sl = slice(t * LANES, (t + 1) * LANES)
        kl = jnp.concatenate([kp_ref[:, sl], kc_ref[:, sl], kn_ref[:, sl]], axis=0).astype(F32)
        kl = _rope(kl, cos, sin, 16)
        ka = jnp.concatenate([kl, kx_ref[:, sl].astype(F32)], axis=0)
        va = jnp.concatenate([vp_ref[:, sl], vc_ref[:, sl], vn_ref[:, sl], vx_ref[:, sl]], axis=0).astype(F32)
        kr, vr = pltpu.roll(ka, 64, 1), pltpu.roll(va, 64, 1)
        for b in range(2):
            hm = _half_mask(ka.shape, b)
            kd.append(jnp.where(hm, ka, kr).astype(BF16))
            vd.append(jnp.where(hm, va, vr).astype(BF16))
    nk = N_LOC + Lc
    rr = lax.broadcasted_iota(jnp.int32, (4 * BLK, nk), 0) % BLK
    ss = lax.broadcasted_iota(jnp.int32, (4 * BLK, nk), 1)
    lo = jnp.where(i == 0, BLK, 0)
    hi = jnp.where(i == n - 1, 2 * BLK, N_LOC)
    valid = (ss >= N_LOC) | ((ss >= rr) & (ss <= rr + 2 * BLK) & (ss >= lo) & (ss < hi))
    return kd, vd, valid, rc_ref[0], rc_ref[1]


def _stack4(ref, g, f=None):
    parts = []
    for jp in range(2):
        t = ref[:, (2 * g + jp) * LANES:(2 * g + jp + 1) * LANES].astype(F32)
        if f is not None:
            t = f(t)
        for a in range(2):
            parts.append(jnp.where(_half_mask(t.shape, a), t, 0.0))
    return jnp.concatenate(parts, axis=0)


def _unstack4(x4, jp):
    r0 = 2 * jp * BLK
    lo = x4[r0:r0 + BLK]
    hi = x4[r0 + BLK:r0 + 2 * BLK]
    return jnp.where(_half_mask(lo.shape, 0), lo, hi)


def _softmax_sink(sc, valid, sink_col):
    sc = jnp.where(valid, sc, NEG)
    m = jnp.maximum(jnp.max(sc, axis=-1, keepdims=True), sink_col)
    e = jnp.exp(sc - m)
    es = jnp.exp(sink_col - m)
    inv = 1.0 / (jnp.sum(e, axis=-1, keepdims=True) + es)
    return e * inv, es * inv


def _sink_col(sink_ref, g):
    return jnp.concatenate(
        [jnp.zeros((BLK, 1), F32) + sink_ref[4 * g + r:4 * g + r + 1, 0:1] for r in range(4)], axis=0)


def _att_fwd(P, rope, sinkb, L, Lc):
    n = L // BLK
    specs, args = _att_inputs(P, rope, L, Lc)

    def body(*refs):
        sink_ref, o_ref = refs[12], refs[13]
        i = pl.program_id(0)
        kd, vd, valid, cq, sq = _att_prep(i, n, refs[:12], Lc)
        for g in range(ATT_KV):
            q4 = _stack4(refs[0], g, lambda t: _rope(t, cq, sq, 16) * A_SCALE).astype(BF16)
            p, _ = _softmax_sink(_dot(q4, kd[g], NT), valid, _sink_col(sink_ref, g))
            o4 = _dot(p.astype(BF16), vd[g], NN)
            for jp in range(2):
                c0 = (2 * g + jp) * LANES
                o_ref[:, c0:c0 + LANES] = _unstack4(o4, jp).astype(o_ref.dtype)

    return pl.pallas_call(
        body, name="att_fwd", grid=(n,),
        in_specs=specs + [pl.BlockSpec((ATT_HEADS, LANES), lambda i: (0, 0))],
        out_specs=pl.BlockSpec((BLK, 1024), lambda i: (i, 0)),
        out_shape=jax.ShapeDtypeStruct((L, 1024), BF16),
        compiler_params=_cparams(("parallel",)),
    )(*args, sinkb)


def _att_bwd(P, rope, sinkb, y_att, dY, L, Lc):
    n = L // BLK
    specs, args = _att_inputs(P, rope, L, Lc)
    nk = N_LOC + Lc

    def body(*refs):
        sink_ref, y_ref, dy_ref = refs[12], refs[13], refs[14]
        dq_ref, dkl_ref, dvl_ref, dkx_ref, dvx_ref, dsk_ref = refs[15:21]
        i = pl.program_id(0)

        @pl.when(i == 0)
        def _():
            dkx_ref[...] = jnp.zeros_like(dkx_ref)
            dvx_ref[...] = jnp.zeros_like(dvx_ref)
            dsk_ref[...] = jnp.zeros_like(dsk_ref)

        kd, vd, valid, cq, sq = _att_prep(i, n, refs[:12], Lc)
        dk_t = [jnp.zeros((nk, LANES), F32) for _ in range(ATT_KV // 2)]
        dv_t = [jnp.zeros((nk, LANES), F32) for _ in range(ATT_KV // 2)]
        for g in range(ATT_KV):
            t, b = g // 2, g % 2
            q4 = _stack4(refs[0], g, lambda x: _rope(x, cq, sq, 16) * A_SCALE).astype(BF16)
            do4 = _stack4(dy_ref, g)
            o4 = _stack4(y_ref, g)
            p, ps = _softmax_sink(_dot(q4, kd[g], NT), valid, _sink_col(sink_ref, g))
            delta = jnp.sum(do4 * o4, axis=-1, keepdims=True)
            do4b = do4.astype(BF16)
            dsc = p * (_dot(do4b, vd[g], NT) - delta)
            dsr = -ps * delta
            for r in range(4):
                h = 4 * g + r
                dsk_ref[h:h + 1, :] += jnp.zeros((1, LANES), F32) + jnp.sum(dsr[r * BLK:(r + 1) * BLK])
            dscb = dsc.astype(BF16)
            dq4 = _dot(dscb, kd[g], NN) * A_SCALE
            for jp in range(2):
                c0 = (2 * g + jp) * LANES
                dq_ref[:, c0:c0 + LANES] = _rope_t(_unstack4(dq4, jp), cq, sq, 16)
            dkd = _dot(dscb, q4, TN)
            dvd = _dot(p.astype(BF16), do4b, TN)
            hm = _half_mask(dkd.shape, b)
            dk_t[t] += jnp.where(hm, dkd + pltpu.roll(dkd, 64, 1), 0.0)
            dv_t[t] += jnp.where(hm, dvd + pltpu.roll(dvd, 64, 1), 0.0)
        for t in range(ATT_KV // 2):
            sl = slice(t * LANES, (t + 1) * LANES)
            dkl_ref[0, :, sl] = dk_t[t][:N_LOC]
            dvl_ref[0, :, sl] = dv_t[t][:N_LOC]
            dkx_ref[:, sl] += dk_t[t][N_LOC:]
            dvx_ref[:, sl] += dv_t[t][N_LOC:]

    row = pl.BlockSpec((BLK, 1024), lambda i: (i, 0))
    loc = pl.BlockSpec((1, N_LOC, 256), lambda i: (i, 0, 0))
    cx = pl.BlockSpec((Lc, 256), lambda i: (0, 0))
    return pl.pallas_call(
        body, name="att_bwd", grid=(n,),
        in_specs=specs + [pl.BlockSpec((ATT_HEADS, LANES), lambda i: (0, 0)), row,
                          pl.BlockSpec((BLK, 1024), lambda i: (i, 1))],
        out_specs=[row, loc, loc, cx, cx, pl.BlockSpec((ATT_HEADS, LANES), lambda i: (0, 0))],
        out_shape=[jax.ShapeDtypeStruct((L, 1024), F32), jax.ShapeDtypeStruct((n, N_LOC, 256), F32),
                   jax.ShapeDtypeStruct((n, N_LOC, 256), F32), jax.ShapeDtypeStruct((Lc, 256), F32),
                   jax.ShapeDtypeStruct((Lc, 256), F32), jax.ShapeDtypeStruct((ATT_HEADS, LANES), F32)],
        compiler_params=_cparams(("arbitrary",)),
    )(*args, sinkb, y_att, dY)


def _assemble_dp(L, Lc, dqf, dqb, dkf, dkb, dvf, dvb, drg, daq, dkl, dvl, rope_att, dck, dcv, dkx, dvx):
    n = L // BLK
    nc = Lc // BLK

    def body(dqf_r, dqb_r, dkf_r, dkb_r, dvf_r, dvb_r, drg_r, daq_r, kl0, kl1, kl2, vl0, vl1, vl2, rp_r,
             dck_r, dcv_r, dkx_r, dvx_r, o_ref):
        i = pl.program_id(0)

        @pl.when(i < n)
        def _():
            o_ref[:, C_RQ:C_RK] = (dqf_r[...] + dqb_r[...]).astype(o_ref.dtype)
            o_ref[:, C_RK:C_RV] = (dkf_r[...] + dkb_r[...]).astype(o_ref.dtype)
            o_ref[:, C_RV:C_RG] = (dvf_r[...] + dvb_r[...]).astype(o_ref.dtype)
            o_ref[:, C_RG:C_AQ] = drg_r[...].astype(o_ref.dtype)
            o_ref[:, C_AQ:C_AK] = daq_r[...].astype(o_ref.dtype)
            w0 = jnp.where(i > 0, 1.0, 0.0)
            w2 = jnp.where(i < n - 1, 1.0, 0.0)
            dk = kl0[0, 2 * BLK:3 * BLK, :] * w0 + kl1[0, BLK:2 * BLK, :] + kl2[0, 0:BLK, :] * w2
            dv = vl0[0, 2 * BLK:3 * BLK, :] * w0 + vl1[0, BLK:2 * BLK, :] + vl2[0, 0:BLK, :] * w2
            for t in range(ATT_KV // 2):
                sl = slice(t * LANES, (t + 1) * LANES)
                o_ref[:, C_AK + t * LANES:C_AK + (t + 1) * LANES] = _rope_t(
                    dk[:, sl], rp_r[0], rp_r[1], 16).astype(o_ref.dtype)
            o_ref[:, C_AV:D_PROJ] = dv.astype(o_ref.dtype)

        @pl.when(i >= n)
        def _():
            o_ref[:, C_RQ:C_RK] = jnp.zeros((BLK, C_RK - C_RQ), o_ref.dtype)
            o_ref[:, C_RK:C_RV] = dck_r[...].astype(o_ref.dtype)
            o_ref[:, C_RV:C_RG] = dcv_r[...].astype(o_ref.dtype)
            o_ref[:, C_RG:C_AK] = jnp.zeros((BLK, C_AK - C_RG), o_ref.dtype)
            o_ref[:, C_AK:C_AV] = dkx_r[...].astype(o_ref.dtype)
            o_ref[:, C_AV:D_PROJ] = dvx_r[...].astype(o_ref.dtype)

    xm = lambda i: jnp.minimum(i, n - 1)
    cm = lambda i: jnp.clip(i - n, 0, nc - 1)
    r512 = pl.BlockSpec((BLK, 512), lambda i: (xm(i), 0))
    r1024 = pl.BlockSpec((BLK, 1024), lambda i: (xm(i), 0))
    part = lambda off: pl.BlockSpec((1, N_LOC, 256), lambda i: (jnp.clip(xm(i) + off, 0, n - 1), 0, 0))
    return pl.pallas_call(
        body, name="assemble_dp", grid=(n + nc,),
        in_specs=[r512, r512, r512, r512, r1024, r1024, r1024, r1024,
                  part(-1), part(0), part(1), part(-1), part(0), part(1),
                  pl.BlockSpec((2, BLK, LANES), lambda i: (0, xm(i), 0)),
                  pl.BlockSpec((BLK, 512), lambda i: (cm(i), 0)), pl.BlockSpec((BLK, 1024), lambda i: (cm(i), 0)),
                  pl.BlockSpec((BLK, 256), lambda i: (cm(i), 0)), pl.BlockSpec((BLK, 256), lambda i: (cm(i), 0))],
        out_specs=pl.BlockSpec((BLK, D_PROJ), lambda i: (i, 0)),
        out_shape=jax.ShapeDtypeStruct((L + Lc, D_PROJ), BF16),
        compiler_params=_cparams(("parallel",)),
    )(dqf, dqb, dkf, dkb, dvf, dvb, drg, daq, dkl, dkl, dkl, dvl, dvl, dvl, rope_att, dck, dcv, dkx, dvx)


def _adam_math(w, g, m, v):
    m = ADAM_B1 * m + (1.0 - ADAM_B1) * g
    v = ADAM_B2 * v + (1.0 - ADAM_B2) * (g * g)
    m_hat = m / (1.0 - ADAM_B1 ** ADAM_STEP)
    v_hat = v / (1.0 - ADAM_B2 ** ADAM_STEP)
    delta = -ADAM_LR * (m_hat / (jnp.sqrt(v_hat) + ADAM_EPS) + ADAM_WD * w)
    return delta, m, v


def _adam(name, w, m, v, g=None, parts=None):
    R, C = w.shape
    tr = _pick(R, (256, 128, 64, 32, 16, 8))
    summed = parts is not None
    n_parts = parts.shape[0] if summed else 0

    def body(w_ref, m_ref, v_ref, g_ref, go_ref, d_ref, mo_ref, vo_ref):
        if summed:
            gv = g_ref[0].astype(F32)
            for j in range(1, n_parts):
                gv = gv + g_ref[j].astype(F32)
        else:
            gv = g_ref[...]
        d, mn, vn = _adam_math(w_ref[...], gv, m_ref[...], v_ref[...])
        go_ref[...] = gv
        d_ref[...] = d
        mo_ref[...] = mn
        vo_ref[...] = vn

    row = pl.BlockSpec((tr, C), lambda i: (i, 0))
    gspec = pl.BlockSpec((n_parts, tr, C), lambda i: (0, i, 0)) if summed else row
    return pl.pallas_call(
        body, name=name, grid=(R // tr,),
        in_specs=[row, row, row, gspec], out_specs=[row] * 4,
        out_shape=[jax.ShapeDtypeStruct((R, C), F32)] * 4,
        compiler_params=_cparams(("parallel",)),
    )(w, m, v, parts if summed else g)


def _gather_cols(w_l, name):
    D, C = w_l.shape
    g = _allgather(w_l.astype(BF16), name)
    return jnp.transpose(g, (1, 0, 2)).reshape(D, N_DEV * C)


def _gather_rows(w_l, name):
    R, D = w_l.shape
    return _allgather(w_l.astype(BF16), name).reshape(N_DEV * R, D)


def _scatter_cols(g, name):
    D, N = g.shape
    C = N // N_DEV
    return _alltoall(jnp.transpose(g.reshape(D, N_DEV, C), (1, 0, 2)), name)


def _scatter_rows(g, name):
    N, D = g.shape
    return _alltoall(g.reshape(N_DEV, N // N_DEV, D), name)


def _pad_rows(a, rows):
    return jnp.concatenate([a, jnp.zeros((rows - a.shape[0],) + a.shape[1:], a.dtype)], axis=0)


def kernel(x, c, ctx, c_ctx, w_mod, b_mod, norm_mix, norm_ffn, w_in, ret_decay, attn_sink, w_out, w_gate, w_up, w_down, norm_final, loss_target, m_c_ctx, m_w_mod, m_b_mod, m_norm_mix, m_norm_ffn, m_w_in, m_ret_decay, m_attn_sink, m_w_out, m_w_gate, m_w_up, m_w_down, m_norm_final, v_c_ctx, v_w_mod, v_b_mod, v_norm_mix, v_norm_ffn, v_w_in, v_ret_decay, v_attn_sink, v_w_out, v_w_gate, v_w_up, v_w_down, v_norm_final):
    L, D = x.shape[1], x.shape[2]
    Lc = ctx.shape[1]
    DF = w_gate.shape[2] * N_DEV
    C6 = w_mod.shape[2]
    me = _my_id()
    xs, cx, tgt = x[0], ctx[0], loss_target[0]

    W_in = _gather_cols(w_in[0], "ag_w_in")
    W_out = _gather_rows(w_out[0], "ag_w_out")
    W_gate = _gather_cols(w_gate[0], "ag_w_gate")
    W_up = _gather_cols(w_up[0], "ag_w_up")
    W_down = _gather_rows(w_down[0], "ag_w_down")

    cs = _allgather(c, "ag_c")[:, 0, :]
    s_in = _pad_rows(jnp.concatenate([cs, c_ctx[None, :]], axis=0), 16)
    b_l = lax.dynamic_slice_in_dim(b_mod, me * C6, C6, axis=1)
    mod_parts = _allgather(_mod_fwd(s_in, w_mod[0], b_l), "ag_mod")
    mod = _pad_rows(lax.dynamic_index_in_dim(mod_parts, me, axis=1, keepdims=False).reshape(6, D), 8)
    modc = _pad_rows(mod_parts[:, N_DEV, :].reshape(6, D), 8)
    mix_mod, ffn_mod = mod, jnp.roll(mod, -3, axis=0)
    gt_m, gt_f = mod[2:3], mod[5:6]

    rope_ret, rope_att = _rope_tables(L)
    rdb = jnp.broadcast_to(ret_decay[0].reshape(2 * RET_HEADS, 1, 1), (2 * RET_HEADS, 1, LANES))
    sinkb = jnp.broadcast_to(attn_sink[0].reshape(ATT_HEADS, 1), (ATT_HEADS, LANES))

    tm = _pick(L + Lc, (768, 512, 384, 256, 128))
    tmx = _pick(L, (1024, 512, 256, 128))

    H = _modulate_fwd("mod_mix_fwd", xs, cx, norm_mix, mix_mod, modc)
    P = _matmul("mm_in", [(H, W_in, 0)], 1, L + Lc, D_PROJ, D, "nn",
                (tm, _pick(D_PROJ, (1152, 768, 512)), D), [], [BF16], lambda a, e: a)[0]
    tabs, ctab = _ret_tables(rdb, Lc)
    s0 = _ret_ctx_state(P, ctab, L, Lc)
    o_f, o_b, st_f, st_b = _ret_fwd(P, rope_ret, tabs, s0, L)
    y_ret = _ret_finish_fwd(o_f, o_b, P, L)
    y_att = _att_fwd(P, rope_att, sinkb, L, Lc)
    Y = jnp.concatenate([y_ret, y_att], axis=1)
    KO = Y.shape[1]
    x1, f_mix = _matmul("mm_out", [(Y, W_out, 0)], 1, L, D, KO, "nn", (tmx, _pick(D, (1024, 512)), KO),
                        [(xs, "mn"), (gt_m, "n")], [F32, BF16],
                        lambda a, e: [e[0] + e[1] * a[0], a[0]])

    H2 = _modulate_fwd("mod_ffn_fwd", x1, None, norm_ffn, ffn_mod, None)

    def swiglu_epi(a, e):
        return [a[0], a[1], a[0] * _sigmoid(a[0]) * a[1]]

    tf = _pick(DF, (512, 256, 128))
    ga, up, hmid = _matmul("mm_gate_up", [(H2, W_gate, 0), (H2, W_up, 1)], 2, L, DF, D, "nn", (tmx, tf, D),
                           [], [BF16, BF16, BF16], swiglu_epi)
    x2, f_ffn = _matmul("mm_down", [(hmid, W_down, 0)], 1, L, D, DF, "nn", (tmx, _pick(D, (1024, 512)), tf),
                        [(x1, "mn"), (gt_f, "n")], [F32, BF16],
                        lambda a, e: [e[0] + e[1] * a[0], a[0]])

    dx2, dFf, sums_l = _loss_head(x2, tgt, norm_final.reshape(1, D), f_ffn, gt_f)

    def dswiglu_epi(a, e):
        av, uv = e[0].astype(F32), e[1].astype(F32)
        sg = _sigmoid(av)
        return [a[0] * uv * (sg * (1.0 + av * (1.0 - sg))), a[0] * (av * sg)]

    dga, dup = _matmul("mm_d_down", [(dFf, W_down, 0)], 1, L, DF, D, "nt", (tmx, tf, D),
                       [(ga, "mn"), (up, "mn")], [BF16, BF16], dswiglu_epi)
    tkt = _pick(L, (512, 256, 128))
    dW_down = _matmul("mm_gw_down", [(hmid, dFf, 0)], 1, DF, D, L, "tn", (tf, _pick(D, (1024, 512)), tkt),
                      [], [BF16], lambda a, e: a)[0]
    dH2 = _matmul("mm_d_gate_up", [(dga, W_gate, 0), (dup, W_up, 0)], 1, L, D, DF, "nt",
                  (tmx, _pick(D, (1024, 512)), tf), [], [BF16], lambda a, e: a)[0]
    dW_gate, dW_up = _matmul("mm_gw_gate_up", [(H2, dga, 0), (H2, dup, 1)], 2, D, DF, L, "tn",
                             (_pick(D, (1024, 512)), tf, tkt), [], [BF16, BF16], lambda a, e: a)
    dx1, dFm, sums_f = _modulate_bwd("mod_ffn_bwd", x1, None, dH2, norm_ffn, ffn_mod, None, dx2, f_mix, gt_m)

    dY = _matmul("mm_d_out", [(dFm, W_out, 0)], 1, L, KO, D, "nt", (tmx, _pick(KO, (1024, 512)), D),
                 [], [BF16], lambda a, e: a)[0]
    dW_out = _matmul("mm_gw_out", [(Y, dFm, 0)], 1, KO, D, L, "tn",
                     (_pick(KO, (1024, 512)), _pick(D, (1024, 512)), tkt), [], [BF16], lambda a, e: a)[0]
    dO, drg = _ret_finish_bwd(o_f, o_b, P, dY, L)
    dqf, dkf, dvf, dqb, dkb, dvb, ds0, dlg = _ret_bwd(P, rope_ret, tabs, st_f, st_b, dO, L)
    dck, dcv, d_rd = _ret_ctx_bwd(P, ctab, ds0, dlg, rdb, L, Lc)
    daq, dkl, dvl, dkx, dvx, d_sink = _att_bwd(P, rope_att, sinkb, y_att, dY, L, Lc)
    dP = _assemble_dp(L, Lc, dqf, dqb, dkf, dkb, dvf, dvb, drg, daq, dkl, dvl, rope_att, dck, dcv, dkx, dvx)
    dH = _matmul("mm_d_in", [(dP, W_in, 0)], 1, L + Lc, D, D_PROJ, "nt",
                 (tm, _pick(D, (1024, 512)), _pick(D_PROJ, (1152, 768, 512))), [], [BF16], lambda a, e: a)[0]
    tkc = _pick(L + Lc, (768, 384, 256, 128))
    dW_in = _matmul("mm_gw_in", [(H, dP, 0)], 1, D, D_PROJ, L + Lc, "tn",
                    (_pick(D, (1024, 512)), _pick(D_PROJ, (1152, 768, 512)), tkc), [], [BF16], lambda a, e: a)[0]
    grad_x, sums_m = _modulate_bwd("mod_mix_bwd", xs, cx, dH, norm_mix, mix_mod, modc, dx1, None, None)

    zero = jnp.zeros((1, D), F32)
    dmod = jnp.concatenate([sums_m[0:1], sums_m[1:2], sums_f[6:7], sums_f[0:1], sums_f[1:2], sums_l[2:3]], axis=1)
    dmodc = jnp.concatenate([sums_m[3:4], sums_m[4:5], zero, zero, zero, zero], axis=1)
    dm_all = _allgather(jnp.concatenate([dmod, dmodc], axis=0), "ag_dmod")
    dm_cols = lax.dynamic_slice_in_dim(dm_all, me * C6, C6, axis=2)
    dm_in = jnp.concatenate([dm_cols[:, 0, :], dm_cols[:, 1, :]], axis=0)
    s_bwd = jnp.concatenate([cs, jnp.broadcast_to(c_ctx[None, :], (N_DEV, D))], axis=0)
    g_w_mod, dsil = _mod_bwd(s_bwd, dm_in, w_mod[0])

    lane_pad = lambda a: _pad_rows(a.reshape(-1, 1), LANES).reshape(1, LANES)
    pack = jnp.concatenate([dsil[0:1], sums_m[2:3], sums_f[2:3], sums_l[1:2],
                            lane_pad(d_rd[:, 0]), lane_pad(d_sink[:, 0]), sums_l[3:4, 0:LANES]], axis=1)
    packs = _allgather(pack, "ag_small")
    zl = jnp.zeros((1, LANES), F32)

    def pack_w(a_c, a_nm, a_nf, a_fin, a_rd, a_sk):
        return jnp.concatenate([a_c.reshape(1, D), a_nm, a_nf, a_fin.reshape(1, D), lane_pad(a_rd.reshape(-1)),
                                lane_pad(a_sk.reshape(-1)), zl], axis=1)

    sg, sd, sm, sv = _adam("adam_small", pack_w(c_ctx, norm_mix, norm_ffn, norm_final, ret_decay, attn_sink),
                           pack_w(m_c_ctx, m_norm_mix, m_norm_ffn, m_norm_final, m_ret_decay, m_attn_sink),
                           pack_w(v_c_ctx, v_norm_mix, v_norm_ffn, v_norm_final, v_ret_decay, v_attn_sink),
                           parts=packs)
    loss = sg[0, 4 * D + 2 * LANES]

    def unpack(a):
        return (a[0, 0:D], a[:, D:2 * D], a[:, 2 * D:3 * D], a[0, 3 * D:4 * D],
                a[0, 4 * D:4 * D + 2 * RET_HEADS].reshape(1, 2, RET_HEADS),
                a[:, 4 * D + LANES:4 * D + LANES + ATT_HEADS])

    bg, bd, bm, bv = _adam("adam_b_mod", b_mod, m_b_mod, v_b_mod, parts=dm_all.reshape(2 * N_DEV, 1, 6 * D))
    wg, wd, wm, wv = _adam("adam_w_mod", w_mod[0], m_w_mod[0], v_w_mod[0], g=g_w_mod)

    big = {}
    for nm, w, m, v, parts in (
            ("w_in", w_in, m_w_in, v_w_in, _scatter_cols(dW_in, "a2a_w_in")),
            ("w_out", w_out, m_w_out, v_w_out, _scatter_rows(dW_out, "a2a_w_out")),
            ("w_gate", w_gate, m_w_gate, v_w_gate, _scatter_cols(dW_gate, "a2a_w_gate")),
            ("w_up", w_up, m_w_up, v_w_up, _scatter_cols(dW_up, "a2a_w_up")),
            ("w_down", w_down, m_w_down, v_w_down, _scatter_rows(dW_down, "a2a_w_down"))):
        big[nm] = [a[None] for a in _adam("adam_" + nm, w[0], m[0], v[0], parts=parts)]

    g_s, d_s, m_s, v_s = unpack(sg), unpack(sd), unpack(sm), unpack(sv)

    def leaves(k, small, bmod, wmod):
        return (small[0], wmod[None], bmod, small[1], small[2], big["w_in"][k], small[4], small[5],
                big["w_out"][k], big["w_gate"][k], big["w_up"][k], big["w_down"][k], small[3])

    return (loss, grad_x[None], *leaves(0, g_s, bg, wg), *leaves(1, d_s, bd, wd),
            *leaves(2, m_s, bm, wm), *leaves(3, v_s, bv, wv))
```

```python
import functools

import jax
import jax.numpy as jnp
from jax import lax
from jax.experimental import pallas as pl
from jax.experimental.pallas import tpu as pltpu

F32 = jnp.float32
BF16 = jnp.bfloat16

N_DEV = 8
LANES = 128
RET_HEADS = 8
RET_DK = 64
RET_DV = 128
CHUNK = 128
ATT_HEADS = 16
ATT_KV = 4
ATT_DH = 64
GRID_W = 64
ROPE_BASE = 10000.0
EPS = 1e-6
NEG = -1e30
C_RQ, C_RK, C_RV, C_RG, C_AQ, C_AK, C_AV, D_PROJ = 0, 512, 1024, 2048, 3072, 4096, 4352, 4608
K_SCALE = RET_DK ** -0.5
A_SCALE = ATT_DH ** -0.5

ADAM_LR, ADAM_B1, ADAM_B2, ADAM_EPS, ADAM_WD, ADAM_STEP = 0.001, 0.9, 0.999, 1e-08, 0.01, 10

VMEM_BIG = 52 * 1024 * 1024

NN = (((1,), (0,)), ((), ()))
NT = (((1,), (1,)), ((), ()))
TN = (((0,), (0,)), ((), ()))


def _dot(a, b, dims):
    return lax.dot_general(a, b, dims, preferred_element_type=F32)


def _cparams(sem, vmem=VMEM_BIG):
    return pltpu.CompilerParams(dimension_semantics=sem, vmem_limit_bytes=vmem)


def _pick(dim, prefs):
    for p in prefs:
        if dim % p == 0:
            return p
    return dim


def _my_id():
    return lax.axis_index("x") * 4 + lax.axis_index("y") * 2 + lax.axis_index("c")


def _sigmoid(x):
    return 1.0 / (1.0 + jnp.exp(-x))


def _peers():
    mx, my, mc = lax.axis_index("x"), lax.axis_index("y"), lax.axis_index("c")
    out = []
    for k in range(1, N_DEV):
        kx, ky, kc = (k >> 2) & 1, (k >> 1) & 1, k & 1
        px = 1 - mx if kx else mx
        py = 1 - my if ky else my
        pc = 1 - mc if kc else mc
        out.append(((px, py, pc), px * 4 + py * 2 + pc))
    return out


def _exchange_copies(kind, x_ref, o_ref, ssem, rsem, lsem):
    me = _my_id()
    loc = pltpu.make_async_copy(x_ref if kind == "ag" else x_ref.at[me], o_ref.at[me], lsem)
    cps = []
    for k, (peer, pid) in enumerate(_peers()):
        cps.append(pltpu.make_async_remote_copy(
            src_ref=x_ref if kind == "ag" else x_ref.at[pid], dst_ref=o_ref.at[me],
            send_sem=ssem.at[k], recv_sem=rsem.at[k], device_id=peer, device_id_type=pl.DeviceIdType.MESH))
    return loc, cps


def _exchange_start(kind, x_ref, o_ref, ssem, rsem, lsem):
    loc, cps = _exchange_copies(kind, x_ref, o_ref, ssem, rsem, lsem)
    loc.start()
    for cp in cps:
        cp.start()


def _exchange_wait(kind, x_ref, o_ref, ssem, rsem, lsem):
    loc, cps = _exchange_copies(kind, x_ref, o_ref, ssem, rsem, lsem)
    for cp in cps:
        cp.wait_recv()
    for cp in cps:
        cp.wait_send()
    loc.wait()


_EXCHANGE_SEMS = [pltpu.SemaphoreType.DMA((N_DEV - 1,)), pltpu.SemaphoreType.DMA((N_DEV - 1,)),
                  pltpu.SemaphoreType.DMA(())]


def _exchange_shape(kind, x):
    return jax.ShapeDtypeStruct(((N_DEV,) + x.shape) if kind == "ag" else x.shape, x.dtype)


def _exchange(kind, x, name):
    def body(x_ref, o_ref, ssem, rsem, lsem):
        _exchange_start(kind, x_ref, o_ref, ssem, rsem, lsem)
        _exchange_wait(kind, x_ref, o_ref, ssem, rsem, lsem)

    return pl.pallas_call(
        body, name=name, out_shape=_exchange_shape(kind, x),
        in_specs=[pl.BlockSpec(memory_space=pl.ANY)], out_specs=pl.BlockSpec(memory_space=pl.ANY),
        scratch_shapes=list(_EXCHANGE_SEMS),
    )(x)


def _allgather(x, name):
    return _exchange("ag", x, name)


def _call(body, name, grid, in_specs, out_specs, out_shape, scratch_shapes, sem, args, comm=()):
    in_specs, out_specs, out_shape = list(in_specs), list(out_specs), list(out_shape)
    scratch_shapes = list(scratch_shapes)
    if not comm:
        outs = pl.pallas_call(body, name=name, grid=grid, in_specs=in_specs, out_specs=out_specs, out_shape=out_shape,
                              scratch_shapes=scratch_shapes, compiler_params=_cparams(sem))(*args)
        return list(outs), []
    n_in, n_out, n_scr, n_c = len(in_specs), len(out_specs), len(scratch_shapes), len(comm)
    hbm = pl.BlockSpec(memory_space=pl.ANY)

    def wrapped(*refs):
        ins, cins = refs[:n_in], refs[n_in:n_in + n_c]
        outs = refs[n_in + n_c:n_in + n_c + n_out]
        couts = refs[n_in + n_c + n_out:n_in + 2 * n_c + n_out]
        scr = refs[n_in + 2 * n_c + n_out:n_in + 2 * n_c + n_out + n_scr]
        sems = refs[n_in + 2 * n_c + n_out + n_scr:]
        first = pl.program_id(0) == 0
        last = pl.program_id(0) == grid[0] - 1
        for ax in range(1, len(grid)):
            first = first & (pl.program_id(ax) == 0)
            last = last & (pl.program_id(ax) == grid[ax] - 1)

        @pl.when(first)
        def _():
            for c, (kind, _) in enumerate(comm):
                _exchange_start(kind, cins[c], couts[c], *sems[3 * c:3 * c + 3])

        body(*ins, *outs, *scr)

        @pl.when(last)
        def _():
            for c, (kind, _) in enumerate(comm):
                _exchange_wait(kind, cins[c], couts[c], *sems[3 * c:3 * c + 3])

    res = pl.pallas_call(
        wrapped, name=name, grid=grid,
        in_specs=in_specs + [hbm] * n_c, out_specs=out_specs + [hbm] * n_c,
        out_shape=out_shape + [_exchange_shape(kind, arr) for kind, arr in comm],
        scratch_shapes=scratch_shapes + list(_EXCHANGE_SEMS) * n_c,
        compiler_params=_cparams(("arbitrary",) * len(grid)),
    )(*args, *[arr for _, arr in comm])
    return list(res[:n_out]), list(res[n_out:])


def _matmul(name, pairs, n_acc, M, N, K, mode, tiles, extras, out_dtypes, epilogue, j_outer=False, comm=()):
    tm, tn, tk = tiles
    gm, gn, nk = M // tm, N // tn, K // tk
    assert gm * tm == M and gn * tn == N and nk * tk == K, (name, M, N, K, tiles)
    if j_outer:
        grid = (gn, gm, nk)
        ij = lambda g0, g1: (g1, g0)
    else:
        grid = (gm, gn, nk)
        ij = lambda g0, g1: (g0, g1)

    if mode in ("nn", "nt"):
        a_spec = pl.BlockSpec((tm, tk), lambda g0, g1, k: (ij(g0, g1)[0], k))
    else:
        a_spec = pl.BlockSpec((tk, tm), lambda g0, g1, k: (k, ij(g0, g1)[0]))
    if mode == "nt":
        b_spec = pl.BlockSpec((tn, tk), lambda g0, g1, k: (ij(g0, g1)[1], k))
    else:
        b_spec = pl.BlockSpec((tk, tn), lambda g0, g1, k: (k, ij(g0, g1)[1]))
    dims = {"nn": NN, "nt": NT, "tn": TN}[mode]
    mn_spec = pl.BlockSpec((tm, tn), lambda g0, g1, k: ij(g0, g1))
    n_spec = pl.BlockSpec((1, tn), lambda g0, g1, k: (0, ij(g0, g1)[1]))

    in_specs, args = [], []
    for a, b, _ in pairs:
        in_specs += [a_spec, b_spec]
        args += [a, b]
    for arr, kind in extras:
        in_specs.append(mn_spec if kind == "mn" else n_spec)
        args.append(arr)
    n_p, n_e, n_o = len(pairs), len(extras), len(out_dtypes)

    def body(*refs):
        ab = refs[:2 * n_p]
        ex = refs[2 * n_p:2 * n_p + n_e]
        outs = refs[2 * n_p + n_e:2 * n_p + n_e + n_o]
        accs = refs[2 * n_p + n_e + n_o:]
        k = pl.program_id(2)

        def partial_sums():
            sums = [None] * n_acc
            for p, (_, _, ai) in enumerate(pairs):
                d = _dot(ab[2 * p][...], ab[2 * p + 1][...], dims)
                sums[ai] = d if sums[ai] is None else sums[ai] + d
            return sums

        def finish(acc_vals):
            res = epilogue(acc_vals, [e[...] for e in ex])
            for o, r in zip(outs, res):
                o[...] = r.astype(o.dtype)

        if nk == 1:
            finish(partial_sums())
        else:
            @pl.when(k == 0)
            def _():
                for ai, s in enumerate(partial_sums()):
                    accs[ai][...] = s

            @pl.when(k > 0)
            def _():
                for ai, s in enumerate(partial_sums()):
                    accs[ai][...] += s

            @pl.when(k == nk - 1)
            def _():
                finish([a[...] for a in accs])

    outs, couts = _call(
        body, name, grid, in_specs, [mn_spec] * n_o,
        [jax.ShapeDtypeStruct((M, N), dt) for dt in out_dtypes],
        [pltpu.VMEM((tm, tn), F32) for _ in range(n_acc if nk > 1 else 0)],
        ("parallel", "parallel", "arbitrary"), args, comm)
    return (outs, couts) if comm else outs


def _rope_tables(L):
    t = jnp.arange(L, dtype=jnp.int32)
    lane = jnp.arange(LANES, dtype=jnp.int32)
    hl = lane % 64
    f = (hl % 32).astype(F32)
    inv = ROPE_BASE ** (-f / 32.0)
    ang = t.astype(F32)[:, None] * inv[None, :]
    sgn = jnp.where(hl < 32, -1.0, 1.0)[None, :]
    ret = jnp.stack([jnp.cos(ang), jnp.sin(ang) * sgn])
    q = hl % 32
    f2 = (q % 16).astype(F32)
    inv2 = ROPE_BASE ** (-f2 / 16.0)
    pos = jnp.where((hl < 32)[None, :], (t // GRID_W)[:, None], (t % GRID_W)[:, None]).astype(F32)
    ang2 = pos * inv2[None, :]
    sgn2 = jnp.where(q < 16, -1.0, 1.0)[None, :]
    att = jnp.stack([jnp.cos(ang2), jnp.sin(ang2) * sgn2])
    return ret.astype(F32), att.astype(F32)


def _swap(x, sh):
    lane = lax.broadcasted_iota(jnp.int32, x.shape, 1)
    ra = pltpu.roll(x, LANES - sh, 1)
    rb = pltpu.roll(x, sh, 1)
    la = pltpu.roll(lane, LANES - sh, 1)
    partner = jnp.where((lane % (2 * sh)) < sh, lane + sh, lane - sh)
    return jnp.where(la == partner, ra, rb)


def _rope(x, cos, sin, sh):
    return x * cos + _swap(x, sh) * sin


def _rope_t(d, cos, sin, sh):
    return d * cos + _swap(d * sin, sh)


def _half_mask(shape, a):
    lane = lax.broadcasted_iota(jnp.int32, shape, 1)
    return (lane < 64) if a == 0 else (lane >= 64)


def _mod_fwd(s_in, w_l, b_l):
    D, C6 = w_l.shape
    tk = _pick(D, (512, 256, 128))
    nk = D // tk

    def body(s_ref, w_ref, b_ref, o_ref):
        k = pl.program_id(0)
        s = s_ref[...]
        s = s * _sigmoid(s)
        d = jnp.dot(s, w_ref[...], preferred_element_type=F32, precision=lax.Precision.HIGHEST)

        @pl.when(k == 0)
        def _():
            o_ref[...] = d + b_ref[...]

        @pl.when(k > 0)
        def _():
            o_ref[...] += d

    return pl.pallas_call(
        body, name="mod_fwd", grid=(nk,),
        in_specs=[pl.BlockSpec((16, tk), lambda k: (0, k)), pl.BlockSpec((tk, C6), lambda k: (k, 0)),
                  pl.BlockSpec((1, C6), lambda k: (0, 0))],
        out_specs=pl.BlockSpec((16, C6), lambda k: (0, 0)),
        out_shape=jax.ShapeDtypeStruct((16, C6), F32),
        compiler_params=_cparams(("arbitrary",)),
    )(s_in, w_l, b_l)


def _mod_bwd(s_in, dm, w_l):
    D, C6 = w_l.shape
    tk = _pick(D, (512, 256, 128))
    nk = D // tk

    def body(s_ref, dm_ref, w_ref, gw_ref, gc_ref):
        s = s_ref[...]
        sg = _sigmoid(s)
        act = s * sg
        dmv = dm_ref[...]
        gw_ref[...] = lax.dot_general(act, dmv, TN, preferred_element_type=F32, precision=lax.Precision.HIGHEST)
        ds = lax.dot_general(dmv, w_ref[...], NT, preferred_element_type=F32, precision=lax.Precision.HIGHEST)
        dsil = (sg * (1.0 + s * (1.0 - sg)))[8:9, :]
        gc_ref[...] = jnp.zeros((8, tk), F32) + jnp.sum(ds[8:16, :], axis=0, keepdims=True) * dsil

    return pl.pallas_call(
        body, name="mod_bwd", grid=(nk,),
        in_specs=[pl.BlockSpec((16, tk), lambda k: (0, k)), pl.BlockSpec((16, C6), lambda k: (0, 0)),
                  pl.BlockSpec((tk, C6), lambda k: (k, 0))],
        out_specs=[pl.BlockSpec((tk, C6), lambda k: (k, 0)), pl.BlockSpec((8, tk), lambda k: (0, k))],
        out_shape=[jax.ShapeDtypeStruct((D, C6), F32), jax.ShapeDtypeStruct((8, D), F32)],
        compiler_params=_cparams(("parallel",)),
    )(s_in, dm, w_l)


def _norm_rows(x):
    r = lax.rsqrt(jnp.mean(x * x, axis=-1, keepdims=True) + EPS)
    return x * r, r


def _modulate_fwd(name, x, ctx, g, mod, modc):
    L, D = x.shape
    tr = ctx.shape[0] if ctx is not None else _pick(L, (256, 128))
    nx = L // tr
    nt = nx + (1 if ctx is not None else 0)

    def body(*refs):
        if ctx is not None:
            x_ref, c_ref, g_ref, m_ref, mc_ref, o_ref = refs
        else:
            x_ref, g_ref, m_ref, o_ref = refs
        i = pl.program_id(0)

        def run(src, m):
            n, _ = _norm_rows(src[...])
            o_ref[...] = (n * g_ref[...] * (1.0 + m[1:2, :]) + m[0:1, :]).astype(o_ref.dtype)

        if ctx is None:
            run(x_ref, m_ref)
        else:
            @pl.when(i < nx)
            def _():
                run(x_ref, m_ref)

            @pl.when(i >= nx)
            def _():
                run(c_ref, mc_ref)

    row = pl.BlockSpec((tr, D), lambda i: (jnp.minimum(i, nx - 1), 0))
    vec = pl.BlockSpec((1, D), lambda i: (0, 0))
    mv = pl.BlockSpec((8, D), lambda i: (0, 0))
    if ctx is not None:
        in_specs = [row, pl.BlockSpec((tr, D), lambda i: (0, 0)), vec, mv, mv]
        args = (x, ctx, g, mod, modc)
    else:
        in_specs = [row, vec, mv]
        args = (x, g, mod)
    return pl.pallas_call(
        body, name=name, grid=(nt,), in_specs=in_specs,
        out_specs=pl.BlockSpec((tr, D), lambda i: (i, 0)),
        out_shape=jax.ShapeDtypeStruct((nt * tr, D), BF16),
        compiler_params=_cparams(("parallel",)),
    )(*args)


def _modulate_bwd(name, x, ctx, dh, g, mod, modc, dres, fbr, gate):
    L, D = x.shape
    tr = ctx.shape[0] if ctx is not None else _pick(L, (256, 128))
    nx = L // tr
    nt = nx + (1 if ctx is not None else 0)
    has_f = fbr is not None

    def body(*refs):
        refs = list(refs)
        x_ref = refs.pop(0)
        c_ref = refs.pop(0) if ctx is not None else None
        dh_ref, g_ref, m_ref = refs.pop(0), refs.pop(0), refs.pop(0)
        mc_ref = refs.pop(0) if ctx is not None else None
        dr_ref = refs.pop(0)
        f_ref = refs.pop(0) if has_f else None
        gt_ref = refs.pop(0) if has_f else None
        dx_ref = refs.pop(0)
        df_ref = refs.pop(0) if has_f else None
        acc_ref = refs.pop(0)
        i = pl.program_id(0)

        @pl.when(i == 0)
        def _():
            acc_ref[...] = jnp.zeros_like(acc_ref)

        def sums(src, m, base, grow):
            n, r = _norm_rows(src[...])
            d = dh_ref[...].astype(F32)
            gg = g_ref[...]
            sc1 = 1.0 + m[1:2, :]
            acc_ref[base:base + 1, :] += jnp.sum(d, axis=0, keepdims=True)
            dn = d * n
            acc_ref[base + 1:base + 2, :] += jnp.sum(dn, axis=0, keepdims=True) * gg
            acc_ref[grow:grow + 1, :] += jnp.sum(dn, axis=0, keepdims=True) * sc1
            dnv = d * (gg * sc1)
            return r * (dnv - n * jnp.mean(dnv * n, axis=-1, keepdims=True))

        def x_rows():
            dx = sums(x_ref, m_ref, 0, 2) + dr_ref[...]
            dx_ref[...] = dx
            if has_f:
                acc_ref[6:7, :] += jnp.sum(dx * f_ref[...].astype(F32), axis=0, keepdims=True)
                df_ref[...] = (dx * gt_ref[...]).astype(df_ref.dtype)

        if ctx is None:
            x_rows()
        else:
            pl.when(i < nx)(x_rows)

            @pl.when(i >= nx)
            def _():
                sums(c_ref, mc_ref, 3, 2)

    row = pl.BlockSpec((tr, D), lambda i: (jnp.minimum(i, nx - 1), 0))
    vec = pl.BlockSpec((1, D), lambda i: (0, 0))
    mv = pl.BlockSpec((8, D), lambda i: (0, 0))
    in_specs, args = [row], [x]
    if ctx is not None:
        in_specs.append(pl.BlockSpec((tr, D), lambda i: (0, 0)))
        args.append(ctx)
    in_specs += [pl.BlockSpec((tr, D), lambda i: (i, 0)), vec, mv]
    args += [dh, g, mod]
    if ctx is not None:
        in_specs.append(mv)
        args.append(modc)
    in_specs.append(row)
    args.append(dres)
    out_specs = [row]
    out_shape = [jax.ShapeDtypeStruct((L, D), F32)]
    if has_f:
        in_specs += [row, vec]
        args += [fbr, gate]
        out_specs.append(row)
        out_shape.append(jax.ShapeDtypeStruct((L, D), BF16))
    out_specs.append(pl.BlockSpec((16, D), lambda i: (0, 0)))
    out_shape.append(jax.ShapeDtypeStruct((16, D), F32))
    return pl.pallas_call(
        body, name=name, grid=(nt,), in_specs=in_specs, out_specs=out_specs, out_shape=out_shape,
        compiler_params=_cparams(("arbitrary",)),
    )(*args)


def _loss_head(x2, tgt, nf, fbr, gate):
    L, D = x2.shape
    tr = _pick(L, (256, 128))

    def body(x_ref, t_ref, w_ref, f_ref, gt_ref, dx_ref, df_ref, acc_ref):
        i = pl.program_id(0)

        @pl.when(i == 0)
        def _():
            acc_ref[...] = jnp.zeros_like(acc_ref)

        n, r = _norm_rows(x_ref[...])
        w = w_ref[...]
        e = n * w - t_ref[...]
        acc_ref[0:1, :] += jnp.sum(e * e, axis=0, keepdims=True) * (0.5 / D)
        dout = e * (1.0 / D)
        acc_ref[1:2, :] += jnp.sum(dout * n, axis=0, keepdims=True)
        dn = dout * w
        dx = r * (dn - n * jnp.mean(dn * n, axis=-1, keepdims=True))
        dx_ref[...] = dx
        acc_ref[2:3, :] += jnp.sum(dx * f_ref[...].astype(F32), axis=0, keepdims=True)
        df_ref[...] = (dx * gt_ref[...]).astype(df_ref.dtype)

        @pl.when(i == pl.num_programs(0) - 1)
        def _():
            acc_ref[3:4, :] = jnp.zeros((1, D), F32) + jnp.sum(acc_ref[0:1, :])

    row = pl.BlockSpec((tr, D), lambda i: (i, 0))
    vec = pl.BlockSpec((1, D), lambda i: (0, 0))
    return pl.pallas_call(
        body, name="loss_head", grid=(L // tr,),
        in_specs=[row, row, vec, row, vec],
        out_specs=[row, row, pl.BlockSpec((8, D), lambda i: (0, 0))],
        out_shape=[jax.ShapeDtypeStruct((L, D), F32), jax.ShapeDtypeStruct((L, D), BF16),
                   jax.ShapeDtypeStruct((8, D), F32)],
        compiler_params=_cparams(("arbitrary",)),
    )(x2, tgt, nf, fbr, gate)


N_TAB = 7


def _ret_tables(rdb, Lc):
    def body(rd_ref, t_ref, c_ref):
        d = pl.program_id(0) // RET_HEADS
        fwd = d == 0
        lg = -jnp.exp(rd_ref[0])
        i = lax.broadcasted_iota(jnp.int32, (CHUNK, CHUNK), 0).astype(F32)
        j = lax.broadcasted_iota(jnp.int32, (CHUNK, CHUNK), 1).astype(F32)
        rel = jnp.where(fwd, i - j, j - i)
        mask = (rel > 0.0) | ((rel == 0.0) & fwd)
        dm = jnp.where(mask, jnp.exp(lg * jnp.maximum(rel, 0.0)), 0.0)
        t_ref[0, 0] = dm
        t_ref[0, 1] = rel * dm
        qc = jnp.where(fwd, i + 1.0, CHUNK - i)
        qw = jnp.exp(lg * qc)
        t_ref[0, 2] = qw
        t_ref[0, 3] = qw * qc
        kc = jnp.where(fwd, CHUNK - 1.0 - i, i)
        kw = jnp.exp(lg * kc)
        t_ref[0, 4] = kw
        t_ref[0, 5] = kw * kc
        t_ref[0, 6] = jnp.exp(lg * float(CHUNK)) + jnp.zeros((CHUNK, CHUNK), F32)
        m = lax.broadcasted_iota(jnp.int32, (Lc, LANES), 0).astype(F32)
        cc = jnp.where(fwd, Lc - 1.0 - m, m)
        cw = jnp.exp(lg * cc)
        c_ref[0, 0] = cw
        c_ref[0, 1] = cw * cc

    return pl.pallas_call(
        body, name="ret_tables", grid=(2 * RET_HEADS,),
        in_specs=[pl.BlockSpec((1, 1, LANES), lambda r: (r, 0, 0))],
        out_specs=[pl.BlockSpec((1, N_TAB, CHUNK, CHUNK), lambda r: (r, 0, 0, 0)),
                   pl.BlockSpec((1, 2, Lc, LANES), lambda r: (r, 0, 0, 0))],
        out_shape=[jax.ShapeDtypeStruct((2 * RET_HEADS, N_TAB, CHUNK, CHUNK), F32),
                   jax.ShapeDtypeStruct((2 * RET_HEADS, 2, Lc, LANES), F32)],
        compiler_params=_cparams(("parallel",)),
    )(rdb)


def _ret_ctx_state(P, ctab, L, Lc):
    cb = L // Lc

    def body(k_ref, v_ref, c_ref, s_ref):
        for p in range(RET_HEADS // 2):
            kp = k_ref[:, p * LANES:(p + 1) * LANES].astype(F32) * K_SCALE
            for a in range(2):
                h = 2 * p + a
                kh = jnp.where(_half_mask(kp.shape, a), kp, 0.0)
                vh = v_ref[:, h * RET_DV:(h + 1) * RET_DV]
                for d in range(2):
                    kw = (kh * c_ref[d * RET_HEADS + h, 0]).astype(BF16)
                    s_ref[d * RET_HEADS + h] = _dot(kw, vh, TN)

    return pl.pallas_call(
        body, name="ret_ctx_state", grid=(1,),
        in_specs=[pl.BlockSpec((Lc, 512), lambda i: (cb, C_RK // 512)),
                  pl.BlockSpec((Lc, 1024), lambda i: (cb, C_RV // 1024)),
                  pl.BlockSpec((2 * RET_HEADS, 2, Lc, LANES), lambda i: (0, 0, 0, 0))],
        out_specs=pl.BlockSpec((2 * RET_HEADS, LANES, RET_DV), lambda i: (0, 0, 0)),
        out_shape=jax.ShapeDtypeStruct((2 * RET_HEADS, LANES, RET_DV), F32),
        compiler_params=_cparams(("arbitrary",)),
    )(P, P, ctab)


def _ret_fwd(P, rope, tabs, s0, L, comm=()):
    n = L // CHUNK

    def body(qf, kf, vf, rf, qb, kb, vb, rb, t_ref, s0_ref, of_ref, ob_ref, stf_ref, stb_ref, st):
        s = pl.program_id(0)

        @pl.when(s == 0)
        def _():
            st[...] = s0_ref[...]

        for d, (q_ref, k_ref, v_ref, r_ref, o_ref, so_ref) in enumerate(
                ((qf, kf, vf, rf, of_ref, stf_ref), (qb, kb, vb, rb, ob_ref, stb_ref))):
            cos, sin = r_ref[0], r_ref[1]
            for p in range(RET_HEADS // 2):
                qp = _rope(q_ref[:, p * LANES:(p + 1) * LANES].astype(F32), cos, sin, 32)
                kp = _rope(k_ref[:, p * LANES:(p + 1) * LANES].astype(F32), cos, sin, 32) * K_SCALE
                for a in range(2):
                    h = 2 * p + a
                    r = d * RET_HEADS + h
                    hm = _half_mask(qp.shape, a)
                    qh = jnp.where(hm, qp, 0.0)
                    kh = jnp.where(hm, kp, 0.0)
                    vh = v_ref[:, h * RET_DV:(h + 1) * RET_DV]
                    sp = st[r]
                    so_ref[0, h] = sp[a * RET_DK:(a + 1) * RET_DK, :]
                    sc = _dot(qh.astype(BF16), kh.astype(BF16), NT) * t_ref[r, 0]
                    o = _dot(sc.astype(BF16), vh, NN)
                    o += _dot((qh * t_ref[r, 2]).astype(BF16), sp.astype(BF16), NN)
                    o_ref[:, h * RET_DV:(h + 1) * RET_DV] = o
                    st[r] = t_ref[r, 6] * sp + _dot((kh * t_ref[r, 4]).astype(BF16), vh, TN)

    fw = lambda s: s
    bw = lambda s: n - 1 - s

    def specs(cm):
        return [pl.BlockSpec((CHUNK, 512), lambda s: (cm(s), C_RQ // 512)),
                pl.BlockSpec((CHUNK, 512), lambda s: (cm(s), C_RK // 512)),
                pl.BlockSpec((CHUNK, 1024), lambda s: (cm(s), C_RV // 1024)),
                pl.BlockSpec((2, CHUNK, LANES), lambda s: (0, cm(s), 0))]

    full = lambda shp: pl.BlockSpec(shp, lambda s: (0,) * len(shp))
    return _call(
        body, "ret_fwd", (n,),
        specs(fw) + specs(bw) + [full((2 * RET_HEADS, N_TAB, CHUNK, CHUNK)), full((2 * RET_HEADS, LANES, RET_DV))],
        [pl.BlockSpec((CHUNK, 1024), lambda s: (fw(s), 0)),
         pl.BlockSpec((CHUNK, 1024), lambda s: (bw(s), 0)),
         pl.BlockSpec((1, RET_HEADS, RET_DK, RET_DV), lambda s: (fw(s), 0, 0, 0)),
         pl.BlockSpec((1, RET_HEADS, RET_DK, RET_DV), lambda s: (bw(s), 0, 0, 0))],
        [jax.ShapeDtypeStruct((L, 1024), F32), jax.ShapeDtypeStruct((L, 1024), F32),
         jax.ShapeDtypeStruct((n, RET_HEADS, RET_DK, RET_DV), F32),
         jax.ShapeDtypeStruct((n, RET_HEADS, RET_DK, RET_DV), F32)],
        [pltpu.VMEM((2 * RET_HEADS, LANES, RET_DV), F32)],
        ("arbitrary",), (P, P, P, rope, P, P, P, rope, tabs, s0), comm)


def _ret_finish_fwd(of, ob, P, L):
    tr = _pick(L, (256, 128))

    def body(f_ref, b_ref, g_ref, y_ref):
        for h in range(RET_HEADS):
            sl = slice(h * RET_DV, (h + 1) * RET_DV)
            n, _ = _norm_rows(f_ref[:, sl] + b_ref[:, sl])
            g = g_ref[:, sl].astype(F32)
            y_ref[:, sl] = (n * (g * _sigmoid(g))).astype(y_ref.dtype)

    row = pl.BlockSpec((tr, 1024), lambda i: (i, 0))
    return pl.pallas_call(
        body, name="ret_finish_fwd", grid=(L // tr,),
        in_specs=[row, row, pl.BlockSpec((tr, 1024), lambda i: (i, C_RG // 1024))],
        out_specs=row, out_shape=jax.ShapeDtypeStruct((L, 1024), BF16),
        compiler_params=_cparams(("parallel",)),
    )(of, ob, P)


def _ret_finish_bwd(of, ob, P, dY, L):
    tr = _pick(L, (256, 128))

    def body(f_ref, b_ref, g_ref, dy_ref, do_ref, dg_ref):
        for h in range(RET_HEADS):
            sl = slice(h * RET_DV, (h + 1) * RET_DV)
            n, r = _norm_rows(f_ref[:, sl] + b_ref[:, sl])
            g = g_ref[:, sl].astype(F32)
            sg = _sigmoid(g)
            dy = dy_ref[:, sl].astype(F32)
            dg_ref[:, sl] = (dy * n * (sg * (1.0 + g * (1.0 - sg)))).astype(dg_ref.dtype)
            dn = dy * (g * sg)
            do_ref[:, sl] = (r * (dn - n * jnp.mean(dn * n, axis=-1, keepdims=True))).astype(do_ref.dtype)

    row = pl.BlockSpec((tr, 1024), lambda i: (i, 0))
    return pl.pallas_call(
        body, name="ret_finish_bwd", grid=(L // tr,),
        in_specs=[row, row, pl.BlockSpec((tr, 1024), lambda i: (i, C_RG // 1024)), row],
        out_specs=[row, row],
        out_shape=[jax.ShapeDtypeStruct((L, 1024), BF16), jax.ShapeDtypeStruct((L, 1024), BF16)],
        compiler_params=_cparams(("parallel",)),
    )(of, ob, P, dY)


def _ret_bwd(P, rope, tabs, stf, stb, dO, L, comm=()):
    n = L // CHUNK

    def body(qf, kf, vf, rf, gf, sf, qb, kb, vb, rb, gb, sb, t_ref,
             dqf, dkf, dvf, dqb, dkb, dvb, ds0_ref, dlg_ref, ds):
        s = pl.program_id(0)

        @pl.when(s == 0)
        def _():
            ds[...] = jnp.zeros_like(ds)
            dlg_ref[...] = jnp.zeros_like(dlg_ref)

        for d, (q_ref, k_ref, v_ref, r_ref, g_ref, s_ref, dq_ref, dk_ref, dv_ref) in enumerate(
                ((qf, kf, vf, rf, gf, sf, dqf, dkf, dvf), (qb, kb, vb, rb, gb, sb, dqb, dkb, dvb))):
            cos, sin = r_ref[0], r_ref[1]
            for p in range(RET_HEADS // 2):
                qp = _rope(q_ref[:, p * LANES:(p + 1) * LANES].astype(F32), cos, sin, 32)
                kp = _rope(k_ref[:, p * LANES:(p + 1) * LANES].astype(F32), cos, sin, 32) * K_SCALE
                dqp = jnp.zeros((CHUNK, LANES), F32)
                dkp = jnp.zeros((CHUNK, LANES), F32)
                for a in range(2):
                    h = 2 * p + a
                    r = d * RET_HEADS + h
                    hm = _half_mask(qp.shape, a)
                    qh = jnp.where(hm, qp, 0.0)
                    kh = jnp.where(hm, kp, 0.0)
                    qhb, khb = qh.astype(BF16), kh.astype(BF16)
                    vh = v_ref[:, h * RET_DV:(h + 1) * RET_DV]
                    gh = g_ref[:, h * RET_DV:(h + 1) * RET_DV]
                    zero = jnp.zeros((RET_DK, RET_DV), F32)
                    sp = s_ref[0, h]
                    sp = jnp.concatenate([sp, zero] if a == 0 else [zero, sp], axis=0)
                    dsn = ds[r]
                    dm, rm = t_ref[r, 0], t_ref[r, 1]
                    qw, qwc, kw, kwc, gch = t_ref[r, 2], t_ref[r, 3], t_ref[r, 4], t_ref[r, 5], t_ref[r, 6]
                    am = _dot(qhb, khb, NT)
                    dar = _dot(gh, vh, NT)
                    da = (dar * dm).astype(BF16)
                    xq = _dot(gh, sp.astype(BF16), NT)
                    yk = _dot(vh, dsn.astype(BF16), NT)
                    dqp += _dot(da, khb, NN) + xq * qw
                    dkp += _dot(da, qhb, TN) + yk * kw
                    dvh = _dot((am * dm).astype(BF16), gh, TN) + _dot((kh * kw).astype(BF16), dsn.astype(BF16), NN)
                    dv_ref[:, h * RET_DV:(h + 1) * RET_DV] = dvh
                    part = (jnp.sum(am * dar * rm) + jnp.sum(qh * qwc * xq) + jnp.sum(kh * kwc * yk)
                            + float(CHUNK) * jnp.sum(gch * dsn * sp))
                    dlg_ref[r:r + 1, :] += jnp.zeros((1, LANES), F32) + part
                    ds[r] = gch * dsn + _dot((qh * qw).astype(BF16), gh, TN)
                dq_ref[:, p * LANES:(p + 1) * LANES] = _rope_t(dqp, cos, sin, 32)
                dk_ref[:, p * LANES:(p + 1) * LANES] = _rope_t(dkp * K_SCALE, cos, sin, 32)

        @pl.when(s == n - 1)
        def _():
            ds0_ref[...] = ds[...]

    fw = lambda s: n - 1 - s
    bw = lambda s: s

    def specs(cm):
        return [pl.BlockSpec((CHUNK, 512), lambda s: (cm(s), C_RQ // 512)),
                pl.BlockSpec((CHUNK, 512), lambda s: (cm(s), C_RK // 512)),
                pl.BlockSpec((CHUNK, 1024), lambda s: (cm(s), C_RV // 1024)),
                pl.BlockSpec((2, CHUNK, LANES), lambda s: (0, cm(s), 0)),
                pl.BlockSpec((CHUNK, 1024), lambda s: (cm(s), 0)),
                pl.BlockSpec((1, RET_HEADS, RET_DK, RET_DV), lambda s: (cm(s), 0, 0, 0))]

    def ospecs(cm):
        return [pl.BlockSpec((CHUNK, 512), lambda s: (cm(s), 0)), pl.BlockSpec((CHUNK, 512), lambda s: (cm(s), 0)),
                pl.BlockSpec((CHUNK, 1024), lambda s: (cm(s), 0))]

    oshape = [jax.ShapeDtypeStruct((L, 512), F32), jax.ShapeDtypeStruct((L, 512), F32),
              jax.ShapeDtypeStruct((L, 1024), F32)]
    full = lambda shp: pl.BlockSpec(shp, lambda s: (0,) * len(shp))
    return _call(
        body, "ret_bwd", (n,),
        specs(fw) + specs(bw) + [full((2 * RET_HEADS, N_TAB, CHUNK, CHUNK))],
        ospecs(fw) + ospecs(bw) + [full((2 * RET_HEADS, LANES, RET_DV)), full((2 * RET_HEADS, LANES))],
        oshape + oshape + [jax.ShapeDtypeStruct((2 * RET_HEADS, LANES, RET_DV), F32),
                           jax.ShapeDtypeStruct((2 * RET_HEADS, LANES), F32)],
        [pltpu.VMEM((2 * RET_HEADS, LANES, RET_DV), F32)],
        ("arbitrary",), (P, P, P, rope, dO, stf, P, P, P, rope, dO, stb, tabs), comm)


def _ret_ctx_bwd(P, ctab, ds0, dlg, rdb, L, Lc):
    cb = L // Lc

    def body(k_ref, v_ref, c_ref, ds_ref, dlg_ref, rd_ref, dk_ref, dv_ref, drd_ref):
        for p in range(RET_HEADS // 2):
            kp = k_ref[:, p * LANES:(p + 1) * LANES].astype(F32) * K_SCALE
            dkp = jnp.zeros((Lc, LANES), F32)
            for a in range(2):
                h = 2 * p + a
                kh = jnp.where(_half_mask(kp.shape, a), kp, 0.0)
                vh = v_ref[:, h * RET_DV:(h + 1) * RET_DV]
                dvh = jnp.zeros((Lc, RET_DV), F32)
                for d in range(2):
                    r = d * RET_HEADS + h
                    dsb = ds_ref[r].astype(BF16)
                    cw, cwc = c_ref[r, 0], c_ref[r, 1]
                    y = _dot(vh, dsb, NT)
                    dkp += y * cw
                    dvh += _dot((kh * cw).astype(BF16), dsb, NN)
                    lg = -jnp.exp(rd_ref[r])
                    drd_ref[r:r + 1, :] = (dlg_ref[r:r + 1, :] + jnp.sum(kh * cwc * y)) * lg
                dv_ref[:, h * RET_DV:(h + 1) * RET_DV] = dvh
            dk_ref[:, p * LANES:(p + 1) * LANES] = dkp * K_SCALE

    full = lambda shp: pl.BlockSpec(shp, lambda i: (0,) * len(shp))
    return pl.pallas_call(
        body, name="ret_ctx_bwd", grid=(1,),
        in_specs=[pl.BlockSpec((Lc, 512), lambda i: (cb, C_RK // 512)),
                  pl.BlockSpec((Lc, 1024), lambda i: (cb, C_RV // 1024)),
                  full((2 * RET_HEADS, 2, Lc, LANES)), full((2 * RET_HEADS, LANES, RET_DV)),
                  full((2 * RET_HEADS, LANES)), full((2 * RET_HEADS, 1, LANES))],
        out_specs=[full((Lc, 512)), full((Lc, 1024)), full((2 * RET_HEADS, LANES))],
        out_shape=[jax.ShapeDtypeStruct((Lc, 512), F32), jax.ShapeDtypeStruct((Lc, 1024), F32),
                   jax.ShapeDtypeStruct((2 * RET_HEADS, LANES), F32)],
        compiler_params=_cparams(("arbitrary",)),
    )(P, P, ctab, ds0, dlg, rdb)


BLK = 128
N_LOC = 3 * BLK


def _att_inputs(P, rope, L, Lc):
    n = L // BLK
    cb = L // Lc
    prev = lambda i: jnp.maximum(i - 1, 0)
    nxt = lambda i: jnp.minimum(i + 1, n - 1)
    specs = [pl.BlockSpec((BLK, 1024), lambda i: (i, C_AQ // 1024))]
    args = [P]
    for col in (C_AK // 256, C_AV // 256):
        for rm in (prev, lambda i: i, nxt):
            specs.append(pl.BlockSpec((BLK, 256), functools.partial(lambda i, rm, col: (rm(i), col), rm=rm, col=col)))
            args.append(P)
        specs.append(pl.BlockSpec((Lc, 256), functools.partial(lambda i, col: (cb, col), col=col)))
        args.append(P)
    for rm in (prev, lambda i: i, nxt):
        specs.append(pl.BlockSpec((2, BLK, LANES), functools.partial(lambda i, rm: (0, rm(i), 0), rm=rm)))
        args.append(rope)
    return specs, args


def _att_prep(i, n, refs, Lc):
    q_ref, kp_ref, kc_ref, kn_ref, kx_ref, vp_ref, vc_ref, vn_ref, vx_ref, rp_ref, rc_ref, rn_ref = refs
    cos = jnp.concatenate([rp_ref[0], rc_ref[0], rn_ref[0]], axis=0)
    sin = jnp.concatenate([rp_ref[1], rc_ref[1], rn_ref[1]], axis=0)
    kd, vd = [], []
    for t in range(ATT_KV // 2):
        sl = slice(t * LANES, (t + 1) * LANES)
        kl = jnp.concatenate([kp_ref[:, sl], kc_ref[:, sl], kn_ref[:, sl]], axis=0).astype(F32)
        kl = _rope(kl, cos, sin, 16)
        ka = jnp.concatenate([kl, kx_ref[:, sl].astype(F32)], axis=0)
        va = jnp.concatenate([vp_ref[:, sl], vc_ref[:, sl], vn_ref[:, sl], vx_ref[:, sl]], axis=0).astype(F32)
        kr, vr = pltpu.roll(ka, 64, 1), pltpu.roll(va, 64, 1)
        for b in range(2):
            hm = _half_mask(ka.shape, b)
            kd.append(jnp.where(hm, ka, kr).astype(BF16))
            vd.append(jnp.where(hm, va, vr).astype(BF16))
    nk = N_LOC + Lc
    rr = lax.broadcasted_iota(jnp.int32, (4 * BLK, nk), 0) % BLK
    ss = lax.broadcasted_iota(jnp.int32, (4 * BLK, nk), 1)
    lo = jnp.where(i == 0, BLK, 0)
    hi = jnp.where(i == n - 1, 2 * BLK, N_LOC)
    valid = (ss >= N_LOC) | ((ss >= rr) & (ss <= rr + 2 * BLK) & (ss >= lo) & (ss < hi))
    return kd, vd, valid, rc_ref[0], rc_ref[1]


def _stack4(ref, g, f=None):
    parts = []
    for jp in range(2):
        t = ref[:, (2 * g + jp) * LANES:(2 * g + jp + 1) * LANES].astype(F32)
        if f is not None:
            t = f(t)
        for a in range(2):
            parts.append(jnp.where(_half_mask(t.shape, a), t, 0.0))
    return jnp.concatenate(parts, axis=0)


def _unstack4(x4, jp):
    r0 = 2 * jp * BLK
    lo = x4[r0:r0 + BLK]
    hi = x4[r0 + BLK:r0 + 2 * BLK]
    return jnp.where(_half_mask(lo.shape, 0), lo, hi)


def _softmax_sink(sc, valid, sink_col):
    sc = jnp.where(valid, sc, NEG)
    m = jnp.maximum(jnp.max(sc, axis=-1, keepdims=True), sink_col)
    e = jnp.exp(sc - m)
    es = jnp.exp(sink_col - m)
    inv = 1.0 / (jnp.sum(e, axis=-1, keepdims=True) + es)
    return e * inv, es * inv


def _sink_col(sink_ref, g):
    return jnp.concatenate(
        [jnp.zeros((BLK, 1), F32) + sink_ref[4 * g + r:4 * g + r + 1, 0:1] for r in range(4)], axis=0)


def _att_fwd(P, rope, sinkb, L, Lc, comm=()):
    n = L // BLK
    specs, args = _att_inputs(P, rope, L, Lc)

    def body(*refs):
        sink_ref, o_ref = refs[12], refs[13]
        i = pl.program_id(0)
        kd, vd, valid, cq, sq = _att_prep(i, n, refs[:12], Lc)
        for g in range(ATT_KV):
            q4 = _stack4(refs[0], g, lambda t: _rope(t, cq, sq, 16) * A_SCALE).astype(BF16)
            p, _ = _softmax_sink(_dot(q4, kd[g], NT), valid, _sink_col(sink_ref, g))
            o4 = _dot(p.astype(BF16), vd[g], NN)
            for jp in range(2):
                c0 = (2 * g + jp) * LANES
                o_ref[:, c0:c0 + LANES] = _unstack4(o4, jp).astype(o_ref.dtype)

    return _call(
        body, "att_fwd", (n,),
        specs + [pl.BlockSpec((ATT_HEADS, LANES), lambda i: (0, 0))],
        [pl.BlockSpec((BLK, 1024), lambda i: (i, 0))], [jax.ShapeDtypeStruct((L, 1024), BF16)], [],
        ("parallel",), (*args, sinkb), comm)


def _att_bwd(P, rope, sinkb, y_att, dY, L, Lc, comm=()):
    n = L // BLK
    specs, args = _att_inputs(P, rope, L, Lc)
    nk = N_LOC + Lc

    def body(*refs):
        sink_ref, y_ref, dy_ref = refs[12], refs[13], refs[14]
        dq_ref, dkl_ref, dvl_ref, dkx_ref, dvx_ref, dsk_ref = refs[15:21]
        i = pl.program_id(0)

        @pl.when(i == 0)
        def _():
            dkx_ref[...] = jnp.zeros_like(dkx_ref)
            dvx_ref[...] = jnp.zeros_like(dvx_ref)
            dsk_ref[...] = jnp.zeros_like(dsk_ref)

        kd, vd, valid, cq, sq = _att_prep(i, n, refs[:12], Lc)
        dk_t = [jnp.zeros((nk, LANES), F32) for _ in range(ATT_KV // 2)]
        dv_t = [jnp.zeros((nk, LANES), F32) for _ in range(ATT_KV // 2)]
        for g in range(ATT_KV):
            t, b = g // 2, g % 2
            q4 = _stack4(refs[0], g, lambda x: _rope(x, cq, sq, 16) * A_SCALE).astype(BF16)
            do4 = _stack4(dy_ref, g)
            o4 = _stack4(y_ref, g)
            p, ps = _softmax_sink(_dot(q4, kd[g], NT), valid, _sink_col(sink_ref, g))
            delta = jnp.sum(do4 * o4, axis=-1, keepdims=True)
            do4b = do4.astype(BF16)
            dsc = p * (_dot(do4b, vd[g], NT) - delta)
            dsr = -ps * delta
            for r in range(4):
                h = 4 * g + r
                dsk_ref[h:h + 1, :] += jnp.zeros((1, LANES), F32) + jnp.sum(dsr[r * BLK:(r + 1) * BLK])
            dscb = dsc.astype(BF16)
            dq4 = _dot(dscb, kd[g], NN) * A_SCALE
            for jp in range(2):
                c0 = (2 * g + jp) * LANES
                dq_ref[:, c0:c0 + LANES] = _rope_t(_unstack4(dq4, jp), cq, sq, 16)
            dkd = _dot(dscb, q4, TN)
            dvd = _dot(p.astype(BF16), do4b, TN)
            hm = _half_mask(dkd.shape, b)
            dk_t[t] += jnp.where(hm, dkd + pltpu.roll(dkd, 64, 1), 0.0)
            dv_t[t] += jnp.where(hm, dvd + pltpu.roll(dvd, 64, 1), 0.0)
        for t in range(ATT_KV // 2):
            sl = slice(t * LANES, (t + 1) * LANES)
            dkl_ref[0, :, sl] = dk_t[t][:N_LOC]
            dvl_ref[0, :, sl] = dv_t[t][:N_LOC]
            dkx_ref[:, sl] += dk_t[t][N_LOC:]
            dvx_ref[:, sl] += dv_t[t][N_LOC:]

    row = pl.BlockSpec((BLK, 1024), lambda i: (i, 0))
    loc = pl.BlockSpec((1, N_LOC, 256), lambda i: (i, 0, 0))
    cx = pl.BlockSpec((Lc, 256), lambda i: (0, 0))
    return _call(
        body, "att_bwd", (n,),
        specs + [pl.BlockSpec((ATT_HEADS, LANES), lambda i: (0, 0)), row, pl.BlockSpec((BLK, 1024), lambda i: (i, 1))],
        [row, loc, loc, cx, cx, pl.BlockSpec((ATT_HEADS, LANES), lambda i: (0, 0))],
        [jax.ShapeDtypeStruct((L, 1024), F32), jax.ShapeDtypeStruct((n, N_LOC, 256), F32),
         jax.ShapeDtypeStruct((n, N_LOC, 256), F32), jax.ShapeDtypeStruct((Lc, 256), F32),
         jax.ShapeDtypeStruct((Lc, 256), F32), jax.ShapeDtypeStruct((ATT_HEADS, LANES), F32)], [],
        ("arbitrary",), (*args, sinkb, y_att, dY), comm)


def _assemble_dp(L, Lc, dqf, dqb, dkf, dkb, dvf, dvb, drg, daq, dkl, dvl, rope_att, dck, dcv, dkx, dvx):
    n = L // BLK
    nc = Lc // BLK

    def body(dqf_r, dqb_r, dkf_r, dkb_r, dvf_r, dvb_r, drg_r, daq_r, kl0, kl1, kl2, vl0, vl1, vl2, rp_r,
             dck_r, dcv_r, dkx_r, dvx_r, o_ref):
        i = pl.program_id(0)

        @pl.when(i < n)
        def _():
            o_ref[:, C_RQ:C_RK] = (dqf_r[...] + dqb_r[...]).astype(o_ref.dtype)
            o_ref[:, C_RK:C_RV] = (dkf_r[...] + dkb_r[...]).astype(o_ref.dtype)
            o_ref[:, C_RV:C_RG] = (dvf_r[...] + dvb_r[...]).astype(o_ref.dtype)
            o_ref[:, C_RG:C_AQ] = drg_r[...].astype(o_ref.dtype)
            o_ref[:, C_AQ:C_AK] = daq_r[...].astype(o_ref.dtype)
            w0 = jnp.where(i > 0, 1.0, 0.0)
            w2 = jnp.where(i < n - 1, 1.0, 0.0)
            dk = kl0[0, 2 * BLK:3 * BLK, :] * w0 + kl1[0, BLK:2 * BLK, :] + kl2[0, 0:BLK, :] * w2
            dv = vl0[0, 2 * BLK:3 * BLK, :] * w0 + vl1[0, BLK:2 * BLK, :] + vl2[0, 0:BLK, :] * w2
            for t in range(ATT_KV // 2):
                sl = slice(t * LANES, (t + 1) * LANES)
                o_ref[:, C_AK + t * LANES:C_AK + (t + 1) * LANES] = _rope_t(
                    dk[:, sl], rp_r[0], rp_r[1], 16).astype(o_ref.dtype)
            o_ref[:, C_AV:D_PROJ] = dv.astype(o_ref.dtype)

        @pl.when(i >= n)
        def _():
            o_ref[:, C_RQ:C_RK] = jnp.zeros((BLK, C_RK - C_RQ), o_ref.dtype)
            o_ref[:, C_RK:C_RV] = dck_r[...].astype(o_ref.dtype)
            o_ref[:, C_RV:C_RG] = dcv_r[...].astype(o_ref.dtype)
            o_ref[:, C_RG:C_AK] = jnp.zeros((BLK, C_AK - C_RG), o_ref.dtype)
            o_ref[:, C_AK:C_AV] = dkx_r[...].astype(o_ref.dtype)
            o_ref[:, C_AV:D_PROJ] = dvx_r[...].astype(o_ref.dtype)

    xm = lambda i: jnp.minimum(i, n - 1)
    cm = lambda i: jnp.clip(i - n, 0, nc - 1)
    r512 = pl.BlockSpec((BLK, 512), lambda i: (xm(i), 0))
    r1024 = pl.BlockSpec((BLK, 1024), lambda i: (xm(i), 0))
    part = lambda off: pl.BlockSpec((1, N_LOC, 256), lambda i: (jnp.clip(xm(i) + off, 0, n - 1), 0, 0))
    return pl.pallas_call(
        body, name="assemble_dp", grid=(n + nc,),
        in_specs=[r512, r512, r512, r512, r1024, r1024, r1024, r1024,
                  part(-1), part(0), part(1), part(-1), part(0), part(1),
                  pl.BlockSpec((2, BLK, LANES), lambda i: (0, xm(i), 0)),
                  pl.BlockSpec((BLK, 512), lambda i: (cm(i), 0)), pl.BlockSpec((BLK, 1024), lambda i: (cm(i), 0)),
                  pl.BlockSpec((BLK, 256), lambda i: (cm(i), 0)), pl.BlockSpec((BLK, 256), lambda i: (cm(i), 0))],
        out_specs=pl.BlockSpec((BLK, D_PROJ), lambda i: (i, 0)),
        out_shape=jax.ShapeDtypeStruct((L + Lc, D_PROJ), BF16),
        compiler_params=_cparams(("parallel",)),
    )(dqf, dqb, dkf, dkb, dvf, dvb, drg, daq, dkl, dkl, dkl, dvl, dvl, dvl, rope_att, dck, dcv, dkx, dvx)


def _adam_math(w, g, m, v):
    m = ADAM_B1 * m + (1.0 - ADAM_B1) * g
    v = ADAM_B2 * v + (1.0 - ADAM_B2) * (g * g)
    m_hat = m / (1.0 - ADAM_B1 ** ADAM_STEP)
    v_hat = v / (1.0 - ADAM_B2 ** ADAM_STEP)
    delta = -ADAM_LR * (m_hat / (jnp.sqrt(v_hat) + ADAM_EPS) + ADAM_WD * w)
    return delta, m, v


def _adam(name, w, m, v, g=None, parts=None):
    R, C = w.shape
    tr = _pick(R, (256, 128, 64, 32, 16, 8))
    summed = parts is not None
    n_parts = parts.shape[0] if summed else 0

    def body(w_ref, m_ref, v_ref, g_ref, go_ref, d_ref, mo_ref, vo_ref):
        if summed:
            gv = g_ref[0].astype(F32)
            for j in range(1, n_parts):
                gv = gv + g_ref[j].astype(F32)
        else:
            gv = g_ref[...]
        d, mn, vn = _adam_math(w_ref[...], gv, m_ref[...], v_ref[...])
        go_ref[...] = gv
        d_ref[...] = d
        mo_ref[...] = mn
        vo_ref[...] = vn

    row = pl.BlockSpec((tr, C), lambda i: (i, 0))
    gspec = pl.BlockSpec((n_parts, tr, C), lambda i: (0, i, 0)) if summed else row
    return pl.pallas_call(
        body, name=name, grid=(R // tr,),
        in_specs=[row, row, row, gspec], out_specs=[row] * 4,
        out_shape=[jax.ShapeDtypeStruct((R, C), F32)] * 4,
        compiler_params=_cparams(("parallel",)),
    )(w, m, v, parts if summed else g)


def _cols_full(g):
    _, D, C = g.shape
    return jnp.transpose(g, (1, 0, 2)).reshape(D, N_DEV * C)


def _rows_full(g):
    _, R, D = g.shape
    return g.reshape(N_DEV * R, D)


def _cols_slots(g):
    D, N = g.shape
    return jnp.transpose(g.reshape(D, N_DEV, N // N_DEV), (1, 0, 2))


def _rows_slots(g):
    N, D = g.shape
    return g.reshape(N_DEV, N // N_DEV, D)


def _pad_rows(a, rows):
    return jnp.concatenate([a, jnp.zeros((rows - a.shape[0],) + a.shape[1:], a.dtype)], axis=0)


def kernel(x, c, ctx, c_ctx, w_mod, b_mod, norm_mix, norm_ffn, w_in, ret_decay, attn_sink, w_out, w_gate, w_up, w_down, norm_final, loss_target, m_c_ctx, m_w_mod, m_b_mod, m_norm_mix, m_norm_ffn, m_w_in, m_ret_decay, m_attn_sink, m_w_out, m_w_gate, m_w_up, m_w_down, m_norm_final, v_c_ctx, v_w_mod, v_b_mod, v_norm_mix, v_norm_ffn, v_w_in, v_ret_decay, v_attn_sink, v_w_out, v_w_gate, v_w_up, v_w_down, v_norm_final):
    L, D = x.shape[1], x.shape[2]
    Lc = ctx.shape[1]
    DF = w_gate.shape[2] * N_DEV
    C6 = w_mod.shape[2]
    me = _my_id()
    xs, cx, tgt = x[0], ctx[0], loss_target[0]

    W_in = _cols_full(_allgather(w_in[0].astype(BF16), "ag_w_in"))
    ag_out, ag_gate = ("ag", w_out[0].astype(BF16)), ("ag", w_gate[0].astype(BF16))
    ag_up, ag_down = ("ag", w_up[0].astype(BF16)), ("ag", w_down[0].astype(BF16))

    cs = _allgather(c, "ag_c")[:, 0, :]
    s_in = _pad_rows(jnp.concatenate([cs, c_ctx[None, :]], axis=0), 16)
    b_l = lax.dynamic_slice_in_dim(b_mod, me * C6, C6, axis=1)
    mod_parts = _allgather(_mod_fwd(s_in, w_mod[0], b_l), "ag_mod")
    mod = _pad_rows(lax.dynamic_index_in_dim(mod_parts, me, axis=1, keepdims=False).reshape(6, D), 8)
    modc = _pad_rows(mod_parts[:, N_DEV, :].reshape(6, D), 8)
    mix_mod, ffn_mod = mod, jnp.roll(mod, -3, axis=0)
    gt_m, gt_f = mod[2:3], mod[5:6]

    rope_ret, rope_att = _rope_tables(L)
    rdb = jnp.broadcast_to(ret_decay[0].reshape(2 * RET_HEADS, 1, 1), (2 * RET_HEADS, 1, LANES))
    sinkb = jnp.broadcast_to(attn_sink[0].reshape(ATT_HEADS, 1), (ATT_HEADS, LANES))

    tm = _pick(L + Lc, (768, 512, 384, 256, 128))
    tmx = _pick(L, (1024, 512, 256, 128))

    H = _modulate_fwd("mod_mix_fwd", xs, cx, norm_mix, mix_mod, modc)
    (P,), (g_gate,) = _matmul("mm_in", [(H, W_in, 0)], 1, L + Lc, D_PROJ, D, "nn",
                              (tm, _pick(D_PROJ, (1152, 768, 512)), D), [], [BF16], lambda a, e: a, comm=[ag_gate])
    W_gate = _cols_full(g_gate)
    tabs, ctab = _ret_tables(rdb, Lc)
    s0 = _ret_ctx_state(P, ctab, L, Lc)
    (o_f, o_b, st_f, st_b), (g_out,) = _ret_fwd(P, rope_ret, tabs, s0, L, comm=[ag_out])
    W_out = _rows_full(g_out)
    y_ret = _ret_finish_fwd(o_f, o_b, P, L)
    (y_att,), (g_up,) = _att_fwd(P, rope_att, sinkb, L, Lc, comm=[ag_up])
    W_up = _cols_full(g_up)
    Y = jnp.concatenate([y_ret, y_att], axis=1)
    KO = Y.shape[1]
    x1, f_mix = _matmul("mm_out", [(Y, W_out, 0)], 1, L, D, KO, "nn", (tmx, _pick(D, (1024, 512)), KO),
                        [(xs, "mn"), (gt_m, "n")], [F32, BF16],
                        lambda a, e: [e[0] + e[1] * a[0], a[0]])

    H2 = _modulate_fwd("mod_ffn_fwd", x1, None, norm_ffn, ffn_mod, None)

    def swiglu_epi(a, e):
        return [a[0], a[1], a[0] * _sigmoid(a[0]) * a[1]]

    tf = _pick(DF, (512, 256, 128))
    (ga, up, hmid), (g_down,) = _matmul("mm_gate_up", [(H2, W_gate, 0), (H2, W_up, 1)], 2, L, DF, D, "nn",
                                        (tmx, tf, D), [], [BF16, BF16, BF16], swiglu_epi, comm=[ag_down])
    W_down = _rows_full(g_down)
    x2, f_ffn = _matmul("mm_down", [(hmid, W_down, 0)], 1, L, D, DF, "nn", (tmx, _pick(D, (1024, 512)), tf),
                        [(x1, "mn"), (gt_f, "n")], [F32, BF16],
                        lambda a, e: [e[0] + e[1] * a[0], a[0]])

    dx2, dFf, sums_l = _loss_head(x2, tgt, norm_final.reshape(1, D), f_ffn, gt_f)

    def dswiglu_epi(a, e):
        av, uv = e[0].astype(F32), e[1].astype(F32)
        sg = _sigmoid(av)
        return [a[0] * uv * (sg * (1.0 + av * (1.0 - sg))), a[0] * (av * sg)]

    dga, dup = _matmul("mm_d_down", [(dFf, W_down, 0)], 1, L, DF, D, "nt", (tmx, tf, D),
                       [(ga, "mn"), (up, "mn")], [BF16, BF16], dswiglu_epi)
    tkt = _pick(L, (512, 256, 128))
    dW_down = _matmul("mm_gw_down", [(hmid, dFf, 0)], 1, DF, D, L, "tn", (tf, _pick(D, (1024, 512)), tkt),
                      [], [BF16], lambda a, e: a)[0]
    (dH2,), (p_down,) = _matmul("mm_d_gate_up", [(dga, W_gate, 0), (dup, W_up, 0)], 1, L, D, DF, "nt",
                                (tmx, _pick(D, (1024, 512)), tf), [], [BF16], lambda a, e: a,
                                comm=[("a2a", _rows_slots(dW_down))])
    dW_gate, dW_up = _matmul("mm_gw_gate_up", [(H2, dga, 0), (H2, dup, 1)], 2, D, DF, L, "tn",
                             (_pick(D, (1024, 512)), tf, tkt), [], [BF16, BF16], lambda a, e: a)
    dx1, dFm, sums_f = _modulate_bwd("mod_ffn_bwd", x1, None, dH2, norm_ffn, ffn_mod, None, dx2, f_mix, gt_m)

    dY = _matmul("mm_d_out", [(dFm, W_out, 0)], 1, L, KO, D, "nt", (tmx, _pick(KO, (1024, 512)), D),
                 [], [BF16], lambda a, e: a)[0]
    dW_out = _matmul("mm_gw_out", [(Y, dFm, 0)], 1, KO, D, L, "tn",
                     (_pick(KO, (1024, 512)), _pick(D, (1024, 512)), tkt), [], [BF16], lambda a, e: a)[0]
    dO, drg = _ret_finish_bwd(o_f, o_b, P, dY, L)
    (dqf, dkf, dvf, dqb, dkb, dvb, ds0, dlg), (p_gate,) = _ret_bwd(
        P, rope_ret, tabs, st_f, st_b, dO, L, comm=[("a2a", _cols_slots(dW_gate))])
    dck, dcv, d_rd = _ret_ctx_bwd(P, ctab, ds0, dlg, rdb, L, Lc)
    (daq, dkl, dvl, dkx, dvx, d_sink), (p_up,) = _att_bwd(
        P, rope_att, sinkb, y_att, dY, L, Lc, comm=[("a2a", _cols_slots(dW_up))])
    dP = _assemble_dp(L, Lc, dqf, dqb, dkf, dkb, dvf, dvb, drg, daq, dkl, dvl, rope_att, dck, dcv, dkx, dvx)
    tkc = _pick(L + Lc, (768, 384, 256, 128))
    (dW_in,), (p_out,) = _matmul("mm_gw_in", [(H, dP, 0)], 1, D, D_PROJ, L + Lc, "tn",
                                 (_pick(D, (1024, 512)), _pick(D_PROJ, (1152, 768, 512)), tkc), [], [BF16],
                                 lambda a, e: a, comm=[("a2a", _rows_slots(dW_out))])
    (dH,), (p_in,) = _matmul("mm_d_in", [(dP, W_in, 0)], 1, L + Lc, D, D_PROJ, "nt",
                             (tm, _pick(D, (1024, 512)), _pick(D_PROJ, (1152, 768, 512))), [], [BF16],
                             lambda a, e: a, comm=[("a2a", _cols_slots(dW_in))])
    grad_x, sums_m = _modulate_bwd("mod_mix_bwd", xs, cx, dH, norm_mix, mix_mod, modc, dx1, None, None)

    zero = jnp.zeros((1, D), F32)
    dmod = jnp.concatenate([sums_m[0:1], sums_m[1:2], sums_f[6:7], sums_f[0:1], sums_f[1:2], sums_l[2:3]], axis=1)
    dmodc = jnp.concatenate([sums_m[3:4], sums_m[4:5], zero, zero, zero, zero], axis=1)
    dm_all = _allgather(jnp.concatenate([dmod, dmodc], axis=0), "ag_dmod")
    dm_cols = lax.dynamic_slice_in_dim(dm_all, me * C6, C6, axis=2)
    dm_in = jnp.concatenate([dm_cols[:, 0, :], dm_cols[:, 1, :]], axis=0)
    s_bwd = jnp.concatenate([cs, jnp.broadcast_to(c_ctx[None, :], (N_DEV, D))], axis=0)
    g_w_mod, dsil = _mod_bwd(s_bwd, dm_in, w_mod[0])

    lane_pad = lambda a: _pad_rows(a.reshape(-1, 1), LANES).reshape(1, LANES)
    pack = jnp.concatenate([dsil[0:1], sums_m[2:3], sums_f[2:3], sums_l[1:2],
                            lane_pad(d_rd[:, 0]), lane_pad(d_sink[:, 0]), sums_l[3:4, 0:LANES]], axis=1)
    packs = _allgather(pack, "ag_small")
    zl = jnp.zeros((1, LANES), F32)

    def pack_w(a_c, a_nm, a_nf, a_fin, a_rd, a_sk):
        return jnp.concatenate([a_c.reshape(1, D), a_nm, a_nf, a_fin.reshape(1, D), lane_pad(a_rd.reshape(-1)),
                                lane_pad(a_sk.reshape(-1)), zl], axis=1)

    sg, sd, sm, sv = _adam("adam_small", pack_w(c_ctx, norm_mix, norm_ffn, norm_final, ret_decay, attn_sink),
                           pack_w(m_c_ctx, m_norm_mix, m_norm_ffn, m_norm_final, m_ret_decay, m_attn_sink),
                           pack_w(v_c_ctx, v_norm_mix, v_norm_ffn, v_norm_final, v_ret_decay, v_attn_sink),
                           parts=packs)
    loss = sg[0, 4 * D + 2 * LANES]

    def unpack(a):
        return (a[0, 0:D], a[:, D:2 * D], a[:, 2 * D:3 * D], a[0, 3 * D:4 * D],
                a[0, 4 * D:4 * D + 2 * RET_HEADS].reshape(1, 2, RET_HEADS),
                a[:, 4 * D + LANES:4 * D + LANES + ATT_HEADS])

    bg, bd, bm, bv = _adam("adam_b_mod", b_mod, m_b_mod, v_b_mod, parts=dm_all.reshape(2 * N_DEV, 1, 6 * D))
    wg, wd, wm, wv = _adam("adam_w_mod", w_mod[0], m_w_mod[0], v_w_mod[0], g=g_w_mod)

    big = {}
    for nm, w, m, v, parts in (
            ("w_in", w_in, m_w_in, v_w_in, p_in), ("w_out", w_out, m_w_out, v_w_out, p_out),
            ("w_gate", w_gate, m_w_gate, v_w_gate, p_gate), ("w_up", w_up, m_w_up, v_w_up, p_up),
            ("w_down", w_down, m_w_down, v_w_down, p_down)):
        big[nm] = [a[None] for a in _adam("adam_" + nm, w[0], m[0], v[0], parts=parts)]

    g_s, d_s, m_s, v_s = unpack(sg), unpack(sd), unpack(sm), unpack(sv)

    def leaves(k, small, bmod, wmod):
        return (small[0], wmod[None], bmod, small[1], small[2], big["w_in"][k], small[4], small[5],
                big["w_out"][k], big["w_gate"][k], big["w_up"][k], big["w_down"][k], small[3])

    return (loss, grad_x[None], *leaves(0, g_s, bg, wg), *leaves(1, d_s, bd, wd),
            *leaves(2, m_s, bm, wm), *leaves(3, v_s, bv, wv))
```

```python
import functools

import jax
import jax.numpy as jnp
from jax import lax
from jax.experimental import pallas as pl
from jax.experimental.pallas import tpu as pltpu

F32 = jnp.float32
BF16 = jnp.bfloat16

N_DEV = 8
LANES = 128
RET_HEADS = 8
RET_DK = 64
RET_DV = 128
CHUNK = 128
ATT_HEADS = 16
ATT_KV = 4
ATT_DH = 64
GRID_W = 64
ROPE_BASE = 10000.0
EPS = 1e-6
NEG = -1e30
C_RQ, C_RK, C_RV, C_RG, C_AQ, C_AK, C_AV, D_PROJ = 0, 512, 1024, 2048, 3072, 4096, 4352, 4608
K_SCALE = RET_DK ** -0.5
A_SCALE = ATT_DH ** -0.5

ADAM_LR, ADAM_B1, ADAM_B2, ADAM_EPS, ADAM_WD, ADAM_STEP = 0.001, 0.9, 0.999, 1e-08, 0.01, 10

VMEM_BIG = 52 * 1024 * 1024

NN = (((1,), (0,)), ((), ()))
NT = (((1,), (1,)), ((), ()))
TN = (((0,), (0,)), ((), ()))


def _dot(a, b, dims):
    return lax.dot_general(a, b, dims, preferred_element_type=F32)


def _cparams(sem, vmem=VMEM_BIG):
    return pltpu.CompilerParams(dimension_semantics=sem, vmem_limit_bytes=vmem)


def _pick(dim, prefs):
    for p in prefs:
        if dim % p == 0:
            return p
    return dim


def _my_id():
    return lax.axis_index("x") * 4 + lax.axis_index("y") * 2 + lax.axis_index("c")


def _sigmoid(x):
    return 1.0 / (1.0 + jnp.exp(-x))


def _peers():
    mx, my, mc = lax.axis_index("x"), lax.axis_index("y"), lax.axis_index("c")
    out = []
    for k in range(1, N_DEV):
        kx, ky, kc = (k >> 2) & 1, (k >> 1) & 1, k & 1
        px = 1 - mx if kx else mx
        py = 1 - my if ky else my
        pc = 1 - mc if kc else mc
        out.append(((px, py, pc), px * 4 + py * 2 + pc))
    return out


def _exchange_copies(kind, x_ref, o_ref, ssem, rsem, lsem):
    me = _my_id()
    loc = pltpu.make_async_copy(x_ref if kind == "ag" else x_ref.at[me], o_ref.at[me], lsem)
    cps = []
    for k, (peer, pid) in enumerate(_peers()):
        cps.append(pltpu.make_async_remote_copy(
            src_ref=x_ref if kind == "ag" else x_ref.at[pid], dst_ref=o_ref.at[me],
            send_sem=ssem.at[k], recv_sem=rsem.at[k], device_id=peer, device_id_type=pl.DeviceIdType.MESH))
    return loc, cps


def _two_level_copies(x_ref, o_ref, ssem, rsem, lsem):
    mx, my, mc = lax.axis_index("x"), lax.axis_index("y"), lax.axis_index("c")
    me = mx * 4 + my * 2 + mc
    sibling = (mx, my, 1 - mc)
    chips = [(1 - mx, my), (mx, 1 - my), (1 - mx, 1 - my)]

    def copy(k, slot, to, src=None):
        return pltpu.make_async_remote_copy(
            src_ref=o_ref.at[slot] if src is None else src, dst_ref=o_ref.at[slot],
            send_sem=ssem.at[k], recv_sem=rsem.at[k], device_id=to, device_id_type=pl.DeviceIdType.MESH)

    loc = pltpu.make_async_copy(x_ref, o_ref.at[me], lsem)
    first = [copy(0, me, sibling, src=x_ref)]
    first += [copy(1 + j, me, (cx, cy, mc), src=x_ref) for j, (cx, cy) in enumerate(chips)]
    passed = [copy(4 + j, cx * 4 + cy * 2 + mc, sibling) for j, (cx, cy) in enumerate(chips)]
    return loc, first, passed


def _exchange_start(kind, x_ref, o_ref, ssem, rsem, lsem):
    if kind == "ag2":
        loc, first, _ = _two_level_copies(x_ref, o_ref, ssem, rsem, lsem)
        cps = first
    else:
        loc, cps = _exchange_copies(kind, x_ref, o_ref, ssem, rsem, lsem)
    loc.start()
    for cp in cps:
        cp.start()


def _exchange_wait(kind, x_ref, o_ref, ssem, rsem, lsem):
    if kind == "ag2":
        loc, first, passed = _two_level_copies(x_ref, o_ref, ssem, rsem, lsem)
        for j in range(3):
            first[1 + j].wait_recv()
            passed[j].start()
        first[0].wait_recv()
        for cp in passed:
            cp.wait_recv()
        cps = first + passed
    else:
        loc, cps = _exchange_copies(kind, x_ref, o_ref, ssem, rsem, lsem)
        for cp in cps:
            cp.wait_recv()
    for cp in cps:
        cp.wait_send()
    loc.wait()


_EXCHANGE_SEMS = [pltpu.SemaphoreType.DMA((N_DEV - 1,)), pltpu.SemaphoreType.DMA((N_DEV - 1,)),
                  pltpu.SemaphoreType.DMA(())]


def _exchange_shape(kind, x):
    return jax.ShapeDtypeStruct(x.shape if kind == "a2a" else (N_DEV,) + x.shape, x.dtype)


def _exchange(kind, x, name):
    def body(x_ref, o_ref, ssem, rsem, lsem):
        _exchange_start(kind, x_ref, o_ref, ssem, rsem, lsem)
        _exchange_wait(kind, x_ref, o_ref, ssem, rsem, lsem)

    return pl.pallas_call(
        body, name=name, out_shape=_exchange_shape(kind, x),
        in_specs=[pl.BlockSpec(memory_space=pl.ANY)], out_specs=pl.BlockSpec(memory_space=pl.ANY),
        scratch_shapes=list(_EXCHANGE_SEMS),
    )(x)


def _allgather(x, name):
    return _exchange("ag", x, name)


def _call(body, name, grid, in_specs, out_specs, out_shape, scratch_shapes, sem, args, comm=()):
    in_specs, out_specs, out_shape = list(in_specs), list(out_specs), list(out_shape)
    scratch_shapes = list(scratch_shapes)
    if not comm:
        outs = pl.pallas_call(body, name=name, grid=grid, in_specs=in_specs, out_specs=out_specs, out_shape=out_shape,
                              scratch_shapes=scratch_shapes, compiler_params=_cparams(sem))(*args)
        return list(outs), []
    n_in, n_out, n_scr, n_c = len(in_specs), len(out_specs), len(scratch_shapes), len(comm)
    hbm = pl.BlockSpec(memory_space=pl.ANY)

    def wrapped(*refs):
        ins, cins = refs[:n_in], refs[n_in:n_in + n_c]
        outs = refs[n_in + n_c:n_in + n_c + n_out]
        couts = refs[n_in + n_c + n_out:n_in + 2 * n_c + n_out]
        scr = refs[n_in + 2 * n_c + n_out:n_in + 2 * n_c + n_out + n_scr]
        sems = refs[n_in + 2 * n_c + n_out + n_scr:]
        first = pl.program_id(0) == 0
        last = pl.program_id(0) == grid[0] - 1
        for ax in range(1, len(grid)):
            first = first & (pl.program_id(ax) == 0)
            last = last & (pl.program_id(ax) == grid[ax] - 1)

        @pl.when(first)
        def _():
            for c, (kind, _) in enumerate(comm):
                _exchange_start(kind, cins[c], couts[c], *sems[3 * c:3 * c + 3])

        body(*ins, *outs, *scr)

        @pl.when(last)
        def _():
            for c, (kind, _) in enumerate(comm):
                _exchange_wait(kind, cins[c], couts[c], *sems[3 * c:3 * c + 3])

    res = pl.pallas_call(
        wrapped, name=name, grid=grid,
        in_specs=in_specs + [hbm] * n_c, out_specs=out_specs + [hbm] * n_c,
        out_shape=out_shape + [_exchange_shape(kind, arr) for kind, arr in comm],
        scratch_shapes=scratch_shapes + list(_EXCHANGE_SEMS) * n_c,
        compiler_params=_cparams(("arbitrary",) * len(grid)),
    )(*args, *[arr for _, arr in comm])
    return list(res[:n_out]), list(res[n_out:])


def _matmul(name, pairs, n_acc, M, N, K, mode, tiles, extras, out_dtypes, epilogue, j_outer=False, comm=()):
    tm, tn, tk = tiles
    gm, gn, nk = M // tm, N // tn, K // tk
    assert gm * tm == M and gn * tn == N and nk * tk == K, (name, M, N, K, tiles)
    if j_outer:
        grid = (gn, gm, nk)
        ij = lambda g0, g1: (g1, g0)
    else:
        grid = (gm, gn, nk)
        ij = lambda g0, g1: (g0, g1)

    if mode in ("nn", "nt"):
        a_spec = pl.BlockSpec((tm, tk), lambda g0, g1, k: (ij(g0, g1)[0], k))
    else:
        a_spec = pl.BlockSpec((tk, tm), lambda g0, g1, k: (k, ij(g0, g1)[0]))
    if mode == "nt":
        b_spec = pl.BlockSpec((tn, tk), lambda g0, g1, k: (ij(g0, g1)[1], k))
    else:
        b_spec = pl.BlockSpec((tk, tn), lambda g0, g1, k: (k, ij(g0, g1)[1]))
    dims = {"nn": NN, "nt": NT, "tn": TN}[mode]
    mn_spec = pl.BlockSpec((tm, tn), lambda g0, g1, k: ij(g0, g1))
    n_spec = pl.BlockSpec((1, tn), lambda g0, g1, k: (0, ij(g0, g1)[1]))

    in_specs, args = [], []
    for a, b, _ in pairs:
        in_specs += [a_spec, b_spec]
        args += [a, b]
    for arr, kind in extras:
        in_specs.append(mn_spec if kind == "mn" else n_spec)
        args.append(arr)
    n_p, n_e, n_o = len(pairs), len(extras), len(out_dtypes)

    def body(*refs):
        ab = refs[:2 * n_p]
        ex = refs[2 * n_p:2 * n_p + n_e]
        outs = refs[2 * n_p + n_e:2 * n_p + n_e + n_o]
        accs = refs[2 * n_p + n_e + n_o:]
        k = pl.program_id(2)

        def partial_sums():
            sums = [None] * n_acc
            for p, (_, _, ai) in enumerate(pairs):
                d = _dot(ab[2 * p][...], ab[2 * p + 1][...], dims)
                sums[ai] = d if sums[ai] is None else sums[ai] + d
            return sums

        def finish(acc_vals):
            res = epilogue(acc_vals, [e[...] for e in ex])
            for o, r in zip(outs, res):
                o[...] = r.astype(o.dtype)

        if nk == 1:
            finish(partial_sums())
        else:
            @pl.when(k == 0)
            def _():
                for ai, s in enumerate(partial_sums()):
                    accs[ai][...] = s

            @pl.when(k > 0)
            def _():
                for ai, s in enumerate(partial_sums()):
                    accs[ai][...] += s

            @pl.when(k == nk - 1)
            def _():
                finish([a[...] for a in accs])

    outs, couts = _call(
        body, name, grid, in_specs, [mn_spec] * n_o,
        [jax.ShapeDtypeStruct((M, N), dt) for dt in out_dtypes],
        [pltpu.VMEM((tm, tn), F32) for _ in range(n_acc if nk > 1 else 0)],
        ("parallel", "parallel", "arbitrary"), args, comm)
    return (outs, couts) if comm else outs


def _rope_tables(L):
    t = jnp.arange(L, dtype=jnp.int32)
    lane = jnp.arange(LANES, dtype=jnp.int32)
    hl = lane % 64
    f = (hl % 32).astype(F32)
    inv = ROPE_BASE ** (-f / 32.0)
    ang = t.astype(F32)[:, None] * inv[None, :]
    sgn = jnp.where(hl < 32, -1.0, 1.0)[None, :]
    ret = jnp.stack([jnp.cos(ang), jnp.sin(ang) * sgn])
    q = hl % 32
    f2 = (q % 16).astype(F32)
    inv2 = ROPE_BASE ** (-f2 / 16.0)
    pos = jnp.where((hl < 32)[None, :], (t // GRID_W)[:, None], (t % GRID_W)[:, None]).astype(F32)
    ang2 = pos * inv2[None, :]
    sgn2 = jnp.where(q < 16, -1.0, 1.0)[None, :]
    att = jnp.stack([jnp.cos(ang2), jnp.sin(ang2) * sgn2])
    return ret.astype(F32), att.astype(F32)


def _swap(x, sh):
    lane = lax.broadcasted_iota(jnp.int32, x.shape, 1)
    ra = pltpu.roll(x, LANES - sh, 1)
    rb = pltpu.roll(x, sh, 1)
    la = pltpu.roll(lane, LANES - sh, 1)
    partner = jnp.where((lane % (2 * sh)) < sh, lane + sh, lane - sh)
    return jnp.where(la == partner, ra, rb)


def _rope(x, cos, sin, sh):
    return x * cos + _swap(x, sh) * sin


def _rope_t(d, cos, sin, sh):
    return d * cos + _swap(d * sin, sh)


def _half_mask(shape, a):
    lane = lax.broadcasted_iota(jnp.int32, shape, 1)
    return (lane < 64) if a == 0 else (lane >= 64)


def _mod_fwd(s_in, w_l, b_l):
    D, C6 = w_l.shape
    tk = _pick(D, (512, 256, 128))
    nk = D // tk

    def body(s_ref, w_ref, b_ref, o_ref):
        k = pl.program_id(0)
        s = s_ref[...]
        s = s * _sigmoid(s)
        d = jnp.dot(s, w_ref[...], preferred_element_type=F32, precision=lax.Precision.HIGHEST)

        @pl.when(k == 0)
        def _():
            o_ref[...] = d + b_ref[...]

        @pl.when(k > 0)
        def _():
            o_ref[...] += d

    return pl.pallas_call(
        body, name="mod_fwd", grid=(nk,),
        in_specs=[pl.BlockSpec((16, tk), lambda k: (0, k)), pl.BlockSpec((tk, C6), lambda k: (k, 0)),
                  pl.BlockSpec((1, C6), lambda k: (0, 0))],
        out_specs=pl.BlockSpec((16, C6), lambda k: (0, 0)),
        out_shape=jax.ShapeDtypeStruct((16, C6), F32),
        compiler_params=_cparams(("arbitrary",)),
    )(s_in, w_l, b_l)


def _mod_bwd(s_in, dm, w_l):
    D, C6 = w_l.shape
    tk = _pick(D, (512, 256, 128))
    nk = D // tk

    def body(s_ref, dm_ref, w_ref, gw_ref, gc_ref):
        s = s_ref[...]
        sg = _sigmoid(s)
        act = s * sg
        dmv = dm_ref[...]
        gw_ref[...] = lax.dot_general(act, dmv, TN, preferred_element_type=F32, precision=lax.Precision.HIGHEST)
        ds = lax.dot_general(dmv, w_ref[...], NT, preferred_element_type=F32, precision=lax.Precision.HIGHEST)
        dsil = (sg * (1.0 + s * (1.0 - sg)))[8:9, :]
        gc_ref[...] = jnp.zeros((8, tk), F32) + jnp.sum(ds[8:16, :], axis=0, keepdims=True) * dsil

    return pl.pallas_call(
        body, name="mod_bwd", grid=(nk,),
        in_specs=[pl.BlockSpec((16, tk), lambda k: (0, k)), pl.BlockSpec((16, C6), lambda k: (0, 0)),
                  pl.BlockSpec((tk, C6), lambda k: (k, 0))],
        out_specs=[pl.BlockSpec((tk, C6), lambda k: (k, 0)), pl.BlockSpec((8, tk), lambda k: (0, k))],
        out_shape=[jax.ShapeDtypeStruct((D, C6), F32), jax.ShapeDtypeStruct((8, D), F32)],
        compiler_params=_cparams(("parallel",)),
    )(s_in, dm, w_l)


def _norm_rows(x):
    r = lax.rsqrt(jnp.mean(x * x, axis=-1, keepdims=True) + EPS)
    return x * r, r


def _modulate_fwd(name, x, ctx, g, mod, modc):
    L, D = x.shape
    tr = ctx.shape[0] if ctx is not None else _pick(L, (256, 128))
    nx = L // tr
    nt = nx + (1 if ctx is not None else 0)

    def body(*refs):
        if ctx is not None:
            x_ref, c_ref, g_ref, m_ref, mc_ref, o_ref = refs
        else:
            x_ref, g_ref, m_ref, o_ref = refs
        i = pl.program_id(0)

        def run(src, m):
            n, _ = _norm_rows(src[...])
            o_ref[...] = (n * g_ref[...] * (1.0 + m[1:2, :]) + m[0:1, :]).astype(o_ref.dtype)

        if ctx is None:
            run(x_ref, m_ref)
        else:
            @pl.when(i < nx)
            def _():
                run(x_ref, m_ref)

            @pl.when(i >= nx)
            def _():
                run(c_ref, mc_ref)

    row = pl.BlockSpec((tr, D), lambda i: (jnp.minimum(i, nx - 1), 0))
    vec = pl.BlockSpec((1, D), lambda i: (0, 0))
    mv = pl.BlockSpec((8, D), lambda i: (0, 0))
    if ctx is not None:
        in_specs = [row, pl.BlockSpec((tr, D), lambda i: (0, 0)), vec, mv, mv]
        args = (x, ctx, g, mod, modc)
    else:
        in_specs = [row, vec, mv]
        args = (x, g, mod)
    return pl.pallas_call(
        body, name=name, grid=(nt,), in_specs=in_specs,
        out_specs=pl.BlockSpec((tr, D), lambda i: (i, 0)),
        out_shape=jax.ShapeDtypeStruct((nt * tr, D), BF16),
        compiler_params=_cparams(("parallel",)),
    )(*args)


def _residual_modulate_fwd(name, x, fbr, gate, g, mod):
    L, D = x.shape
    tr = _pick(L, (256, 128))

    def body(x_ref, f_ref, gt_ref, g_ref, m_ref, x1_ref, o_ref):
        x1 = x_ref[...] + gt_ref[...] * f_ref[...].astype(F32)
        x1_ref[...] = x1
        n, _ = _norm_rows(x1)
        o_ref[...] = (n * g_ref[...] * (1.0 + m_ref[1:2, :]) + m_ref[0:1, :]).astype(o_ref.dtype)

    row = pl.BlockSpec((tr, D), lambda i: (i, 0))
    vec = pl.BlockSpec((1, D), lambda i: (0, 0))
    return pl.pallas_call(
        body, name=name, grid=(L // tr,),
        in_specs=[row, row, vec, vec, pl.BlockSpec((8, D), lambda i: (0, 0))],
        out_specs=[row, row],
        out_shape=[jax.ShapeDtypeStruct((L, D), F32), jax.ShapeDtypeStruct((L, D), BF16)],
        compiler_params=_cparams(("parallel",)),
    )(x, fbr, gate, g, mod)


def _modulate_bwd(name, x, ctx, dh, g, mod, modc, dres, fbr, gate):
    L, D = x.shape
    tr = ctx.shape[0] if ctx is not None else _pick(L, (256, 128))
    nx = L // tr
    nt = nx + (1 if ctx is not None else 0)
    has_f = fbr is not None

    def body(*refs):
        refs = list(refs)
        x_ref = refs.pop(0)
        c_ref = refs.pop(0) if ctx is not None else None
        dh_ref, g_ref, m_ref = refs.pop(0), refs.pop(0), refs.pop(0)
        mc_ref = refs.pop(0) if ctx is not None else None
        dr_ref = refs.pop(0)
        f_ref = refs.pop(0) if has_f else None
        gt_ref = refs.pop(0) if has_f else None
        dx_ref = refs.pop(0)
        df_ref = refs.pop(0) if has_f else None
        acc_ref = refs.pop(0)
        i = pl.program_id(0)

        @pl.when(i == 0)
        def _():
            acc_ref[...] = jnp.zeros_like(acc_ref)

        def sums(src, m, base, grow):
            n, r = _norm_rows(src[...])
            d = dh_ref[...].astype(F32)
            gg = g_ref[...]
            sc1 = 1.0 + m[1:2, :]
            acc_ref[base:base + 1, :] += jnp.sum(d, axis=0, keepdims=True)
            dn = d * n
            acc_ref[base + 1:base + 2, :] += jnp.sum(dn, axis=0, keepdims=True) * gg
            acc_ref[grow:grow + 1, :] += jnp.sum(dn, axis=0, keepdims=True) * sc1
            dnv = d * (gg * sc1)
            return r * (dnv - n * jnp.mean(dnv * n, axis=-1, keepdims=True))

        def x_rows():
            dx = sums(x_ref, m_ref, 0, 2) + dr_ref[...]
            dx_ref[...] = dx
            if has_f:
                acc_ref[6:7, :] += jnp.sum(dx * f_ref[...].astype(F32), axis=0, keepdims=True)
                df_ref[...] = (dx * gt_ref[...]).astype(df_ref.dtype)

        if ctx is None:
            x_rows()
        else:
            pl.when(i < nx)(x_rows)

            @pl.when(i >= nx)
            def _():
                sums(c_ref, mc_ref, 3, 2)

    row = pl.BlockSpec((tr, D), lambda i: (jnp.minimum(i, nx - 1), 0))
    vec = pl.BlockSpec((1, D), lambda i: (0, 0))
    mv = pl.BlockSpec((8, D), lambda i: (0, 0))
    in_specs, args = [row], [x]
    if ctx is not None:
        in_specs.append(pl.BlockSpec((tr, D), lambda i: (0, 0)))
        args.append(ctx)
    in_specs += [pl.BlockSpec((tr, D), lambda i: (i, 0)), vec, mv]
    args += [dh, g, mod]
    if ctx is not None:
        in_specs.append(mv)
        args.append(modc)
    in_specs.append(row)
    args.append(dres)
    out_specs = [row]
    out_shape = [jax.ShapeDtypeStruct((L, D), F32)]
    if has_f:
        in_specs += [row, vec]
        args += [fbr, gate]
        out_specs.append(row)
        out_shape.append(jax.ShapeDtypeStruct((L, D), BF16))
    out_specs.append(pl.BlockSpec((16, D), lambda i: (0, 0)))
    out_shape.append(jax.ShapeDtypeStruct((16, D), F32))
    return pl.pallas_call(
        body, name=name, grid=(nt,), in_specs=in_specs, out_specs=out_specs, out_shape=out_shape,
        compiler_params=_cparams(("arbitrary",)),
    )(*args)


def _loss_head(x1, tgt, nf, fbr, gate):
    L, D = x1.shape
    tr = _pick(L, (256, 128))

    def body(x_ref, t_ref, w_ref, f_ref, gt_ref, dx_ref, df_ref, acc_ref):
        i = pl.program_id(0)

        @pl.when(i == 0)
        def _():
            acc_ref[...] = jnp.zeros_like(acc_ref)

        n, r = _norm_rows(x_ref[...] + gt_ref[...] * f_ref[...].astype(F32))
        w = w_ref[...]
        e = n * w - t_ref[...]
        acc_ref[0:1, :] += jnp.sum(e * e, axis=0, keepdims=True) * (0.5 / D)
        dout = e * (1.0 / D)
        acc_ref[1:2, :] += jnp.sum(dout * n, axis=0, keepdims=True)
        dn = dout * w
        dx = r * (dn - n * jnp.mean(dn * n, axis=-1, keepdims=True))
        dx_ref[...] = dx
        acc_ref[2:3, :] += jnp.sum(dx * f_ref[...].astype(F32), axis=0, keepdims=True)
        df_ref[...] = (dx * gt_ref[...]).astype(df_ref.dtype)

        @pl.when(i == pl.num_programs(0) - 1)
        def _():
            acc_ref[3:4, :] = jnp.zeros((1, D), F32) + jnp.sum(acc_ref[0:1, :])

    row = pl.BlockSpec((tr, D), lambda i: (i, 0))
    vec = pl.BlockSpec((1, D), lambda i: (0, 0))
    return pl.pallas_call(
        body, name="loss_head", grid=(L // tr,),
        in_specs=[row, row, vec, row, vec],
        out_specs=[row, row, pl.BlockSpec((8, D), lambda i: (0, 0))],
        out_shape=[jax.ShapeDtypeStruct((L, D), F32), jax.ShapeDtypeStruct((L, D), BF16),
                   jax.ShapeDtypeStruct((8, D), F32)],
        compiler_params=_cparams(("arbitrary",)),
    )(x1, tgt, nf, fbr, gate)


N_TAB = 7


def _ret_tables(rdb, Lc):
    def body(rd_ref, t_ref, c_ref):
        d = pl.program_id(0) // RET_HEADS
        fwd = d == 0
        lg = -jnp.exp(rd_ref[0])
        i = lax.broadcasted_iota(jnp.int32, (CHUNK, CHUNK), 0).astype(F32)
        j = lax.broadcasted_iota(jnp.int32, (CHUNK, CHUNK), 1).astype(F32)
        rel = jnp.where(fwd, i - j, j - i)
        mask = (rel > 0.0) | ((rel == 0.0) & fwd)
        dm = jnp.where(mask, jnp.exp(lg * jnp.maximum(rel, 0.0)), 0.0)
        t_ref[0, 0] = dm
        t_ref[0, 1] = rel * dm
        qc = jnp.where(fwd, i + 1.0, CHUNK - i)
        qw = jnp.exp(lg * qc)
        t_ref[0, 2] = qw
        t_ref[0, 3] = qw * qc
        kc = jnp.where(fwd, CHUNK - 1.0 - i, i)
        kw = jnp.exp(lg * kc)
        t_ref[0, 4] = kw
        t_ref[0, 5] = kw * kc
        t_ref[0, 6] = jnp.exp(lg * float(CHUNK)) + jnp.zeros((CHUNK, CHUNK), F32)
        m = lax.broadcasted_iota(jnp.int32, (Lc, LANES), 0).astype(F32)
        cc = jnp.where(fwd, Lc - 1.0 - m, m)
        cw = jnp.exp(lg * cc)
        c_ref[0, 0] = cw
        c_ref[0, 1] = cw * cc

    return pl.pallas_call(
        body, name="ret_tables", grid=(2 * RET_HEADS,),
        in_specs=[pl.BlockSpec((1, 1, LANES), lambda r: (r, 0, 0))],
        out_specs=[pl.BlockSpec((1, N_TAB, CHUNK, CHUNK), lambda r: (r, 0, 0, 0)),
                   pl.BlockSpec((1, 2, Lc, LANES), lambda r: (r, 0, 0, 0))],
        out_shape=[jax.ShapeDtypeStruct((2 * RET_HEADS, N_TAB, CHUNK, CHUNK), F32),
                   jax.ShapeDtypeStruct((2 * RET_HEADS, 2, Lc, LANES), F32)],
        compiler_params=_cparams(("parallel",)),
    )(rdb)


def _ret_ctx_state(P, ctab, L, Lc):
    cb = L // Lc

    def body(k_ref, v_ref, c_ref, s_ref):
        for p in range(RET_HEADS // 2):
            kp = k_ref[:, p * LANES:(p + 1) * LANES].astype(F32) * K_SCALE
            for a in range(2):
                h = 2 * p + a
                kh = jnp.where(_half_mask(kp.shape, a), kp, 0.0)
                vh = v_ref[:, h * RET_DV:(h + 1) * RET_DV]
                for d in range(2):
                    kw = (kh * c_ref[d * RET_HEADS + h, 0]).astype(BF16)
                    s_ref[d * RET_HEADS + h] = _dot(kw, vh, TN)

    return pl.pallas_call(
        body, name="ret_ctx_state", grid=(1,),
        in_specs=[pl.BlockSpec((Lc, 512), lambda i: (cb, C_RK // 512)),
                  pl.BlockSpec((Lc, 1024), lambda i: (cb, C_RV // 1024)),
                  pl.BlockSpec((2 * RET_HEADS, 2, Lc, LANES), lambda i: (0, 0, 0, 0))],
        out_specs=pl.BlockSpec((2 * RET_HEADS, LANES, RET_DV), lambda i: (0, 0, 0)),
        out_shape=jax.ShapeDtypeStruct((2 * RET_HEADS, LANES, RET_DV), F32),
        compiler_params=_cparams(("arbitrary",)),
    )(P, P, ctab)


def _ret_fwd(P, rope, tabs, s0, L, comm=()):
    n = L // CHUNK

    def body(qf, kf, vf, rf, qb, kb, vb, rb, t_ref, s0_ref, of_ref, ob_ref, stf_ref, stb_ref, st):
        s = pl.program_id(0)

        @pl.when(s == 0)
        def _():
            st[...] = s0_ref[...]

        for d, (q_ref, k_ref, v_ref, r_ref, o_ref, so_ref) in enumerate(
                ((qf, kf, vf, rf, of_ref, stf_ref), (qb, kb, vb, rb, ob_ref, stb_ref))):
            cos, sin = r_ref[0], r_ref[1]
            for p in range(RET_HEADS // 2):
                qp = _rope(q_ref[:, p * LANES:(p + 1) * LANES].astype(F32), cos, sin, 32)
                kp = _rope(k_ref[:, p * LANES:(p + 1) * LANES].astype(F32), cos, sin, 32) * K_SCALE
                for a in range(2):
                    h = 2 * p + a
                    r = d * RET_HEADS + h
                    hm = _half_mask(qp.shape, a)
                    qh = jnp.where(hm, qp, 0.0)
                    kh = jnp.where(hm, kp, 0.0)
                    vh = v_ref[:, h * RET_DV:(h + 1) * RET_DV]
                    sp = st[r]
                    so_ref[0, h] = sp[a * RET_DK:(a + 1) * RET_DK, :]
                    sc = _dot(qh.astype(BF16), kh.astype(BF16), NT) * t_ref[r, 0]
                    o = _dot(sc.astype(BF16), vh, NN)
                    o += _dot((qh * t_ref[r, 2]).astype(BF16), sp.astype(BF16), NN)
                    o_ref[:, h * RET_DV:(h + 1) * RET_DV] = o
                    st[r] = t_ref[r, 6] * sp + _dot((kh * t_ref[r, 4]).astype(BF16), vh, TN)

    fw = lambda s: s
    bw = lambda s: n - 1 - s

    def specs(cm):
        return [pl.BlockSpec((CHUNK, 512), lambda s: (cm(s), C_RQ // 512)),
                pl.BlockSpec((CHUNK, 512), lambda s: (cm(s), C_RK // 512)),
                pl.BlockSpec((CHUNK, 1024), lambda s: (cm(s), C_RV // 1024)),
                pl.BlockSpec((2, CHUNK, LANES), lambda s: (0, cm(s), 0))]

    full = lambda shp: pl.BlockSpec(shp, lambda s: (0,) * len(shp))
    return _call(
        body, "ret_fwd", (n,),
        specs(fw) + specs(bw) + [full((2 * RET_HEADS, N_TAB, CHUNK, CHUNK)), full((2 * RET_HEADS, LANES, RET_DV))],
        [pl.BlockSpec((CHUNK, 1024), lambda s: (fw(s), 0)),
         pl.BlockSpec((CHUNK, 1024), lambda s: (bw(s), 0)),
         pl.BlockSpec((1, RET_HEADS, RET_DK, RET_DV), lambda s: (fw(s), 0, 0, 0)),
         pl.BlockSpec((1, RET_HEADS, RET_DK, RET_DV), lambda s: (bw(s), 0, 0, 0))],
        [jax.ShapeDtypeStruct((L, 1024), F32), jax.ShapeDtypeStruct((L, 1024), F32),
         jax.ShapeDtypeStruct((n, RET_HEADS, RET_DK, RET_DV), F32),
         jax.ShapeDtypeStruct((n, RET_HEADS, RET_DK, RET_DV), F32)],
        [pltpu.VMEM((2 * RET_HEADS, LANES, RET_DV), F32)],
        ("arbitrary",), (P, P, P, rope, P, P, P, rope, tabs, s0), comm)


def _ret_finish_fwd(of, ob, P, L):
    tr = _pick(L, (256, 128))

    def body(f_ref, b_ref, g_ref, y_ref):
        for h in range(RET_HEADS):
            sl = slice(h * RET_DV, (h + 1) * RET_DV)
            n, _ = _norm_rows(f_ref[:, sl] + b_ref[:, sl])
            g = g_ref[:, sl].astype(F32)
            y_ref[:, sl] = (n * (g * _sigmoid(g))).astype(y_ref.dtype)

    row = pl.BlockSpec((tr, 1024), lambda i: (i, 0))
    return pl.pallas_call(
        body, name="ret_finish_fwd", grid=(L // tr,),
        in_specs=[row, row, pl.BlockSpec((tr, 1024), lambda i: (i, C_RG // 1024))],
        out_specs=row, out_shape=jax.ShapeDtypeStruct((L, 1024), BF16),
        compiler_params=_cparams(("parallel",)),
    )(of, ob, P)


def _ret_finish_bwd(of, ob, P, dY, L):
    tr = _pick(L, (256, 128))

    def body(f_ref, b_ref, g_ref, dy_ref, do_ref, dg_ref):
        for h in range(RET_HEADS):
            sl = slice(h * RET_DV, (h + 1) * RET_DV)
            n, r = _norm_rows(f_ref[:, sl] + b_ref[:, sl])
            g = g_ref[:, sl].astype(F32)
            sg = _sigmoid(g)
            dy = dy_ref[:, sl].astype(F32)
            dg_ref[:, sl] = (dy * n * (sg * (1.0 + g * (1.0 - sg)))).astype(dg_ref.dtype)
            dn = dy * (g * sg)
            do_ref[:, sl] = (r * (dn - n * jnp.mean(dn * n, axis=-1, keepdims=True))).astype(do_ref.dtype)

    row = pl.BlockSpec((tr, 1024), lambda i: (i, 0))
    return pl.pallas_call(
        body, name="ret_finish_bwd", grid=(L // tr,),
        in_specs=[row, row, pl.BlockSpec((tr, 1024), lambda i: (i, C_RG // 1024)), row],
        out_specs=[row, row],
        out_shape=[jax.ShapeDtypeStruct((L, 1024), BF16), jax.ShapeDtypeStruct((L, 1024), BF16)],
        compiler_params=_cparams(("parallel",)),
    )(of, ob, P, dY)


def _ret_bwd(P, rope, tabs, stf, stb, dO, L, comm=()):
    n = L // CHUNK

    def body(qf, kf, vf, rf, gf, sf, qb, kb, vb, rb, gb, sb, t_ref,
             dqf, dkf, dvf, dqb, dkb, dvb, ds0_ref, dlg_ref, ds):
        s = pl.program_id(0)

        @pl.when(s == 0)
        def _():
            ds[...] = jnp.zeros_like(ds)
            dlg_ref[...] = jnp.zeros_like(dlg_ref)

        for d, (q_ref, k_ref, v_ref, r_ref, g_ref, s_ref, dq_ref, dk_ref, dv_ref) in enumerate(
                ((qf, kf, vf, rf, gf, sf, dqf, dkf, dvf), (qb, kb, vb, rb, gb, sb, dqb, dkb, dvb))):
            cos, sin = r_ref[0], r_ref[1]
            for p in range(RET_HEADS // 2):
                qp = _rope(q_ref[:, p * LANES:(p + 1) * LANES].astype(F32), cos, sin, 32)
                kp = _rope(k_ref[:, p * LANES:(p + 1) * LANES].astype(F32), cos, sin, 32) * K_SCALE
                dqp = jnp.zeros((CHUNK, LANES), F32)
                dkp = jnp.zeros((CHUNK, LANES), F32)
                for a in range(2):
                    h = 2 * p + a
                    r = d * RET_HEADS + h
                    hm = _half_mask(qp.shape, a)
                    qh = jnp.where(hm, qp, 0.0)
                    kh = jnp.where(hm, kp, 0.0)
                    qhb, khb = qh.astype(BF16), kh.astype(BF16)
                    vh = v_ref[:, h * RET_DV:(h + 1) * RET_DV]
                    gh = g_ref[:, h * RET_DV:(h + 1) * RET_DV]
                    zero = jnp.zeros((RET_DK, RET_DV), F32)
                    sp = s_ref[0, h]
                    sp = jnp.concatenate([sp, zero] if a == 0 else [zero, sp], axis=0)
                    dsn = ds[r]
                    dm, rm = t_ref[r, 0], t_ref[r, 1]
                    qw, qwc, kw, kwc, gch = t_ref[r, 2], t_ref[r, 3], t_ref[r, 4], t_ref[r, 5], t_ref[r, 6]
                    am = _dot(qhb, khb, NT)
                    dar = _dot(gh, vh, NT)
                    da = (dar * dm).astype(BF16)
                    xq = _dot(gh, sp.astype(BF16), NT)
                    yk = _dot(vh, dsn.astype(BF16), NT)
                    dqp += _dot(da, khb, NN) + xq * qw
                    dkp += _dot(da, qhb, TN) + yk * kw
                    dvh = _dot((am * dm).astype(BF16), gh, TN) + _dot((kh * kw).astype(BF16), dsn.astype(BF16), NN)
                    dv_ref[:, h * RET_DV:(h + 1) * RET_DV] = dvh
                    part = (jnp.sum(am * dar * rm) + jnp.sum(qh * qwc * xq) + jnp.sum(kh * kwc * yk)
                            + float(CHUNK) * jnp.sum(gch * dsn * sp))
                    dlg_ref[r:r + 1, :] += jnp.zeros((1, LANES), F32) + part
                    ds[r] = gch * dsn + _dot((qh * qw).astype(BF16), gh, TN)
                dq_ref[:, p * LANES:(p + 1) * LANES] = _rope_t(dqp, cos, sin, 32)
                dk_ref[:, p * LANES:(p + 1) * LANES] = _rope_t(dkp * K_SCALE, cos, sin, 32)

        @pl.when(s == n - 1)
        def _():
            ds0_ref[...] = ds[...]

    fw = lambda s: n - 1 - s
    bw = lambda s: s

    def specs(cm):
        return [pl.BlockSpec((CHUNK, 512), lambda s: (cm(s), C_RQ // 512)),
                pl.BlockSpec((CHUNK, 512), lambda s: (cm(s), C_RK // 512)),
                pl.BlockSpec((CHUNK, 1024), lambda s: (cm(s), C_RV // 1024)),
                pl.BlockSpec((2, CHUNK, LANES), lambda s: (0, cm(s), 0)),
                pl.BlockSpec((CHUNK, 1024), lambda s: (cm(s), 0)),
                pl.BlockSpec((1, RET_HEADS, RET_DK, RET_DV), lambda s: (cm(s), 0, 0, 0))]

    def ospecs(cm):
        return [pl.BlockSpec((CHUNK, 512), lambda s: (cm(s), 0)), pl.BlockSpec((CHUNK, 512), lambda s: (cm(s), 0)),
                pl.BlockSpec((CHUNK, 1024), lambda s: (cm(s), 0))]

    oshape = [jax.ShapeDtypeStruct((L, 512), F32), jax.ShapeDtypeStruct((L, 512), F32),
              jax.ShapeDtypeStruct((L, 1024), F32)]
    full = lambda shp: pl.BlockSpec(shp, lambda s: (0,) * len(shp))
    return _call(
        body, "ret_bwd", (n,),
        specs(fw) + specs(bw) + [full((2 * RET_HEADS, N_TAB, CHUNK, CHUNK))],
        ospecs(fw) + ospecs(bw) + [full((2 * RET_HEADS, LANES, RET_DV)), full((2 * RET_HEADS, LANES))],
        oshape + oshape + [jax.ShapeDtypeStruct((2 * RET_HEADS, LANES, RET_DV), F32),
                           jax.ShapeDtypeStruct((2 * RET_HEADS, LANES), F32)],
        [pltpu.VMEM((2 * RET_HEADS, LANES, RET_DV), F32)],
        ("arbitrary",), (P, P, P, rope, dO, stf, P, P, P, rope, dO, stb, tabs), comm)


def _ret_ctx_bwd(P, ctab, ds0, dlg, rdb, L, Lc):
    cb = L // Lc

    def body(k_ref, v_ref, c_ref, ds_ref, dlg_ref, rd_ref, dk_ref, dv_ref, drd_ref):
        for p in range(RET_HEADS // 2):
            kp = k_ref[:, p * LANES:(p + 1) * LANES].astype(F32) * K_SCALE
            dkp = jnp.zeros((Lc, LANES), F32)
            for a in range(2):
                h = 2 * p + a
                kh = jnp.where(_half_mask(kp.shape, a), kp, 0.0)
                vh = v_ref[:, h * RET_DV:(h + 1) * RET_DV]
                dvh = jnp.zeros((Lc, RET_DV), F32)
                for d in range(2):
                    r = d * RET_HEADS + h
                    dsb = ds_ref[r].astype(BF16)
                    cw, cwc = c_ref[r, 0], c_ref[r, 1]
                    y = _dot(vh, dsb, NT)
                    dkp += y * cw
                    dvh += _dot((kh * cw).astype(BF16), dsb, NN)
                    lg = -jnp.exp(rd_ref[r])
                    drd_ref[r:r + 1, :] = (dlg_ref[r:r + 1, :] + jnp.sum(kh * cwc * y)) * lg
                dv_ref[:, h * RET_DV:(h + 1) * RET_DV] = dvh
            dk_ref[:, p * LANES:(p + 1) * LANES] = dkp * K_SCALE

    full = lambda shp: pl.BlockSpec(shp, lambda i: (0,) * len(shp))
    return pl.pallas_call(
        body, name="ret_ctx_bwd", grid=(1,),
        in_specs=[pl.BlockSpec((Lc, 512), lambda i: (cb, C_RK // 512)),
                  pl.BlockSpec((Lc, 1024), lambda i: (cb, C_RV // 1024)),
                  full((2 * RET_HEADS, 2, Lc, LANES)), full((2 * RET_HEADS, LANES, RET_DV)),
                  full((2 * RET_HEADS, LANES)), full((2 * RET_HEADS, 1, LANES))],
        out_specs=[full((Lc, 512)), full((Lc, 1024)), full((2 * RET_HEADS, LANES))],
        out_shape=[jax.ShapeDtypeStruct((Lc, 512), F32), jax.ShapeDtypeStruct((Lc, 1024), F32),
                   jax.ShapeDtypeStruct((2 * RET_HEADS, LANES), F32)],
        compiler_params=_cparams(("arbitrary",)),
    )(P, P, ctab, ds0, dlg, rdb)


BLK = 128
N_LOC = 3 * BLK


def _att_inputs(P, rope, L, Lc):
    n = L // BLK
    cb = L // Lc
    prev = lambda i: jnp.maximum(i - 1, 0)
    nxt = lambda i: jnp.minimum(i + 1, n - 1)
    specs = [pl.BlockSpec((BLK, 1024), lambda i: (i, C_AQ // 1024))]
    args = [P]
    for col in (C_AK // 256, C_AV // 256):
        for rm in (prev, lambda i: i, nxt):
            specs.append(pl.BlockSpec((BLK, 256), functools.partial(lambda i, rm, col: (rm(i), col), rm=rm, col=col)))
            args.append(P)
        specs.append(pl.BlockSpec((Lc, 256), functools.partial(lambda i, col: (cb, col), col=col)))
        args.append(P)
    for rm in (prev, lambda i: i, nxt):
        specs.append(pl.BlockSpec((2, BLK, LANES), functools.partial(lambda i, rm: (0, rm(i), 0), rm=rm)))
        args.append(rope)
    return specs, args


def _att_prep(i, n, refs, Lc):
    q_ref, kp_ref, kc_ref, kn_ref, kx_ref, vp_ref, vc_ref, vn_ref, vx_ref, rp_ref, rc_ref, rn_ref = refs
    cos = jnp.concatenate([rp_ref[0], rc_ref[0], rn_ref[0]], axis=0)
    sin = jnp.concatenate([rp_ref[1], rc_ref[1], rn_ref[1]], axis=0)
    kd, vd = [], []
    for t in range(ATT_KV // 2):
        sl = slice(t * LANES, (t + 1) * LANES)
        kl = jnp.concatenate([kp_ref[:, sl], kc_ref[:, sl], kn_ref[:, sl]], axis=0).astype(F32)
        kl = _rope(kl, cos, sin, 16)
        ka = jnp.concatenate([kl, kx_ref[:, sl].astype(F32)], axis=0)
        va = jnp.concatenate([vp_ref[:, sl], vc_ref[:, sl], vn_ref[:, sl], vx_ref[:, sl]], axis=0).astype(F32)
        kr, vr = pltpu.roll(ka, 64, 1), pltpu.roll(va, 64, 1)
        for b in range(2):
            hm = _half_mask(ka.shape, b)
            kd.append(jnp.where(hm, ka, kr).astype(BF16))
            vd.append(jnp.where(hm, va, vr).astype(BF16))
    nk = N_LOC + Lc
    rr = lax.broadcasted_iota(jnp.int32, (4 * BLK, nk), 0) % BLK
    ss = lax.broadcasted_iota(jnp.int32, (4 * BLK, nk), 1)
    lo = jnp.where(i == 0, BLK, 0)
    hi = jnp.where(i == n - 1, 2 * BLK, N_LOC)
    valid = (ss >= N_LOC) | ((ss >= rr) & (ss <= rr + 2 * BLK) & (ss >= lo) & (ss < hi))
    return kd, vd, valid, rc_ref[0], rc_ref[1]


def _stack4(ref, g, f=None):
    parts = []
    for jp in range(2):
        t = ref[:, (2 * g + jp) * LANES:(2 * g + jp + 1) * LANES].astype(F32)
        if f is not None:
            t = f(t)
        for a in range(2):
            parts.append(jnp.where(_half_mask(t.shape, a), t, 0.0))
    return jnp.concatenate(parts, axis=0)


def _unstack4(x4, jp):
    r0 = 2 * jp * BLK
    lo = x4[r0:r0 + BLK]
    hi = x4[r0 + BLK:r0 + 2 * BLK]
    return jnp.where(_half_mask(lo.shape, 0), lo, hi)


def _softmax_sink(sc, valid, sink_col):
    sc = jnp.where(valid, sc, NEG)
    m = jnp.maximum(jnp.max(sc, axis=-1, keepdims=True), sink_col)
    e = jnp.exp(sc - m)
    es = jnp.exp(sink_col - m)
    inv = 1.0 / (jnp.sum(e, axis=-1, keepdims=True) + es)
    return e * inv, es * inv


def _sink_col(sink_ref, g):
    return jnp.concatenate(
        [jnp.zeros((BLK, 1), F32) + sink_ref[4 * g + r:4 * g + r + 1, 0:1] for r in range(4)], axis=0)


def _att_fwd(P, rope, sinkb, L, Lc, comm=()):
    n = L // BLK
    specs, args = _att_inputs(P, rope, L, Lc)

    def body(*refs):
        sink_ref, o_ref = refs[12], refs[13]
        i = pl.program_id(0)
        kd, vd, valid, cq, sq = _att_prep(i, n, refs[:12], Lc)
        for g in range(ATT_KV):
            q4 = _stack4(refs[0], g, lambda t: _rope(t, cq, sq, 16) * A_SCALE).astype(BF16)
            p, _ = _softmax_sink(_dot(q4, kd[g], NT), valid, _sink_col(sink_ref, g))
            o4 = _dot(p.astype(BF16), vd[g], NN)
            for jp in range(2):
                c0 = (2 * g + jp) * LANES
                o_ref[:, c0:c0 + LANES] = _unstack4(o4, jp).astype(o_ref.dtype)

    return _call(
        body, "att_fwd", (n,),
        specs + [pl.BlockSpec((ATT_HEADS, LANES), lambda i: (0, 0))],
        [pl.BlockSpec((BLK, 1024), lambda i: (i, 0))], [jax.ShapeDtypeStruct((L, 1024), BF16)], [],
        ("parallel",), (*args, sinkb), comm)


def _att_bwd(P, rope, sinkb, y_att, dY, L, Lc, comm=()):
    n = L // BLK
    specs, args = _att_inputs(P, rope, L, Lc)
    nk = N_LOC + Lc

    def body(*refs):
        sink_ref, y_ref, dy_ref = refs[12], refs[13], refs[14]
        dq_ref, dkl_ref, dvl_ref, dkx_ref, dvx_ref, dsk_ref = refs[15:21]
        i = pl.program_id(0)

        @pl.when(i == 0)
        def _():
            dkx_ref[...] = jnp.zeros_like(dkx_ref)
            dvx_ref[...] = jnp.zeros_like(dvx_ref)
            dsk_ref[...] = jnp.zeros_like(dsk_ref)

        kd, vd, valid, cq, sq = _att_prep(i, n, refs[:12], Lc)
        dk_t = [jnp.zeros((nk, LANES), F32) for _ in range(ATT_KV // 2)]
        dv_t = [jnp.zeros((nk, LANES), F32) for _ in range(ATT_KV // 2)]
        for g in range(ATT_KV):
            t, b = g // 2, g % 2
            q4 = _stack4(refs[0], g, lambda x: _rope(x, cq, sq, 16) * A_SCALE).astype(BF16)
            do4 = _stack4(dy_ref, g)
            o4 = _stack4(y_ref, g)
            p, ps = _softmax_sink(_dot(q4, kd[g], NT), valid, _sink_col(sink_ref, g))
            delta = jnp.sum(do4 * o4, axis=-1, keepdims=True)
            do4b = do4.astype(BF16)
            dsc = p * (_dot(do4b, vd[g], NT) - delta)
            dsr = -ps * delta
            for r in range(4):
                h = 4 * g + r
                dsk_ref[h:h + 1, :] += jnp.zeros((1, LANES), F32) + jnp.sum(dsr[r * BLK:(r + 1) * BLK])
            dscb = dsc.astype(BF16)
            dq4 = _dot(dscb, kd[g], NN) * A_SCALE
            for jp in range(2):
                c0 = (2 * g + jp) * LANES
                dq_ref[:, c0:c0 + LANES] = _rope_t(_unstack4(dq4, jp), cq, sq, 16)
            dkd = _dot(dscb, q4, TN)
            dvd = _dot(p.astype(BF16), do4b, TN)
            hm = _half_mask(dkd.shape, b)
            dk_t[t] += jnp.where(hm, dkd + pltpu.roll(dkd, 64, 1), 0.0)
            dv_t[t] += jnp.where(hm, dvd + pltpu.roll(dvd, 64, 1), 0.0)
        for t in range(ATT_KV // 2):
            sl = slice(t * LANES, (t + 1) * LANES)
            dkl_ref[0, :, sl] = dk_t[t][:N_LOC]
            dvl_ref[0, :, sl] = dv_t[t][:N_LOC]
            dkx_ref[:, sl] += dk_t[t][N_LOC:]
            dvx_ref[:, sl] += dv_t[t][N_LOC:]

    row = pl.BlockSpec((BLK, 1024), lambda i: (i, 0))
    loc = pl.BlockSpec((1, N_LOC, 256), lambda i: (i, 0, 0))
    cx = pl.BlockSpec((Lc, 256), lambda i: (0, 0))
    return _call(
        body, "att_bwd", (n,),
        specs + [pl.BlockSpec((ATT_HEADS, LANES), lambda i: (0, 0)), row, pl.BlockSpec((BLK, 1024), lambda i: (i, 1))],
        [row, loc, loc, cx, cx, pl.BlockSpec((ATT_HEADS, LANES), lambda i: (0, 0))],
        [jax.ShapeDtypeStruct((L, 1024), F32), jax.ShapeDtypeStruct((n, N_LOC, 256), F32),
         jax.ShapeDtypeStruct((n, N_LOC, 256), F32), jax.ShapeDtypeStruct((Lc, 256), F32),
         jax.ShapeDtypeStruct((Lc, 256), F32), jax.ShapeDtypeStruct((ATT_HEADS, LANES), F32)], [],
        ("arbitrary",), (*args, sinkb, y_att, dY), comm)


def _assemble_dp(L, Lc, dqf, dqb, dkf, dkb, dvf, dvb, drg, daq, dkl, dvl, rope_att, dck, dcv, dkx, dvx):
    n = L // BLK
    nc = Lc // BLK

    def body(dqf_r, dqb_r, dkf_r, dkb_r, dvf_r, dvb_r, drg_r, daq_r, kl0, kl1, kl2, vl0, vl1, vl2, rp_r,
             dck_r, dcv_r, dkx_r, dvx_r, o_ref):
        i = pl.program_id(0)

        @pl.when(i < n)
        def _():
            o_ref[:, C_RQ:C_RK] = (dqf_r[...] + dqb_r[...]).astype(o_ref.dtype)
            o_ref[:, C_RK:C_RV] = (dkf_r[...] + dkb_r[...]).astype(o_ref.dtype)
            o_ref[:, C_RV:C_RG] = (dvf_r[...] + dvb_r[...]).astype(o_ref.dtype)
            o_ref[:, C_RG:C_AQ] = drg_r[...].astype(o_ref.dtype)
            o_ref[:, C_AQ:C_AK] = daq_r[...].astype(o_ref.dtype)
            w0 = jnp.where(i > 0, 1.0, 0.0)
            w2 = jnp.where(i < n - 1, 1.0, 0.0)
            dk = kl0[0, 2 * BLK:3 * BLK, :] * w0 + kl1[0, BLK:2 * BLK, :] + kl2[0, 0:BLK, :] * w2
            dv = vl0[0, 2 * BLK:3 * BLK, :] * w0 + vl1[0, BLK:2 * BLK, :] + vl2[0, 0:BLK, :] * w2
            for t in range(ATT_KV // 2):
                sl = slice(t * LANES, (t + 1) * LANES)
                o_ref[:, C_AK + t * LANES:C_AK + (t + 1) * LANES] = _rope_t(
                    dk[:, sl], rp_r[0], rp_r[1], 16).astype(o_ref.dtype)
            o_ref[:, C_AV:D_PROJ] = dv.astype(o_ref.dtype)

        @pl.when(i >= n)
        def _():
            o_ref[:, C_RQ:C_RK] = jnp.zeros((BLK, C_RK - C_RQ), o_ref.dtype)
            o_ref[:, C_RK:C_RV] = dck_r[...].astype(o_ref.dtype)
            o_ref[:, C_RV:C_RG] = dcv_r[...].astype(o_ref.dtype)
            o_ref[:, C_RG:C_AK] = jnp.zeros((BLK, C_AK - C_RG), o_ref.dtype)
            o_ref[:, C_AK:C_AV] = dkx_r[...].astype(o_ref.dtype)
            o_ref[:, C_AV:D_PROJ] = dvx_r[...].astype(o_ref.dtype)

    xm = lambda i: jnp.minimum(i, n - 1)
    cm = lambda i: jnp.clip(i - n, 0, nc - 1)
    r512 = pl.BlockSpec((BLK, 512), lambda i: (xm(i), 0))
    r1024 = pl.BlockSpec((BLK, 1024), lambda i: (xm(i), 0))
    part = lambda off: pl.BlockSpec((1, N_LOC, 256), lambda i: (jnp.clip(xm(i) + off, 0, n - 1), 0, 0))
    return pl.pallas_call(
        body, name="assemble_dp", grid=(n + nc,),
        in_specs=[r512, r512, r512, r512, r1024, r1024, r1024, r1024,
                  part(-1), part(0), part(1), part(-1), part(0), part(1),
                  pl.BlockSpec((2, BLK, LANES), lambda i: (0, xm(i), 0)),
                  pl.BlockSpec((BLK, 512), lambda i: (cm(i), 0)), pl.BlockSpec((BLK, 1024), lambda i: (cm(i), 0)),
                  pl.BlockSpec((BLK, 256), lambda i: (cm(i), 0)), pl.BlockSpec((BLK, 256), lambda i: (cm(i), 0))],
        out_specs=pl.BlockSpec((BLK, D_PROJ), lambda i: (i, 0)),
        out_shape=jax.ShapeDtypeStruct((L + Lc, D_PROJ), BF16),
        compiler_params=_cparams(("parallel",)),
    )(dqf, dqb, dkf, dkb, dvf, dvb, drg, daq, dkl, dkl, dkl, dvl, dvl, dvl, rope_att, dck, dcv, dkx, dvx)


def _adam_math(w, g, m, v):
    m = ADAM_B1 * m + (1.0 - ADAM_B1) * g
    v = ADAM_B2 * v + (1.0 - ADAM_B2) * (g * g)
    m_hat = m / (1.0 - ADAM_B1 ** ADAM_STEP)
    v_hat = v / (1.0 - ADAM_B2 ** ADAM_STEP)
    delta = -ADAM_LR * (m_hat / (jnp.sqrt(v_hat) + ADAM_EPS) + ADAM_WD * w)
    return delta, m, v


def _adam(name, w, m, v, g=None, parts=None):
    R, C = w.shape
    tr = _pick(R, (256, 128, 64, 32, 16, 8))
    summed = parts is not None
    n_parts = parts.shape[0] if summed else 0

    def body(w_ref, m_ref, v_ref, g_ref, go_ref, d_ref, mo_ref, vo_ref):
        if summed:
            gv = g_ref[0].astype(F32)
            for j in range(1, n_parts):
                gv = gv + g_ref[j].astype(F32)
        else:
            gv = g_ref[...]
        d, mn, vn = _adam_math(w_ref[...], gv, m_ref[...], v_ref[...])
        go_ref[...] = gv
        d_ref[...] = d
        mo_ref[...] = mn
        vo_ref[...] = vn

    row = pl.BlockSpec((tr, C), lambda i: (i, 0))
    gspec = pl.BlockSpec((n_parts, tr, C), lambda i: (0, i, 0)) if summed else row
    return pl.pallas_call(
        body, name=name, grid=(R // tr,),
        in_specs=[row, row, row, gspec], out_specs=[row] * 4,
        out_shape=[jax.ShapeDtypeStruct((R, C), F32)] * 4,
        compiler_params=_cparams(("parallel",)),
    )(w, m, v, parts if summed else g)


def _cols_full(g):
    _, D, C = g.shape
    return jnp.transpose(g, (1, 0, 2)).reshape(D, N_DEV * C)


def _rows_full(g):
    _, R, D = g.shape
    return g.reshape(N_DEV * R, D)


def _cols_slots(g):
    D, N = g.shape
    return jnp.transpose(g.reshape(D, N_DEV, N // N_DEV), (1, 0, 2))


def _rows_slots(g):
    N, D = g.shape
    return g.reshape(N_DEV, N // N_DEV, D)


def _pad_rows(a, rows):
    return jnp.concatenate([a, jnp.zeros((rows - a.shape[0],) + a.shape[1:], a.dtype)], axis=0)


def kernel(x, c, ctx, c_ctx, w_mod, b_mod, norm_mix, norm_ffn, w_in, ret_decay, attn_sink, w_out, w_gate, w_up, w_down, norm_final, loss_target, m_c_ctx, m_w_mod, m_b_mod, m_norm_mix, m_norm_ffn, m_w_in, m_ret_decay, m_attn_sink, m_w_out, m_w_gate, m_w_up, m_w_down, m_norm_final, v_c_ctx, v_w_mod, v_b_mod, v_norm_mix, v_norm_ffn, v_w_in, v_ret_decay, v_attn_sink, v_w_out, v_w_gate, v_w_up, v_w_down, v_norm_final):
    L, D = x.shape[1], x.shape[2]
    Lc = ctx.shape[1]
    DF = w_gate.shape[2] * N_DEV
    C6 = w_mod.shape[2]
    me = _my_id()
    xs, cx, tgt = x[0], ctx[0], loss_target[0]

    W_in = _cols_full(_exchange("ag2", w_in[0].astype(BF16), "ag_w_in"))
    ag_out, ag_gate = ("ag2", w_out[0].astype(BF16)), ("ag2", w_gate[0].astype(BF16))
    ag_up, ag_down = ("ag2", w_up[0].astype(BF16)), ("ag2", w_down[0].astype(BF16))

    cs = _allgather(c, "ag_c")[:, 0, :]
    s_in = _pad_rows(jnp.concatenate([cs, c_ctx[None, :]], axis=0), 16)
    b_l = lax.dynamic_slice_in_dim(b_mod, me * C6, C6, axis=1)
    mod_parts = _allgather(_mod_fwd(s_in, w_mod[0], b_l), "ag_mod")
    mod = _pad_rows(lax.dynamic_index_in_dim(mod_parts, me, axis=1, keepdims=False).reshape(6, D), 8)
    modc = _pad_rows(mod_parts[:, N_DEV, :].reshape(6, D), 8)
    mix_mod, ffn_mod = mod, jnp.roll(mod, -3, axis=0)
    gt_m, gt_f = mod[2:3], mod[5:6]

    rope_ret, rope_att = _rope_tables(L)
    rdb = jnp.broadcast_to(ret_decay[0].reshape(2 * RET_HEADS, 1, 1), (2 * RET_HEADS, 1, LANES))
    sinkb = jnp.broadcast_to(attn_sink[0].reshape(ATT_HEADS, 1), (ATT_HEADS, LANES))

    tm = _pick(L + Lc, (1408, 768, 512, 384, 256, 128))
    tmx = _pick(L, (1024, 512, 256, 128))

    H = _modulate_fwd("mod_mix_fwd", xs, cx, norm_mix, mix_mod, modc)
    ident = lambda a, e: a
    tP, tD, tF = _pick(D_PROJ, (1152, 768, 512)), _pick(D, (2048, 1024, 512)), _pick(DF, (512, 256, 128))
    (P,), (g_gate,) = _matmul("mm_in", [(H, W_in, 0)], 1, L + Lc, D_PROJ, D, "nn", (tm, tP, D), [], [BF16], ident,
                              comm=[ag_gate])
    W_gate = _cols_full(g_gate)
    tabs, ctab = _ret_tables(rdb, Lc)
    s0 = _ret_ctx_state(P, ctab, L, Lc)
    (o_f, o_b, st_f, st_b), (g_out,) = _ret_fwd(P, rope_ret, tabs, s0, L, comm=[ag_out])
    W_out = _rows_full(g_out)
    y_ret = _ret_finish_fwd(o_f, o_b, P, L)
    (y_att,), (g_up,) = _att_fwd(P, rope_att, sinkb, L, Lc, comm=[ag_up])
    W_up = _cols_full(g_up)
    Y = jnp.concatenate([y_ret, y_att], axis=1)
    KO = Y.shape[1]
    f_mix = _matmul("mm_out", [(Y, W_out, 0)], 1, L, D, KO, "nn", (tmx, tD, KO), [], [BF16], ident)[0]

    x1, H2 = _residual_modulate_fwd("mod_ffn_fwd", xs, f_mix, gt_m, norm_ffn, ffn_mod)

    def swiglu_epi(a, e):
        return [a[0], a[1], a[0] * _sigmoid(a[0]) * a[1]]

    tm2 = tmx
    (ga, up, hmid), (g_down,) = _matmul("mm_gate_up", [(H2, W_gate, 0), (H2, W_up, 1)], 2, L, DF, D, "nn",
                                        (tm2, tF, D), [], [BF16, BF16, BF16], swiglu_epi, comm=[ag_down])
    W_down = _rows_full(g_down)
    f_ffn = _matmul("mm_down", [(hmid, W_down, 0)], 1, L, D, DF, "nn", (tm2, tD, tF), [], [BF16], ident)[0]

    dx2, dFf, sums_l = _loss_head(x1, tgt, norm_final.reshape(1, D), f_ffn, gt_f)

    def dswiglu_epi(a, e):
        av, uv = e[0].astype(F32), e[1].astype(F32)
        sg = _sigmoid(av)
        return [a[0] * uv * (sg * (1.0 + av * (1.0 - sg))), a[0] * (av * sg)]

    dga, dup = _matmul("mm_d_down", [(dFf, W_down, 0)], 1, L, DF, D, "nt", (tm2, tF, D),
                       [(ga, "mn"), (up, "mn")], [BF16, BF16], dswiglu_epi)
    tkt = _pick(L, (512, 256, 128))
    dW_down = _matmul("mm_gw_down", [(hmid, dFf, 0)], 1, DF, D, L, "tn",
                      (_pick(DF, (1408, 512, 256, 128)), tD, tkt), [], [BF16], ident)[0]
    (dH2,), (p_down,) = _matmul("mm_d_gate_up", [(dga, W_gate, 0), (dup, W_up, 0)], 1, L, D, DF, "nt",
                                (tm2, tD, tF), [], [BF16], ident,
                                comm=[("a2a", _rows_slots(dW_down))])
    dW_gate, dW_up = _matmul("mm_gw_gate_up", [(H2, dga, 0), (H2, dup, 1)], 2, D, DF, L, "tn",
                             (_pick(D, (1024, 512)), _pick(DF, (1408, 512, 256, 128)), tkt), [], [BF16, BF16], ident)
    dx1, dFm, sums_f = _modulate_bwd("mod_ffn_bwd", x1, None, dH2, norm_ffn, ffn_mod, None, dx2, f_mix, gt_m)

    tO = _pick(KO, (2048, 1024, 512))
    dY = _matmul("mm_d_out", [(dFm, W_out, 0)], 1, L, KO, D, "nt", (tmx, tO, D), [], [BF16], ident)[0]
    dW_out = _matmul("mm_gw_out", [(Y, dFm, 0)], 1, KO, D, L, "tn", (_pick(KO, (1024, 512)), tD, tkt), [], [BF16],
                     ident)[0]
    dO, drg = _ret_finish_bwd(o_f, o_b, P, dY, L)
    (dqf, dkf, dvf, dqb, dkb, dvb, ds0, dlg), (p_gate,) = _ret_bwd(
        P, rope_ret, tabs, st_f, st_b, dO, L, comm=[("a2a", _cols_slots(dW_gate))])
    dck, dcv, d_rd = _ret_ctx_bwd(P, ctab, ds0, dlg, rdb, L, Lc)
    (daq, dkl, dvl, dkx, dvx, d_sink), (p_up,) = _att_bwd(
        P, rope_att, sinkb, y_att, dY, L, Lc, comm=[("a2a", _cols_slots(dW_up))])
    dP = _assemble_dp(L, Lc, dqf, dqb, dkf, dkb, dvf, dvb, drg, daq, dkl, dvl, rope_att, dck, dcv, dkx, dvx)
    tkc = _pick(L + Lc, (384, 256, 128))
    (dW_in,), (p_out,) = _matmul("mm_gw_in", [(H, dP, 0)], 1, D, D_PROJ, L + Lc, "tn",
                                 (tD, tP, tkc), [], [BF16], ident,
                                 comm=[("a2a", _rows_slots(dW_out))])
    (dH,), (p_in,) = _matmul("mm_d_in", [(dP, W_in, 0)], 1, L + Lc, D, D_PROJ, "nt",
                             (tm, tD, _pick(D_PROJ, (512, 256))), [], [BF16], ident,
                             comm=[("a2a", _cols_slots(dW_in))])
    grad_x, sums_m = _modulate_bwd("mod_mix_bwd", xs, cx, dH, norm_mix, mix_mod, modc, dx1, None, None)

    zero = jnp.zeros((1, D), F32)
    dmod = jnp.concatenate([sums_m[0:1], sums_m[1:2], sums_f[6:7], sums_f[0:1], sums_f[1:2], sums_l[2:3]], axis=1)
    dmodc = jnp.concatenate([sums_m[3:4], sums_m[4:5], zero, zero, zero, zero], axis=1)
    dm_all = _allgather(jnp.concatenate([dmod, dmodc], axis=0), "ag_dmod")
    dm_cols = lax.dynamic_slice_in_dim(dm_all, me * C6, C6, axis=2)
    dm_in = jnp.concatenate([dm_cols[:, 0, :], dm_cols[:, 1, :]], axis=0)
    s_bwd = jnp.concatenate([cs, jnp.broadcast_to(c_ctx[None, :], (N_DEV, D))], axis=0)
    g_w_mod, dsil = _mod_bwd(s_bwd, dm_in, w_mod[0])

    lane_pad = lambda a: _pad_rows(a.reshape(-1, 1), LANES).reshape(1, LANES)
    pack = jnp.concatenate([dsil[0:1], sums_m[2:3], sums_f[2:3], sums_l[1:2],
                            lane_pad(d_rd[:, 0]), lane_pad(d_sink[:, 0]), sums_l[3:4, 0:LANES]], axis=1)
    packs = _allgather(pack, "ag_small")
    zl = jnp.zeros((1, LANES), F32)

    def pack_w(a_c, a_nm, a_nf, a_fin, a_rd, a_sk):
        return jnp.concatenate([a_c.reshape(1, D), a_nm, a_nf, a_fin.reshape(1, D), lane_pad(a_rd.reshape(-1)),
                                lane_pad(a_sk.reshape(-1)), zl], axis=1)

    sg, sd, sm, sv = _adam("adam_small", pack_w(c_ctx, norm_mix, norm_ffn, norm_final, ret_decay, attn_sink),
                           pack_w(m_c_ctx, m_norm_mix, m_norm_ffn, m_norm_final, m_ret_decay, m_attn_sink),
                           pack_w(v_c_ctx, v_norm_mix, v_norm_ffn, v_norm_final, v_ret_decay, v_attn_sink),
                           parts=packs)
    loss = sg[0, 4 * D + 2 * LANES]

    def unpack(a):
        return (a[0, 0:D], a[:, D:2 * D], a[:, 2 * D:3 * D], a[0, 3 * D:4 * D],
                a[0, 4 * D:4 * D + 2 * RET_HEADS].reshape(1, 2, RET_HEADS),
                a[:, 4 * D + LANES:4 * D + LANES + ATT_HEADS])

    bg, bd, bm, bv = _adam("adam_b_mod", b_mod, m_b_mod, v_b_mod, parts=dm_all.reshape(2 * N_DEV, 1, 6 * D))
    wg, wd, wm, wv = _adam("adam_w_mod", w_mod[0], m_w_mod[0], v_w_mod[0], g=g_w_mod)

    big = {}
    for nm, w, m, v, parts in (
            ("w_in", w_in, m_w_in, v_w_in, p_in), ("w_out", w_out, m_w_out, v_w_out, p_out),
            ("w_gate", w_gate, m_w_gate, v_w_gate, p_gate), ("w_up", w_up, m_w_up, v_w_up, p_up),
            ("w_down", w_down, m_w_down, v_w_down, p_down)):
        big[nm] = [a[None] for a in _adam("adam_" + nm, w[0], m[0], v[0], parts=parts)]

    g_s, d_s, m_s, v_s = unpack(sg), unpack(sd), unpack(sm), unpack(sv)

    def leaves(k, small, bmod, wmod):
        return (small[0], wmod[None], bmod, small[1], small[2], big["w_in"][k], small[4], small[5],
                big["w_out"][k], big["w_gate"][k], big["w_up"][k], big["w_down"][k], small[3])

    return (loss, grad_x[None], *leaves(0, g_s, bg, wg), *leaves(1, d_s, bd, wd),
            *leaves(2, m_s, bm, wm), *leaves(3, v_s, bv, wv))
```

```python
import functools

import jax
import jax.numpy as jnp
from jax import lax
from jax.experimental import pallas as pl
from jax.experimental.pallas import tpu as pltpu

F32 = jnp.float32
BF16 = jnp.bfloat16

N_DEV = 8
LANES = 128
RET_HEADS = 8
RET_DK = 64
RET_DV = 128
CHUNK = 128
ATT_HEADS = 16
ATT_KV = 4
ATT_DH = 64
GRID_W = 64
ROPE_BASE = 10000.0
EPS = 1e-6
NEG = -1e30
C_RQ, C_RK, C_RV, C_RG, C_AQ, C_AK, C_AV, D_PROJ = 0, 512, 1024, 2048, 3072, 4096, 4352, 4608
K_SCALE = RET_DK ** -0.5
A_SCALE = ATT_DH ** -0.5

ADAM_LR, ADAM_B1, ADAM_B2, ADAM_EPS, ADAM_WD, ADAM_STEP = 0.001, 0.9, 0.999, 1e-08, 0.01, 10

VMEM_BIG = 52 * 1024 * 1024

NN = (((1,), (0,)), ((), ()))
NT = (((1,), (1,)), ((), ()))
TN = (((0,), (0,)), ((), ()))


def _dot(a, b, dims):
    return lax.dot_general(a, b, dims, preferred_element_type=F32)


def _cparams(sem, vmem=VMEM_BIG):
    return pltpu.CompilerParams(dimension_semantics=sem, vmem_limit_bytes=vmem)


def _pick(dim, prefs):
    for p in prefs:
        if dim % p == 0:
            return p
    return dim


def _my_id():
    return lax.axis_index("x") * 4 + lax.axis_index("y") * 2 + lax.axis_index("c")


def _sigmoid(x):
    return 0.5 * jnp.tanh(0.5 * x) + 0.5


def _peers():
    mx, my, mc = lax.axis_index("x"), lax.axis_index("y"), lax.axis_index("c")
    out = []
    for k in range(1, N_DEV):
        kx, ky, kc = (k >> 2) & 1, (k >> 1) & 1, k & 1
        px = 1 - mx if kx else mx
        py = 1 - my if ky else my
        pc = 1 - mc if kc else mc
        out.append(((px, py, pc), px * 4 + py * 2 + pc))
    return out


def _exchange_copies(kind, x_ref, o_ref, ssem, rsem, lsem):
    me = _my_id()
    loc = pltpu.make_async_copy(x_ref if kind == "ag" else x_ref.at[me], o_ref.at[me], lsem)
    cps = []
    for k, (peer, pid) in enumerate(_peers()):
        cps.append(pltpu.make_async_remote_copy(
            src_ref=x_ref if kind == "ag" else x_ref.at[pid], dst_ref=o_ref.at[me],
            send_sem=ssem.at[k], recv_sem=rsem.at[k], device_id=peer, device_id_type=pl.DeviceIdType.MESH))
    return loc, cps


def _two_level_copies(x_ref, o_ref, ssem, rsem, lsem):
    mx, my, mc = lax.axis_index("x"), lax.axis_index("y"), lax.axis_index("c")
    me = mx * 4 + my * 2 + mc
    sibling = (mx, my, 1 - mc)
    chips = [(1 - mx, my), (mx, 1 - my), (1 - mx, 1 - my)]

    def copy(k, slot, to, src=None):
        return pltpu.make_async_remote_copy(
            src_ref=o_ref.at[slot] if src is None else src, dst_ref=o_ref.at[slot],
            send_sem=ssem.at[k], recv_sem=rsem.at[k], device_id=to, device_id_type=pl.DeviceIdType.MESH)

    loc = pltpu.make_async_copy(x_ref, o_ref.at[me], lsem)
    first = [copy(0, me, sibling, src=x_ref)]
    first += [copy(1 + j, me, (cx, cy, mc), src=x_ref) for j, (cx, cy) in enumerate(chips)]
    passed = [copy(4 + j, cx * 4 + cy * 2 + mc, sibling) for j, (cx, cy) in enumerate(chips)]
    return loc, first, passed


def _exchange_start(kind, x_ref, o_ref, ssem, rsem, lsem):
    if kind == "ag2":
        loc, first, _ = _two_level_copies(x_ref, o_ref, ssem, rsem, lsem)
        cps = first
    else:
        loc, cps = _exchange_copies(kind, x_ref, o_ref, ssem, rsem, lsem)
    loc.start()
    for cp in cps:
        cp.start()


def _exchange_wait(kind, x_ref, o_ref, ssem, rsem, lsem):
    if kind == "ag2":
        loc, first, passed = _two_level_copies(x_ref, o_ref, ssem, rsem, lsem)
        for j in range(3):
            first[1 + j].wait_recv()
            passed[j].start()
        first[0].wait_recv()
        for cp in passed:
            cp.wait_recv()
        cps = first + passed
    else:
        loc, cps = _exchange_copies(kind, x_ref, o_ref, ssem, rsem, lsem)
        for cp in cps:
            cp.wait_recv()
    for cp in cps:
        cp.wait_send()
    loc.wait()


_EXCHANGE_SEMS = [pltpu.SemaphoreType.DMA((N_DEV - 1,)), pltpu.SemaphoreType.DMA((N_DEV - 1,)),
                  pltpu.SemaphoreType.DMA(())]


def _exchange_shape(kind, x):
    return jax.ShapeDtypeStruct(x.shape if kind == "a2a" else (N_DEV,) + x.shape, x.dtype)


def _exchange(kind, x, name):
    def body(x_ref, o_ref, ssem, rsem, lsem):
        _exchange_start(kind, x_ref, o_ref, ssem, rsem, lsem)
        _exchange_wait(kind, x_ref, o_ref, ssem, rsem, lsem)

    return pl.pallas_call(
        body, name=name, out_shape=_exchange_shape(kind, x),
        in_specs=[pl.BlockSpec(memory_space=pl.ANY)], out_specs=pl.BlockSpec(memory_space=pl.ANY),
        scratch_shapes=list(_EXCHANGE_SEMS),
    )(x)


def _allgather(x, name):
    return _exchange("ag", x, name)


def _call(body, name, grid, in_specs, out_specs, out_shape, scratch_shapes, sem, args, comm=()):
    in_specs, out_specs, out_shape = list(in_specs), list(out_specs), list(out_shape)
    scratch_shapes = list(scratch_shapes)
    if not comm:
        outs = pl.pallas_call(body, name=name, grid=grid, in_specs=in_specs, out_specs=out_specs, out_shape=out_shape,
                              scratch_shapes=scratch_shapes, compiler_params=_cparams(sem))(*args)
        return list(outs), []
    n_in, n_out, n_scr, n_c = len(in_specs), len(out_specs), len(scratch_shapes), len(comm)
    hbm = pl.BlockSpec(memory_space=pl.ANY)

    def wrapped(*refs):
        ins, cins = refs[:n_in], refs[n_in:n_in + n_c]
        outs = refs[n_in + n_c:n_in + n_c + n_out]
        couts = refs[n_in + n_c + n_out:n_in + 2 * n_c + n_out]
        scr = refs[n_in + 2 * n_c + n_out:n_in + 2 * n_c + n_out + n_scr]
        sems = refs[n_in + 2 * n_c + n_out + n_scr:]
        first = pl.program_id(0) == 0
        last = pl.program_id(0) == grid[0] - 1
        for ax in range(1, len(grid)):
            first = first & (pl.program_id(ax) == 0)
            last = last & (pl.program_id(ax) == grid[ax] - 1)

        @pl.when(first)
        def _():
            for c, (kind, _) in enumerate(comm):
                _exchange_start(kind, cins[c], couts[c], *sems[3 * c:3 * c + 3])

        body(*ins, *outs, *scr)

        @pl.when(last)
        def _():
            for c, (kind, _) in enumerate(comm):
                _exchange_wait(kind, cins[c], couts[c], *sems[3 * c:3 * c + 3])

    res = pl.pallas_call(
        wrapped, name=name, grid=grid,
        in_specs=in_specs + [hbm] * n_c, out_specs=out_specs + [hbm] * n_c,
        out_shape=out_shape + [_exchange_shape(kind, arr) for kind, arr in comm],
        scratch_shapes=scratch_shapes + list(_EXCHANGE_SEMS) * n_c,
        compiler_params=_cparams(("arbitrary",) * len(grid)),
    )(*args, *[arr for _, arr in comm])
    return list(res[:n_out]), list(res[n_out:])


def _matmul(name, pairs, n_acc, M, N, K, mode, tiles, extras, out_dtypes, epilogue, j_outer=False, comm=()):
    tm, tn, tk = tiles
    gm, gn, nk = M // tm, N // tn, K // tk
    assert gm * tm == M and gn * tn == N and nk * tk == K, (name, M, N, K, tiles)
    if j_outer:
        grid = (gn, gm, nk)
        ij = lambda g0, g1: (g1, g0)
    else:
        grid = (gm, gn, nk)
        ij = lambda g0, g1: (g0, g1)

    if mode in ("nn", "nt"):
        a_spec = pl.BlockSpec((tm, tk), lambda g0, g1, k: (ij(g0, g1)[0], k))
    else:
        a_spec = pl.BlockSpec((tk, tm), lambda g0, g1, k: (k, ij(g0, g1)[0]))
    if mode == "nt":
        b_spec = pl.BlockSpec((tn, tk), lambda g0, g1, k: (ij(g0, g1)[1], k))
    else:
        b_spec = pl.BlockSpec((tk, tn), lambda g0, g1, k: (k, ij(g0, g1)[1]))
    dims = {"nn": NN, "nt": NT, "tn": TN}[mode]
    mn_spec = pl.BlockSpec((tm, tn), lambda g0, g1, k: ij(g0, g1))
    n_spec = pl.BlockSpec((1, tn), lambda g0, g1, k: (0, ij(g0, g1)[1]))

    in_specs, args = [], []
    for a, b, _ in pairs:
        in_specs += [a_spec, b_spec]
        args += [a, b]
    for arr, kind in extras:
        in_specs.append(mn_spec if kind == "mn" else n_spec)
        args.append(arr)
    n_p, n_e, n_o = len(pairs), len(extras), len(out_dtypes)

    def body(*refs):
        ab = refs[:2 * n_p]
        ex = refs[2 * n_p:2 * n_p + n_e]
        outs = refs[2 * n_p + n_e:2 * n_p + n_e + n_o]
        accs = refs[2 * n_p + n_e + n_o:]
        k = pl.program_id(2)

        def partial_sums():
            sums = [None] * n_acc
            for p, (_, _, ai) in enumerate(pairs):
                d = _dot(ab[2 * p][...], ab[2 * p + 1][...], dims)
                sums[ai] = d if sums[ai] is None else sums[ai] + d
            return sums

        def finish(acc_vals):
            res = epilogue(acc_vals, [e[...] for e in ex])
            for o, r in zip(outs, res):
                o[...] = r.astype(o.dtype)

        if nk == 1:
            finish(partial_sums())
        else:
            @pl.when(k == 0)
            def _():
                for ai, s in enumerate(partial_sums()):
                    accs[ai][...] = s

            @pl.when(k > 0)
            def _():
                for ai, s in enumerate(partial_sums()):
                    accs[ai][...] += s

            @pl.when(k == nk - 1)
            def _():
                finish([a[...] for a in accs])

    outs, couts = _call(
        body, name, grid, in_specs, [mn_spec] * n_o,
        [jax.ShapeDtypeStruct((M, N), dt) for dt in out_dtypes],
        [pltpu.VMEM((tm, tn), F32) for _ in range(n_acc if nk > 1 else 0)],
        ("parallel", "parallel", "arbitrary"), args, comm)
    return (outs, couts) if comm else outs


def _rope_tables(L):
    t = jnp.arange(L, dtype=jnp.int32)
    lane = jnp.arange(LANES, dtype=jnp.int32)
    hl = lane % 64
    f = (hl % 32).astype(F32)
    inv = ROPE_BASE ** (-f / 32.0)
    ang = t.astype(F32)[:, None] * inv[None, :]
    sgn = jnp.where(hl < 32, -1.0, 1.0)[None, :]
    ret = jnp.stack([jnp.cos(ang), jnp.sin(ang) * sgn])
    q = hl % 32
    f2 = (q % 16).astype(F32)
    inv2 = ROPE_BASE ** (-f2 / 16.0)
    pos = jnp.where((hl < 32)[None, :], (t // GRID_W)[:, None], (t % GRID_W)[:, None]).astype(F32)
    ang2 = pos * inv2[None, :]
    sgn2 = jnp.where(q < 16, -1.0, 1.0)[None, :]
    att = jnp.stack([jnp.cos(ang2), jnp.sin(ang2) * sgn2])
    return ret.astype(F32), att.astype(F32)


def _swap(x, sh):
    lane = lax.broadcasted_iota(jnp.int32, x.shape, 1)
    ra = pltpu.roll(x, LANES - sh, 1)
    rb = pltpu.roll(x, sh, 1)
    la = pltpu.roll(lane, LANES - sh, 1)
    partner = jnp.where((lane % (2 * sh)) < sh, lane + sh, lane - sh)
    return jnp.where(la == partner, ra, rb)


def _rope(x, cos, sin, sh):
    return x * cos + _swap(x, sh) * sin


def _rope_t(d, cos, sin, sh):
    return d * cos + _swap(d * sin, sh)


def _half_mask(shape, a):
    lane = lax.broadcasted_iota(jnp.int32, shape, 1)
    return (lane < 64) if a == 0 else (lane >= 64)


def _mod_fwd(s_in, w_l, b_l):
    D, C6 = w_l.shape
    tk = _pick(D, (512, 256, 128))
    nk = D // tk

    def body(s_ref, w_ref, b_ref, o_ref):
        k = pl.program_id(0)
        s = s_ref[...]
        s = s * _sigmoid(s)
        d = jnp.dot(s, w_ref[...], preferred_element_type=F32, precision=lax.Precision.HIGHEST)

        @pl.when(k == 0)
        def _():
            o_ref[...] = d + b_ref[...]

        @pl.when(k > 0)
        def _():
            o_ref[...] += d

    return pl.pallas_call(
        body, name="mod_fwd", grid=(nk,),
        in_specs=[pl.BlockSpec((16, tk), lambda k: (0, k)), pl.BlockSpec((tk, C6), lambda k: (k, 0)),
                  pl.BlockSpec((1, C6), lambda k: (0, 0))],
        out_specs=pl.BlockSpec((16, C6), lambda k: (0, 0)),
        out_shape=jax.ShapeDtypeStruct((16, C6), F32),
        compiler_params=_cparams(("arbitrary",)),
    )(s_in, w_l, b_l)


def _mod_bwd(s_in, dm, w_l):
    D, C6 = w_l.shape
    tk = _pick(D, (512, 256, 128))
    nk = D // tk

    def body(s_ref, dm_ref, w_ref, gw_ref, gc_ref):
        s = s_ref[...]
        sg = _sigmoid(s)
        act = s * sg
        dmv = dm_ref[...]
        gw_ref[...] = lax.dot_general(act, dmv, TN, preferred_element_type=F32, precision=lax.Precision.HIGHEST)
        ds = lax.dot_general(dmv, w_ref[...], NT, preferred_element_type=F32, precision=lax.Precision.HIGHEST)
        dsil = (sg * (1.0 + s * (1.0 - sg)))[8:9, :]
        gc_ref[...] = jnp.zeros((8, tk), F32) + jnp.sum(ds[8:16, :], axis=0, keepdims=True) * dsil

    return pl.pallas_call(
        body, name="mod_bwd", grid=(nk,),
        in_specs=[pl.BlockSpec((16, tk), lambda k: (0, k)), pl.BlockSpec((16, C6), lambda k: (0, 0)),
                  pl.BlockSpec((tk, C6), lambda k: (k, 0))],
        out_specs=[pl.BlockSpec((tk, C6), lambda k: (k, 0)), pl.BlockSpec((8, tk), lambda k: (0, k))],
        out_shape=[jax.ShapeDtypeStruct((D, C6), F32), jax.ShapeDtypeStruct((8, D), F32)],
        compiler_params=_cparams(("parallel",)),
    )(s_in, dm, w_l)


def _norm_rows(x):
    r = lax.rsqrt(jnp.mean(x * x, axis=-1, keepdims=True) + EPS)
    return x * r, r


def _modulate_fwd(name, x, ctx, g, mod, modc):
    L, D = x.shape
    tr = ctx.shape[0] if ctx is not None else _pick(L, (256, 128))
    nx = L // tr
    nt = nx + (1 if ctx is not None else 0)

    def body(*refs):
        if ctx is not None:
            x_ref, c_ref, g_ref, m_ref, mc_ref, o_ref = refs
        else:
            x_ref, g_ref, m_ref, o_ref = refs
        i = pl.program_id(0)

        def run(src, m):
            n, _ = _norm_rows(src[...])
            o_ref[...] = (n * g_ref[...] * (1.0 + m[1:2, :]) + m[0:1, :]).astype(o_ref.dtype)

        if ctx is None:
            run(x_ref, m_ref)
        else:
            @pl.when(i < nx)
            def _():
                run(x_ref, m_ref)

            @pl.when(i >= nx)
            def _():
                run(c_ref, mc_ref)

    row = pl.BlockSpec((tr, D), lambda i: (jnp.minimum(i, nx - 1), 0))
    vec = pl.BlockSpec((1, D), lambda i: (0, 0))
    mv = pl.BlockSpec((8, D), lambda i: (0, 0))
    if ctx is not None:
        in_specs = [row, pl.BlockSpec((tr, D), lambda i: (0, 0)), vec, mv, mv]
        args = (x, ctx, g, mod, modc)
    else:
        in_specs = [row, vec, mv]
        args = (x, g, mod)
    return pl.pallas_call(
        body, name=name, grid=(nt,), in_specs=in_specs,
        out_specs=pl.BlockSpec((tr, D), lambda i: (i, 0)),
        out_shape=jax.ShapeDtypeStruct((nt * tr, D), BF16),
        compiler_params=_cparams(("parallel",)),
    )(*args)


def _residual_modulate_fwd(name, x, fbr, gate, g, mod):
    L, D = x.shape
    tr = _pick(L, (256, 128))

    def body(x_ref, f_ref, gt_ref, g_ref, m_ref, x1_ref, o_ref):
        x1 = x_ref[...] + gt_ref[...] * f_ref[...].astype(F32)
        x1_ref[...] = x1
        n, _ = _norm_rows(x1)
        o_ref[...] = (n * g_ref[...] * (1.0 + m_ref[1:2, :]) + m_ref[0:1, :]).astype(o_ref.dtype)

    row = pl.BlockSpec((tr, D), lambda i: (i, 0))
    vec = pl.BlockSpec((1, D), lambda i: (0, 0))
    return pl.pallas_call(
        body, name=name, grid=(L // tr,),
        in_specs=[row, row, vec, vec, pl.BlockSpec((8, D), lambda i: (0, 0))],
        out_specs=[row, row],
        out_shape=[jax.ShapeDtypeStruct((L, D), F32), jax.ShapeDtypeStruct((L, D), BF16)],
        compiler_params=_cparams(("parallel",)),
    )(x, fbr, gate, g, mod)


def _modulate_bwd(name, x, ctx, dh, g, mod, modc, dres, fbr, gate):
    L, D = x.shape
    tr = ctx.shape[0] if ctx is not None else _pick(L, (256, 128))
    nx = L // tr
    nt = nx + (1 if ctx is not None else 0)
    has_f = fbr is not None

    def body(*refs):
        refs = list(refs)
        x_ref = refs.pop(0)
        c_ref = refs.pop(0) if ctx is not None else None
        dh_ref, g_ref, m_ref = refs.pop(0), refs.pop(0), refs.pop(0)
        mc_ref = refs.pop(0) if ctx is not None else None
        dr_ref = refs.pop(0)
        f_ref = refs.pop(0) if has_f else None
        gt_ref = refs.pop(0) if has_f else None
        dx_ref = refs.pop(0)
        df_ref = refs.pop(0) if has_f else None
        acc_ref = refs.pop(0)
        i = pl.program_id(0)

        @pl.when(i == 0)
        def _():
            acc_ref[...] = jnp.zeros_like(acc_ref)

        def sums(src, m, base, grow):
            n, r = _norm_rows(src[...])
            d = dh_ref[...].astype(F32)
            gg = g_ref[...]
            sc1 = 1.0 + m[1:2, :]
            acc_ref[base:base + 1, :] += jnp.sum(d, axis=0, keepdims=True)
            dn = d * n
            acc_ref[base + 1:base + 2, :] += jnp.sum(dn, axis=0, keepdims=True) * gg
            acc_ref[grow:grow + 1, :] += jnp.sum(dn, axis=0, keepdims=True) * sc1
            dnv = d * (gg * sc1)
            return r * (dnv - n * jnp.mean(dnv * n, axis=-1, keepdims=True))

        def x_rows():
            dx = sums(x_ref, m_ref, 0, 2) + dr_ref[...]
            dx_ref[...] = dx
            if has_f:
                acc_ref[6:7, :] += jnp.sum(dx * f_ref[...].astype(F32), axis=0, keepdims=True)
                df_ref[...] = (dx * gt_ref[...]).astype(df_ref.dtype)

        if ctx is None:
            x_rows()
        else:
            pl.when(i < nx)(x_rows)

            @pl.when(i >= nx)
            def _():
                sums(c_ref, mc_ref, 3, 2)

    row = pl.BlockSpec((tr, D), lambda i: (jnp.minimum(i, nx - 1), 0))
    vec = pl.BlockSpec((1, D), lambda i: (0, 0))
    mv = pl.BlockSpec((8, D), lambda i: (0, 0))
    in_specs, args = [row], [x]
    if ctx is not None:
        in_specs.append(pl.BlockSpec((tr, D), lambda i: (0, 0)))
        args.append(ctx)
    in_specs += [pl.BlockSpec((tr, D), lambda i: (i, 0)), vec, mv]
    args += [dh, g, mod]
    if ctx is not None:
        in_specs.append(mv)
        args.append(modc)
    in_specs.append(row)
    args.append(dres)
    out_specs = [row]
    out_shape = [jax.ShapeDtypeStruct((L, D), F32)]
    if has_f:
        in_specs += [row, vec]
        args += [fbr, gate]
        out_specs.append(row)
        out_shape.append(jax.ShapeDtypeStruct((L, D), BF16))
    out_specs.append(pl.BlockSpec((16, D), lambda i: (0, 0)))
    out_shape.append(jax.ShapeDtypeStruct((16, D), F32))
    return pl.pallas_call(
        body, name=name, grid=(nt,), in_specs=in_specs, out_specs=out_specs, out_shape=out_shape,
        compiler_params=_cparams(("arbitrary",)),
    )(*args)


def _loss_head(x1, tgt, nf, fbr, gate):
    L, D = x1.shape
    tr = _pick(L, (256, 128))

    def body(x_ref, t_ref, w_ref, f_ref, gt_ref, dx_ref, df_ref, acc_ref):
        i = pl.program_id(0)

        @pl.when(i == 0)
        def _():
            acc_ref[...] = jnp.zeros_like(acc_ref)

        n, r = _norm_rows(x_ref[...] + gt_ref[...] * f_ref[...].astype(F32))
        w = w_ref[...]
        e = n * w - t_ref[...]
        acc_ref[0:1, :] += jnp.sum(e * e, axis=0, keepdims=True) * (0.5 / D)
        dout = e * (1.0 / D)
        acc_ref[1:2, :] += jnp.sum(dout * n, axis=0, keepdims=True)
        dn = dout * w
        dx = r * (dn - n * jnp.mean(dn * n, axis=-1, keepdims=True))
        dx_ref[...] = dx
        acc_ref[2:3, :] += jnp.sum(dx * f_ref[...].astype(F32), axis=0, keepdims=True)
        df_ref[...] = (dx * gt_ref[...]).astype(df_ref.dtype)

        @pl.when(i == pl.num_programs(0) - 1)
        def _():
            acc_ref[3:4, :] = jnp.zeros((1, D), F32) + jnp.sum(acc_ref[0:1, :])

    row = pl.BlockSpec((tr, D), lambda i: (i, 0))
    vec = pl.BlockSpec((1, D), lambda i: (0, 0))
    return pl.pallas_call(
        body, name="loss_head", grid=(L // tr,),
        in_specs=[row, row, vec, row, vec],
        out_specs=[row, row, pl.BlockSpec((8, D), lambda i: (0, 0))],
        out_shape=[jax.ShapeDtypeStruct((L, D), F32), jax.ShapeDtypeStruct((L, D), BF16),
                   jax.ShapeDtypeStruct((8, D), F32)],
        compiler_params=_cparams(("arbitrary",)),
    )(x1, tgt, nf, fbr, gate)


N_TAB = 7


def _ret_tables(rdb, Lc):
    def body(rd_ref, t_ref, c_ref):
        d = pl.program_id(0) // RET_HEADS
        fwd = d == 0
        lg = -jnp.exp(rd_ref[0])
        i = lax.broadcasted_iota(jnp.int32, (CHUNK, CHUNK), 0).astype(F32)
        j = lax.broadcasted_iota(jnp.int32, (CHUNK, CHUNK), 1).astype(F32)
        rel = jnp.where(fwd, i - j, j - i)
        mask = (rel > 0.0) | ((rel == 0.0) & fwd)
        dm = jnp.where(mask, jnp.exp(lg * jnp.maximum(rel, 0.0)), 0.0)
        t_ref[0, 0] = dm
        t_ref[0, 1] = rel * dm
        qc = jnp.where(fwd, i + 1.0, CHUNK - i)
        qw = jnp.exp(lg * qc)
        t_ref[0, 2] = qw
        t_ref[0, 3] = qw * qc
        kc = jnp.where(fwd, CHUNK - 1.0 - i, i)
        kw = jnp.exp(lg * kc)
        t_ref[0, 4] = kw
        t_ref[0, 5] = kw * kc
        t_ref[0, 6] = jnp.exp(lg * float(CHUNK)) + jnp.zeros((CHUNK, CHUNK), F32)
        m = lax.broadcasted_iota(jnp.int32, (Lc, LANES), 0).astype(F32)
        cc = jnp.where(fwd, Lc - 1.0 - m, m)
        cw = jnp.exp(lg * cc)
        c_ref[0, 0] = cw
        c_ref[0, 1] = cw * cc

    return pl.pallas_call(
        body, name="ret_tables", grid=(2 * RET_HEADS,),
        in_specs=[pl.BlockSpec((1, 1, LANES), lambda r: (r, 0, 0))],
        out_specs=[pl.BlockSpec((1, N_TAB, CHUNK, CHUNK), lambda r: (r, 0, 0, 0)),
                   pl.BlockSpec((1, 2, Lc, LANES), lambda r: (r, 0, 0, 0))],
        out_shape=[jax.ShapeDtypeStruct((2 * RET_HEADS, N_TAB, CHUNK, CHUNK), F32),
                   jax.ShapeDtypeStruct((2 * RET_HEADS, 2, Lc, LANES), F32)],
        compiler_params=_cparams(("parallel",)),
    )(rdb)


def _ret_ctx_state(P, ctab, L, Lc):
    cb = L // Lc

    def body(k_ref, v_ref, c_ref, s_ref):
        for p in range(RET_HEADS // 2):
            kp = k_ref[:, p * LANES:(p + 1) * LANES].astype(F32) * K_SCALE
            for a in range(2):
                h = 2 * p + a
                kh = jnp.where(_half_mask(kp.shape, a), kp, 0.0)
                vh = v_ref[:, h * RET_DV:(h + 1) * RET_DV]
                for d in range(2):
                    kw = (kh * c_ref[d * RET_HEADS + h, 0]).astype(BF16)
                    s_ref[d * RET_HEADS + h] = _dot(kw, vh, TN)

    return pl.pallas_call(
        body, name="ret_ctx_state", grid=(1,),
        in_specs=[pl.BlockSpec((Lc, 512), lambda i: (cb, C_RK // 512)),
                  pl.BlockSpec((Lc, 1024), lambda i: (cb, C_RV // 1024)),
                  pl.BlockSpec((2 * RET_HEADS, 2, Lc, LANES), lambda i: (0, 0, 0, 0))],
        out_specs=pl.BlockSpec((2 * RET_HEADS, LANES, RET_DV), lambda i: (0, 0, 0)),
        out_shape=jax.ShapeDtypeStruct((2 * RET_HEADS, LANES, RET_DV), F32),
        compiler_params=_cparams(("arbitrary",)),
    )(P, P, ctab)


def _ret_fwd(P, rope, tabs, s0, L, comm=()):
    n = L // CHUNK

    def body(qf, kf, vf, rf, qb, kb, vb, rb, t_ref, s0_ref, of_ref, ob_ref, stf_ref, stb_ref, st):
        s = pl.program_id(0)

        @pl.when(s == 0)
        def _():
            st[...] = s0_ref[...]

        units = []
        for d, (q_ref, k_ref, v_ref, r_ref, o_ref, so_ref) in enumerate(
                ((qf, kf, vf, rf, of_ref, stf_ref), (qb, kb, vb, rb, ob_ref, stb_ref))):
            cos, sin = r_ref[0], r_ref[1]
            for p in range(RET_HEADS // 2):
                qp = _rope(q_ref[:, p * LANES:(p + 1) * LANES].astype(F32), cos, sin, 32)
                kp = _rope(k_ref[:, p * LANES:(p + 1) * LANES].astype(F32), cos, sin, 32) * K_SCALE
                for a in range(2):
                    h = 2 * p + a
                    hm = _half_mask(qp.shape, a)
                    units.append(dict(r=d * RET_HEADS + h, h=h, a=a, o_ref=o_ref, so_ref=so_ref, v_ref=v_ref,
                                      qh=jnp.where(hm, qp, 0.0), kh=jnp.where(hm, kp, 0.0)))
        for u in units:
            u["sc"] = _dot(u["qh"].astype(BF16), u["kh"].astype(BF16), NT)
        for u in units:
            r, h = u["r"], u["h"]
            sp = st[r]
            u["so_ref"][0, h] = sp[u["a"] * RET_DK:(u["a"] + 1) * RET_DK, :]
            vh = u["v_ref"][:, h * RET_DV:(h + 1) * RET_DV]
            o = _dot((u["sc"] * t_ref[r, 0]).astype(BF16), vh, NN)
            o += _dot((u["qh"] * t_ref[r, 2]).astype(BF16), sp.astype(BF16), NN)
            u["o_ref"][:, h * RET_DV:(h + 1) * RET_DV] = o
        for u in units:
            r, h = u["r"], u["h"]
            vh = u["v_ref"][:, h * RET_DV:(h + 1) * RET_DV]
            st[r] = t_ref[r, 6] * st[r] + _dot((u["kh"] * t_ref[r, 4]).astype(BF16), vh, TN)

    fw = lambda s: s
    bw = lambda s: n - 1 - s

    def specs(cm):
        return [pl.BlockSpec((CHUNK, 512), lambda s: (cm(s), C_RQ // 512)),
                pl.BlockSpec((CHUNK, 512), lambda s: (cm(s), C_RK // 512)),
                pl.BlockSpec((CHUNK, 1024), lambda s: (cm(s), C_RV // 1024)),
                pl.BlockSpec((2, CHUNK, LANES), lambda s: (0, cm(s), 0))]

    full = lambda shp: pl.BlockSpec(shp, lambda s: (0,) * len(shp))
    return _call(
        body, "ret_fwd", (n,),
        specs(fw) + specs(bw) + [full((2 * RET_HEADS, N_TAB, CHUNK, CHUNK)), full((2 * RET_HEADS, LANES, RET_DV))],
        [pl.BlockSpec((CHUNK, 1024), lambda s: (fw(s), 0)),
         pl.BlockSpec((CHUNK, 1024), lambda s: (bw(s), 0)),
         pl.BlockSpec((1, RET_HEADS, RET_DK, RET_DV), lambda s: (fw(s), 0, 0, 0)),
         pl.BlockSpec((1, RET_HEADS, RET_DK, RET_DV), lambda s: (bw(s), 0, 0, 0))],
        [jax.ShapeDtypeStruct((L, 1024), F32), jax.ShapeDtypeStruct((L, 1024), F32),
         jax.ShapeDtypeStruct((n, RET_HEADS, RET_DK, RET_DV), F32),
         jax.ShapeDtypeStruct((n, RET_HEADS, RET_DK, RET_DV), F32)],
        [pltpu.VMEM((2 * RET_HEADS, LANES, RET_DV), F32)],
        ("arbitrary",), (P, P, P, rope, P, P, P, rope, tabs, s0), comm)


def _ret_finish_fwd(of, ob, P, L):
    tr = _pick(L, (256, 128))

    def body(f_ref, b_ref, g_ref, y_ref):
        for h in range(RET_HEADS):
            sl = slice(h * RET_DV, (h + 1) * RET_DV)
            n, _ = _norm_rows(f_ref[:, sl] + b_ref[:, sl])
            g = g_ref[:, sl].astype(F32)
            y_ref[:, sl] = (n * (g * _sigmoid(g))).astype(y_ref.dtype)

    row = pl.BlockSpec((tr, 1024), lambda i: (i, 0))
    return pl.pallas_call(
        body, name="ret_finish_fwd", grid=(L // tr,),
        in_specs=[row, row, pl.BlockSpec((tr, 1024), lambda i: (i, C_RG // 1024))],
        out_specs=row, out_shape=jax.ShapeDtypeStruct((L, 1024), BF16),
        compiler_params=_cparams(("parallel",)),
    )(of, ob, P)


def _ret_finish_bwd(of, ob, P, dY, L):
    tr = _pick(L, (256, 128))

    def body(f_ref, b_ref, g_ref, dy_ref, do_ref, dg_ref):
        for h in range(RET_HEADS):
            sl = slice(h * RET_DV, (h + 1) * RET_DV)
            n, r = _norm_rows(f_ref[:, sl] + b_ref[:, sl])
            g = g_ref[:, sl].astype(F32)
            sg = _sigmoid(g)
            dy = dy_ref[:, sl].astype(F32)
            dg_ref[:, sl] = (dy * n * (sg * (1.0 + g * (1.0 - sg)))).astype(dg_ref.dtype)
            dn = dy * (g * sg)
            do_ref[:, sl] = (r * (dn - n * jnp.mean(dn * n, axis=-1, keepdims=True))).astype(do_ref.dtype)

    row = pl.BlockSpec((tr, 1024), lambda i: (i, 0))
    return pl.pallas_call(
        body, name="ret_finish_bwd", grid=(L // tr,),
        in_specs=[row, row, pl.BlockSpec((tr, 1024), lambda i: (i, C_RG // 1024)), row],
        out_specs=[row, row],
        out_shape=[jax.ShapeDtypeStruct((L, 1024), BF16), jax.ShapeDtypeStruct((L, 1024), BF16)],
        compiler_params=_cparams(("parallel",)),
    )(of, ob, P, dY)


def _ret_bwd(P, rope, tabs, stf, stb, dO, L, comm=()):
    n = L // CHUNK

    def body(qf, kf, vf, rf, gf, sf, qb, kb, vb, rb, gb, sb, t_ref,
             dqf, dkf, dvf, dqb, dkb, dvb, ds0_ref, dlg_ref, ds):
        s = pl.program_id(0)

        @pl.when(s == 0)
        def _():
            ds[...] = jnp.zeros_like(ds)
            dlg_ref[...] = jnp.zeros_like(dlg_ref)

        units, pairs = [], []
        for d, (q_ref, k_ref, v_ref, r_ref, g_ref, s_ref, dq_ref, dk_ref, dv_ref) in enumerate(
                ((qf, kf, vf, rf, gf, sf, dqf, dkf, dvf), (qb, kb, vb, rb, gb, sb, dqb, dkb, dvb))):
            cos, sin = r_ref[0], r_ref[1]
            for p in range(RET_HEADS // 2):
                qp = _rope(q_ref[:, p * LANES:(p + 1) * LANES].astype(F32), cos, sin, 32)
                kp = _rope(k_ref[:, p * LANES:(p + 1) * LANES].astype(F32), cos, sin, 32) * K_SCALE
                pair = dict(p=p, cos=cos, sin=sin, dq_ref=dq_ref, dk_ref=dk_ref, us=[])
                pairs.append(pair)
                for a in range(2):
                    h = 2 * p + a
                    r = d * RET_HEADS + h
                    hm = _half_mask(qp.shape, a)
                    zero = jnp.zeros((RET_DK, RET_DV), F32)
                    sp = s_ref[0, h]
                    u = dict(r=r, h=h, dv_ref=dv_ref, qh=jnp.where(hm, qp, 0.0), kh=jnp.where(hm, kp, 0.0),
                             vh=v_ref[:, h * RET_DV:(h + 1) * RET_DV], gh=g_ref[:, h * RET_DV:(h + 1) * RET_DV],
                             sp=jnp.concatenate([sp, zero] if a == 0 else [zero, sp], axis=0),
                             dsn=ds[r])
                    u["qhb"], u["khb"] = u["qh"].astype(BF16), u["kh"].astype(BF16)
                    units.append(u)
                    pair["us"].append(u)
        for u in units:
            u["am"] = _dot(u["qhb"], u["khb"], NT)
            u["dar"] = _dot(u["gh"], u["vh"], NT)
            u["xq"] = _dot(u["gh"], u["sp"].astype(BF16), NT)
            u["yk"] = _dot(u["vh"], u["dsn"].astype(BF16), NT)
        for u in units:
            r = u["r"]
            dm = t_ref[r, 0]
            u["da"] = (u["dar"] * dm).astype(BF16)
            u["amd"] = (u["am"] * dm).astype(BF16)
            part = (jnp.sum(u["am"] * u["dar"] * t_ref[r, 1]) + jnp.sum(u["qh"] * t_ref[r, 3] * u["xq"])
                    + jnp.sum(u["kh"] * t_ref[r, 5] * u["yk"])
                    + float(CHUNK) * jnp.sum(t_ref[r, 6] * u["dsn"] * u["sp"]))
            dlg_ref[r:r + 1, :] += jnp.zeros((1, LANES), F32) + part
        for u in units:
            r, h = u["r"], u["h"]
            u["dq"] = _dot(u["da"], u["khb"], NN) + u["xq"] * t_ref[r, 2]
            u["dk"] = _dot(u["da"], u["qhb"], TN) + u["yk"] * t_ref[r, 4]
            u["dv_ref"][:, h * RET_DV:(h + 1) * RET_DV] = (
                _dot(u["amd"], u["gh"], TN) + _dot((u["kh"] * t_ref[r, 4]).astype(BF16), u["dsn"].astype(BF16), NN))
            ds[r] = t_ref[r, 6] * u["dsn"] + _dot((u["qh"] * t_ref[r, 2]).astype(BF16), u["gh"], TN)
        for pair in pairs:
            sl = slice(pair["p"] * LANES, (pair["p"] + 1) * LANES)
            u0, u1 = pair["us"]
            pair["dq_ref"][:, sl] = _rope_t(u0["dq"] + u1["dq"], pair["cos"], pair["sin"], 32)
            pair["dk_ref"][:, sl] = _rope_t((u0["dk"] + u1["dk"]) * K_SCALE, pair["cos"], pair["sin"], 32)

        @pl.when(s == n - 1)
        def _():
            ds0_ref[...] = ds[...]

    fw = lambda s: n - 1 - s
    bw = lambda s: s

    def specs(cm):
        return [pl.BlockSpec((CHUNK, 512), lambda s: (cm(s), C_RQ // 512)),
                pl.BlockSpec((CHUNK, 512), lambda s: (cm(s), C_RK // 512)),
                pl.BlockSpec((CHUNK, 1024), lambda s: (cm(s), C_RV // 1024)),
                pl.BlockSpec((2, CHUNK, LANES), lambda s: (0, cm(s), 0)),
                pl.BlockSpec((CHUNK, 1024), lambda s: (cm(s), 0)),
                pl.BlockSpec((1, RET_HEADS, RET_DK, RET_DV), lambda s: (cm(s), 0, 0, 0))]

    def ospecs(cm):
        return [pl.BlockSpec((CHUNK, 512), lambda s: (cm(s), 0)), pl.BlockSpec((CHUNK, 512), lambda s: (cm(s), 0)),
                pl.BlockSpec((CHUNK, 1024), lambda s: (cm(s), 0))]

    oshape = [jax.ShapeDtypeStruct((L, 512), F32), jax.ShapeDtypeStruct((L, 512), F32),
              jax.ShapeDtypeStruct((L, 1024), F32)]
    full = lambda shp: pl.BlockSpec(shp, lambda s: (0,) * len(shp))
    return _call(
        body, "ret_bwd", (n,),
        specs(fw) + specs(bw) + [full((2 * RET_HEADS, N_TAB, CHUNK, CHUNK))],
        ospecs(fw) + ospecs(bw) + [full((2 * RET_HEADS, LANES, RET_DV)), full((2 * RET_HEADS, LANES))],
        oshape + oshape + [jax.ShapeDtypeStruct((2 * RET_HEADS, LANES, RET_DV), F32),
                           jax.ShapeDtypeStruct((2 * RET_HEADS, LANES), F32)],
        [pltpu.VMEM((2 * RET_HEADS, LANES, RET_DV), F32)],
        ("arbitrary",), (P, P, P, rope, dO, stf, P, P, P, rope, dO, stb, tabs), comm)


def _ret_ctx_bwd(P, ctab, ds0, dlg, rdb, L, Lc):
    cb = L // Lc

    def body(k_ref, v_ref, c_ref, ds_ref, dlg_ref, rd_ref, dk_ref, dv_ref, drd_ref):
        for p in range(RET_HEADS // 2):
            kp = k_ref[:, p * LANES:(p + 1) * LANES].astype(F32) * K_SCALE
            dkp = jnp.zeros((Lc, LANES), F32)
            for a in range(2):
                h = 2 * p + a
                kh = jnp.where(_half_mask(kp.shape, a), kp, 0.0)
                vh = v_ref[:, h * RET_DV:(h + 1) * RET_DV]
                dvh = jnp.zeros((Lc, RET_DV), F32)
                for d in range(2):
                    r = d * RET_HEADS + h
                    dsb = ds_ref[r].astype(BF16)
                    cw, cwc = c_ref[r, 0], c_ref[r, 1]
                    y = _dot(vh, dsb, NT)
                    dkp += y * cw
                    dvh += _dot((kh * cw).astype(BF16), dsb, NN)
                    lg = -jnp.exp(rd_ref[r])
                    drd_ref[r:r + 1, :] = (dlg_ref[r:r + 1, :] + jnp.sum(kh * cwc * y)) * lg
                dv_ref[:, h * RET_DV:(h + 1) * RET_DV] = dvh
            dk_ref[:, p * LANES:(p + 1) * LANES] = dkp * K_SCALE

    full = lambda shp: pl.BlockSpec(shp, lambda i: (0,) * len(shp))
    return pl.pallas_call(
        body, name="ret_ctx_bwd", grid=(1,),
        in_specs=[pl.BlockSpec((Lc, 512), lambda i: (cb, C_RK // 512)),
                  pl.BlockSpec((Lc, 1024), lambda i: (cb, C_RV // 1024)),
                  full((2 * RET_HEADS, 2, Lc, LANES)), full((2 * RET_HEADS, LANES, RET_DV)),
                  full((2 * RET_HEADS, LANES)), full((2 * RET_HEADS, 1, LANES))],
        out_specs=[full((Lc, 512)), full((Lc, 1024)), full((2 * RET_HEADS, LANES))],
        out_shape=[jax.ShapeDtypeStruct((Lc, 512), F32), jax.ShapeDtypeStruct((Lc, 1024), F32),
                   jax.ShapeDtypeStruct((2 * RET_HEADS, LANES), F32)],
        compiler_params=_cparams(("arbitrary",)),
    )(P, P, ctab, ds0, dlg, rdb)


BLK = 128
N_LOC = 3 * BLK


def _att_inputs(P, rope, L, Lc):
    n = L // BLK
    cb = L // Lc
    prev = lambda i: jnp.maximum(i - 1, 0)
    nxt = lambda i: jnp.minimum(i + 1, n - 1)
    specs = [pl.BlockSpec((BLK, 1024), lambda i: (i, C_AQ // 1024))]
    args = [P]
    for col in (C_AK // 256, C_AV // 256):
        for rm in (prev, lambda i: i, nxt):
            specs.append(pl.BlockSpec((BLK, 256), functools.partial(lambda i, rm, col: (rm(i), col), rm=rm, col=col)))
            args.append(P)
        specs.append(pl.BlockSpec((Lc, 256), functools.partial(lambda i, col: (cb, col), col=col)))
        args.append(P)
    for rm in (prev, lambda i: i, nxt):
        specs.append(pl.BlockSpec((2, BLK, LANES), functools.partial(lambda i, rm: (0, rm(i), 0), rm=rm)))
        args.append(rope)
    return specs, args


def _att_prep(i, n, refs, Lc):
    q_ref, kp_ref, kc_ref, kn_ref, kx_ref, vp_ref, vc_ref, vn_ref, vx_ref, rp_ref, rc_ref, rn_ref = refs
    cos = jnp.concatenate([rp_ref[0], rc_ref[0], rn_ref[0]], axis=0)
    sin = jnp.concatenate([rp_ref[1], rc_ref[1], rn_ref[1]], axis=0)
    kd, vd = [], []
    for t in range(ATT_KV // 2):
        sl = slice(t * LANES, (t + 1) * LANES)
        kl = jnp.concatenate([kp_ref[:, sl], kc_ref[:, sl], kn_ref[:, sl]], axis=0).astype(F32)
        kl = _rope(kl, cos, sin, 16)
        ka = jnp.concatenate([kl, kx_ref[:, sl].astype(F32)], axis=0)
        va = jnp.concatenate([vp_ref[:, sl], vc_ref[:, sl], vn_ref[:, sl], vx_ref[:, sl]], axis=0).astype(F32)
        kr, vr = pltpu.roll(ka, 64, 1), pltpu.roll(va, 64, 1)
        for b in range(2):
            hm = _half_mask(ka.shape, b)
            kd.append(jnp.where(hm, ka, kr).astype(BF16))
            vd.append(jnp.where(hm, va, vr).astype(BF16))
    nk = N_LOC + Lc
    rr = lax.broadcasted_iota(jnp.int32, (BLK, nk), 0)
    ss = lax.broadcasted_iota(jnp.int32, (BLK, nk), 1)
    lo = jnp.where(i == 0, BLK, 0)
    hi = jnp.where(i == n - 1, 2 * BLK, N_LOC)
    valid = (ss >= N_LOC) | ((ss >= rr) & (ss <= rr + 2 * BLK) & (ss >= lo) & (ss < hi))
    bias = jnp.where(valid, 0.0, NEG)
    return kd, vd, jnp.concatenate([bias] * 4, axis=0), rc_ref[0], rc_ref[1]


LOG2E = 1.4426950408889634
LN2 = 0.6931471805599453
Q_SCALE = A_SCALE * LOG2E


def _stack4(ref, g, f=None):
    parts = []
    for jp in range(2):
        t = ref[:, (2 * g + jp) * LANES:(2 * g + jp + 1) * LANES].astype(F32)
        if f is not None:
            t = f(t)
        for a in range(2):
            parts.append(jnp.where(_half_mask(t.shape, a), t, 0.0))
    return jnp.concatenate(parts, axis=0)


def _unstack4(x4, jp):
    r0 = 2 * jp * BLK
    lo = x4[r0:r0 + BLK]
    hi = x4[r0 + BLK:r0 + 2 * BLK]
    return jnp.where(_half_mask(lo.shape, 0), lo, hi)


def _softmax_parts(s, bias4, sink_ref, g):
    sink_col = LOG2E * jnp.concatenate(
        [jnp.zeros((BLK, 1), F32) + sink_ref[4 * g + r:4 * g + r + 1, 0:1] for r in range(4)], axis=0)
    s = s + bias4
    m = jnp.maximum(jnp.max(s, axis=-1, keepdims=True), sink_col)
    e = jnp.exp2(s - m)
    es = jnp.exp2(sink_col - m)
    return e, es, jnp.sum(e, axis=-1, keepdims=True) + es


def _att_fwd(P, rope, sinkb, L, Lc, comm=()):
    n = L // BLK
    specs, args = _att_inputs(P, rope, L, Lc)

    def body(*refs):
        sink_ref, o_ref = refs[12], refs[13]
        i = pl.program_id(0)
        kd, vd, bias4, cq, sq = _att_prep(i, n, refs[:12], Lc)
        def raw_scores(g):
            q4 = _stack4(refs[0], g, lambda t: _rope(t, cq, sq, 16) * Q_SCALE).astype(BF16)
            return _dot(q4, kd[g], NT)

        s_next = raw_scores(0)
        for g in range(ATT_KV):
            s = s_next
            if g + 1 < ATT_KV:
                s_next = raw_scores(g + 1)
            e, _, l = _softmax_parts(s, bias4, sink_ref, g)
            o4 = _dot(e.astype(BF16), vd[g], NN) * (1.0 / l)
            for jp in range(2):
                c0 = (2 * g + jp) * LANES
                o_ref[:, c0:c0 + LANES] = _unstack4(o4, jp).astype(o_ref.dtype)

    return _call(
        body, "att_fwd", (n,),
        specs + [pl.BlockSpec((ATT_HEADS, LANES), lambda i: (0, 0))],
        [pl.BlockSpec((BLK, 1024), lambda i: (i, 0))], [jax.ShapeDtypeStruct((L, 1024), BF16)], [],
        ("parallel",), (*args, sinkb), comm)


def _att_bwd(P, rope, sinkb, y_att, dY, L, Lc, comm=()):
    n = L // BLK
    specs, args = _att_inputs(P, rope, L, Lc)
    nk = N_LOC + Lc

    def body(*refs):
        sink_ref, y_ref, dy_ref = refs[12], refs[13], refs[14]
        dq_ref, dkl_ref, dvl_ref, dkx_ref, dvx_ref, dsk_ref = refs[15:21]
        i = pl.program_id(0)

        @pl.when(i == 0)
        def _():
            dkx_ref[...] = jnp.zeros_like(dkx_ref)
            dvx_ref[...] = jnp.zeros_like(dvx_ref)
            dsk_ref[...] = jnp.zeros_like(dsk_ref)

        kd, vd, bias4, cq, sq = _att_prep(i, n, refs[:12], Lc)
        def first_matmuls(g):
            q4 = _stack4(refs[0], g, lambda x: _rope(x, cq, sq, 16) * Q_SCALE).astype(BF16)
            do4 = _stack4(dy_ref, g)
            delta = jnp.sum(do4 * _stack4(y_ref, g), axis=-1, keepdims=True)
            do4b = do4.astype(BF16)
            return q4, do4b, delta, _dot(q4, kd[g], NT), _dot(do4b, vd[g], NT)

        nxt = first_matmuls(0)
        for t in range(ATT_KV // 2):
            dk_halves, dv_halves = [], []
            for b in range(2):
                g = 2 * t + b
                q4, do4b, delta, s, dpm = nxt
                if g + 1 < ATT_KV:
                    nxt = first_matmuls(g + 1)
                e, es, l = _softmax_parts(s, bias4, sink_ref, g)
                inv = 1.0 / l
                p = e * inv
                dsc = (p * (dpm - delta)).astype(BF16)
                dsr = es * inv * delta
                for r in range(4):
                    h = 4 * g + r
                    dsk_ref[h:h + 1, :] += jnp.zeros((1, LANES), F32) - jnp.sum(dsr[r * BLK:(r + 1) * BLK])
                dq4 = _dot(dsc, kd[g], NN) * A_SCALE
                for jp in range(2):
                    c0 = (2 * g + jp) * LANES
                    dq_ref[:, c0:c0 + LANES] = _rope_t(_unstack4(dq4, jp), cq, sq, 16)
                dkd = _dot(q4, dsc, TN) * LN2
                dvd = _dot(do4b, p.astype(BF16), TN)
                dk_halves.append(dkd[:ATT_DH] + dkd[ATT_DH:])
                dv_halves.append(dvd[:ATT_DH] + dvd[ATT_DH:])
            dk_t = jnp.concatenate(dk_halves, axis=0).T
            dv_t = jnp.concatenate(dv_halves, axis=0).T
            sl = slice(t * LANES, (t + 1) * LANES)
            dkl_ref[0, :, sl] = dk_t[:N_LOC]
            dvl_ref[0, :, sl] = dv_t[:N_LOC]
            dkx_ref[:, sl] += dk_t[N_LOC:]
            dvx_ref[:, sl] += dv_t[N_LOC:]

    row = pl.BlockSpec((BLK, 1024), lambda i: (i, 0))
    loc = pl.BlockSpec((1, N_LOC, 256), lambda i: (i, 0, 0))
    cx = pl.BlockSpec((Lc, 256), lambda i: (0, 0))
    return _call(
        body, "att_bwd", (n,),
        specs + [pl.BlockSpec((ATT_HEADS, LANES), lambda i: (0, 0)), row, pl.BlockSpec((BLK, 1024), lambda i: (i, 1))],
        [row, loc, loc, cx, cx, pl.BlockSpec((ATT_HEADS, LANES), lambda i: (0, 0))],
        [jax.ShapeDtypeStruct((L, 1024), F32), jax.ShapeDtypeStruct((n, N_LOC, 256), F32),
         jax.ShapeDtypeStruct((n, N_LOC, 256), F32), jax.ShapeDtypeStruct((Lc, 256), F32),
         jax.ShapeDtypeStruct((Lc, 256), F32), jax.ShapeDtypeStruct((ATT_HEADS, LANES), F32)], [],
        ("arbitrary",), (*args, sinkb, y_att, dY), comm)


def _assemble_dp(L, Lc, dqf, dqb, dkf, dkb, dvf, dvb, drg, daq, dkl, dvl, rope_att, dck, dcv, dkx, dvx):
    n = L // BLK
    nc = Lc // BLK

    def body(dqf_r, dqb_r, dkf_r, dkb_r, dvf_r, dvb_r, drg_r, daq_r, kl0, kl1, kl2, vl0, vl1, vl2, rp_r,
             dck_r, dcv_r, dkx_r, dvx_r, o_ref):
        i = pl.program_id(0)

        @pl.when(i < n)
        def _():
            o_ref[:, C_RQ:C_RK] = (dqf_r[...] + dqb_r[...]).astype(o_ref.dtype)
            o_ref[:, C_RK:C_RV] = (dkf_r[...] + dkb_r[...]).astype(o_ref.dtype)
            o_ref[:, C_RV:C_RG] = (dvf_r[...] + dvb_r[...]).astype(o_ref.dtype)
            o_ref[:, C_RG:C_AQ] = drg_r[...].astype(o_ref.dtype)
            o_ref[:, C_AQ:C_AK] = daq_r[...].astype(o_ref.dtype)
            w0 = jnp.where(i > 0, 1.0, 0.0)
            w2 = jnp.where(i < n - 1, 1.0, 0.0)
            dk = kl0[0, 2 * BLK:3 * BLK, :] * w0 + kl1[0, BLK:2 * BLK, :] + kl2[0, 0:BLK, :] * w2
            dv = vl0[0, 2 * BLK:3 * BLK, :] * w0 + vl1[0, BLK:2 * BLK, :] + vl2[0, 0:BLK, :] * w2
            for t in range(ATT_KV // 2):
                sl = slice(t * LANES, (t + 1) * LANES)
                o_ref[:, C_AK + t * LANES:C_AK + (t + 1) * LANES] = _rope_t(
                    dk[:, sl], rp_r[0], rp_r[1], 16).astype(o_ref.dtype)
            o_ref[:, C_AV:D_PROJ] = dv.astype(o_ref.dtype)

        @pl.when(i >= n)
        def _():
            o_ref[:, C_RQ:C_RK] = jnp.zeros((BLK, C_RK - C_RQ), o_ref.dtype)
            o_ref[:, C_RK:C_RV] = dck_r[...].astype(o_ref.dtype)
            o_ref[:, C_RV:C_RG] = dcv_r[...].astype(o_ref.dtype)
            o_ref[:, C_RG:C_AK] = jnp.zeros((BLK, C_AK - C_RG), o_ref.dtype)
            o_ref[:, C_AK:C_AV] = dkx_r[...].astype(o_ref.dtype)
            o_ref[:, C_AV:D_PROJ] = dvx_r[...].astype(o_ref.dtype)

    xm = lambda i: jnp.minimum(i, n - 1)
    cm = lambda i: jnp.clip(i - n, 0, nc - 1)
    r512 = pl.BlockSpec((BLK, 512), lambda i: (xm(i), 0))
    r1024 = pl.BlockSpec((BLK, 1024), lambda i: (xm(i), 0))
    part = lambda off: pl.BlockSpec((1, N_LOC, 256), lambda i: (jnp.clip(xm(i) + off, 0, n - 1), 0, 0))
    return pl.pallas_call(
        body, name="assemble_dp", grid=(n + nc,),
        in_specs=[r512, r512, r512, r512, r1024, r1024, r1024, r1024,
                  part(-1), part(0), part(1), part(-1), part(0), part(1),
                  pl.BlockSpec((2, BLK, LANES), lambda i: (0, xm(i), 0)),
                  pl.BlockSpec((BLK, 512), lambda i: (cm(i), 0)), pl.BlockSpec((BLK, 1024), lambda i: (cm(i), 0)),
                  pl.BlockSpec((BLK, 256), lambda i: (cm(i), 0)), pl.BlockSpec((BLK, 256), lambda i: (cm(i), 0))],
        out_specs=pl.BlockSpec((BLK, D_PROJ), lambda i: (i, 0)),
        out_shape=jax.ShapeDtypeStruct((L + Lc, D_PROJ), BF16),
        compiler_params=_cparams(("parallel",)),
    )(dqf, dqb, dkf, dkb, dvf, dvb, drg, daq, dkl, dkl, dkl, dvl, dvl, dvl, rope_att, dck, dcv, dkx, dvx)


def _adam_math(w, g, m, v):
    m = ADAM_B1 * m + (1.0 - ADAM_B1) * g
    v = ADAM_B2 * v + (1.0 - ADAM_B2) * (g * g)
    m_hat = m / (1.0 - ADAM_B1 ** ADAM_STEP)
    v_hat = v / (1.0 - ADAM_B2 ** ADAM_STEP)
    delta = -ADAM_LR * (m_hat / (jnp.sqrt(v_hat) + ADAM_EPS) + ADAM_WD * w)
    return delta, m, v


def _adam(name, w, m, v, g=None, parts=None):
    R, C = w.shape
    tr = _pick(R, (256, 128, 64, 32, 16, 8))
    summed = parts is not None
    n_parts = parts.shape[0] if summed else 0

    def body(w_ref, m_ref, v_ref, g_ref, go_ref, d_ref, mo_ref, vo_ref):
        if summed:
            gv = g_ref[0].astype(F32)
            for j in range(1, n_parts):
                gv = gv + g_ref[j].astype(F32)
        else:
            gv = g_ref[...]
        d, mn, vn = _adam_math(w_ref[...], gv, m_ref[...], v_ref[...])
        go_ref[...] = gv
        d_ref[...] = d
        mo_ref[...] = mn
        vo_ref[...] = vn

    row = pl.BlockSpec((tr, C), lambda i: (i, 0))
    gspec = pl.BlockSpec((n_parts, tr, C), lambda i: (0, i, 0)) if summed else row
    return pl.pallas_call(
        body, name=name, grid=(R // tr,),
        in_specs=[row, row, row, gspec], out_specs=[row] * 4,
        out_shape=[jax.ShapeDtypeStruct((R, C), F32)] * 4,
        compiler_params=_cparams(("parallel",)),
    )(w, m, v, parts if summed else g)


def _cols_full(g):
    _, D, C = g.shape
    return jnp.transpose(g, (1, 0, 2)).reshape(D, N_DEV * C)


def _rows_full(g):
    _, R, D = g.shape
    return g.reshape(N_DEV * R, D)


def _cols_slots(g):
    D, N = g.shape
    return jnp.transpose(g.reshape(D, N_DEV, N // N_DEV), (1, 0, 2))


def _rows_slots(g):
    N, D = g.shape
    return g.reshape(N_DEV, N // N_DEV, D)


def _pad_rows(a, rows):
    return jnp.concatenate([a, jnp.zeros((rows - a.shape[0],) + a.shape[1:], a.dtype)], axis=0)


def kernel(x, c, ctx, c_ctx, w_mod, b_mod, norm_mix, norm_ffn, w_in, ret_decay, attn_sink, w_out, w_gate, w_up, w_down, norm_final, loss_target, m_c_ctx, m_w_mod, m_b_mod, m_norm_mix, m_norm_ffn, m_w_in, m_ret_decay, m_attn_sink, m_w_out, m_w_gate, m_w_up, m_w_down, m_norm_final, v_c_ctx, v_w_mod, v_b_mod, v_norm_mix, v_norm_ffn, v_w_in, v_ret_decay, v_attn_sink, v_w_out, v_w_gate, v_w_up, v_w_down, v_norm_final):
    L, D = x.shape[1], x.shape[2]
    Lc = ctx.shape[1]
    DF = w_gate.shape[2] * N_DEV
    C6 = w_mod.shape[2]
    me = _my_id()
    xs, cx, tgt = x[0], ctx[0], loss_target[0]

    W_in = _cols_full(_exchange("ag2", w_in[0].astype(BF16), "ag_w_in"))
    ag_out, ag_gate = ("ag2", w_out[0].astype(BF16)), ("ag2", w_gate[0].astype(BF16))
    ag_up, ag_down = ("ag2", w_up[0].astype(BF16)), ("ag2", w_down[0].astype(BF16))

    cs = _allgather(c, "ag_c")[:, 0, :]
    s_in = _pad_rows(jnp.concatenate([cs, c_ctx[None, :]], axis=0), 16)
    b_l = lax.dynamic_slice_in_dim(b_mod, me * C6, C6, axis=1)
    mod_parts = _allgather(_mod_fwd(s_in, w_mod[0], b_l), "ag_mod")
    mod = _pad_rows(lax.dynamic_index_in_dim(mod_parts, me, axis=1, keepdims=False).reshape(6, D), 8)
    modc = _pad_rows(mod_parts[:, N_DEV, :].reshape(6, D), 8)
    mix_mod, ffn_mod = mod, jnp.roll(mod, -3, axis=0)
    gt_m, gt_f = mod[2:3], mod[5:6]

    rope_ret, rope_att = _rope_tables(L)
    rdb = jnp.broadcast_to(ret_decay[0].reshape(2 * RET_HEADS, 1, 1), (2 * RET_HEADS, 1, LANES))
    sinkb = jnp.broadcast_to(attn_sink[0].reshape(ATT_HEADS, 1), (ATT_HEADS, LANES))

    tm = _pick(L + Lc, (1408, 768, 512, 384, 256, 128))
    tmx = _pick(L, (1024, 512, 256, 128))

    H = _modulate_fwd("mod_mix_fwd", xs, cx, norm_mix, mix_mod, modc)
    ident = lambda a, e: a
    tP, tD, tF = _pick(D_PROJ, (1152, 768, 512)), _pick(D, (2048, 1024, 512)), _pick(DF, (512, 256, 128))
    (P,), (g_gate,) = _matmul("mm_in", [(H, W_in, 0)], 1, L + Lc, D_PROJ, D, "nn", (tm, tP, D), [], [BF16], ident,
                              comm=[ag_gate])
    W_gate = _cols_full(g_gate)
    tabs, ctab = _ret_tables(rdb, Lc)
    s0 = _ret_ctx_state(P, ctab, L, Lc)
    (o_f, o_b, st_f, st_b), (g_out,) = _ret_fwd(P, rope_ret, tabs, s0, L, comm=[ag_out])
    W_out = _rows_full(g_out)
    y_ret = _ret_finish_fwd(o_f, o_b, P, L)
    (y_att,), (g_up,) = _att_fwd(P, rope_att, sinkb, L, Lc, comm=[ag_up])
    W_up = _cols_full(g_up)
    Y = jnp.concatenate([y_ret, y_att], axis=1)
    KO = Y.shape[1]
    f_mix = _matmul("mm_out", [(Y, W_out, 0)], 1, L, D, KO, "nn", (tmx, tD, KO), [], [BF16], ident)[0]

    x1, H2 = _residual_modulate_fwd("mod_ffn_fwd", xs, f_mix, gt_m, norm_ffn, ffn_mod)

    def swiglu_epi(a, e):
        return [a[0], a[1], a[0] * _sigmoid(a[0]) * a[1]]

    tm2 = tmx
    (ga, up, hmid), (g_down,) = _matmul("mm_gate_up", [(H2, W_gate, 0), (H2, W_up, 1)], 2, L, DF, D, "nn",
                                        (tm2, tF, D), [], [BF16, BF16, BF16], swiglu_epi, comm=[ag_down])
    W_down = _rows_full(g_down)
    f_ffn = _matmul("mm_down", [(hmid, W_down, 0)], 1, L, D, DF, "nn", (tm2, tD, tF), [], [BF16], ident)[0]

    dx2, dFf, sums_l = _loss_head(x1, tgt, norm_final.reshape(1, D), f_ffn, gt_f)

    def dswiglu_epi(a, e):
        av, uv = e[0].astype(F32), e[1].astype(F32)
        sg = _sigmoid(av)
        return [a[0] * uv * (sg * (1.0 + av * (1.0 - sg))), a[0] * (av * sg)]

    dga, dup = _matmul("mm_d_down", [(dFf, W_down, 0)], 1, L, DF, D, "nt", (tm2, tF, D),
                       [(ga, "mn"), (up, "mn")], [BF16, BF16], dswiglu_epi)
    tkt = _pick(L, (512, 256, 128))
    dW_down = _matmul("mm_gw_down", [(hmid, dFf, 0)], 1, DF, D, L, "tn",
                      (_pick(DF, (1408, 512, 256, 128)), tD, tkt), [], [BF16], ident)[0]
    (dH2,), (p_down,) = _matmul("mm_d_gate_up", [(dga, W_gate, 0), (dup, W_up, 0)], 1, L, D, DF, "nt",
                                (tm2, tD, tF), [], [BF16], ident,
                                comm=[("a2a", _rows_slots(dW_down))])
    dW_gate, dW_up = _matmul("mm_gw_gate_up", [(H2, dga, 0), (H2, dup, 1)], 2, D, DF, L, "tn",
                             (_pick(D, (1024, 512)), _pick(DF, (1408, 512, 256, 128)), tkt), [], [BF16, BF16], ident)
    dx1, dFm, sums_f = _modulate_bwd("mod_ffn_bwd", x1, None, dH2, norm_ffn, ffn_mod, None, dx2, f_mix, gt_m)

    tO = _pick(KO, (2048, 1024, 512))
    dY = _matmul("mm_d_out", [(dFm, W_out, 0)], 1, L, KO, D, "nt", (tmx, tO, D), [], [BF16], ident)[0]
    dW_out = _matmul("mm_gw_out", [(Y, dFm, 0)], 1, KO, D, L, "tn", (_pick(KO, (1024, 512)), tD, tkt), [], [BF16],
                     ident)[0]
    dO, drg = _ret_finish_bwd(o_f, o_b, P, dY, L)
    (dqf, dkf, dvf, dqb, dkb, dvb, ds0, dlg), (p_gate,) = _ret_bwd(
        P, rope_ret, tabs, st_f, st_b, dO, L, comm=[("a2a", _cols_slots(dW_gate))])
    dck, dcv, d_rd = _ret_ctx_bwd(P, ctab, ds0, dlg, rdb, L, Lc)
    (daq, dkl, dvl, dkx, dvx, d_sink), (p_up,) = _att_bwd(
        P, rope_att, sinkb, y_att, dY, L, Lc, comm=[("a2a", _cols_slots(dW_up))])
    dP = _assemble_dp(L, Lc, dqf, dqb, dkf, dkb, dvf, dvb, drg, daq, dkl, dvl, rope_att, dck, dcv, dkx, dvx)
    tkc = _pick(L + Lc, (768, 384, 256, 128))
    (dW_in,), (p_out,) = _matmul("mm_gw_in", [(H, dP, 0)], 1, D, D_PROJ, L + Lc, "tn",
                                 (tD, tP, tkc), [], [BF16], ident,
                                 comm=[("a2a", _rows_slots(dW_out))])
    (dH,), (p_in,) = _matmul("mm_d_in", [(dP, W_in, 0)], 1, L + Lc, D, D_PROJ, "nt",
                             (tm, tD, _pick(D_PROJ, (512, 256))), [], [BF16], ident,
                             comm=[("a2a", _cols_slots(dW_in))])
    grad_x, sums_m = _modulate_bwd("mod_mix_bwd", xs, cx, dH, norm_mix, mix_mod, modc, dx1, None, None)

    zero = jnp.zeros((1, D), F32)
    dmod = jnp.concatenate([sums_m[0:1], sums_m[1:2], sums_f[6:7], sums_f[0:1], sums_f[1:2], sums_l[2:3]], axis=1)
    dmodc = jnp.concatenate([sums_m[3:4], sums_m[4:5], zero, zero, zero, zero], axis=1)
    dm_all = _allgather(jnp.concatenate([dmod, dmodc], axis=0), "ag_dmod")
    dm_cols = lax.dynamic_slice_in_dim(dm_all, me * C6, C6, axis=2)
    dm_in = jnp.concatenate([dm_cols[:, 0, :], dm_cols[:, 1, :]], axis=0)
    s_bwd = jnp.concatenate([cs, jnp.broadcast_to(c_ctx[None, :], (N_DEV, D))], axis=0)
    g_w_mod, dsil = _mod_bwd(s_bwd, dm_in, w_mod[0])

    lane_pad = lambda a: _pad_rows(a.reshape(-1, 1), LANES).reshape(1, LANES)
    pack = jnp.concatenate([dsil[0:1], sums_m[2:3], sums_f[2:3], sums_l[1:2],
                            lane_pad(d_rd[:, 0]), lane_pad(d_sink[:, 0]), sums_l[3:4, 0:LANES]], axis=1)
    packs = _allgather(pack, "ag_small")
    zl = jnp.zeros((1, LANES), F32)

    def pack_w(a_c, a_nm, a_nf, a_fin, a_rd, a_sk):
        return jnp.concatenate([a_c.reshape(1, D), a_nm, a_nf, a_fin.reshape(1, D), lane_pad(a_rd.reshape(-1)),
                                lane_pad(a_sk.reshape(-1)), zl], axis=1)

    sg, sd, sm, sv = _adam("adam_small", pack_w(c_ctx, norm_mix, norm_ffn, norm_final, ret_decay, attn_sink),
                           pack_w(m_c_ctx, m_norm_mix, m_norm_ffn, m_norm_final, m_ret_decay, m_attn_sink),
                           pack_w(v_c_ctx, v_norm_mix, v_norm_ffn, v_norm_final, v_ret_decay, v_attn_sink),
                           parts=packs)
    loss = sg[0, 4 * D + 2 * LANES]

    def unpack(a):
        return (a[0, 0:D], a[:, D:2 * D], a[:, 2 * D:3 * D], a[0, 3 * D:4 * D],
                a[0, 4 * D:4 * D + 2 * RET_HEADS].reshape(1, 2, RET_HEADS),
                a[:, 4 * D + LANES:4 * D + LANES + ATT_HEADS])

    bg, bd, bm, bv = _adam("adam_b_mod", b_mod, m_b_mod, v_b_mod, parts=dm_all.reshape(2 * N_DEV, 1, 6 * D))
    wg, wd, wm, wv = _adam("adam_w_mod", w_mod[0], m_w_mod[0], v_w_mod[0], g=g_w_mod)

    big = {}
    for nm, w, m, v, parts in (
            ("w_in", w_in, m_w_in, v_w_in, p_in), ("w_out", w_out, m_w_out, v_w_out, p_out),
            ("w_gate", w_gate, m_w_gate, v_w_gate, p_gate), ("w_up", w_up, m_w_up, v_w_up, p_up),
            ("w_down", w_down, m_w_down, v_w_down, p_down)):
        big[nm] = [a[None] for a in _adam("adam_" + nm, w[0], m[0], v[0], parts=parts)]

    g_s, d_s, m_s, v_s = unpack(sg), unpack(sd), unpack(sm), unpack(sv)

    def leaves(k, small, bmod, wmod):
        return (small[0], wmod[None], bmod, small[1], small[2], big["w_in"][k], small[4], small[5],
                big["w_out"][k], big["w_gate"][k], big["w_up"][k], big["w_down"][k], small[3])

    return (loss, grad_x[None], *leaves(0, g_s, bg, wg), *leaves(1, d_s, bd, wd),
            *leaves(2, m_s, bm, wm), *leaves(3, v_s, bv, wv))
```

```python
import functools

import jax
import jax.numpy as jnp
from jax import lax
from jax.experimental import pallas as pl
from jax.experimental.pallas import tpu as pltpu

F32 = jnp.float32
BF16 = jnp.bfloat16

N_DEV = 8
LANES = 128
RET_HEADS = 8
RET_DK = 64
RET_DV = 128
CHUNK = 128
ATT_HEADS = 16
ATT_KV = 4
ATT_DH = 64
GRID_W = 64
ROPE_BASE = 10000.0
EPS = 1e-6
NEG = -1e30
C_RQ, C_RK, C_RV, C_RG, C_AQ, C_AK, C_AV, D_PROJ = 0, 512, 1024, 2048, 3072, 4096, 4352, 4608
K_SCALE = RET_DK ** -0.5
A_SCALE = ATT_DH ** -0.5

ADAM_LR, ADAM_B1, ADAM_B2, ADAM_EPS, ADAM_WD, ADAM_STEP = 0.001, 0.9, 0.999, 1e-08, 0.01, 10

VMEM_BIG = 52 * 1024 * 1024

NN = (((1,), (0,)), ((), ()))
NT = (((1,), (1,)), ((), ()))
TN = (((0,), (0,)), ((), ()))


def _dot(a, b, dims):
    return lax.dot_general(a, b, dims, preferred_element_type=F32)


def _cparams(sem, vmem=VMEM_BIG):
    return pltpu.CompilerParams(dimension_semantics=sem, vmem_limit_bytes=vmem)


def _pick(dim, prefs):
    for p in prefs:
        if dim % p == 0:
            return p
    return dim


def _my_id():
    return lax.axis_index("x") * 4 + lax.axis_index("y") * 2 + lax.axis_index("c")


def _sigmoid(x):
    return 0.5 * jnp.tanh(0.5 * x) + 0.5


def _peers():
    mx, my, mc = lax.axis_index("x"), lax.axis_index("y"), lax.axis_index("c")
    out = []
    for k in range(1, N_DEV):
        kx, ky, kc = (k >> 2) & 1, (k >> 1) & 1, k & 1
        px = 1 - mx if kx else mx
        py = 1 - my if ky else my
        pc = 1 - mc if kc else mc
        out.append(((px, py, pc), px * 4 + py * 2 + pc))
    return out


def _exchange_copies(kind, x_ref, o_ref, ssem, rsem, lsem):
    me = _my_id()
    loc = pltpu.make_async_copy(x_ref if kind == "ag" else x_ref.at[me], o_ref.at[me], lsem)
    cps = []
    for k, (peer, pid) in enumerate(_peers()):
        cps.append(pltpu.make_async_remote_copy(
            src_ref=x_ref if kind == "ag" else x_ref.at[pid], dst_ref=o_ref.at[me],
            send_sem=ssem.at[k], recv_sem=rsem.at[k], device_id=peer, device_id_type=pl.DeviceIdType.MESH))
    return loc, cps


def _two_level_copies(x_ref, o_ref, ssem, rsem, lsem):
    mx, my, mc = lax.axis_index("x"), lax.axis_index("y"), lax.axis_index("c")
    me = mx * 4 + my * 2 + mc
    sibling = (mx, my, 1 - mc)
    chips = [(1 - mx, my), (mx, 1 - my), (1 - mx, 1 - my)]

    def copy(k, slot, to, src=None):
        return pltpu.make_async_remote_copy(
            src_ref=o_ref.at[slot] if src is None else src, dst_ref=o_ref.at[slot],
            send_sem=ssem.at[k], recv_sem=rsem.at[k], device_id=to, device_id_type=pl.DeviceIdType.MESH)

    loc = pltpu.make_async_copy(x_ref, o_ref.at[me], lsem)
    first = [copy(0, me, sibling, src=x_ref)]
    first += [copy(1 + j, me, (cx, cy, mc), src=x_ref) for j, (cx, cy) in enumerate(chips)]
    passed = [copy(4 + j, cx * 4 + cy * 2 + mc, sibling) for j, (cx, cy) in enumerate(chips)]
    return loc, first, passed


def _exchange_start(kind, x_ref, o_ref, ssem, rsem, lsem):
    if kind == "ag2":
        loc, first, _ = _two_level_copies(x_ref, o_ref, ssem, rsem, lsem)
        cps = first
    else:
        loc, cps = _exchange_copies(kind, x_ref, o_ref, ssem, rsem, lsem)
    loc.start()
    for cp in cps:
        cp.start()


def _exchange_wait(kind, x_ref, o_ref, ssem, rsem, lsem):
    if kind == "ag2":
        loc, first, passed = _two_level_copies(x_ref, o_ref, ssem, rsem, lsem)
        for j in range(3):
            first[1 + j].wait_recv()
            passed[j].start()
        first[0].wait_recv()
        for cp in passed:
            cp.wait_recv()
        cps = first + passed
    else:
        loc, cps = _exchange_copies(kind, x_ref, o_ref, ssem, rsem, lsem)
        for cp in cps:
            cp.wait_recv()
    for cp in cps:
        cp.wait_send()
    loc.wait()


_EXCHANGE_SEMS = [pltpu.SemaphoreType.DMA((N_DEV - 1,)), pltpu.SemaphoreType.DMA((N_DEV - 1,)),
                  pltpu.SemaphoreType.DMA(())]


def _exchange_shape(kind, x):
    return jax.ShapeDtypeStruct(x.shape if kind == "a2a" else (N_DEV,) + x.shape, x.dtype)


def _exchange(kind, x, name):
    def body(x_ref, o_ref, ssem, rsem, lsem):
        _exchange_start(kind, x_ref, o_ref, ssem, rsem, lsem)
        _exchange_wait(kind, x_ref, o_ref, ssem, rsem, lsem)

    return pl.pallas_call(
        body, name=name, out_shape=_exchange_shape(kind, x),
        in_specs=[pl.BlockSpec(memory_space=pl.ANY)], out_specs=pl.BlockSpec(memory_space=pl.ANY),
        scratch_shapes=list(_EXCHANGE_SEMS),
    )(x)


def _allgather(x, name):
    return _exchange("ag", x, name)


def _call(body, name, grid, in_specs, out_specs, out_shape, scratch_shapes, sem, args, comm=()):
    in_specs, out_specs, out_shape = list(in_specs), list(out_specs), list(out_shape)
    scratch_shapes = list(scratch_shapes)
    if not comm:
        outs = pl.pallas_call(body, name=name, grid=grid, in_specs=in_specs, out_specs=out_specs, out_shape=out_shape,
                              scratch_shapes=scratch_shapes, compiler_params=_cparams(sem))(*args)
        return list(outs), []
    n_in, n_out, n_scr, n_c = len(in_specs), len(out_specs), len(scratch_shapes), len(comm)
    hbm = pl.BlockSpec(memory_space=pl.ANY)

    def wrapped(*refs):
        ins, cins = refs[:n_in], refs[n_in:n_in + n_c]
        outs = refs[n_in + n_c:n_in + n_c + n_out]
        couts = refs[n_in + n_c + n_out:n_in + 2 * n_c + n_out]
        scr = refs[n_in + 2 * n_c + n_out:n_in + 2 * n_c + n_out + n_scr]
        sems = refs[n_in + 2 * n_c + n_out + n_scr:]
        first = pl.program_id(0) == 0
        last = pl.program_id(0) == grid[0] - 1
        for ax in range(1, len(grid)):
            first = first & (pl.program_id(ax) == 0)
            last = last & (pl.program_id(ax) == grid[ax] - 1)

        @pl.when(first)
        def _():
            for c, (kind, _) in enumerate(comm):
                _exchange_start(kind, cins[c], couts[c], *sems[3 * c:3 * c + 3])

        body(*ins, *outs, *scr)

        @pl.when(last)
        def _():
            for c, (kind, _) in enumerate(comm):
                _exchange_wait(kind, cins[c], couts[c], *sems[3 * c:3 * c + 3])

    res = pl.pallas_call(
        wrapped, name=name, grid=grid,
        in_specs=in_specs + [hbm] * n_c, out_specs=out_specs + [hbm] * n_c,
        out_shape=out_shape + [_exchange_shape(kind, arr) for kind, arr in comm],
        scratch_shapes=scratch_shapes + list(_EXCHANGE_SEMS) * n_c,
        compiler_params=_cparams(("arbitrary",) * len(grid)),
    )(*args, *[arr for _, arr in comm])
    return list(res[:n_out]), list(res[n_out:])


def _matmul(name, pairs, n_acc, M, N, K, mode, tiles, extras, out_dtypes, epilogue, j_outer=False, comm=()):
    tm, tn, tk = tiles
    gm, gn, nk = M // tm, N // tn, K // tk
    assert gm * tm == M and gn * tn == N and nk * tk == K, (name, M, N, K, tiles)
    if j_outer:
        grid = (gn, gm, nk)
        ij = lambda g0, g1: (g1, g0)
    else:
        grid = (gm, gn, nk)
        ij = lambda g0, g1: (g0, g1)

    if mode in ("nn", "nt"):
        a_spec = pl.BlockSpec((tm, tk), lambda g0, g1, k: (ij(g0, g1)[0], k))
    else:
        a_spec = pl.BlockSpec((tk, tm), lambda g0, g1, k: (k, ij(g0, g1)[0]))
    if mode == "nt":
        b_spec = pl.BlockSpec((tn, tk), lambda g0, g1, k: (ij(g0, g1)[1], k))
    else:
        b_spec = pl.BlockSpec((tk, tn), lambda g0, g1, k: (k, ij(g0, g1)[1]))
    dims = {"nn": NN, "nt": NT, "tn": TN}[mode]
    mn_spec = pl.BlockSpec((tm, tn), lambda g0, g1, k: ij(g0, g1))
    n_spec = pl.BlockSpec((1, tn), lambda g0, g1, k: (0, ij(g0, g1)[1]))

    in_specs, args = [], []
    for a, b, _ in pairs:
        in_specs += [a_spec, b_spec]
        args += [a, b]
    for arr, kind in extras:
        in_specs.append(mn_spec if kind == "mn" else n_spec)
        args.append(arr)
    n_p, n_e, n_o = len(pairs), len(extras), len(out_dtypes)

    def body(*refs):
        ab = refs[:2 * n_p]
        ex = refs[2 * n_p:2 * n_p + n_e]
        outs = refs[2 * n_p + n_e:2 * n_p + n_e + n_o]
        accs = refs[2 * n_p + n_e + n_o:]
        k = pl.program_id(2)

        def partial_sums():
            sums = [None] * n_acc
            for p, (_, _, ai) in enumerate(pairs):
                d = _dot(ab[2 * p][...], ab[2 * p + 1][...], dims)
                sums[ai] = d if sums[ai] is None else sums[ai] + d
            return sums

        def finish(acc_vals):
            res = epilogue(acc_vals, [e[...] for e in ex])
            for o, r in zip(outs, res):
                o[...] = r.astype(o.dtype)

        if nk == 1:
            finish(partial_sums())
        else:
            @pl.when(k == 0)
            def _():
                for ai, s in enumerate(partial_sums()):
                    accs[ai][...] = s

            @pl.when(k > 0)
            def _():
                for ai, s in enumerate(partial_sums()):
                    accs[ai][...] += s

            @pl.when(k == nk - 1)
            def _():
                finish([a[...] for a in accs])

    outs, couts = _call(
        body, name, grid, in_specs, [mn_spec] * n_o,
        [jax.ShapeDtypeStruct((M, N), dt) for dt in out_dtypes],
        [pltpu.VMEM((tm, tn), F32) for _ in range(n_acc if nk > 1 else 0)],
        ("parallel", "parallel", "arbitrary"), args, comm)
    return (outs, couts) if comm else outs


def _rope_tables(L):
    t = jnp.arange(L, dtype=jnp.int32)
    lane = jnp.arange(LANES, dtype=jnp.int32)
    hl = lane % 64
    f = (hl % 32).astype(F32)
    inv = ROPE_BASE ** (-f / 32.0)
    ang = t.astype(F32)[:, None] * inv[None, :]
    sgn = jnp.where(hl < 32, -1.0, 1.0)[None, :]
    ret = jnp.stack([jnp.cos(ang), jnp.sin(ang) * sgn])
    q = hl % 32
    f2 = (q % 16).astype(F32)
    inv2 = ROPE_BASE ** (-f2 / 16.0)
    pos = jnp.where((hl < 32)[None, :], (t // GRID_W)[:, None], (t % GRID_W)[:, None]).astype(F32)
    ang2 = pos * inv2[None, :]
    sgn2 = jnp.where(q < 16, -1.0, 1.0)[None, :]
    att = jnp.stack([jnp.cos(ang2), jnp.sin(ang2) * sgn2])
    return ret.astype(F32), att.astype(F32)


def _swap(x, sh):
    lane = lax.broadcasted_iota(jnp.int32, x.shape, 1)
    ra = pltpu.roll(x, LANES - sh, 1)
    rb = pltpu.roll(x, sh, 1)
    la = pltpu.roll(lane, LANES - sh, 1)
    partner = jnp.where((lane % (2 * sh)) < sh, lane + sh, lane - sh)
    return jnp.where(la == partner, ra, rb)


def _rope(x, cos, sin, sh):
    return x * cos + _swap(x, sh) * sin


def _rope_t(d, cos, sin, sh):
    return d * cos + _swap(d * sin, sh)


def _half_mask(shape, a):
    lane = lax.broadcasted_iota(jnp.int32, shape, 1)
    return (lane < 64) if a == 0 else (lane >= 64)


def _mod_fwd(s_in, w_l, b_l):
    D, C6 = w_l.shape
    tk = _pick(D, (512, 256, 128))
    nk = D // tk

    def body(s_ref, w_ref, b_ref, o_ref):
        k = pl.program_id(0)
        s = s_ref[...]
        s = s * _sigmoid(s)
        d = jnp.dot(s, w_ref[...], preferred_element_type=F32, precision=lax.Precision.HIGHEST)

        @pl.when(k == 0)
        def _():
            o_ref[...] = d + b_ref[...]

        @pl.when(k > 0)
        def _():
            o_ref[...] += d

    return pl.pallas_call(
        body, name="mod_fwd", grid=(nk,),
        in_specs=[pl.BlockSpec((16, tk), lambda k: (0, k)), pl.BlockSpec((tk, C6), lambda k: (k, 0)),
                  pl.BlockSpec((1, C6), lambda k: (0, 0))],
        out_specs=pl.BlockSpec((16, C6), lambda k: (0, 0)),
        out_shape=jax.ShapeDtypeStruct((16, C6), F32),
        compiler_params=_cparams(("arbitrary",)),
    )(s_in, w_l, b_l)


def _mod_bwd(s_in, dm, w_l):
    D, C6 = w_l.shape
    tk = _pick(D, (512, 256, 128))
    nk = D // tk

    def body(s_ref, dm_ref, w_ref, gw_ref, gc_ref):
        s = s_ref[...]
        sg = _sigmoid(s)
        act = s * sg
        dmv = dm_ref[...]
        gw_ref[...] = lax.dot_general(act, dmv, TN, preferred_element_type=F32, precision=lax.Precision.HIGHEST)
        ds = lax.dot_general(dmv, w_ref[...], NT, preferred_element_type=F32, precision=lax.Precision.HIGHEST)
        dsil = (sg * (1.0 + s * (1.0 - sg)))[8:9, :]
        gc_ref[...] = jnp.zeros((8, tk), F32) + jnp.sum(ds[8:16, :], axis=0, keepdims=True) * dsil

    return pl.pallas_call(
        body, name="mod_bwd", grid=(nk,),
        in_specs=[pl.BlockSpec((16, tk), lambda k: (0, k)), pl.BlockSpec((16, C6), lambda k: (0, 0)),
                  pl.BlockSpec((tk, C6), lambda k: (k, 0))],
        out_specs=[pl.BlockSpec((tk, C6), lambda k: (k, 0)), pl.BlockSpec((8, tk), lambda k: (0, k))],
        out_shape=[jax.ShapeDtypeStruct((D, C6), F32), jax.ShapeDtypeStruct((8, D), F32)],
        compiler_params=_cparams(("parallel",)),
    )(s_in, dm, w_l)


def _norm_rows(x):
    r = lax.rsqrt(jnp.mean(x * x, axis=-1, keepdims=True) + EPS)
    return x * r, r


def _modulate_fwd(name, x, ctx, g, mod, modc, comm=()):
    L, D = x.shape
    tr = ctx.shape[0]
    nx = L // tr

    def body(x_ref, c_ref, g_ref, m_ref, mc_ref, o_ref):
        i = pl.program_id(0)

        def run(src, m):
            n, _ = _norm_rows(src[...])
            o_ref[...] = (n * g_ref[...] * (1.0 + m[1:2, :]) + m[0:1, :]).astype(o_ref.dtype)

        @pl.when(i < nx)
        def _():
            run(x_ref, m_ref)

        @pl.when(i >= nx)
        def _():
            run(c_ref, mc_ref)

    row = pl.BlockSpec((tr, D), lambda i: (jnp.minimum(i, nx - 1), 0))
    vec = pl.BlockSpec((1, D), lambda i: (0, 0))
    mv = pl.BlockSpec((8, D), lambda i: (0, 0))
    return _call(
        body, name, (nx + 1,), [row, pl.BlockSpec((tr, D), lambda i: (0, 0)), vec, mv, mv],
        [pl.BlockSpec((tr, D), lambda i: (i, 0))], [jax.ShapeDtypeStruct((L + tr, D), BF16)], [],
        ("parallel",), (x, ctx, g, mod, modc), comm)


def _residual_modulate_fwd(name, x, fbr, gate, g, mod):
    L, D = x.shape
    tr = _pick(L, (256, 128))

    def body(x_ref, f_ref, gt_ref, g_ref, m_ref, x1_ref, o_ref):
        x1 = x_ref[...] + gt_ref[...] * f_ref[...].astype(F32)
        x1_ref[...] = x1
        n, _ = _norm_rows(x1)
        o_ref[...] = (n * g_ref[...] * (1.0 + m_ref[1:2, :]) + m_ref[0:1, :]).astype(o_ref.dtype)

    row = pl.BlockSpec((tr, D), lambda i: (i, 0))
    vec = pl.BlockSpec((1, D), lambda i: (0, 0))
    return pl.pallas_call(
        body, name=name, grid=(L // tr,),
        in_specs=[row, row, vec, vec, pl.BlockSpec((8, D), lambda i: (0, 0))],
        out_specs=[row, row],
        out_shape=[jax.ShapeDtypeStruct((L, D), F32), jax.ShapeDtypeStruct((L, D), BF16)],
        compiler_params=_cparams(("parallel",)),
    )(x, fbr, gate, g, mod)


def _modulate_bwd(name, x, ctx, dh, g, mod, modc, dres, fbr, gate):
    L, D = x.shape
    tr = ctx.shape[0] if ctx is not None else _pick(L, (256, 128))
    nx = L // tr
    nt = nx + (1 if ctx is not None else 0)
    has_f = fbr is not None

    def body(*refs):
        refs = list(refs)
        x_ref = refs.pop(0)
        c_ref = refs.pop(0) if ctx is not None else None
        dh_ref, g_ref, m_ref = refs.pop(0), refs.pop(0), refs.pop(0)
        mc_ref = refs.pop(0) if ctx is not None else None
        dr_ref = refs.pop(0)
        f_ref = refs.pop(0) if has_f else None
        gt_ref = refs.pop(0) if has_f else None
        dx_ref = refs.pop(0)
        df_ref = refs.pop(0) if has_f else None
        acc_ref = refs.pop(0)
        i = pl.program_id(0)

        @pl.when(i == 0)
        def _():
            acc_ref[...] = jnp.zeros_like(acc_ref)

        def sums(src, m, base, grow):
            n, r = _norm_rows(src[...])
            d = dh_ref[...].astype(F32)
            gg = g_ref[...]
            sc1 = 1.0 + m[1:2, :]
            acc_ref[base:base + 1, :] += jnp.sum(d, axis=0, keepdims=True)
            dn = d * n
            acc_ref[base + 1:base + 2, :] += jnp.sum(dn, axis=0, keepdims=True) * gg
            acc_ref[grow:grow + 1, :] += jnp.sum(dn, axis=0, keepdims=True) * sc1
            dnv = d * (gg * sc1)
            return r * (dnv - n * jnp.mean(dnv * n, axis=-1, keepdims=True))

        def x_rows():
            dx = sums(x_ref, m_ref, 0, 2) + dr_ref[...]
            dx_ref[...] = dx
            if has_f:
                acc_ref[6:7, :] += jnp.sum(dx * f_ref[...].astype(F32), axis=0, keepdims=True)
                df_ref[...] = (dx * gt_ref[...]).astype(df_ref.dtype)

        if ctx is None:
            x_rows()
        else:
            pl.when(i < nx)(x_rows)

            @pl.when(i >= nx)
            def _():
                sums(c_ref, mc_ref, 3, 2)

    row = pl.BlockSpec((tr, D), lambda i: (jnp.minimum(i, nx - 1), 0))
    vec = pl.BlockSpec((1, D), lambda i: (0, 0))
    mv = pl.BlockSpec((8, D), lambda i: (0, 0))
    in_specs, args = [row], [x]
    if ctx is not None:
        in_specs.append(pl.BlockSpec((tr, D), lambda i: (0, 0)))
        args.append(ctx)
    in_specs += [pl.BlockSpec((tr, D), lambda i: (i, 0)), vec, mv]
    args += [dh, g, mod]
    if ctx is not None:
        in_specs.append(mv)
        args.append(modc)
    in_specs.append(row)
    args.append(dres)
    out_specs = [row]
    out_shape = [jax.ShapeDtypeStruct((L, D), F32)]
    if has_f:
        in_specs += [row, vec]
        args += [fbr, gate]
        out_specs.append(row)
        out_shape.append(jax.ShapeDtypeStruct((L, D), BF16))
    out_specs.append(pl.BlockSpec((16, D), lambda i: (0, 0)))
    out_shape.append(jax.ShapeDtypeStruct((16, D), F32))
    return pl.pallas_call(
        body, name=name, grid=(nt,), in_specs=in_specs, out_specs=out_specs, out_shape=out_shape,
        compiler_params=_cparams(("arbitrary",)),
    )(*args)


def _loss_head(x1, tgt, nf, fbr, gate):
    L, D = x1.shape
    tr = _pick(L, (256, 128))

    def body(x_ref, t_ref, w_ref, f_ref, gt_ref, dx_ref, df_ref, acc_ref):
        i = pl.program_id(0)

        @pl.when(i == 0)
        def _():
            acc_ref[...] = jnp.zeros_like(acc_ref)

        n, r = _norm_rows(x_ref[...] + gt_ref[...] * f_ref[...].astype(F32))
        w = w_ref[...]
        e = n * w - t_ref[...]
        acc_ref[0:1, :] += jnp.sum(e * e, axis=0, keepdims=True) * (0.5 / D)
        dout = e * (1.0 / D)
        acc_ref[1:2, :] += jnp.sum(dout * n, axis=0, keepdims=True)
        dn = dout * w
        dx = r * (dn - n * jnp.mean(dn * n, axis=-1, keepdims=True))
        dx_ref[...] = dx
        acc_ref[2:3, :] += jnp.sum(dx * f_ref[...].astype(F32), axis=0, keepdims=True)
        df_ref[...] = (dx * gt_ref[...]).astype(df_ref.dtype)

        @pl.when(i == pl.num_programs(0) - 1)
        def _():
            acc_ref[3:4, :] = jnp.zeros((1, D), F32) + jnp.sum(acc_ref[0:1, :])

    row = pl.BlockSpec((tr, D), lambda i: (i, 0))
    vec = pl.BlockSpec((1, D), lambda i: (0, 0))
    return pl.pallas_call(
        body, name="loss_head", grid=(L // tr,),
        in_specs=[row, row, vec, row, vec],
        out_specs=[row, row, pl.BlockSpec((8, D), lambda i: (0, 0))],
        out_shape=[jax.ShapeDtypeStruct((L, D), F32), jax.ShapeDtypeStruct((L, D), BF16),
                   jax.ShapeDtypeStruct((8, D), F32)],
        compiler_params=_cparams(("arbitrary",)),
    )(x1, tgt, nf, fbr, gate)


N_TAB = 7


def _ret_tables(rdb, Lc):
    def body(rd_ref, t_ref, c_ref):
        d = pl.program_id(0) // RET_HEADS
        fwd = d == 0
        lg = -jnp.exp(rd_ref[0])
        i = lax.broadcasted_iota(jnp.int32, (CHUNK, CHUNK), 0).astype(F32)
        j = lax.broadcasted_iota(jnp.int32, (CHUNK, CHUNK), 1).astype(F32)
        rel = jnp.where(fwd, i - j, j - i)
        mask = (rel > 0.0) | ((rel == 0.0) & fwd)
        dm = jnp.where(mask, jnp.exp(lg * jnp.maximum(rel, 0.0)), 0.0)
        t_ref[0, 0] = dm
        t_ref[0, 1] = rel * dm
        qc = jnp.where(fwd, i + 1.0, CHUNK - i)
        qw = jnp.exp(lg * qc)
        t_ref[0, 2] = qw
        t_ref[0, 3] = qw * qc
        kc = jnp.where(fwd, CHUNK - 1.0 - i, i)
        kw = jnp.exp(lg * kc)
        t_ref[0, 4] = kw
        t_ref[0, 5] = kw * kc
        t_ref[0, 6] = jnp.exp(lg * float(CHUNK)) + jnp.zeros((CHUNK, CHUNK), F32)
        m = lax.broadcasted_iota(jnp.int32, (Lc, LANES), 0).astype(F32)
        cc = jnp.where(fwd, Lc - 1.0 - m, m)
        cw = jnp.exp(lg * cc)
        c_ref[0, 0] = cw
        c_ref[0, 1] = cw * cc

    return pl.pallas_call(
        body, name="ret_tables", grid=(2 * RET_HEADS,),
        in_specs=[pl.BlockSpec((1, 1, LANES), lambda r: (r, 0, 0))],
        out_specs=[pl.BlockSpec((1, N_TAB, CHUNK, CHUNK), lambda r: (r, 0, 0, 0)),
                   pl.BlockSpec((1, 2, Lc, LANES), lambda r: (r, 0, 0, 0))],
        out_shape=[jax.ShapeDtypeStruct((2 * RET_HEADS, N_TAB, CHUNK, CHUNK), F32),
                   jax.ShapeDtypeStruct((2 * RET_HEADS, 2, Lc, LANES), F32)],
        compiler_params=_cparams(("parallel",)),
    )(rdb)


def _ret_ctx_state(P, ctab, L, Lc):
    cb = L // Lc

    def body(k_ref, v_ref, c_ref, s_ref):
        for p in range(RET_HEADS // 2):
            kp = k_ref[:, p * LANES:(p + 1) * LANES].astype(F32) * K_SCALE
            for a in range(2):
                h = 2 * p + a
                kh = jnp.where(_half_mask(kp.shape, a), kp, 0.0)
                vh = v_ref[:, h * RET_DV:(h + 1) * RET_DV]
                for d in range(2):
                    kw = (kh * c_ref[d * RET_HEADS + h, 0]).astype(BF16)
                    s_ref[d * RET_HEADS + h] = _dot(kw, vh, TN)

    return pl.pallas_call(
        body, name="ret_ctx_state", grid=(1,),
        in_specs=[pl.BlockSpec((Lc, 512), lambda i: (cb, C_RK // 512)),
                  pl.BlockSpec((Lc, 1024), lambda i: (cb, C_RV // 1024)),
                  pl.BlockSpec((2 * RET_HEADS, 2, Lc, LANES), lambda i: (0, 0, 0, 0))],
        out_specs=pl.BlockSpec((2 * RET_HEADS, LANES, RET_DV), lambda i: (0, 0, 0)),
        out_shape=jax.ShapeDtypeStruct((2 * RET_HEADS, LANES, RET_DV), F32),
        compiler_params=_cparams(("arbitrary",)),
    )(P, P, ctab)


def _ret_fwd(P, rope, tabs, s0, L, comm=()):
    n = L // CHUNK

    def body(qf, kf, vf, rf, qb, kb, vb, rb, t_ref, s0_ref, of_ref, ob_ref, stf_ref, stb_ref, st):
        s = pl.program_id(0)

        @pl.when(s == 0)
        def _():
            st[...] = s0_ref[...]

        units = []
        for d, (q_ref, k_ref, v_ref, r_ref, o_ref, so_ref) in enumerate(
                ((qf, kf, vf, rf, of_ref, stf_ref), (qb, kb, vb, rb, ob_ref, stb_ref))):
            cos, sin = r_ref[0], r_ref[1]
            for p in range(RET_HEADS // 2):
                qp = _rope(q_ref[:, p * LANES:(p + 1) * LANES].astype(F32), cos, sin, 32)
                kp = _rope(k_ref[:, p * LANES:(p + 1) * LANES].astype(F32), cos, sin, 32) * K_SCALE
                for a in range(2):
                    h = 2 * p + a
                    hm = _half_mask(qp.shape, a)
                    units.append(dict(r=d * RET_HEADS + h, h=h, a=a, o_ref=o_ref, so_ref=so_ref, v_ref=v_ref,
                                      qh=jnp.where(hm, qp, 0.0), kh=jnp.where(hm, kp, 0.0)))
        for u in units:
            u["sc"] = _dot(u["qh"].astype(BF16), u["kh"].astype(BF16), NT)
        for u in units:
            r, h = u["r"], u["h"]
            sp = st[r]
            u["so_ref"][0, h] = sp[u["a"] * RET_DK:(u["a"] + 1) * RET_DK, :]
            vh = u["v_ref"][:, h * RET_DV:(h + 1) * RET_DV]
            o = _dot((u["sc"] * t_ref[r, 0]).astype(BF16), vh, NN)
            o += _dot((u["qh"] * t_ref[r, 2]).astype(BF16), sp.astype(BF16), NN)
            u["o_ref"][:, h * RET_DV:(h + 1) * RET_DV] = o
        for u in units:
            r, h = u["r"], u["h"]
            vh = u["v_ref"][:, h * RET_DV:(h + 1) * RET_DV]
            st[r] = t_ref[r, 6] * st[r] + _dot((u["kh"] * t_ref[r, 4]).astype(BF16), vh, TN)

    fw = lambda s: s
    bw = lambda s: n - 1 - s

    def specs(cm):
        return [pl.BlockSpec((CHUNK, 512), lambda s: (cm(s), C_RQ // 512)),
                pl.BlockSpec((CHUNK, 512), lambda s: (cm(s), C_RK // 512)),
                pl.BlockSpec((CHUNK, 1024), lambda s: (cm(s), C_RV // 1024)),
                pl.BlockSpec((2, CHUNK, LANES), lambda s: (0, cm(s), 0))]

    full = lambda shp: pl.BlockSpec(shp, lambda s: (0,) * len(shp))
    return _call(
        body, "ret_fwd", (n,),
        specs(fw) + specs(bw) + [full((2 * RET_HEADS, N_TAB, CHUNK, CHUNK)), full((2 * RET_HEADS, LANES, RET_DV))],
        [pl.BlockSpec((CHUNK, 1024), lambda s: (fw(s), 0)),
         pl.BlockSpec((CHUNK, 1024), lambda s: (bw(s), 0)),
         pl.BlockSpec((1, RET_HEADS, RET_DK, RET_DV), lambda s: (fw(s), 0, 0, 0)),
         pl.BlockSpec((1, RET_HEADS, RET_DK, RET_DV), lambda s: (bw(s), 0, 0, 0))],
        [jax.ShapeDtypeStruct((L, 1024), F32), jax.ShapeDtypeStruct((L, 1024), F32),
         jax.ShapeDtypeStruct((n, RET_HEADS, RET_DK, RET_DV), F32),
         jax.ShapeDtypeStruct((n, RET_HEADS, RET_DK, RET_DV), F32)],
        [pltpu.VMEM((2 * RET_HEADS, LANES, RET_DV), F32)],
        ("arbitrary",), (P, P, P, rope, P, P, P, rope, tabs, s0), comm)


def _ret_finish_fwd(of, ob, P, L):
    tr = _pick(L, (256, 128))

    def body(f_ref, b_ref, g_ref, y_ref):
        for h in range(RET_HEADS):
            sl = slice(h * RET_DV, (h + 1) * RET_DV)
            n, _ = _norm_rows(f_ref[:, sl] + b_ref[:, sl])
            g = g_ref[:, sl].astype(F32)
            y_ref[:, sl] = (n * (g * _sigmoid(g))).astype(y_ref.dtype)

    row = pl.BlockSpec((tr, 1024), lambda i: (i, 0))
    return pl.pallas_call(
        body, name="ret_finish_fwd", grid=(L // tr,),
        in_specs=[row, row, pl.BlockSpec((tr, 1024), lambda i: (i, C_RG // 1024))],
        out_specs=row, out_shape=jax.ShapeDtypeStruct((L, 1024), BF16),
        compiler_params=_cparams(("parallel",)),
    )(of, ob, P)


def _ret_finish_bwd(of, ob, P, dY, L):
    tr = _pick(L, (256, 128))

    def body(f_ref, b_ref, g_ref, dy_ref, do_ref, dg_ref):
        for h in range(RET_HEADS):
            sl = slice(h * RET_DV, (h + 1) * RET_DV)
            n, r = _norm_rows(f_ref[:, sl] + b_ref[:, sl])
            g = g_ref[:, sl].astype(F32)
            sg = _sigmoid(g)
            dy = dy_ref[:, sl].astype(F32)
            dg_ref[:, sl] = (dy * n * (sg * (1.0 + g * (1.0 - sg)))).astype(dg_ref.dtype)
            dn = dy * (g * sg)
            do_ref[:, sl] = (r * (dn - n * jnp.mean(dn * n, axis=-1, keepdims=True))).astype(do_ref.dtype)

    row = pl.BlockSpec((tr, 1024), lambda i: (i, 0))
    return pl.pallas_call(
        body, name="ret_finish_bwd", grid=(L // tr,),
        in_specs=[row, row, pl.BlockSpec((tr, 1024), lambda i: (i, C_RG // 1024)), row],
        out_specs=[row, row],
        out_shape=[jax.ShapeDtypeStruct((L, 1024), BF16), jax.ShapeDtypeStruct((L, 1024), BF16)],
        compiler_params=_cparams(("parallel",)),
    )(of, ob, P, dY)


def _ret_bwd(P, rope, tabs, stf, stb, dO, L, comm=()):
    n = L // CHUNK

    def body(qf, kf, vf, rf, gf, sf, qb, kb, vb, rb, gb, sb, t_ref,
             dqf, dkf, dvf, dqb, dkb, dvb, ds0_ref, dlg_ref, ds):
        s = pl.program_id(0)

        @pl.when(s == 0)
        def _():
            ds[...] = jnp.zeros_like(ds)
            dlg_ref[...] = jnp.zeros_like(dlg_ref)

        units, pairs = [], []
        for d, (q_ref, k_ref, v_ref, r_ref, g_ref, s_ref, dq_ref, dk_ref, dv_ref) in enumerate(
                ((qf, kf, vf, rf, gf, sf, dqf, dkf, dvf), (qb, kb, vb, rb, gb, sb, dqb, dkb, dvb))):
            cos, sin = r_ref[0], r_ref[1]
            for p in range(RET_HEADS // 2):
                qp = _rope(q_ref[:, p * LANES:(p + 1) * LANES].astype(F32), cos, sin, 32)
                kp = _rope(k_ref[:, p * LANES:(p + 1) * LANES].astype(F32), cos, sin, 32) * K_SCALE
                pair = dict(p=p, cos=cos, sin=sin, dq_ref=dq_ref, dk_ref=dk_ref, us=[])
                pairs.append(pair)
                for a in range(2):
                    h = 2 * p + a
                    r = d * RET_HEADS + h
                    hm = _half_mask(qp.shape, a)
                    zero = jnp.zeros((RET_DK, RET_DV), F32)
                    sp = s_ref[0, h]
                    u = dict(r=r, h=h, dv_ref=dv_ref, qh=jnp.where(hm, qp, 0.0), kh=jnp.where(hm, kp, 0.0),
                             vh=v_ref[:, h * RET_DV:(h + 1) * RET_DV], gh=g_ref[:, h * RET_DV:(h + 1) * RET_DV],
                             sp=jnp.concatenate([sp, zero] if a == 0 else [zero, sp], axis=0),
                             dsn=ds[r])
                    u["qhb"], u["khb"] = u["qh"].astype(BF16), u["kh"].astype(BF16)
                    units.append(u)
                    pair["us"].append(u)
        for u in units:
            u["am"] = _dot(u["qhb"], u["khb"], NT)
            u["dar"] = _dot(u["gh"], u["vh"], NT)
            u["xq"] = _dot(u["gh"], u["sp"].astype(BF16), NT)
            u["yk"] = _dot(u["vh"], u["dsn"].astype(BF16), NT)
        for u in units:
            r = u["r"]
            dm = t_ref[r, 0]
            u["da"] = (u["dar"] * dm).astype(BF16)
            u["amd"] = (u["am"] * dm).astype(BF16)
            part = (jnp.sum(u["am"] * u["dar"] * t_ref[r, 1]) + jnp.sum(u["qh"] * t_ref[r, 3] * u["xq"])
                    + jnp.sum(u["kh"] * t_ref[r, 5] * u["yk"])
                    + float(CHUNK) * jnp.sum(t_ref[r, 6] * u["dsn"] * u["sp"]))
            dlg_ref[r:r + 1, :] += jnp.zeros((1, LANES), F32) + part
        for u in units:
            r, h = u["r"], u["h"]
            u["dq"] = _dot(u["da"], u["khb"], NN) + u["xq"] * t_ref[r, 2]
            u["dk"] = _dot(u["da"], u["qhb"], TN) + u["yk"] * t_ref[r, 4]
            u["dv_ref"][:, h * RET_DV:(h + 1) * RET_DV] = (
                _dot(u["amd"], u["gh"], TN) + _dot((u["kh"] * t_ref[r, 4]).astype(BF16), u["dsn"].astype(BF16), NN))
            ds[r] = t_ref[r, 6] * u["dsn"] + _dot((u["qh"] * t_ref[r, 2]).astype(BF16), u["gh"], TN)
        for pair in pairs:
            sl = slice(pair["p"] * LANES, (pair["p"] + 1) * LANES)
            u0, u1 = pair["us"]
            pair["dq_ref"][:, sl] = _rope_t(u0["dq"] + u1["dq"], pair["cos"], pair["sin"], 32)
            pair["dk_ref"][:, sl] = _rope_t((u0["dk"] + u1["dk"]) * K_SCALE, pair["cos"], pair["sin"], 32)

        @pl.when(s == n - 1)
        def _():
            ds0_ref[...] = ds[...]

    fw = lambda s: n - 1 - s
    bw = lambda s: s

    def specs(cm):
        return [pl.BlockSpec((CHUNK, 512), lambda s: (cm(s), C_RQ // 512)),
                pl.BlockSpec((CHUNK, 512), lambda s: (cm(s), C_RK // 512)),
                pl.BlockSpec((CHUNK, 1024), lambda s: (cm(s), C_RV // 1024)),
                pl.BlockSpec((2, CHUNK, LANES), lambda s: (0, cm(s), 0)),
                pl.BlockSpec((CHUNK, 1024), lambda s: (cm(s), 0)),
                pl.BlockSpec((1, RET_HEADS, RET_DK, RET_DV), lambda s: (cm(s), 0, 0, 0))]

    def ospecs(cm):
        return [pl.BlockSpec((CHUNK, 512), lambda s: (cm(s), 0)), pl.BlockSpec((CHUNK, 512), lambda s: (cm(s), 0)),
                pl.BlockSpec((CHUNK, 1024), lambda s: (cm(s), 0))]

    oshape = [jax.ShapeDtypeStruct((L, 512), F32), jax.ShapeDtypeStruct((L, 512), F32),
              jax.ShapeDtypeStruct((L, 1024), F32)]
    full = lambda shp: pl.BlockSpec(shp, lambda s: (0,) * len(shp))
    return _call(
        body, "ret_bwd", (n,),
        specs(fw) + specs(bw) + [full((2 * RET_HEADS, N_TAB, CHUNK, CHUNK))],
        ospecs(fw) + ospecs(bw) + [full((2 * RET_HEADS, LANES, RET_DV)), full((2 * RET_HEADS, LANES))],
        oshape + oshape + [jax.ShapeDtypeStruct((2 * RET_HEADS, LANES, RET_DV), F32),
                           jax.ShapeDtypeStruct((2 * RET_HEADS, LANES), F32)],
        [pltpu.VMEM((2 * RET_HEADS, LANES, RET_DV), F32)],
        ("arbitrary",), (P, P, P, rope, dO, stf, P, P, P, rope, dO, stb, tabs), comm)


def _ret_ctx_bwd(P, ctab, ds0, dlg, rdb, L, Lc):
    cb = L // Lc

    def body(k_ref, v_ref, c_ref, ds_ref, dlg_ref, rd_ref, dk_ref, dv_ref, drd_ref):
        for p in range(RET_HEADS // 2):
            kp = k_ref[:, p * LANES:(p + 1) * LANES].astype(F32) * K_SCALE
            dkp = jnp.zeros((Lc, LANES), F32)
            for a in range(2):
                h = 2 * p + a
                kh = jnp.where(_half_mask(kp.shape, a), kp, 0.0)
                vh = v_ref[:, h * RET_DV:(h + 1) * RET_DV]
                dvh = jnp.zeros((Lc, RET_DV), F32)
                for d in range(2):
                    r = d * RET_HEADS + h
                    dsb = ds_ref[r].astype(BF16)
                    cw, cwc = c_ref[r, 0], c_ref[r, 1]
                    y = _dot(vh, dsb, NT)
                    dkp += y * cw
                    dvh += _dot((kh * cw).astype(BF16), dsb, NN)
                    lg = -jnp.exp(rd_ref[r])
                    drd_ref[r:r + 1, :] = (dlg_ref[r:r + 1, :] + jnp.sum(kh * cwc * y)) * lg
                dv_ref[:, h * RET_DV:(h + 1) * RET_DV] = dvh
            dk_ref[:, p * LANES:(p + 1) * LANES] = dkp * K_SCALE

    full = lambda shp: pl.BlockSpec(shp, lambda i: (0,) * len(shp))
    return pl.pallas_call(
        body, name="ret_ctx_bwd", grid=(1,),
        in_specs=[pl.BlockSpec((Lc, 512), lambda i: (cb, C_RK // 512)),
                  pl.BlockSpec((Lc, 1024), lambda i: (cb, C_RV // 1024)),
                  full((2 * RET_HEADS, 2, Lc, LANES)), full((2 * RET_HEADS, LANES, RET_DV)),
                  full((2 * RET_HEADS, LANES)), full((2 * RET_HEADS, 1, LANES))],
        out_specs=[full((Lc, 512)), full((Lc, 1024)), full((2 * RET_HEADS, LANES))],
        out_shape=[jax.ShapeDtypeStruct((Lc, 512), F32), jax.ShapeDtypeStruct((Lc, 1024), F32),
                   jax.ShapeDtypeStruct((2 * RET_HEADS, LANES), F32)],
        compiler_params=_cparams(("arbitrary",)),
    )(P, P, ctab, ds0, dlg, rdb)


BLK = 128
N_LOC = 3 * BLK


def _att_inputs(P, rope, L, Lc):
    n = L // BLK
    cb = L // Lc
    prev = lambda i: jnp.maximum(i - 1, 0)
    nxt = lambda i: jnp.minimum(i + 1, n - 1)
    specs = [pl.BlockSpec((BLK, 1024), lambda i: (i, C_AQ // 1024))]
    args = [P]
    for col in (C_AK // 256, C_AV // 256):
        for rm in (prev, lambda i: i, nxt):
            specs.append(pl.BlockSpec((BLK, 256), functools.partial(lambda i, rm, col: (rm(i), col), rm=rm, col=col)))
            args.append(P)
        specs.append(pl.BlockSpec((Lc, 256), functools.partial(lambda i, col: (cb, col), col=col)))
        args.append(P)
    for rm in (prev, lambda i: i, nxt):
        specs.append(pl.BlockSpec((2, BLK, LANES), functools.partial(lambda i, rm: (0, rm(i), 0), rm=rm)))
        args.append(rope)
    return specs, args


def _att_prep(i, n, refs, Lc):
    q_ref, kp_ref, kc_ref, kn_ref, kx_ref, vp_ref, vc_ref, vn_ref, vx_ref, rp_ref, rc_ref, rn_ref = refs
    cos = jnp.concatenate([rp_ref[0], rc_ref[0], rn_ref[0]], axis=0)
    sin = jnp.concatenate([rp_ref[1], rc_ref[1], rn_ref[1]], axis=0)
    kd, vd = [], []
    for t in range(ATT_KV // 2):
        sl = slice(t * LANES, (t + 1) * LANES)
        kl = jnp.concatenate([kp_ref[:, sl], kc_ref[:, sl], kn_ref[:, sl]], axis=0).astype(F32)
        kl = _rope(kl, cos, sin, 16)
        ka = jnp.concatenate([kl, kx_ref[:, sl].astype(F32)], axis=0)
        va = jnp.concatenate([vp_ref[:, sl], vc_ref[:, sl], vn_ref[:, sl], vx_ref[:, sl]], axis=0).astype(F32)
        kr, vr = pltpu.roll(ka, 64, 1), pltpu.roll(va, 64, 1)
        for b in range(2):
            hm = _half_mask(ka.shape, b)
            kd.append(jnp.where(hm, ka, kr).astype(BF16))
            vd.append(jnp.where(hm, va, vr).astype(BF16))
    nk = N_LOC + Lc
    rr = lax.broadcasted_iota(jnp.int32, (BLK, nk), 0)
    ss = lax.broadcasted_iota(jnp.int32, (BLK, nk), 1)
    lo = jnp.where(i == 0, BLK, 0)
    hi = jnp.where(i == n - 1, 2 * BLK, N_LOC)
    valid = (ss >= N_LOC) | ((ss >= rr) & (ss <= rr + 2 * BLK) & (ss >= lo) & (ss < hi))
    bias = jnp.where(valid, 0.0, NEG)
    return kd, vd, jnp.concatenate([bias] * 4, axis=0), rc_ref[0], rc_ref[1]


LOG2E = 1.4426950408889634
LN2 = 0.6931471805599453
Q_SCALE = A_SCALE * LOG2E


def _stack4(ref, g, f=None):
    parts = []
    for jp in range(2):
        t = ref[:, (2 * g + jp) * LANES:(2 * g + jp + 1) * LANES].astype(F32)
        if f is not None:
            t = f(t)
        for a in range(2):
            parts.append(jnp.where(_half_mask(t.shape, a), t, 0.0))
    return jnp.concatenate(parts, axis=0)


def _unstack4(x4, jp):
    r0 = 2 * jp * BLK
    lo = x4[r0:r0 + BLK]
    hi = x4[r0 + BLK:r0 + 2 * BLK]
    return jnp.where(_half_mask(lo.shape, 0), lo, hi)


def _softmax_parts(s, bias4, sink_ref, g):
    sink_col = LOG2E * jnp.concatenate(
        [jnp.zeros((BLK, 1), F32) + sink_ref[4 * g + r:4 * g + r + 1, 0:1] for r in range(4)], axis=0)
    s = s + bias4
    m = jnp.maximum(jnp.max(s, axis=-1, keepdims=True), sink_col)
    e = jnp.exp2(s - m)
    es = jnp.exp2(sink_col - m)
    return e, es, jnp.sum(e, axis=-1, keepdims=True) + es


def _att_fwd(P, rope, sinkb, L, Lc, comm=()):
    n = L // BLK
    specs, args = _att_inputs(P, rope, L, Lc)

    def body(*refs):
        sink_ref, o_ref = refs[12], refs[13]
        i = pl.program_id(0)
        kd, vd, bias4, cq, sq = _att_prep(i, n, refs[:12], Lc)
        def raw_scores(g):
            q4 = _stack4(refs[0], g, lambda t: _rope(t, cq, sq, 16) * Q_SCALE).astype(BF16)
            return _dot(q4, kd[g], NT)

        s_next = raw_scores(0)
        for g in range(ATT_KV):
            s = s_next
            if g + 1 < ATT_KV:
                s_next = raw_scores(g + 1)
            e, _, l = _softmax_parts(s, bias4, sink_ref, g)
            o4 = _dot(e.astype(BF16), vd[g], NN) * (1.0 / l)
            for jp in range(2):
                c0 = (2 * g + jp) * LANES
                o_ref[:, c0:c0 + LANES] = _unstack4(o4, jp).astype(o_ref.dtype)

    return _call(
        body, "att_fwd", (n,),
        specs + [pl.BlockSpec((ATT_HEADS, LANES), lambda i: (0, 0))],
        [pl.BlockSpec((BLK, 1024), lambda i: (i, 0))], [jax.ShapeDtypeStruct((L, 1024), BF16)], [],
        ("parallel",), (*args, sinkb), comm)


def _att_bwd(P, rope, sinkb, y_att, dY, L, Lc, comm=()):
    n = L // BLK
    specs, args = _att_inputs(P, rope, L, Lc)
    nk = N_LOC + Lc

    def body(*refs):
        sink_ref, y_ref, dy_ref = refs[12], refs[13], refs[14]
        dq_ref, dkl_ref, dvl_ref, dkx_ref, dvx_ref, dsk_ref = refs[15:21]
        i = pl.program_id(0)

        @pl.when(i == 0)
        def _():
            dkx_ref[...] = jnp.zeros_like(dkx_ref)
            dvx_ref[...] = jnp.zeros_like(dvx_ref)
            dsk_ref[...] = jnp.zeros_like(dsk_ref)

        kd, vd, bias4, cq, sq = _att_prep(i, n, refs[:12], Lc)
        def first_matmuls(g):
            q4 = _stack4(refs[0], g, lambda x: _rope(x, cq, sq, 16) * Q_SCALE).astype(BF16)
            do4 = _stack4(dy_ref, g)
            delta = jnp.sum(do4 * _stack4(y_ref, g), axis=-1, keepdims=True)
            do4b = do4.astype(BF16)
            return q4, do4b, delta, _dot(q4, kd[g], NT), _dot(do4b, vd[g], NT)

        nxt = first_matmuls(0)
        for t in range(ATT_KV // 2):
            dk_halves, dv_halves = [], []
            for b in range(2):
                g = 2 * t + b
                q4, do4b, delta, s, dpm = nxt
                if g + 1 < ATT_KV:
                    nxt = first_matmuls(g + 1)
                e, es, l = _softmax_parts(s, bias4, sink_ref, g)
                inv = 1.0 / l
                p = e * inv
                dsc = (p * (dpm - delta)).astype(BF16)
                dsr = es * inv * delta
                for r in range(4):
                    h = 4 * g + r
                    dsk_ref[h:h + 1, :] += jnp.zeros((1, LANES), F32) - jnp.sum(dsr[r * BLK:(r + 1) * BLK])
                dq4 = _dot(dsc, kd[g], NN) * A_SCALE
                for jp in range(2):
                    c0 = (2 * g + jp) * LANES
                    dq_ref[:, c0:c0 + LANES] = _rope_t(_unstack4(dq4, jp), cq, sq, 16)
                dkd = _dot(q4, dsc, TN) * LN2
                dvd = _dot(do4b, p.astype(BF16), TN)
                dk_halves.append(dkd[:ATT_DH] + dkd[ATT_DH:])
                dv_halves.append(dvd[:ATT_DH] + dvd[ATT_DH:])
            dk_t = jnp.concatenate(dk_halves, axis=0).T
            dv_t = jnp.concatenate(dv_halves, axis=0).T
            sl = slice(t * LANES, (t + 1) * LANES)
            dkl_ref[0, :, sl] = dk_t[:N_LOC]
            dvl_ref[0, :, sl] = dv_t[:N_LOC]
            dkx_ref[:, sl] += dk_t[N_LOC:]
            dvx_ref[:, sl] += dv_t[N_LOC:]

    row = pl.BlockSpec((BLK, 1024), lambda i: (i, 0))
    loc = pl.BlockSpec((1, N_LOC, 256), lambda i: (i, 0, 0))
    cx = pl.BlockSpec((Lc, 256), lambda i: (0, 0))
    return _call(
        body, "att_bwd", (n,),
        specs + [pl.BlockSpec((ATT_HEADS, LANES), lambda i: (0, 0)), row, pl.BlockSpec((BLK, 1024), lambda i: (i, 1))],
        [row, loc, loc, cx, cx, pl.BlockSpec((ATT_HEADS, LANES), lambda i: (0, 0))],
        [jax.ShapeDtypeStruct((L, 1024), F32), jax.ShapeDtypeStruct((n, N_LOC, 256), F32),
         jax.ShapeDtypeStruct((n, N_LOC, 256), F32), jax.ShapeDtypeStruct((Lc, 256), F32),
         jax.ShapeDtypeStruct((Lc, 256), F32), jax.ShapeDtypeStruct((ATT_HEADS, LANES), F32)], [],
        ("arbitrary",), (*args, sinkb, y_att, dY), comm)


def _assemble_dp(L, Lc, dqf, dqb, dkf, dkb, dvf, dvb, drg, daq, dkl, dvl, rope_att, dck, dcv, dkx, dvx):
    n = L // BLK
    nc = Lc // BLK

    def body(dqf_r, dqb_r, dkf_r, dkb_r, dvf_r, dvb_r, drg_r, daq_r, kl0, kl1, kl2, vl0, vl1, vl2, rp_r,
             dck_r, dcv_r, dkx_r, dvx_r, o_ref):
        i = pl.program_id(0)

        @pl.when(i < n)
        def _():
            o_ref[:, C_RQ:C_RK] = (dqf_r[...] + dqb_r[...]).astype(o_ref.dtype)
            o_ref[:, C_RK:C_RV] = (dkf_r[...] + dkb_r[...]).astype(o_ref.dtype)
            o_ref[:, C_RV:C_RG] = (dvf_r[...] + dvb_r[...]).astype(o_ref.dtype)
            o_ref[:, C_RG:C_AQ] = drg_r[...].astype(o_ref.dtype)
            o_ref[:, C_AQ:C_AK] = daq_r[...].astype(o_ref.dtype)
            w0 = jnp.where(i > 0, 1.0, 0.0)
            w2 = jnp.where(i < n - 1, 1.0, 0.0)
            dk = kl0[0, 2 * BLK:3 * BLK, :] * w0 + kl1[0, BLK:2 * BLK, :] + kl2[0, 0:BLK, :] * w2
            dv = vl0[0, 2 * BLK:3 * BLK, :] * w0 + vl1[0, BLK:2 * BLK, :] + vl2[0, 0:BLK, :] * w2
            for t in range(ATT_KV // 2):
                sl = slice(t * LANES, (t + 1) * LANES)
                o_ref[:, C_AK + t * LANES:C_AK + (t + 1) * LANES] = _rope_t(
                    dk[:, sl], rp_r[0], rp_r[1], 16).astype(o_ref.dtype)
            o_ref[:, C_AV:D_PROJ] = dv.astype(o_ref.dtype)

        @pl.when(i >= n)
        def _():
            o_ref[:, C_RQ:C_RK] = jnp.zeros((BLK, C_RK - C_RQ), o_ref.dtype)
            o_ref[:, C_RK:C_RV] = dck_r[...].astype(o_ref.dtype)
            o_ref[:, C_RV:C_RG] = dcv_r[...].astype(o_ref.dtype)
            o_ref[:, C_RG:C_AK] = jnp.zeros((BLK, C_AK - C_RG), o_ref.dtype)
            o_ref[:, C_AK:C_AV] = dkx_r[...].astype(o_ref.dtype)
            o_ref[:, C_AV:D_PROJ] = dvx_r[...].astype(o_ref.dtype)

    xm = lambda i: jnp.minimum(i, n - 1)
    cm = lambda i: jnp.clip(i - n, 0, nc - 1)
    r512 = pl.BlockSpec((BLK, 512), lambda i: (xm(i), 0))
    r1024 = pl.BlockSpec((BLK, 1024), lambda i: (xm(i), 0))
    part = lambda off: pl.BlockSpec((1, N_LOC, 256), lambda i: (jnp.clip(xm(i) + off, 0, n - 1), 0, 0))
    return pl.pallas_call(
        body, name="assemble_dp", grid=(n + nc,),
        in_specs=[r512, r512, r512, r512, r1024, r1024, r1024, r1024,
                  part(-1), part(0), part(1), part(-1), part(0), part(1),
                  pl.BlockSpec((2, BLK, LANES), lambda i: (0, xm(i), 0)),
                  pl.BlockSpec((BLK, 512), lambda i: (cm(i), 0)), pl.BlockSpec((BLK, 1024), lambda i: (cm(i), 0)),
                  pl.BlockSpec((BLK, 256), lambda i: (cm(i), 0)), pl.BlockSpec((BLK, 256), lambda i: (cm(i), 0))],
        out_specs=pl.BlockSpec((BLK, D_PROJ), lambda i: (i, 0)),
        out_shape=jax.ShapeDtypeStruct((L + Lc, D_PROJ), BF16),
        compiler_params=_cparams(("parallel",)),
    )(dqf, dqb, dkf, dkb, dvf, dvb, drg, daq, dkl, dkl, dkl, dvl, dvl, dvl, rope_att, dck, dcv, dkx, dvx)


def _adam_math(w, g, m, v):
    m = ADAM_B1 * m + (1.0 - ADAM_B1) * g
    v = ADAM_B2 * v + (1.0 - ADAM_B2) * (g * g)
    m_hat = m / (1.0 - ADAM_B1 ** ADAM_STEP)
    v_hat = v / (1.0 - ADAM_B2 ** ADAM_STEP)
    delta = -ADAM_LR * (m_hat / (jnp.sqrt(v_hat) + ADAM_EPS) + ADAM_WD * w)
    return delta, m, v


def _sum_parts(name, parts):
    n_parts, R, C = parts.shape
    tr = _pick(R, (64, 32, 16, 8))

    def body(p_ref, o_ref):
        acc = p_ref[0].astype(F32)
        for j in range(1, n_parts):
            acc = acc + p_ref[j].astype(F32)
        o_ref[...] = acc

    return pl.pallas_call(
        body, name=name, grid=(R // tr,),
        in_specs=[pl.BlockSpec((n_parts, tr, C), lambda i: (0, i, 0))],
        out_specs=pl.BlockSpec((tr, C), lambda i: (i, 0)),
        out_shape=jax.ShapeDtypeStruct((R, C), F32),
        compiler_params=_cparams(("parallel",)),
    )(parts)


def _adam(name, w, m, v, g=None, parts=None):
    R, C = w.shape
    tr = _pick(R, (256, 128, 64, 32, 16, 8))
    summed = parts is not None
    n_parts = parts.shape[0] if summed else 0

    def body(w_ref, m_ref, v_ref, g_ref, go_ref, d_ref, mo_ref, vo_ref):
        if summed:
            gv = g_ref[0].astype(F32)
            for j in range(1, n_parts):
                gv = gv + g_ref[j].astype(F32)
        else:
            gv = g_ref[...]
        d, mn, vn = _adam_math(w_ref[...], gv, m_ref[...], v_ref[...])
        go_ref[...] = gv
        d_ref[...] = d
        mo_ref[...] = mn
        vo_ref[...] = vn

    row = pl.BlockSpec((tr, C), lambda i: (i, 0))
    gspec = pl.BlockSpec((n_parts, tr, C), lambda i: (0, i, 0)) if summed else row
    return pl.pallas_call(
        body, name=name, grid=(R // tr,),
        in_specs=[row, row, row, gspec], out_specs=[row] * 4,
        out_shape=[jax.ShapeDtypeStruct((R, C), F32)] * 4,
        compiler_params=_cparams(("parallel",)),
    )(w, m, v, parts if summed else g)


def _rows_full(g):
    _, R, D = g.shape
    return g.reshape(N_DEV * R, D)


def _rows_slots(g):
    N, D = g.shape
    return g.reshape(N_DEV, N // N_DEV, D)


def _pad_rows(a, rows):
    return jnp.concatenate([a, jnp.zeros((rows - a.shape[0],) + a.shape[1:], a.dtype)], axis=0)


def kernel(x, c, ctx, c_ctx, w_mod, b_mod, norm_mix, norm_ffn, w_in, ret_decay, attn_sink, w_out, w_gate, w_up, w_down, norm_final, loss_target, m_c_ctx, m_w_mod, m_b_mod, m_norm_mix, m_norm_ffn, m_w_in, m_ret_decay, m_attn_sink, m_w_out, m_w_gate, m_w_up, m_w_down, m_norm_final, v_c_ctx, v_w_mod, v_b_mod, v_norm_mix, v_norm_ffn, v_w_in, v_ret_decay, v_attn_sink, v_w_out, v_w_gate, v_w_up, v_w_down, v_norm_final):
    L, D = x.shape[1], x.shape[2]
    Lc = ctx.shape[1]
    DF = w_gate.shape[2] * N_DEV
    C6 = w_mod.shape[2]
    me = _my_id()
    xs, cx, tgt = x[0], ctx[0], loss_target[0]

    ag_in = ("ag2", w_in[0].T.astype(BF16))
    ag_out, ag_gate = ("ag2", w_out[0].astype(BF16)), ("ag2", w_gate[0].T.astype(BF16))
    ag_up, ag_down = ("ag2", w_up[0].T.astype(BF16)), ("ag2", w_down[0].astype(BF16))

    cs = _allgather(c, "ag_c")[:, 0, :]
    s_in = _pad_rows(jnp.concatenate([cs, c_ctx[None, :]], axis=0), 16)
    b_l = lax.dynamic_slice_in_dim(b_mod, me * C6, C6, axis=1)
    mod_parts = _allgather(_mod_fwd(s_in, w_mod[0], b_l), "ag_mod")
    mod = _pad_rows(lax.dynamic_index_in_dim(mod_parts, me, axis=1, keepdims=False).reshape(6, D), 8)
    modc = _pad_rows(mod_parts[:, N_DEV, :].reshape(6, D), 8)
    mix_mod, ffn_mod = mod, jnp.roll(mod, -3, axis=0)
    gt_m, gt_f = mod[2:3], mod[5:6]

    rope_ret, rope_att = _rope_tables(L)
    rdb = jnp.broadcast_to(ret_decay[0].reshape(2 * RET_HEADS, 1, 1), (2 * RET_HEADS, 1, LANES))
    sinkb = jnp.broadcast_to(attn_sink[0].reshape(ATT_HEADS, 1), (ATT_HEADS, LANES))

    tm = _pick(L + Lc, (1408, 768, 512, 384, 256, 128))
    tmx = _pick(L, (1024, 512, 256, 128))

    (H,), (g_in,) = _modulate_fwd("mod_mix_fwd", xs, cx, norm_mix, mix_mod, modc, comm=[ag_in])
    W_inT = _rows_full(g_in)
    ident = lambda a, e: a
    tP, tD, tF = _pick(D_PROJ, (1152, 768, 512)), _pick(D, (2048, 1024, 512)), _pick(DF, (512, 256, 128))
    (P,), (g_gate,) = _matmul("mm_in", [(H, W_inT, 0)], 1, L + Lc, D_PROJ, D, "nt", (tm, tP, D), [], [BF16], ident,
                              comm=[ag_gate])
    W_gateT = _rows_full(g_gate)
    tabs, ctab = _ret_tables(rdb, Lc)
    s0 = _ret_ctx_state(P, ctab, L, Lc)
    (o_f, o_b, st_f, st_b), (g_out,) = _ret_fwd(P, rope_ret, tabs, s0, L, comm=[ag_out])
    W_out = _rows_full(g_out)
    y_ret = _ret_finish_fwd(o_f, o_b, P, L)
    (y_att,), (g_up,) = _att_fwd(P, rope_att, sinkb, L, Lc, comm=[ag_up])
    W_upT = _rows_full(g_up)
    Y =jnp.concatenate([y_ret, y_att], axis=1)
    KO = Y.shape[1]
    f_mix = _matmul("mm_out", [(Y, W_out, 0)], 1, L, D, KO, "nn", (tmx, tD, KO), [], [BF16], ident)[0]

    x1, H2 = _residual_modulate_fwd("mod_ffn_fwd", xs, f_mix, gt_m, norm_ffn, ffn_mod)

    def swiglu_epi(a, e):
        return [a[0], a[1], a[0] * _sigmoid(a[0]) * a[1]]

    tm2 = tmx
    (ga, up, hmid), (g_down,) = _matmul("mm_gate_up", [(H2, W_gateT, 0), (H2, W_upT, 1)], 2, L, DF, D, "nt",
                                        (tm2, tF, D), [], [BF16, BF16, BF16], swiglu_epi, comm=[ag_down])
    W_down = _rows_full(g_down)
    f_ffn = _matmul("mm_down", [(hmid, W_down, 0)], 1, L, D, DF, "nn", (tm2, tD, tF), [], [BF16], ident)[0]

    dx2, dFf, sums_l = _loss_head(x1, tgt, norm_final.reshape(1, D), f_ffn, gt_f)

    def dswiglu_epi(a, e):
        av, uv = e[0].astype(F32), e[1].astype(F32)
        sg = _sigmoid(av)
        return [a[0] * uv * (sg * (1.0 + av * (1.0 - sg))), a[0] * (av * sg)]

    dga, dup = _matmul("mm_d_down", [(dFf, W_down, 0)], 1, L, DF, D, "nt", (tm2, tF, D),
                       [(ga, "mn"), (up, "mn")], [BF16, BF16], dswiglu_epi)
    tkt = _pick(L, (512, 256, 128))
    dW_down = _matmul("mm_gw_down", [(hmid, dFf, 0)], 1, DF, D, L, "tn",
                      (_pick(DF, (1408, 512, 256, 128)), tD, tkt), [], [BF16], ident)[0]
    (dW_gateT, dW_upT), (p_down,) = _matmul("mm_gw_gate_up", [(dga, H2, 0), (dup, H2, 1)], 2, DF, D, L, "tn",
                                            (tF, tD, tkt), [], [BF16, BF16], ident,
                                            comm=[("a2a", _rows_slots(dW_down))])
    (dH2,), (p_gate,) = _matmul("mm_d_gate_up", [(dga, W_gateT, 0), (dup, W_upT, 0)], 1, L, D, DF, "nn",
                                (tm2, tD, tF), [], [BF16], ident, comm=[("a2a", _rows_slots(dW_gateT))])
    dx1, dFm, sums_f = _modulate_bwd("mod_ffn_bwd", x1, None, dH2, norm_ffn, ffn_mod, None, dx2, f_mix, gt_m)

    tO = _pick(KO, (2048, 1024, 512))
    dY = _matmul("mm_d_out", [(dFm, W_out, 0)], 1, L, KO, D, "nt", (tmx, tO, D), [], [BF16], ident)[0]
    dW_out = _matmul("mm_gw_out", [(Y, dFm, 0)], 1, KO, D, L, "tn", (_pick(KO, (1024, 512)), tD, tkt), [], [BF16],
                     ident)[0]
    dO, drg = _ret_finish_bwd(o_f, o_b, P, dY, L)
    (dqf, dkf, dvf, dqb, dkb, dvb, ds0, dlg), (p_out,) = _ret_bwd(
        P, rope_ret, tabs, st_f, st_b, dO, L, comm=[("a2a", _rows_slots(dW_out))])
    dck, dcv, d_rd = _ret_ctx_bwd(P, ctab, ds0, dlg, rdb, L, Lc)
    (daq, dkl, dvl, dkx, dvx, d_sink), (p_up,) = _att_bwd(
        P, rope_att, sinkb, y_att, dY, L, Lc, comm=[("a2a", _rows_slots(dW_upT))])
    dP = _assemble_dp(L, Lc, dqf, dqb, dkf, dkb, dvf, dvb, drg, daq, dkl, dvl, rope_att, dck, dcv, dkx, dvx)
    tkc = _pick(L + Lc, (768, 384, 256, 128))
    dW_inT = _matmul("mm_gw_in", [(dP, H, 0)], 1, D_PROJ, D, L + Lc, "tn", (tP, tD, tkc), [], [BF16], ident)[0]
    (dH,), (p_in,) = _matmul("mm_d_in", [(dP, W_inT, 0)], 1, L + Lc, D, D_PROJ, "nn",
                             (tm, tD, _pick(D_PROJ, (512, 256))), [], [BF16], ident,
                             comm=[("a2a", _rows_slots(dW_inT))])
    grad_x, sums_m = _modulate_bwd("mod_mix_bwd", xs, cx, dH, norm_mix, mix_mod, modc, dx1, None, None)

    zero = jnp.zeros((1, D), F32)
    dmod = jnp.concatenate([sums_m[0:1], sums_m[1:2], sums_f[6:7], sums_f[0:1], sums_f[1:2], sums_l[2:3]], axis=1)
    dmodc = jnp.concatenate([sums_m[3:4], sums_m[4:5], zero, zero, zero, zero], axis=1)
    dm_all = _allgather(jnp.concatenate([dmod, dmodc], axis=0), "ag_dmod")
    dm_cols = lax.dynamic_slice_in_dim(dm_all, me * C6, C6, axis=2)
    dm_in = jnp.concatenate([dm_cols[:, 0, :], dm_cols[:, 1, :]], axis=0)
    s_bwd = jnp.concatenate([cs, jnp.broadcast_to(c_ctx[None, :], (N_DEV, D))], axis=0)
    g_w_mod, dsil = _mod_bwd(s_bwd, dm_in, w_mod[0])

    lane_pad = lambda a: _pad_rows(a.reshape(-1, 1), LANES).reshape(1, LANES)
    pack = jnp.concatenate([dsil[0:1], sums_m[2:3], sums_f[2:3], sums_l[1:2],
                            lane_pad(d_rd[:, 0]), lane_pad(d_sink[:, 0]), sums_l[3:4, 0:LANES]], axis=1)
    packs = _allgather(pack, "ag_small")
    zl = jnp.zeros((1, LANES), F32)

    def pack_w(a_c, a_nm, a_nf, a_fin, a_rd, a_sk):
        return jnp.concatenate([a_c.reshape(1, D), a_nm, a_nf, a_fin.reshape(1, D), lane_pad(a_rd.reshape(-1)),
                                lane_pad(a_sk.reshape(-1)), zl], axis=1)

    sg, sd, sm, sv = _adam("adam_small", pack_w(c_ctx, norm_mix, norm_ffn, norm_final, ret_decay, attn_sink),
                           pack_w(m_c_ctx, m_norm_mix, m_norm_ffn, m_norm_final, m_ret_decay, m_attn_sink),
                           pack_w(v_c_ctx, v_norm_mix, v_norm_ffn, v_norm_final, v_ret_decay, v_attn_sink),
                           parts=packs)
    loss = sg[0, 4 * D + 2 * LANES]

    def unpack(a):
        return (a[0, 0:D], a[:, D:2 * D], a[:, 2 * D:3 * D], a[0, 3 * D:4 * D],
                a[0, 4 * D:4 * D + 2 * RET_HEADS].reshape(1, 2, RET_HEADS),
                a[:, 4 * D + LANES:4 * D + LANES + ATT_HEADS])

    bg, bd, bm, bv = _adam("adam_b_mod", b_mod, m_b_mod, v_b_mod, parts=dm_all.reshape(2 * N_DEV, 1, 6 * D))
    wg, wd, wm, wv = _adam("adam_w_mod", w_mod[0], m_w_mod[0], v_w_mod[0], g=g_w_mod)

    big = {}
    for nm, w, m, v, parts, transposed in (
            ("w_in", w_in, m_w_in, v_w_in, p_in, True), ("w_out", w_out, m_w_out, v_w_out, p_out, False),
            ("w_gate", w_gate, m_w_gate, v_w_gate, p_gate, True), ("w_up", w_up, m_w_up, v_w_up, p_up, True),
            ("w_down", w_down, m_w_down, v_w_down, p_down, False)):
        if transposed:
            res = _adam("adam_" + nm, w[0], m[0], v[0], g=_sum_parts("sum_" + nm, parts).T)
        else:
            res = _adam("adam_" + nm, w[0], m[0], v[0], parts=parts)
        big[nm] = [a[None] for a in res]

    g_s, d_s, m_s, v_s = unpack(sg), unpack(sd), unpack(sm), unpack(sv)

    def leaves(k, small, bmod, wmod):
        return (small[0], wmod[None], bmod, small[1], small[2], big["w_in"][k], small[4], small[5],
                big["w_out"][k], big["w_gate"][k], big["w_up"][k], big["w_down"][k], small[3])

    return (loss, grad_x[None], *leaves(0, g_s, bg, wg), *leaves(1, d_s, bd, wd),
            *leaves(2, m_s, bm, wm), *leaves(3, v_s, bv, wv))
```

```python
import functools

import jax
import jax.numpy as jnp
from jax import lax
from jax.experimental import pallas as pl
from jax.experimental.pallas import tpu as pltpu

F32 = jnp.float32
BF16 = jnp.bfloat16

N_DEV = 8
LANES = 128
RET_HEADS = 8
RET_DK = 64
RET_DV = 128
CHUNK = 128
ATT_HEADS = 16
ATT_KV = 4
ATT_DH = 64
GRID_W = 64
ROPE_BASE = 10000.0
EPS = 1e-6
NEG = -1e30
C_RQ, C_RK, C_RV, C_RG, C_AQ, C_AK, C_AV, D_PROJ = 0, 512, 1024, 2048, 3072, 4096, 4352, 4608
K_SCALE = RET_DK ** -0.5
A_SCALE = ATT_DH ** -0.5

ADAM_LR, ADAM_B1, ADAM_B2, ADAM_EPS, ADAM_WD, ADAM_STEP = 0.001, 0.9, 0.999, 1e-08, 0.01, 10

VMEM_BIG = 52 * 1024 * 1024

NN = (((1,), (0,)), ((), ()))
NT = (((1,), (1,)), ((), ()))
TN = (((0,), (0,)), ((), ()))


def _dot(a, b, dims):
    return lax.dot_general(a, b, dims, preferred_element_type=F32)


def _cparams(sem, vmem=VMEM_BIG):
    return pltpu.CompilerParams(dimension_semantics=sem, vmem_limit_bytes=vmem)


def _pick(dim, prefs):
    for p in prefs:
        if dim % p == 0:
            return p
    return dim


def _my_id():
    return lax.axis_index("x") * 4 + lax.axis_index("y") * 2 + lax.axis_index("c")


def _sigmoid(x):
    return 0.5 * jnp.tanh(0.5 * x) + 0.5


def _peers():
    mx, my, mc = lax.axis_index("x"), lax.axis_index("y"), lax.axis_index("c")
    out = []
    for k in range(1, N_DEV):
        kx, ky, kc = (k >> 2) & 1, (k >> 1) & 1, k & 1
        px = 1 - mx if kx else mx
        py = 1 - my if ky else my
        pc = 1 - mc if kc else mc
        out.append(((px, py, pc), px * 4 + py * 2 + pc))
    return out


def _exchange_copies(kind, x_ref, o_ref, ssem, rsem, lsem):
    me = _my_id()
    loc = pltpu.make_async_copy(x_ref if kind == "ag" else x_ref.at[me], o_ref.at[me], lsem)
    cps = []
    for k, (peer, pid) in enumerate(_peers()):
        cps.append(pltpu.make_async_remote_copy(
            src_ref=x_ref if kind == "ag" else x_ref.at[pid], dst_ref=o_ref.at[me],
            send_sem=ssem.at[k], recv_sem=rsem.at[k], device_id=peer, device_id_type=pl.DeviceIdType.MESH))
    return loc, cps


def _two_level_copies(x_ref, o_ref, ssem, rsem, lsem):
    mx, my, mc = lax.axis_index("x"), lax.axis_index("y"), lax.axis_index("c")
    me = mx * 4 + my * 2 + mc
    sibling = (mx, my, 1 - mc)
    chips = [(1 - mx, my), (mx, 1 - my), (1 - mx, 1 - my)]

    def copy(k, slot, to, src=None):
        return pltpu.make_async_remote_copy(
            src_ref=o_ref.at[slot] if src is None else src, dst_ref=o_ref.at[slot],
            send_sem=ssem.at[k], recv_sem=rsem.at[k], device_id=to, device_id_type=pl.DeviceIdType.MESH)

    loc = pltpu.make_async_copy(x_ref, o_ref.at[me], lsem)
    first = [copy(0, me, sibling, src=x_ref)]
    first += [copy(1 + j, me, (cx, cy, mc), src=x_ref) for j, (cx, cy) in enumerate(chips)]
    passed = [copy(4 + j, cx * 4 + cy * 2 + mc, sibling) for j, (cx, cy) in enumerate(chips)]
    return loc, first, passed


def _exchange_start(kind, x_ref, o_ref, ssem, rsem, lsem):
    if kind == "ag2":
        loc, first, _ = _two_level_copies(x_ref, o_ref, ssem, rsem, lsem)
        cps = first
    else:
        loc, cps = _exchange_copies(kind, x_ref, o_ref, ssem, rsem, lsem)
    loc.start()
    for cp in cps:
        cp.start()


def _exchange_wait(kind, x_ref, o_ref, ssem, rsem, lsem):
    if kind == "ag2":
        loc, first, passed = _two_level_copies(x_ref, o_ref, ssem, rsem, lsem)
        for j in range(3):
            first[1 + j].wait_recv()
            passed[j].start()
        first[0].wait_recv()
        for cp in passed:
            cp.wait_recv()
        cps = first + passed
    else:
        loc, cps = _exchange_copies(kind, x_ref, o_ref, ssem, rsem, lsem)
        for cp in cps:
            cp.wait_recv()
    for cp in cps:
        cp.wait_send()
    loc.wait()


_EXCHANGE_SEMS = [pltpu.SemaphoreType.DMA((N_DEV - 1,)), pltpu.SemaphoreType.DMA((N_DEV - 1,)),
                  pltpu.SemaphoreType.DMA(())]


def _exchange_shape(kind, x):
    return jax.ShapeDtypeStruct(x.shape if kind == "a2a" else (N_DEV,) + x.shape, x.dtype)


def _exchange(kind, x, name):
    def body(x_ref, o_ref, ssem, rsem, lsem):
        _exchange_start(kind, x_ref, o_ref, ssem, rsem, lsem)
        _exchange_wait(kind, x_ref, o_ref, ssem, rsem, lsem)

    return pl.pallas_call(
        body, name=name, out_shape=_exchange_shape(kind, x),
        in_specs=[pl.BlockSpec(memory_space=pl.ANY)], out_specs=pl.BlockSpec(memory_space=pl.ANY),
        scratch_shapes=list(_EXCHANGE_SEMS),
    )(x)


def _allgather(x, name):
    return _exchange("ag", x, name)


def _call(body, name, grid, in_specs, out_specs, out_shape, scratch_shapes, sem, args, comm=(), aliases=None):
    in_specs, out_specs, out_shape = list(in_specs), list(out_specs), list(out_shape)
    scratch_shapes = list(scratch_shapes)
    aliases = aliases or {}
    if not comm:
        outs = pl.pallas_call(body, name=name, grid=grid, in_specs=in_specs, out_specs=out_specs, out_shape=out_shape,
                              scratch_shapes=scratch_shapes, input_output_aliases=aliases,
                              compiler_params=_cparams(sem))(*args)
        return list(outs), []
    n_in, n_out, n_scr, n_c = len(in_specs), len(out_specs), len(scratch_shapes), len(comm)
    hbm = pl.BlockSpec(memory_space=pl.ANY)

    def wrapped(*refs):
        ins, cins = refs[:n_in], refs[n_in:n_in + n_c]
        outs = refs[n_in + n_c:n_in + n_c + n_out]
        couts = refs[n_in + n_c + n_out:n_in + 2 * n_c + n_out]
        scr = refs[n_in + 2 * n_c + n_out:n_in + 2 * n_c + n_out + n_scr]
        sems = refs[n_in + 2 * n_c + n_out + n_scr:]
        first = pl.program_id(0) == 0
        last = pl.program_id(0) == grid[0] - 1
        for ax in range(1, len(grid)):
            first = first & (pl.program_id(ax) == 0)
            last = last & (pl.program_id(ax) == grid[ax] - 1)

        @pl.when(first)
        def _():
            for c, (kind, _) in enumerate(comm):
                _exchange_start(kind, cins[c], couts[c], *sems[3 * c:3 * c + 3])

        body(*ins, *outs, *scr)

        @pl.when(last)
        def _():
            for c, (kind, _) in enumerate(comm):
                _exchange_wait(kind, cins[c], couts[c], *sems[3 * c:3 * c + 3])

    res = pl.pallas_call(
        wrapped, name=name, grid=grid,
        in_specs=in_specs + [hbm] * n_c, out_specs=out_specs + [hbm] * n_c,
        out_shape=out_shape + [_exchange_shape(kind, arr) for kind, arr in comm],
        scratch_shapes=scratch_shapes + list(_EXCHANGE_SEMS) * n_c, input_output_aliases=aliases,
        compiler_params=_cparams(("arbitrary",) * len(grid)),
    )(*args, *[arr for _, arr in comm])
    return list(res[:n_out]), list(res[n_out:])


def _matmul(name, pairs, n_acc, M, N, K, mode, tiles, extras, out_dtypes, epilogue, j_outer=False, comm=()):
    tm, tn, tk = tiles
    gm, gn, nk = M // tm, N // tn, K // tk
    assert gm * tm == M and gn * tn == N and nk * tk == K, (name, M, N, K, tiles)
    if j_outer:
        grid = (gn, gm, nk)
        ij = lambda g0, g1: (g1, g0)
    else:
        grid = (gm, gn, nk)
        ij = lambda g0, g1: (g0, g1)

    if mode in ("nn", "nt"):
        a_spec = pl.BlockSpec((tm, tk), lambda g0, g1, k: (ij(g0, g1)[0], k))
    else:
        a_spec = pl.BlockSpec((tk, tm), lambda g0, g1, k: (k, ij(g0, g1)[0]))
    if mode == "nt":
        b_spec = pl.BlockSpec((tn, tk), lambda g0, g1, k: (ij(g0, g1)[1], k))
    else:
        b_spec = pl.BlockSpec((tk, tn), lambda g0, g1, k: (k, ij(g0, g1)[1]))
    dims = {"nn": NN, "nt": NT, "tn": TN}[mode]
    mn_spec = pl.BlockSpec((tm, tn), lambda g0, g1, k: ij(g0, g1))
    n_spec = pl.BlockSpec((1, tn), lambda g0, g1, k: (0, ij(g0, g1)[1]))

    in_specs, args = [], []
    for a, b, _ in pairs:
        in_specs += [a_spec, b_spec]
        args += [a, b]
    for arr, kind in extras:
        in_specs.append(mn_spec if kind == "mn" else n_spec)
        args.append(arr)
    n_p, n_e, n_o = len(pairs), len(extras), len(out_dtypes)

    def body(*refs):
        ab = refs[:2 * n_p]
        ex = refs[2 * n_p:2 * n_p + n_e]
        outs = refs[2 * n_p + n_e:2 * n_p + n_e + n_o]
        accs = refs[2 * n_p + n_e + n_o:]
        k = pl.program_id(2)

        def partial_sums():
            sums = [None] * n_acc
            for p, (_, _, ai) in enumerate(pairs):
                d = _dot(ab[2 * p][...], ab[2 * p + 1][...], dims)
                sums[ai] = d if sums[ai] is None else sums[ai] + d
            return sums

        def finish(acc_vals):
            res = epilogue(acc_vals, [e[...] for e in ex])
            for o, r in zip(outs, res):
                o[...] = r.astype(o.dtype)

        if nk == 1:
            finish(partial_sums())
        else:
            @pl.when(k == 0)
            def _():
                for ai, s in enumerate(partial_sums()):
                    accs[ai][...] = s

            @pl.when(k > 0)
            def _():
                for ai, s in enumerate(partial_sums()):
                    accs[ai][...] += s

            @pl.when(k == nk - 1)
            def _():
                finish([a[...] for a in accs])

    outs, couts = _call(
        body, name, grid, in_specs, [mn_spec] * n_o,
        [jax.ShapeDtypeStruct((M, N), dt) for dt in out_dtypes],
        [pltpu.VMEM((tm, tn), F32) for _ in range(n_acc if nk > 1 else 0)],
        ("parallel", "parallel", "arbitrary"), args, comm)
    return (outs, couts) if comm else outs


def _rope_tables(L):
    t = jnp.arange(L, dtype=jnp.int32)
    f = jnp.arange(32, dtype=jnp.int32).astype(F32)
    ang = t.astype(F32)[:, None] * (ROPE_BASE ** (-f / 32.0))[None, :]
    cos, sin = jnp.cos(ang), jnp.sin(ang)
    ret = jnp.stack([jnp.tile(cos, (1, 4)), jnp.tile(jnp.concatenate([-sin, sin], axis=1), (1, 2))])
    f2 = jnp.arange(16, dtype=jnp.int32).astype(F32)
    inv2 = (ROPE_BASE ** (-f2 / 16.0))[None, :]
    ang_r = (t // GRID_W).astype(F32)[:, None] * inv2
    ang_c = (t % GRID_W).astype(F32)[:, None] * inv2
    cr, sr, cc, sc = jnp.cos(ang_r), jnp.sin(ang_r), jnp.cos(ang_c), jnp.sin(ang_c)
    att = jnp.stack([jnp.tile(jnp.concatenate([cr, cr, cc, cc], axis=1), (1, 2)),
                     jnp.tile(jnp.concatenate([-sr, sr, -sc, sc], axis=1), (1, 2))])
    return ret.astype(F32), att.astype(F32)


def _swap(x, sh):
    lane = lax.broadcasted_iota(jnp.int32, x.shape, 1)
    ra = pltpu.roll(x, LANES - sh, 1)
    rb = pltpu.roll(x, sh, 1)
    la = pltpu.roll(lane, LANES - sh, 1)
    partner = jnp.where((lane % (2 * sh)) < sh, lane + sh, lane - sh)
    return jnp.where(la == partner, ra, rb)


def _rope(x, cos, sin, sh):
    return x * cos + _swap(x, sh) * sin


def _rope_t(d, cos, sin, sh):
    return d * cos + _swap(d * sin, sh)


def _half_mask(shape, a):
    lane = lax.broadcasted_iota(jnp.int32, shape, 1)
    return (lane < 64) if a == 0 else (lane >= 64)


def _mod_fwd(s_in, w_l, b_l):
    D, C6 = w_l.shape
    tk = _pick(D, (512, 256, 128))
    nk = D // tk

    def body(s_ref, w_ref, b_ref, o_ref):
        k = pl.program_id(0)
        s = s_ref[...]
        s = s * _sigmoid(s)
        d = jnp.dot(s, w_ref[...], preferred_element_type=F32, precision=lax.Precision.HIGHEST)

        @pl.when(k == 0)
        def _():
            o_ref[...] = d + b_ref[...]

        @pl.when(k > 0)
        def _():
            o_ref[...] += d

    return pl.pallas_call(
        body, name="mod_fwd", grid=(nk,),
        in_specs=[pl.BlockSpec((16, tk), lambda k: (0, k)), pl.BlockSpec((tk, C6), lambda k: (k, 0)),
                  pl.BlockSpec((1, C6), lambda k: (0, 0))],
        out_specs=pl.BlockSpec((16, C6), lambda k: (0, 0)),
        out_shape=jax.ShapeDtypeStruct((16, C6), F32),
        compiler_params=_cparams(("arbitrary",)),
    )(s_in, w_l, b_l)


def _mod_bwd(s_in, dm, w_l):
    D, C6 = w_l.shape
    tk = _pick(D, (512, 256, 128))
    nk = D // tk

    def body(s_ref, dm_ref, w_ref, gw_ref, gc_ref):
        s = s_ref[...]
        sg = _sigmoid(s)
        act = s * sg
        dmv = dm_ref[...]
        gw_ref[...] = lax.dot_general(act, dmv, TN, preferred_element_type=F32, precision=lax.Precision.HIGHEST)
        ds = lax.dot_general(dmv, w_ref[...], NT, preferred_element_type=F32, precision=lax.Precision.HIGHEST)
        dsil = (sg * (1.0 + s * (1.0 - sg)))[8:9, :]
        gc_ref[...] = jnp.zeros((8, tk), F32) + jnp.sum(ds[8:16, :], axis=0, keepdims=True) * dsil

    return pl.pallas_call(
        body, name="mod_bwd", grid=(nk,),
        in_specs=[pl.BlockSpec((16, tk), lambda k: (0, k)), pl.BlockSpec((16, C6), lambda k: (0, 0)),
                  pl.BlockSpec((tk, C6), lambda k: (k, 0))],
        out_specs=[pl.BlockSpec((tk, C6), lambda k: (k, 0)), pl.BlockSpec((8, tk), lambda k: (0, k))],
        out_shape=[jax.ShapeDtypeStruct((D, C6), F32), jax.ShapeDtypeStruct((8, D), F32)],
        compiler_params=_cparams(("parallel",)),
    )(s_in, dm, w_l)


def _norm_rows(x):
    r = lax.rsqrt(jnp.mean(x * x, axis=-1, keepdims=True) + EPS)
    return x * r, r


def _modulate_fwd(name, x, ctx, g, mod, modc, comm=()):
    L, D = x.shape
    tr = ctx.shape[0]
    nx = L // tr

    def body(x_ref, c_ref, g_ref, m_ref, mc_ref, o_ref):
        i = pl.program_id(0)

        def run(src, m):
            n, _ = _norm_rows(src[...])
            o_ref[...] = (n * g_ref[...] * (1.0 + m[1:2, :]) + m[0:1, :]).astype(o_ref.dtype)

        @pl.when(i < nx)
        def _():
            run(x_ref, m_ref)

        @pl.when(i >= nx)
        def _():
            run(c_ref, mc_ref)

    row = pl.BlockSpec((tr, D), lambda i: (jnp.minimum(i, nx - 1), 0))
    vec = pl.BlockSpec((1, D), lambda i: (0, 0))
    mv = pl.BlockSpec((8, D), lambda i: (0, 0))
    return _call(
        body, name, (nx + 1,), [row, pl.BlockSpec((tr, D), lambda i: (0, 0)), vec, mv, mv],
        [pl.BlockSpec((tr, D), lambda i: (i, 0))], [jax.ShapeDtypeStruct((L + tr, D), BF16)], [],
        ("parallel",), (x, ctx, g, mod, modc), comm)


def _residual_modulate_fwd(name, x, fbr, gate, g, mod):
    L, D = x.shape
    tr = _pick(L, (256, 128))

    def body(x_ref, f_ref, gt_ref, g_ref, m_ref, x1_ref, o_ref):
        x1 = x_ref[...] + gt_ref[...] * f_ref[...].astype(F32)
        x1_ref[...] = x1
        n, _ = _norm_rows(x1)
        o_ref[...] = (n * g_ref[...] * (1.0 + m_ref[1:2, :]) + m_ref[0:1, :]).astype(o_ref.dtype)

    row = pl.BlockSpec((tr, D), lambda i: (i, 0))
    vec = pl.BlockSpec((1, D), lambda i: (0, 0))
    return pl.pallas_call(
        body, name=name, grid=(L // tr,),
        in_specs=[row, row, vec, vec, pl.BlockSpec((8, D), lambda i: (0, 0))],
        out_specs=[row, row],
        out_shape=[jax.ShapeDtypeStruct((L, D), F32), jax.ShapeDtypeStruct((L, D), BF16)],
        compiler_params=_cparams(("parallel",)),
    )(x, fbr, gate, g, mod)


def _modulate_bwd(name, x, ctx, dh, g, mod, modc, dres, fbr, gate):
    L, D = x.shape
    tr = ctx.shape[0] if ctx is not None else _pick(L, (256, 128))
    nx = L // tr
    nt = nx + (1 if ctx is not None else 0)
    has_f = fbr is not None

    def body(*refs):
        refs = list(refs)
        x_ref = refs.pop(0)
        c_ref = refs.pop(0) if ctx is not None else None
        dh_ref, g_ref, m_ref = refs.pop(0), refs.pop(0), refs.pop(0)
        mc_ref = refs.pop(0) if ctx is not None else None
        dr_ref = refs.pop(0)
        f_ref = refs.pop(0) if has_f else None
        gt_ref = refs.pop(0) if has_f else None
        dx_ref = refs.pop(0)
        df_ref = refs.pop(0) if has_f else None
        acc_ref = refs.pop(0)
        i = pl.program_id(0)

        @pl.when(i == 0)
        def _():
            acc_ref[...] = jnp.zeros_like(acc_ref)

        def sums(src, m, base, grow):
            n, r = _norm_rows(src[...])
            d = dh_ref[...].astype(F32)
            gg = g_ref[...]
            sc1 = 1.0 + m[1:2, :]
            acc_ref[base:base + 1, :] += jnp.sum(d, axis=0, keepdims=True)
            dn = d * n
            acc_ref[base + 1:base + 2, :] += jnp.sum(dn, axis=0, keepdims=True) * gg
            acc_ref[grow:grow + 1, :] += jnp.sum(dn, axis=0, keepdims=True) * sc1
            dnv = d * (gg * sc1)
            return r * (dnv - n * jnp.mean(dnv * n, axis=-1, keepdims=True))

        def x_rows():
            dx = sums(x_ref, m_ref, 0, 2) + dr_ref[...]
            dx_ref[...] = dx
            if has_f:
                acc_ref[6:7, :] += jnp.sum(dx * f_ref[...].astype(F32), axis=0, keepdims=True)
                df_ref[...] = (dx * gt_ref[...]).astype(df_ref.dtype)

        if ctx is None:
            x_rows()
        else:
            pl.when(i < nx)(x_rows)

            @pl.when(i >= nx)
            def _():
                sums(c_ref, mc_ref, 3, 2)

    row = pl.BlockSpec((tr, D), lambda i: (jnp.minimum(i, nx - 1), 0))
    vec = pl.BlockSpec((1, D), lambda i: (0, 0))
    mv = pl.BlockSpec((8, D), lambda i: (0, 0))
    in_specs, args = [row], [x]
    if ctx is not None:
        in_specs.append(pl.BlockSpec((tr, D), lambda i: (0, 0)))
        args.append(ctx)
    in_specs += [pl.BlockSpec((tr, D), lambda i: (i, 0)), vec, mv]
    args += [dh, g, mod]
    if ctx is not None:
        in_specs.append(mv)
        args.append(modc)
    in_specs.append(row)
    args.append(dres)
    out_specs = [row]
    out_shape = [jax.ShapeDtypeStruct((L, D), F32)]
    if has_f:
        in_specs += [row, vec]
        args += [fbr, gate]
        out_specs.append(row)
        out_shape.append(jax.ShapeDtypeStruct((L, D), BF16))
    out_specs.append(pl.BlockSpec((16, D), lambda i: (0, 0)))
    out_shape.append(jax.ShapeDtypeStruct((16, D), F32))
    return pl.pallas_call(
        body, name=name, grid=(nt,), in_specs=in_specs, out_specs=out_specs, out_shape=out_shape,
        compiler_params=_cparams(("arbitrary",)),
    )(*args)


def _loss_head(x1, tgt, nf, fbr, gate):
    L, D = x1.shape
    tr = _pick(L, (256, 128))

    def body(x_ref, t_ref, w_ref, f_ref, gt_ref, dx_ref, df_ref, acc_ref):
        i = pl.program_id(0)

        @pl.when(i == 0)
        def _():
            acc_ref[...] = jnp.zeros_like(acc_ref)

        n, r = _norm_rows(x_ref[...] + gt_ref[...] * f_ref[...].astype(F32))
        w = w_ref[...]
        e = n * w - t_ref[...]
        acc_ref[0:1, :] += jnp.sum(e * e, axis=0, keepdims=True) * (0.5 / D)
        dout = e * (1.0 / D)
        acc_ref[1:2, :] += jnp.sum(dout * n, axis=0, keepdims=True)
        dn = dout * w
        dx = r * (dn - n * jnp.mean(dn * n, axis=-1, keepdims=True))
        dx_ref[...] = dx
        acc_ref[2:3, :] += jnp.sum(dx * f_ref[...].astype(F32), axis=0, keepdims=True)
        df_ref[...] = (dx * gt_ref[...]).astype(df_ref.dtype)

        @pl.when(i == pl.num_programs(0) - 1)
        def _():
            acc_ref[3:4, :] = jnp.zeros((1, D), F32) + jnp.sum(acc_ref[0:1, :])

    row = pl.BlockSpec((tr, D), lambda i: (i, 0))
    vec = pl.BlockSpec((1, D), lambda i: (0, 0))
    return pl.pallas_call(
        body, name="loss_head", grid=(L // tr,),
        in_specs=[row, row, vec, row, vec],
        out_specs=[row, row, pl.BlockSpec((8, D), lambda i: (0, 0))],
        out_shape=[jax.ShapeDtypeStruct((L, D), F32), jax.ShapeDtypeStruct((L, D), BF16),
                   jax.ShapeDtypeStruct((8, D), F32)],
        compiler_params=_cparams(("arbitrary",)),
    )(x1, tgt, nf, fbr, gate)


N_TAB = 7


def _ret_tables(rdb, Lc):
    def body(rd_ref, t_ref, c_ref):
        d = pl.program_id(0) // RET_HEADS
        fwd = d == 0
        lg = -jnp.exp(rd_ref[0])
        i = lax.broadcasted_iota(jnp.int32, (CHUNK, CHUNK), 0).astype(F32)
        j = lax.broadcasted_iota(jnp.int32, (CHUNK, CHUNK), 1).astype(F32)
        rel = jnp.where(fwd, i - j, j - i)
        mask = (rel > 0.0) | ((rel == 0.0) & fwd)
        dm = jnp.where(mask, jnp.exp(lg * jnp.maximum(rel, 0.0)), 0.0)
        t_ref[0, 0] = dm
        t_ref[0, 1] = rel * dm
        qc = jnp.where(fwd, i + 1.0, CHUNK - i)
        qw = jnp.exp(lg * qc)
        t_ref[0, 2] = qw
        t_ref[0, 3] = qw * qc
        kc = jnp.where(fwd, CHUNK - 1.0 - i, i)
        kw = jnp.exp(lg * kc)
        t_ref[0, 4] = kw
        t_ref[0, 5] = kw * kc
        t_ref[0, 6] = jnp.exp(lg * float(CHUNK)) + jnp.zeros((CHUNK, CHUNK), F32)
        m = lax.broadcasted_iota(jnp.int32, (Lc, LANES), 0).astype(F32)
        cc = jnp.where(fwd, Lc - 1.0 - m, m)
        cw = jnp.exp(lg * cc)
        c_ref[0, 0] = cw
        c_ref[0, 1] = cw * cc

    return pl.pallas_call(
        body, name="ret_tables", grid=(2 * RET_HEADS,),
        in_specs=[pl.BlockSpec((1, 1, LANES), lambda r: (r, 0, 0))],
        out_specs=[pl.BlockSpec((1, N_TAB, CHUNK, CHUNK), lambda r: (r, 0, 0, 0)),
                   pl.BlockSpec((1, 2, Lc, LANES), lambda r: (r, 0, 0, 0))],
        out_shape=[jax.ShapeDtypeStruct((2 * RET_HEADS, N_TAB, CHUNK, CHUNK), F32),
                   jax.ShapeDtypeStruct((2 * RET_HEADS, 2, Lc, LANES), F32)],
        compiler_params=_cparams(("parallel",)),
    )(rdb)


def _ret_ctx_state(P, ctab, L, Lc):
    cb = L // Lc

    def body(k_ref, v_ref, c_ref, s_ref):
        for p in range(RET_HEADS // 2):
            kp = k_ref[:, p * LANES:(p + 1) * LANES].astype(F32) * K_SCALE
            for a in range(2):
                h = 2 * p + a
                kh = jnp.where(_half_mask(kp.shape, a), kp, 0.0)
                vh = v_ref[:, h * RET_DV:(h + 1) * RET_DV]
                for d in range(2):
                    kw = (kh * c_ref[d * RET_HEADS + h, 0]).astype(BF16)
                    s_ref[d * RET_HEADS + h] = _dot(kw, vh, TN)

    return pl.pallas_call(
        body, name="ret_ctx_state", grid=(1,),
        in_specs=[pl.BlockSpec((Lc, 512), lambda i: (cb, C_RK // 512)),
                  pl.BlockSpec((Lc, 1024), lambda i: (cb, C_RV // 1024)),
                  pl.BlockSpec((2 * RET_HEADS, 2, Lc, LANES), lambda i: (0, 0, 0, 0))],
        out_specs=pl.BlockSpec((2 * RET_HEADS, LANES, RET_DV), lambda i: (0, 0, 0)),
        out_shape=jax.ShapeDtypeStruct((2 * RET_HEADS, LANES, RET_DV), F32),
        compiler_params=_cparams(("arbitrary",)),
    )(P, P, ctab)


def _ret_fwd(P, rope, tabs, s0, L, comm=()):
    n = L // CHUNK

    def body(qf, kf, vf, rf, qb, kb, vb, rb, t_ref, s0_ref, of_ref, ob_ref, stf_ref, stb_ref, st):
        s = pl.program_id(0)

        @pl.when(s == 0)
        def _():
            st[...] = s0_ref[...]

        units = []
        for d, (q_ref, k_ref, v_ref, r_ref, o_ref, so_ref) in enumerate(
                ((qf, kf, vf, rf, of_ref, stf_ref), (qb, kb, vb, rb, ob_ref, stb_ref))):
            cos, sin = r_ref[0], r_ref[1]
            for p in range(RET_HEADS // 2):
                qp = _rope(q_ref[:, p * LANES:(p + 1) * LANES].astype(F32), cos, sin, 32)
                kp = _rope(k_ref[:, p * LANES:(p + 1) * LANES].astype(F32), cos, sin, 32) * K_SCALE
                for a in range(2):
                    h = 2 * p + a
                    hm = _half_mask(qp.shape, a)
                    units.append(dict(r=d * RET_HEADS + h, h=h, a=a, o_ref=o_ref, so_ref=so_ref, v_ref=v_ref,
                                      qh=jnp.where(hm, qp, 0.0), kh=jnp.where(hm, kp, 0.0)))
        for u in units:
            u["sc"] = _dot(u["qh"].astype(BF16), u["kh"].astype(BF16), NT)
        for u in units:
            r, h = u["r"], u["h"]
            sp = st[r]
            u["so_ref"][0, h] = sp[u["a"] * RET_DK:(u["a"] + 1) * RET_DK, :]
            vh = u["v_ref"][:, h * RET_DV:(h + 1) * RET_DV]
            o = _dot((u["sc"] * t_ref[r, 0]).astype(BF16), vh, NN)
            o += _dot((u["qh"] * t_ref[r, 2]).astype(BF16), sp.astype(BF16), NN)
            u["o_ref"][:, h * RET_DV:(h + 1) * RET_DV] = o
        for u in units:
            r, h = u["r"], u["h"]
            vh = u["v_ref"][:, h * RET_DV:(h + 1) * RET_DV]
            st[r] = t_ref[r, 6] * st[r] + _dot((u["kh"] * t_ref[r, 4]).astype(BF16), vh, TN)

    fw = lambda s: s
    bw = lambda s: n - 1 - s

    def specs(cm):
        return [pl.BlockSpec((CHUNK, 512), lambda s: (cm(s), C_RQ // 512)),
                pl.BlockSpec((CHUNK, 512), lambda s: (cm(s), C_RK // 512)),
                pl.BlockSpec((CHUNK, 1024), lambda s: (cm(s), C_RV // 1024)),
                pl.BlockSpec((2, CHUNK, LANES), lambda s: (0, cm(s), 0))]

    full = lambda shp: pl.BlockSpec(shp, lambda s: (0,) * len(shp))
    return _call(
        body, "ret_fwd", (n,),
        specs(fw) + specs(bw) + [full((2 * RET_HEADS, N_TAB, CHUNK, CHUNK)), full((2 * RET_HEADS, LANES, RET_DV))],
        [pl.BlockSpec((CHUNK, 1024), lambda s: (fw(s), 0)),
         pl.BlockSpec((CHUNK, 1024), lambda s: (bw(s), 0)),
         pl.BlockSpec((1, RET_HEADS, RET_DK, RET_DV), lambda s: (fw(s), 0, 0, 0)),
         pl.BlockSpec((1, RET_HEADS, RET_DK, RET_DV), lambda s: (bw(s), 0, 0, 0))],
        [jax.ShapeDtypeStruct((L, 1024), F32), jax.ShapeDtypeStruct((L, 1024), F32),
         jax.ShapeDtypeStruct((n, RET_HEADS, RET_DK, RET_DV), F32),
         jax.ShapeDtypeStruct((n, RET_HEADS, RET_DK, RET_DV), F32)],
        [pltpu.VMEM((2 * RET_HEADS, LANES, RET_DV), F32)],
        ("arbitrary",), (P, P, P, rope, P, P, P, rope, tabs, s0), comm)


def _ret_finish_fwd(of, ob, P, L):
    tr = _pick(L, (256, 128))

    def body(f_ref, b_ref, g_ref, y_ref):
        for h in range(RET_HEADS):
            sl = slice(h * RET_DV, (h + 1) * RET_DV)
            n, _ = _norm_rows(f_ref[:, sl] + b_ref[:, sl])
            g = g_ref[:, sl].astype(F32)
            y_ref[:, sl] = (n * (g * _sigmoid(g))).astype(y_ref.dtype)

    row = pl.BlockSpec((tr, 1024), lambda i: (i, 0))
    return pl.pallas_call(
        body, name="ret_finish_fwd", grid=(L // tr,),
        in_specs=[row, row, pl.BlockSpec((tr, 1024), lambda i: (i, C_RG // 1024))],
        out_specs=row, out_shape=jax.ShapeDtypeStruct((L, 2048), BF16),
        compiler_params=_cparams(("parallel",)),
    )(of, ob, P)


def _ret_finish_bwd(of, ob, P, dY, L):
    tr = _pick(L, (256, 128))

    def body(f_ref, b_ref, g_ref, dy_ref, do_ref, dg_ref):
        for h in range(RET_HEADS):
            sl = slice(h * RET_DV, (h + 1) * RET_DV)
            n, r = _norm_rows(f_ref[:, sl] + b_ref[:, sl])
            g = g_ref[:, sl].astype(F32)
            sg = _sigmoid(g)
            dy = dy_ref[:, sl].astype(F32)
            dg_ref[:, sl] = (dy * n * (sg * (1.0 + g * (1.0 - sg)))).astype(dg_ref.dtype)
            dn = dy * (g * sg)
            do_ref[:, sl] = (r * (dn - n * jnp.mean(dn * n, axis=-1, keepdims=True))).astype(do_ref.dtype)

    row = pl.BlockSpec((tr, 1024), lambda i: (i, 0))
    return pl.pallas_call(
        body, name="ret_finish_bwd", grid=(L // tr,),
        in_specs=[row, row, pl.BlockSpec((tr, 1024), lambda i: (i, C_RG // 1024)), row],
        out_specs=[row, row],
        out_shape=[jax.ShapeDtypeStruct((L, 1024), BF16), jax.ShapeDtypeStruct((L, 1024), BF16)],
        compiler_params=_cparams(("parallel",)),
    )(of, ob, P, dY)


def _ret_bwd(P, rope, tabs, stf, stb, dO, L, comm=()):
    n = L // CHUNK

    def body(qf, kf, vf, rf, gf, sf, qb, kb, vb, rb, gb, sb, t_ref,
             dqf, dkf, dvf, dqb, dkb, dvb, ds0_ref, dlg_ref, ds):
        s = pl.program_id(0)

        @pl.when(s == 0)
        def _():
            ds[...] = jnp.zeros_like(ds)
            dlg_ref[...] = jnp.zeros_like(dlg_ref)

        units, pairs = [], []
        for d, (q_ref, k_ref, v_ref, r_ref, g_ref, s_ref, dq_ref, dk_ref, dv_ref) in enumerate(
                ((qf, kf, vf, rf, gf, sf, dqf, dkf, dvf), (qb, kb, vb, rb, gb, sb, dqb, dkb, dvb))):
            cos, sin = r_ref[0], r_ref[1]
            for p in range(RET_HEADS // 2):
                qp = _rope(q_ref[:, p * LANES:(p + 1) * LANES].astype(F32), cos, sin, 32)
                kp = _rope(k_ref[:, p * LANES:(p + 1) * LANES].astype(F32), cos, sin, 32) * K_SCALE
                pair = dict(p=p, cos=cos, sin=sin, dq_ref=dq_ref, dk_ref=dk_ref, us=[])
                pairs.append(pair)
                for a in range(2):
                    h = 2 * p + a
                    r = d * RET_HEADS + h
                    hm = _half_mask(qp.shape, a)
                    zero = jnp.zeros((RET_DK, RET_DV), F32)
                    sp = s_ref[0, h]
                    u = dict(r=r, h=h, dv_ref=dv_ref, qh=jnp.where(hm, qp, 0.0), kh=jnp.where(hm, kp, 0.0),
                             vh=v_ref[:, h * RET_DV:(h + 1) * RET_DV], gh=g_ref[:, h * RET_DV:(h + 1) * RET_DV],
                             sp=jnp.concatenate([sp, zero] if a == 0 else [zero, sp], axis=0),
                             dsn=ds[r])
                    u["qhb"], u["khb"] = u["qh"].astype(BF16), u["kh"].astype(BF16)
                    units.append(u)
                    pair["us"].append(u)
        for u in units:
            u["am"] = _dot(u["qhb"], u["khb"], NT)
            u["dar"] = _dot(u["gh"], u["vh"], NT)
            u["xq"] = _dot(u["gh"], u["sp"].astype(BF16), NT)
            u["yk"] = _dot(u["vh"], u["dsn"].astype(BF16), NT)
        for u in units:
            r = u["r"]
            dm = t_ref[r, 0]
            u["da"] = (u["dar"] * dm).astype(BF16)
            u["amd"] = (u["am"] * dm).astype(BF16)
            part = (jnp.sum(u["am"] * u["dar"] * t_ref[r, 1]) + jnp.sum(u["qh"] * t_ref[r, 3] * u["xq"])
                    + jnp.sum(u["kh"] * t_ref[r, 5] * u["yk"])
                    + float(CHUNK) * jnp.sum(t_ref[r, 6] * u["dsn"] * u["sp"]))
            dlg_ref[r:r + 1, :] += jnp.zeros((1, LANES), F32) + part
        for u in units:
            r, h = u["r"], u["h"]
            u["dq"] = _dot(u["da"], u["khb"], NN) + u["xq"] * t_ref[r, 2]
            u["dk"] = _dot(u["da"], u["qhb"], TN) + u["yk"] * t_ref[r, 4]
            u["dv_ref"][:, h * RET_DV:(h + 1) * RET_DV] = (
                _dot(u["amd"], u["gh"], TN) + _dot((u["kh"] * t_ref[r, 4]).astype(BF16), u["dsn"].astype(BF16), NN))
            ds[r] = t_ref[r, 6] * u["dsn"] + _dot((u["qh"] * t_ref[r, 2]).astype(BF16), u["gh"], TN)
        for pair in pairs:
            sl = slice(pair["p"] * LANES, (pair["p"] + 1) * LANES)
            u0, u1 = pair["us"]
            pair["dq_ref"][:, sl] = _rope_t(u0["dq"] + u1["dq"], pair["cos"], pair["sin"], 32)
            pair["dk_ref"][:, sl] = _rope_t((u0["dk"] + u1["dk"]) * K_SCALE, pair["cos"], pair["sin"], 32)

        @pl.when(s == n - 1)
        def _():
            ds0_ref[...] = ds[...]

    fw = lambda s: n - 1 - s
    bw = lambda s: s

    def specs(cm):
        return [pl.BlockSpec((CHUNK, 512), lambda s: (cm(s), C_RQ // 512)),
                pl.BlockSpec((CHUNK, 512), lambda s: (cm(s), C_RK // 512)),
                pl.BlockSpec((CHUNK, 1024), lambda s: (cm(s), C_RV // 1024)),
                pl.BlockSpec((2, CHUNK, LANES), lambda s: (0, cm(s), 0)),
                pl.BlockSpec((CHUNK, 1024), lambda s: (cm(s), 0)),
                pl.BlockSpec((1, RET_HEADS, RET_DK, RET_DV), lambda s: (cm(s), 0, 0, 0))]

    def ospecs(cm):
        return [pl.BlockSpec((CHUNK, 512), lambda s: (cm(s), 0)), pl.BlockSpec((CHUNK, 512), lambda s: (cm(s), 0)),
                pl.BlockSpec((CHUNK, 1024), lambda s: (cm(s), 0))]

    oshape = [jax.ShapeDtypeStruct((L, 512), F32), jax.ShapeDtypeStruct((L, 512), F32),
              jax.ShapeDtypeStruct((L, 1024), F32)]
    full = lambda shp: pl.BlockSpec(shp, lambda s: (0,) * len(shp))
    return _call(
        body, "ret_bwd", (n,),
        specs(fw) + specs(bw) + [full((2 * RET_HEADS, N_TAB, CHUNK, CHUNK))],
        ospecs(fw) + ospecs(bw) + [full((2 * RET_HEADS, LANES, RET_DV)), full((2 * RET_HEADS, LANES))],
        oshape + oshape + [jax.ShapeDtypeStruct((2 * RET_HEADS, LANES, RET_DV), F32),
                           jax.ShapeDtypeStruct((2 * RET_HEADS, LANES), F32)],
        [pltpu.VMEM((2 * RET_HEADS, LANES, RET_DV), F32)],
        ("arbitrary",), (P, P, P, rope, dO, stf, P, P, P, rope, dO, stb, tabs), comm)


def _ret_ctx_bwd(P, ctab, ds0, dlg, rdb, L, Lc):
    cb = L // Lc

    def body(k_ref, v_ref, c_ref, ds_ref, dlg_ref, rd_ref, dk_ref, dv_ref, drd_ref):
        for p in range(RET_HEADS // 2):
            kp = k_ref[:, p * LANES:(p + 1) * LANES].astype(F32) * K_SCALE
            dkp = jnp.zeros((Lc, LANES), F32)
            for a in range(2):
                h = 2 * p + a
                kh = jnp.where(_half_mask(kp.shape, a), kp, 0.0)
                vh = v_ref[:, h * RET_DV:(h + 1) * RET_DV]
                dvh = jnp.zeros((Lc, RET_DV), F32)
                for d in range(2):
                    r = d * RET_HEADS + h
                    dsb = ds_ref[r].astype(BF16)
                    cw, cwc = c_ref[r, 0], c_ref[r, 1]
                    y = _dot(vh, dsb, NT)
                    dkp += y * cw
                    dvh += _dot((kh * cw).astype(BF16), dsb, NN)
                    lg = -jnp.exp(rd_ref[r])
                    drd_ref[r:r + 1, :] = (dlg_ref[r:r + 1, :] + jnp.sum(kh * cwc * y)) * lg
                dv_ref[:, h * RET_DV:(h + 1) * RET_DV] = dvh
            dk_ref[:, p * LANES:(p + 1) * LANES] = dkp * K_SCALE

    full = lambda shp: pl.BlockSpec(shp, lambda i: (0,) * len(shp))
    return pl.pallas_call(
        body, name="ret_ctx_bwd", grid=(1,),
        in_specs=[pl.BlockSpec((Lc, 512), lambda i: (cb, C_RK // 512)),
                  pl.BlockSpec((Lc, 1024), lambda i: (cb, C_RV // 1024)),
                  full((2 * RET_HEADS, 2, Lc, LANES)), full((2 * RET_HEADS, LANES, RET_DV)),
                  full((2 * RET_HEADS, LANES)), full((2 * RET_HEADS, 1, LANES))],
        out_specs=[full((Lc, 512)), full((Lc, 1024)), full((2 * RET_HEADS, LANES))],
        out_shape=[jax.ShapeDtypeStruct((Lc, 512), F32), jax.ShapeDtypeStruct((Lc, 1024), F32),
                   jax.ShapeDtypeStruct((2 * RET_HEADS, LANES), F32)],
        compiler_params=_cparams(("arbitrary",)),
    )(P, P, ctab, ds0, dlg, rdb)


BLK = 128
N_LOC = 3 * BLK


def _att_inputs(P, rope, L, Lc):
    n = L // BLK
    cb = L // Lc
    prev = lambda i: jnp.maximum(i - 1, 0)
    nxt = lambda i: jnp.minimum(i + 1, n - 1)
    specs = [pl.BlockSpec((BLK, 1024), lambda i: (i, C_AQ // 1024))]
    args = [P]
    for col in (C_AK // 256, C_AV // 256):
        for rm in (prev, lambda i: i, nxt):
            specs.append(pl.BlockSpec((BLK, 256), functools.partial(lambda i, rm, col: (rm(i), col), rm=rm, col=col)))
            args.append(P)
        specs.append(pl.BlockSpec((Lc, 256), functools.partial(lambda i, col: (cb, col), col=col)))
        args.append(P)
    for rm in (prev, lambda i: i, nxt):
        specs.append(pl.BlockSpec((2, BLK, LANES), functools.partial(lambda i, rm: (0, rm(i), 0), rm=rm)))
        args.append(rope)
    return specs, args


def _att_prep(i, n, refs, Lc):
    q_ref, kp_ref, kc_ref, kn_ref, kx_ref, vp_ref, vc_ref, vn_ref, vx_ref, rp_ref, rc_ref, rn_ref = refs
    cos = jnp.concatenate([rp_ref[0], rc_ref[0], rn_ref[0]], axis=0)
    sin = jnp.concatenate([rp_ref[1], rc_ref[1], rn_ref[1]], axis=0)
    kd, vd = [], []
    for t in range(ATT_KV // 2):
        sl = slice(t * LANES, (t + 1) * LANES)
        kl = jnp.concatenate([kp_ref[:, sl], kc_ref[:, sl], kn_ref[:, sl]], axis=0).astype(F32)
        kl = _rope(kl, cos, sin, 16)
        ka = jnp.concatenate([kl, kx_ref[:, sl].astype(F32)], axis=0)
        va = jnp.concatenate([vp_ref[:, sl], vc_ref[:, sl], vn_ref[:, sl], vx_ref[:, sl]], axis=0).astype(F32)
        kr, vr = pltpu.roll(ka, 64, 1), pltpu.roll(va, 64, 1)
        for b in range(2):
            hm = _half_mask(ka.shape, b)
            kd.append(jnp.where(hm, ka, kr).astype(BF16))
            vd.append(jnp.where(hm, va, vr).astype(BF16))
    nk = N_LOC + Lc
    rr = lax.broadcasted_iota(jnp.int32, (BLK, nk), 0)
    ss = lax.broadcasted_iota(jnp.int32, (BLK, nk), 1)
    lo = jnp.where(i == 0, BLK, 0)
    hi = jnp.where(i == n - 1, 2 * BLK, N_LOC)
    valid = (ss >= N_LOC) | ((ss >= rr) & (ss <= rr + 2 * BLK) & (ss >= lo) & (ss < hi))
    bias = jnp.where(valid, 0.0, NEG)
    return kd, vd, jnp.concatenate([bias] * 4, axis=0), rc_ref[0], rc_ref[1]


LOG2E = 1.4426950408889634
LN2 = 0.6931471805599453
Q_SCALE = A_SCALE * LOG2E


def _stack4(ref, g, f=None):
    parts = []
    for jp in range(2):
        t = ref[:, (2 * g + jp) * LANES:(2 * g + jp + 1) * LANES].astype(F32)
        if f is not None:
            t = f(t)
        for a in range(2):
            parts.append(jnp.where(_half_mask(t.shape, a), t, 0.0))
    return jnp.concatenate(parts, axis=0)


def _unstack4(x4, jp):
    r0 = 2 * jp * BLK
    lo = x4[r0:r0 + BLK]
    hi = x4[r0 + BLK:r0 + 2 * BLK]
    return jnp.where(_half_mask(lo.shape, 0), lo, hi)


def _softmax_parts(s, bias4, sink_ref, g):
    sink_col = LOG2E * jnp.concatenate(
        [jnp.zeros((BLK, 1), F32) + sink_ref[4 * g + r:4 * g + r + 1, 0:1] for r in range(4)], axis=0)
    s = s + bias4
    m = jnp.maximum(jnp.max(s, axis=-1, keepdims=True), sink_col)
    e = jnp.exp2(s - m)
    es = jnp.exp2(sink_col - m)
    return e, es, jnp.sum(e, axis=-1, keepdims=True) + es


def _att_fwd(P, rope, sinkb, Y, L, Lc, comm=()):
    n = L // BLK
    specs, args = _att_inputs(P, rope, L, Lc)

    def body(*refs):
        sink_ref, o_ref = refs[12], refs[14]
        i = pl.program_id(0)
        kd, vd, bias4, cq, sq = _att_prep(i, n, refs[:12], Lc)
        def raw_scores(g):
            q4 = _stack4(refs[0], g, lambda t: _rope(t, cq, sq, 16) * Q_SCALE).astype(BF16)
            return _dot(q4, kd[g], NT)

        s_next = raw_scores(0)
        for g in range(ATT_KV):
            s = s_next
            if g + 1 < ATT_KV:
                s_next = raw_scores(g + 1)
            e, _, l = _softmax_parts(s, bias4, sink_ref, g)
            o4 = _dot(e.astype(BF16), vd[g], NN) * (1.0 / l)
            for jp in range(2):
                c0 = (2 * g + jp) * LANES
                o_ref[:, c0:c0 + LANES] = _unstack4(o4, jp).astype(o_ref.dtype)

    return _call(
        body, "att_fwd", (n,),
        specs + [pl.BlockSpec((ATT_HEADS, LANES), lambda i: (0, 0)), pl.BlockSpec(memory_space=pl.ANY)],
        [pl.BlockSpec((BLK, 1024), lambda i: (i, 1))], [jax.ShapeDtypeStruct((L, 2048), BF16)], [],
        ("parallel",), (*args, sinkb, Y), comm, aliases={13: 0})


def _att_bwd(P, rope, sinkb, Y, dY, L, Lc, comm=()):
    n = L // BLK
    specs, args = _att_inputs(P, rope, L, Lc)
    nk = N_LOC + Lc

    def body(*refs):
        sink_ref, y_ref, dy_ref = refs[12], refs[13], refs[14]
        dq_ref, dkl_ref, dvl_ref, dkx_ref, dvx_ref, dsk_ref = refs[15:21]
        i = pl.program_id(0)

        @pl.when(i == 0)
        def _():
            dkx_ref[...] = jnp.zeros_like(dkx_ref)
            dvx_ref[...] = jnp.zeros_like(dvx_ref)
            dsk_ref[...] = jnp.zeros_like(dsk_ref)

        kd, vd, bias4, cq, sq = _att_prep(i, n, refs[:12], Lc)
        def first_matmuls(g):
            q4 = _stack4(refs[0], g, lambda x: _rope(x, cq, sq, 16) * Q_SCALE).astype(BF16)
            do4 = _stack4(dy_ref, g)
            delta = jnp.sum(do4 * _stack4(y_ref, g), axis=-1, keepdims=True)
            do4b = do4.astype(BF16)
            return q4, do4b, delta, _dot(q4, kd[g], NT), _dot(do4b, vd[g], NT)

        nxt = first_matmuls(0)
        for t in range(ATT_KV // 2):
            dk_halves, dv_halves = [], []
            for b in range(2):
                g = 2 * t + b
                q4, do4b, delta, s, dpm = nxt
                if g + 1 < ATT_KV:
                    nxt = first_matmuls(g + 1)
                e, es, l = _softmax_parts(s, bias4, sink_ref, g)
                inv = 1.0 / l
                p = e * inv
                dsc = (p * (dpm - delta)).astype(BF16)
                dsr = es * inv * delta
                for r in range(4):
                    h = 4 * g + r
                    dsk_ref[h:h + 1, :] += jnp.zeros((1, LANES), F32) - jnp.sum(dsr[r * BLK:(r + 1) * BLK])
                dq4 = _dot(dsc, kd[g], NN) * A_SCALE
                for jp in range(2):
                    c0 = (2 * g + jp) * LANES
                    dq_ref[:, c0:c0 + LANES] = _rope_t(_unstack4(dq4, jp), cq, sq, 16)
                dkd = _dot(q4, dsc, TN) * LN2
                dvd = _dot(do4b, p.astype(BF16), TN)
                dk_halves.append(dkd[:ATT_DH] + dkd[ATT_DH:])
                dv_halves.append(dvd[:ATT_DH] + dvd[ATT_DH:])
            dk_t = jnp.concatenate(dk_halves, axis=0).T
            dv_t = jnp.concatenate(dv_halves, axis=0).T
            sl = slice(t * LANES, (t + 1) * LANES)
            dkl_ref[0, :, sl] = dk_t[:N_LOC]
            dvl_ref[0, :, sl] = dv_t[:N_LOC]
            dkx_ref[:, sl] += dk_t[N_LOC:]
            dvx_ref[:, sl] += dv_t[N_LOC:]

    row = pl.BlockSpec((BLK, 1024), lambda i: (i, 0))
    loc = pl.BlockSpec((1, N_LOC, 256), lambda i: (i, 0, 0))
    cx = pl.BlockSpec((Lc, 256), lambda i: (0, 0))
    return _call(
        body, "att_bwd", (n,),
        specs + [pl.BlockSpec((ATT_HEADS, LANES), lambda i: (0, 0))] + [pl.BlockSpec((BLK, 1024), lambda i: (i, 1))] * 2,
        [row, loc, loc, cx, cx, pl.BlockSpec((ATT_HEADS, LANES), lambda i: (0, 0))],
        [jax.ShapeDtypeStruct((L, 1024), F32), jax.ShapeDtypeStruct((n, N_LOC, 256), F32),
         jax.ShapeDtypeStruct((n, N_LOC, 256), F32), jax.ShapeDtypeStruct((Lc, 256), F32),
         jax.ShapeDtypeStruct((Lc, 256), F32), jax.ShapeDtypeStruct((ATT_HEADS, LANES), F32)], [],
        ("arbitrary",), (*args, sinkb, Y, dY), comm)


def _assemble_dp(L, Lc, dqf, dqb, dkf, dkb, dvf, dvb, drg, daq, dkl, dvl, rope_att, dck, dcv, dkx, dvx):
    n = L // BLK
    nc = Lc // BLK

    def body(dqf_r, dqb_r, dkf_r, dkb_r, dvf_r, dvb_r, drg_r, daq_r, kl0, kl1, kl2, vl0, vl1, vl2, rp_r,
             dck_r, dcv_r, dkx_r, dvx_r, o_ref):
        i = pl.program_id(0)

        @pl.when(i < n)
        def _():
            o_ref[:, C_RQ:C_RK] = (dqf_r[...] + dqb_r[...]).astype(o_ref.dtype)
            o_ref[:, C_RK:C_RV] = (dkf_r[...] + dkb_r[...]).astype(o_ref.dtype)
            o_ref[:, C_RV:C_RG] = (dvf_r[...] + dvb_r[...]).astype(o_ref.dtype)
            o_ref[:, C_RG:C_AQ] = drg_r[...].astype(o_ref.dtype)
            o_ref[:, C_AQ:C_AK] = daq_r[...].astype(o_ref.dtype)
            w0 = jnp.where(i > 0, 1.0, 0.0)
            w2 = jnp.where(i < n - 1, 1.0, 0.0)
            dk = kl0[0, 2 * BLK:3 * BLK, :] * w0 + kl1[0, BLK:2 * BLK, :] + kl2[0, 0:BLK, :] * w2
            dv = vl0[0, 2 * BLK:3 * BLK, :] * w0 + vl1[0, BLK:2 * BLK, :] + vl2[0, 0:BLK, :] * w2
            for t in range(ATT_KV // 2):
                sl = slice(t * LANES, (t + 1) * LANES)
                o_ref[:, C_AK + t * LANES:C_AK + (t + 1) * LANES] = _rope_t(
                    dk[:, sl], rp_r[0], rp_r[1], 16).astype(o_ref.dtype)
            o_ref[:, C_AV:D_PROJ] = dv.astype(o_ref.dtype)

        @pl.when(i >= n)
        def _():
            o_ref[:, C_RQ:C_RK] = jnp.zeros((BLK, C_RK - C_RQ), o_ref.dtype)
            o_ref[:, C_RK:C_RV] = dck_r[...].astype(o_ref.dtype)
            o_ref[:, C_RV:C_RG] = dcv_r[...].astype(o_ref.dtype)
            o_ref[:, C_RG:C_AK] = jnp.zeros((BLK, C_AK - C_RG), o_ref.dtype)
            o_ref[:, C_AK:C_AV] = dkx_r[...].astype(o_ref.dtype)
            o_ref[:, C_AV:D_PROJ] = dvx_r[...].astype(o_ref.dtype)

    xm = lambda i: jnp.minimum(i, n - 1)
    cm = lambda i: jnp.clip(i - n, 0, nc - 1)
    r512 = pl.BlockSpec((BLK, 512), lambda i: (xm(i), 0))
    r1024 = pl.BlockSpec((BLK, 1024), lambda i: (xm(i), 0))
    part = lambda off: pl.BlockSpec((1, N_LOC, 256), lambda i: (jnp.clip(xm(i) + off, 0, n - 1), 0, 0))
    return pl.pallas_call(
        body, name="assemble_dp", grid=(n + nc,),
        in_specs=[r512, r512, r512, r512, r1024, r1024, r1024, r1024,
                  part(-1), part(0), part(1), part(-1), part(0), part(1),
                  pl.BlockSpec((2, BLK, LANES), lambda i: (0, xm(i), 0)),
                  pl.BlockSpec((BLK, 512), lambda i: (cm(i), 0)), pl.BlockSpec((BLK, 1024), lambda i: (cm(i), 0)),
                  pl.BlockSpec((BLK, 256), lambda i: (cm(i), 0)), pl.BlockSpec((BLK, 256), lambda i: (cm(i), 0))],
        out_specs=pl.BlockSpec((BLK, D_PROJ), lambda i: (i, 0)),
        out_shape=jax.ShapeDtypeStruct((L + Lc, D_PROJ), BF16),
        compiler_params=_cparams(("parallel",)),
    )(dqf, dqb, dkf, dkb, dvf, dvb, drg, daq, dkl, dkl, dkl, dvl, dvl, dvl, rope_att, dck, dcv, dkx, dvx)


def _adam_math(w, g, m, v):
    m = ADAM_B1 * m + (1.0 - ADAM_B1) * g
    v = ADAM_B2 * v + (1.0 - ADAM_B2) * (g * g)
    m_hat = m / (1.0 - ADAM_B1 ** ADAM_STEP)
    v_hat = v / (1.0 - ADAM_B2 ** ADAM_STEP)
    delta = -ADAM_LR * (m_hat / (jnp.sqrt(v_hat) + ADAM_EPS) + ADAM_WD * w)
    return delta, m, v


def _adam(name, w, m, v, g=None, parts=None):
    R, C = w.shape
    tr = _pick(R, (256, 128, 64, 32, 16, 8))
    summed = parts is not None
    n_parts = parts.shape[0] if summed else 0

    def body(w_ref, m_ref, v_ref, g_ref, go_ref, d_ref, mo_ref, vo_ref):
        if summed:
            gv = g_ref[0].astype(F32)
            for j in range(1, n_parts):
                gv = gv + g_ref[j].astype(F32)
        else:
            gv = g_ref[...]
        d, mn, vn = _adam_math(w_ref[...], gv, m_ref[...], v_ref[...])
        go_ref[...] = gv
        d_ref[...] = d
        mo_ref[...] = mn
        vo_ref[...] = vn

    row = pl.BlockSpec((tr, C), lambda i: (i, 0))
    gspec = pl.BlockSpec((n_parts, tr, C), lambda i: (0, i, 0)) if summed else row
    return pl.pallas_call(
        body, name=name, grid=(R // tr,),
        in_specs=[row, row, row, gspec], out_specs=[row] * 4,
        out_shape=[jax.ShapeDtypeStruct((R, C), F32)] * 4,
        compiler_params=_cparams(("parallel",)),
    )(w, m, v, parts if summed else g)


def _rows_full(g):
    _, R, D = g.shape
    return g.reshape(N_DEV * R, D)


def _rows_slots(g):
    N, D = g.shape
    return g.reshape(N_DEV, N // N_DEV, D)


def _pad_rows(a, rows):
    return jnp.concatenate([a, jnp.zeros((rows - a.shape[0],) + a.shape[1:], a.dtype)], axis=0)


def kernel(x, c, ctx, c_ctx, w_mod, b_mod, norm_mix, norm_ffn, w_in, ret_decay, attn_sink, w_out, w_gate, w_up, w_down, norm_final, loss_target, m_c_ctx, m_w_mod, m_b_mod, m_norm_mix, m_norm_ffn, m_w_in, m_ret_decay, m_attn_sink, m_w_out, m_w_gate, m_w_up, m_w_down, m_norm_final, v_c_ctx, v_w_mod, v_b_mod, v_norm_mix, v_norm_ffn, v_w_in, v_ret_decay, v_attn_sink, v_w_out, v_w_gate, v_w_up, v_w_down, v_norm_final):
    L, D = x.shape[1], x.shape[2]
    Lc = ctx.shape[1]
    DF = w_gate.shape[2] * N_DEV
    C6 = w_mod.shape[2]
    me = _my_id()
    xs, cx, tgt = x[0], ctx[0], loss_target[0]

    ag_in = ("ag2", w_in[0].T.astype(BF16))
    ag_out, ag_gate = ("ag2", w_out[0].astype(BF16)), ("ag2", w_gate[0].T.astype(BF16))
    ag_up, ag_down = ("ag2", w_up[0].T.astype(BF16)), ("ag2", w_down[0].astype(BF16))

    cs = _allgather(c, "ag_c")[:, 0, :]
    s_in = _pad_rows(jnp.concatenate([cs, c_ctx[None, :]], axis=0), 16)
    b_l = lax.dynamic_slice_in_dim(b_mod, me * C6, C6, axis=1)
    mod_parts = _allgather(_mod_fwd(s_in, w_mod[0], b_l), "ag_mod")
    mod = _pad_rows(lax.dynamic_index_in_dim(mod_parts, me, axis=1, keepdims=False).reshape(6, D), 8)
    modc = _pad_rows(mod_parts[:, N_DEV, :].reshape(6, D), 8)
    mix_mod, ffn_mod = mod, jnp.roll(mod, -3, axis=0)
    gt_m, gt_f = mod[2:3], mod[5:6]

    rope_ret, rope_att = _rope_tables(L)
    rdb = jnp.broadcast_to(ret_decay[0].reshape(2 * RET_HEADS, 1, 1), (2 * RET_HEADS, 1, LANES))
    sinkb = jnp.broadcast_to(attn_sink[0].reshape(ATT_HEADS, 1), (ATT_HEADS, LANES))

    tm = _pick(L + Lc, (1408, 768, 512, 384, 256, 128))
    tmx = _pick(L, (1024, 512, 256, 128))

    (H,), (g_in,) = _modulate_fwd("mod_mix_fwd", xs, cx, norm_mix, mix_mod, modc, comm=[ag_in])
    W_inT = _rows_full(g_in)
    ident = lambda a, e: a
    tP, tD, tF = _pick(D_PROJ, (1152, 768, 512)), _pick(D, (2048, 1024, 512)), _pick(DF, (512, 256, 128))
    (P,), (g_gate,) = _matmul("mm_in", [(H, W_inT, 0)], 1, L + Lc, D_PROJ, D, "nt", (tm, tP, D), [], [BF16], ident,
                              comm=[ag_gate])
    W_gateT = _rows_full(g_gate)
    tabs, ctab = _ret_tables(rdb, Lc)
    s0 = _ret_ctx_state(P, ctab, L, Lc)
    (o_f, o_b, st_f, st_b), (g_out,) = _ret_fwd(P, rope_ret, tabs, s0, L, comm=[ag_out])
    W_out = _rows_full(g_out)
    Y_half = _ret_finish_fwd(o_f, o_b, P, L)
    (Y,), (g_up,) = _att_fwd(P, rope_att, sinkb, Y_half, L, Lc, comm=[ag_up])
    W_upT = _rows_full(g_up)
    KO = Y.shape[1]
    f_mix = _matmul("mm_out", [(Y, W_out, 0)], 1, L, D, KO, "nn", (tmx, tD, KO), [], [BF16], ident)[0]

    x1, H2 = _residual_modulate_fwd("mod_ffn_fwd", xs, f_mix, gt_m, norm_ffn, ffn_mod)

    def swiglu_epi(a, e):
        return [a[0], a[1], a[0] * _sigmoid(a[0]) * a[1]]

    tm2 = tmx
    (ga, up, hmid), (g_down,) = _matmul("mm_gate_up", [(H2, W_gateT, 0), (H2, W_upT, 1)], 2, L, DF, D, "nt",
                                        (tm2, tF, D), [], [BF16, BF16, BF16], swiglu_epi, comm=[ag_down])
    W_down = _rows_full(g_down)
    f_ffn = _matmul("mm_down", [(hmid, W_down, 0)], 1, L, D, DF, "nn", (tm2, tD, tF), [], [BF16], ident)[0]

    dx2, dFf, sums_l = _loss_head(x1, tgt, norm_final.reshape(1, D), f_ffn, gt_f)

    def dswiglu_epi(a, e):
        av, uv = e[0].astype(F32), e[1].astype(F32)
        sg = _sigmoid(av)
        return [a[0] * uv * (sg * (1.0 + av * (1.0 - sg))), a[0] * (av * sg)]

    dga, dup = _matmul("mm_d_down", [(dFf, W_down, 0)], 1, L, DF, D, "nt", (tm2, tF, D),
                       [(ga, "mn"), (up, "mn")], [BF16, BF16], dswiglu_epi)
    tkt = _pick(L, (512, 256, 128))
    dW_down = _matmul("mm_gw_down", [(hmid, dFf, 0)], 1, DF, D, L, "tn",
                      (_pick(DF, (1408, 512, 256, 128)), tD, tkt), [], [BF16], ident)[0]
    (dW_gateT, dW_upT), (p_down,) = _matmul("mm_gw_gate_up", [(dga, H2, 0), (dup, H2, 1)], 2, DF, D, L, "tn",
                                            (tF, tD, tkt), [], [BF16, BF16], ident,
                                            comm=[("a2a", _rows_slots(dW_down))])
    (dH2,), (p_gate,) = _matmul("mm_d_gate_up", [(dga, W_gateT, 0), (dup, W_upT, 0)], 1, L, D, DF, "nn",
                                (tm2, tD, tF), [], [BF16], ident, comm=[("a2a", _rows_slots(dW_gateT))])
    dx1, dFm, sums_f = _modulate_bwd("mod_ffn_bwd", x1, None, dH2, norm_ffn, ffn_mod, None, dx2, f_mix, gt_m)

    tO = _pick(KO, (2048, 1024, 512))
    dY = _matmul("mm_d_out", [(dFm, W_out, 0)], 1, L, KO, D, "nt", (tmx, tO, D), [], [BF16], ident)[0]
    dW_out = _matmul("mm_gw_out", [(Y, dFm, 0)], 1, KO, D, L, "tn", (_pick(KO, (1024, 512)), tD, tkt), [], [BF16],
                     ident)[0]
    dO, drg = _ret_finish_bwd(o_f, o_b, P, dY, L)
    (dqf, dkf, dvf, dqb, dkb, dvb, ds0, dlg), (p_out,) = _ret_bwd(
        P, rope_ret, tabs, st_f, st_b, dO, L, comm=[("a2a", _rows_slots(dW_out))])
    dck, dcv, d_rd = _ret_ctx_bwd(P, ctab, ds0, dlg, rdb, L, Lc)
    (daq, dkl, dvl, dkx, dvx, d_sink), (p_up,) = _att_bwd(
        P, rope_att, sinkb, Y, dY, L, Lc, comm=[("a2a", _rows_slots(dW_upT))])
    dP = _assemble_dp(L, Lc, dqf, dqb, dkf, dkb, dvf, dvb, drg, daq, dkl, dvl, rope_att, dck, dcv, dkx, dvx)
    tkc = _pick(L + Lc, (768, 384, 256, 128))
    dW_inT = _matmul("mm_gw_in", [(dP, H, 0)], 1, D_PROJ, D, L + Lc, "tn", (tP, tD, tkc), [], [BF16], ident)[0]
    (dH,), (p_in,) = _matmul("mm_d_in", [(dP, W_inT, 0)], 1, L + Lc, D, D_PROJ, "nn",
                             (tm, tD, _pick(D_PROJ, (512, 256))), [], [BF16], ident,
                             comm=[("a2a", _rows_slots(dW_inT))])
    grad_x, sums_m = _modulate_bwd("mod_mix_bwd", xs, cx, dH, norm_mix, mix_mod, modc, dx1, None, None)

    zero = jnp.zeros((1, D), F32)
    dmod = jnp.concatenate([sums_m[0:1], sums_m[1:2], sums_f[6:7], sums_f[0:1], sums_f[1:2], sums_l[2:3]], axis=1)
    dmodc = jnp.concatenate([sums_m[3:4], sums_m[4:5], zero, zero, zero, zero], axis=1)
    dm_all = _allgather(jnp.concatenate([dmod, dmodc], axis=0), "ag_dmod")
    dm_cols = lax.dynamic_slice_in_dim(dm_all, me * C6, C6, axis=2)
    dm_in = jnp.concatenate([dm_cols[:, 0, :], dm_cols[:, 1, :]], axis=0)
    s_bwd = jnp.concatenate([cs, jnp.broadcast_to(c_ctx[None, :], (N_DEV, D))], axis=0)
    g_w_mod, dsil = _mod_bwd(s_bwd, dm_in, w_mod[0])

    lane_pad = lambda a: _pad_rows(a.reshape(-1, 1), LANES).reshape(1, LANES)
    pack = jnp.concatenate([dsil[0:1], sums_m[2:3], sums_f[2:3], sums_l[1:2],
                            lane_pad(d_rd[:, 0]), lane_pad(d_sink[:, 0]), sums_l[3:4, 0:LANES]], axis=1)
    packs = _allgather(pack, "ag_small")
    zl = jnp.zeros((1, LANES), F32)

    def pack_w(a_c, a_nm, a_nf, a_fin, a_rd, a_sk):
        return jnp.concatenate([a_c.reshape(1, D), a_nm, a_nf, a_fin.reshape(1, D), lane_pad(a_rd.reshape(-1)),
                                lane_pad(a_sk.reshape(-1)), zl], axis=1)

    sg, sd, sm, sv = _adam("adam_small", pack_w(c_ctx, norm_mix, norm_ffn, norm_final, ret_decay, attn_sink),
                           pack_w(m_c_ctx, m_norm_mix, m_norm_ffn, m_norm_final, m_ret_decay, m_attn_sink),
                           pack_w(v_c_ctx, v_norm_mix, v_norm_ffn, v_norm_final, v_ret_decay, v_attn_sink),
                           parts=packs)
    loss = sg[0, 4 * D + 2 * LANES]

    def unpack(a):
        return (a[0, 0:D], a[:, D:2 * D], a[:, 2 * D:3 * D], a[0, 3 * D:4 * D],
                a[0, 4 * D:4 * D + 2 * RET_HEADS].reshape(1, 2, RET_HEADS),
                a[:, 4 * D + LANES:4 * D + LANES + ATT_HEADS])

    bg, bd, bm, bv = _adam("adam_b_mod", b_mod, m_b_mod, v_b_mod, parts=dm_all.reshape(2 * N_DEV, 1, 6 * D))
    wg, wd, wm, wv = _adam("adam_w_mod", w_mod[0], m_w_mod[0], v_w_mod[0], g=g_w_mod)

    big = {}
    for nm, w, m, v, parts, transposed in (
            ("w_in", w_in, m_w_in, v_w_in, p_in, True), ("w_out", w_out, m_w_out, v_w_out, p_out, False),
            ("w_gate", w_gate, m_w_gate, v_w_gate, p_gate, True), ("w_up", w_up, m_w_up, v_w_up, p_up, True),
            ("w_down", w_down, m_w_down, v_w_down, p_down, False)):
        if transposed:
            res = [a.T for a in _adam("adam_" + nm, w[0].T, m[0].T, v[0].T, parts=parts)]
        else:
            res = _adam("adam_" + nm, w[0], m[0], v[0], parts=parts)
        big[nm] = [a[None] for a in res]

    g_s, d_s, m_s, v_s = unpack(sg), unpack(sd), unpack(sm), unpack(sv)

    def leaves(k, small, bmod, wmod):
        return (small[0], wmod[None], bmod, small[1], small[2], big["w_in"][k], small[4], small[5],
                big["w_out"][k], big["w_gate"][k], big["w_up"][k], big["w_down"][k], small[3])

    return (loss, grad_x[None], *leaves(0, g_s, bg, wg), *leaves(1, d_s, bd, wd),
            *leaves(2, m_s, bm, wm), *leaves(3, v_s, bv, wv))
```

```python
import functools

import jax
import jax.numpy as jnp
from jax import lax
from jax.experimental import pallas as pl
from jax.experimental.pallas import tpu as pltpu

F32 = jnp.float32
BF16 = jnp.bfloat16

N_DEV = 8
LANES = 128
RET_HEADS = 8
RET_DK = 64
RET_DV = 128
CHUNK = 128
ATT_HEADS = 16
ATT_KV = 4
ATT_DH = 64
GRID_W = 64
ROPE_BASE = 10000.0
EPS = 1e-6
NEG = -1e30
C_RQ, C_RK, C_RV, C_RG, C_AQ, C_AK, C_AV, D_PROJ = 0, 512, 1024, 2048, 3072, 4096, 4352, 4608
K_SCALE = RET_DK ** -0.5
A_SCALE = ATT_DH ** -0.5

ADAM_LR, ADAM_B1, ADAM_B2, ADAM_EPS, ADAM_WD, ADAM_STEP = 0.001, 0.9, 0.999, 1e-08, 0.01, 10

VMEM_BIG = 52 * 1024 * 1024

NN = (((1,), (0,)), ((), ()))
NT = (((1,), (1,)), ((), ()))
TN = (((0,), (0,)), ((), ()))


def _dot(a, b, dims):
    return lax.dot_general(a, b, dims, preferred_element_type=F32)


def _cparams(sem, vmem=VMEM_BIG):
    return pltpu.CompilerParams(dimension_semantics=sem, vmem_limit_bytes=vmem)


def _pick(dim, prefs):
    for p in prefs:
        if dim % p == 0:
            return p
    return dim


def _my_id():
    return lax.axis_index("x") * 4 + lax.axis_index("y") * 2 + lax.axis_index("c")


def _sigmoid(x):
    return 0.5 * jnp.tanh(0.5 * x) + 0.5


def _peers():
    mx, my, mc = lax.axis_index("x"), lax.axis_index("y"), lax.axis_index("c")
    out = []
    for k in range(1, N_DEV):
        kx, ky, kc = (k >> 2) & 1, (k >> 1) & 1, k & 1
        px = 1 - mx if kx else mx
        py = 1 - my if ky else my
        pc = 1 - mc if kc else mc
        out.append(((px, py, pc), px * 4 + py * 2 + pc))
    return out


def _exchange_copies(kind, x_ref, o_ref, ssem, rsem, lsem):
    me = _my_id()
    loc = pltpu.make_async_copy(x_ref if kind == "ag" else x_ref.at[me], o_ref.at[me], lsem)
    cps = []
    for k, (peer, pid) in enumerate(_peers()):
        cps.append(pltpu.make_async_remote_copy(
            src_ref=x_ref if kind == "ag" else x_ref.at[pid], dst_ref=o_ref.at[me],
            send_sem=ssem.at[k], recv_sem=rsem.at[k], device_id=peer, device_id_type=pl.DeviceIdType.MESH))
    return loc, cps


def _two_level_copies(x_ref, o_ref, ssem, rsem, lsem):
    mx, my, mc = lax.axis_index("x"), lax.axis_index("y"), lax.axis_index("c")
    me = mx * 4 + my * 2 + mc
    sibling = (mx, my, 1 - mc)
    chips = [(1 - mx, my), (mx, 1 - my), (1 - mx, 1 - my)]

    def copy(k, slot, to, src=None):
        return pltpu.make_async_remote_copy(
            src_ref=o_ref.at[slot] if src is None else src, dst_ref=o_ref.at[slot],
            send_sem=ssem.at[k], recv_sem=rsem.at[k], device_id=to, device_id_type=pl.DeviceIdType.MESH)

    loc = pltpu.make_async_copy(x_ref, o_ref.at[me], lsem)
    first = [copy(0, me, sibling, src=x_ref)]
    first += [copy(1 + j, me, (cx, cy, mc), src=x_ref) for j, (cx, cy) in enumerate(chips)]
    passed = [copy(4 + j, cx * 4 + cy * 2 + mc, sibling) for j, (cx, cy) in enumerate(chips)]
    return loc, first, passed


def _exchange_start(kind, x_ref, o_ref, ssem, rsem, lsem):
    if kind == "ag2":
        loc, first, _ = _two_level_copies(x_ref, o_ref, ssem, rsem, lsem)
        cps = first
    else:
        loc, cps = _exchange_copies(kind, x_ref, o_ref, ssem, rsem, lsem)
    loc.start()
    for cp in cps:
        cp.start()


def _exchange_wait(kind, x_ref, o_ref, ssem, rsem, lsem):
    if kind == "ag2":
        loc, first, passed = _two_level_copies(x_ref, o_ref, ssem, rsem, lsem)
        for j in range(3):
            first[1 + j].wait_recv()
            passed[j].start()
        first[0].wait_recv()
        for cp in passed:
            cp.wait_recv()
        cps = first + passed
    else:
        loc, cps = _exchange_copies(kind, x_ref, o_ref, ssem, rsem, lsem)
        for cp in cps:
            cp.wait_recv()
    for cp in cps:
        cp.wait_send()
    loc.wait()


_EXCHANGE_SEMS = [pltpu.SemaphoreType.DMA((N_DEV - 1,)), pltpu.SemaphoreType.DMA((N_DEV - 1,)),
                  pltpu.SemaphoreType.DMA(())]


def _exchange_shape(kind, x):
    return jax.ShapeDtypeStruct(x.shape if kind == "a2a" else (N_DEV,) + x.shape, x.dtype)


def _exchange(kind, x, name):
    def body(x_ref, o_ref, ssem, rsem, lsem):
        _exchange_start(kind, x_ref, o_ref, ssem, rsem, lsem)
        _exchange_wait(kind, x_ref, o_ref, ssem, rsem, lsem)

    return pl.pallas_call(
        body, name=name, out_shape=_exchange_shape(kind, x),
        in_specs=[pl.BlockSpec(memory_space=pl.ANY)], out_specs=pl.BlockSpec(memory_space=pl.ANY),
        scratch_shapes=list(_EXCHANGE_SEMS),
    )(x)


def _allgather(x, name):
    return _exchange("ag", x, name)


def _call(body, name, grid, in_specs, out_specs, out_shape, scratch_shapes, sem, args, comm=(), aliases=None):
    in_specs, out_specs, out_shape = list(in_specs), list(out_specs), list(out_shape)
    scratch_shapes = list(scratch_shapes)
    aliases = aliases or {}
    if not comm:
        outs = pl.pallas_call(body, name=name, grid=grid, in_specs=in_specs, out_specs=out_specs, out_shape=out_shape,
                              scratch_shapes=scratch_shapes, input_output_aliases=aliases,
                              compiler_params=_cparams(sem))(*args)
        return list(outs), []
    n_in, n_out, n_scr, n_c = len(in_specs), len(out_specs), len(scratch_shapes), len(comm)
    hbm = pl.BlockSpec(memory_space=pl.ANY)

    def wrapped(*refs):
        ins, cins = refs[:n_in], refs[n_in:n_in + n_c]
        outs = refs[n_in + n_c:n_in + n_c + n_out]
        couts = refs[n_in + n_c + n_out:n_in + 2 * n_c + n_out]
        scr = refs[n_in + 2 * n_c + n_out:n_in + 2 * n_c + n_out + n_scr]
        sems = refs[n_in + 2 * n_c + n_out + n_scr:]
        first = pl.program_id(0) == 0
        last = pl.program_id(0) == grid[0] - 1
        for ax in range(1, len(grid)):
            first = first & (pl.program_id(ax) == 0)
            last = last & (pl.program_id(ax) == grid[ax] - 1)

        @pl.when(first)
        def _():
            for c, (kind, _) in enumerate(comm):
                _exchange_start(kind, cins[c], couts[c], *sems[3 * c:3 * c + 3])

        body(*ins, *outs, *scr)

        @pl.when(last)
        def _():
            for c, (kind, _) in enumerate(comm):
                _exchange_wait(kind, cins[c], couts[c], *sems[3 * c:3 * c + 3])

    res = pl.pallas_call(
        wrapped, name=name, grid=grid,
        in_specs=in_specs + [hbm] * n_c, out_specs=out_specs + [hbm] * n_c,
        out_shape=out_shape + [_exchange_shape(kind, arr) for kind, arr in comm],
        scratch_shapes=scratch_shapes + list(_EXCHANGE_SEMS) * n_c, input_output_aliases=aliases,
        compiler_params=_cparams(("arbitrary",) * len(grid)),
    )(*args, *[arr for _, arr in comm])
    return list(res[:n_out]), list(res[n_out:])


def _matmul(name, pairs, n_acc, M, N, K, mode, tiles, extras, out_dtypes, epilogue, j_outer=False, comm=()):
    tm, tn, tk = tiles
    gm, gn, nk = M // tm, N // tn, K // tk
    assert gm * tm == M and gn * tn == N and nk * tk == K, (name, M, N, K, tiles)
    if j_outer:
        grid = (gn, gm, nk)
        ij = lambda g0, g1: (g1, g0)
    else:
        grid = (gm, gn, nk)
        ij = lambda g0, g1: (g0, g1)

    if mode in ("nn", "nt"):
        a_spec = pl.BlockSpec((tm, tk), lambda g0, g1, k: (ij(g0, g1)[0], k))
    else:
        a_spec = pl.BlockSpec((tk, tm), lambda g0, g1, k: (k, ij(g0, g1)[0]))
    if mode == "nt":
        b_spec = pl.BlockSpec((tn, tk), lambda g0, g1, k: (ij(g0, g1)[1], k))
    else:
        b_spec = pl.BlockSpec((tk, tn), lambda g0, g1, k: (k, ij(g0, g1)[1]))
    dims = {"nn": NN, "nt": NT, "tn": TN}[mode]
    mn_spec = pl.BlockSpec((tm, tn), lambda g0, g1, k: ij(g0, g1))
    n_spec = pl.BlockSpec((1, tn), lambda g0, g1, k: (0, ij(g0, g1)[1]))

    in_specs, args = [], []
    for a, b, _ in pairs:
        in_specs += [a_spec, b_spec]
        args += [a, b]
    for arr, kind in extras:
        in_specs.append(mn_spec if kind == "mn" else n_spec)
        args.append(arr)
    n_p, n_e, n_o = len(pairs), len(extras), len(out_dtypes)

    def body(*refs):
        ab = refs[:2 * n_p]
        ex = refs[2 * n_p:2 * n_p + n_e]
        outs = refs[2 * n_p + n_e:2 * n_p + n_e + n_o]
        accs = refs[2 * n_p + n_e + n_o:]
        k = pl.program_id(2)

        def partial_sums():
            sums = [None] * n_acc
            for p, (_, _, ai) in enumerate(pairs):
                d = _dot(ab[2 * p][...], ab[2 * p + 1][...], dims)
                sums[ai] = d if sums[ai] is None else sums[ai] + d
            return sums

        def finish(acc_vals):
            res = epilogue(acc_vals, [e[...] for e in ex])
            for o, r in zip(outs, res):
                o[...] = r.astype(o.dtype)

        def accumulate(first):
            w = _pick(tm if mode == "tn" else tn, (512, 384, 256))
            for c in range((tm if mode == "tn" else tn) // w):
                sl = slice(c * w, (c + 1) * w)
                sums = [None] * n_acc
                for p, (_, _, ai) in enumerate(pairs):
                    a_ref, b_ref = ab[2 * p], ab[2 * p + 1]
                    if mode == "tn":
                        d = _dot(a_ref[:, sl], b_ref[...], dims)
                    elif mode == "nn":
                        d = _dot(a_ref[...], b_ref[:, sl], dims)
                    else:
                        d = _dot(a_ref[...], b_ref[sl, :], dims)
                    sums[ai] = d if sums[ai] is None else sums[ai] + d
                idx = (sl, slice(None)) if mode == "tn" else (slice(None), sl)
                for ai, s in enumerate(sums):
                    if first:
                        accs[ai][idx] = s
                    else:
                        accs[ai][idx] += s

        if nk == 1:
            finish(partial_sums())
        else:
            pl.when(k == 0)(functools.partial(accumulate, True))
            pl.when(k > 0)(functools.partial(accumulate, False))

            @pl.when(k == nk - 1)
            def _():
                finish([a[...] for a in accs])

    outs, couts = _call(
        body, name, grid, in_specs, [mn_spec] * n_o,
        [jax.ShapeDtypeStruct((M, N), dt) for dt in out_dtypes],
        [pltpu.VMEM((tm, tn), F32) for _ in range(n_acc if nk > 1 else 0)],
        ("parallel", "parallel", "arbitrary"), args, comm)
    return (outs, couts) if comm else outs


def _rope_tables(L):
    t = jnp.arange(L, dtype=jnp.int32)
    f = jnp.arange(32, dtype=jnp.int32).astype(F32)
    ang = t.astype(F32)[:, None] * (ROPE_BASE ** (-f / 32.0))[None, :]
    cos, sin = jnp.cos(ang), jnp.sin(ang)
    ret = jnp.stack([jnp.tile(cos, (1, 4)), jnp.tile(jnp.concatenate([-sin, sin], axis=1), (1, 2))])
    f2 = jnp.arange(16, dtype=jnp.int32).astype(F32)
    inv2 = (ROPE_BASE ** (-f2 / 16.0))[None, :]
    ang_r = (t // GRID_W).astype(F32)[:, None] * inv2
    ang_c = (t % GRID_W).astype(F32)[:, None] * inv2
    cr, sr, cc, sc = jnp.cos(ang_r), jnp.sin(ang_r), jnp.cos(ang_c), jnp.sin(ang_c)
    att = jnp.stack([jnp.tile(jnp.concatenate([cr, cr, cc, cc], axis=1), (1, 2)),
                     jnp.tile(jnp.concatenate([-sr, sr, -sc, sc], axis=1), (1, 2))])
    return ret.astype(F32), att.astype(F32)


def _swap(x, sh):
    lane = lax.broadcasted_iota(jnp.int32, x.shape, 1)
    ra = pltpu.roll(x, LANES - sh, 1)
    rb = pltpu.roll(x, sh, 1)
    la = pltpu.roll(lane, LANES - sh, 1)
    partner = jnp.where((lane % (2 * sh)) < sh, lane + sh, lane - sh)
    return jnp.where(la == partner, ra, rb)


def _rope(x, cos, sin, sh):
    return x * cos + _swap(x, sh) * sin


def _rope_t(d, cos, sin, sh):
    return d * cos + _swap(d * sin, sh)


def _half_mask(shape, a):
    lane = lax.broadcasted_iota(jnp.int32, shape, 1)
    return (lane < 64) if a == 0 else (lane >= 64)


def _mod_fwd(s_in, w_l, b_l):
    D, C6 = w_l.shape
    tk = _pick(D, (512, 256, 128))
    nk = D // tk

    def body(s_ref, w_ref, b_ref, o_ref):
        k = pl.program_id(0)
        s = s_ref[...]
        s = s * _sigmoid(s)
        d = jnp.dot(s, w_ref[...], preferred_element_type=F32, precision=lax.Precision.HIGHEST)

        @pl.when(k == 0)
        def _():
            o_ref[...] = d + b_ref[...]

        @pl.when(k > 0)
        def _():
            o_ref[...] += d

    return pl.pallas_call(
        body, name="mod_fwd", grid=(nk,),
        in_specs=[pl.BlockSpec((16, tk), lambda k: (0, k)), pl.BlockSpec((tk, C6), lambda k: (k, 0)),
                  pl.BlockSpec((1, C6), lambda k: (0, 0))],
        out_specs=pl.BlockSpec((16, C6), lambda k: (0, 0)),
        out_shape=jax.ShapeDtypeStruct((16, C6), F32),
        compiler_params=_cparams(("arbitrary",)),
    )(s_in, w_l, b_l)


def _mod_bwd(s_in, dm, w_l):
    D, C6 = w_l.shape
    tk = _pick(D, (512, 256, 128))
    nk = D // tk

    def body(s_ref, dm_ref, w_ref, gw_ref, gc_ref):
        s = s_ref[...]
        sg = _sigmoid(s)
        act = s * sg
        dmv = dm_ref[...]
        gw_ref[...] = lax.dot_general(act, dmv, TN, preferred_element_type=F32, precision=lax.Precision.HIGHEST)
        ds = lax.dot_general(dmv, w_ref[...], NT, preferred_element_type=F32, precision=lax.Precision.HIGHEST)
        dsil = (sg * (1.0 + s * (1.0 - sg)))[8:9, :]
        gc_ref[...] = jnp.zeros((8, tk), F32) + jnp.sum(ds[8:16, :], axis=0, keepdims=True) * dsil

    return pl.pallas_call(
        body, name="mod_bwd", grid=(nk,),
        in_specs=[pl.BlockSpec((16, tk), lambda k: (0, k)), pl.BlockSpec((16, C6), lambda k: (0, 0)),
                  pl.BlockSpec((tk, C6), lambda k: (k, 0))],
        out_specs=[pl.BlockSpec((tk, C6), lambda k: (k, 0)), pl.BlockSpec((8, tk), lambda k: (0, k))],
        out_shape=[jax.ShapeDtypeStruct((D, C6), F32), jax.ShapeDtypeStruct((8, D), F32)],
        compiler_params=_cparams(("parallel",)),
    )(s_in, dm, w_l)


def _norm_rows(x):
    r = lax.rsqrt(jnp.mean(x * x, axis=-1, keepdims=True) + EPS)
    return x * r, r


def _modulate_fwd(name, x, ctx, g, mod, modc, comm=()):
    L, D = x.shape
    tr = ctx.shape[0]
    nx = L // tr

    def body(x_ref, c_ref, g_ref, m_ref, mc_ref, o_ref):
        i = pl.program_id(0)

        def run(src, m):
            n, _ = _norm_rows(src[...])
            o_ref[...] = (n * g_ref[...] * (1.0 + m[1:2, :]) + m[0:1, :]).astype(o_ref.dtype)

        @pl.when(i < nx)
        def _():
            run(x_ref, m_ref)

        @pl.when(i >= nx)
        def _():
            run(c_ref, mc_ref)

    row = pl.BlockSpec((tr, D), lambda i: (jnp.minimum(i, nx - 1), 0))
    vec = pl.BlockSpec((1, D), lambda i: (0, 0))
    mv = pl.BlockSpec((8, D), lambda i: (0, 0))
    return _call(
        body, name, (nx + 1,), [row, pl.BlockSpec((tr, D), lambda i: (0, 0)), vec, mv, mv],
        [pl.BlockSpec((tr, D), lambda i: (i, 0))], [jax.ShapeDtypeStruct((L + tr, D), BF16)], [],
        ("parallel",), (x, ctx, g, mod, modc), comm)


def _residual_modulate_fwd(name, x, fbr, gate, g, mod):
    L, D = x.shape
    tr = _pick(L, (256, 128))

    def body(x_ref, f_ref, gt_ref, g_ref, m_ref, x1_ref, o_ref):
        x1 = x_ref[...] + gt_ref[...] * f_ref[...].astype(F32)
        x1_ref[...] = x1
        n, _ = _norm_rows(x1)
        o_ref[...] = (n * g_ref[...] * (1.0 + m_ref[1:2, :]) + m_ref[0:1, :]).astype(o_ref.dtype)

    row = pl.BlockSpec((tr, D), lambda i: (i, 0))
    vec = pl.BlockSpec((1, D), lambda i: (0, 0))
    return pl.pallas_call(
        body, name=name, grid=(L // tr,),
        in_specs=[row, row, vec, vec, pl.BlockSpec((8, D), lambda i: (0, 0))],
        out_specs=[row, row],
        out_shape=[jax.ShapeDtypeStruct((L, D), F32), jax.ShapeDtypeStruct((L, D), BF16)],
        compiler_params=_cparams(("parallel",)),
    )(x, fbr, gate, g, mod)


def _modulate_bwd(name, x, ctx, dh, g, mod, modc, dres, fbr, gate):
    L, D = x.shape
    tr = ctx.shape[0] if ctx is not None else _pick(L, (256, 128))
    nx = L // tr
    nt = nx + (1 if ctx is not None else 0)
    has_f = fbr is not None

    def body(*refs):
        refs = list(refs)
        x_ref = refs.pop(0)
        c_ref = refs.pop(0) if ctx is not None else None
        dh_ref, g_ref, m_ref = refs.pop(0), refs.pop(0), refs.pop(0)
        mc_ref = refs.pop(0) if ctx is not None else None
        dr_ref = refs.pop(0)
        f_ref = refs.pop(0) if has_f else None
        gt_ref = refs.pop(0) if has_f else None
        dx_ref = refs.pop(0)
        df_ref = refs.pop(0) if has_f else None
        acc_ref = refs.pop(0)
        i = pl.program_id(0)

        @pl.when(i == 0)
        def _():
            acc_ref[...] = jnp.zeros_like(acc_ref)

        def sums(src, m, base, grow):
            n, r = _norm_rows(src[...])
            d = dh_ref[...].astype(F32)
            gg = g_ref[...]
            sc1 = 1.0 + m[1:2, :]
            acc_ref[base:base + 1, :] += jnp.sum(d, axis=0, keepdims=True)
            dn = d * n
            acc_ref[base + 1:base + 2, :] += jnp.sum(dn, axis=0, keepdims=True) * gg
            acc_ref[grow:grow + 1, :] += jnp.sum(dn, axis=0, keepdims=True) * sc1
            dnv = d * (gg * sc1)
            return r * (dnv - n * jnp.mean(dnv * n, axis=-1, keepdims=True))

        def x_rows():
            dx = sums(x_ref, m_ref, 0, 2) + dr_ref[...]
            dx_ref[...] = dx
            if has_f:
                acc_ref[6:7, :] += jnp.sum(dx * f_ref[...].astype(F32), axis=0, keepdims=True)
                df_ref[...] = (dx * gt_ref[...]).astype(df_ref.dtype)

        if ctx is None:
            x_rows()
        else:
            pl.when(i < nx)(x_rows)

            @pl.when(i >= nx)
            def _():
                sums(c_ref, mc_ref, 3, 2)

    row = pl.BlockSpec((tr, D), lambda i: (jnp.minimum(i, nx - 1), 0))
    vec = pl.BlockSpec((1, D), lambda i: (0, 0))
    mv = pl.BlockSpec((8, D), lambda i: (0, 0))
    in_specs, args = [row], [x]
    if ctx is not None:
        in_specs.append(pl.BlockSpec((tr, D), lambda i: (0, 0)))
        args.append(ctx)
    in_specs += [pl.BlockSpec((tr, D), lambda i: (i, 0)), vec, mv]
    args += [dh, g, mod]
    if ctx is not None:
        in_specs.append(mv)
        args.append(modc)
    in_specs.append(row)
    args.append(dres)
    out_specs = [row]
    out_shape = [jax.ShapeDtypeStruct((L, D), F32)]
    if has_f:
        in_specs += [row, vec]
        args += [fbr, gate]
        out_specs.append(row)
        out_shape.append(jax.ShapeDtypeStruct((L, D), BF16))
    out_specs.append(pl.BlockSpec((16, D), lambda i: (0, 0)))
    out_shape.append(jax.ShapeDtypeStruct((16, D), F32))
    return pl.pallas_call(
        body, name=name, grid=(nt,), in_specs=in_specs, out_specs=out_specs, out_shape=out_shape,
        compiler_params=_cparams(("arbitrary",)),
    )(*args)


def _loss_head(x1, tgt, nf, fbr, gate):
    L, D = x1.shape
    tr = _pick(L, (256, 128))

    def body(x_ref, t_ref, w_ref, f_ref, gt_ref, dx_ref, df_ref, acc_ref):
        i = pl.program_id(0)

        @pl.when(i == 0)
        def _():
            acc_ref[...] = jnp.zeros_like(acc_ref)

        n, r = _norm_rows(x_ref[...] + gt_ref[...] * f_ref[...].astype(F32))
        w = w_ref[...]
        e = n * w - t_ref[...]
        acc_ref[0:1, :] += jnp.sum(e * e, axis=0, keepdims=True) * (0.5 / D)
        dout = e * (1.0 / D)
        acc_ref[1:2, :] += jnp.sum(dout * n, axis=0, keepdims=True)
        dn = dout * w
        dx = r * (dn - n * jnp.mean(dn * n, axis=-1, keepdims=True))
        dx_ref[...] = dx
        acc_ref[2:3, :] += jnp.sum(dx * f_ref[...].astype(F32), axis=0, keepdims=True)
        df_ref[...] = (dx * gt_ref[...]).astype(df_ref.dtype)

        @pl.when(i == pl.num_programs(0) - 1)
        def _():
            acc_ref[3:4, :] = jnp.zeros((1, D), F32) + jnp.sum(acc_ref[0:1, :])

    row = pl.BlockSpec((tr, D), lambda i: (i, 0))
    vec = pl.BlockSpec((1, D), lambda i: (0, 0))
    return pl.pallas_call(
        body, name="loss_head", grid=(L // tr,),
        in_specs=[row, row, vec, row, vec],
        out_specs=[row, row, pl.BlockSpec((8, D), lambda i: (0, 0))],
        out_shape=[jax.ShapeDtypeStruct((L, D), F32), jax.ShapeDtypeStruct((L, D), BF16),
                   jax.ShapeDtypeStruct((8, D), F32)],
        compiler_params=_cparams(("arbitrary",)),
    )(x1, tgt, nf, fbr, gate)


N_TAB = 7


def _ret_tables(rdb, Lc):
    def body(rd_ref, t_ref, c_ref):
        d = pl.program_id(0) // RET_HEADS
        fwd = d == 0
        lg = -jnp.exp(rd_ref[0])
        i = lax.broadcasted_iota(jnp.int32, (CHUNK, CHUNK), 0).astype(F32)
        j = lax.broadcasted_iota(jnp.int32, (CHUNK, CHUNK), 1).astype(F32)
        rel = jnp.where(fwd, i - j, j - i)
        mask = (rel > 0.0) | ((rel == 0.0) & fwd)
        dm = jnp.where(mask, jnp.exp(lg * jnp.maximum(rel, 0.0)), 0.0)
        t_ref[0, 0] = dm
        t_ref[0, 1] = rel * dm
        qc = jnp.where(fwd, i + 1.0, CHUNK - i)
        qw = jnp.exp(lg * qc)
        t_ref[0, 2] = qw
        t_ref[0, 3] = qw * qc
        kc = jnp.where(fwd, CHUNK - 1.0 - i, i)
        kw = jnp.exp(lg * kc)
        t_ref[0, 4] = kw
        t_ref[0, 5] = kw * kc
        t_ref[0, 6] = jnp.exp(lg * float(CHUNK)) + jnp.zeros((CHUNK, CHUNK), F32)
        m = lax.broadcasted_iota(jnp.int32, (Lc, LANES), 0).astype(F32)
        cc = jnp.where(fwd, Lc - 1.0 - m, m)
        cw = jnp.exp(lg * cc)
        c_ref[0, 0] = cw
        c_ref[0, 1] = cw * cc

    return pl.pallas_call(
        body, name="ret_tables", grid=(2 * RET_HEADS,),
        in_specs=[pl.BlockSpec((1, 1, LANES), lambda r: (r, 0, 0))],
        out_specs=[pl.BlockSpec((1, N_TAB, CHUNK, CHUNK), lambda r: (r, 0, 0, 0)),
                   pl.BlockSpec((1, 2, Lc, LANES), lambda r: (r, 0, 0, 0))],
        out_shape=[jax.ShapeDtypeStruct((2 * RET_HEADS, N_TAB, CHUNK, CHUNK), F32),
                   jax.ShapeDtypeStruct((2 * RET_HEADS, 2, Lc, LANES), F32)],
        compiler_params=_cparams(("parallel",)),
    )(rdb)


def _ret_ctx_state(P, ctab, L, Lc):
    cb = L // Lc

    def body(k_ref, v_ref, c_ref, s_ref):
        for p in range(RET_HEADS // 2):
            kp = k_ref[:, p * LANES:(p + 1) * LANES].astype(F32) * K_SCALE
            for a in range(2):
                h = 2 * p + a
                kh = jnp.where(_half_mask(kp.shape, a), kp, 0.0)
                vh = v_ref[:, h * RET_DV:(h + 1) * RET_DV]
                for d in range(2):
                    kw = (kh * c_ref[d * RET_HEADS + h, 0]).astype(BF16)
                    s_ref[d * RET_HEADS + h] = _dot(kw, vh, TN)

    return pl.pallas_call(
        body, name="ret_ctx_state", grid=(1,),
        in_specs=[pl.BlockSpec((Lc, 512), lambda i: (cb, C_RK // 512)),
                  pl.BlockSpec((Lc, 1024), lambda i: (cb, C_RV // 1024)),
                  pl.BlockSpec((2 * RET_HEADS, 2, Lc, LANES), lambda i: (0, 0, 0, 0))],
        out_specs=pl.BlockSpec((2 * RET_HEADS, LANES, RET_DV), lambda i: (0, 0, 0)),
        out_shape=jax.ShapeDtypeStruct((2 * RET_HEADS, LANES, RET_DV), F32),
        compiler_params=_cparams(("arbitrary",)),
    )(P, P, ctab)


def _ret_fwd(P, rope, tabs, s0, L, comm=()):
    n = L // CHUNK

    def body(qf, kf, vf, rf, qb, kb, vb, rb, t_ref, s0_ref, of_ref, ob_ref, stf_ref, stb_ref, st):
        s = pl.program_id(0)

        @pl.when(s == 0)
        def _():
            st[...] = s0_ref[...]

        units = []
        for d, (q_ref, k_ref, v_ref, r_ref, o_ref, so_ref) in enumerate(
                ((qf, kf, vf, rf, of_ref, stf_ref), (qb, kb, vb, rb, ob_ref, stb_ref))):
            cos, sin = r_ref[0], r_ref[1]
            for p in range(RET_HEADS // 2):
                qp = _rope(q_ref[:, p * LANES:(p + 1) * LANES].astype(F32), cos, sin, 32)
                kp = _rope(k_ref[:, p * LANES:(p + 1) * LANES].astype(F32), cos, sin, 32) * K_SCALE
                for a in range(2):
                    h = 2 * p + a
                    hm = _half_mask(qp.shape, a)
                    units.append(dict(r=d * RET_HEADS + h, h=h, a=a, o_ref=o_ref, so_ref=so_ref, v_ref=v_ref,
                                      qh=jnp.where(hm, qp, 0.0), kh=jnp.where(hm, kp, 0.0)))
        for u in units:
            u["sc"] = _dot(u["qh"].astype(BF16), u["kh"].astype(BF16), NT)
        for u in units:
            r, h = u["r"], u["h"]
            sp = st[r]
            u["so_ref"][0, h] = sp[u["a"] * RET_DK:(u["a"] + 1) * RET_DK, :]
            vh = u["v_ref"][:, h * RET_DV:(h + 1) * RET_DV]
            o = _dot((u["sc"] * t_ref[r, 0]).astype(BF16), vh, NN)
            o += _dot((u["qh"] * t_ref[r, 2]).astype(BF16), sp.astype(BF16), NN)
            u["o_ref"][:, h * RET_DV:(h + 1) * RET_DV] = o
        for u in units:
            r, h = u["r"], u["h"]
            vh = u["v_ref"][:, h * RET_DV:(h + 1) * RET_DV]
            st[r] = t_ref[r, 6] * st[r] + _dot((u["kh"] * t_ref[r, 4]).astype(BF16), vh, TN)

    fw = lambda s: s
    bw = lambda s: n - 1 - s

    def specs(cm):
        return [pl.BlockSpec((CHUNK, 512), lambda s: (cm(s), C_RQ // 512)),
                pl.BlockSpec((CHUNK, 512), lambda s: (cm(s), C_RK // 512)),
                pl.BlockSpec((CHUNK, 1024), lambda s: (cm(s), C_RV // 1024)),
                pl.BlockSpec((2, CHUNK, LANES), lambda s: (0, cm(s), 0))]

    full = lambda shp: pl.BlockSpec(shp, lambda s: (0,) * len(shp))
    return _call(
        body, "ret_fwd", (n,),
        specs(fw) + specs(bw) + [full((2 * RET_HEADS, N_TAB, CHUNK, CHUNK)), full((2 * RET_HEADS, LANES, RET_DV))],
        [pl.BlockSpec((CHUNK, 1024), lambda s: (fw(s), 0)),
         pl.BlockSpec((CHUNK, 1024), lambda s: (bw(s), 0)),
         pl.BlockSpec((1, RET_HEADS, RET_DK, RET_DV), lambda s: (fw(s), 0, 0, 0)),
         pl.BlockSpec((1, RET_HEADS, RET_DK, RET_DV), lambda s: (bw(s), 0, 0, 0))],
        [jax.ShapeDtypeStruct((L, 1024), F32), jax.ShapeDtypeStruct((L, 1024), F32),
         jax.ShapeDtypeStruct((n, RET_HEADS, RET_DK, RET_DV), F32),
         jax.ShapeDtypeStruct((n, RET_HEADS, RET_DK, RET_DV), F32)],
        [pltpu.VMEM((2 * RET_HEADS, LANES, RET_DV), F32)],
        ("arbitrary",), (P, P, P, rope, P, P, P, rope, tabs, s0), comm)


def _ret_finish_fwd(of, ob, P, L):
    tr = _pick(L, (256, 128))

    def body(f_ref, b_ref, g_ref, y_ref):
        for h in range(RET_HEADS):
            sl = slice(h * RET_DV, (h + 1) * RET_DV)
            n, _ = _norm_rows(f_ref[:, sl] + b_ref[:, sl])
            g = g_ref[:, sl].astype(F32)
            y_ref[:, sl] = (n * (g * _sigmoid(g))).astype(y_ref.dtype)

    row = pl.BlockSpec((tr, 1024), lambda i: (i, 0))
    return pl.pallas_call(
        body, name="ret_finish_fwd", grid=(L // tr,),
        in_specs=[row, row, pl.BlockSpec((tr, 1024), lambda i: (i, C_RG // 1024))],
        out_specs=row, out_shape=jax.ShapeDtypeStruct((L, 2048), BF16),
        compiler_params=_cparams(("parallel",)),
    )(of, ob, P)


def _ret_finish_bwd(of, ob, P, dY, L):
    tr = _pick(L, (256, 128))

    def body(f_ref, b_ref, g_ref, dy_ref, do_ref, dg_ref):
        for h in range(RET_HEADS):
            sl = slice(h * RET_DV, (h + 1) * RET_DV)
            n, r = _norm_rows(f_ref[:, sl] + b_ref[:, sl])
            g = g_ref[:, sl].astype(F32)
            sg = _sigmoid(g)
            dy = dy_ref[:, sl].astype(F32)
            dg_ref[:, sl] = (dy * n * (sg * (1.0 + g * (1.0 - sg)))).astype(dg_ref.dtype)
            dn = dy * (g * sg)
            do_ref[:, sl] = (r * (dn - n * jnp.mean(dn * n, axis=-1, keepdims=True))).astype(do_ref.dtype)

    row = pl.BlockSpec((tr, 1024), lambda i: (i, 0))
    return pl.pallas_call(
        body, name="ret_finish_bwd", grid=(L // tr,),
        in_specs=[row, row, pl.BlockSpec((tr, 1024), lambda i: (i, C_RG // 1024)), row],
        out_specs=[row, row],
        out_shape=[jax.ShapeDtypeStruct((L, 1024), BF16), jax.ShapeDtypeStruct((L, 1024), BF16)],
        compiler_params=_cparams(("parallel",)),
    )(of, ob, P, dY)


def _ret_bwd(P, rope, tabs, stf, stb, dO, L, comm=()):
    n = L // CHUNK

    def body(qf, kf, vf, rf, gf, sf, qb, kb, vb, rb, gb, sb, t_ref,
             dqf, dkf, dvf, dqb, dkb, dvb, ds0_ref, dlg_ref, ds):
        s = pl.program_id(0)

        @pl.when(s == 0)
        def _():
            ds[...] = jnp.zeros_like(ds)
            dlg_ref[...] = jnp.zeros_like(dlg_ref)

        units, pairs = [], []
        for d, (q_ref, k_ref, v_ref, r_ref, g_ref, s_ref, dq_ref, dk_ref, dv_ref) in enumerate(
                ((qf, kf, vf, rf, gf, sf, dqf, dkf, dvf), (qb, kb, vb, rb, gb, sb, dqb, dkb, dvb))):
            cos, sin = r_ref[0], r_ref[1]
            for p in range(RET_HEADS // 2):
                qp = _rope(q_ref[:, p * LANES:(p + 1) * LANES].astype(F32), cos, sin, 32)
                kp = _rope(k_ref[:, p * LANES:(p + 1) * LANES].astype(F32), cos, sin, 32) * K_SCALE
                pair = dict(p=p, cos=cos, sin=sin, dq_ref=dq_ref, dk_ref=dk_ref, us=[])
                pairs.append(pair)
                for a in range(2):
                    h = 2 * p + a
                    r = d * RET_HEADS + h
                    hm = _half_mask(qp.shape, a)
                    zero = jnp.zeros((RET_DK, RET_DV), F32)
                    sp = s_ref[0, h]
                    u = dict(r=r, h=h, dv_ref=dv_ref, qh=jnp.where(hm, qp, 0.0), kh=jnp.where(hm, kp, 0.0),
                             vh=v_ref[:, h * RET_DV:(h + 1) * RET_DV], gh=g_ref[:, h * RET_DV:(h + 1) * RET_DV],
                             sp=jnp.concatenate([sp, zero] if a == 0 else [zero, sp], axis=0),
                             dsn=ds[r])
                    u["qhb"], u["khb"] = u["qh"].astype(BF16), u["kh"].astype(BF16)
                    units.append(u)
                    pair["us"].append(u)
        for u in units:
            u["am"] = _dot(u["qhb"], u["khb"], NT)
            u["dar"] = _dot(u["gh"], u["vh"], NT)
            u["xq"] = _dot(u["gh"], u["sp"].astype(BF16), NT)
            u["yk"] = _dot(u["vh"], u["dsn"].astype(BF16), NT)
        for u in units:
            r = u["r"]
            dm = t_ref[r, 0]
            u["da"] = (u["dar"] * dm).astype(BF16)
            u["amd"] = (u["am"] * dm).astype(BF16)
            part = (jnp.sum(u["am"] * u["dar"] * t_ref[r, 1]) + jnp.sum(u["qh"] * t_ref[r, 3] * u["xq"])
                    + jnp.sum(u["kh"] * t_ref[r, 5] * u["yk"])
                    + float(CHUNK) * jnp.sum(t_ref[r, 6] * u["dsn"] * u["sp"]))
            dlg_ref[r:r + 1, :] += jnp.zeros((1, LANES), F32) + part
        for u in units:
            r, h = u["r"], u["h"]
            u["dq"] = _dot(u["da"], u["khb"], NN) + u["xq"] * t_ref[r, 2]
            u["dk"] = _dot(u["da"], u["qhb"], TN) + u["yk"] * t_ref[r, 4]
            u["dv_ref"][:, h * RET_DV:(h + 1) * RET_DV] = (
                _dot(u["amd"], u["gh"], TN) + _dot((u["kh"] * t_ref[r, 4]).astype(BF16), u["dsn"].astype(BF16), NN))
            ds[r] = t_ref[r, 6] * u["dsn"] + _dot((u["qh"] * t_ref[r, 2]).astype(BF16), u["gh"], TN)
        for pair in pairs:
            sl = slice(pair["p"] * LANES, (pair["p"] + 1) * LANES)
            u0, u1 = pair["us"]
            pair["dq_ref"][:, sl] = _rope_t(u0["dq"] + u1["dq"], pair["cos"], pair["sin"], 32)
            pair["dk_ref"][:, sl] = _rope_t((u0["dk"] + u1["dk"]) * K_SCALE, pair["cos"], pair["sin"], 32)

        @pl.when(s == n - 1)
        def _():
            ds0_ref[...] = ds[...]

    fw = lambda s: n - 1 - s
    bw = lambda s: s

    def specs(cm):
        return [pl.BlockSpec((CHUNK, 512), lambda s: (cm(s), C_RQ // 512)),
                pl.BlockSpec((CHUNK, 512), lambda s: (cm(s), C_RK // 512)),
                pl.BlockSpec((CHUNK, 1024), lambda s: (cm(s), C_RV // 1024)),
                pl.BlockSpec((2, CHUNK, LANES), lambda s: (0, cm(s), 0)),
                pl.BlockSpec((CHUNK, 1024), lambda s: (cm(s), 0)),
                pl.BlockSpec((1, RET_HEADS, RET_DK, RET_DV), lambda s: (cm(s), 0, 0, 0))]

    def ospecs(cm):
        return [pl.BlockSpec((CHUNK, 512), lambda s: (cm(s), 0)), pl.BlockSpec((CHUNK, 512), lambda s: (cm(s), 0)),
                pl.BlockSpec((CHUNK, 1024), lambda s: (cm(s), 0))]

    oshape = [jax.ShapeDtypeStruct((L, 512), F32), jax.ShapeDtypeStruct((L, 512), F32),
              jax.ShapeDtypeStruct((L, 1024), F32)]
    full = lambda shp: pl.BlockSpec(shp, lambda s: (0,) * len(shp))
    return _call(
        body, "ret_bwd", (n,),
        specs(fw) + specs(bw) + [full((2 * RET_HEADS, N_TAB, CHUNK, CHUNK))],
        ospecs(fw) + ospecs(bw) + [full((2 * RET_HEADS, LANES, RET_DV)), full((2 * RET_HEADS, LANES))],
        oshape + oshape + [jax.ShapeDtypeStruct((2 * RET_HEADS, LANES, RET_DV), F32),
                           jax.ShapeDtypeStruct((2 * RET_HEADS, LANES), F32)],
        [pltpu.VMEM((2 * RET_HEADS, LANES, RET_DV), F32)],
        ("arbitrary",), (P, P, P, rope, dO, stf, P, P, P, rope, dO, stb, tabs), comm)


def _ret_ctx_bwd(P, ctab, ds0, dlg, rdb, L, Lc):
    cb = L // Lc

    def body(k_ref, v_ref, c_ref, ds_ref, dlg_ref, rd_ref, dk_ref, dv_ref, drd_ref):
        for p in range(RET_HEADS // 2):
            kp = k_ref[:, p * LANES:(p + 1) * LANES].astype(F32) * K_SCALE
            dkp = jnp.zeros((Lc, LANES), F32)
            for a in range(2):
                h = 2 * p + a
                kh = jnp.where(_half_mask(kp.shape, a), kp, 0.0)
                vh = v_ref[:, h * RET_DV:(h + 1) * RET_DV]
                dvh = jnp.zeros((Lc, RET_DV), F32)
                for d in range(2):
                    r = d * RET_HEADS + h
                    dsb = ds_ref[r].astype(BF16)
                    cw, cwc = c_ref[r, 0], c_ref[r, 1]
                    y = _dot(vh, dsb, NT)
                    dkp += y * cw
                    dvh += _dot((kh * cw).astype(BF16), dsb, NN)
                    lg = -jnp.exp(rd_ref[r])
                    drd_ref[r:r + 1, :] = (dlg_ref[r:r + 1, :] + jnp.sum(kh * cwc * y)) * lg
                dv_ref[:, h * RET_DV:(h + 1) * RET_DV] = dvh
            dk_ref[:, p * LANES:(p + 1) * LANES] = dkp * K_SCALE

    full = lambda shp: pl.BlockSpec(shp, lambda i: (0,) * len(shp))
    return pl.pallas_call(
        body, name="ret_ctx_bwd", grid=(1,),
        in_specs=[pl.BlockSpec((Lc, 512), lambda i: (cb, C_RK // 512)),
                  pl.BlockSpec((Lc, 1024), lambda i: (cb, C_RV // 1024)),
                  full((2 * RET_HEADS, 2, Lc, LANES)), full((2 * RET_HEADS, LANES, RET_DV)),
                  full((2 * RET_HEADS, LANES)), full((2 * RET_HEADS, 1, LANES))],
        out_specs=[full((Lc, 512)), full((Lc, 1024)), full((2 * RET_HEADS, LANES))],
        out_shape=[jax.ShapeDtypeStruct((Lc, 512), F32), jax.ShapeDtypeStruct((Lc, 1024), F32),
                   jax.ShapeDtypeStruct((2 * RET_HEADS, LANES), F32)],
        compiler_params=_cparams(("arbitrary",)),
    )(P, P, ctab, ds0, dlg, rdb)


BLK = 128
N_LOC = 3 * BLK


def _att_inputs(P, rope, L, Lc):
    n = L // BLK
    cb = L // Lc
    prev = lambda i: jnp.maximum(i - 1, 0)
    nxt = lambda i: jnp.minimum(i + 1, n - 1)
    specs = [pl.BlockSpec((BLK, 1024), lambda i: (i, C_AQ // 1024))]
    args = [P]
    for col in (C_AK // 256, C_AV // 256):
        for rm in (prev, lambda i: i, nxt):
            specs.append(pl.BlockSpec((BLK, 256), functools.partial(lambda i, rm, col: (rm(i), col), rm=rm, col=col)))
            args.append(P)
        specs.append(pl.BlockSpec((Lc, 256), functools.partial(lambda i, col: (cb, col), col=col)))
        args.append(P)
    for rm in (prev, lambda i: i, nxt):
        specs.append(pl.BlockSpec((2, BLK, LANES), functools.partial(lambda i, rm: (0, rm(i), 0), rm=rm)))
        args.append(rope)
    return specs, args


def _att_prep(i, n, refs, Lc):
    q_ref, kp_ref, kc_ref, kn_ref, kx_ref, vp_ref, vc_ref, vn_ref, vx_ref, rp_ref, rc_ref, rn_ref = refs
    cos = jnp.concatenate([rp_ref[0], rc_ref[0], rn_ref[0]], axis=0)
    sin = jnp.concatenate([rp_ref[1], rc_ref[1], rn_ref[1]], axis=0)
    kd, vd = [], []
    for t in range(ATT_KV // 2):
        sl = slice(t * LANES, (t + 1) * LANES)
        kl = jnp.concatenate([kp_ref[:, sl], kc_ref[:, sl], kn_ref[:, sl]], axis=0).astype(F32)
        kl = _rope(kl, cos, sin, 16)
        ka = jnp.concatenate([kl, kx_ref[:, sl].astype(F32)], axis=0)
        va = jnp.concatenate([vp_ref[:, sl], vc_ref[:, sl], vn_ref[:, sl], vx_ref[:, sl]], axis=0).astype(F32)
        kr, vr = pltpu.roll(ka, 64, 1), pltpu.roll(va, 64, 1)
        for b in range(2):
            hm = _half_mask(ka.shape, b)
            kd.append(jnp.where(hm, ka, kr).astype(BF16))
            vd.append(jnp.where(hm, va, vr).astype(BF16))
    nk = N_LOC + Lc
    rr = lax.broadcasted_iota(jnp.int32, (BLK, nk), 0)
    ss = lax.broadcasted_iota(jnp.int32, (BLK, nk), 1)
    lo = jnp.where(i == 0, BLK, 0)
    hi = jnp.where(i == n - 1, 2 * BLK, N_LOC)
    valid = (ss >= N_LOC) | ((ss >= rr) & (ss <= rr + 2 * BLK) & (ss >= lo) & (ss < hi))
    bias = jnp.where(valid, 0.0, NEG)
    return kd, vd, jnp.concatenate([bias] * 4, axis=0), rc_ref[0], rc_ref[1]


LOG2E = 1.4426950408889634
LN2 = 0.6931471805599453
Q_SCALE = A_SCALE * LOG2E


def _stack4(ref, g, f=None):
    parts = []
    for jp in range(2):
        t = ref[:, (2 * g + jp) * LANES:(2 * g + jp + 1) * LANES].astype(F32)
        if f is not None:
            t = f(t)
        for a in range(2):
            parts.append(jnp.where(_half_mask(t.shape, a), t, 0.0))
    return jnp.concatenate(parts, axis=0)


def _unstack4(x4, jp):
    r0 = 2 * jp * BLK
    lo = x4[r0:r0 + BLK]
    hi = x4[r0 + BLK:r0 + 2 * BLK]
    return jnp.where(_half_mask(lo.shape, 0), lo, hi)


def _softmax_parts(s, bias4, sink_ref, g):
    sink_col = LOG2E * jnp.concatenate(
        [jnp.zeros((BLK, 1), F32) + sink_ref[4 * g + r:4 * g + r + 1, 0:1] for r in range(4)], axis=0)
    s = s + bias4
    m = jnp.maximum(jnp.max(s, axis=-1, keepdims=True), sink_col)
    e = jnp.exp2(s - m)
    es = jnp.exp2(sink_col - m)
    return e, es, jnp.sum(e, axis=-1, keepdims=True) + es


def _att_fwd(P, rope, sinkb, Y, L, Lc, comm=()):
    n = L // BLK
    specs, args = _att_inputs(P, rope, L, Lc)

    def body(*refs):
        sink_ref, o_ref = refs[12], refs[14]
        i = pl.program_id(0)
        kd, vd, bias4, cq, sq = _att_prep(i, n, refs[:12], Lc)
        def raw_scores(g):
            q4 = _stack4(refs[0], g, lambda t: _rope(t, cq, sq, 16) * Q_SCALE).astype(BF16)
            return _dot(q4, kd[g], NT)

        s_next = raw_scores(0)
        for g in range(ATT_KV):
            s = s_next
            if g + 1 < ATT_KV:
                s_next = raw_scores(g + 1)
            e, _, l = _softmax_parts(s, bias4, sink_ref, g)
            o4 = _dot(e.astype(BF16), vd[g], NN) * (1.0 / l)
            for jp in range(2):
                c0 = (2 * g + jp) * LANES
                o_ref[:, c0:c0 + LANES] = _unstack4(o4, jp).astype(o_ref.dtype)

    return _call(
        body, "att_fwd", (n,),
        specs + [pl.BlockSpec((ATT_HEADS, LANES), lambda i: (0, 0)), pl.BlockSpec(memory_space=pl.ANY)],
        [pl.BlockSpec((BLK, 1024), lambda i: (i, 1))], [jax.ShapeDtypeStruct((L, 2048), BF16)], [],
        ("parallel",), (*args, sinkb, Y), comm, aliases={13: 0})


def _att_bwd(P, rope, sinkb, Y, dY, L, Lc, comm=()):
    n = L // BLK
    specs, args = _att_inputs(P, rope, L, Lc)
    nk = N_LOC + Lc

    def body(*refs):
        sink_ref, y_ref, dy_ref = refs[12], refs[13], refs[14]
        dq_ref, dkl_ref, dvl_ref, dkx_ref, dvx_ref, dsk_ref = refs[15:21]
        i = pl.program_id(0)

        @pl.when(i == 0)
        def _():
            dkx_ref[...] = jnp.zeros_like(dkx_ref)
            dvx_ref[...] = jnp.zeros_like(dvx_ref)
            dsk_ref[...] = jnp.zeros_like(dsk_ref)

        kd, vd, bias4, cq, sq = _att_prep(i, n, refs[:12], Lc)
        def first_matmuls(g):
            q4 = _stack4(refs[0], g, lambda x: _rope(x, cq, sq, 16) * Q_SCALE).astype(BF16)
            do4 = _stack4(dy_ref, g)
            delta = jnp.sum(do4 * _stack4(y_ref, g), axis=-1, keepdims=True)
            do4b = do4.astype(BF16)
            return q4, do4b, delta, _dot(q4, kd[g], NT), _dot(do4b, vd[g], NT)

        nxt = first_matmuls(0)
        for t in range(ATT_KV // 2):
            dk_halves, dv_halves = [], []
            for b in range(2):
                g = 2 * t + b
                q4, do4b, delta, s, dpm = nxt
                if g + 1 < ATT_KV:
                    nxt = first_matmuls(g + 1)
                e, es, l = _softmax_parts(s, bias4, sink_ref, g)
                inv = 1.0 / l
                p = e * inv
                dsc = (p * (dpm - delta)).astype(BF16)
                dsr = es * inv * delta
                for r in range(4):
                    h = 4 * g + r
                    dsk_ref[h:h + 1, :] += jnp.zeros((1, LANES), F32) - jnp.sum(dsr[r * BLK:(r + 1) * BLK])
                dq4 = _dot(dsc, kd[g], NN) * A_SCALE
                for jp in range(2):
                    c0 = (2 * g + jp) * LANES
                    dq_ref[:, c0:c0 + LANES] = _rope_t(_unstack4(dq4, jp), cq, sq, 16).astype(dq_ref.dtype)
                dkd = _dot(q4, dsc, TN) * LN2
                dvd = _dot(do4b, p.astype(BF16), TN)
                dk_halves.append(dkd[:ATT_DH] + dkd[ATT_DH:])
                dv_halves.append(dvd[:ATT_DH] + dvd[ATT_DH:])
            dk_t = jnp.concatenate(dk_halves, axis=0).T
            dv_t = jnp.concatenate(dv_halves, axis=0).T
            sl = slice(t * LANES, (t + 1) * LANES)
            dkl_ref[0, :, sl] = dk_t[:N_LOC]
            dvl_ref[0, :, sl] = dv_t[:N_LOC]
            dkx_ref[:, sl] += dk_t[N_LOC:]
            dvx_ref[:, sl] += dv_t[N_LOC:]

    row = pl.BlockSpec((BLK, 1024), lambda i: (i, 0))
    loc = pl.BlockSpec((1, N_LOC, 256), lambda i: (i, 0, 0))
    cx = pl.BlockSpec((Lc, 256), lambda i: (0, 0))
    return _call(
        body, "att_bwd", (n,),
        specs + [pl.BlockSpec((ATT_HEADS, LANES), lambda i: (0, 0))] + [pl.BlockSpec((BLK, 1024), lambda i: (i, 1))] * 2,
        [row, loc, loc, cx, cx, pl.BlockSpec((ATT_HEADS, LANES), lambda i: (0, 0))],
        [jax.ShapeDtypeStruct((L, 1024), BF16), jax.ShapeDtypeStruct((n, N_LOC, 256), F32),
         jax.ShapeDtypeStruct((n, N_LOC, 256), F32), jax.ShapeDtypeStruct((Lc, 256), F32),
         jax.ShapeDtypeStruct((Lc, 256), F32), jax.ShapeDtypeStruct((ATT_HEADS, LANES), F32)], [],
        ("arbitrary",), (*args, sinkb, Y, dY), comm)


def _assemble_dp(L, Lc, dqf, dqb, dkf, dkb, dvf, dvb, drg, daq, dkl, dvl, rope_att, dck, dcv, dkx, dvx):
    n = L // BLK
    nc = Lc // BLK

    def body(dqf_r, dqb_r, dkf_r, dkb_r, dvf_r, dvb_r, drg_r, daq_r, kl0, kl1, kl2, vl0, vl1, vl2, rp_r,
             dck_r, dcv_r, dkx_r, dvx_r, o_ref):
        i = pl.program_id(0)

        @pl.when(i < n)
        def _():
            o_ref[:, C_RQ:C_RK] = (dqf_r[...] + dqb_r[...]).astype(o_ref.dtype)
            o_ref[:, C_RK:C_RV] = (dkf_r[...] + dkb_r[...]).astype(o_ref.dtype)
            o_ref[:, C_RV:C_RG] = (dvf_r[...] + dvb_r[...]).astype(o_ref.dtype)
            o_ref[:, C_RG:C_AQ] = drg_r[...].astype(o_ref.dtype)
            o_ref[:, C_AQ:C_AK] = daq_r[...].astype(o_ref.dtype)
            w0 = jnp.where(i > 0, 1.0, 0.0)
            w2 = jnp.where(i < n - 1, 1.0, 0.0)
            dk = kl0[0] * w0 + kl1[0] + kl2[0] * w2
            dv = vl0[0] * w0 + vl1[0] + vl2[0] * w2
            for t in range(ATT_KV // 2):
                sl = slice(t * LANES, (t + 1) * LANES)
                o_ref[:, C_AK + t * LANES:C_AK + (t + 1) * LANES] = _rope_t(
                    dk[:, sl], rp_r[0], rp_r[1], 16).astype(o_ref.dtype)
            o_ref[:, C_AV:D_PROJ] = dv.astype(o_ref.dtype)

        @pl.when(i >= n)
        def _():
            o_ref[:, C_RQ:C_RK] = jnp.zeros((BLK, C_RK - C_RQ), o_ref.dtype)
            o_ref[:, C_RK:C_RV] = dck_r[...].astype(o_ref.dtype)
            o_ref[:, C_RV:C_RG] = dcv_r[...].astype(o_ref.dtype)
            o_ref[:, C_RG:C_AK] = jnp.zeros((BLK, C_AK - C_RG), o_ref.dtype)
            o_ref[:, C_AK:C_AV] = dkx_r[...].astype(o_ref.dtype)
            o_ref[:, C_AV:D_PROJ] = dvx_r[...].astype(o_ref.dtype)

    xm = lambda i: jnp.minimum(i, n - 1)
    cm = lambda i: jnp.clip(i - n, 0, nc - 1)
    r512 = pl.BlockSpec((BLK, 512), lambda i: (xm(i), 0))
    r1024 = pl.BlockSpec((BLK, 1024), lambda i: (xm(i), 0))
    part = lambda off: pl.BlockSpec((1, BLK, 256), lambda i: (jnp.clip(xm(i) + off, 0, n - 1), 1 - off, 0))
    return pl.pallas_call(
        body, name="assemble_dp", grid=(n + nc,),
        in_specs=[r512, r512, r512, r512, r1024, r1024, r1024, r1024,
                  part(-1), part(0), part(1), part(-1), part(0), part(1),
                  pl.BlockSpec((2, BLK, LANES), lambda i: (0, xm(i), 0)),
                  pl.BlockSpec((BLK, 512), lambda i: (cm(i), 0)), pl.BlockSpec((BLK, 1024), lambda i: (cm(i), 0)),
                  pl.BlockSpec((BLK, 256), lambda i: (cm(i), 0)), pl.BlockSpec((BLK, 256), lambda i: (cm(i), 0))],
        out_specs=pl.BlockSpec((BLK, D_PROJ), lambda i: (i, 0)),
        out_shape=jax.ShapeDtypeStruct((L + Lc, D_PROJ), BF16),
        compiler_params=_cparams(("parallel",)),
    )(dqf, dqb, dkf, dkb, dvf, dvb, drg, daq, dkl, dkl, dkl, dvl, dvl, dvl, rope_att, dck, dcv, dkx, dvx)


def _adam_math(w, g, m, v):
    m = ADAM_B1 * m + (1.0 - ADAM_B1) * g
    v = ADAM_B2 * v + (1.0 - ADAM_B2) * (g * g)
    m_hat = m / (1.0 - ADAM_B1 ** ADAM_STEP)
    v_hat = v / (1.0 - ADAM_B2 ** ADAM_STEP)
    delta = -ADAM_LR * (m_hat / (jnp.sqrt(v_hat) + ADAM_EPS) + ADAM_WD * w)
    return delta, m, v


def _adam(name, w, m, v, g=None, parts=None):
    R, C = w.shape
    tr = _pick(R, (256, 128, 64, 32, 16, 8))
    summed = parts is not None
    n_parts = parts.shape[0] if summed else 0

    def body(w_ref, m_ref, v_ref, g_ref, go_ref, d_ref, mo_ref, vo_ref):
        if summed:
            gv = g_ref[0].astype(F32)
            for j in range(1, n_parts):
                gv = gv + g_ref[j].astype(F32)
        else:
            gv = g_ref[...]
        d, mn, vn = _adam_math(w_ref[...], gv, m_ref[...], v_ref[...])
        go_ref[...] = gv
        d_ref[...] = d
        mo_ref[...] = mn
        vo_ref[...] = vn

    row = pl.BlockSpec((tr, C), lambda i: (i, 0))
    gspec = pl.BlockSpec((n_parts, tr, C), lambda i: (0, i, 0)) if summed else row
    return pl.pallas_call(
        body, name=name, grid=(R // tr,),
        in_specs=[row, row, row, gspec], out_specs=[row] * 4,
        out_shape=[jax.ShapeDtypeStruct((R, C), F32)] * 4,
        compiler_params=_cparams(("parallel",)),
    )(w, m, v, parts if summed else g)


def _rows_full(g):
    _, R, D = g.shape
    return g.reshape(N_DEV * R, D)


def _rows_slots(g):
    N, D = g.shape
    return g.reshape(N_DEV, N // N_DEV, D)


def _pad_rows(a, rows):
    return jnp.concatenate([a, jnp.zeros((rows - a.shape[0],) + a.shape[1:], a.dtype)], axis=0)


def kernel(x, c, ctx, c_ctx, w_mod, b_mod, norm_mix, norm_ffn, w_in, ret_decay, attn_sink, w_out, w_gate, w_up, w_down, norm_final, loss_target, m_c_ctx, m_w_mod, m_b_mod, m_norm_mix, m_norm_ffn, m_w_in, m_ret_decay, m_attn_sink, m_w_out, m_w_gate, m_w_up, m_w_down, m_norm_final, v_c_ctx, v_w_mod, v_b_mod, v_norm_mix, v_norm_ffn, v_w_in, v_ret_decay, v_attn_sink, v_w_out, v_w_gate, v_w_up, v_w_down, v_norm_final):
    L, D = x.shape[1], x.shape[2]
    Lc = ctx.shape[1]
    DF = w_gate.shape[2] * N_DEV
    C6 = w_mod.shape[2]
    me = _my_id()
    xs, cx, tgt = x[0], ctx[0], loss_target[0]

    ag_in = ("ag2", w_in[0].T.astype(BF16))
    ag_out, ag_gate = ("ag2", w_out[0].astype(BF16)), ("ag2", w_gate[0].T.astype(BF16))
    ag_up, ag_down = ("ag2", w_up[0].T.astype(BF16)), ("ag2", w_down[0].astype(BF16))

    cs = _allgather(c, "ag_c")[:, 0, :]
    s_in = _pad_rows(jnp.concatenate([cs, c_ctx[None, :]], axis=0), 16)
    b_l = lax.dynamic_slice_in_dim(b_mod, me * C6, C6, axis=1)
    mod_parts = _allgather(_mod_fwd(s_in, w_mod[0], b_l), "ag_mod")
    mod = _pad_rows(lax.dynamic_index_in_dim(mod_parts, me, axis=1, keepdims=False).reshape(6, D), 8)
    modc = _pad_rows(mod_parts[:, N_DEV, :].reshape(6, D), 8)
    mix_mod, ffn_mod = mod, jnp.roll(mod, -3, axis=0)
    gt_m, gt_f = mod[2:3], mod[5:6]

    rope_ret, rope_att = _rope_tables(L)
    rdb = jnp.broadcast_to(ret_decay[0].reshape(2 * RET_HEADS, 1, 1), (2 * RET_HEADS, 1, LANES))
    sinkb = jnp.broadcast_to(attn_sink[0].reshape(ATT_HEADS, 1), (ATT_HEADS, LANES))

    tm = _pick(L + Lc, (1408, 768, 512, 384, 256, 128))
    tmx = _pick(L, (1024, 512, 256, 128))

    (H,), (g_in,) = _modulate_fwd("mod_mix_fwd", xs, cx, norm_mix, mix_mod, modc, comm=[ag_in])
    W_inT = _rows_full(g_in)
    ident = lambda a, e: a
    tP, tD, tF = _pick(D_PROJ, (1152, 768, 512)), _pick(D, (2048, 1024, 512)), _pick(DF, (512, 256, 128))
    (P,), (g_gate,) = _matmul("mm_in", [(H, W_inT, 0)], 1, L + Lc, D_PROJ, D, "nt", (tm, tP, D), [], [BF16], ident,
                              comm=[ag_gate])
    W_gateT = _rows_full(g_gate)
    tabs, ctab = _ret_tables(rdb, Lc)
    s0 = _ret_ctx_state(P, ctab, L, Lc)
    (o_f, o_b, st_f, st_b), (g_out,) = _ret_fwd(P, rope_ret, tabs, s0, L, comm=[ag_out])
    W_out = _rows_full(g_out)
    Y_half = _ret_finish_fwd(o_f, o_b, P, L)
    (Y,), (g_up,) = _att_fwd(P, rope_att, sinkb, Y_half, L, Lc, comm=[ag_up])
    W_upT = _rows_full(g_up)
    KO = Y.shape[1]
    f_mix = _matmul("mm_out", [(Y, W_out, 0)], 1, L, D, KO, "nn", (tmx, tD, KO), [], [BF16], ident)[0]

    x1, H2 = _residual_modulate_fwd("mod_ffn_fwd", xs, f_mix, gt_m, norm_ffn, ffn_mod)

    def swiglu_epi(a, e):
        sg = _sigmoid(a[0])
        act = a[0] * sg
        return [act, a[1] * (sg * (1.0 + a[0] * (1.0 - sg))), act * a[1]]

    tm2 = tmx
    (act, up_dact, hmid), (g_down,) = _matmul("mm_gate_up", [(H2, W_gateT, 0), (H2, W_upT, 1)], 2, L, DF, D, "nt",
                                              (tm2, tF, D), [], [BF16, BF16, BF16], swiglu_epi, comm=[ag_down])
    W_down = _rows_full(g_down)
    f_ffn = _matmul("mm_down", [(hmid, W_down, 0)], 1, L, D, DF, "nn", (tm2, tD, _pick(DF, (1408, 512, 256, 128))),
                    [], [BF16], ident)[0]

    dx2, dFf, sums_l = _loss_head(x1, tgt, norm_final.reshape(1, D), f_ffn, gt_f)

    def dswiglu_epi(a, e):
        return [a[0] * e[0].astype(F32), a[0] * e[1].astype(F32)]

    dga, dup = _matmul("mm_d_down", [(dFf, W_down, 0)], 1, L, DF, D, "nt", (tm2, tF, D),
                       [(up_dact, "mn"), (act, "mn")], [BF16, BF16], dswiglu_epi)
    tkt, tkl = _pick(L, (512, 256, 128)), _pick(L, (1024, 512, 256, 128))
    dW_down =_matmul("mm_gw_down", [(hmid, dFf, 0)], 1, DF, D, L, "tn",
                      (_pick(DF, (1408, 512, 256, 128)), tD, tkt), [], [BF16], ident)[0]
    (dW_gateT, dW_upT), (p_down,) = _matmul("mm_gw_gate_up", [(dga, H2, 0), (dup, H2, 1)], 2, DF, D, L, "tn",
                                            (tF, tD, tkl), [], [BF16, BF16], ident,
                                            comm=[("a2a", _rows_slots(dW_down))])
    (dH2,), (p_gate,) = _matmul("mm_d_gate_up", [(dga, W_gateT, 0), (dup, W_upT, 0)], 1, L, D, DF, "nn",
                                (tm2, tD, tF), [], [BF16], ident, comm=[("a2a", _rows_slots(dW_gateT))])
    dx1, dFm, sums_f = _modulate_bwd("mod_ffn_bwd", x1, None, dH2, norm_ffn, ffn_mod, None, dx2, f_mix, gt_m)

    tO = _pick(KO, (2048, 1024, 512))
    dY = _matmul("mm_d_out", [(dFm, W_out, 0)], 1, L, KO, D, "nt", (tmx, tO, D), [], [BF16], ident)[0]
    dW_out = _matmul("mm_gw_out", [(Y, dFm, 0)], 1, KO, D, L, "tn", (_pick(KO, (1024, 512)), tD, tkl), [], [BF16],
                     ident)[0]
    dO, drg = _ret_finish_bwd(o_f, o_b, P, dY, L)
    (dqf, dkf, dvf, dqb, dkb, dvb, ds0, dlg), (p_out,) = _ret_bwd(
        P, rope_ret, tabs, st_f, st_b, dO, L, comm=[("a2a", _rows_slots(dW_out))])
    dck, dcv, d_rd = _ret_ctx_bwd(P, ctab, ds0, dlg, rdb, L, Lc)
    (daq, dkl, dvl, dkx, dvx, d_sink), (p_up,) = _att_bwd(
        P, rope_att, sinkb, Y, dY, L, Lc, comm=[("a2a", _rows_slots(dW_upT))])
    dP = _assemble_dp(L, Lc, dqf, dqb, dkf, dkb, dvf, dvb, drg, daq, dkl, dvl, rope_att, dck, dcv, dkx, dvx)
    tkc = _pick(L + Lc, (1408, 768, 384, 256, 128))
    dW_inT = _matmul("mm_gw_in", [(dP, H, 0)], 1, D_PROJ, D, L + Lc, "tn", (tP, tD, tkc), [], [BF16], ident)[0]
    (dH,), (p_in,) = _matmul("mm_d_in", [(dP, W_inT, 0)], 1, L + Lc, D, D_PROJ, "nn",
                             (tm, tD, _pick(D_PROJ, (1152, 512, 256))), [], [BF16], ident,
                             comm=[("a2a", _rows_slots(dW_inT))])
    grad_x, sums_m = _modulate_bwd("mod_mix_bwd", xs, cx, dH, norm_mix, mix_mod, modc, dx1, None, None)

    zero = jnp.zeros((1, D), F32)
    dmod = jnp.concatenate([sums_m[0:1], sums_m[1:2], sums_f[6:7], sums_f[0:1], sums_f[1:2], sums_l[2:3]], axis=1)
    dmodc = jnp.concatenate([sums_m[3:4], sums_m[4:5], zero, zero, zero, zero], axis=1)
    dm_all = _allgather(jnp.concatenate([dmod, dmodc], axis=0), "ag_dmod")
    dm_cols = lax.dynamic_slice_in_dim(dm_all, me * C6, C6, axis=2)
    dm_in = jnp.concatenate([dm_cols[:, 0, :], dm_cols[:, 1, :]], axis=0)
    s_bwd = jnp.concatenate([cs, jnp.broadcast_to(c_ctx[None, :], (N_DEV, D))], axis=0)
    g_w_mod, dsil = _mod_bwd(s_bwd, dm_in, w_mod[0])

    lane_pad = lambda a: _pad_rows(a.reshape(-1, 1), LANES).reshape(1, LANES)
    pack = jnp.concatenate([dsil[0:1], sums_m[2:3], sums_f[2:3], sums_l[1:2],
                            lane_pad(d_rd[:, 0]), lane_pad(d_sink[:, 0]), sums_l[3:4, 0:LANES]], axis=1)
    packs = _allgather(pack, "ag_small")
    zl = jnp.zeros((1, LANES), F32)

    def pack_w(a_c, a_nm, a_nf, a_fin, a_rd, a_sk):
        return jnp.concatenate([a_c.reshape(1, D), a_nm, a_nf, a_fin.reshape(1, D), lane_pad(a_rd.reshape(-1)),
                                lane_pad(a_sk.reshape(-1)), zl], axis=1)

    sg, sd, sm, sv = _adam("adam_small", pack_w(c_ctx, norm_mix, norm_ffn, norm_final, ret_decay, attn_sink),
                           pack_w(m_c_ctx, m_norm_mix, m_norm_ffn, m_norm_final, m_ret_decay, m_attn_sink),
                           pack_w(v_c_ctx, v_norm_mix, v_norm_ffn, v_norm_final, v_ret_decay, v_attn_sink),
                           parts=packs)
    loss = sg[0, 4 * D + 2 * LANES]

    def unpack(a):
        return (a[0, 0:D], a[:, D:2 * D], a[:, 2 * D:3 * D], a[0, 3 * D:4 * D],
                a[0, 4 * D:4 * D + 2 * RET_HEADS].reshape(1, 2, RET_HEADS),
                a[:, 4 * D + LANES:4 * D + LANES + ATT_HEADS])

    bg, bd, bm, bv = _adam("adam_b_mod", b_mod, m_b_mod, v_b_mod, parts=dm_all.reshape(2 * N_DEV, 1, 6 * D))
    wg, wd, wm, wv = _adam("adam_w_mod", w_mod[0], m_w_mod[0], v_w_mod[0], g=g_w_mod)

    big = {}
    for nm, w, m, v, parts, transposed in (
            ("w_in", w_in, m_w_in, v_w_in, p_in, True), ("w_out", w_out, m_w_out, v_w_out, p_out, False),
            ("w_gate", w_gate, m_w_gate, v_w_gate, p_gate, True), ("w_up", w_up, m_w_up, v_w_up, p_up, True),
            ("w_down", w_down, m_w_down, v_w_down, p_down, False)):
        if transposed:
            res = [a.T for a in _adam("adam_" + nm, w[0].T, m[0].T, v[0].T, parts=parts)]
        else:
            res = _adam("adam_" + nm, w[0], m[0], v[0], parts=parts)
        big[nm] = [a[None] for a in res]

    g_s, d_s, m_s, v_s = unpack(sg), unpack(sd), unpack(sm), unpack(sv)

    def leaves(k, small, bmod, wmod):
        return (small[0], wmod[None], bmod, small[1], small[2], big["w_in"][k], small[4], small[5],
                big["w_out"][k], big["w_gate"][k], big["w_up"][k], big["w_down"][k], small[3])

    return (loss, grad_x[None], *leaves(0, g_s, bg, wg), *leaves(1, d_s, bd, wd),
            *leaves(2, m_s, bm, wm), *leaves(3, v_s, bv, wv))
```

```python
import functools

import jax
import jax.numpy as jnp
from jax import lax
from jax.experimental import pallas as pl
from jax.experimental.pallas import tpu as pltpu

F32 = jnp.float32
BF16 = jnp.bfloat16

N_DEV = 8
LANES = 128
RET_HEADS = 8
RET_DK = 64
RET_DV = 128
CHUNK = 128
ATT_HEADS = 16
ATT_KV = 4
ATT_DH = 64
GRID_W = 64
ROPE_BASE = 10000.0
EPS = 1e-6
NEG = -1e30
C_RQ, C_RK, C_RV, C_RG, C_AQ, C_AK, C_AV, D_PROJ = 0, 512, 1024, 2048, 3072, 4096, 4352, 4608
K_SCALE = RET_DK ** -0.5
A_SCALE = ATT_DH ** -0.5

ADAM_LR, ADAM_B1, ADAM_B2, ADAM_EPS, ADAM_WD, ADAM_STEP = 0.001, 0.9, 0.999, 1e-08, 0.01, 10

VMEM_BIG = 52 * 1024 * 1024

NN = (((1,), (0,)), ((), ()))
NT = (((1,), (1,)), ((), ()))
TN = (((0,), (0,)), ((), ()))


def _dot(a, b, dims):
    return lax.dot_general(a, b, dims, preferred_element_type=F32)


def _cparams(sem, vmem=VMEM_BIG):
    return pltpu.CompilerParams(dimension_semantics=sem, vmem_limit_bytes=vmem)


def _pick(dim, prefs):
    for p in prefs:
        if dim % p == 0:
            return p
    return dim


def _my_id():
    return lax.axis_index("x") * 4 + lax.axis_index("y") * 2 + lax.axis_index("c")


def _sigmoid(x):
    return 0.5 * jnp.tanh(0.5 * x) + 0.5


def _peers():
    mx, my, mc = lax.axis_index("x"), lax.axis_index("y"), lax.axis_index("c")
    out = []
    for k in range(1, N_DEV):
        kx, ky, kc = (k >> 2) & 1, (k >> 1) & 1, k & 1
        px = 1 - mx if kx else mx
        py = 1 - my if ky else my
        pc = 1 - mc if kc else mc
        out.append(((px, py, pc), px * 4 + py * 2 + pc))
    return out


def _exchange_copies(kind, x_ref, o_ref, ssem, rsem, lsem):
    me = _my_id()
    loc = pltpu.make_async_copy(x_ref if kind == "ag" else x_ref.at[me], o_ref.at[me], lsem)
    cps = []
    for k, (peer, pid) in enumerate(_peers()):
        cps.append(pltpu.make_async_remote_copy(
            src_ref=x_ref if kind == "ag" else x_ref.at[pid], dst_ref=o_ref.at[me],
            send_sem=ssem.at[k], recv_sem=rsem.at[k], device_id=peer, device_id_type=pl.DeviceIdType.MESH))
    return loc, cps


def _two_level_copies(x_ref, o_ref, ssem, rsem, lsem):
    mx, my, mc = lax.axis_index("x"), lax.axis_index("y"), lax.axis_index("c")
    me = mx * 4 + my * 2 + mc
    sibling = (mx, my, 1 - mc)
    chips = [(1 - mx, my), (mx, 1 - my), (1 - mx, 1 - my)]

    def copy(k, slot, to, src=None):
        return pltpu.make_async_remote_copy(
            src_ref=o_ref.at[slot] if src is None else src, dst_ref=o_ref.at[slot],
            send_sem=ssem.at[k], recv_sem=rsem.at[k], device_id=to, device_id_type=pl.DeviceIdType.MESH)

    loc = pltpu.make_async_copy(x_ref, o_ref.at[me], lsem)
    first = [copy(0, me, sibling, src=x_ref)]
    first += [copy(1 + j, me, (cx, cy, mc), src=x_ref) for j, (cx, cy) in enumerate(chips)]
    passed = [copy(4 + j, cx * 4 + cy * 2 + mc, sibling) for j, (cx, cy) in enumerate(chips)]
    return loc, first, passed


def _exchange_start(kind, x_ref, o_ref, ssem, rsem, lsem):
    if kind == "ag2":
        loc, first, _ = _two_level_copies(x_ref, o_ref, ssem, rsem, lsem)
        cps = first
    else:
        loc, cps = _exchange_copies(kind, x_ref, o_ref, ssem, rsem, lsem)
    loc.start()
    for cp in cps:
        cp.start()


def _exchange_wait(kind, x_ref, o_ref, ssem, rsem, lsem):
    if kind == "ag2":
        loc, first, passed = _two_level_copies(x_ref, o_ref, ssem, rsem, lsem)
        for j in range(3):
            first[1 + j].wait_recv()
            passed[j].start()
        first[0].wait_recv()
        for cp in passed:
            cp.wait_recv()
        cps = first + passed
    else:
        loc, cps = _exchange_copies(kind, x_ref, o_ref, ssem, rsem, lsem)
        for cp in cps:
            cp.wait_recv()
    for cp in cps:
        cp.wait_send()
    loc.wait()


_EXCHANGE_SEMS = [pltpu.SemaphoreType.DMA((N_DEV - 1,)), pltpu.SemaphoreType.DMA((N_DEV - 1,)),
                  pltpu.SemaphoreType.DMA(())]


def _exchange_shape(kind, x):
    return jax.ShapeDtypeStruct(x.shape if kind == "a2a" else (N_DEV,) + x.shape, x.dtype)


def _exchange(kind, x, name):
    def body(x_ref, o_ref, ssem, rsem, lsem):
        _exchange_start(kind, x_ref, o_ref, ssem, rsem, lsem)
        _exchange_wait(kind, x_ref, o_ref, ssem, rsem, lsem)

    return pl.pallas_call(
        body, name=name, out_shape=_exchange_shape(kind, x),
        in_specs=[pl.BlockSpec(memory_space=pl.ANY)], out_specs=pl.BlockSpec(memory_space=pl.ANY),
        scratch_shapes=list(_EXCHANGE_SEMS),
    )(x)


def _allgather(x, name):
    return _exchange("ag", x, name)


def _call(body, name, grid, in_specs, out_specs, out_shape, scratch_shapes, sem, args, comm=(), aliases=None):
    in_specs, out_specs, out_shape = list(in_specs), list(out_specs), list(out_shape)
    scratch_shapes = list(scratch_shapes)
    aliases = aliases or {}
    if not comm:
        outs = pl.pallas_call(body, name=name, grid=grid, in_specs=in_specs, out_specs=out_specs, out_shape=out_shape,
                              scratch_shapes=scratch_shapes, input_output_aliases=aliases,
                              compiler_params=_cparams(sem))(*args)
        return list(outs), []
    n_in, n_out, n_scr, n_c = len(in_specs), len(out_specs), len(scratch_shapes), len(comm)
    hbm = pl.BlockSpec(memory_space=pl.ANY)

    def wrapped(*refs):
        ins, cins = refs[:n_in], refs[n_in:n_in + n_c]
        outs = refs[n_in + n_c:n_in + n_c + n_out]
        couts = refs[n_in + n_c + n_out:n_in + 2 * n_c + n_out]
        scr = refs[n_in + 2 * n_c + n_out:n_in + 2 * n_c + n_out + n_scr]
        sems = refs[n_in + 2 * n_c + n_out + n_scr:]
        first = pl.program_id(0) == 0
        last = pl.program_id(0) == grid[0] - 1
        for ax in range(1, len(grid)):
            first = first & (pl.program_id(ax) == 0)
            last = last & (pl.program_id(ax) == grid[ax] - 1)

        @pl.when(first)
        def _():
            for c, (kind, _) in enumerate(comm):
                _exchange_start(kind, cins[c], couts[c], *sems[3 * c:3 * c + 3])

        body(*ins, *outs, *scr)

        @pl.when(last)
        def _():
            for c, (kind, _) in enumerate(comm):
                _exchange_wait(kind, cins[c], couts[c], *sems[3 * c:3 * c + 3])

    res = pl.pallas_call(
        wrapped, name=name, grid=grid,
        in_specs=in_specs + [hbm] * n_c, out_specs=out_specs + [hbm] * n_c,
        out_shape=out_shape + [_exchange_shape(kind, arr) for kind, arr in comm],
        scratch_shapes=scratch_shapes + list(_EXCHANGE_SEMS) * n_c, input_output_aliases=aliases,
        compiler_params=_cparams(("arbitrary",) * len(grid)),
    )(*args, *[arr for _, arr in comm])
    return list(res[:n_out]), list(res[n_out:])


def _matmul(name, pairs, n_acc, M, N, K, mode, tiles, extras, out_dtypes, epilogue, j_outer=False, comm=()):
    tm, tn, tk = tiles
    gm, gn, nk = M // tm, N // tn, K // tk
    assert gm * tm == M and gn * tn == N and nk * tk == K, (name, M, N, K, tiles)
    if j_outer:
        grid = (gn, gm, nk)
        ij = lambda g0, g1: (g1, g0)
    else:
        grid = (gm, gn, nk)
        ij = lambda g0, g1: (g0, g1)

    if mode in ("nn", "nt"):
        a_spec = pl.BlockSpec((tm, tk), lambda g0, g1, k: (ij(g0, g1)[0], k))
    else:
        a_spec = pl.BlockSpec((tk, tm), lambda g0, g1, k: (k, ij(g0, g1)[0]))
    if mode == "nt":
        b_spec = pl.BlockSpec((tn, tk), lambda g0, g1, k: (ij(g0, g1)[1], k))
    else:
        b_spec = pl.BlockSpec((tk, tn), lambda g0, g1, k: (k, ij(g0, g1)[1]))
    dims = {"nn": NN, "nt": NT, "tn": TN}[mode]
    mn_spec = pl.BlockSpec((tm, tn), lambda g0, g1, k: ij(g0, g1))
    n_spec = pl.BlockSpec((1, tn), lambda g0, g1, k: (0, ij(g0, g1)[1]))

    in_specs, args = [], []
    for a, b, _ in pairs:
        in_specs += [a_spec, b_spec]
        args += [a, b]
    for arr, kind in extras:
        in_specs.append(mn_spec if kind == "mn" else n_spec)
        args.append(arr)
    n_p, n_e, n_o = len(pairs), len(extras), len(out_dtypes)

    def body(*refs):
        ab = refs[:2 * n_p]
        ex = refs[2 * n_p:2 * n_p + n_e]
        outs = refs[2 * n_p + n_e:2 * n_p + n_e + n_o]
        accs = refs[2 * n_p + n_e + n_o:]
        k = pl.program_id(2)

        def partial_sums():
            sums = [None] * n_acc
            for p, (_, _, ai) in enumerate(pairs):
                d = _dot(ab[2 * p][...], ab[2 * p + 1][...], dims)
                sums[ai] = d if sums[ai] is None else sums[ai] + d
            return sums

        def finish(acc_vals):
            res = epilogue(acc_vals, [e[...] for e in ex])
            for o, r in zip(outs, res):
                o[...] = r.astype(o.dtype)

        def accumulate(first):
            w = _pick(tm if mode == "tn" else tn, (512, 384, 256))
            for c in range((tm if mode == "tn" else tn) // w):
                sl = slice(c * w, (c + 1) * w)
                sums = [None] * n_acc
                for p, (_, _, ai) in enumerate(pairs):
                    a_ref, b_ref = ab[2 * p], ab[2 * p + 1]
                    if mode == "tn":
                        d = _dot(a_ref[:, sl], b_ref[...], dims)
                    elif mode == "nn":
                        d = _dot(a_ref[...], b_ref[:, sl], dims)
                    else:
                        d = _dot(a_ref[...], b_ref[sl, :], dims)
                    sums[ai] = d if sums[ai] is None else sums[ai] + d
                idx = (sl, slice(None)) if mode == "tn" else (slice(None), sl)
                for ai, s in enumerate(sums):
                    if first:
                        accs[ai][idx] = s
                    else:
                        accs[ai][idx] += s

        if nk == 1:
            finish(partial_sums())
        else:
            pl.when(k == 0)(functools.partial(accumulate, True))
            pl.when(k > 0)(functools.partial(accumulate, False))

            @pl.when(k == nk - 1)
            def _():
                finish([a[...] for a in accs])

    outs, couts = _call(
        body, name, grid, in_specs, [mn_spec] * n_o,
        [jax.ShapeDtypeStruct((M, N), dt) for dt in out_dtypes],
        [pltpu.VMEM((tm, tn), F32) for _ in range(n_acc if nk > 1 else 0)],
        ("parallel", "parallel", "arbitrary"), args, comm)
    return (outs, couts) if comm else outs


def _rope_tables(L):
    t = jnp.arange(L, dtype=jnp.int32)
    f = jnp.arange(32, dtype=jnp.int32).astype(F32)
    ang = t.astype(F32)[:, None] * (ROPE_BASE ** (-f / 32.0))[None, :]
    cos, sin = jnp.cos(ang), jnp.sin(ang)
    ret = jnp.stack([jnp.tile(cos, (1, 4)), jnp.tile(jnp.concatenate([-sin, sin], axis=1), (1, 2))])
    f2 = jnp.arange(16, dtype=jnp.int32).astype(F32)
    inv2 = (ROPE_BASE ** (-f2 / 16.0))[None, :]
    ang_r = (t // GRID_W).astype(F32)[:, None] * inv2
    ang_c = (t % GRID_W).astype(F32)[:, None] * inv2
    cr, sr, cc, sc = jnp.cos(ang_r), jnp.sin(ang_r), jnp.cos(ang_c), jnp.sin(ang_c)
    att = jnp.stack([jnp.tile(jnp.concatenate([cr, cr, cc, cc], axis=1), (1, 2)),
                     jnp.tile(jnp.concatenate([-sr, sr, -sc, sc], axis=1), (1, 2))])
    return ret.astype(F32), att.astype(F32)


def _swap(x, sh):
    lane = lax.broadcasted_iota(jnp.int32, x.shape, 1)
    ra = pltpu.roll(x, LANES - sh, 1)
    rb = pltpu.roll(x, sh, 1)
    la = pltpu.roll(lane, LANES - sh, 1)
    partner = jnp.where((lane % (2 * sh)) < sh, lane + sh, lane - sh)
    return jnp.where(la == partner, ra, rb)


def _rope(x, cos, sin, sh):
    return x * cos + _swap(x, sh) * sin


def _rope_t(d, cos, sin, sh):
    return d * cos + _swap(d * sin, sh)


def _half_mask(shape, a):
    lane = lax.broadcasted_iota(jnp.int32, shape, 1)
    return (lane < 64) if a == 0 else (lane >= 64)


def _mod_fwd(s_in, w_l, b_l):
    D, C6 = w_l.shape
    tk = _pick(D, (512, 256, 128))
    nk = D // tk

    def body(s_ref, w_ref, b_ref, o_ref):
        k = pl.program_id(0)
        s = s_ref[...]
        s = s * _sigmoid(s)
        d = jnp.dot(s, w_ref[...], preferred_element_type=F32, precision=lax.Precision.HIGHEST)

        @pl.when(k == 0)
        def _():
            o_ref[...] = d + b_ref[...]

        @pl.when(k > 0)
        def _():
            o_ref[...] += d

    return pl.pallas_call(
        body, name="mod_fwd", grid=(nk,),
        in_specs=[pl.BlockSpec((16, tk), lambda k: (0, k)), pl.BlockSpec((tk, C6), lambda k: (k, 0)),
                  pl.BlockSpec((1, C6), lambda k: (0, 0))],
        out_specs=pl.BlockSpec((16, C6), lambda k: (0, 0)),
        out_shape=jax.ShapeDtypeStruct((16, C6), F32),
        compiler_params=_cparams(("arbitrary",)),
    )(s_in, w_l, b_l)


def _mod_bwd(s_in, dm, w_l):
    D, C6 = w_l.shape
    tk = _pick(D, (512, 256, 128))
    nk = D // tk

    def body(s_ref, dm_ref, w_ref, gw_ref, gc_ref):
        s = s_ref[...]
        sg = _sigmoid(s)
        act = s * sg
        dmv = dm_ref[...]
        gw_ref[...] = lax.dot_general(act, dmv, TN, preferred_element_type=F32, precision=lax.Precision.HIGHEST)
        ds = lax.dot_general(dmv, w_ref[...], NT, preferred_element_type=F32, precision=lax.Precision.HIGHEST)
        dsil = (sg * (1.0 + s * (1.0 - sg)))[8:9, :]
        gc_ref[...] = jnp.zeros((8, tk), F32) + jnp.sum(ds[8:16, :], axis=0, keepdims=True) * dsil

    return pl.pallas_call(
        body, name="mod_bwd", grid=(nk,),
        in_specs=[pl.BlockSpec((16, tk), lambda k: (0, k)), pl.BlockSpec((16, C6), lambda k: (0, 0)),
                  pl.BlockSpec((tk, C6), lambda k: (k, 0))],
        out_specs=[pl.BlockSpec((tk, C6), lambda k: (k, 0)), pl.BlockSpec((8, tk), lambda k: (0, k))],
        out_shape=[jax.ShapeDtypeStruct((D, C6), F32), jax.ShapeDtypeStruct((8, D), F32)],
        compiler_params=_cparams(("parallel",)),
    )(s_in, dm, w_l)


def _norm_rows(x):
    r = lax.rsqrt(jnp.mean(x * x, axis=-1, keepdims=True) + EPS)
    return x * r, r


def _modulate_fwd(name, x, ctx, g, mod, modc, comm=()):
    L, D = x.shape
    tr = ctx.shape[0]
    nx = L // tr

    def body(x_ref, c_ref, g_ref, m_ref, mc_ref, o_ref):
        i = pl.program_id(0)

        def run(src, m):
            n, _ = _norm_rows(src[...])
            o_ref[...] = (n * g_ref[...] * (1.0 + m[1:2, :]) + m[0:1, :]).astype(o_ref.dtype)

        @pl.when(i < nx)
        def _():
            run(x_ref, m_ref)

        @pl.when(i >= nx)
        def _():
            run(c_ref, mc_ref)

    row = pl.BlockSpec((tr, D), lambda i: (jnp.minimum(i, nx - 1), 0))
    vec = pl.BlockSpec((1, D), lambda i: (0, 0))
    mv = pl.BlockSpec((8, D), lambda i: (0, 0))
    return _call(
        body, name, (nx + 1,), [row, pl.BlockSpec((tr, D), lambda i: (0, 0)), vec, mv, mv],
        [pl.BlockSpec((tr, D), lambda i: (i, 0))], [jax.ShapeDtypeStruct((L + tr, D), BF16)], [],
        ("parallel",), (x, ctx, g, mod, modc), comm)


def _residual_modulate_fwd(name, x, fbr, gate, g, mod):
    L, D = x.shape
    tr = _pick(L, (256, 128))

    def body(x_ref, f_ref, gt_ref, g_ref, m_ref, x1_ref, o_ref):
        x1 = x_ref[...] + gt_ref[...] * f_ref[...].astype(F32)
        x1_ref[...] = x1
        n, _ = _norm_rows(x1)
        o_ref[...] = (n * g_ref[...] * (1.0 + m_ref[1:2, :]) + m_ref[0:1, :]).astype(o_ref.dtype)

    row = pl.BlockSpec((tr, D), lambda i: (i, 0))
    vec = pl.BlockSpec((1, D), lambda i: (0, 0))
    return pl.pallas_call(
        body, name=name, grid=(L // tr,),
        in_specs=[row, row, vec, vec, pl.BlockSpec((8, D), lambda i: (0, 0))],
        out_specs=[row, row],
        out_shape=[jax.ShapeDtypeStruct((L, D), F32), jax.ShapeDtypeStruct((L, D), BF16)],
        compiler_params=_cparams(("parallel",)),
    )(x, fbr, gate, g, mod)


def _modulate_bwd(name, x, ctx, dh, g, mod, modc, dres, fbr, gate):
    L, D = x.shape
    tr = ctx.shape[0] if ctx is not None else _pick(L, (256, 128))
    nx = L // tr
    nt = nx + (1 if ctx is not None else 0)
    has_f = fbr is not None

    def body(*refs):
        refs = list(refs)
        x_ref = refs.pop(0)
        c_ref = refs.pop(0) if ctx is not None else None
        dh_ref, g_ref, m_ref = refs.pop(0), refs.pop(0), refs.pop(0)
        mc_ref = refs.pop(0) if ctx is not None else None
        dr_ref = refs.pop(0)
        f_ref = refs.pop(0) if has_f else None
        gt_ref = refs.pop(0) if has_f else None
        dx_ref = refs.pop(0)
        df_ref = refs.pop(0) if has_f else None
        acc_ref = refs.pop(0)
        i = pl.program_id(0)

        @pl.when(i == 0)
        def _():
            acc_ref[...] = jnp.zeros_like(acc_ref)

        def sums(src, m, base, grow):
            n, r = _norm_rows(src[...])
            d = dh_ref[...].astype(F32)
            gg = g_ref[...]
            sc1 = 1.0 + m[1:2, :]
            acc_ref[base:base + 1, :] += jnp.sum(d, axis=0, keepdims=True)
            dn = d * n
            acc_ref[base + 1:base + 2, :] += jnp.sum(dn, axis=0, keepdims=True) * gg
            acc_ref[grow:grow + 1, :] += jnp.sum(dn, axis=0, keepdims=True) * sc1
            dnv = d * (gg * sc1)
            return r * (dnv - n * jnp.mean(dnv * n, axis=-1, keepdims=True))

        def x_rows():
            dx = sums(x_ref, m_ref, 0, 2) + dr_ref[...]
            dx_ref[...] = dx
            if has_f:
                acc_ref[6:7, :] += jnp.sum(dx * f_ref[...].astype(F32), axis=0, keepdims=True)
                df_ref[...] = (dx * gt_ref[...]).astype(df_ref.dtype)

        if ctx is None:
            x_rows()
        else:
            pl.when(i < nx)(x_rows)

            @pl.when(i >= nx)
            def _():
                sums(c_ref, mc_ref, 3, 2)

    row = pl.BlockSpec((tr, D), lambda i: (jnp.minimum(i, nx - 1), 0))
    vec = pl.BlockSpec((1, D), lambda i: (0, 0))
    mv = pl.BlockSpec((8, D), lambda i: (0, 0))
    in_specs, args = [row], [x]
    if ctx is not None:
        in_specs.append(pl.BlockSpec((tr, D), lambda i: (0, 0)))
        args.append(ctx)
    in_specs += [pl.BlockSpec((tr, D), lambda i: (i, 0)), vec, mv]
    args += [dh, g, mod]
    if ctx is not None:
        in_specs.append(mv)
        args.append(modc)
    in_specs.append(row)
    args.append(dres)
    out_specs = [row]
    out_shape = [jax.ShapeDtypeStruct((L, D), F32)]
    if has_f:
        in_specs += [row, vec]
        args += [fbr, gate]
        out_specs.append(row)
        out_shape.append(jax.ShapeDtypeStruct((L, D), BF16))
    out_specs.append(pl.BlockSpec((16, D), lambda i: (0, 0)))
    out_shape.append(jax.ShapeDtypeStruct((16, D), F32))
    return pl.pallas_call(
        body, name=name, grid=(nt,), in_specs=in_specs, out_specs=out_specs, out_shape=out_shape,
        compiler_params=_cparams(("arbitrary",)),
    )(*args)


def _loss_head(x1, tgt, nf, fbr, gate):
    L, D = x1.shape
    tr = _pick(L, (256, 128))

    def body(x_ref, t_ref, w_ref, f_ref, gt_ref, dx_ref, df_ref, acc_ref):
        i = pl.program_id(0)

        @pl.when(i == 0)
        def _():
            acc_ref[...] = jnp.zeros_like(acc_ref)

        n, r = _norm_rows(x_ref[...] + gt_ref[...] * f_ref[...].astype(F32))
        w = w_ref[...]
        e = n * w - t_ref[...]
        acc_ref[0:1, :] += jnp.sum(e * e, axis=0, keepdims=True) * (0.5 / D)
        dout = e * (1.0 / D)
        acc_ref[1:2, :] += jnp.sum(dout * n, axis=0, keepdims=True)
        dn = dout * w
        dx = r * (dn - n * jnp.mean(dn * n, axis=-1, keepdims=True))
        dx_ref[...] = dx
        acc_ref[2:3, :] += jnp.sum(dx * f_ref[...].astype(F32), axis=0, keepdims=True)
        df_ref[...] = (dx * gt_ref[...]).astype(df_ref.dtype)

        @pl.when(i == pl.num_programs(0) - 1)
        def _():
            acc_ref[3:4, :] = jnp.zeros((1, D), F32) + jnp.sum(acc_ref[0:1, :])

    row = pl.BlockSpec((tr, D), lambda i: (i, 0))
    vec = pl.BlockSpec((1, D), lambda i: (0, 0))
    return pl.pallas_call(
        body, name="loss_head", grid=(L // tr,),
        in_specs=[row, row, vec, row, vec],
        out_specs=[row, row, pl.BlockSpec((8, D), lambda i: (0, 0))],
        out_shape=[jax.ShapeDtypeStruct((L, D), F32), jax.ShapeDtypeStruct((L, D), BF16),
                   jax.ShapeDtypeStruct((8, D), F32)],
        compiler_params=_cparams(("arbitrary",)),
    )(x1, tgt, nf, fbr, gate)


N_TAB = 7


def _ret_tables(rdb, Lc):
    def body(rd_ref, t_ref, c_ref):
        d = pl.program_id(0) // RET_HEADS
        fwd = d == 0
        lg = -jnp.exp(rd_ref[0])
        i = lax.broadcasted_iota(jnp.int32, (CHUNK, CHUNK), 0).astype(F32)
        j = lax.broadcasted_iota(jnp.int32, (CHUNK, CHUNK), 1).astype(F32)
        rel = jnp.where(fwd, i - j, j - i)
        mask = (rel > 0.0) | ((rel == 0.0) & fwd)
        dm = jnp.where(mask, jnp.exp(lg * jnp.maximum(rel, 0.0)), 0.0)
        t_ref[0, 0] = dm
        t_ref[0, 1] = rel * dm
        qc = jnp.where(fwd, i + 1.0, CHUNK - i)
        qw = jnp.exp(lg * qc)
        t_ref[0, 2] = qw
        t_ref[0, 3] = qw * qc
        kc = jnp.where(fwd, CHUNK - 1.0 - i, i)
        kw = jnp.exp(lg * kc)
        t_ref[0, 4] = kw
        t_ref[0, 5] = kw * kc
        t_ref[0, 6] = jnp.exp(lg * float(CHUNK)) + jnp.zeros((CHUNK, CHUNK), F32)
        m = lax.broadcasted_iota(jnp.int32, (Lc, LANES), 0).astype(F32)
        cc = jnp.where(fwd, Lc - 1.0 - m, m)
        cw = jnp.exp(lg * cc)
        c_ref[0, 0] = cw
        c_ref[0, 1] = cw * cc

    return pl.pallas_call(
        body, name="ret_tables", grid=(2 * RET_HEADS,),
        in_specs=[pl.BlockSpec((1, 1, LANES), lambda r: (r, 0, 0))],
        out_specs=[pl.BlockSpec((1, N_TAB, CHUNK, CHUNK), lambda r: (r, 0, 0, 0)),
                   pl.BlockSpec((1, 2, Lc, LANES), lambda r: (r, 0, 0, 0))],
        out_shape=[jax.ShapeDtypeStruct((2 * RET_HEADS, N_TAB, CHUNK, CHUNK), F32),
                   jax.ShapeDtypeStruct((2 * RET_HEADS, 2, Lc, LANES), F32)],
        compiler_params=_cparams(("parallel",)),
    )(rdb)


def _ret_ctx_state(P, ctab, L, Lc):
    cb = L // Lc

    def body(k_ref, v_ref, c_ref, s_ref):
        for p in range(RET_HEADS // 2):
            kp = k_ref[:, p * LANES:(p + 1) * LANES].astype(F32) * K_SCALE
            for a in range(2):
                h = 2 * p + a
                kh = jnp.where(_half_mask(kp.shape, a), kp, 0.0)
                vh = v_ref[:, h * RET_DV:(h + 1) * RET_DV]
                for d in range(2):
                    kw = (kh * c_ref[d * RET_HEADS + h, 0]).astype(BF16)
                    s_ref[d * RET_HEADS + h] = _dot(kw, vh, TN)

    return pl.pallas_call(
        body, name="ret_ctx_state", grid=(1,),
        in_specs=[pl.BlockSpec((Lc, 512), lambda i: (cb, C_RK // 512)),
                  pl.BlockSpec((Lc, 1024), lambda i: (cb, C_RV // 1024)),
                  pl.BlockSpec((2 * RET_HEADS, 2, Lc, LANES), lambda i: (0, 0, 0, 0))],
        out_specs=pl.BlockSpec((2 * RET_HEADS, LANES, RET_DV), lambda i: (0, 0, 0)),
        out_shape=jax.ShapeDtypeStruct((2 * RET_HEADS, LANES, RET_DV), F32),
        compiler_params=_cparams(("arbitrary",)),
    )(P, P, ctab)


def _ret_fwd(P, rope, tabs, s0, L, comm=()):
    n = L // CHUNK

    def body(qf, kf, vf, rf, qb, kb, vb, rb, t_ref, s0_ref, of_ref, ob_ref, stf_ref, stb_ref, st):
        s = pl.program_id(0)

        @pl.when(s == 0)
        def _():
            st[...] = s0_ref[...]

        units = []
        for d, (q_ref, k_ref, v_ref, r_ref, o_ref, so_ref) in enumerate(
                ((qf, kf, vf, rf, of_ref, stf_ref), (qb, kb, vb, rb, ob_ref, stb_ref))):
            cos, sin = r_ref[0], r_ref[1]
            for p in range(RET_HEADS // 2):
                qp = _rope(q_ref[:, p * LANES:(p + 1) * LANES].astype(F32), cos, sin, 32)
                kp = _rope(k_ref[:, p * LANES:(p + 1) * LANES].astype(F32), cos, sin, 32) * K_SCALE
                for a in range(2):
                    h = 2 * p + a
                    hm = _half_mask(qp.shape, a)
                    units.append(dict(r=d * RET_HEADS + h, h=h, a=a, o_ref=o_ref, so_ref=so_ref, v_ref=v_ref,
                                      qh=jnp.where(hm, qp, 0.0), kh=jnp.where(hm, kp, 0.0)))
        for u in units:
            u["sc"] = _dot(u["qh"].astype(BF16), u["kh"].astype(BF16), NT)
        for u in units:
            r, h = u["r"], u["h"]
            sp = st[r]
            u["so_ref"][0, h] = sp[u["a"] * RET_DK:(u["a"] + 1) * RET_DK, :]
            vh = u["v_ref"][:, h * RET_DV:(h + 1) * RET_DV]
            o = _dot((u["sc"] * t_ref[r, 0]).astype(BF16), vh, NN)
            o += _dot((u["qh"] * t_ref[r, 2]).astype(BF16), sp.astype(BF16), NN)
            u["o_ref"][:, h * RET_DV:(h + 1) * RET_DV] = o
        for u in units:
            r, h = u["r"], u["h"]
            vh = u["v_ref"][:, h * RET_DV:(h + 1) * RET_DV]
            st[r] = t_ref[r, 6] * st[r] + _dot((u["kh"] * t_ref[r, 4]).astype(BF16), vh, TN)

    fw = lambda s: s
    bw = lambda s: n - 1 - s

    def specs(cm):
        return [pl.BlockSpec((CHUNK, 512), lambda s: (cm(s), C_RQ // 512)),
                pl.BlockSpec((CHUNK, 512), lambda s: (cm(s), C_RK // 512)),
                pl.BlockSpec((CHUNK, 1024), lambda s: (cm(s), C_RV // 1024)),
                pl.BlockSpec((2, CHUNK, LANES), lambda s: (0, cm(s), 0))]

    full = lambda shp: pl.BlockSpec(shp, lambda s: (0,) * len(shp))
    return _call(
        body, "ret_fwd", (n,),
        specs(fw) + specs(bw) + [full((2 * RET_HEADS, N_TAB, CHUNK, CHUNK)), full((2 * RET_HEADS, LANES, RET_DV))],
        [pl.BlockSpec((CHUNK, 1024), lambda s: (fw(s), 0)),
         pl.BlockSpec((CHUNK, 1024), lambda s: (bw(s), 0)),
         pl.BlockSpec((1, RET_HEADS, RET_DK, RET_DV), lambda s: (fw(s), 0, 0, 0)),
         pl.BlockSpec((1, RET_HEADS, RET_DK, RET_DV), lambda s: (bw(s), 0, 0, 0))],
        [jax.ShapeDtypeStruct((L, 1024), F32), jax.ShapeDtypeStruct((L, 1024), F32),
         jax.ShapeDtypeStruct((n, RET_HEADS, RET_DK, RET_DV), F32),
         jax.ShapeDtypeStruct((n, RET_HEADS, RET_DK, RET_DV), F32)],
        [pltpu.VMEM((2 * RET_HEADS, LANES, RET_DV), F32)],
        ("arbitrary",), (P, P, P, rope, P, P, P, rope, tabs, s0), comm)


def _ret_finish_fwd(of, ob, P, L):
    tr = _pick(L, (256, 128))

    def body(f_ref, b_ref, g_ref, y_ref):
        for h in range(RET_HEADS):
            sl = slice(h * RET_DV, (h + 1) * RET_DV)
            n, _ = _norm_rows(f_ref[:, sl] + b_ref[:, sl])
            g = g_ref[:, sl].astype(F32)
            y_ref[:, sl] = (n * (g * _sigmoid(g))).astype(y_ref.dtype)

    row = pl.BlockSpec((tr, 1024), lambda i: (i, 0))
    return pl.pallas_call(
        body, name="ret_finish_fwd", grid=(L // tr,),
        in_specs=[row, row, pl.BlockSpec((tr, 1024), lambda i: (i, C_RG // 1024))],
        out_specs=row, out_shape=jax.ShapeDtypeStruct((L, 2048), BF16),
        compiler_params=_cparams(("parallel",)),
    )(of, ob, P)


def _ret_finish_bwd(of, ob, P, dY, L):
    tr = _pick(L, (256, 128))

    def body(f_ref, b_ref, g_ref, dy_ref, do_ref, dg_ref):
        for h in range(RET_HEADS):
            sl = slice(h * RET_DV, (h + 1) * RET_DV)
            n, r = _norm_rows(f_ref[:, sl] + b_ref[:, sl])
            g = g_ref[:, sl].astype(F32)
            sg = _sigmoid(g)
            dy = dy_ref[:, sl].astype(F32)
            dg_ref[:, sl] = (dy * n * (sg * (1.0 + g * (1.0 - sg)))).astype(dg_ref.dtype)
            dn = dy * (g * sg)
            do_ref[:, sl] = (r * (dn - n * jnp.mean(dn * n, axis=-1, keepdims=True))).astype(do_ref.dtype)

    row = pl.BlockSpec((tr, 1024), lambda i: (i, 0))
    return pl.pallas_call(
        body, name="ret_finish_bwd", grid=(L // tr,),
        in_specs=[row, row, pl.BlockSpec((tr, 1024), lambda i: (i, C_RG // 1024)), row],
        out_specs=[row, row],
        out_shape=[jax.ShapeDtypeStruct((L, 1024), BF16), jax.ShapeDtypeStruct((L, 1024), BF16)],
        compiler_params=_cparams(("parallel",)),
    )(of, ob, P, dY)


def _ret_bwd(P, rope, tabs, stf, stb, dO, L, comm=()):
    n = L // CHUNK

    def body(qf, kf, vf, rf, gf, sf, qb, kb, vb, rb, gb, sb, t_ref,
             dqf, dkf, dvf, dqb, dkb, dvb, ds0_ref, dlg_ref, ds):
        s = pl.program_id(0)

        @pl.when(s == 0)
        def _():
            ds[...] = jnp.zeros_like(ds)
            dlg_ref[...] = jnp.zeros_like(dlg_ref)

        units, pairs = [], []
        for d, (q_ref, k_ref, v_ref, r_ref, g_ref, s_ref, dq_ref, dk_ref, dv_ref) in enumerate(
                ((qf, kf, vf, rf, gf, sf, dqf, dkf, dvf), (qb, kb, vb, rb, gb, sb, dqb, dkb, dvb))):
            cos, sin = r_ref[0], r_ref[1]
            for p in range(RET_HEADS // 2):
                qp = _rope(q_ref[:, p * LANES:(p + 1) * LANES].astype(F32), cos, sin, 32)
                kp = _rope(k_ref[:, p * LANES:(p + 1) * LANES].astype(F32), cos, sin, 32) * K_SCALE
                pair = dict(p=p, cos=cos, sin=sin, dq_ref=dq_ref, dk_ref=dk_ref, us=[])
                pairs.append(pair)
                for a in range(2):
                    h = 2 * p + a
                    r = d * RET_HEADS + h
                    hm = _half_mask(qp.shape, a)
                    zero = jnp.zeros((RET_DK, RET_DV), F32)
                    sp = s_ref[0, h]
                    u = dict(r=r, h=h, dv_ref=dv_ref, qh=jnp.where(hm, qp, 0.0), kh=jnp.where(hm, kp, 0.0),
                             vh=v_ref[:, h * RET_DV:(h + 1) * RET_DV], gh=g_ref[:, h * RET_DV:(h + 1) * RET_DV],
                             sp=jnp.concatenate([sp, zero] if a == 0 else [zero, sp], axis=0),
                             dsn=ds[r])
                    u["qhb"], u["khb"] = u["qh"].astype(BF16), u["kh"].astype(BF16)
                    units.append(u)
                    pair["us"].append(u)
        for u in units:
            u["am"] = _dot(u["qhb"], u["khb"], NT)
            u["dar"] = _dot(u["gh"], u["vh"], NT)
            u["xq"] = _dot(u["gh"], u["sp"].astype(BF16), NT)
            u["yk"] = _dot(u["vh"], u["dsn"].astype(BF16), NT)
        for u in units:
            r = u["r"]
            dm = t_ref[r, 0]
            u["da"] = (u["dar"] * dm).astype(BF16)
            u["amd"] = (u["am"] * dm).astype(BF16)
            part = (jnp.sum(u["am"] * u["dar"] * t_ref[r, 1]) + jnp.sum(u["qh"] * t_ref[r, 3] * u["xq"])
                    + jnp.sum(u["kh"] * t_ref[r, 5] * u["yk"])
                    + float(CHUNK) * jnp.sum(t_ref[r, 6] * u["dsn"] * u["sp"]))
            dlg_ref[r:r + 1, :] += jnp.zeros((1, LANES), F32) + part
        for u in units:
            r, h = u["r"], u["h"]
            u["dq"] = _dot(u["da"], u["khb"], NN) + u["xq"] * t_ref[r, 2]
            u["dk"] = _dot(u["da"], u["qhb"], TN) + u["yk"] * t_ref[r, 4]
            u["dv_ref"][:, h * RET_DV:(h + 1) * RET_DV] = (
                _dot(u["amd"], u["gh"], TN) + _dot((u["kh"] * t_ref[r, 4]).astype(BF16), u["dsn"].astype(BF16), NN))
            ds[r] = t_ref[r, 6] * u["dsn"] + _dot((u["qh"] * t_ref[r, 2]).astype(BF16), u["gh"], TN)
        for pair in pairs:
            sl = slice(pair["p"] * LANES, (pair["p"] + 1) * LANES)
            u0, u1 = pair["us"]
            pair["dq_ref"][:, sl] = _rope_t(u0["dq"] + u1["dq"], pair["cos"], pair["sin"], 32)
            pair["dk_ref"][:, sl] = _rope_t((u0["dk"] + u1["dk"]) * K_SCALE, pair["cos"], pair["sin"], 32)

        @pl.when(s == n - 1)
        def _():
            ds0_ref[...] = ds[...]

    fw = lambda s: n - 1 - s
    bw = lambda s: s

    def specs(cm):
        return [pl.BlockSpec((CHUNK, 512), lambda s: (cm(s), C_RQ // 512)),
                pl.BlockSpec((CHUNK, 512), lambda s: (cm(s), C_RK // 512)),
                pl.BlockSpec((CHUNK, 1024), lambda s: (cm(s), C_RV // 1024)),
                pl.BlockSpec((2, CHUNK, LANES), lambda s: (0, cm(s), 0)),
                pl.BlockSpec((CHUNK, 1024), lambda s: (cm(s), 0)),
                pl.BlockSpec((1, RET_HEADS, RET_DK, RET_DV), lambda s: (cm(s), 0, 0, 0))]

    def ospecs(cm):
        return [pl.BlockSpec((CHUNK, 512), lambda s: (cm(s), 0)), pl.BlockSpec((CHUNK, 512), lambda s: (cm(s), 0)),
                pl.BlockSpec((CHUNK, 1024), lambda s: (cm(s), 0))]

    oshape = [jax.ShapeDtypeStruct((L, 512), F32), jax.ShapeDtypeStruct((L, 512), F32),
              jax.ShapeDtypeStruct((L, 1024), F32)]
    full = lambda shp: pl.BlockSpec(shp, lambda s: (0,) * len(shp))
    return _call(
        body, "ret_bwd", (n,),
        specs(fw) + specs(bw) + [full((2 * RET_HEADS, N_TAB, CHUNK, CHUNK))],
        ospecs(fw) + ospecs(bw) + [full((2 * RET_HEADS, LANES, RET_DV)), full((2 * RET_HEADS, LANES))],
        oshape + oshape + [jax.ShapeDtypeStruct((2 * RET_HEADS, LANES, RET_DV), F32),
                           jax.ShapeDtypeStruct((2 * RET_HEADS, LANES), F32)],
        [pltpu.VMEM((2 * RET_HEADS, LANES, RET_DV), F32)],
        ("arbitrary",), (P, P, P, rope, dO, stf, P, P, P, rope, dO, stb, tabs), comm)


def _ret_ctx_bwd(P, ctab, ds0, dlg, rdb, L, Lc):
    cb = L // Lc

    def body(k_ref, v_ref, c_ref, ds_ref, dlg_ref, rd_ref, dk_ref, dv_ref, drd_ref):
        for p in range(RET_HEADS // 2):
            kp = k_ref[:, p * LANES:(p + 1) * LANES].astype(F32) * K_SCALE
            dkp = jnp.zeros((Lc, LANES), F32)
            for a in range(2):
                h = 2 * p + a
                kh = jnp.where(_half_mask(kp.shape, a), kp, 0.0)
                vh = v_ref[:, h * RET_DV:(h + 1) * RET_DV]
                dvh = jnp.zeros((Lc, RET_DV), F32)
                for d in range(2):
                    r = d * RET_HEADS + h
                    dsb = ds_ref[r].astype(BF16)
                    cw, cwc = c_ref[r, 0], c_ref[r, 1]
                    y = _dot(vh, dsb, NT)
                    dkp += y * cw
                    dvh += _dot((kh * cw).astype(BF16), dsb, NN)
                    lg = -jnp.exp(rd_ref[r])
                    drd_ref[r:r + 1, :] = (dlg_ref[r:r + 1, :] + jnp.sum(kh * cwc * y)) * lg
                dv_ref[:, h * RET_DV:(h + 1) * RET_DV] = dvh
            dk_ref[:, p * LANES:(p + 1) * LANES] = dkp * K_SCALE

    full = lambda shp: pl.BlockSpec(shp, lambda i: (0,) * len(shp))
    return pl.pallas_call(
        body, name="ret_ctx_bwd", grid=(1,),
        in_specs=[pl.BlockSpec((Lc, 512), lambda i: (cb, C_RK // 512)),
                  pl.BlockSpec((Lc, 1024), lambda i: (cb, C_RV // 1024)),
                  full((2 * RET_HEADS, 2, Lc, LANES)), full((2 * RET_HEADS, LANES, RET_DV)),
                  full((2 * RET_HEADS, LANES)), full((2 * RET_HEADS, 1, LANES))],
        out_specs=[full((Lc, 512)), full((Lc, 1024)), full((2 * RET_HEADS, LANES))],
        out_shape=[jax.ShapeDtypeStruct((Lc, 512), F32), jax.ShapeDtypeStruct((Lc, 1024), F32),
                   jax.ShapeDtypeStruct((2 * RET_HEADS, LANES), F32)],
        compiler_params=_cparams(("arbitrary",)),
    )(P, P, ctab, ds0, dlg, rdb)


BLK = 128
N_LOC = 3 * BLK


def _att_inputs(P, rope, L, Lc):
    n = L // BLK
    cb = L // Lc
    prev = lambda i: jnp.maximum(i - 1, 0)
    nxt = lambda i: jnp.minimum(i + 1, n - 1)
    specs = [pl.BlockSpec((BLK, 1024), lambda i: (i, C_AQ // 1024))]
    args = [P]
    for col in (C_AK // 256, C_AV // 256):
        for rm in (prev, lambda i: i, nxt):
            specs.append(pl.BlockSpec((BLK, 256), functools.partial(lambda i, rm, col: (rm(i), col), rm=rm, col=col)))
            args.append(P)
        specs.append(pl.BlockSpec((Lc, 256), functools.partial(lambda i, col: (cb, col), col=col)))
        args.append(P)
    for rm in (prev, lambda i: i, nxt):
        specs.append(pl.BlockSpec((2, BLK, LANES), functools.partial(lambda i, rm: (0, rm(i), 0), rm=rm)))
        args.append(rope)
    return specs, args


def _att_prep(i, n, refs, Lc):
    q_ref, kp_ref, kc_ref, kn_ref, kx_ref, vp_ref, vc_ref, vn_ref, vx_ref, rp_ref, rc_ref, rn_ref = refs
    cos = jnp.concatenate([rp_ref[0], rc_ref[0], rn_ref[0]], axis=0)
    sin = jnp.concatenate([rp_ref[1], rc_ref[1], rn_ref[1]], axis=0)
    kd, vd = [], []
    for t in range(ATT_KV // 2):
        sl = slice(t * LANES, (t + 1) * LANES)
        kl = jnp.concatenate([kp_ref[:, sl], kc_ref[:, sl], kn_ref[:, sl]], axis=0).astype(F32)
        kl = _rope(kl, cos, sin, 16)
        ka = jnp.concatenate([kl, kx_ref[:, sl].astype(F32)], axis=0)
        va = jnp.concatenate([vp_ref[:, sl], vc_ref[:, sl], vn_ref[:, sl], vx_ref[:, sl]], axis=0).astype(F32)
        kr, vr = pltpu.roll(ka, 64, 1), pltpu.roll(va, 64, 1)
        for b in range(2):
            hm = _half_mask(ka.shape, b)
            kd.append(jnp.where(hm, ka, kr).astype(BF16))
            vd.append(jnp.where(hm, va, vr).astype(BF16))
    nk = N_LOC + Lc
    rr = lax.broadcasted_iota(jnp.int32, (BLK, nk), 0)
    ss = lax.broadcasted_iota(jnp.int32, (BLK, nk), 1)
    lo = jnp.where(i == 0, BLK, 0)
    hi = jnp.where(i == n - 1, 2 * BLK, N_LOC)
    valid = (ss >= N_LOC) | ((ss >= rr) & (ss <= rr + 2 * BLK) & (ss >= lo) & (ss < hi))
    bias = jnp.where(valid, 0.0, NEG)
    return kd, vd, jnp.concatenate([bias] * 4, axis=0), rc_ref[0], rc_ref[1]


LOG2E = 1.4426950408889634
LN2 = 0.6931471805599453
Q_SCALE = A_SCALE * LOG2E


def _stack4(ref, g, f=None):
    parts = []
    for jp in range(2):
        t = ref[:, (2 * g + jp) * LANES:(2 * g + jp + 1) * LANES].astype(F32)
        if f is not None:
            t = f(t)
        for a in range(2):
            parts.append(jnp.where(_half_mask(t.shape, a), t, 0.0))
    return jnp.concatenate(parts, axis=0)


def _unstack4(x4, jp):
    r0 = 2 * jp * BLK
    lo = x4[r0:r0 + BLK]
    hi = x4[r0 + BLK:r0 + 2 * BLK]
    return jnp.where(_half_mask(lo.shape, 0), lo, hi)


def _softmax_parts(s, bias4, sink_ref, g):
    sink_col = LOG2E * jnp.concatenate(
        [jnp.zeros((BLK, 1), F32) + sink_ref[4 * g + r:4 * g + r + 1, 0:1] for r in range(4)], axis=0)
    s = s + bias4
    m = jnp.maximum(jnp.max(s, axis=-1, keepdims=True), sink_col)
    e = jnp.exp2(s - m)
    es = jnp.exp2(sink_col - m)
    return e, es, jnp.sum(e, axis=-1, keepdims=True) + es


def _att_fwd(P, rope, sinkb, Y, L, Lc, comm=()):
    n = L // BLK
    specs, args = _att_inputs(P, rope, L, Lc)

    def body(*refs):
        sink_ref, o_ref = refs[12], refs[14]
        i = pl.program_id(0)
        kd, vd, bias4, cq, sq = _att_prep(i, n, refs[:12], Lc)
        for g in range(ATT_KV):
            q4 = _stack4(refs[0], g, lambda t: _rope(t, cq, sq, 16) * Q_SCALE).astype(BF16)
            e, _, l = _softmax_parts(_dot(q4, kd[g], NT), bias4, sink_ref, g)
            o4 = _dot(e.astype(BF16), vd[g], NN) * (1.0 / l)
            for jp in range(2):
                c0 = (2 * g + jp) * LANES
                o_ref[:, c0:c0 + LANES] = _unstack4(o4, jp).astype(o_ref.dtype)

    return _call(
        body, "att_fwd", (n,),
        specs + [pl.BlockSpec((ATT_HEADS, LANES), lambda i: (0, 0)), pl.BlockSpec(memory_space=pl.ANY)],
        [pl.BlockSpec((BLK, 1024), lambda i: (i, 1))], [jax.ShapeDtypeStruct((L, 2048), BF16)], [],
        ("parallel",), (*args, sinkb, Y), comm, aliases={13: 0})


def _att_bwd(P, rope, sinkb, Y, dY, L, Lc, comm=()):
    n = L // BLK
    specs, args = _att_inputs(P, rope, L, Lc)
    nk = N_LOC + Lc

    def body(*refs):
        sink_ref, y_ref, dy_ref = refs[12], refs[13], refs[14]
        dq_ref, dkl_ref, dvl_ref, dkx_ref, dvx_ref, dsk_ref = refs[15:21]
        i = pl.program_id(0)

        @pl.when(i == 0)
        def _():
            dkx_ref[...] = jnp.zeros_like(dkx_ref)
            dvx_ref[...] = jnp.zeros_like(dvx_ref)
            dsk_ref[...] = jnp.zeros_like(dsk_ref)

        kd, vd, bias4, cq, sq = _att_prep(i, n, refs[:12], Lc)
        for t in range(ATT_KV // 2):
            dk_halves, dv_halves = [], []
            for b in range(2):
                g = 2 * t + b
                q4 = _stack4(refs[0], g, lambda x: _rope(x, cq, sq, 16) * Q_SCALE).astype(BF16)
                do4 = _stack4(dy_ref, g)
                delta = jnp.sum(do4 * _stack4(y_ref, g), axis=-1, keepdims=True)
                do4b = do4.astype(BF16)
                e, es, l = _softmax_parts(_dot(q4, kd[g], NT), bias4, sink_ref, g)
                inv = 1.0 / l
                p = e * inv
                dsc = (p * (_dot(do4b, vd[g], NT) - delta)).astype(BF16)
                dsr = es * inv * delta
                for r in range(4):
                    h = 4 * g + r
                    dsk_ref[h:h + 1, :] += jnp.zeros((1, LANES), F32) - jnp.sum(dsr[r * BLK:(r + 1) * BLK])
                dq4 = _dot(dsc, kd[g], NN) * A_SCALE
                for jp in range(2):
                    c0 = (2 * g + jp) * LANES
                    dq_ref[:, c0:c0 + LANES] = _rope_t(_unstack4(dq4, jp), cq, sq, 16).astype(dq_ref.dtype)
                dkd = _dot(q4, dsc, TN) * LN2
                dvd = _dot(do4b, p.astype(BF16), TN)
                dk_halves.append(dkd[:ATT_DH] + dkd[ATT_DH:])
                dv_halves.append(dvd[:ATT_DH] + dvd[ATT_DH:])
            dk_t = jnp.concatenate(dk_halves, axis=0).T
            dv_t = jnp.concatenate(dv_halves, axis=0).T
            sl = slice(t * LANES, (t + 1) * LANES)
            dkl_ref[0, :, sl] = dk_t[:N_LOC]
            dvl_ref[0, :, sl] = dv_t[:N_LOC]
            dkx_ref[:, sl] += dk_t[N_LOC:]
            dvx_ref[:, sl] += dv_t[N_LOC:]

    row = pl.BlockSpec((BLK, 1024), lambda i: (i, 0))
    loc = pl.BlockSpec((1, N_LOC, 256), lambda i: (i, 0, 0))
    cx = pl.BlockSpec((Lc, 256), lambda i: (0, 0))
    return _call(
        body, "att_bwd", (n,),
        specs + [pl.BlockSpec((ATT_HEADS, LANES), lambda i: (0, 0))] + [pl.BlockSpec((BLK, 1024), lambda i: (i, 1))] * 2,
        [row, loc, loc, cx, cx, pl.BlockSpec((ATT_HEADS, LANES), lambda i: (0, 0))],
        [jax.ShapeDtypeStruct((L, 1024), BF16), jax.ShapeDtypeStruct((n, N_LOC, 256), F32),
         jax.ShapeDtypeStruct((n, N_LOC, 256), F32), jax.ShapeDtypeStruct((Lc, 256), F32),
         jax.ShapeDtypeStruct((Lc, 256), F32), jax.ShapeDtypeStruct((ATT_HEADS, LANES), F32)], [],
        ("arbitrary",), (*args, sinkb, Y, dY), comm)


def _assemble_dp(L, Lc, dqf, dqb, dkf, dkb, dvf, dvb, drg, daq, dkl, dvl, rope_att, dck, dcv, dkx, dvx):
    n = L // BLK
    nc = Lc // BLK

    def body(dqf_r, dqb_r, dkf_r, dkb_r, dvf_r, dvb_r, drg_r, daq_r, kl0, kl1, kl2, vl0, vl1, vl2, rp_r,
             dck_r, dcv_r, dkx_r, dvx_r, o_ref):
        i = pl.program_id(0)

        @pl.when(i < n)
        def _():
            o_ref[:, C_RQ:C_RK] = (dqf_r[...] + dqb_r[...]).astype(o_ref.dtype)
            o_ref[:, C_RK:C_RV] = (dkf_r[...] + dkb_r[...]).astype(o_ref.dtype)
            o_ref[:, C_RV:C_RG] = (dvf_r[...] + dvb_r[...]).astype(o_ref.dtype)
            o_ref[:, C_RG:C_AQ] = drg_r[...].astype(o_ref.dtype)
            o_ref[:, C_AQ:C_AK] = daq_r[...].astype(o_ref.dtype)
            w0 = jnp.where(i > 0, 1.0, 0.0)
            w2 = jnp.where(i < n - 1, 1.0, 0.0)
            dk = kl0[0] * w0 + kl1[0] + kl2[0] * w2
            dv = vl0[0] * w0 + vl1[0] + vl2[0] * w2
            for t in range(ATT_KV // 2):
                sl = slice(t * LANES, (t + 1) * LANES)
                o_ref[:, C_AK + t * LANES:C_AK + (t + 1) * LANES] = _rope_t(
                    dk[:, sl], rp_r[0], rp_r[1], 16).astype(o_ref.dtype)
            o_ref[:, C_AV:D_PROJ] = dv.astype(o_ref.dtype)

        @pl.when(i >= n)
        def _():
            o_ref[:, C_RQ:C_RK] = jnp.zeros((BLK, C_RK - C_RQ), o_ref.dtype)
            o_ref[:, C_RK:C_RV] = dck_r[...].astype(o_ref.dtype)
            o_ref[:, C_RV:C_RG] = dcv_r[...].astype(o_ref.dtype)
            o_ref[:, C_RG:C_AK] = jnp.zeros((BLK, C_AK - C_RG), o_ref.dtype)
            o_ref[:, C_AK:C_AV] = dkx_r[...].astype(o_ref.dtype)
            o_ref[:, C_AV:D_PROJ] = dvx_r[...].astype(o_ref.dtype)

    xm = lambda i: jnp.minimum(i, n - 1)
    cm = lambda i: jnp.clip(i - n, 0, nc - 1)
    r512 = pl.BlockSpec((BLK, 512), lambda i: (xm(i), 0))
    r1024 = pl.BlockSpec((BLK, 1024), lambda i: (xm(i), 0))
    part = lambda off: pl.BlockSpec((1, BLK, 256), lambda i: (jnp.clip(xm(i) + off, 0, n - 1), 1 - off, 0))
    return pl.pallas_call(
        body, name="assemble_dp", grid=(n + nc,),
        in_specs=[r512, r512, r512, r512, r1024, r1024, r1024, r1024,
                  part(-1), part(0), part(1), part(-1), part(0), part(1),
                  pl.BlockSpec((2, BLK, LANES), lambda i: (0, xm(i), 0)),
                  pl.BlockSpec((BLK, 512), lambda i: (cm(i), 0)), pl.BlockSpec((BLK, 1024), lambda i: (cm(i), 0)),
                  pl.BlockSpec((BLK, 256), lambda i: (cm(i), 0)), pl.BlockSpec((BLK, 256), lambda i: (cm(i), 0))],
        out_specs=pl.BlockSpec((BLK, D_PROJ), lambda i: (i, 0)),
        out_shape=jax.ShapeDtypeStruct((L + Lc, D_PROJ), BF16),
        compiler_params=_cparams(("parallel",)),
    )(dqf, dqb, dkf, dkb, dvf, dvb, drg, daq, dkl, dkl, dkl, dvl, dvl, dvl, rope_att, dck, dcv, dkx, dvx)


def _adam_math(w, g, m, v):
    m = ADAM_B1 * m + (1.0 - ADAM_B1) * g
    v = ADAM_B2 * v + (1.0 - ADAM_B2) * (g * g)
    m_hat = m / (1.0 - ADAM_B1 ** ADAM_STEP)
    v_hat = v / (1.0 - ADAM_B2 ** ADAM_STEP)
    delta = -ADAM_LR * (m_hat / (jnp.sqrt(v_hat) + ADAM_EPS) + ADAM_WD * w)
    return delta, m, v


def _adam(name, w, m, v, g=None, parts=None):
    R, C = w.shape
    tr = _pick(R, (256, 128, 64, 32, 16, 8))
    summed = parts is not None
    n_parts = parts.shape[0] if summed else 0

    def body(w_ref, m_ref, v_ref, g_ref, go_ref, d_ref, mo_ref, vo_ref):
        if summed:
            gv = g_ref[0].astype(F32)
            for j in range(1, n_parts):
                gv = gv + g_ref[j].astype(F32)
        else:
            gv = g_ref[...]
        d, mn, vn = _adam_math(w_ref[...], gv, m_ref[...], v_ref[...])
        go_ref[...] = gv
        d_ref[...] = d
        mo_ref[...] = mn
        vo_ref[...] = vn

    row = pl.BlockSpec((tr, C), lambda i: (i, 0))
    gspec = pl.BlockSpec((n_parts, tr, C), lambda i: (0, i, 0)) if summed else row
    return pl.pallas_call(
        body, name=name, grid=(R // tr,),
        in_specs=[row, row, row, gspec], out_specs=[row] * 4,
        out_shape=[jax.ShapeDtypeStruct((R, C), F32)] * 4,
        compiler_params=_cparams(("parallel",)),
    )(w, m, v, parts if summed else g)


def _rows_full(g):
    _, R, D = g.shape
    return g.reshape(N_DEV * R, D)


def _rows_slots(g):
    N, D = g.shape
    return g.reshape(N_DEV, N // N_DEV, D)


def _pad_rows(a, rows):
    return jnp.concatenate([a, jnp.zeros((rows - a.shape[0],) + a.shape[1:], a.dtype)], axis=0)


def kernel(x, c, ctx, c_ctx, w_mod, b_mod, norm_mix, norm_ffn, w_in, ret_decay, attn_sink, w_out, w_gate, w_up, w_down, norm_final, loss_target, m_c_ctx, m_w_mod, m_b_mod, m_norm_mix, m_norm_ffn, m_w_in, m_ret_decay, m_attn_sink, m_w_out, m_w_gate, m_w_up, m_w_down, m_norm_final, v_c_ctx, v_w_mod, v_b_mod, v_norm_mix, v_norm_ffn, v_w_in, v_ret_decay, v_attn_sink, v_w_out, v_w_gate, v_w_up, v_w_down, v_norm_final):
    L, D = x.shape[1], x.shape[2]
    Lc = ctx.shape[1]
    DF = w_gate.shape[2] * N_DEV
    C6 = w_mod.shape[2]
    me = _my_id()
    xs, cx, tgt = x[0], ctx[0], loss_target[0]

    ag_in = ("ag2", w_in[0].T.astype(BF16))
    ag_out, ag_gate = ("ag2", w_out[0].astype(BF16)), ("ag2", w_gate[0].T.astype(BF16))
    ag_up, ag_down = ("ag2", w_up[0].T.astype(BF16)), ("ag2", w_down[0].astype(BF16))

    cs = _allgather(c, "ag_c")[:, 0, :]
    s_in = _pad_rows(jnp.concatenate([cs, c_ctx[None, :]], axis=0), 16)
    b_l = lax.dynamic_slice_in_dim(b_mod, me * C6, C6, axis=1)
    mod_parts = _allgather(_mod_fwd(s_in, w_mod[0], b_l), "ag_mod")
    mod = _pad_rows(lax.dynamic_index_in_dim(mod_parts, me, axis=1, keepdims=False).reshape(6, D), 8)
    modc = _pad_rows(mod_parts[:, N_DEV, :].reshape(6, D), 8)
    mix_mod, ffn_mod = mod, jnp.roll(mod, -3, axis=0)
    gt_m, gt_f = mod[2:3], mod[5:6]

    rope_ret, rope_att = _rope_tables(L)
    rdb = jnp.broadcast_to(ret_decay[0].reshape(2 * RET_HEADS, 1, 1), (2 * RET_HEADS, 1, LANES))
    sinkb = jnp.broadcast_to(attn_sink[0].reshape(ATT_HEADS, 1), (ATT_HEADS, LANES))

    tm = _pick(L + Lc, (1408, 768, 512, 384, 256, 128))
    tmx = _pick(L, (1024, 512, 256, 128))

    (H,), (g_in,) = _modulate_fwd("mod_mix_fwd", xs, cx, norm_mix, mix_mod, modc, comm=[ag_in])
    W_inT = _rows_full(g_in)
    ident = lambda a, e: a
    tP, tD, tF = _pick(D_PROJ, (1152, 768, 512)), _pick(D, (2048, 1024, 512)), _pick(DF, (512, 256, 128))
    (P,), (g_gate,) = _matmul("mm_in", [(H, W_inT, 0)], 1, L + Lc, D_PROJ, D, "nt",
                              (tm, _pick(D_PROJ, (1536, 768, 512)), D), [], [BF16], ident,
                              comm=[ag_gate])
    W_gateT = _rows_full(g_gate)
    tabs, ctab = _ret_tables(rdb, Lc)
    s0 = _ret_ctx_state(P, ctab, L, Lc)
    (o_f, o_b, st_f, st_b), (g_out,) = _ret_fwd(P, rope_ret, tabs, s0, L, comm=[ag_out])
    W_out = _rows_full(g_out)
    Y_half = _ret_finish_fwd(o_f, o_b, P, L)
    (Y,), (g_up,) = _att_fwd(P, rope_att, sinkb, Y_half, L, Lc, comm=[ag_up])
    W_upT = _rows_full(g_up)
    KO = Y.shape[1]
    f_mix = _matmul("mm_out", [(Y, W_out, 0)], 1, L, D, KO, "nn", (tmx, tD, KO), [], [BF16], ident)[0]

    x1, H2 = _residual_modulate_fwd("mod_ffn_fwd", xs, f_mix, gt_m, norm_ffn, ffn_mod)

    def swiglu_epi(a, e):
        sg = _sigmoid(a[0])
        act = a[0] * sg
        return [act, a[1] * (sg * (1.0 + a[0] * (1.0 - sg))), act * a[1]]

    tm2 = tmx
    (act, up_dact, hmid), (g_down,) = _matmul("mm_gate_up", [(H2, W_gateT, 0), (H2, W_upT, 1)], 2, L, DF, D, "nt",
                                              (tm2, tF, D), [], [BF16, BF16, BF16], swiglu_epi, comm=[ag_down])
    W_down = _rows_full(g_down)
    f_ffn = _matmul("mm_down", [(hmid, W_down, 0)], 1, L, D, DF, "nn", (tm2, tD, _pick(DF, (1408, 512, 256, 128))),
                    [], [BF16], ident)[0]

    dx2, dFf, sums_l = _loss_head(x1, tgt, norm_final.reshape(1, D), f_ffn, gt_f)

    def dswiglu_epi(a, e):
        return [a[0] * e[0].astype(F32), a[0] * e[1].astype(F32)]

    dga, dup = _matmul("mm_d_down", [(dFf, W_down, 0)], 1, L, DF, D, "nt", (tm2, tF, D),
                       [(up_dact, "mn"), (act, "mn")], [BF16, BF16], dswiglu_epi)
    tkt, tkl = _pick(L, (512, 256, 128)), _pick(L, (1024, 512, 256, 128))
    dW_down =_matmul("mm_gw_down", [(hmid, dFf, 0)], 1, DF, D, L, "tn",
                      (_pick(DF, (1408, 512, 256, 128)), tD, tkt), [], [BF16], ident)[0]
    (dW_gateT, dW_upT), (p_down,) = _matmul("mm_gw_gate_up", [(dga, H2, 0), (dup, H2, 1)], 2, DF, D, L, "tn",
                                            (tF, tD, tkl), [], [BF16, BF16], ident,
                                            comm=[("a2a", _rows_slots(dW_down))])
    (dH2,), (p_gate,) = _matmul("mm_d_gate_up", [(dga, W_gateT, 0), (dup, W_upT, 0)], 1, L, D, DF, "nn",
                                (tm2, tD, tF), [], [BF16], ident, comm=[("a2a", _rows_slots(dW_gateT))])
    dx1, dFm, sums_f = _modulate_bwd("mod_ffn_bwd", x1, None, dH2, norm_ffn, ffn_mod, None, dx2, f_mix, gt_m)

    tO = _pick(KO, (2048, 1024, 512))
    dY = _matmul("mm_d_out", [(dFm, W_out, 0)], 1, L, KO, D, "nt", (tmx, tO, D), [], [BF16], ident)[0]
    dW_out = _matmul("mm_gw_out", [(Y, dFm, 0)], 1, KO, D, L, "tn", (_pick(KO, (1024, 512)), tD, tkl), [], [BF16],
                     ident)[0]
    dO, drg = _ret_finish_bwd(o_f, o_b, P, dY, L)
    (dqf, dkf, dvf, dqb, dkb, dvb, ds0, dlg), (p_out,) = _ret_bwd(
        P, rope_ret, tabs, st_f, st_b, dO, L, comm=[("a2a", _rows_slots(dW_out))])
    dck, dcv, d_rd = _ret_ctx_bwd(P, ctab, ds0, dlg, rdb, L, Lc)
    (daq, dkl, dvl, dkx, dvx, d_sink), (p_up,) = _att_bwd(
        P, rope_att, sinkb, Y, dY, L, Lc, comm=[("a2a", _rows_slots(dW_upT))])
    dP = _assemble_dp(L, Lc, dqf, dqb, dkf, dkb, dvf, dvb, drg, daq, dkl, dvl, rope_att, dck, dcv, dkx, dvx)
    tkc = _pick(L + Lc, (768, 256, 128))
    dW_inT = _matmul("mm_gw_in", [(dP, H, 0)], 1, D_PROJ, D, L + Lc, "tn", (tP, tD, tkc), [], [BF16], ident)[0]
    (dH,), (p_in,) = _matmul("mm_d_in", [(dP, W_inT, 0)], 1, L + Lc, D, D_PROJ, "nn",
                             (tm, tD, _pick(D_PROJ, (768, 512, 256))), [], [BF16], ident,
                             comm=[("a2a", _rows_slots(dW_inT))])
    grad_x, sums_m = _modulate_bwd("mod_mix_bwd", xs, cx, dH, norm_mix, mix_mod, modc, dx1, None, None)

    zero = jnp.zeros((1, D), F32)
    dmod = jnp.concatenate([sums_m[0:1], sums_m[1:2], sums_f[6:7], sums_f[0:1], sums_f[1:2], sums_l[2:3]], axis=1)
    dmodc = jnp.concatenate([sums_m[3:4], sums_m[4:5], zero, zero, zero, zero], axis=1)
    dm_all = _allgather(jnp.concatenate([dmod, dmodc], axis=0), "ag_dmod")
    dm_cols = lax.dynamic_slice_in_dim(dm_all, me * C6, C6, axis=2)
    dm_in = jnp.concatenate([dm_cols[:, 0, :], dm_cols[:, 1, :]], axis=0)
    s_bwd = jnp.concatenate([cs, jnp.broadcast_to(c_ctx[None, :], (N_DEV, D))], axis=0)
    g_w_mod, dsil = _mod_bwd(s_bwd, dm_in, w_mod[0])

    lane_pad = lambda a: _pad_rows(a.reshape(-1, 1), LANES).reshape(1, LANES)
    pack = jnp.concatenate([dsil[0:1], sums_m[2:3], sums_f[2:3], sums_l[1:2],
                            lane_pad(d_rd[:, 0]), lane_pad(d_sink[:, 0]), sums_l[3:4, 0:LANES]], axis=1)
    packs = _allgather(pack, "ag_small")
    zl = jnp.zeros((1, LANES), F32)

    def pack_w(a_c, a_nm, a_nf, a_fin, a_rd, a_sk):
        return jnp.concatenate([a_c.reshape(1, D), a_nm, a_nf, a_fin.reshape(1, D), lane_pad(a_rd.reshape(-1)),
                                lane_pad(a_sk.reshape(-1)), zl], axis=1)

    sg, sd, sm, sv = _adam("adam_small", pack_w(c_ctx, norm_mix, norm_ffn, norm_final, ret_decay, attn_sink),
                           pack_w(m_c_ctx, m_norm_mix, m_norm_ffn, m_norm_final, m_ret_decay, m_attn_sink),
                           pack_w(v_c_ctx, v_norm_mix, v_norm_ffn, v_norm_final, v_ret_decay, v_attn_sink),
                           parts=packs)
    loss = sg[0, 4 * D + 2 * LANES]

    def unpack(a):
        return (a[0, 0:D], a[:, D:2 * D], a[:, 2 * D:3 * D], a[0, 3 * D:4 * D],
                a[0, 4 * D:4 * D + 2 * RET_HEADS].reshape(1, 2, RET_HEADS),
                a[:, 4 * D + LANES:4 * D + LANES + ATT_HEADS])

    bg, bd, bm, bv = _adam("adam_b_mod", b_mod, m_b_mod, v_b_mod, parts=dm_all.reshape(2 * N_DEV, 1, 6 * D))
    wg, wd, wm, wv = _adam("adam_w_mod", w_mod[0], m_w_mod[0], v_w_mod[0], g=g_w_mod)

    big = {}
    for nm, w, m, v, parts, transposed in (
            ("w_in", w_in, m_w_in, v_w_in, p_in, True), ("w_out", w_out, m_w_out, v_w_out, p_out, False),
            ("w_gate", w_gate, m_w_gate, v_w_gate, p_gate, True), ("w_up", w_up, m_w_up, v_w_up, p_up, True),
            ("w_down", w_down, m_w_down, v_w_down, p_down, False)):
        if transposed:
            res = [a.T for a in _adam("adam_" + nm, w[0].T, m[0].T, v[0].T, parts=parts)]
        else:
            res = _adam("adam_" + nm, w[0], m[0], v[0], parts=parts)
        big[nm] = [a[None] for a in res]

    g_s, d_s, m_s, v_s = unpack(sg), unpack(sd), unpack(sm), unpack(sv)

    def leaves(k, small, bmod, wmod):
        return (small[0], wmod[None], bmod, small[1], small[2], big["w_in"][k], small[4], small[5],
                big["w_out"][k], big["w_gate"][k], big["w_up"][k], big["w_down"][k], small[3])

    return (loss, grad_x[None], *leaves(0, g_s, bg, wg), *leaves(1, d_s, bd, wd),
            *leaves(2, m_s, bm, wm), *leaves(3, v_s, bv, wv))
```

```python
import functools

import jax
import jax.numpy as jnp
from jax import lax
from jax.experimental import pallas as pl
from jax.experimental.pallas import tpu as pltpu

F32 = jnp.float32
BF16 = jnp.bfloat16

N_DEV = 8
LANES = 128
RET_HEADS = 8
RET_DK = 64
RET_DV = 128
CHUNK = 128
ATT_HEADS = 16
ATT_KV = 4
ATT_DH = 64
GRID_W = 64
ROPE_BASE = 10000.0
EPS = 1e-6
NEG = -1e30
C_RQ, C_RK, C_RV, C_RG, C_AQ, C_AK, C_AV, D_PROJ = 0, 512, 1024, 2048, 3072, 4096, 4352, 4608
K_SCALE = RET_DK ** -0.5
A_SCALE = ATT_DH ** -0.5

ADAM_LR, ADAM_B1, ADAM_B2, ADAM_EPS, ADAM_WD, ADAM_STEP = 0.001, 0.9, 0.999, 1e-08, 0.01, 10

VMEM_BIG = 52 * 1024 * 1024

NN = (((1,), (0,)), ((), ()))
NT = (((1,), (1,)), ((), ()))
TN = (((0,), (0,)), ((), ()))


def _dot(a, b, dims):
    return lax.dot_general(a, b, dims, preferred_element_type=F32)


def _cparams(sem, vmem=VMEM_BIG):
    return pltpu.CompilerParams(dimension_semantics=sem, vmem_limit_bytes=vmem)


def _pick(dim, prefs):
    for p in prefs:
        if dim % p == 0:
            return p
    return dim


def _my_id():
    return lax.axis_index("x") * 4 + lax.axis_index("y") * 2 + lax.axis_index("c")


def _sigmoid(x):
    return 0.5 * jnp.tanh(0.5 * x) + 0.5


def _peers():
    mx, my, mc = lax.axis_index("x"), lax.axis_index("y"), lax.axis_index("c")
    out = []
    for k in range(1, N_DEV):
        kx, ky, kc = (k >> 2) & 1, (k >> 1) & 1, k & 1
        px = 1 - mx if kx else mx
        py = 1 - my if ky else my
        pc = 1 - mc if kc else mc
        out.append(((px, py, pc), px * 4 + py * 2 + pc))
    return out


def _exchange_copies(kind, x_ref, o_ref, ssem, rsem, lsem):
    me = _my_id()
    loc = pltpu.make_async_copy(x_ref if kind == "ag" else x_ref.at[me], o_ref.at[me], lsem)
    cps = []
    for k, (peer, pid) in enumerate(_peers()):
        cps.append(pltpu.make_async_remote_copy(
            src_ref=x_ref if kind == "ag" else x_ref.at[pid], dst_ref=o_ref.at[me],
            send_sem=ssem.at[k], recv_sem=rsem.at[k], device_id=peer, device_id_type=pl.DeviceIdType.MESH))
    return loc, cps


def _two_level_copies(x_ref, o_ref, ssem, rsem, lsem):
    mx, my, mc = lax.axis_index("x"), lax.axis_index("y"), lax.axis_index("c")
    me = mx * 4 + my * 2 + mc
    sibling = (mx, my, 1 - mc)
    chips = [(1 - mx, my), (mx, 1 - my), (1 - mx, 1 - my)]

    def copy(k, slot, to, src=None):
        return pltpu.make_async_remote_copy(
            src_ref=o_ref.at[slot] if src is None else src, dst_ref=o_ref.at[slot],
            send_sem=ssem.at[k], recv_sem=rsem.at[k], device_id=to, device_id_type=pl.DeviceIdType.MESH)

    loc = pltpu.make_async_copy(x_ref, o_ref.at[me], lsem)
    first = [copy(0, me, sibling, src=x_ref)]
    first += [copy(1 + j, me, (cx, cy, mc), src=x_ref) for j, (cx, cy) in enumerate(chips)]
    passed = [copy(4 + j, cx * 4 + cy * 2 + mc, sibling) for j, (cx, cy) in enumerate(chips)]
    return loc, first, passed


def _exchange_start(kind, x_ref, o_ref, ssem, rsem, lsem):
    if kind == "ag2":
        loc, first, _ = _two_level_copies(x_ref, o_ref, ssem, rsem, lsem)
        cps = first
    else:
        loc, cps = _exchange_copies(kind, x_ref, o_ref, ssem, rsem, lsem)
    loc.start()
    for cp in cps:
        cp.start()


def _exchange_wait(kind, x_ref, o_ref, ssem, rsem, lsem):
    if kind == "ag2":
        loc, first, passed = _two_level_copies(x_ref, o_ref, ssem, rsem, lsem)
        for j in range(3):
            first[1 + j].wait_recv()
            passed[j].start()
        first[0].wait_recv()
        for cp in passed:
            cp.wait_recv()
        cps = first + passed
    else:
        loc, cps = _exchange_copies(kind, x_ref, o_ref, ssem, rsem, lsem)
        for cp in cps:
            cp.wait_recv()
    for cp in cps:
        cp.wait_send()
    loc.wait()


_EXCHANGE_SEMS = [pltpu.SemaphoreType.DMA((N_DEV - 1,)), pltpu.SemaphoreType.DMA((N_DEV - 1,)),
                  pltpu.SemaphoreType.DMA(())]


def _exchange_shape(kind, x):
    return jax.ShapeDtypeStruct(x.shape if kind == "a2a" else (N_DEV,) + x.shape, x.dtype)


def _exchange(kind, x, name):
    def body(x_ref, o_ref, ssem, rsem, lsem):
        _exchange_start(kind, x_ref, o_ref, ssem, rsem, lsem)
        _exchange_wait(kind, x_ref, o_ref, ssem, rsem, lsem)

    return pl.pallas_call(
        body, name=name, out_shape=_exchange_shape(kind, x),
        in_specs=[pl.BlockSpec(memory_space=pl.ANY)], out_specs=pl.BlockSpec(memory_space=pl.ANY),
        scratch_shapes=list(_EXCHANGE_SEMS),
    )(x)


def _allgather(x, name):
    return _exchange("ag", x, name)


def _call(body, name, grid, in_specs, out_specs, out_shape, scratch_shapes, sem, args, comm=(), aliases=None):
    in_specs, out_specs, out_shape = list(in_specs), list(out_specs), list(out_shape)
    scratch_shapes = list(scratch_shapes)
    aliases = aliases or {}
    if not comm:
        outs = pl.pallas_call(body, name=name, grid=grid, in_specs=in_specs, out_specs=out_specs, out_shape=out_shape,
                              scratch_shapes=scratch_shapes, input_output_aliases=aliases,
                              compiler_params=_cparams(sem))(*args)
        return list(outs), []
    n_in, n_out, n_scr, n_c = len(in_specs), len(out_specs), len(scratch_shapes), len(comm)
    hbm = pl.BlockSpec(memory_space=pl.ANY)

    def wrapped(*refs):
        ins, cins = refs[:n_in], refs[n_in:n_in + n_c]
        outs = refs[n_in + n_c:n_in + n_c + n_out]
        couts = refs[n_in + n_c + n_out:n_in + 2 * n_c + n_out]
        scr = refs[n_in + 2 * n_c + n_out:n_in + 2 * n_c + n_out + n_scr]
        sems = refs[n_in + 2 * n_c + n_out + n_scr:]
        first = pl.program_id(0) == 0
        last = pl.program_id(0) == grid[0] - 1
        for ax in range(1, len(grid)):
            first = first & (pl.program_id(ax) == 0)
            last = last & (pl.program_id(ax) == grid[ax] - 1)

        @pl.when(first)
        def _():
            for c, (kind, _) in enumerate(comm):
                _exchange_start(kind, cins[c], couts[c], *sems[3 * c:3 * c + 3])

        body(*ins, *outs, *scr)

        @pl.when(last)
        def _():
            for c, (kind, _) in enumerate(comm):
                _exchange_wait(kind, cins[c], couts[c], *sems[3 * c:3 * c + 3])

    res = pl.pallas_call(
        wrapped, name=name, grid=grid,
        in_specs=in_specs + [hbm] * n_c, out_specs=out_specs + [hbm] * n_c,
        out_shape=out_shape + [_exchange_shape(kind, arr) for kind, arr in comm],
        scratch_shapes=scratch_shapes + list(_EXCHANGE_SEMS) * n_c, input_output_aliases=aliases,
        compiler_params=_cparams(("arbitrary",) * len(grid)),
    )(*args, *[arr for _, arr in comm])
    return list(res[:n_out]), list(res[n_out:])


def _matmul(name, pairs, n_acc, M, N, K, mode, tiles, extras, out_dtypes, epilogue, j_outer=False, comm=()):
    tm, tn, tk = tiles
    gm, gn, nk = M // tm, N // tn, K // tk
    assert gm * tm == M and gn * tn == N and nk * tk == K, (name, M, N, K, tiles)
    if j_outer:
        grid = (gn, gm, nk)
        ij = lambda g0, g1: (g1, g0)
    else:
        grid = (gm, gn, nk)
        ij = lambda g0, g1: (g0, g1)

    if mode in ("nn", "nt"):
        a_spec = pl.BlockSpec((tm, tk), lambda g0, g1, k: (ij(g0, g1)[0], k))
    else:
        a_spec = pl.BlockSpec((tk, tm), lambda g0, g1, k: (k, ij(g0, g1)[0]))
    if mode == "nt":
        b_spec = pl.BlockSpec((tn, tk), lambda g0, g1, k: (ij(g0, g1)[1], k))
    else:
        b_spec = pl.BlockSpec((tk, tn), lambda g0, g1, k: (k, ij(g0, g1)[1]))
    dims = {"nn": NN, "nt": NT, "tn": TN}[mode]
    mn_spec = pl.BlockSpec((tm, tn), lambda g0, g1, k: ij(g0, g1))
    n_spec = pl.BlockSpec((1, tn), lambda g0, g1, k: (0, ij(g0, g1)[1]))

    in_specs, args = [], []
    for a, b, _ in pairs:
        in_specs += [a_spec, b_spec]
        args += [a, b]
    for arr, kind in extras:
        in_specs.append(mn_spec if kind == "mn" else n_spec)
        args.append(arr)
    n_p, n_e, n_o = len(pairs), len(extras), len(out_dtypes)

    def body(*refs):
        ab = refs[:2 * n_p]
        ex = refs[2 * n_p:2 * n_p + n_e]
        outs = refs[2 * n_p + n_e:2 * n_p + n_e + n_o]
        accs = refs[2 * n_p + n_e + n_o:]
        k = pl.program_id(2)

        def partial_sums():
            sums = [None] * n_acc
            for p, (_, _, ai) in enumerate(pairs):
                d = _dot(ab[2 * p][...], ab[2 * p + 1][...], dims)
                sums[ai] = d if sums[ai] is None else sums[ai] + d
            return sums

        def finish(acc_vals):
            res = epilogue(acc_vals, [e[...] for e in ex])
            for o, r in zip(outs, res):
                o[...] = r.astype(o.dtype)

        def accumulate(first):
            w = _pick(tm if mode == "tn" else tn, (512, 384, 256))
            for c in range((tm if mode == "tn" else tn) // w):
                sl = slice(c * w, (c + 1) * w)
                sums = [None] * n_acc
                for p, (_, _, ai) in enumerate(pairs):
                    a_ref, b_ref = ab[2 * p], ab[2 * p + 1]
                    if mode == "tn":
                        d = _dot(a_ref[:, sl], b_ref[...], dims)
                    elif mode == "nn":
                        d = _dot(a_ref[...], b_ref[:, sl], dims)
                    else:
                        d = _dot(a_ref[...], b_ref[sl, :], dims)
                    sums[ai] = d if sums[ai] is None else sums[ai] + d
                idx = (sl, slice(None)) if mode == "tn" else (slice(None), sl)
                for ai, s in enumerate(sums):
                    if first:
                        accs[ai][idx] = s
                    else:
                        accs[ai][idx] += s

        if nk == 1:
            finish(partial_sums())
        else:
            pl.when(k == 0)(functools.partial(accumulate, True))
            pl.when(k > 0)(functools.partial(accumulate, False))

            @pl.when(k == nk - 1)
            def _():
                finish([a[...] for a in accs])

    outs, couts = _call(
        body, name, grid, in_specs, [mn_spec] * n_o,
        [jax.ShapeDtypeStruct((M, N), dt) for dt in out_dtypes],
        [pltpu.VMEM((tm, tn), F32) for _ in range(n_acc if nk > 1 else 0)],
        ("parallel", "parallel", "arbitrary"), args, comm)
    return (outs, couts) if comm else outs


def _rope_tables(L):
    t = jnp.arange(L, dtype=jnp.int32)
    f = jnp.arange(32, dtype=jnp.int32).astype(F32)
    ang = t.astype(F32)[:, None] * (ROPE_BASE ** (-f / 32.0))[None, :]
    cos, sin = jnp.cos(ang), jnp.sin(ang)
    ret = jnp.stack([jnp.tile(cos, (1, 4)), jnp.tile(jnp.concatenate([-sin, sin], axis=1), (1, 2))])
    f2 = jnp.arange(16, dtype=jnp.int32).astype(F32)
    inv2 = (ROPE_BASE ** (-f2 / 16.0))[None, :]
    ang_r = (t // GRID_W).astype(F32)[:, None] * inv2
    ang_c = (t % GRID_W).astype(F32)[:, None] * inv2
    cr, sr, cc, sc = jnp.cos(ang_r), jnp.sin(ang_r), jnp.cos(ang_c), jnp.sin(ang_c)
    att = jnp.stack([jnp.tile(jnp.concatenate([cr, cr, cc, cc], axis=1), (1, 2)),
                     jnp.tile(jnp.concatenate([-sr, sr, -sc, sc], axis=1), (1, 2))])
    return ret.astype(F32), att.astype(F32)


def _swap(x, sh):
    lane = lax.broadcasted_iota(jnp.int32, x.shape, 1)
    ra = pltpu.roll(x, LANES - sh, 1)
    rb = pltpu.roll(x, sh, 1)
    la = pltpu.roll(lane, LANES - sh, 1)
    partner = jnp.where((lane % (2 * sh)) < sh, lane + sh, lane - sh)
    return jnp.where(la == partner, ra, rb)


def _rope(x, cos, sin, sh):
    return x * cos + _swap(x, sh) * sin


def _rope_t(d, cos, sin, sh):
    return d * cos + _swap(d * sin, sh)


def _half_mask(shape, a):
    lane = lax.broadcasted_iota(jnp.int32, shape, 1)
    return (lane < 64) if a == 0 else (lane >= 64)


def _mod_fwd(s_in, w_l, b_l):
    D, C6 = w_l.shape
    tk = _pick(D, (512, 256, 128))
    nk = D // tk

    def body(s_ref, w_ref, b_ref, o_ref):
        k = pl.program_id(0)
        s = s_ref[...]
        s = s * _sigmoid(s)
        d = jnp.dot(s, w_ref[...], preferred_element_type=F32, precision=lax.Precision.HIGHEST)

        @pl.when(k == 0)
        def _():
            o_ref[...] = d + b_ref[...]

        @pl.when(k > 0)
        def _():
            o_ref[...] += d

    return pl.pallas_call(
        body, name="mod_fwd", grid=(nk,),
        in_specs=[pl.BlockSpec((16, tk), lambda k: (0, k)), pl.BlockSpec((tk, C6), lambda k: (k, 0)),
                  pl.BlockSpec((1, C6), lambda k: (0, 0))],
        out_specs=pl.BlockSpec((16, C6), lambda k: (0, 0)),
        out_shape=jax.ShapeDtypeStruct((16, C6), F32),
        compiler_params=_cparams(("arbitrary",)),
    )(s_in, w_l, b_l)


def _mod_bwd(s_in, dm, w_l):
    D, C6 = w_l.shape
    tk = _pick(D, (512, 256, 128))
    nk = D // tk

    def body(s_ref, dm_ref, w_ref, gw_ref, gc_ref):
        s = s_ref[...]
        sg = _sigmoid(s)
        act = s * sg
        dmv = dm_ref[...]
        gw_ref[...] = lax.dot_general(act, dmv, TN, preferred_element_type=F32, precision=lax.Precision.HIGHEST)
        ds = lax.dot_general(dmv, w_ref[...], NT, preferred_element_type=F32, precision=lax.Precision.HIGHEST)
        dsil = (sg * (1.0 + s * (1.0 - sg)))[8:9, :]
        gc_ref[...] = jnp.zeros((8, tk), F32) + jnp.sum(ds[8:16, :], axis=0, keepdims=True) * dsil

    return pl.pallas_call(
        body, name="mod_bwd", grid=(nk,),
        in_specs=[pl.BlockSpec((16, tk), lambda k: (0, k)), pl.BlockSpec((16, C6), lambda k: (0, 0)),
                  pl.BlockSpec((tk, C6), lambda k: (k, 0))],
        out_specs=[pl.BlockSpec((tk, C6), lambda k: (k, 0)), pl.BlockSpec((8, tk), lambda k: (0, k))],
        out_shape=[jax.ShapeDtypeStruct((D, C6), F32), jax.ShapeDtypeStruct((8, D), F32)],
        compiler_params=_cparams(("parallel",)),
    )(s_in, dm, w_l)


def _norm_rows(x):
    r = lax.rsqrt(jnp.mean(x * x, axis=-1, keepdims=True) + EPS)
    return x * r, r


def _modulate_fwd(name, x, ctx, g, mod, modc, comm=()):
    L, D = x.shape
    tr = ctx.shape[0]
    nx = L // tr

    def body(x_ref, c_ref, g_ref, m_ref, mc_ref, o_ref):
        i = pl.program_id(0)

        def run(src, m):
            n, _ = _norm_rows(src[...])
            o_ref[...] = (n * g_ref[...] * (1.0 + m[1:2, :]) + m[0:1, :]).astype(o_ref.dtype)

        @pl.when(i < nx)
        def _():
            run(x_ref, m_ref)

        @pl.when(i >= nx)
        def _():
            run(c_ref, mc_ref)

    row = pl.BlockSpec((tr, D), lambda i: (jnp.minimum(i, nx - 1), 0))
    vec = pl.BlockSpec((1, D), lambda i: (0, 0))
    mv = pl.BlockSpec((8, D), lambda i: (0, 0))
    return _call(
        body, name, (nx + 1,), [row, pl.BlockSpec((tr, D), lambda i: (0, 0)), vec, mv, mv],
        [pl.BlockSpec((tr, D), lambda i: (i, 0))], [jax.ShapeDtypeStruct((L + tr, D), BF16)], [],
        ("parallel",), (x, ctx, g, mod, modc), comm)


def _residual_modulate_fwd(name, x, fbr, gate, g, mod):
    L, D = x.shape
    tr = _pick(L, (512, 256, 128))

    def body(x_ref, f_ref, gt_ref, g_ref, m_ref, x1_ref, o_ref):
        x1 = x_ref[...] + gt_ref[...] * f_ref[...].astype(F32)
        x1_ref[...] = x1
        n, _ = _norm_rows(x1)
        o_ref[...] = (n * g_ref[...] * (1.0 + m_ref[1:2, :]) + m_ref[0:1, :]).astype(o_ref.dtype)

    row = pl.BlockSpec((tr, D), lambda i: (i, 0))
    vec = pl.BlockSpec((1, D), lambda i: (0, 0))
    return pl.pallas_call(
        body, name=name, grid=(L // tr,),
        in_specs=[row, row, vec, vec, pl.BlockSpec((8, D), lambda i: (0, 0))],
        out_specs=[row, row],
        out_shape=[jax.ShapeDtypeStruct((L, D), F32), jax.ShapeDtypeStruct((L, D), BF16)],
        compiler_params=_cparams(("parallel",)),
    )(x, fbr, gate, g, mod)


def _modulate_bwd(name, x, ctx, dh, g, mod, modc, dres, fbr, gate):
    L, D = x.shape
    tr = ctx.shape[0] if ctx is not None else _pick(L, (256, 128))
    nx = L // tr
    nt = nx + (1 if ctx is not None else 0)
    has_f = fbr is not None

    def body(*refs):
        refs = list(refs)
        x_ref = refs.pop(0)
        c_ref = refs.pop(0) if ctx is not None else None
        dh_ref, g_ref, m_ref = refs.pop(0), refs.pop(0), refs.pop(0)
        mc_ref = refs.pop(0) if ctx is not None else None
        dr_ref = refs.pop(0)
        f_ref = refs.pop(0) if has_f else None
        gt_ref = refs.pop(0) if has_f else None
        dx_ref = refs.pop(0)
        df_ref = refs.pop(0) if has_f else None
        acc_ref = refs.pop(0)
        i = pl.program_id(0)

        @pl.when(i == 0)
        def _():
            acc_ref[...] = jnp.zeros_like(acc_ref)

        def sums(src, m, base, grow):
            n, r = _norm_rows(src[...])
            d = dh_ref[...].astype(F32)
            gg = g_ref[...]
            sc1 = 1.0 + m[1:2, :]
            acc_ref[base:base + 1, :] += jnp.sum(d, axis=0, keepdims=True)
            dn = d * n
            acc_ref[base + 1:base + 2, :] += jnp.sum(dn, axis=0, keepdims=True) * gg
            acc_ref[grow:grow + 1, :] += jnp.sum(dn, axis=0, keepdims=True) * sc1
            dnv = d * (gg * sc1)
            return r * (dnv - n * jnp.mean(dnv * n, axis=-1, keepdims=True))

        def x_rows():
            dx = sums(x_ref, m_ref, 0, 2) + dr_ref[...]
            dx_ref[...] = dx
            if has_f:
                acc_ref[6:7, :] += jnp.sum(dx * f_ref[...].astype(F32), axis=0, keepdims=True)
                df_ref[...] = (dx * gt_ref[...]).astype(df_ref.dtype)

        if ctx is None:
            x_rows()
        else:
            pl.when(i < nx)(x_rows)

            @pl.when(i >= nx)
            def _():
                sums(c_ref, mc_ref, 3, 2)

    row = pl.BlockSpec((tr, D), lambda i: (jnp.minimum(i, nx - 1), 0))
    vec = pl.BlockSpec((1, D), lambda i: (0, 0))
    mv = pl.BlockSpec((8, D), lambda i: (0, 0))
    in_specs, args = [row], [x]
    if ctx is not None:
        in_specs.append(pl.BlockSpec((tr, D), lambda i: (0, 0)))
        args.append(ctx)
    in_specs += [pl.BlockSpec((tr, D), lambda i: (i, 0)), vec, mv]
    args += [dh, g, mod]
    if ctx is not None:
        in_specs.append(mv)
        args.append(modc)
    in_specs.append(row)
    args.append(dres)
    out_specs = [row]
    out_shape = [jax.ShapeDtypeStruct((L, D), F32)]
    if has_f:
        in_specs += [row, vec]
        args += [fbr, gate]
        out_specs.append(row)
        out_shape.append(jax.ShapeDtypeStruct((L, D), BF16))
    out_specs.append(pl.BlockSpec((16, D), lambda i: (0, 0)))
    out_shape.append(jax.ShapeDtypeStruct((16, D), F32))
    return pl.pallas_call(
        body, name=name, grid=(nt,), in_specs=in_specs, out_specs=out_specs, out_shape=out_shape,
        compiler_params=_cparams(("arbitrary",)),
    )(*args)


def _loss_head(x1, tgt, nf, fbr, gate):
    L, D = x1.shape
    tr = _pick(L, (256, 128))

    def body(x_ref, t_ref, w_ref, f_ref, gt_ref, dx_ref, df_ref, acc_ref):
        i = pl.program_id(0)

        @pl.when(i == 0)
        def _():
            acc_ref[...] = jnp.zeros_like(acc_ref)

        n, r = _norm_rows(x_ref[...] + gt_ref[...] * f_ref[...].astype(F32))
        w = w_ref[...]
        e = n * w - t_ref[...]
        acc_ref[0:1, :] += jnp.sum(e * e, axis=0, keepdims=True) * (0.5 / D)
        dout = e * (1.0 / D)
        acc_ref[1:2, :] += jnp.sum(dout * n, axis=0, keepdims=True)
        dn = dout * w
        dx = r * (dn - n * jnp.mean(dn * n, axis=-1, keepdims=True))
        dx_ref[...] = dx
        acc_ref[2:3, :] += jnp.sum(dx * f_ref[...].astype(F32), axis=0, keepdims=True)
        df_ref[...] = (dx * gt_ref[...]).astype(df_ref.dtype)

        @pl.when(i == pl.num_programs(0) - 1)
        def _():
            acc_ref[3:4, :] = jnp.zeros((1, D), F32) + jnp.sum(acc_ref[0:1, :])

    row = pl.BlockSpec((tr, D), lambda i: (i, 0))
    vec = pl.BlockSpec((1, D), lambda i: (0, 0))
    return pl.pallas_call(
        body, name="loss_head", grid=(L // tr,),
        in_specs=[row, row, vec, row, vec],
        out_specs=[row, row, pl.BlockSpec((8, D), lambda i: (0, 0))],
        out_shape=[jax.ShapeDtypeStruct((L, D), F32), jax.ShapeDtypeStruct((L, D), BF16),
                   jax.ShapeDtypeStruct((8, D), F32)],
        compiler_params=_cparams(("arbitrary",)),
    )(x1, tgt, nf, fbr, gate)


N_TAB = 7


def _ret_tables(rdb, Lc):
    def body(rd_ref, t_ref, c_ref):
        d = pl.program_id(0) // RET_HEADS
        fwd = d == 0
        lg = -jnp.exp(rd_ref[0])
        i = lax.broadcasted_iota(jnp.int32, (CHUNK, CHUNK), 0).astype(F32)
        j = lax.broadcasted_iota(jnp.int32, (CHUNK, CHUNK), 1).astype(F32)
        rel = jnp.where(fwd, i - j, j - i)
        mask = (rel > 0.0) | ((rel == 0.0) & fwd)
        dm = jnp.where(mask, jnp.exp(lg * jnp.maximum(rel, 0.0)), 0.0)
        t_ref[0, 0] = dm
        t_ref[0, 1] = rel * dm
        qc = jnp.where(fwd, i + 1.0, CHUNK - i)
        qw = jnp.exp(lg * qc)
        t_ref[0, 2] = qw
        t_ref[0, 3] = qw * qc
        kc = jnp.where(fwd, CHUNK - 1.0 - i, i)
        kw = jnp.exp(lg * kc)
        t_ref[0, 4] = kw
        t_ref[0, 5] = kw * kc
        t_ref[0, 6] = jnp.exp(lg * float(CHUNK)) + jnp.zeros((CHUNK, CHUNK), F32)
        m = lax.broadcasted_iota(jnp.int32, (Lc, LANES), 0).astype(F32)
        cc = jnp.where(fwd, Lc - 1.0 - m, m)
        cw = jnp.exp(lg * cc)
        c_ref[0, 0] = cw
        c_ref[0, 1] = cw * cc

    return pl.pallas_call(
        body, name="ret_tables", grid=(2 * RET_HEADS,),
        in_specs=[pl.BlockSpec((1, 1, LANES), lambda r: (r, 0, 0))],
        out_specs=[pl.BlockSpec((1, N_TAB, CHUNK, CHUNK), lambda r: (r, 0, 0, 0)),
                   pl.BlockSpec((1, 2, Lc, LANES), lambda r: (r, 0, 0, 0))],
        out_shape=[jax.ShapeDtypeStruct((2 * RET_HEADS, N_TAB, CHUNK, CHUNK), F32),
                   jax.ShapeDtypeStruct((2 * RET_HEADS, 2, Lc, LANES), F32)],
        compiler_params=_cparams(("parallel",)),
    )(rdb)


def _ret_ctx_state(P, ctab, L, Lc):
    cb = L // Lc

    def body(k_ref, v_ref, c_ref, s_ref):
        for p in range(RET_HEADS // 2):
            kp = k_ref[:, p * LANES:(p + 1) * LANES].astype(F32) * K_SCALE
            for a in range(2):
                h = 2 * p + a
                kh = jnp.where(_half_mask(kp.shape, a), kp, 0.0)
                vh = v_ref[:, h * RET_DV:(h + 1) * RET_DV]
                for d in range(2):
                    kw = (kh * c_ref[d * RET_HEADS + h, 0]).astype(BF16)
                    s_ref[d * RET_HEADS + h] = _dot(kw, vh, TN)

    return pl.pallas_call(
        body, name="ret_ctx_state", grid=(1,),
        in_specs=[pl.BlockSpec((Lc, 512), lambda i: (cb, C_RK // 512)),
                  pl.BlockSpec((Lc, 1024), lambda i: (cb, C_RV // 1024)),
                  pl.BlockSpec((2 * RET_HEADS, 2, Lc, LANES), lambda i: (0, 0, 0, 0))],
        out_specs=pl.BlockSpec((2 * RET_HEADS, LANES, RET_DV), lambda i: (0, 0, 0)),
        out_shape=jax.ShapeDtypeStruct((2 * RET_HEADS, LANES, RET_DV), F32),
        compiler_params=_cparams(("arbitrary",)),
    )(P, P, ctab)


def _ret_fwd(P, rope, tabs, s0, L, comm=()):
    n = L // CHUNK

    def body(qf, kf, vf, rf, qb, kb, vb, rb, t_ref, s0_ref, of_ref, ob_ref, stf_ref, stb_ref, st):
        s = pl.program_id(0)

        @pl.when(s == 0)
        def _():
            st[...] = s0_ref[...]

        units = []
        for d, (q_ref, k_ref, v_ref, r_ref, o_ref, so_ref) in enumerate(
                ((qf, kf, vf, rf, of_ref, stf_ref), (qb, kb, vb, rb, ob_ref, stb_ref))):
            cos, sin = r_ref[0], r_ref[1]
            for p in range(RET_HEADS // 2):
                qp = _rope(q_ref[:, p * LANES:(p + 1) * LANES].astype(F32), cos, sin, 32)
                kp = _rope(k_ref[:, p * LANES:(p + 1) * LANES].astype(F32), cos, sin, 32) * K_SCALE
                for a in range(2):
                    h = 2 * p + a
                    hm = _half_mask(qp.shape, a)
                    units.append(dict(r=d * RET_HEADS + h, h=h, a=a, o_ref=o_ref, so_ref=so_ref, v_ref=v_ref,
                                      qh=jnp.where(hm, qp, 0.0), kh=jnp.where(hm, kp, 0.0)))
        for u in units:
            u["sc"] = _dot(u["qh"].astype(BF16), u["kh"].astype(BF16), NT)
        for u in units:
            r, h = u["r"], u["h"]
            sp = st[r]
            u["so_ref"][0, h] = sp[u["a"] * RET_DK:(u["a"] + 1) * RET_DK, :]
            vh = u["v_ref"][:, h * RET_DV:(h + 1) * RET_DV]
            o = _dot((u["sc"] * t_ref[r, 0]).astype(BF16), vh, NN)
            o += _dot((u["qh"] * t_ref[r, 2]).astype(BF16), sp.astype(BF16), NN)
            u["o_ref"][:, h * RET_DV:(h + 1) * RET_DV] = o
        for u in units:
            r, h = u["r"], u["h"]
            vh = u["v_ref"][:, h * RET_DV:(h + 1) * RET_DV]
            st[r] = t_ref[r, 6] * st[r] + _dot((u["kh"] * t_ref[r, 4]).astype(BF16), vh, TN)

    fw = lambda s: s
    bw = lambda s: n - 1 - s

    def specs(cm):
        return [pl.BlockSpec((CHUNK, 512), lambda s: (cm(s), C_RQ // 512)),
                pl.BlockSpec((CHUNK, 512), lambda s: (cm(s), C_RK // 512)),
                pl.BlockSpec((CHUNK, 1024), lambda s: (cm(s), C_RV // 1024)),
                pl.BlockSpec((2, CHUNK, LANES), lambda s: (0, cm(s), 0))]

    full = lambda shp: pl.BlockSpec(shp, lambda s: (0,) * len(shp))
    return _call(
        body, "ret_fwd", (n,),
        specs(fw) + specs(bw) + [full((2 * RET_HEADS, N_TAB, CHUNK, CHUNK)), full((2 * RET_HEADS, LANES, RET_DV))],
        [pl.BlockSpec((CHUNK, 1024), lambda s: (fw(s), 0)),
         pl.BlockSpec((CHUNK, 1024), lambda s: (bw(s), 0)),
         pl.BlockSpec((1, RET_HEADS, RET_DK, RET_DV), lambda s: (fw(s), 0, 0, 0)),
         pl.BlockSpec((1, RET_HEADS, RET_DK, RET_DV), lambda s: (bw(s), 0, 0, 0))],
        [jax.ShapeDtypeStruct((L, 1024), F32), jax.ShapeDtypeStruct((L, 1024), F32),
         jax.ShapeDtypeStruct((n, RET_HEADS, RET_DK, RET_DV), F32),
         jax.ShapeDtypeStruct((n, RET_HEADS, RET_DK, RET_DV), F32)],
        [pltpu.VMEM((2 * RET_HEADS, LANES, RET_DV), F32)],
        ("arbitrary",), (P, P, P, rope, P, P, P, rope, tabs, s0), comm)


def _ret_finish_fwd(of, ob, P, L):
    tr = _pick(L, (512, 256, 128))

    def body(f_ref, b_ref, g_ref, y_ref):
        for h in range(RET_HEADS):
            sl = slice(h * RET_DV, (h + 1) * RET_DV)
            n, _ = _norm_rows(f_ref[:, sl] + b_ref[:, sl])
            g = g_ref[:, sl].astype(F32)
            y_ref[:, sl] = (n * (g * _sigmoid(g))).astype(y_ref.dtype)

    row = pl.BlockSpec((tr, 1024), lambda i: (i, 0))
    return pl.pallas_call(
        body, name="ret_finish_fwd", grid=(L // tr,),
        in_specs=[row, row, pl.BlockSpec((tr, 1024), lambda i: (i, C_RG // 1024))],
        out_specs=row, out_shape=jax.ShapeDtypeStruct((L, 2048), BF16),
        compiler_params=_cparams(("parallel",)),
    )(of, ob, P)


def _ret_finish_bwd(of, ob, P, dY, L):
    tr = _pick(L, (512, 256, 128))

    def body(f_ref, b_ref, g_ref, dy_ref, do_ref, dg_ref):
        for h in range(RET_HEADS):
            sl = slice(h * RET_DV, (h + 1) * RET_DV)
            n, r = _norm_rows(f_ref[:, sl] + b_ref[:, sl])
            g = g_ref[:, sl].astype(F32)
            sg = _sigmoid(g)
            dy = dy_ref[:, sl].astype(F32)
            dg_ref[:, sl] = (dy * n * (sg * (1.0 + g * (1.0 - sg)))).astype(dg_ref.dtype)
            dn = dy * (g * sg)
            do_ref[:, sl] = (r * (dn - n * jnp.mean(dn * n, axis=-1, keepdims=True))).astype(do_ref.dtype)

    row = pl.BlockSpec((tr, 1024), lambda i: (i, 0))
    return pl.pallas_call(
        body, name="ret_finish_bwd", grid=(L // tr,),
        in_specs=[row, row, pl.BlockSpec((tr, 1024), lambda i: (i, C_RG // 1024)), row],
        out_specs=[row, row],
        out_shape=[jax.ShapeDtypeStruct((L, 1024), BF16), jax.ShapeDtypeStruct((L, 1024), BF16)],
        compiler_params=_cparams(("parallel",)),
    )(of, ob, P, dY)


def _ret_bwd(P, rope, tabs, stf, stb, dO, L, comm=()):
    n = L // CHUNK

    def body(qf, kf, vf, rf, gf, sf, qb, kb, vb, rb, gb, sb, t_ref,
             dqf, dkf, dvf, dqb, dkb, dvb, ds0_ref, dlg_ref, ds):
        s = pl.program_id(0)

        @pl.when(s == 0)
        def _():
            ds[...] = jnp.zeros_like(ds)
            dlg_ref[...] = jnp.zeros_like(dlg_ref)

        units, pairs = [], []
        for d, (q_ref, k_ref, v_ref, r_ref, g_ref, s_ref, dq_ref, dk_ref, dv_ref) in enumerate(
                ((qf, kf, vf, rf, gf, sf, dqf, dkf, dvf), (qb, kb, vb, rb, gb, sb, dqb, dkb, dvb))):
            cos, sin = r_ref[0], r_ref[1]
            for p in range(RET_HEADS // 2):
                qp = _rope(q_ref[:, p * LANES:(p + 1) * LANES].astype(F32), cos, sin, 32)
                kp = _rope(k_ref[:, p * LANES:(p + 1) * LANES].astype(F32), cos, sin, 32) * K_SCALE
                pair = dict(p=p, cos=cos, sin=sin, dq_ref=dq_ref, dk_ref=dk_ref, us=[])
                pairs.append(pair)
                for a in range(2):
                    h = 2 * p + a
                    r = d * RET_HEADS + h
                    hm = _half_mask(qp.shape, a)
                    zero = jnp.zeros((RET_DK, RET_DV), F32)
                    sp = s_ref[0, h]
                    u = dict(r=r, h=h, dv_ref=dv_ref, qh=jnp.where(hm, qp, 0.0), kh=jnp.where(hm, kp, 0.0),
                             vh=v_ref[:, h * RET_DV:(h + 1) * RET_DV], gh=g_ref[:, h * RET_DV:(h + 1) * RET_DV],
                             sp=jnp.concatenate([sp, zero] if a == 0 else [zero, sp], axis=0),
                             dsn=ds[r])
                    u["qhb"], u["khb"] = u["qh"].astype(BF16), u["kh"].astype(BF16)
                    units.append(u)
                    pair["us"].append(u)
        for u in units:
            u["am"] = _dot(u["qhb"], u["khb"], NT)
            u["dar"] = _dot(u["gh"], u["vh"], NT)
            u["xq"] = _dot(u["gh"], u["sp"].astype(BF16), NT)
            u["yk"] = _dot(u["vh"], u["dsn"].astype(BF16), NT)
        for u in units:
            r = u["r"]
            dm = t_ref[r, 0]
            u["da"] = (u["dar"] * dm).astype(BF16)
            u["amd"] = (u["am"] * dm).astype(BF16)
            part = (jnp.sum(u["am"] * u["dar"] * t_ref[r, 1]) + jnp.sum(u["qh"] * t_ref[r, 3] * u["xq"])
                    + jnp.sum(u["kh"] * t_ref[r, 5] * u["yk"])
                    + float(CHUNK) * jnp.sum(t_ref[r, 6] * u["dsn"] * u["sp"]))
            dlg_ref[r:r + 1, :] += jnp.zeros((1, LANES), F32) + part
        for u in units:
            r, h = u["r"], u["h"]
            u["dq"] = _dot(u["da"], u["khb"], NN) + u["xq"] * t_ref[r, 2]
            u["dk"] = _dot(u["da"], u["qhb"], TN) + u["yk"] * t_ref[r, 4]
            u["dv_ref"][:, h * RET_DV:(h + 1) * RET_DV] = (
                _dot(u["amd"], u["gh"], TN) + _dot((u["kh"] * t_ref[r, 4]).astype(BF16), u["dsn"].astype(BF16), NN)
            ).astype(u["dv_ref"].dtype)
            ds[r] = t_ref[r, 6] * u["dsn"] + _dot((u["qh"] * t_ref[r, 2]).astype(BF16), u["gh"], TN)
        for pair in pairs:
            sl = slice(pair["p"] * LANES, (pair["p"] + 1) * LANES)
            u0, u1 = pair["us"]
            pair["dq_ref"][:, sl] = _rope_t(u0["dq"] + u1["dq"], pair["cos"], pair["sin"], 32).astype(BF16)
            pair["dk_ref"][:, sl] = _rope_t((u0["dk"] + u1["dk"]) * K_SCALE, pair["cos"], pair["sin"], 32).astype(BF16)

        @pl.when(s == n - 1)
        def _():
            ds0_ref[...] = ds[...]

    fw = lambda s: n - 1 - s
    bw = lambda s: s

    def specs(cm):
        return [pl.BlockSpec((CHUNK, 512), lambda s: (cm(s), C_RQ // 512)),
                pl.BlockSpec((CHUNK, 512), lambda s: (cm(s), C_RK // 512)),
                pl.BlockSpec((CHUNK, 1024), lambda s: (cm(s), C_RV // 1024)),
                pl.BlockSpec((2, CHUNK, LANES), lambda s: (0, cm(s), 0)),
                pl.BlockSpec((CHUNK, 1024), lambda s: (cm(s), 0)),
                pl.BlockSpec((1, RET_HEADS, RET_DK, RET_DV), lambda s: (cm(s), 0, 0, 0))]

    def ospecs(cm):
        return [pl.BlockSpec((CHUNK, 512), lambda s: (cm(s), 0)), pl.BlockSpec((CHUNK, 512), lambda s: (cm(s), 0)),
                pl.BlockSpec((CHUNK, 1024), lambda s: (cm(s), 0))]

    oshape = [jax.ShapeDtypeStruct((L, 512), BF16), jax.ShapeDtypeStruct((L, 512), BF16),
              jax.ShapeDtypeStruct((L, 1024), BF16)]
    full = lambda shp: pl.BlockSpec(shp, lambda s: (0,) * len(shp))
    return _call(
        body, "ret_bwd", (n,),
        specs(fw) + specs(bw) + [full((2 * RET_HEADS, N_TAB, CHUNK, CHUNK))],
        ospecs(fw) + ospecs(bw) + [full((2 * RET_HEADS, LANES, RET_DV)), full((2 * RET_HEADS, LANES))],
        oshape + oshape + [jax.ShapeDtypeStruct((2 * RET_HEADS, LANES, RET_DV), F32),
                           jax.ShapeDtypeStruct((2 * RET_HEADS, LANES), F32)],
        [pltpu.VMEM((2 * RET_HEADS, LANES, RET_DV), F32)],
        ("arbitrary",), (P, P, P, rope, dO, stf, P, P, P, rope, dO, stb, tabs), comm)


def _ret_ctx_bwd(P, ctab, ds0, dlg, rdb, L, Lc):
    cb = L // Lc

    def body(k_ref, v_ref, c_ref, ds_ref, dlg_ref, rd_ref, dk_ref, dv_ref, drd_ref):
        for p in range(RET_HEADS // 2):
            kp = k_ref[:, p * LANES:(p + 1) * LANES].astype(F32) * K_SCALE
            dkp = jnp.zeros((Lc, LANES), F32)
            for a in range(2):
                h = 2 * p + a
                kh = jnp.where(_half_mask(kp.shape, a), kp, 0.0)
                vh = v_ref[:, h * RET_DV:(h + 1) * RET_DV]
                dvh = jnp.zeros((Lc, RET_DV), F32)
                for d in range(2):
                    r = d * RET_HEADS + h
                    dsb = ds_ref[r].astype(BF16)
                    cw, cwc = c_ref[r, 0], c_ref[r, 1]
                    y = _dot(vh, dsb, NT)
                    dkp += y * cw
                    dvh += _dot((kh * cw).astype(BF16), dsb, NN)
                    lg = -jnp.exp(rd_ref[r])
                    drd_ref[r:r + 1, :] = (dlg_ref[r:r + 1, :] + jnp.sum(kh * cwc * y)) * lg
                dv_ref[:, h * RET_DV:(h + 1) * RET_DV] = dvh
            dk_ref[:, p * LANES:(p + 1) * LANES] = dkp * K_SCALE

    full = lambda shp: pl.BlockSpec(shp, lambda i: (0,) * len(shp))
    return pl.pallas_call(
        body, name="ret_ctx_bwd", grid=(1,),
        in_specs=[pl.BlockSpec((Lc, 512), lambda i: (cb, C_RK // 512)),
                  pl.BlockSpec((Lc, 1024), lambda i: (cb, C_RV // 1024)),
                  full((2 * RET_HEADS, 2, Lc, LANES)), full((2 * RET_HEADS, LANES, RET_DV)),
                  full((2 * RET_HEADS, LANES)), full((2 * RET_HEADS, 1, LANES))],
        out_specs=[full((Lc, 512)), full((Lc, 1024)), full((2 * RET_HEADS, LANES))],
        out_shape=[jax.ShapeDtypeStruct((Lc, 512), F32), jax.ShapeDtypeStruct((Lc, 1024), F32),
                   jax.ShapeDtypeStruct((2 * RET_HEADS, LANES), F32)],
        compiler_params=_cparams(("arbitrary",)),
    )(P, P, ctab, ds0, dlg, rdb)


BLK = 128
N_LOC = 3 * BLK


def _att_inputs(P, rope, L, Lc):
    n = L // BLK
    cb = L // Lc
    prev = lambda i: jnp.maximum(i - 1, 0)
    nxt = lambda i: jnp.minimum(i + 1, n - 1)
    specs = [pl.BlockSpec((BLK, 1024), lambda i: (i, C_AQ // 1024))]
    args = [P]
    for col in (C_AK // 256, C_AV // 256):
        for rm in (prev, lambda i: i, nxt):
            specs.append(pl.BlockSpec((BLK, 256), functools.partial(lambda i, rm, col: (rm(i), col), rm=rm, col=col)))
            args.append(P)
        specs.append(pl.BlockSpec((Lc, 256), functools.partial(lambda i, col: (cb, col), col=col)))
        args.append(P)
    for rm in (prev, lambda i: i, nxt):
        specs.append(pl.BlockSpec((2, BLK, LANES), functools.partial(lambda i, rm: (0, rm(i), 0), rm=rm)))
        args.append(rope)
    return specs, args


def _att_prep(i, n, refs, Lc):
    q_ref, kp_ref, kc_ref, kn_ref, kx_ref, vp_ref, vc_ref, vn_ref, vx_ref, rp_ref, rc_ref, rn_ref = refs
    cos = jnp.concatenate([rp_ref[0], rc_ref[0], rn_ref[0]], axis=0)
    sin = jnp.concatenate([rp_ref[1], rc_ref[1], rn_ref[1]], axis=0)
    kd, vd = [], []
    for t in range(ATT_KV // 2):
        sl = slice(t * LANES, (t + 1) * LANES)
        kl = jnp.concatenate([kp_ref[:, sl], kc_ref[:, sl], kn_ref[:, sl]], axis=0).astype(F32)
        kl = _rope(kl, cos, sin, 16)
        ka = jnp.concatenate([kl, kx_ref[:, sl].astype(F32)], axis=0)
        va = jnp.concatenate([vp_ref[:, sl], vc_ref[:, sl], vn_ref[:, sl], vx_ref[:, sl]], axis=0).astype(F32)
        kr, vr = pltpu.roll(ka, 64, 1), pltpu.roll(va, 64, 1)
        for b in range(2):
            hm = _half_mask(ka.shape, b)
            kd.append(jnp.where(hm, ka, kr).astype(BF16))
            vd.append(jnp.where(hm, va, vr).astype(BF16))
    nk = N_LOC + Lc
    rr = lax.broadcasted_iota(jnp.int32, (BLK, nk), 0)
    ss = lax.broadcasted_iota(jnp.int32, (BLK, nk), 1)
    lo = jnp.where(i == 0, BLK, 0)
    hi = jnp.where(i == n - 1, 2 * BLK, N_LOC)
    valid = (ss >= N_LOC) | ((ss >= rr) & (ss <= rr + 2 * BLK) & (ss >= lo) & (ss < hi))
    bias = jnp.where(valid, 0.0, NEG)
    return kd, vd, jnp.concatenate([bias] * 4, axis=0), rc_ref[0], rc_ref[1]


LOG2E = 1.4426950408889634
LN2 = 0.6931471805599453
Q_SCALE = A_SCALE * LOG2E


def _stack4(ref, g, f=None):
    parts = []
    for jp in range(2):
        t = ref[:, (2 * g + jp) * LANES:(2 * g + jp + 1) * LANES].astype(F32)
        if f is not None:
            t = f(t)
        for a in range(2):
            parts.append(jnp.where(_half_mask(t.shape, a), t, 0.0))
    return jnp.concatenate(parts, axis=0)


def _unstack4(x4, jp):
    r0 = 2 * jp * BLK
    lo = x4[r0:r0 + BLK]
    hi = x4[r0 + BLK:r0 + 2 * BLK]
    return jnp.where(_half_mask(lo.shape, 0), lo, hi)


def _softmax_parts(s, bias4, sink_ref, g):
    sink_col = LOG2E * jnp.concatenate(
        [jnp.zeros((BLK, 1), F32) + sink_ref[4 * g + r:4 * g + r + 1, 0:1] for r in range(4)], axis=0)
    s = s + bias4
    m = jnp.maximum(jnp.max(s, axis=-1, keepdims=True), sink_col)
    e = jnp.exp2(s - m)
    es = jnp.exp2(sink_col - m)
    return e, es, jnp.sum(e, axis=-1, keepdims=True) + es


def _att_fwd(P, rope, sinkb, Y, L, Lc, comm=()):
    n = L // BLK
    specs, args = _att_inputs(P, rope, L, Lc)

    def body(*refs):
        sink_ref, o_ref = refs[12], refs[14]
        i = pl.program_id(0)
        kd, vd, bias4, cq, sq = _att_prep(i, n, refs[:12], Lc)
        for g in range(ATT_KV):
            q4 = _stack4(refs[0], g, lambda t: _rope(t, cq, sq, 16) * Q_SCALE).astype(BF16)
            e, _, l = _softmax_parts(_dot(q4, kd[g], NT), bias4, sink_ref, g)
            o4 = _dot(e.astype(BF16), vd[g], NN) * (1.0 / l)
            for jp in range(2):
                c0 = (2 * g + jp) * LANES
                o_ref[:, c0:c0 + LANES] = _unstack4(o4, jp).astype(o_ref.dtype)

    return _call(
        body, "att_fwd", (n,),
        specs + [pl.BlockSpec((ATT_HEADS, LANES), lambda i: (0, 0)), pl.BlockSpec(memory_space=pl.ANY)],
        [pl.BlockSpec((BLK, 1024), lambda i: (i, 1))], [jax.ShapeDtypeStruct((L, 2048), BF16)], [],
        ("parallel",), (*args, sinkb, Y), comm, aliases={13: 0})


def _att_bwd(P, rope, sinkb, Y, dY, L, Lc, comm=()):
    n = L // BLK
    specs, args = _att_inputs(P, rope, L, Lc)
    nk = N_LOC + Lc

    def body(*refs):
        sink_ref, y_ref, dy_ref = refs[12], refs[13], refs[14]
        dq_ref, dkl_ref, dvl_ref, dkx_ref, dvx_ref, dsk_ref = refs[15:21]
        i = pl.program_id(0)

        @pl.when(i == 0)
        def _():
            dkx_ref[...] = jnp.zeros_like(dkx_ref)
            dvx_ref[...] = jnp.zeros_like(dvx_ref)
            dsk_ref[...] = jnp.zeros_like(dsk_ref)

        kd, vd, bias4, cq, sq = _att_prep(i, n, refs[:12], Lc)
        for t in range(ATT_KV // 2):
            dk_halves, dv_halves = [], []
            for b in range(2):
                g = 2 * t + b
                q4 = _stack4(refs[0], g, lambda x: _rope(x, cq, sq, 16) * Q_SCALE).astype(BF16)
                do4 = _stack4(dy_ref, g)
                delta = jnp.sum(do4 * _stack4(y_ref, g), axis=-1, keepdims=True)
                do4b = do4.astype(BF16)
                e, es, l = _softmax_parts(_dot(q4, kd[g], NT), bias4, sink_ref, g)
                inv = 1.0 / l
                p = e * inv
                dsc = (p * (_dot(do4b, vd[g], NT) - delta)).astype(BF16)
                dsr = es * inv * delta
                for r in range(4):
                    h = 4 * g + r
                    dsk_ref[h:h + 1, :] += jnp.zeros((1, LANES), F32) - jnp.sum(dsr[r * BLK:(r + 1) * BLK])
                dq4 = _dot(dsc, kd[g], NN) * A_SCALE
                for jp in range(2):
                    c0 = (2 * g + jp) * LANES
                    dq_ref[:, c0:c0 + LANES] = _rope_t(_unstack4(dq4, jp), cq, sq, 16).astype(dq_ref.dtype)
                dkd = _dot(q4, dsc, TN) * LN2
                dvd = _dot(do4b, p.astype(BF16), TN)
                dk_halves.append(dkd[:ATT_DH] + dkd[ATT_DH:])
                dv_halves.append(dvd[:ATT_DH] + dvd[ATT_DH:])
            dk_t = jnp.concatenate(dk_halves, axis=0).T
            dv_t = jnp.concatenate(dv_halves, axis=0).T
            sl = slice(t * LANES, (t + 1) * LANES)
            dkl_ref[0, :, sl] = dk_t[:N_LOC]
            dvl_ref[0, :, sl] = dv_t[:N_LOC]
            dkx_ref[:, sl] += dk_t[N_LOC:]
            dvx_ref[:, sl] += dv_t[N_LOC:]

    row = pl.BlockSpec((BLK, 1024), lambda i: (i, 0))
    loc = pl.BlockSpec((1, N_LOC, 256), lambda i: (i, 0, 0))
    cx = pl.BlockSpec((Lc, 256), lambda i: (0, 0))
    return _call(
        body, "att_bwd", (n,),
        specs + [pl.BlockSpec((ATT_HEADS, LANES), lambda i: (0, 0))] + [pl.BlockSpec((BLK, 1024), lambda i: (i, 1))] * 2,
        [row, loc, loc, cx, cx, pl.BlockSpec((ATT_HEADS, LANES), lambda i: (0, 0))],
        [jax.ShapeDtypeStruct((L, 1024), BF16), jax.ShapeDtypeStruct((n, N_LOC, 256), F32),
         jax.ShapeDtypeStruct((n, N_LOC, 256), F32), jax.ShapeDtypeStruct((Lc, 256), F32),
         jax.ShapeDtypeStruct((Lc, 256), F32), jax.ShapeDtypeStruct((ATT_HEADS, LANES), F32)], [],
        ("arbitrary",), (*args, sinkb, Y, dY), comm)


def _assemble_dp(L, Lc, dqf, dqb, dkf, dkb, dvf, dvb, drg, daq, dkl, dvl, rope_att, dck, dcv, dkx, dvx):
    n = L // BLK
    nc = Lc // BLK

    def body(dqf_r, dqb_r, dkf_r, dkb_r, dvf_r, dvb_r, drg_r, daq_r, kl0, kl1, kl2, vl0, vl1, vl2, rp_r,
             dck_r, dcv_r, dkx_r, dvx_r, o_ref):
        i = pl.program_id(0)

        @pl.when(i < n)
        def _():
            add = lambda a, b: (a[...].astype(F32) + b[...].astype(F32)).astype(o_ref.dtype)
            o_ref[:, C_RQ:C_RK] = add(dqf_r, dqb_r)
            o_ref[:, C_RK:C_RV] = add(dkf_r, dkb_r)
            o_ref[:, C_RV:C_RG] = add(dvf_r, dvb_r)
            o_ref[:, C_RG:C_AQ] = drg_r[...].astype(o_ref.dtype)
            o_ref[:, C_AQ:C_AK] = daq_r[...].astype(o_ref.dtype)
            w0 = jnp.where(i > 0, 1.0, 0.0)
            w2 = jnp.where(i < n - 1, 1.0, 0.0)
            dk = kl0[0] * w0 + kl1[0] + kl2[0] * w2
            dv = vl0[0] * w0 + vl1[0] + vl2[0] * w2
            for t in range(ATT_KV // 2):
                sl = slice(t * LANES, (t + 1) * LANES)
                o_ref[:, C_AK + t * LANES:C_AK + (t + 1) * LANES] = _rope_t(
                    dk[:, sl], rp_r[0], rp_r[1], 16).astype(o_ref.dtype)
            o_ref[:, C_AV:D_PROJ] = dv.astype(o_ref.dtype)

        @pl.when(i >= n)
        def _():
            o_ref[:, C_RQ:C_RK] = jnp.zeros((BLK, C_RK - C_RQ), o_ref.dtype)
            o_ref[:, C_RK:C_RV] = dck_r[...].astype(o_ref.dtype)
            o_ref[:, C_RV:C_RG] = dcv_r[...].astype(o_ref.dtype)
            o_ref[:, C_RG:C_AK] = jnp.zeros((BLK, C_AK - C_RG), o_ref.dtype)
            o_ref[:, C_AK:C_AV] = dkx_r[...].astype(o_ref.dtype)
            o_ref[:, C_AV:D_PROJ] = dvx_r[...].astype(o_ref.dtype)

    xm = lambda i: jnp.minimum(i, n - 1)
    cm = lambda i: jnp.clip(i - n, 0, nc - 1)
    r512 = pl.BlockSpec((BLK, 512), lambda i: (xm(i), 0))
    r1024 = pl.BlockSpec((BLK, 1024), lambda i: (xm(i), 0))
    part = lambda off: pl.BlockSpec((1, BLK, 256), lambda i: (jnp.clip(xm(i) + off, 0, n - 1), 1 - off, 0))
    return pl.pallas_call(
        body, name="assemble_dp", grid=(n + nc,),
        in_specs=[r512, r512, r512, r512, r1024, r1024, r1024, r1024,
                  part(-1), part(0), part(1), part(-1), part(0), part(1),
                  pl.BlockSpec((2, BLK, LANES), lambda i: (0, xm(i), 0)),
                  pl.BlockSpec((BLK, 512), lambda i: (cm(i), 0)), pl.BlockSpec((BLK, 1024), lambda i: (cm(i), 0)),
                  pl.BlockSpec((BLK, 256), lambda i: (cm(i), 0)), pl.BlockSpec((BLK, 256), lambda i: (cm(i), 0))],
        out_specs=pl.BlockSpec((BLK, D_PROJ), lambda i: (i, 0)),
        out_shape=jax.ShapeDtypeStruct((L + Lc, D_PROJ), BF16),
        compiler_params=_cparams(("parallel",)),
    )(dqf, dqb, dkf, dkb, dvf, dvb, drg, daq, dkl, dkl, dkl, dvl, dvl, dvl, rope_att, dck, dcv, dkx, dvx)


def _adam_math(w, g, m, v):
    m = ADAM_B1 * m + (1.0 - ADAM_B1) * g
    v = ADAM_B2 * v + (1.0 - ADAM_B2) * (g * g)
    m_hat = m / (1.0 - ADAM_B1 ** ADAM_STEP)
    v_hat = v / (1.0 - ADAM_B2 ** ADAM_STEP)
    delta = -ADAM_LR * (m_hat / (jnp.sqrt(v_hat) + ADAM_EPS) + ADAM_WD * w)
    return delta, m, v


def _adam(name, w, m, v, g=None, parts=None):
    R, C = w.shape
    tr = _pick(R, (256, 128, 64, 32, 16, 8))
    summed = parts is not None
    n_parts = parts.shape[0] if summed else 0

    def body(w_ref, m_ref, v_ref, g_ref, go_ref, d_ref, mo_ref, vo_ref):
        if summed:
            gv = g_ref[0].astype(F32)
            for j in range(1, n_parts):
                gv = gv + g_ref[j].astype(F32)
        else:
            gv = g_ref[...]
        d, mn, vn = _adam_math(w_ref[...], gv, m_ref[...], v_ref[...])
        go_ref[...] = gv
        d_ref[...] = d
        mo_ref[...] = mn
        vo_ref[...] = vn

    row = pl.BlockSpec((tr, C), lambda i: (i, 0))
    gspec = pl.BlockSpec((n_parts, tr, C), lambda i: (0, i, 0)) if summed else row
    return pl.pallas_call(
        body, name=name, grid=(R // tr,),
        in_specs=[row, row, row, gspec], out_specs=[row] * 4,
        out_shape=[jax.ShapeDtypeStruct((R, C), F32)] * 4,
        compiler_params=_cparams(("parallel",)),
    )(w, m, v, parts if summed else g)


def _rows_full(g):
    _, R, D = g.shape
    return g.reshape(N_DEV * R, D)


def _rows_slots(g):
    N, D = g.shape
    return g.reshape(N_DEV, N // N_DEV, D)


def _pad_rows(a, rows):
    return jnp.concatenate([a, jnp.zeros((rows - a.shape[0],) + a.shape[1:], a.dtype)], axis=0)


def kernel(x, c, ctx, c_ctx, w_mod, b_mod, norm_mix, norm_ffn, w_in, ret_decay, attn_sink, w_out, w_gate, w_up, w_down, norm_final, loss_target, m_c_ctx, m_w_mod, m_b_mod, m_norm_mix, m_norm_ffn, m_w_in, m_ret_decay, m_attn_sink, m_w_out, m_w_gate, m_w_up, m_w_down, m_norm_final, v_c_ctx, v_w_mod, v_b_mod, v_norm_mix, v_norm_ffn, v_w_in, v_ret_decay, v_attn_sink, v_w_out, v_w_gate, v_w_up, v_w_down, v_norm_final):
    L, D = x.shape[1], x.shape[2]
    Lc = ctx.shape[1]
    DF = w_gate.shape[2] * N_DEV
    C6 = w_mod.shape[2]
    me = _my_id()
    xs, cx, tgt = x[0], ctx[0], loss_target[0]

    ag_in = ("ag2", w_in[0].T.astype(BF16))
    ag_out, ag_gate = ("ag2", w_out[0].astype(BF16)), ("ag2", w_gate[0].T.astype(BF16))
    ag_up, ag_down = ("ag2", w_up[0].T.astype(BF16)), ("ag2", w_down[0].astype(BF16))

    cs = _allgather(c, "ag_c")[:, 0, :]
    s_in = _pad_rows(jnp.concatenate([cs, c_ctx[None, :]], axis=0), 16)
    b_l = lax.dynamic_slice_in_dim(b_mod, me * C6, C6, axis=1)
    mod_parts = _allgather(_mod_fwd(s_in, w_mod[0], b_l), "ag_mod")
    mod = _pad_rows(lax.dynamic_index_in_dim(mod_parts, me, axis=1, keepdims=False).reshape(6, D), 8)
    modc = _pad_rows(mod_parts[:, N_DEV, :].reshape(6, D), 8)
    mix_mod, ffn_mod = mod, jnp.roll(mod, -3, axis=0)
    gt_m, gt_f = mod[2:3], mod[5:6]

    rope_ret, rope_att = _rope_tables(L)
    rdb = jnp.broadcast_to(ret_decay[0].reshape(2 * RET_HEADS, 1, 1), (2 * RET_HEADS, 1, LANES))
    sinkb = jnp.broadcast_to(attn_sink[0].reshape(ATT_HEADS, 1), (ATT_HEADS, LANES))

    tm = _pick(L + Lc, (1408, 768, 512, 384, 256, 128))
    tmx = _pick(L, (1024, 512, 256, 128))

    (H,), (g_in,) = _modulate_fwd("mod_mix_fwd", xs, cx, norm_mix, mix_mod, modc, comm=[ag_in])
    W_inT = _rows_full(g_in)
    ident = lambda a, e: a
    tP, tD, tF = _pick(D_PROJ, (1152, 768, 512)), _pick(D, (2048, 1024, 512)), _pick(DF, (512, 256, 128))
    (P,), (g_gate,) = _matmul("mm_in", [(H, W_inT, 0)], 1, L + Lc, D_PROJ, D, "nt",
                              (tm, _pick(D_PROJ, (1536, 768, 512)), D), [], [BF16], ident,
                              comm=[ag_gate])
    W_gateT = _rows_full(g_gate)
    tabs, ctab = _ret_tables(rdb, Lc)
    s0 = _ret_ctx_state(P, ctab, L, Lc)
    (o_f, o_b, st_f, st_b), (g_out,) = _ret_fwd(P, rope_ret, tabs, s0, L, comm=[ag_out])
    W_out = _rows_full(g_out)
    Y_half = _ret_finish_fwd(o_f, o_b, P, L)
    (Y,), (g_up,) = _att_fwd(P, rope_att, sinkb, Y_half, L, Lc, comm=[ag_up])
    W_upT = _rows_full(g_up)
    KO = Y.shape[1]
    f_mix = _matmul("mm_out", [(Y, W_out, 0)], 1, L, D, KO, "nn", (tmx, tD, KO), [], [BF16], ident)[0]

    x1, H2 = _residual_modulate_fwd("mod_ffn_fwd", xs, f_mix, gt_m, norm_ffn, ffn_mod)

    def swiglu_epi(a, e):
        sg = _sigmoid(a[0])
        act = a[0] * sg
        return [act, a[1] * (sg * (1.0 + a[0] * (1.0 - sg))), act * a[1]]

    tm2 = tmx
    (act, up_dact, hmid), (g_down,) = _matmul("mm_gate_up", [(H2, W_gateT, 0), (H2, W_upT, 1)], 2, L, DF, D, "nt",
                                              (tm2, tF, D), [], [BF16, BF16, BF16], swiglu_epi, comm=[ag_down])
    W_down = _rows_full(g_down)
    f_ffn = _matmul("mm_down", [(hmid, W_down, 0)], 1, L, D, DF, "nn", (tm2, tD, _pick(DF, (1408, 512, 256, 128))),
                    [], [BF16], ident)[0]

    dx2, dFf, sums_l = _loss_head(x1, tgt, norm_final.reshape(1, D), f_ffn, gt_f)

    def dswiglu_epi(a, e):
        return [a[0] * e[0].astype(F32), a[0] * e[1].astype(F32)]

    dga, dup = _matmul("mm_d_down", [(dFf, W_down, 0)], 1, L, DF, D, "nt", (tm2, tF, D),
                       [(up_dact, "mn"), (act, "mn")], [BF16, BF16], dswiglu_epi)
    tkt, tkl = _pick(L, (512, 256, 128)), _pick(L, (1024, 512, 256, 128))
    dW_down =_matmul("mm_gw_down", [(hmid, dFf, 0)], 1, DF, D, L, "tn",
                      (_pick(DF, (1408, 512, 256, 128)), tD, tkt), [], [BF16], ident)[0]
    (dW_gateT, dW_upT), (p_down,) = _matmul("mm_gw_gate_up", [(dga, H2, 0), (dup, H2, 1)], 2, DF, D, L, "tn",
                                            (tF, tD, tkl), [], [BF16, BF16], ident,
                                            comm=[("a2a", _rows_slots(dW_down))])
    (dH2,), (p_gate,) = _matmul("mm_d_gate_up", [(dga, W_gateT, 0), (dup, W_upT, 0)], 1, L, D, DF, "nn",
                                (tm2, tD, tF), [], [BF16], ident, comm=[("a2a", _rows_slots(dW_gateT))])
    dx1, dFm, sums_f = _modulate_bwd("mod_ffn_bwd", x1, None, dH2, norm_ffn, ffn_mod, None, dx2, f_mix, gt_m)

    tO = _pick(KO, (2048, 1024, 512))
    dY = _matmul("mm_d_out", [(dFm, W_out, 0)], 1, L, KO, D, "nt", (tmx, tO, D), [], [BF16], ident)[0]
    dW_out = _matmul("mm_gw_out", [(Y, dFm, 0)], 1, KO, D, L, "tn",
                     (_pick(KO, (1024, 512)), tD, _pick(L, (2048, 1024, 512, 256, 128))), [], [BF16],
                     ident)[0]
    dO, drg = _ret_finish_bwd(o_f, o_b, P, dY, L)
    (dqf, dkf, dvf, dqb, dkb, dvb, ds0, dlg), (p_out,) = _ret_bwd(
        P, rope_ret, tabs, st_f, st_b, dO, L, comm=[("a2a", _rows_slots(dW_out))])
    dck, dcv, d_rd = _ret_ctx_bwd(P, ctab, ds0, dlg, rdb, L, Lc)
    (daq, dkl, dvl, dkx, dvx, d_sink), (p_up,) = _att_bwd(
        P, rope_att, sinkb, Y, dY, L, Lc, comm=[("a2a", _rows_slots(dW_upT))])
    dP = _assemble_dp(L, Lc, dqf, dqb, dkf, dkb, dvf, dvb, drg, daq, dkl, dvl, rope_att, dck, dcv, dkx, dvx)
    tkc = _pick(L + Lc, (768, 256, 128))
    dW_inT = _matmul("mm_gw_in", [(dP, H, 0)], 1, D_PROJ, D, L + Lc, "tn", (tP, tD, tkc), [], [BF16], ident)[0]
    (dH,), (p_in,) = _matmul("mm_d_in", [(dP, W_inT, 0)], 1, L + Lc, D, D_PROJ, "nn",
                             (tm, tD, _pick(D_PROJ, (768, 512, 256))), [], [BF16], ident,
                             comm=[("a2a", _rows_slots(dW_inT))])
    grad_x, sums_m = _modulate_bwd("mod_mix_bwd", xs, cx, dH, norm_mix, mix_mod, modc, dx1, None, None)

    zero = jnp.zeros((1, D), F32)
    dmod = jnp.concatenate([sums_m[0:1], sums_m[1:2], sums_f[6:7], sums_f[0:1], sums_f[1:2], sums_l[2:3]], axis=1)
    dmodc = jnp.concatenate([sums_m[3:4], sums_m[4:5], zero, zero, zero, zero], axis=1)
    dm_all = _allgather(jnp.concatenate([dmod, dmodc], axis=0), "ag_dmod")
    dm_cols = lax.dynamic_slice_in_dim(dm_all, me * C6, C6, axis=2)
    dm_in = jnp.concatenate([dm_cols[:, 0, :], dm_cols[:, 1, :]], axis=0)
    s_bwd = jnp.concatenate([cs, jnp.broadcast_to(c_ctx[None, :], (N_DEV, D))], axis=0)
    g_w_mod, dsil = _mod_bwd(s_bwd, dm_in, w_mod[0])

    lane_pad = lambda a: _pad_rows(a.reshape(-1, 1), LANES).reshape(1, LANES)
    pack = jnp.concatenate([dsil[0:1], sums_m[2:3], sums_f[2:3], sums_l[1:2],
                            lane_pad(d_rd[:, 0]), lane_pad(d_sink[:, 0]), sums_l[3:4, 0:LANES]], axis=1)
    packs = _allgather(pack, "ag_small")
    zl = jnp.zeros((1, LANES), F32)

    def pack_w(a_c, a_nm, a_nf, a_fin, a_rd, a_sk):
        return jnp.concatenate([a_c.reshape(1, D), a_nm, a_nf, a_fin.reshape(1, D), lane_pad(a_rd.reshape(-1)),
                                lane_pad(a_sk.reshape(-1)), zl], axis=1)

    sg, sd, sm, sv = _adam("adam_small", pack_w(c_ctx, norm_mix, norm_ffn, norm_final, ret_decay, attn_sink),
                           pack_w(m_c_ctx, m_norm_mix, m_norm_ffn, m_norm_final, m_ret_decay, m_attn_sink),
                           pack_w(v_c_ctx, v_norm_mix, v_norm_ffn, v_norm_final, v_ret_decay, v_attn_sink),
                           parts=packs)
    loss = sg[0, 4 * D + 2 * LANES]

    def unpack(a):
        return (a[0, 0:D], a[:, D:2 * D], a[:, 2 * D:3 * D], a[0, 3 * D:4 * D],
                a[0, 4 * D:4 * D + 2 * RET_HEADS].reshape(1, 2, RET_HEADS),
                a[:, 4 * D + LANES:4 * D + LANES + ATT_HEADS])

    bg, bd, bm, bv = _adam("adam_b_mod", b_mod, m_b_mod, v_b_mod, parts=dm_all.reshape(2 * N_DEV, 1, 6 * D))
    wg, wd, wm, wv = _adam("adam_w_mod", w_mod[0], m_w_mod[0], v_w_mod[0], g=g_w_mod)

    big = {}
    for nm, w, m, v, parts, transposed in (
            ("w_in", w_in, m_w_in, v_w_in, p_in, True), ("w_out", w_out, m_w_out, v_w_out, p_out, False),
            ("w_gate", w_gate, m_w_gate, v_w_gate, p_gate, True), ("w_up", w_up, m_w_up, v_w_up, p_up, True),
            ("w_down", w_down, m_w_down, v_w_down, p_down, False)):
        if transposed:
            res = [a.T for a in _adam("adam_" + nm, w[0].T, m[0].T, v[0].T, parts=parts)]
        else:
            res = _adam("adam_" + nm, w[0], m[0], v[0], parts=parts)
        big[nm] = [a[None] for a in res]

    g_s, d_s, m_s, v_s = unpack(sg), unpack(sd), unpack(sm), unpack(sv)

    def leaves(k, small, bmod, wmod):
        return (small[0], wmod[None], bmod, small[1], small[2], big["w_in"][k], small[4], small[5],
                big["w_out"][k], big["w_gate"][k], big["w_up"][k], big["w_down"][k], small[3])

    return (loss, grad_x[None], *leaves(0, g_s, bg, wg), *leaves(1, d_s, bd, wd),
            *leaves(2, m_s, bm, wm), *leaves(3, v_s, bv, wv))
```

```python
import functools

import jax
import jax.numpy as jnp
from jax import lax
from jax.experimental import pallas as pl
from jax.experimental.pallas import tpu as pltpu

F32 = jnp.float32
BF16 = jnp.bfloat16

N_DEV = 8
LANES = 128
RET_HEADS = 8
RET_DK = 64
RET_DV = 128
CHUNK = 128
ATT_HEADS = 16
ATT_KV = 4
ATT_DH = 64
GRID_W = 64
ROPE_BASE = 10000.0
EPS = 1e-6
NEG = -1e30
C_RQ, C_RK, C_RV, C_RG, C_AQ, C_AK, C_AV, D_PROJ = 0, 512, 1024, 2048, 3072, 4096, 4352, 4608
K_SCALE = RET_DK ** -0.5
A_SCALE = ATT_DH ** -0.5

ADAM_LR, ADAM_B1, ADAM_B2, ADAM_EPS, ADAM_WD, ADAM_STEP = 0.001, 0.9, 0.999, 1e-08, 0.01, 10

VMEM_BIG = 52 * 1024 * 1024

NN = (((1,), (0,)), ((), ()))
NT = (((1,), (1,)), ((), ()))
TN = (((0,), (0,)), ((), ()))


def _dot(a, b, dims):
    return lax.dot_general(a, b, dims, preferred_element_type=F32)


def _cparams(sem, vmem=VMEM_BIG):
    return pltpu.CompilerParams(dimension_semantics=sem, vmem_limit_bytes=vmem)


def _pick(dim, prefs):
    for p in prefs:
        if dim % p == 0:
            return p
    return dim


def _my_id():
    return lax.axis_index("x") * 4 + lax.axis_index("y") * 2 + lax.axis_index("c")


def _sigmoid(x):
    return 0.5 * jnp.tanh(0.5 * x) + 0.5


def _peers():
    mx, my, mc = lax.axis_index("x"), lax.axis_index("y"), lax.axis_index("c")
    out = []
    for k in range(1, N_DEV):
        kx, ky, kc = (k >> 2) & 1, (k >> 1) & 1, k & 1
        px = 1 - mx if kx else mx
        py = 1 - my if ky else my
        pc = 1 - mc if kc else mc
        out.append(((px, py, pc), px * 4 + py * 2 + pc))
    return out


def _exchange_copies(kind, x_ref, o_ref, ssem, rsem, lsem):
    me = _my_id()
    loc = pltpu.make_async_copy(x_ref if kind == "ag" else x_ref.at[me], o_ref.at[me], lsem)
    cps = []
    for k, (peer, pid) in enumerate(_peers()):
        cps.append(pltpu.make_async_remote_copy(
            src_ref=x_ref if kind == "ag" else x_ref.at[pid], dst_ref=o_ref.at[me],
            send_sem=ssem.at[k], recv_sem=rsem.at[k], device_id=peer, device_id_type=pl.DeviceIdType.MESH))
    return loc, cps


def _two_level_copies(x_ref, o_ref, ssem, rsem, lsem):
    mx, my, mc = lax.axis_index("x"), lax.axis_index("y"), lax.axis_index("c")
    me = mx * 4 + my * 2 + mc
    sibling = (mx, my, 1 - mc)
    chips = [(1 - mx, my), (mx, 1 - my), (1 - mx, 1 - my)]

    def copy(k, slot, to, src=None):
        return pltpu.make_async_remote_copy(
            src_ref=o_ref.at[slot] if src is None else src, dst_ref=o_ref.at[slot],
            send_sem=ssem.at[k], recv_sem=rsem.at[k], device_id=to, device_id_type=pl.DeviceIdType.MESH)

    loc = pltpu.make_async_copy(x_ref, o_ref.at[me], lsem)
    first = [copy(0, me, sibling, src=x_ref)]
    first += [copy(1 + j, me, (cx, cy, mc), src=x_ref) for j, (cx, cy) in enumerate(chips)]
    passed = [copy(4 + j, cx * 4 + cy * 2 + mc, sibling) for j, (cx, cy) in enumerate(chips)]
    return loc, first, passed


def _exchange_start(kind, x_ref, o_ref, ssem, rsem, lsem):
    if kind == "ag2":
        loc, first, _ = _two_level_copies(x_ref, o_ref, ssem, rsem, lsem)
        cps = first
    else:
        loc, cps = _exchange_copies(kind, x_ref, o_ref, ssem, rsem, lsem)
    loc.start()
    for cp in cps:
        cp.start()


def _exchange_pass_on(kind, x_ref, o_ref, ssem, rsem, lsem):
    if kind == "ag2":
        _, first, passed = _two_level_copies(x_ref, o_ref, ssem, rsem, lsem)
        for j in range(3):
            first[1 + j].wait_recv()
            passed[j].start()


def _exchange_wait(kind, x_ref, o_ref, ssem, rsem, lsem):
    if kind == "ag2":
        loc, first, passed = _two_level_copies(x_ref, o_ref, ssem, rsem, lsem)
        first[0].wait_recv()
        for cp in passed:
            cp.wait_recv()
        cps = first + passed
    else:
        loc, cps = _exchange_copies(kind, x_ref, o_ref, ssem, rsem, lsem)
        for cp in cps:
            cp.wait_recv()
    for cp in cps:
        cp.wait_send()
    loc.wait()


_EXCHANGE_SEMS = [pltpu.SemaphoreType.DMA((N_DEV - 1,)), pltpu.SemaphoreType.DMA((N_DEV - 1,)),
                  pltpu.SemaphoreType.DMA(())]


def _exchange_shape(kind, x):
    return jax.ShapeDtypeStruct(x.shape if kind == "a2a" else (N_DEV,) + x.shape, x.dtype)


def _exchange(kind, x, name):
    def body(x_ref, o_ref, ssem, rsem, lsem):
        _exchange_start(kind, x_ref, o_ref, ssem, rsem, lsem)
        _exchange_pass_on(kind, x_ref, o_ref, ssem, rsem, lsem)
        _exchange_wait(kind, x_ref, o_ref, ssem, rsem, lsem)

    return pl.pallas_call(
        body, name=name, out_shape=_exchange_shape(kind, x),
        in_specs=[pl.BlockSpec(memory_space=pl.ANY)], out_specs=pl.BlockSpec(memory_space=pl.ANY),
        scratch_shapes=list(_EXCHANGE_SEMS),
    )(x)


def _allgather(x, name):
    return _exchange("ag", x, name)


def _call(body, name, grid, in_specs, out_specs, out_shape, scratch_shapes, sem, args, comm=(), aliases=None):
    in_specs, out_specs, out_shape = list(in_specs), list(out_specs), list(out_shape)
    scratch_shapes = list(scratch_shapes)
    aliases = aliases or {}
    if not comm:
        outs = pl.pallas_call(body, name=name, grid=grid, in_specs=in_specs, out_specs=out_specs, out_shape=out_shape,
                              scratch_shapes=scratch_shapes, input_output_aliases=aliases,
                              compiler_params=_cparams(sem))(*args)
        return list(outs), []
    n_in, n_out, n_scr, n_c = len(in_specs), len(out_specs), len(scratch_shapes), len(comm)
    hbm = pl.BlockSpec(memory_space=pl.ANY)

    def wrapped(*refs):
        ins, cins = refs[:n_in], refs[n_in:n_in + n_c]
        outs = refs[n_in + n_c:n_in + n_c + n_out]
        couts = refs[n_in + n_c + n_out:n_in + 2 * n_c + n_out]
        scr = refs[n_in + 2 * n_c + n_out:n_in + 2 * n_c + n_out + n_scr]
        sems = refs[n_in + 2 * n_c + n_out + n_scr:]
        step, total = pl.program_id(0), grid[0]
        for ax in range(1, len(grid)):
            step = step * grid[ax] + pl.program_id(ax)
            total *= grid[ax]

        @pl.when(step == 0)
        def _():
            for c, (kind, _) in enumerate(comm):
                _exchange_start(kind, cins[c], couts[c], *sems[3 * c:3 * c + 3])

        body(*ins, *outs, *scr)

        @pl.when(step == (3 * total) // 4)
        def _():
            for c, (kind, _) in enumerate(comm):
                _exchange_pass_on(kind, cins[c], couts[c], *sems[3 * c:3 * c + 3])

        @pl.when(step == total - 1)
        def _():
            for c, (kind, _) in enumerate(comm):
                _exchange_wait(kind, cins[c], couts[c], *sems[3 * c:3 * c + 3])

    res = pl.pallas_call(
        wrapped, name=name, grid=grid,
        in_specs=in_specs + [hbm] * n_c, out_specs=out_specs + [hbm] * n_c,
        out_shape=out_shape + [_exchange_shape(kind, arr) for kind, arr in comm],
        scratch_shapes=scratch_shapes + list(_EXCHANGE_SEMS) * n_c, input_output_aliases=aliases,
        compiler_params=_cparams(("arbitrary",) * len(grid)),
    )(*args, *[arr for _, arr in comm])
    return list(res[:n_out]), list(res[n_out:])


def _matmul(name, pairs, n_acc, M, N, K, mode, tiles, extras, out_dtypes, epilogue, j_outer=False, comm=()):
    tm, tn, tk = tiles
    gm, gn, nk = M // tm, N // tn, K // tk
    assert gm * tm == M and gn * tn == N and nk * tk == K, (name, M, N, K, tiles)
    if j_outer:
        grid = (gn, gm, nk)
        ij = lambda g0, g1: (g1, g0)
    else:
        grid = (gm, gn, nk)
        ij = lambda g0, g1: (g0, g1)

    if mode in ("nn", "nt"):
        a_spec = pl.BlockSpec((tm, tk), lambda g0, g1, k: (ij(g0, g1)[0], k))
    else:
        a_spec = pl.BlockSpec((tk, tm), lambda g0, g1, k: (k, ij(g0, g1)[0]))
    if mode == "nt":
        b_spec = pl.BlockSpec((tn, tk), lambda g0, g1, k: (ij(g0, g1)[1], k))
    else:
        b_spec = pl.BlockSpec((tk, tn), lambda g0, g1, k: (k, ij(g0, g1)[1]))
    dims = {"nn": NN, "nt": NT, "tn": TN}[mode]
    mn_spec = pl.BlockSpec((tm, tn), lambda g0, g1, k: ij(g0, g1))
    n_spec = pl.BlockSpec((1, tn), lambda g0, g1, k: (0, ij(g0, g1)[1]))

    in_specs, args = [], []
    for a, b, _ in pairs:
        in_specs += [a_spec, b_spec]
        args += [a, b]
    for arr, kind in extras:
        in_specs.append(mn_spec if kind == "mn" else n_spec)
        args.append(arr)
    n_p, n_e, n_o = len(pairs), len(extras), len(out_dtypes)

    def body(*refs):
        ab = refs[:2 * n_p]
        ex = refs[2 * n_p:2 * n_p + n_e]
        outs = refs[2 * n_p + n_e:2 * n_p + n_e + n_o]
        accs = refs[2 * n_p + n_e + n_o:]
        k = pl.program_id(2)

        def partial_sums():
            sums = [None] * n_acc
            for p, (_, _, ai) in enumerate(pairs):
                d = _dot(ab[2 * p][...], ab[2 * p + 1][...], dims)
                sums[ai] = d if sums[ai] is None else sums[ai] + d
            return sums

        def finish(acc_vals):
            res = epilogue(acc_vals, [e[...] for e in ex])
            for o, r in zip(outs, res):
                o[...] = r.astype(o.dtype)

        def accumulate(first):
            w = _pick(tm if mode == "tn" else tn, (512, 384, 256))
            for c in range((tm if mode == "tn" else tn) // w):
                sl = slice(c * w, (c + 1) * w)
                sums = [None] * n_acc
                for p, (_, _, ai) in enumerate(pairs):
                    a_ref, b_ref = ab[2 * p], ab[2 * p + 1]
                    if mode == "tn":
                        d = _dot(a_ref[:, sl], b_ref[...], dims)
                    elif mode == "nn":
                        d = _dot(a_ref[...], b_ref[:, sl], dims)
                    else:
                        d = _dot(a_ref[...], b_ref[sl, :], dims)
                    sums[ai] = d if sums[ai] is None else sums[ai] + d
                idx = (sl, slice(None)) if mode == "tn" else (slice(None), sl)
                for ai, s in enumerate(sums):
                    if first:
                        accs[ai][idx] = s
                    else:
                        accs[ai][idx] += s

        if nk == 1:
            finish(partial_sums())
        else:
            pl.when(k == 0)(functools.partial(accumulate, True))
            pl.when(k > 0)(functools.partial(accumulate, False))

            @pl.when(k == nk - 1)
            def _():
                finish([a[...] for a in accs])

    outs, couts = _call(
        body, name, grid, in_specs, [mn_spec] * n_o,
        [jax.ShapeDtypeStruct((M, N), dt) for dt in out_dtypes],
        [pltpu.VMEM((tm, tn), F32) for _ in range(n_acc if nk > 1 else 0)],
        ("parallel", "parallel", "arbitrary"), args, comm)
    return (outs, couts) if comm else outs


def _rope_tables(L):
    t = jnp.arange(L, dtype=jnp.int32)
    f = jnp.arange(32, dtype=jnp.int32).astype(F32)
    ang = t.astype(F32)[:, None] * (ROPE_BASE ** (-f / 32.0))[None, :]
    cos, sin = jnp.cos(ang), jnp.sin(ang)
    ret = jnp.stack([jnp.tile(cos, (1, 4)), jnp.tile(jnp.concatenate([-sin, sin], axis=1), (1, 2))])
    f2 = jnp.arange(16, dtype=jnp.int32).astype(F32)
    inv2 = (ROPE_BASE ** (-f2 / 16.0))[None, :]
    ang_r = (t // GRID_W).astype(F32)[:, None] * inv2
    ang_c = (t % GRID_W).astype(F32)[:, None] * inv2
    cr, sr, cc, sc = jnp.cos(ang_r), jnp.sin(ang_r), jnp.cos(ang_c), jnp.sin(ang_c)
    att = jnp.stack([jnp.tile(jnp.concatenate([cr, cr, cc, cc], axis=1), (1, 2)),
                     jnp.tile(jnp.concatenate([-sr, sr, -sc, sc], axis=1), (1, 2))])
    return ret.astype(F32), att.astype(F32)


def _swap(x, sh):
    lane = lax.broadcasted_iota(jnp.int32, x.shape, 1)
    ra = pltpu.roll(x, LANES - sh, 1)
    rb = pltpu.roll(x, sh, 1)
    la = pltpu.roll(lane, LANES - sh, 1)
    partner = jnp.where((lane % (2 * sh)) < sh, lane + sh, lane - sh)
    return jnp.where(la == partner, ra, rb)


def _rope(x, cos, sin, sh):
    return x * cos + _swap(x, sh) * sin


def _rope_t(d, cos, sin, sh):
    return d * cos + _swap(d * sin, sh)


def _half_mask(shape, a):
    lane = lax.broadcasted_iota(jnp.int32, shape, 1)
    return (lane < 64) if a == 0 else (lane >= 64)


def _mod_fwd(s_in, w_l, b_l):
    D, C6 = w_l.shape
    tk = _pick(D, (512, 256, 128))
    nk = D // tk

    def body(s_ref, w_ref, b_ref, o_ref):
        k = pl.program_id(0)
        s = s_ref[...]
        s = s * _sigmoid(s)
        d = jnp.dot(s, w_ref[...], preferred_element_type=F32, precision=lax.Precision.HIGHEST)

        @pl.when(k == 0)
        def _():
            o_ref[...] = d + b_ref[...]

        @pl.when(k > 0)
        def _():
            o_ref[...] += d

    return pl.pallas_call(
        body, name="mod_fwd", grid=(nk,),
        in_specs=[pl.BlockSpec((16, tk), lambda k: (0, k)), pl.BlockSpec((tk, C6), lambda k: (k, 0)),
                  pl.BlockSpec((1, C6), lambda k: (0, 0))],
        out_specs=pl.BlockSpec((16, C6), lambda k: (0, 0)),
        out_shape=jax.ShapeDtypeStruct((16, C6), F32),
        compiler_params=_cparams(("arbitrary",)),
    )(s_in, w_l, b_l)


def _mod_bwd(s_in, dm, w_l):
    D, C6 = w_l.shape
    tk = _pick(D, (512, 256, 128))
    nk = D // tk

    def body(s_ref, dm_ref, w_ref, gw_ref, gc_ref):
        s = s_ref[...]
        sg = _sigmoid(s)
        act = s * sg
        dmv = dm_ref[...]
        gw_ref[...] = lax.dot_general(act, dmv, TN, preferred_element_type=F32, precision=lax.Precision.HIGHEST)
        ds = lax.dot_general(dmv, w_ref[...], NT, preferred_element_type=F32, precision=lax.Precision.HIGHEST)
        dsil = (sg * (1.0 + s * (1.0 - sg)))[8:9, :]
        gc_ref[...] = jnp.zeros((8, tk), F32) + jnp.sum(ds[8:16, :], axis=0, keepdims=True) * dsil

    return pl.pallas_call(
        body, name="mod_bwd", grid=(nk,),
        in_specs=[pl.BlockSpec((16, tk), lambda k: (0, k)), pl.BlockSpec((16, C6), lambda k: (0, 0)),
                  pl.BlockSpec((tk, C6), lambda k: (k, 0))],
        out_specs=[pl.BlockSpec((tk, C6), lambda k: (k, 0)), pl.BlockSpec((8, tk), lambda k: (0, k))],
        out_shape=[jax.ShapeDtypeStruct((D, C6), F32), jax.ShapeDtypeStruct((8, D), F32)],
        compiler_params=_cparams(("parallel",)),
    )(s_in, dm, w_l)


def _norm_rows(x):
    r = lax.rsqrt(jnp.mean(x * x, axis=-1, keepdims=True) + EPS)
    return x * r, r


def _modulate_fwd(name, x, ctx, g, mod, modc, comm=()):
    L, D = x.shape
    tr = ctx.shape[0]
    nx = L // tr

    def body(x_ref, c_ref, g_ref, m_ref, mc_ref, o_ref):
        i = pl.program_id(0)

        def run(src, m):
            n, _ = _norm_rows(src[...])
            o_ref[...] = (n * g_ref[...] * (1.0 + m[1:2, :]) + m[0:1, :]).astype(o_ref.dtype)

        @pl.when(i < nx)
        def _():
            run(x_ref, m_ref)

        @pl.when(i >= nx)
        def _():
            run(c_ref, mc_ref)

    row = pl.BlockSpec((tr, D), lambda i: (jnp.minimum(i, nx - 1), 0))
    vec = pl.BlockSpec((1, D), lambda i: (0, 0))
    mv = pl.BlockSpec((8, D), lambda i: (0, 0))
    return _call(
        body, name, (nx + 1,), [row, pl.BlockSpec((tr, D), lambda i: (0, 0)), vec, mv, mv],
        [pl.BlockSpec((tr, D), lambda i: (i, 0))], [jax.ShapeDtypeStruct((L + tr, D), BF16)], [],
        ("parallel",), (x, ctx, g, mod, modc), comm)


def _residual_modulate_fwd(name, x, fbr, gate, g, mod):
    L, D = x.shape
    tr = _pick(L, (512, 256, 128))

    def body(x_ref, f_ref, gt_ref, g_ref, m_ref, x1_ref, o_ref):
        x1 = x_ref[...] + gt_ref[...] * f_ref[...].astype(F32)
        x1_ref[...] = x1
        n, _ = _norm_rows(x1)
        o_ref[...] = (n * g_ref[...] * (1.0 + m_ref[1:2, :]) + m_ref[0:1, :]).astype(o_ref.dtype)

    row = pl.BlockSpec((tr, D), lambda i: (i, 0))
    vec = pl.BlockSpec((1, D), lambda i: (0, 0))
    return pl.pallas_call(
        body, name=name, grid=(L // tr,),
        in_specs=[row, row, vec, vec, pl.BlockSpec((8, D), lambda i: (0, 0))],
        out_specs=[row, row],
        out_shape=[jax.ShapeDtypeStruct((L, D), F32), jax.ShapeDtypeStruct((L, D), BF16)],
        compiler_params=_cparams(("parallel",)),
    )(x, fbr, gate, g, mod)


def _modulate_bwd(name, x, ctx, dh, g, mod, modc, dres, fbr, gate):
    L, D = x.shape
    tr = ctx.shape[0] if ctx is not None else _pick(L, (256, 128))
    nx = L // tr
    nt = nx + (1 if ctx is not None else 0)
    has_f = fbr is not None

    def body(*refs):
        refs = list(refs)
        x_ref = refs.pop(0)
        c_ref = refs.pop(0) if ctx is not None else None
        dh_ref, g_ref, m_ref = refs.pop(0), refs.pop(0), refs.pop(0)
        mc_ref = refs.pop(0) if ctx is not None else None
        dr_ref = refs.pop(0)
        f_ref = refs.pop(0) if has_f else None
        gt_ref = refs.pop(0) if has_f else None
        dx_ref = refs.pop(0)
        df_ref = refs.pop(0) if has_f else None
        acc_ref = refs.pop(0)
        i = pl.program_id(0)

        @pl.when(i == 0)
        def _():
            acc_ref[...] = jnp.zeros_like(acc_ref)

        def sums(src, m, base, grow):
            n, r = _norm_rows(src[...])
            d = dh_ref[...].astype(F32)
            gg = g_ref[...]
            sc1 = 1.0 + m[1:2, :]
            acc_ref[base:base + 1, :] += jnp.sum(d, axis=0, keepdims=True)
            dn = d * n
            acc_ref[base + 1:base + 2, :] += jnp.sum(dn, axis=0, keepdims=True) * gg
            acc_ref[grow:grow + 1, :] += jnp.sum(dn, axis=0, keepdims=True) * sc1
            dnv = d * (gg * sc1)
            return r * (dnv - n * jnp.mean(dnv * n, axis=-1, keepdims=True))

        def x_rows():
            dx = sums(x_ref, m_ref, 0, 2) + dr_ref[...]
            dx_ref[...] = dx
            if has_f:
                acc_ref[6:7, :] += jnp.sum(dx * f_ref[...].astype(F32), axis=0, keepdims=True)
                df_ref[...] = (dx * gt_ref[...]).astype(df_ref.dtype)

        if ctx is None:
            x_rows()
        else:
            pl.when(i < nx)(x_rows)

            @pl.when(i >= nx)
            def _():
                sums(c_ref, mc_ref, 3, 2)

    row = pl.BlockSpec((tr, D), lambda i: (jnp.minimum(i, nx - 1), 0))
    vec = pl.BlockSpec((1, D), lambda i: (0, 0))
    mv = pl.BlockSpec((8, D), lambda i: (0, 0))
    in_specs, args = [row], [x]
    if ctx is not None:
        in_specs.append(pl.BlockSpec((tr, D), lambda i: (0, 0)))
        args.append(ctx)
    in_specs += [pl.BlockSpec((tr, D), lambda i: (i, 0)), vec, mv]
    args += [dh, g, mod]
    if ctx is not None:
        in_specs.append(mv)
        args.append(modc)
    in_specs.append(row)
    args.append(dres)
    out_specs = [row]
    out_shape = [jax.ShapeDtypeStruct((L, D), F32)]
    if has_f:
        in_specs += [row, vec]
        args += [fbr, gate]
        out_specs.append(row)
        out_shape.append(jax.ShapeDtypeStruct((L, D), BF16))
    out_specs.append(pl.BlockSpec((16, D), lambda i: (0, 0)))
    out_shape.append(jax.ShapeDtypeStruct((16, D), F32))
    return pl.pallas_call(
        body, name=name, grid=(nt,), in_specs=in_specs, out_specs=out_specs, out_shape=out_shape,
        compiler_params=_cparams(("arbitrary",)),
    )(*args)


def _loss_head(x1, tgt, nf, fbr, gate):
    L, D = x1.shape
    tr = _pick(L, (256, 128))

    def body(x_ref, t_ref, w_ref, f_ref, gt_ref, dx_ref, df_ref, acc_ref):
        i = pl.program_id(0)

        @pl.when(i == 0)
        def _():
            acc_ref[...] = jnp.zeros_like(acc_ref)

        n, r = _norm_rows(x_ref[...] + gt_ref[...] * f_ref[...].astype(F32))
        w = w_ref[...]
        e = n * w - t_ref[...]
        acc_ref[0:1, :] += jnp.sum(e * e, axis=0, keepdims=True) * (0.5 / D)
        dout = e * (1.0 / D)
        acc_ref[1:2, :] += jnp.sum(dout * n, axis=0, keepdims=True)
        dn = dout * w
        dx = r * (dn - n * jnp.mean(dn * n, axis=-1, keepdims=True))
        dx_ref[...] = dx
        acc_ref[2:3, :] += jnp.sum(dx * f_ref[...].astype(F32), axis=0, keepdims=True)
        df_ref[...] = (dx * gt_ref[...]).astype(df_ref.dtype)

        @pl.when(i == pl.num_programs(0) - 1)
        def _():
            acc_ref[3:4, :] = jnp.zeros((1, D), F32) + jnp.sum(acc_ref[0:1, :])

    row = pl.BlockSpec((tr, D), lambda i: (i, 0))
    vec = pl.BlockSpec((1, D), lambda i: (0, 0))
    return pl.pallas_call(
        body, name="loss_head", grid=(L // tr,),
        in_specs=[row, row, vec, row, vec],
        out_specs=[row, row, pl.BlockSpec((8, D), lambda i: (0, 0))],
        out_shape=[jax.ShapeDtypeStruct((L, D), F32), jax.ShapeDtypeStruct((L, D), BF16),
                   jax.ShapeDtypeStruct((8, D), F32)],
        compiler_params=_cparams(("arbitrary",)),
    )(x1, tgt, nf, fbr, gate)


N_TAB = 7


def _ret_tables(rdb, Lc):
    def body(rd_ref, t_ref, c_ref):
        d = pl.program_id(0) // RET_HEADS
        fwd = d == 0
        lg = -jnp.exp(rd_ref[0])
        i = lax.broadcasted_iota(jnp.int32, (CHUNK, CHUNK), 0).astype(F32)
        j = lax.broadcasted_iota(jnp.int32, (CHUNK, CHUNK), 1).astype(F32)
        rel = jnp.where(fwd, i - j, j - i)
        mask = (rel > 0.0) | ((rel == 0.0) & fwd)
        dm = jnp.where(mask, jnp.exp(lg * jnp.maximum(rel, 0.0)), 0.0)
        t_ref[0, 0] = dm
        t_ref[0, 1] = rel * dm
        qc = jnp.where(fwd, i + 1.0, CHUNK - i)
        qw = jnp.exp(lg * qc)
        t_ref[0, 2] = qw
        t_ref[0, 3] = qw * qc
        kc = jnp.where(fwd, CHUNK - 1.0 - i, i)
        kw = jnp.exp(lg * kc)
        t_ref[0, 4] = kw
        t_ref[0, 5] = kw * kc
        t_ref[0, 6] = jnp.exp(lg * float(CHUNK)) + jnp.zeros((CHUNK, CHUNK), F32)
        m = lax.broadcasted_iota(jnp.int32, (Lc, LANES), 0).astype(F32)
        cc = jnp.where(fwd, Lc - 1.0 - m, m)
        cw = jnp.exp(lg * cc)
        c_ref[0, 0] = cw
        c_ref[0, 1] = cw * cc

    return pl.pallas_call(
        body, name="ret_tables", grid=(2 * RET_HEADS,),
        in_specs=[pl.BlockSpec((1, 1, LANES), lambda r: (r, 0, 0))],
        out_specs=[pl.BlockSpec((1, N_TAB, CHUNK, CHUNK), lambda r: (r, 0, 0, 0)),
                   pl.BlockSpec((1, 2, Lc, LANES), lambda r: (r, 0, 0, 0))],
        out_shape=[jax.ShapeDtypeStruct((2 * RET_HEADS, N_TAB, CHUNK, CHUNK), F32),
                   jax.ShapeDtypeStruct((2 * RET_HEADS, 2, Lc, LANES), F32)],
        compiler_params=_cparams(("parallel",)),
    )(rdb)


def _ret_ctx_state(P, ctab, L, Lc):
    cb = L // Lc

    def body(k_ref, v_ref, c_ref, s_ref):
        for p in range(RET_HEADS // 2):
            kp = k_ref[:, p * LANES:(p + 1) * LANES].astype(F32) * K_SCALE
            for a in range(2):
                h = 2 * p + a
                kh = jnp.where(_half_mask(kp.shape, a), kp, 0.0)
                vh = v_ref[:, h * RET_DV:(h + 1) * RET_DV]
                for d in range(2):
                    kw = (kh * c_ref[d * RET_HEADS + h, 0]).astype(BF16)
                    s_ref[d * RET_HEADS + h] = _dot(kw, vh, TN)

    return pl.pallas_call(
        body, name="ret_ctx_state", grid=(1,),
        in_specs=[pl.BlockSpec((Lc, 512), lambda i: (cb, C_RK // 512)),
                  pl.BlockSpec((Lc, 1024), lambda i: (cb, C_RV // 1024)),
                  pl.BlockSpec((2 * RET_HEADS, 2, Lc, LANES), lambda i: (0, 0, 0, 0))],
        out_specs=pl.BlockSpec((2 * RET_HEADS, LANES, RET_DV), lambda i: (0, 0, 0)),
        out_shape=jax.ShapeDtypeStruct((2 * RET_HEADS, LANES, RET_DV), F32),
        compiler_params=_cparams(("arbitrary",)),
    )(P, P, ctab)


def _ret_fwd(P, rope, tabs, s0, L, comm=()):
    n = L // CHUNK

    def body(qf, kf, vf, rf, qb, kb, vb, rb, t_ref, s0_ref, of_ref, ob_ref, stf_ref, stb_ref, st):
        s = pl.program_id(0)

        @pl.when(s == 0)
        def _():
            st[...] = s0_ref[...]

        units = []
        for d, (q_ref, k_ref, v_ref, r_ref, o_ref, so_ref) in enumerate(
                ((qf, kf, vf, rf, of_ref, stf_ref), (qb, kb, vb, rb, ob_ref, stb_ref))):
            cos, sin = r_ref[0], r_ref[1]
            for p in range(RET_HEADS // 2):
                qp = _rope(q_ref[:, p * LANES:(p + 1) * LANES].astype(F32), cos, sin, 32)
                kp = _rope(k_ref[:, p * LANES:(p + 1) * LANES].astype(F32), cos, sin, 32) * K_SCALE
                for a in range(2):
                    h = 2 * p + a
                    hm = _half_mask(qp.shape, a)
                    units.append(dict(r=d * RET_HEADS + h, h=h, a=a, o_ref=o_ref, so_ref=so_ref, v_ref=v_ref,
                                      qh=jnp.where(hm, qp, 0.0), kh=jnp.where(hm, kp, 0.0)))
        for u in units:
            u["sc"] = _dot(u["qh"].astype(BF16), u["kh"].astype(BF16), NT)
        for u in units:
            r, h = u["r"], u["h"]
            sp = st[r]
            u["so_ref"][0, h] = sp[u["a"] * RET_DK:(u["a"] + 1) * RET_DK, :]
            vh = u["v_ref"][:, h * RET_DV:(h + 1) * RET_DV]
            o = _dot((u["sc"] * t_ref[r, 0]).astype(BF16), vh, NN)
            o += _dot((u["qh"] * t_ref[r, 2]).astype(BF16), sp.astype(BF16), NN)
            u["o_ref"][:, h * RET_DV:(h + 1) * RET_DV] = o
        for u in units:
            r, h = u["r"], u["h"]
            vh = u["v_ref"][:, h * RET_DV:(h + 1) * RET_DV]
            st[r] = t_ref[r, 6] * st[r] + _dot((u["kh"] * t_ref[r, 4]).astype(BF16), vh, TN)

    fw = lambda s: s
    bw = lambda s: n - 1 - s

    def specs(cm):
        return [pl.BlockSpec((CHUNK, 512), lambda s: (cm(s), C_RQ // 512)),
                pl.BlockSpec((CHUNK, 512), lambda s: (cm(s), C_RK // 512)),
                pl.BlockSpec((CHUNK, 1024), lambda s: (cm(s), C_RV // 1024)),
                pl.BlockSpec((2, CHUNK, LANES), lambda s: (0, cm(s), 0))]

    full = lambda shp: pl.BlockSpec(shp, lambda s: (0,) * len(shp))
    return _call(
        body, "ret_fwd", (n,),
        specs(fw) + specs(bw) + [full((2 * RET_HEADS, N_TAB, CHUNK, CHUNK)), full((2 * RET_HEADS, LANES, RET_DV))],
        [pl.BlockSpec((CHUNK, 1024), lambda s: (fw(s), 0)),
         pl.BlockSpec((CHUNK, 1024), lambda s: (bw(s), 0)),
         pl.BlockSpec((1, RET_HEADS, RET_DK, RET_DV), lambda s: (fw(s), 0, 0, 0)),
         pl.BlockSpec((1, RET_HEADS, RET_DK, RET_DV), lambda s: (bw(s), 0, 0, 0))],
        [jax.ShapeDtypeStruct((L, 1024), F32), jax.ShapeDtypeStruct((L, 1024), F32),
         jax.ShapeDtypeStruct((n, RET_HEADS, RET_DK, RET_DV), F32),
         jax.ShapeDtypeStruct((n, RET_HEADS, RET_DK, RET_DV), F32)],
        [pltpu.VMEM((2 * RET_HEADS, LANES, RET_DV), F32)],
        ("arbitrary",), (P, P, P, rope, P, P, P, rope, tabs, s0), comm)


def _ret_finish_fwd(of, ob, P, L):
    tr = _pick(L, (512, 256, 128))

    def body(f_ref, b_ref, g_ref, y_ref):
        for h in range(RET_HEADS):
            sl = slice(h * RET_DV, (h + 1) * RET_DV)
            n, _ = _norm_rows(f_ref[:, sl] + b_ref[:, sl])
            g = g_ref[:, sl].astype(F32)
            y_ref[:, sl] = (n * (g * _sigmoid(g))).astype(y_ref.dtype)

    row = pl.BlockSpec((tr, 1024), lambda i: (i, 0))
    return pl.pallas_call(
        body, name="ret_finish_fwd", grid=(L // tr,),
        in_specs=[row, row, pl.BlockSpec((tr, 1024), lambda i: (i, C_RG // 1024))],
        out_specs=row, out_shape=jax.ShapeDtypeStruct((L, 2048), BF16),
        compiler_params=_cparams(("parallel",)),
    )(of, ob, P)


def _ret_finish_bwd(of, ob, P, dY, L):
    tr = _pick(L, (512, 256, 128))

    def body(f_ref, b_ref, g_ref, dy_ref, do_ref, dg_ref):
        for h in range(RET_HEADS):
            sl = slice(h * RET_DV, (h + 1) * RET_DV)
            n, r = _norm_rows(f_ref[:, sl] + b_ref[:, sl])
            g = g_ref[:, sl].astype(F32)
            sg = _sigmoid(g)
            dy = dy_ref[:, sl].astype(F32)
            dg_ref[:, sl] = (dy * n * (sg * (1.0 + g * (1.0 - sg)))).astype(dg_ref.dtype)
            dn = dy * (g * sg)
            do_ref[:, sl] = (r * (dn - n * jnp.mean(dn * n, axis=-1, keepdims=True))).astype(do_ref.dtype)

    row = pl.BlockSpec((tr, 1024), lambda i: (i, 0))
    return pl.pallas_call(
        body, name="ret_finish_bwd", grid=(L // tr,),
        in_specs=[row, row, pl.BlockSpec((tr, 1024), lambda i: (i, C_RG // 1024)), row],
        out_specs=[row, row],
        out_shape=[jax.ShapeDtypeStruct((L, 1024), BF16), jax.ShapeDtypeStruct((L, 1024), BF16)],
        compiler_params=_cparams(("parallel",)),
    )(of, ob, P, dY)


def _ret_bwd(P, rope, tabs, stf, stb, dO, L, comm=()):
    n = L // CHUNK

    def body(qf, kf, vf, rf, gf, sf, qb, kb, vb, rb, gb, sb, t_ref,
             dqf, dkf, dvf, dqb, dkb, dvb, ds0_ref, dlg_ref, ds):
        s = pl.program_id(0)

        @pl.when(s == 0)
        def _():
            ds[...] = jnp.zeros_like(ds)
            dlg_ref[...] = jnp.zeros_like(dlg_ref)

        units, pairs = [], []
        for d, (q_ref, k_ref, v_ref, r_ref, g_ref, s_ref, dq_ref, dk_ref, dv_ref) in enumerate(
                ((qf, kf, vf, rf, gf, sf, dqf, dkf, dvf), (qb, kb, vb, rb, gb, sb, dqb, dkb, dvb))):
            cos, sin = r_ref[0], r_ref[1]
            for p in range(RET_HEADS // 2):
                qp = _rope(q_ref[:, p * LANES:(p + 1) * LANES].astype(F32), cos, sin, 32)
                kp = _rope(k_ref[:, p * LANES:(p + 1) * LANES].astype(F32), cos, sin, 32) * K_SCALE
                pair = dict(p=p, cos=cos, sin=sin, dq_ref=dq_ref, dk_ref=dk_ref, us=[])
                pairs.append(pair)
                for a in range(2):
                    h = 2 * p + a
                    r = d * RET_HEADS + h
                    hm = _half_mask(qp.shape, a)
                    zero = jnp.zeros((RET_DK, RET_DV), F32)
                    sp = s_ref[0, h]
                    u = dict(r=r, h=h, dv_ref=dv_ref, qh=jnp.where(hm, qp, 0.0), kh=jnp.where(hm, kp, 0.0),
                             vh=v_ref[:, h * RET_DV:(h + 1) * RET_DV], gh=g_ref[:, h * RET_DV:(h + 1) * RET_DV],
                             sp=jnp.concatenate([sp, zero] if a == 0 else [zero, sp], axis=0),
                             dsn=ds[r])
                    u["qhb"], u["khb"] = u["qh"].astype(BF16), u["kh"].astype(BF16)
                    units.append(u)
                    pair["us"].append(u)
        for u in units:
            u["am"] = _dot(u["qhb"], u["khb"], NT)
            u["dar"] = _dot(u["gh"], u["vh"], NT)
            u["xq"] = _dot(u["gh"], u["sp"].astype(BF16), NT)
            u["yk"] = _dot(u["vh"], u["dsn"].astype(BF16), NT)
        for u in units:
            r = u["r"]
            dm = t_ref[r, 0]
            u["da"] = (u["dar"] * dm).astype(BF16)
            u["amd"] = (u["am"] * dm).astype(BF16)
            part = (jnp.sum(u["am"] * u["dar"] * t_ref[r, 1]) + jnp.sum(u["qh"] * t_ref[r, 3] * u["xq"])
                    + jnp.sum(u["kh"] * t_ref[r, 5] * u["yk"])
                    + float(CHUNK) * jnp.sum(t_ref[r, 6] * u["dsn"] * u["sp"]))
            dlg_ref[r:r + 1, :] += jnp.zeros((1, LANES), F32) + part
        for u in units:
            r, h = u["r"], u["h"]
            u["dq"] = _dot(u["da"], u["khb"], NN) + u["xq"] * t_ref[r, 2]
            u["dk"] = _dot(u["da"], u["qhb"], TN) + u["yk"] * t_ref[r, 4]
            u["dv_ref"][:, h * RET_DV:(h + 1) * RET_DV] = (
                _dot(u["amd"], u["gh"], TN) + _dot((u["kh"] * t_ref[r, 4]).astype(BF16), u["dsn"].astype(BF16), NN)
            ).astype(u["dv_ref"].dtype)
            ds[r] = t_ref[r, 6] * u["dsn"] + _dot((u["qh"] * t_ref[r, 2]).astype(BF16), u["gh"], TN)
        for pair in pairs:
            sl = slice(pair["p"] * LANES, (pair["p"] + 1) * LANES)
            u0, u1 = pair["us"]
            pair["dq_ref"][:, sl] = _rope_t(u0["dq"] + u1["dq"], pair["cos"], pair["sin"], 32).astype(BF16)
            pair["dk_ref"][:, sl] = _rope_t((u0["dk"] + u1["dk"]) * K_SCALE, pair["cos"], pair["sin"], 32).astype(BF16)

        @pl.when(s == n - 1)
        def _():
            ds0_ref[...] = ds[...]

    fw = lambda s: n - 1 - s
    bw = lambda s: s

    def specs(cm):
        return [pl.BlockSpec((CHUNK, 512), lambda s: (cm(s), C_RQ // 512)),
                pl.BlockSpec((CHUNK, 512), lambda s: (cm(s), C_RK // 512)),
                pl.BlockSpec((CHUNK, 1024), lambda s: (cm(s), C_RV // 1024)),
                pl.BlockSpec((2, CHUNK, LANES), lambda s: (0, cm(s), 0)),
                pl.BlockSpec((CHUNK, 1024), lambda s: (cm(s), 0)),
                pl.BlockSpec((1, RET_HEADS, RET_DK, RET_DV), lambda s: (cm(s), 0, 0, 0))]

    def ospecs(cm):
        return [pl.BlockSpec((CHUNK, 512), lambda s: (cm(s), 0)), pl.BlockSpec((CHUNK, 512), lambda s: (cm(s), 0)),
                pl.BlockSpec((CHUNK, 1024), lambda s: (cm(s), 0))]

    oshape = [jax.ShapeDtypeStruct((L, 512), BF16), jax.ShapeDtypeStruct((L, 512), BF16),
              jax.ShapeDtypeStruct((L, 1024), BF16)]
    full = lambda shp: pl.BlockSpec(shp, lambda s: (0,) * len(shp))
    return _call(
        body, "ret_bwd", (n,),
        specs(fw) + specs(bw) + [full((2 * RET_HEADS, N_TAB, CHUNK, CHUNK))],
        ospecs(fw) + ospecs(bw) + [full((2 * RET_HEADS, LANES, RET_DV)), full((2 * RET_HEADS, LANES))],
        oshape + oshape + [jax.ShapeDtypeStruct((2 * RET_HEADS, LANES, RET_DV), F32),
                           jax.ShapeDtypeStruct((2 * RET_HEADS, LANES), F32)],
        [pltpu.VMEM((2 * RET_HEADS, LANES, RET_DV), F32)],
        ("arbitrary",), (P, P, P, rope, dO, stf, P, P, P, rope, dO, stb, tabs), comm)


def _ret_ctx_bwd(P, ctab, ds0, dlg, rdb, L, Lc):
    cb = L // Lc

    def body(k_ref, v_ref, c_ref, ds_ref, dlg_ref, rd_ref, dk_ref, dv_ref, drd_ref):
        for p in range(RET_HEADS // 2):
            kp = k_ref[:, p * LANES:(p + 1) * LANES].astype(F32) * K_SCALE
            dkp = jnp.zeros((Lc, LANES), F32)
            for a in range(2):
                h = 2 * p + a
                kh = jnp.where(_half_mask(kp.shape, a), kp, 0.0)
                vh = v_ref[:, h * RET_DV:(h + 1) * RET_DV]
                dvh = jnp.zeros((Lc, RET_DV), F32)
                for d in range(2):
                    r = d * RET_HEADS + h
                    dsb = ds_ref[r].astype(BF16)
                    cw, cwc = c_ref[r, 0], c_ref[r, 1]
                    y = _dot(vh, dsb, NT)
                    dkp += y * cw
                    dvh += _dot((kh * cw).astype(BF16), dsb, NN)
                    lg = -jnp.exp(rd_ref[r])
                    drd_ref[r:r + 1, :] = (dlg_ref[r:r + 1, :] + jnp.sum(kh * cwc * y)) * lg
                dv_ref[:, h * RET_DV:(h + 1) * RET_DV] = dvh
            dk_ref[:, p * LANES:(p + 1) * LANES] = dkp * K_SCALE

    full = lambda shp: pl.BlockSpec(shp, lambda i: (0,) * len(shp))
    return pl.pallas_call(
        body, name="ret_ctx_bwd", grid=(1,),
        in_specs=[pl.BlockSpec((Lc, 512), lambda i: (cb, C_RK // 512)),
                  pl.BlockSpec((Lc, 1024), lambda i: (cb, C_RV // 1024)),
                  full((2 * RET_HEADS, 2, Lc, LANES)), full((2 * RET_HEADS, LANES, RET_DV)),
                  full((2 * RET_HEADS, LANES)), full((2 * RET_HEADS, 1, LANES))],
        out_specs=[full((Lc, 512)), full((Lc, 1024)), full((2 * RET_HEADS, LANES))],
        out_shape=[jax.ShapeDtypeStruct((Lc, 512), F32), jax.ShapeDtypeStruct((Lc, 1024), F32),
                   jax.ShapeDtypeStruct((2 * RET_HEADS, LANES), F32)],
        compiler_params=_cparams(("arbitrary",)),
    )(P, P, ctab, ds0, dlg, rdb)


BLK = 128
N_LOC = 3 * BLK


def _att_inputs(P, rope, L, Lc):
    n = L // BLK
    cb = L // Lc
    prev = lambda i: jnp.maximum(i - 1, 0)
    nxt = lambda i: jnp.minimum(i + 1, n - 1)
    specs = [pl.BlockSpec((BLK, 1024), lambda i: (i, C_AQ // 1024))]
    args = [P]
    for col in (C_AK // 256, C_AV // 256):
        for rm in (prev, lambda i: i, nxt):
            specs.append(pl.BlockSpec((BLK, 256), functools.partial(lambda i, rm, col: (rm(i), col), rm=rm, col=col)))
            args.append(P)
        specs.append(pl.BlockSpec((Lc, 256), functools.partial(lambda i, col: (cb, col), col=col)))
        args.append(P)
    for rm in (prev, lambda i: i, nxt):
        specs.append(pl.BlockSpec((2, BLK, LANES), functools.partial(lambda i, rm: (0, rm(i), 0), rm=rm)))
        args.append(rope)
    return specs, args


def _att_prep(i, n, refs, Lc):
    q_ref, kp_ref, kc_ref, kn_ref, kx_ref, vp_ref, vc_ref, vn_ref, vx_ref, rp_ref, rc_ref, rn_ref = refs
    cos = jnp.concatenate([rp_ref[0], rc_ref[0], rn_ref[0]], axis=0)
    sin = jnp.concatenate([rp_ref[1], rc_ref[1], rn_ref[1]], axis=0)
    kd, vd = [], []
    for t in range(ATT_KV // 2):
        sl = slice(t * LANES, (t + 1) * LANES)
        kl = jnp.concatenate([kp_ref[:, sl], kc_ref[:, sl], kn_ref[:, sl]], axis=0).astype(F32)
        kl = _rope(kl, cos, sin, 16)
        ka = jnp.concatenate([kl, kx_ref[:, sl].astype(F32)], axis=0)
        va = jnp.concatenate([vp_ref[:, sl], vc_ref[:, sl], vn_ref[:, sl], vx_ref[:, sl]], axis=0).astype(F32)
        kr, vr = pltpu.roll(ka, 64, 1), pltpu.roll(va, 64, 1)
        for b in range(2):
            hm = _half_mask(ka.shape, b)
            kd.append(jnp.where(hm, ka, kr).astype(BF16))
            vd.append(jnp.where(hm, va, vr).astype(BF16))
    nk = N_LOC + Lc
    rr = lax.broadcasted_iota(jnp.int32, (BLK, nk), 0)
    ss = lax.broadcasted_iota(jnp.int32, (BLK, nk), 1)
    lo = jnp.where(i == 0, BLK, 0)
    hi = jnp.where(i == n - 1, 2 * BLK, N_LOC)
    valid = (ss >= N_LOC) | ((ss >= rr) & (ss <= rr + 2 * BLK) & (ss >= lo) & (ss < hi))
    bias = jnp.where(valid, 0.0, NEG)
    return kd, vd, jnp.concatenate([bias] * 4, axis=0), rc_ref[0], rc_ref[1]


LOG2E = 1.4426950408889634
LN2 = 0.6931471805599453
Q_SCALE = A_SCALE * LOG2E


def _stack4(ref, g, f=None):
    parts = []
    for jp in range(2):
        t = ref[:, (2 * g + jp) * LANES:(2 * g + jp + 1) * LANES].astype(F32)
        if f is not None:
            t = f(t)
        for a in range(2):
            parts.append(jnp.where(_half_mask(t.shape, a), t, 0.0))
    return jnp.concatenate(parts, axis=0)


def _unstack4(x4, jp):
    r0 = 2 * jp * BLK
    lo = x4[r0:r0 + BLK]
    hi = x4[r0 + BLK:r0 + 2 * BLK]
    return jnp.where(_half_mask(lo.shape, 0), lo, hi)


def _softmax_parts(s, bias4, sink_ref, g):
    sink_col = LOG2E * jnp.concatenate(
        [jnp.zeros((BLK, 1), F32) + sink_ref[4 * g + r:4 * g + r + 1, 0:1] for r in range(4)], axis=0)
    s = s + bias4
    m = jnp.maximum(jnp.max(s, axis=-1, keepdims=True), sink_col)
    e = jnp.exp2(s - m)
    es = jnp.exp2(sink_col - m)
    return e, es, jnp.sum(e, axis=-1, keepdims=True) + es


def _att_fwd(P, rope, sinkb, Y, L, Lc, comm=()):
    n = L // BLK
    specs, args = _att_inputs(P, rope, L, Lc)

    def body(*refs):
        sink_ref, o_ref = refs[12], refs[14]
        i = pl.program_id(0)
        kd, vd, bias4, cq, sq = _att_prep(i, n, refs[:12], Lc)
        for g in range(ATT_KV):
            q4 = _stack4(refs[0], g, lambda t: _rope(t, cq, sq, 16) * Q_SCALE).astype(BF16)
            e, _, l = _softmax_parts(_dot(q4, kd[g], NT), bias4, sink_ref, g)
            o4 = _dot(e.astype(BF16), vd[g], NN) * (1.0 / l)
            for jp in range(2):
                c0 = (2 * g + jp) * LANES
                o_ref[:, c0:c0 + LANES] = _unstack4(o4, jp).astype(o_ref.dtype)

    return _call(
        body, "att_fwd", (n,),
        specs + [pl.BlockSpec((ATT_HEADS, LANES), lambda i: (0, 0)), pl.BlockSpec(memory_space=pl.ANY)],
        [pl.BlockSpec((BLK, 1024), lambda i: (i, 1))], [jax.ShapeDtypeStruct((L, 2048), BF16)], [],
        ("parallel",), (*args, sinkb, Y), comm, aliases={13: 0})


def _att_bwd(P, rope, sinkb, Y, dY, L, Lc, comm=()):
    n = L // BLK
    specs, args = _att_inputs(P, rope, L, Lc)
    nk = N_LOC + Lc

    def body(*refs):
        sink_ref, y_ref, dy_ref = refs[12], refs[13], refs[14]
        dq_ref, dkl_ref, dvl_ref, dkx_ref, dvx_ref, dsk_ref = refs[15:21]
        i = pl.program_id(0)

        @pl.when(i == 0)
        def _():
            dkx_ref[...] = jnp.zeros_like(dkx_ref)
            dvx_ref[...] = jnp.zeros_like(dvx_ref)
            dsk_ref[...] = jnp.zeros_like(dsk_ref)

        kd, vd, bias4, cq, sq = _att_prep(i, n, refs[:12], Lc)
        for t in range(ATT_KV // 2):
            dk_halves, dv_halves = [], []
            for b in range(2):
                g = 2 * t + b
                q4 = _stack4(refs[0], g, lambda x: _rope(x, cq, sq, 16) * Q_SCALE).astype(BF16)
                do4 = _stack4(dy_ref, g)
                delta = jnp.sum(do4 * _stack4(y_ref, g), axis=-1, keepdims=True)
                do4b = do4.astype(BF16)
                e, es, l = _softmax_parts(_dot(q4, kd[g], NT), bias4, sink_ref, g)
                inv = 1.0 / l
                p = e * inv
                dsc = (p * (_dot(do4b, vd[g], NT) - delta)).astype(BF16)
                dsr = es * inv * delta
                for r in range(4):
                    h = 4 * g + r
                    dsk_ref[h:h + 1, :] += jnp.zeros((1, LANES), F32) - jnp.sum(dsr[r * BLK:(r + 1) * BLK])
                dq4 = _dot(dsc, kd[g], NN) * A_SCALE
                for jp in range(2):
                    c0 = (2 * g + jp) * LANES
                    dq_ref[:, c0:c0 + LANES] = _rope_t(_unstack4(dq4, jp), cq, sq, 16).astype(dq_ref.dtype)
                dkd = _dot(q4, dsc, TN) * LN2
                dvd = _dot(do4b, p.astype(BF16), TN)
                dk_halves.append(dkd[:ATT_DH] + dkd[ATT_DH:])
                dv_halves.append(dvd[:ATT_DH] + dvd[ATT_DH:])
            dk_t = jnp.concatenate(dk_halves, axis=0).T
            dv_t = jnp.concatenate(dv_halves, axis=0).T
            sl = slice(t * LANES, (t + 1) * LANES)
            dkl_ref[0, :, sl] = dk_t[:N_LOC]
            dvl_ref[0, :, sl] = dv_t[:N_LOC]
            dkx_ref[:, sl] += dk_t[N_LOC:]
            dvx_ref[:, sl] += dv_t[N_LOC:]

    row = pl.BlockSpec((BLK, 1024), lambda i: (i, 0))
    loc = pl.BlockSpec((1, N_LOC, 256), lambda i: (i, 0, 0))
    cx = pl.BlockSpec((Lc, 256), lambda i: (0, 0))
    return _call(
        body, "att_bwd", (n,),
        specs + [pl.BlockSpec((ATT_HEADS, LANES), lambda i: (0, 0))] + [pl.BlockSpec((BLK, 1024), lambda i: (i, 1))] * 2,
        [row, loc, loc, cx, cx, pl.BlockSpec((ATT_HEADS, LANES), lambda i: (0, 0))],
        [jax.ShapeDtypeStruct((L, 1024), BF16), jax.ShapeDtypeStruct((n, N_LOC, 256), F32),
         jax.ShapeDtypeStruct((n, N_LOC, 256), F32), jax.ShapeDtypeStruct((Lc, 256), F32),
         jax.ShapeDtypeStruct((Lc, 256), F32), jax.ShapeDtypeStruct((ATT_HEADS, LANES), F32)], [],
        ("arbitrary",), (*args, sinkb, Y, dY), comm)


def _assemble_dp(L, Lc, dqf, dqb, dkf, dkb, dvf, dvb, drg, daq, dkl, dvl, rope_att, dck, dcv, dkx, dvx):
    n = L // BLK
    nc = Lc // BLK

    def body(dqf_r, dqb_r, dkf_r, dkb_r, dvf_r, dvb_r, drg_r, daq_r, kl0, kl1, kl2, vl0, vl1, vl2, rp_r,
             dck_r, dcv_r, dkx_r, dvx_r, o_ref):
        i = pl.program_id(0)

        @pl.when(i < n)
        def _():
            add = lambda a, b: (a[...].astype(F32) + b[...].astype(F32)).astype(o_ref.dtype)
            o_ref[:, C_RQ:C_RK] = add(dqf_r, dqb_r)
            o_ref[:, C_RK:C_RV] = add(dkf_r, dkb_r)
            o_ref[:, C_RV:C_RG] = add(dvf_r, dvb_r)
            o_ref[:, C_RG:C_AQ] = drg_r[...].astype(o_ref.dtype)
            o_ref[:, C_AQ:C_AK] = daq_r[...].astype(o_ref.dtype)
            w0 = jnp.where(i > 0, 1.0, 0.0)
            w2 = jnp.where(i < n - 1, 1.0, 0.0)
            dk = kl0[0] * w0 + kl1[0] + kl2[0] * w2
            dv = vl0[0] * w0 + vl1[0] + vl2[0] * w2
            for t in range(ATT_KV // 2):
                sl = slice(t * LANES, (t + 1) * LANES)
                o_ref[:, C_AK + t * LANES:C_AK + (t + 1) * LANES] = _rope_t(
                    dk[:, sl], rp_r[0], rp_r[1], 16).astype(o_ref.dtype)
            o_ref[:, C_AV:D_PROJ] = dv.astype(o_ref.dtype)

        @pl.when(i >= n)
        def _():
            o_ref[:, C_RQ:C_RK] = jnp.zeros((BLK, C_RK - C_RQ), o_ref.dtype)
            o_ref[:, C_RK:C_RV] = dck_r[...].astype(o_ref.dtype)
            o_ref[:, C_RV:C_RG] = dcv_r[...].astype(o_ref.dtype)
            o_ref[:, C_RG:C_AK] = jnp.zeros((BLK, C_AK - C_RG), o_ref.dtype)
            o_ref[:, C_AK:C_AV] = dkx_r[...].astype(o_ref.dtype)
            o_ref[:, C_AV:D_PROJ] = dvx_r[...].astype(o_ref.dtype)

    xm = lambda i: jnp.minimum(i, n - 1)
    cm = lambda i: jnp.clip(i - n, 0, nc - 1)
    r512 = pl.BlockSpec((BLK, 512), lambda i: (xm(i), 0))
    r1024 = pl.BlockSpec((BLK, 1024), lambda i: (xm(i), 0))
    part = lambda off: pl.BlockSpec((1, BLK, 256), lambda i: (jnp.clip(xm(i) + off, 0, n - 1), 1 - off, 0))
    return pl.pallas_call(
        body, name="assemble_dp", grid=(n + nc,),
        in_specs=[r512, r512, r512, r512, r1024, r1024, r1024, r1024,
                  part(-1), part(0), part(1), part(-1), part(0), part(1),
                  pl.BlockSpec((2, BLK, LANES), lambda i: (0, xm(i), 0)),
                  pl.BlockSpec((BLK, 512), lambda i: (cm(i), 0)), pl.BlockSpec((BLK, 1024), lambda i: (cm(i), 0)),
                  pl.BlockSpec((BLK, 256), lambda i: (cm(i), 0)), pl.BlockSpec((BLK, 256), lambda i: (cm(i), 0))],
        out_specs=pl.BlockSpec((BLK, D_PROJ), lambda i: (i, 0)),
        out_shape=jax.ShapeDtypeStruct((L + Lc, D_PROJ), BF16),
        compiler_params=_cparams(("parallel",)),
    )(dqf, dqb, dkf, dkb, dvf, dvb, drg, daq, dkl, dkl, dkl, dvl, dvl, dvl, rope_att, dck, dcv, dkx, dvx)


def _adam_math(w, g, m, v):
    m = ADAM_B1 * m + (1.0 - ADAM_B1) * g
    v = ADAM_B2 * v + (1.0 - ADAM_B2) * (g * g)
    m_hat = m / (1.0 - ADAM_B1 ** ADAM_STEP)
    v_hat = v / (1.0 - ADAM_B2 ** ADAM_STEP)
    delta = -ADAM_LR * (m_hat / (jnp.sqrt(v_hat) + ADAM_EPS) + ADAM_WD * w)
    return delta, m, v


def _adam(name, w, m, v, g=None, parts=None):
    R, C = w.shape
    tr = _pick(R, (256, 128, 64, 32, 16, 8))
    summed = parts is not None
    n_parts = parts.shape[0] if summed else 0

    def body(w_ref, m_ref, v_ref, g_ref, go_ref, d_ref, mo_ref, vo_ref):
        if summed:
            gv = g_ref[0].astype(F32)
            for j in range(1, n_parts):
                gv = gv + g_ref[j].astype(F32)
        else:
            gv = g_ref[...]
        d, mn, vn = _adam_math(w_ref[...], gv, m_ref[...], v_ref[...])
        go_ref[...] = gv
        d_ref[...] = d
        mo_ref[...] = mn
        vo_ref[...] = vn

    row = pl.BlockSpec((tr, C), lambda i: (i, 0))
    gspec = pl.BlockSpec((n_parts, tr, C), lambda i: (0, i, 0)) if summed else row
    return pl.pallas_call(
        body, name=name, grid=(R // tr,),
        in_specs=[row, row, row, gspec], out_specs=[row] * 4,
        out_shape=[jax.ShapeDtypeStruct((R, C), F32)] * 4,
        compiler_params=_cparams(("parallel",)),
    )(w, m, v, parts if summed else g)


def _rows_full(g):
    _, R, D = g.shape
    return g.reshape(N_DEV * R, D)


def _rows_slots(g):
    N, D = g.shape
    return g.reshape(N_DEV, N // N_DEV, D)


def _pad_rows(a, rows):
    return jnp.concatenate([a, jnp.zeros((rows - a.shape[0],) + a.shape[1:], a.dtype)], axis=0)


def kernel(x, c, ctx, c_ctx, w_mod, b_mod, norm_mix, norm_ffn, w_in, ret_decay, attn_sink, w_out, w_gate, w_up, w_down, norm_final, loss_target, m_c_ctx, m_w_mod, m_b_mod, m_norm_mix, m_norm_ffn, m_w_in, m_ret_decay, m_attn_sink, m_w_out, m_w_gate, m_w_up, m_w_down, m_norm_final, v_c_ctx, v_w_mod, v_b_mod, v_norm_mix, v_norm_ffn, v_w_in, v_ret_decay, v_attn_sink, v_w_out, v_w_gate, v_w_up, v_w_down, v_norm_final):
    L, D = x.shape[1], x.shape[2]
    Lc = ctx.shape[1]
    DF = w_gate.shape[2] * N_DEV
    C6 = w_mod.shape[2]
    me = _my_id()
    xs, cx, tgt = x[0], ctx[0], loss_target[0]

    ag_in = ("ag2", w_in[0].T.astype(BF16))
    ag_out, ag_gate = ("ag2", w_out[0].astype(BF16)), ("ag2", w_gate[0].T.astype(BF16))
    ag_up, ag_down = ("ag2", w_up[0].T.astype(BF16)), ("ag2", w_down[0].astype(BF16))

    cs = _allgather(c, "ag_c")[:, 0, :]
    s_in = _pad_rows(jnp.concatenate([cs, c_ctx[None, :]], axis=0), 16)
    b_l = lax.dynamic_slice_in_dim(b_mod, me * C6, C6, axis=1)
    mod_parts = _allgather(_mod_fwd(s_in, w_mod[0], b_l), "ag_mod")
    mod = _pad_rows(lax.dynamic_index_in_dim(mod_parts, me, axis=1, keepdims=False).reshape(6, D), 8)
    modc = _pad_rows(mod_parts[:, N_DEV, :].reshape(6, D), 8)
    mix_mod, ffn_mod = mod, jnp.roll(mod, -3, axis=0)
    gt_m, gt_f = mod[2:3], mod[5:6]

    rope_ret, rope_att = _rope_tables(L)
    rdb = jnp.broadcast_to(ret_decay[0].reshape(2 * RET_HEADS, 1, 1), (2 * RET_HEADS, 1, LANES))
    sinkb = jnp.broadcast_to(attn_sink[0].reshape(ATT_HEADS, 1), (ATT_HEADS, LANES))

    tm = _pick(L + Lc, (1408, 768, 512, 384, 256, 128))
    tmx = _pick(L, (1024, 512, 256, 128))

    (H,), (g_in,) = _modulate_fwd("mod_mix_fwd", xs, cx, norm_mix, mix_mod, modc, comm=[ag_in])
    W_inT = _rows_full(g_in)
    ident = lambda a, e: a
    tP, tD, tF = _pick(D_PROJ, (1152, 768, 512)), _pick(D, (2048, 1024, 512)), _pick(DF, (512, 256, 128))
    (P,), (g_gate,) = _matmul("mm_in", [(H, W_inT, 0)], 1, L + Lc, D_PROJ, D, "nt",
                              (tm, _pick(D_PROJ, (1536, 768, 512)), D), [], [BF16], ident,
                              comm=[ag_gate])
    W_gateT = _rows_full(g_gate)
    tabs, ctab = _ret_tables(rdb, Lc)
    s0 = _ret_ctx_state(P, ctab, L, Lc)
    (o_f, o_b, st_f, st_b), (g_out,) = _ret_fwd(P, rope_ret, tabs, s0, L, comm=[ag_out])
    W_out = _rows_full(g_out)
    Y_half = _ret_finish_fwd(o_f, o_b, P, L)
    (Y,), (g_up,) = _att_fwd(P, rope_att, sinkb, Y_half, L, Lc, comm=[ag_up])
    W_upT = _rows_full(g_up)
    KO = Y.shape[1]
    f_mix = _matmul("mm_out", [(Y, W_out, 0)], 1, L, D, KO, "nn", (tmx, tD, KO), [], [BF16], ident)[0]

    x1, H2 = _residual_modulate_fwd("mod_ffn_fwd", xs, f_mix, gt_m, norm_ffn, ffn_mod)

    def swiglu_epi(a, e):
        sg = _sigmoid(a[0])
        act = a[0] * sg
        return [act, a[1] * (sg * (1.0 + a[0] * (1.0 - sg))), act * a[1]]

    tm2 = tmx
    (act, up_dact, hmid), (g_down,) = _matmul("mm_gate_up", [(H2, W_gateT, 0), (H2, W_upT, 1)], 2, L, DF, D, "nt",
                                              (tm2, tF, D), [], [BF16, BF16, BF16], swiglu_epi, comm=[ag_down])
    W_down = _rows_full(g_down)
    f_ffn = _matmul("mm_down", [(hmid, W_down, 0)], 1, L, D, DF, "nn", (tm2, tD, _pick(DF, (1408, 512, 256, 128))),
                    [], [BF16], ident)[0]

    dx2, dFf, sums_l = _loss_head(x1, tgt, norm_final.reshape(1, D), f_ffn, gt_f)

    def dswiglu_epi(a, e):
        return [a[0] * e[0].astype(F32), a[0] * e[1].astype(F32)]

    dga, dup = _matmul("mm_d_down", [(dFf, W_down, 0)], 1, L, DF, D, "nt", (tm2, tF, D),
                       [(up_dact, "mn"), (act, "mn")], [BF16, BF16], dswiglu_epi)
    tkt, tkl = _pick(L, (512, 256, 128)), _pick(L, (1024, 512, 256, 128))
    dW_down =_matmul("mm_gw_down", [(hmid, dFf, 0)], 1, DF, D, L, "tn",
                      (_pick(DF, (1408, 512, 256, 128)), tD, tkt), [], [BF16], ident)[0]
    (dW_gateT, dW_upT), (p_down,) = _matmul("mm_gw_gate_up", [(dga, H2, 0), (dup, H2, 1)], 2, DF, D, L, "tn",
                                            (tF, tD, tkl), [], [BF16, BF16], ident,
                                            comm=[("a2a", _rows_slots(dW_down))])
    (dH2,), (p_gate,) = _matmul("mm_d_gate_up", [(dga, W_gateT, 0), (dup, W_upT, 0)], 1, L, D, DF, "nn",
                                (tm2, tD, tF), [], [BF16], ident, comm=[("a2a", _rows_slots(dW_gateT))])
    dx1, dFm, sums_f = _modulate_bwd("mod_ffn_bwd", x1, None, dH2, norm_ffn, ffn_mod, None, dx2, f_mix, gt_m)

    tO = _pick(KO, (2048, 1024, 512))
    dY = _matmul("mm_d_out", [(dFm, W_out, 0)], 1, L, KO, D, "nt", (tmx, tO, D), [], [BF16], ident)[0]
    dW_out = _matmul("mm_gw_out", [(Y, dFm, 0)], 1, KO, D, L, "tn",
                     (_pick(KO, (1024, 512)), tD, _pick(L, (2048, 1024, 512, 256, 128))), [], [BF16],
                     ident)[0]
    dO, drg = _ret_finish_bwd(o_f, o_b, P, dY, L)
    (dqf, dkf, dvf, dqb, dkb, dvb, ds0, dlg), (p_out,) = _ret_bwd(
        P, rope_ret, tabs, st_f, st_b, dO, L, comm=[("a2a", _rows_slots(dW_out))])
    dck, dcv, d_rd = _ret_ctx_bwd(P, ctab, ds0, dlg, rdb, L, Lc)
    (daq, dkl, dvl, dkx, dvx, d_sink), (p_up,) = _att_bwd(
        P, rope_att, sinkb, Y, dY, L, Lc, comm=[("a2a", _rows_slots(dW_upT))])
    dP = _assemble_dp(L, Lc, dqf, dqb, dkf, dkb, dvf, dvb, drg, daq, dkl, dvl, rope_att, dck, dcv, dkx, dvx)
    tkc = _pick(L + Lc, (768, 256, 128))
    dW_inT = _matmul("mm_gw_in", [(dP, H, 0)], 1, D_PROJ, D, L + Lc, "tn", (tP, tD, tkc), [], [BF16], ident)[0]
    (dH,), (p_in,) = _matmul("mm_d_in", [(dP, W_inT, 0)], 1, L + Lc, D, D_PROJ, "nn",
                             (tm, tD, _pick(D_PROJ, (768, 512, 256))), [], [BF16], ident,
                             comm=[("a2a", _rows_slots(dW_inT))])
    grad_x, sums_m = _modulate_bwd("mod_mix_bwd", xs, cx, dH, norm_mix, mix_mod, modc, dx1, None, None)

    zero = jnp.zeros((1, D), F32)
    dmod = jnp.concatenate([sums_m[0:1], sums_m[1:2], sums_f[6:7], sums_f[0:1], sums_f[1:2], sums_l[2:3]], axis=1)
    dmodc = jnp.concatenate([sums_m[3:4], sums_m[4:5], zero, zero, zero, zero], axis=1)
    dm_all = _allgather(jnp.concatenate([dmod, dmodc], axis=0), "ag_dmod")
    dm_cols = lax.dynamic_slice_in_dim(dm_all, me * C6, C6, axis=2)
    dm_in = jnp.concatenate([dm_cols[:, 0, :], dm_cols[:, 1, :]], axis=0)
    s_bwd = jnp.concatenate([cs, jnp.broadcast_to(c_ctx[None, :], (N_DEV, D))], axis=0)
    g_w_mod, dsil = _mod_bwd(s_bwd, dm_in, w_mod[0])

    lane_pad = lambda a: _pad_rows(a.reshape(-1, 1), LANES).reshape(1, LANES)
    pack = jnp.concatenate([dsil[0:1], sums_m[2:3], sums_f[2:3], sums_l[1:2],
                            lane_pad(d_rd[:, 0]), lane_pad(d_sink[:, 0]), sums_l[3:4, 0:LANES]], axis=1)
    packs = _allgather(pack, "ag_small")
    zl = jnp.zeros((1, LANES), F32)

    def pack_w(a_c, a_nm, a_nf, a_fin, a_rd, a_sk):
        return jnp.concatenate([a_c.reshape(1, D), a_nm, a_nf, a_fin.reshape(1, D), lane_pad(a_rd.reshape(-1)),
                                lane_pad(a_sk.reshape(-1)), zl], axis=1)

    sg, sd, sm, sv = _adam("adam_small", pack_w(c_ctx, norm_mix, norm_ffn, norm_final, ret_decay, attn_sink),
                           pack_w(m_c_ctx, m_norm_mix, m_norm_ffn, m_norm_final, m_ret_decay, m_attn_sink),
                           pack_w(v_c_ctx, v_norm_mix, v_norm_ffn, v_norm_final, v_ret_decay, v_attn_sink),
                           parts=packs)
    loss = sg[0, 4 * D + 2 * LANES]

    def unpack(a):
        return (a[0, 0:D], a[:, D:2 * D], a[:, 2 * D:3 * D], a[0, 3 * D:4 * D],
                a[0, 4 * D:4 * D + 2 * RET_HEADS].reshape(1, 2, RET_HEADS),
                a[:, 4 * D + LANES:4 * D + LANES + ATT_HEADS])

    bg, bd, bm, bv = _adam("adam_b_mod", b_mod, m_b_mod, v_b_mod, parts=dm_all.reshape(2 * N_DEV, 1, 6 * D))
    wg, wd, wm, wv = _adam("adam_w_mod", w_mod[0], m_w_mod[0], v_w_mod[0], g=g_w_mod)

    big = {}
    for nm, w, m, v, parts, transposed in (
            ("w_in", w_in, m_w_in, v_w_in, p_in, True), ("w_out", w_out, m_w_out, v_w_out, p_out, False),
            ("w_gate", w_gate, m_w_gate, v_w_gate, p_gate, True), ("w_up", w_up, m_w_up, v_w_up, p_up, True),
            ("w_down", w_down, m_w_down, v_w_down, p_down, False)):
        if transposed:
            res = [a.T for a in _adam("adam_" + nm, w[0].T, m[0].T, v[0].T, parts=parts)]
        else:
            res = _adam("adam_" + nm, w[0], m[0], v[0], parts=parts)
        big[nm] = [a[None] for a in res]

    g_s, d_s, m_s, v_s = unpack(sg), unpack(sd), unpack(sm), unpack(sv)

    def leaves(k, small, bmod, wmod):
        return (small[0], wmod[None], bmod, small[1], small[2], big["w_in"][k], small[4], small[5],
                big["w_out"][k], big["w_gate"][k], big["w_up"][k], big["w_down"][k], small[3])

    return (loss, grad_x[None], *leaves(0, g_s, bg, wg), *leaves(1, d_s, bd, wd),
            *leaves(2, m_s, bm, wm), *leaves(3, v_s, bv, wv))
```

```python
import functools

import jax
import jax.numpy as jnp
from jax import lax
from jax.experimental import pallas as pl
from jax.experimental.pallas import tpu as pltpu

F32 = jnp.float32
BF16 = jnp.bfloat16

N_DEV = 8
LANES = 128
RET_HEADS = 8
RET_DK = 64
RET_DV = 128
CHUNK = 128
ATT_HEADS = 16
ATT_KV = 4
ATT_DH = 64
GRID_W = 64
ROPE_BASE = 10000.0
EPS = 1e-6
NEG = -1e30
C_RQ, C_RK, C_RV, C_RG, C_AQ, C_AK, C_AV, D_PROJ = 0, 512, 1024, 2048, 3072, 4096, 4352, 4608
K_SCALE = RET_DK ** -0.5
A_SCALE = ATT_DH ** -0.5

ADAM_LR, ADAM_B1, ADAM_B2, ADAM_EPS, ADAM_WD, ADAM_STEP = 0.001, 0.9, 0.999, 1e-08, 0.01, 10

VMEM_BIG = 52 * 1024 * 1024

NN = (((1,), (0,)), ((), ()))
NT = (((1,), (1,)), ((), ()))
TN = (((0,), (0,)), ((), ()))


def _dot(a, b, dims):
    return lax.dot_general(a, b, dims, preferred_element_type=F32)


def _cparams(sem, vmem=VMEM_BIG):
    return pltpu.CompilerParams(dimension_semantics=sem, vmem_limit_bytes=vmem)


def _pick(dim, prefs):
    for p in prefs:
        if dim % p == 0:
            return p
    return dim


def _my_id():
    return lax.axis_index("x") * 4 + lax.axis_index("y") * 2 + lax.axis_index("c")


def _sigmoid(x):
    return 0.5 * jnp.tanh(0.5 * x) + 0.5


def _peers():
    mx, my, mc = lax.axis_index("x"), lax.axis_index("y"), lax.axis_index("c")
    out = []
    for k in range(1, N_DEV):
        kx, ky, kc = (k >> 2) & 1, (k >> 1) & 1, k & 1
        px = 1 - mx if kx else mx
        py = 1 - my if ky else my
        pc = 1 - mc if kc else mc
        out.append(((px, py, pc), px * 4 + py * 2 + pc))
    return out


def _exchange_copies(kind, x_ref, o_ref, ssem, rsem, lsem):
    me = _my_id()
    loc = pltpu.make_async_copy(x_ref if kind == "ag" else x_ref.at[me], o_ref.at[me], lsem)
    cps = []
    for k, (peer, pid) in enumerate(_peers()):
        cps.append(pltpu.make_async_remote_copy(
            src_ref=x_ref if kind == "ag" else x_ref.at[pid], dst_ref=o_ref.at[me],
            send_sem=ssem.at[k], recv_sem=rsem.at[k], device_id=peer, device_id_type=pl.DeviceIdType.MESH))
    return loc, cps


def _two_level_copies(x_ref, o_ref, ssem, rsem, lsem):
    mx, my, mc = lax.axis_index("x"), lax.axis_index("y"), lax.axis_index("c")
    me = mx * 4 + my * 2 + mc
    sibling = (mx, my, 1 - mc)
    chips = [(1 - mx, my), (mx, 1 - my), (1 - mx, 1 - my)]

    def copy(k, slot, to, src=None):
        return pltpu.make_async_remote_copy(
            src_ref=o_ref.at[slot] if src is None else src, dst_ref=o_ref.at[slot],
            send_sem=ssem.at[k], recv_sem=rsem.at[k], device_id=to, device_id_type=pl.DeviceIdType.MESH)

    loc = pltpu.make_async_copy(x_ref, o_ref.at[me], lsem)
    first = [copy(0, me, sibling, src=x_ref)]
    first += [copy(1 + j, me, (cx, cy, mc), src=x_ref) for j, (cx, cy) in enumerate(chips)]
    passed = [copy(4 + j, cx * 4 + cy * 2 + mc, sibling) for j, (cx, cy) in enumerate(chips)]
    return loc, first, passed


def _exchange_start(kind, x_ref, o_ref, ssem, rsem, lsem):
    if kind == "ag2":
        loc, first, _ = _two_level_copies(x_ref, o_ref, ssem, rsem, lsem)
        cps = first
    else:
        loc, cps = _exchange_copies(kind, x_ref, o_ref, ssem, rsem, lsem)
    loc.start()
    for cp in cps:
        cp.start()


def _exchange_pass_on(kind, x_ref, o_ref, ssem, rsem, lsem):
    if kind == "ag2":
        _, first, passed = _two_level_copies(x_ref, o_ref, ssem, rsem, lsem)
        for j in range(3):
            first[1 + j].wait_recv()
            passed[j].start()


def _exchange_wait(kind, x_ref, o_ref, ssem, rsem, lsem):
    if kind == "ag2":
        loc, first, passed = _two_level_copies(x_ref, o_ref, ssem, rsem, lsem)
        first[0].wait_recv()
        for cp in passed:
            cp.wait_recv()
        cps = first + passed
    else:
        loc, cps = _exchange_copies(kind, x_ref, o_ref, ssem, rsem, lsem)
        for cp in cps:
            cp.wait_recv()
    for cp in cps:
        cp.wait_send()
    loc.wait()


_EXCHANGE_SEMS = [pltpu.SemaphoreType.DMA((N_DEV - 1,)), pltpu.SemaphoreType.DMA((N_DEV - 1,)),
                  pltpu.SemaphoreType.DMA(())]


def _exchange_shape(kind, x):
    return jax.ShapeDtypeStruct(x.shape if kind == "a2a" else (N_DEV,) + x.shape, x.dtype)


def _exchange(kind, x, name):
    def body(x_ref, o_ref, ssem, rsem, lsem):
        _exchange_start(kind, x_ref, o_ref, ssem, rsem, lsem)
        _exchange_pass_on(kind, x_ref, o_ref, ssem, rsem, lsem)
        _exchange_wait(kind, x_ref, o_ref, ssem, rsem, lsem)

    return pl.pallas_call(
        body, name=name, out_shape=_exchange_shape(kind, x),
        in_specs=[pl.BlockSpec(memory_space=pl.ANY)], out_specs=pl.BlockSpec(memory_space=pl.ANY),
        scratch_shapes=list(_EXCHANGE_SEMS),
    )(x)


def _allgather(x, name):
    return _exchange("ag", x, name)


def _call(body, name, grid, in_specs, out_specs, out_shape, scratch_shapes, sem, args, comm=(), aliases=None,
          pass_on_at=0.75):
    in_specs, out_specs, out_shape = list(in_specs), list(out_specs), list(out_shape)
    scratch_shapes = list(scratch_shapes)
    aliases = aliases or {}
    if not comm:
        outs = pl.pallas_call(body, name=name, grid=grid, in_specs=in_specs, out_specs=out_specs, out_shape=out_shape,
                              scratch_shapes=scratch_shapes, input_output_aliases=aliases,
                              compiler_params=_cparams(sem))(*args)
        return list(outs), []
    n_in, n_out, n_scr, n_c = len(in_specs), len(out_specs), len(scratch_shapes), len(comm)
    hbm = pl.BlockSpec(memory_space=pl.ANY)

    def wrapped(*refs):
        ins, cins = refs[:n_in], refs[n_in:n_in + n_c]
        outs = refs[n_in + n_c:n_in + n_c + n_out]
        couts = refs[n_in + n_c + n_out:n_in + 2 * n_c + n_out]
        scr = refs[n_in + 2 * n_c + n_out:n_in + 2 * n_c + n_out + n_scr]
        sems = refs[n_in + 2 * n_c + n_out + n_scr:]
        step, total = pl.program_id(0), grid[0]
        for ax in range(1, len(grid)):
            step = step * grid[ax] + pl.program_id(ax)
            total *= grid[ax]

        @pl.when(step == 0)
        def _():
            for c, (kind, _) in enumerate(comm):
                _exchange_start(kind, cins[c], couts[c], *sems[3 * c:3 * c + 3])

        body(*ins, *outs, *scr)

        @pl.when(step == min(total - 1, int(total * pass_on_at)))
        def _():
            for c, (kind, _) in enumerate(comm):
                _exchange_pass_on(kind, cins[c], couts[c], *sems[3 * c:3 * c + 3])

        @pl.when(step == total - 1)
        def _():
            for c, (kind, _) in enumerate(comm):
                _exchange_wait(kind, cins[c], couts[c], *sems[3 * c:3 * c + 3])

    res = pl.pallas_call(
        wrapped, name=name, grid=grid,
        in_specs=in_specs + [hbm] * n_c, out_specs=out_specs + [hbm] * n_c,
        out_shape=out_shape + [_exchange_shape(kind, arr) for kind, arr in comm],
        scratch_shapes=scratch_shapes + list(_EXCHANGE_SEMS) * n_c, input_output_aliases=aliases,
        compiler_params=_cparams(("arbitrary",) * len(grid)),
    )(*args, *[arr for _, arr in comm])
    return list(res[:n_out]), list(res[n_out:])


def _matmul(name, pairs, n_acc, M, N, K, mode, tiles, extras, out_dtypes, epilogue, j_outer=False, comm=()):
    tm, tn, tk = tiles
    gm, gn, nk = M // tm, N // tn, K // tk
    assert gm * tm == M and gn * tn == N and nk * tk == K, (name, M, N, K, tiles)
    if j_outer:
        grid = (gn, gm, nk)
        ij = lambda g0, g1: (g1, g0)
    else:
        grid = (gm, gn, nk)
        ij = lambda g0, g1: (g0, g1)

    if mode in ("nn", "nt"):
        a_spec = pl.BlockSpec((tm, tk), lambda g0, g1, k: (ij(g0, g1)[0], k))
    else:
        a_spec = pl.BlockSpec((tk, tm), lambda g0, g1, k: (k, ij(g0, g1)[0]))
    if mode == "nt":
        b_spec = pl.BlockSpec((tn, tk), lambda g0, g1, k: (ij(g0, g1)[1], k))
    else:
        b_spec = pl.BlockSpec((tk, tn), lambda g0, g1, k: (k, ij(g0, g1)[1]))
    dims = {"nn": NN, "nt": NT, "tn": TN}[mode]
    mn_spec = pl.BlockSpec((tm, tn), lambda g0, g1, k: ij(g0, g1))
    n_spec = pl.BlockSpec((1, tn), lambda g0, g1, k: (0, ij(g0, g1)[1]))

    in_specs, args = [], []
    for a, b, _ in pairs:
        in_specs += [a_spec, b_spec]
        args += [a, b]
    for arr, kind in extras:
        in_specs.append(mn_spec if kind == "mn" else n_spec)
        args.append(arr)
    n_p, n_e, n_o = len(pairs), len(extras), len(out_dtypes)

    def body(*refs):
        ab = refs[:2 * n_p]
        ex = refs[2 * n_p:2 * n_p + n_e]
        outs = refs[2 * n_p + n_e:2 * n_p + n_e + n_o]
        accs = refs[2 * n_p + n_e + n_o:]
        k = pl.program_id(2)

        def partial_sums():
            sums = [None] * n_acc
            for p, (_, _, ai) in enumerate(pairs):
                d = _dot(ab[2 * p][...], ab[2 * p + 1][...], dims)
                sums[ai] = d if sums[ai] is None else sums[ai] + d
            return sums

        def finish(acc_vals):
            res = epilogue(acc_vals, [e[...] for e in ex])
            for o, r in zip(outs, res):
                o[...] = r.astype(o.dtype)

        def accumulate(first):
            w = _pick(tm if mode == "tn" else tn, (512, 384, 256))
            for c in range((tm if mode == "tn" else tn) // w):
                sl = slice(c * w, (c + 1) * w)
                sums = [None] * n_acc
                for p, (_, _, ai) in enumerate(pairs):
                    a_ref, b_ref = ab[2 * p], ab[2 * p + 1]
                    if mode == "tn":
                        d = _dot(a_ref[:, sl], b_ref[...], dims)
                    elif mode == "nn":
                        d = _dot(a_ref[...], b_ref[:, sl], dims)
                    else:
                        d = _dot(a_ref[...], b_ref[sl, :], dims)
                    sums[ai] = d if sums[ai] is None else sums[ai] + d
                idx = (sl, slice(None)) if mode == "tn" else (slice(None), sl)
                for ai, s in enumerate(sums):
                    if first:
                        accs[ai][idx] = s
                    else:
                        accs[ai][idx] += s

        if nk == 1:
            finish(partial_sums())
        else:
            pl.when(k == 0)(functools.partial(accumulate, True))
            pl.when(k > 0)(functools.partial(accumulate, False))

            @pl.when(k == nk - 1)
            def _():
                finish([a[...] for a in accs])

    outs, couts = _call(
        body, name, grid, in_specs, [mn_spec] * n_o,
        [jax.ShapeDtypeStruct((M, N), dt) for dt in out_dtypes],
        [pltpu.VMEM((tm, tn), F32) for _ in range(n_acc if nk > 1 else 0)],
        ("parallel", "parallel", "arbitrary"), args, comm)
    return (outs, couts) if comm else outs


def _rope_tables(L):
    t = jnp.arange(L, dtype=jnp.int32)
    f = jnp.arange(32, dtype=jnp.int32).astype(F32)
    ang = t.astype(F32)[:, None] * (ROPE_BASE ** (-f / 32.0))[None, :]
    cos, sin = jnp.cos(ang), jnp.sin(ang)
    ret = jnp.stack([jnp.tile(cos, (1, 4)), jnp.tile(jnp.concatenate([-sin, sin], axis=1), (1, 2))])
    f2 = jnp.arange(16, dtype=jnp.int32).astype(F32)
    inv2 = (ROPE_BASE ** (-f2 / 16.0))[None, :]
    ang_r = (t // GRID_W).astype(F32)[:, None] * inv2
    ang_c = (t % GRID_W).astype(F32)[:, None] * inv2
    cr, sr, cc, sc = jnp.cos(ang_r), jnp.sin(ang_r), jnp.cos(ang_c), jnp.sin(ang_c)
    att = jnp.stack([jnp.tile(jnp.concatenate([cr, cr, cc, cc], axis=1), (1, 2)),
                     jnp.tile(jnp.concatenate([-sr, sr, -sc, sc], axis=1), (1, 2))])
    return ret.astype(F32), att.astype(F32)


def _swap(x, sh):
    lane = lax.broadcasted_iota(jnp.int32, x.shape, 1)
    ra = pltpu.roll(x, LANES - sh, 1)
    rb = pltpu.roll(x, sh, 1)
    la = pltpu.roll(lane, LANES - sh, 1)
    partner = jnp.where((lane % (2 * sh)) < sh, lane + sh, lane - sh)
    return jnp.where(la == partner, ra, rb)


def _rope(x, cos, sin, sh):
    return x * cos + _swap(x, sh) * sin


def _rope_t(d, cos, sin, sh):
    return d * cos + _swap(d * sin, sh)


def _half_mask(shape, a):
    lane = lax.broadcasted_iota(jnp.int32, shape, 1)
    return (lane < 64) if a == 0 else (lane >= 64)


def _mod_fwd(s_in, w_l, b_l):
    D, C6 = w_l.shape
    tk = _pick(D, (512, 256, 128))
    nk = D // tk

    def body(s_ref, w_ref, b_ref, o_ref):
        k = pl.program_id(0)
        s = s_ref[...]
        s = s * _sigmoid(s)
        d = jnp.dot(s, w_ref[...], preferred_element_type=F32, precision=lax.Precision.HIGHEST)

        @pl.when(k == 0)
        def _():
            o_ref[...] = d + b_ref[...]

        @pl.when(k > 0)
        def _():
            o_ref[...] += d

    return pl.pallas_call(
        body, name="mod_fwd", grid=(nk,),
        in_specs=[pl.BlockSpec((16, tk), lambda k: (0, k)), pl.BlockSpec((tk, C6), lambda k: (k, 0)),
                  pl.BlockSpec((1, C6), lambda k: (0, 0))],
        out_specs=pl.BlockSpec((16, C6), lambda k: (0, 0)),
        out_shape=jax.ShapeDtypeStruct((16, C6), F32),
        compiler_params=_cparams(("arbitrary",)),
    )(s_in, w_l, b_l)


def _mod_bwd(s_in, dm, w_l):
    D, C6 = w_l.shape
    tk = _pick(D, (512, 256, 128))
    nk = D // tk

    def body(s_ref, dm_ref, w_ref, gw_ref, gc_ref):
        s = s_ref[...]
        sg = _sigmoid(s)
        act = s * sg
        dmv = dm_ref[...]
        gw_ref[...] = lax.dot_general(act, dmv, TN, preferred_element_type=F32, precision=lax.Precision.HIGHEST)
        ds = lax.dot_general(dmv, w_ref[...], NT, preferred_element_type=F32, precision=lax.Precision.HIGHEST)
        dsil = (sg * (1.0 + s * (1.0 - sg)))[8:9, :]
        gc_ref[...] = jnp.zeros((8, tk), F32) + jnp.sum(ds[8:16, :], axis=0, keepdims=True) * dsil

    return pl.pallas_call(
        body, name="mod_bwd", grid=(nk,),
        in_specs=[pl.BlockSpec((16, tk), lambda k: (0, k)), pl.BlockSpec((16, C6), lambda k: (0, 0)),
                  pl.BlockSpec((tk, C6), lambda k: (k, 0))],
        out_specs=[pl.BlockSpec((tk, C6), lambda k: (k, 0)), pl.BlockSpec((8, tk), lambda k: (0, k))],
        out_shape=[jax.ShapeDtypeStruct((D, C6), F32), jax.ShapeDtypeStruct((8, D), F32)],
        compiler_params=_cparams(("parallel",)),
    )(s_in, dm, w_l)


def _norm_rows(x):
    r = lax.rsqrt(jnp.mean(x * x, axis=-1, keepdims=True) + EPS)
    return x * r, r


def _modulate_fwd(name, x, ctx, g, mod, modc, comm=()):
    L, D = x.shape
    tr = ctx.shape[0]
    nx = L // tr

    def body(x_ref, c_ref, g_ref, m_ref, mc_ref, o_ref):
        i = pl.program_id(0)

        def run(src, m):
            n, _ = _norm_rows(src[...])
            o_ref[...] = (n * g_ref[...] * (1.0 + m[1:2, :]) + m[0:1, :]).astype(o_ref.dtype)

        @pl.when(i < nx)
        def _():
            run(x_ref, m_ref)

        @pl.when(i >= nx)
        def _():
            run(c_ref, mc_ref)

    row = pl.BlockSpec((tr, D), lambda i: (jnp.minimum(i, nx - 1), 0))
    vec = pl.BlockSpec((1, D), lambda i: (0, 0))
    mv = pl.BlockSpec((8, D), lambda i: (0, 0))
    return _call(
        body, name, (nx + 1,), [row, pl.BlockSpec((tr, D), lambda i: (0, 0)), vec, mv, mv],
        [pl.BlockSpec((tr, D), lambda i: (i, 0))], [jax.ShapeDtypeStruct((L + tr, D), BF16)], [],
        ("parallel",), (x, ctx, g, mod, modc), comm, pass_on_at=1.0)


def _residual_modulate_fwd(name, x, fbr, gate, g, mod):
    L, D = x.shape
    tr = _pick(L, (512, 256, 128))

    def body(x_ref, f_ref, gt_ref, g_ref, m_ref, x1_ref, o_ref):
        x1 = x_ref[...] + gt_ref[...] * f_ref[...].astype(F32)
        x1_ref[...] = x1
        n, _ = _norm_rows(x1)
        o_ref[...] = (n * g_ref[...] * (1.0 + m_ref[1:2, :]) + m_ref[0:1, :]).astype(o_ref.dtype)

    row = pl.BlockSpec((tr, D), lambda i: (i, 0))
    vec = pl.BlockSpec((1, D), lambda i: (0, 0))
    return pl.pallas_call(
        body, name=name, grid=(L // tr,),
        in_specs=[row, row, vec, vec, pl.BlockSpec((8, D), lambda i: (0, 0))],
        out_specs=[row, row],
        out_shape=[jax.ShapeDtypeStruct((L, D), F32), jax.ShapeDtypeStruct((L, D), BF16)],
        compiler_params=_cparams(("parallel",)),
    )(x, fbr, gate, g, mod)


def _modulate_bwd(name, x, ctx, dh, g, mod, modc, dres, fbr, gate):
    L, D = x.shape
    tr = ctx.shape[0] if ctx is not None else _pick(L, (256, 128))
    nx = L // tr
    nt = nx + (1 if ctx is not None else 0)
    has_f = fbr is not None

    def body(*refs):
        refs = list(refs)
        x_ref = refs.pop(0)
        c_ref = refs.pop(0) if ctx is not None else None
        dh_ref, g_ref, m_ref = refs.pop(0), refs.pop(0), refs.pop(0)
        mc_ref = refs.pop(0) if ctx is not None else None
        dr_ref = refs.pop(0)
        f_ref = refs.pop(0) if has_f else None
        gt_ref = refs.pop(0) if has_f else None
        dx_ref = refs.pop(0)
        df_ref = refs.pop(0) if has_f else None
        acc_ref = refs.pop(0)
        i = pl.program_id(0)

        @pl.when(i == 0)
        def _():
            acc_ref[...] = jnp.zeros_like(acc_ref)

        def sums(src, m, base, grow):
            n, r = _norm_rows(src[...])
            d = dh_ref[...].astype(F32)
            gg = g_ref[...]
            sc1 = 1.0 + m[1:2, :]
            acc_ref[base:base + 1, :] += jnp.sum(d, axis=0, keepdims=True)
            dn = d * n
            acc_ref[base + 1:base + 2, :] += jnp.sum(dn, axis=0, keepdims=True) * gg
            acc_ref[grow:grow + 1, :] += jnp.sum(dn, axis=0, keepdims=True) * sc1
            dnv = d * (gg * sc1)
            return r * (dnv - n * jnp.mean(dnv * n, axis=-1, keepdims=True))

        def x_rows():
            dx = sums(x_ref, m_ref, 0, 2) + dr_ref[...]
            dx_ref[...] = dx
            if has_f:
                acc_ref[6:7, :] += jnp.sum(dx * f_ref[...].astype(F32), axis=0, keepdims=True)
                df_ref[...] = (dx * gt_ref[...]).astype(df_ref.dtype)

        if ctx is None:
            x_rows()
        else:
            pl.when(i < nx)(x_rows)

            @pl.when(i >= nx)
            def _():
                sums(c_ref, mc_ref, 3, 2)

    row = pl.BlockSpec((tr, D), lambda i: (jnp.minimum(i, nx - 1), 0))
    vec = pl.BlockSpec((1, D), lambda i: (0, 0))
    mv = pl.BlockSpec((8, D), lambda i: (0, 0))
    in_specs, args = [row], [x]
    if ctx is not None:
        in_specs.append(pl.BlockSpec((tr, D), lambda i: (0, 0)))
        args.append(ctx)
    in_specs += [pl.BlockSpec((tr, D), lambda i: (i, 0)), vec, mv]
    args += [dh, g, mod]
    if ctx is not None:
        in_specs.append(mv)
        args.append(modc)
    in_specs.append(row)
    args.append(dres)
    out_specs = [row]
    out_shape = [jax.ShapeDtypeStruct((L, D), F32)]
    if has_f:
        in_specs += [row, vec]
        args += [fbr, gate]
        out_specs.append(row)
        out_shape.append(jax.ShapeDtypeStruct((L, D), BF16))
    out_specs.append(pl.BlockSpec((16, D), lambda i: (0, 0)))
    out_shape.append(jax.ShapeDtypeStruct((16, D), F32))
    return pl.pallas_call(
        body, name=name, grid=(nt,), in_specs=in_specs, out_specs=out_specs, out_shape=out_shape,
        compiler_params=_cparams(("arbitrary",)),
    )(*args)


def _loss_head(x1, tgt, nf, fbr, gate):
    L, D = x1.shape
    tr = _pick(L, (256, 128))

    def body(x_ref, t_ref, w_ref, f_ref, gt_ref, dx_ref, df_ref, acc_ref):
        i = pl.program_id(0)

        @pl.when(i == 0)
        def _():
            acc_ref[...] = jnp.zeros_like(acc_ref)

        n, r = _norm_rows(x_ref[...] + gt_ref[...] * f_ref[...].astype(F32))
        w = w_ref[...]
        e = n * w - t_ref[...]
        acc_ref[0:1, :] += jnp.sum(e * e, axis=0, keepdims=True) * (0.5 / D)
        dout = e * (1.0 / D)
        acc_ref[1:2, :] += jnp.sum(dout * n, axis=0, keepdims=True)
        dn = dout * w
        dx = r * (dn - n * jnp.mean(dn * n, axis=-1, keepdims=True))
        dx_ref[...] = dx
        acc_ref[2:3, :] += jnp.sum(dx * f_ref[...].astype(F32), axis=0, keepdims=True)
        df_ref[...] = (dx * gt_ref[...]).astype(df_ref.dtype)

        @pl.when(i == pl.num_programs(0) - 1)
        def _():
            acc_ref[3:4, :] = jnp.zeros((1, D), F32) + jnp.sum(acc_ref[0:1, :])

    row = pl.BlockSpec((tr, D), lambda i: (i, 0))
    vec = pl.BlockSpec((1, D), lambda i: (0, 0))
    return pl.pallas_call(
        body, name="loss_head", grid=(L // tr,),
        in_specs=[row, row, vec, row, vec],
        out_specs=[row, row, pl.BlockSpec((8, D), lambda i: (0, 0))],
        out_shape=[jax.ShapeDtypeStruct((L, D), F32), jax.ShapeDtypeStruct((L, D), BF16),
                   jax.ShapeDtypeStruct((8, D), F32)],
        compiler_params=_cparams(("arbitrary",)),
    )(x1, tgt, nf, fbr, gate)


RET_SUB = 4
N_TAB = 7


def _ret_tables(rdb, Lc):
    def body(rd_ref, t_ref, c_ref):
        d = pl.program_id(0) // RET_HEADS
        fwd = d == 0
        lg = -jnp.exp(rd_ref[0])
        i = lax.broadcasted_iota(jnp.int32, (CHUNK, CHUNK), 0).astype(F32)
        j = lax.broadcasted_iota(jnp.int32, (CHUNK, CHUNK), 1).astype(F32)
        rel = jnp.where(fwd, i - j, j - i)
        mask = (rel > 0.0) | ((rel == 0.0) & fwd)
        dm = jnp.where(mask, jnp.exp(lg * jnp.maximum(rel, 0.0)), 0.0)
        t_ref[0, 0] = dm
        t_ref[0, 1] = rel * dm
        qc = jnp.where(fwd, i + 1.0, CHUNK - i)
        qw = jnp.exp(lg * qc)
        t_ref[0, 2] = qw
        t_ref[0, 3] = qw * qc
        kc = jnp.where(fwd, CHUNK - 1.0 - i, i)
        kw = jnp.exp(lg * kc)
        t_ref[0, 4] = kw
        t_ref[0, 5] = kw * kc
        t_ref[0, 6] = jnp.exp(lg * float(CHUNK)) + jnp.zeros((CHUNK, CHUNK), F32)
        m = lax.broadcasted_iota(jnp.int32, (Lc, LANES), 0).astype(F32)
        cc = jnp.where(fwd, Lc - 1.0 - m, m)
        cw = jnp.exp(lg * cc)
        c_ref[0, 0] = cw
        c_ref[0, 1] = cw * cc

    return pl.pallas_call(
        body, name="ret_tables", grid=(2 * RET_HEADS,),
        in_specs=[pl.BlockSpec((1, 1, LANES), lambda r: (r, 0, 0))],
        out_specs=[pl.BlockSpec((1, N_TAB, CHUNK, CHUNK), lambda r: (r, 0, 0, 0)),
                   pl.BlockSpec((1, 2, Lc, LANES), lambda r: (r, 0, 0, 0))],
        out_shape=[jax.ShapeDtypeStruct((2 * RET_HEADS, N_TAB, CHUNK, CHUNK), F32),
                   jax.ShapeDtypeStruct((2 * RET_HEADS, 2, Lc, LANES), F32)],
        compiler_params=_cparams(("parallel",)),
    )(rdb)


def _ret_ctx_state(P, ctab, L, Lc):
    cb = L // Lc

    def body(k_ref, v_ref, c_ref, s_ref):
        for p in range(RET_HEADS // 2):
            kp = k_ref[:, p * LANES:(p + 1) * LANES].astype(F32) * K_SCALE
            for a in range(2):
                h = 2 * p + a
                kh = jnp.where(_half_mask(kp.shape, a), kp, 0.0)
                vh = v_ref[:, h * RET_DV:(h + 1) * RET_DV]
                for d in range(2):
                    kw = (kh * c_ref[d * RET_HEADS + h, 0]).astype(BF16)
                    s_ref[d * RET_HEADS + h] = _dot(kw, vh, TN)

    return pl.pallas_call(
        body, name="ret_ctx_state", grid=(1,),
        in_specs=[pl.BlockSpec((Lc, 512), lambda i: (cb, C_RK // 512)),
                  pl.BlockSpec((Lc, 1024), lambda i: (cb, C_RV // 1024)),
                  pl.BlockSpec((2 * RET_HEADS, 2, Lc, LANES), lambda i: (0, 0, 0, 0))],
        out_specs=pl.BlockSpec((2 * RET_HEADS, LANES, RET_DV), lambda i: (0, 0, 0)),
        out_shape=jax.ShapeDtypeStruct((2 * RET_HEADS, LANES, RET_DV), F32),
        compiler_params=_cparams(("arbitrary",)),
    )(P, P, ctab)


def _ret_fwd(P, rope, tabs, s0, L, comm=()):
    n = L // CHUNK
    nb = n // RET_SUB

    def body(qf, kf, vf, rf, qb, kb, vb, rb, t_ref, s0_ref, of_ref, ob_ref, stf_ref, stb_ref, st):
        s = pl.program_id(0)

        @pl.when(s == 0)
        def _():
            st[...] = s0_ref[...]

        for rnd in range(RET_SUB):
            units = []
            for d, (q_ref, k_ref, v_ref, r_ref, o_ref, so_ref) in enumerate(
                    ((qf, kf, vf, rf, of_ref, stf_ref), (qb, kb, vb, rb, ob_ref, stb_ref))):
                j = rnd if d == 0 else RET_SUB - 1 - rnd
                rows = slice(j * CHUNK, (j + 1) * CHUNK)
                cos, sin = r_ref[0, rows, :], r_ref[1, rows, :]
                for p in range(RET_HEADS // 2):
                    qp = _rope(q_ref[rows, p * LANES:(p + 1) * LANES].astype(F32), cos, sin, 32)
                    kp = _rope(k_ref[rows, p * LANES:(p + 1) * LANES].astype(F32), cos, sin, 32) * K_SCALE
                    for a in range(2):
                        h = 2 * p + a
                        hm = _half_mask(qp.shape, a)
                        units.append(dict(r=d * RET_HEADS + h, h=h, a=a, j=j, rows=rows, o_ref=o_ref, so_ref=so_ref,
                                          v_ref=v_ref, qh=jnp.where(hm, qp, 0.0), kh=jnp.where(hm, kp, 0.0)))
            for u in units:
                u["sc"] = _dot(u["qh"].astype(BF16), u["kh"].astype(BF16), NT)
            for u in units:
                r, h = u["r"], u["h"]
                sp = st[r]
                u["so_ref"][u["j"], h] = sp[u["a"] * RET_DK:(u["a"] + 1) * RET_DK, :]
                vh = u["v_ref"][u["rows"], h * RET_DV:(h + 1) * RET_DV]
                o = _dot((u["sc"] * t_ref[r, 0]).astype(BF16), vh, NN)
                o += _dot((u["qh"] * t_ref[r, 2]).astype(BF16), sp.astype(BF16), NN)
                u["o_ref"][u["rows"], h * RET_DV:(h + 1) * RET_DV] = o
            for u in units:
                r, h = u["r"], u["h"]
                vh = u["v_ref"][u["rows"], h * RET_DV:(h + 1) * RET_DV]
                st[r] = t_ref[r, 6] * st[r] + _dot((u["kh"] * t_ref[r, 4]).astype(BF16), vh, TN)

    fw = lambda s: s
    bw = lambda s: nb - 1 - s
    RB = RET_SUB * CHUNK

    def specs(cm):
        return [pl.BlockSpec((RB, 512), lambda s: (cm(s), C_RQ // 512)),
                pl.BlockSpec((RB, 512), lambda s: (cm(s), C_RK // 512)),
                pl.BlockSpec((RB, 1024), lambda s: (cm(s), C_RV // 1024)),
                pl.BlockSpec((2, RB, LANES), lambda s: (0, cm(s), 0))]

    full = lambda shp: pl.BlockSpec(shp, lambda s: (0,) * len(shp))
    return _call(
        body, "ret_fwd", (nb,),
        specs(fw) + specs(bw) + [full((2 * RET_HEADS, N_TAB, CHUNK, CHUNK)), full((2 * RET_HEADS, LANES, RET_DV))],
        [pl.BlockSpec((RB, 1024), lambda s: (fw(s), 0)),
         pl.BlockSpec((RB, 1024), lambda s: (bw(s), 0)),
         pl.BlockSpec((RET_SUB, RET_HEADS, RET_DK, RET_DV), lambda s: (fw(s), 0, 0, 0)),
         pl.BlockSpec((RET_SUB, RET_HEADS, RET_DK, RET_DV), lambda s: (bw(s), 0, 0, 0))],
        [jax.ShapeDtypeStruct((L, 1024), F32), jax.ShapeDtypeStruct((L, 1024), F32),
         jax.ShapeDtypeStruct((n, RET_HEADS, RET_DK, RET_DV), F32),
         jax.ShapeDtypeStruct((n, RET_HEADS, RET_DK, RET_DV), F32)],
        [pltpu.VMEM((2 * RET_HEADS, LANES, RET_DV), F32)],
        ("arbitrary",), (P, P, P, rope, P, P, P, rope, tabs, s0), comm)


def _ret_finish_fwd(of, ob, P, L):
    tr = _pick(L, (512, 256, 128))

    def body(f_ref, b_ref, g_ref, y_ref):
        for h in range(RET_HEADS):
            sl = slice(h * RET_DV, (h + 1) * RET_DV)
            n, _ = _norm_rows(f_ref[:, sl] + b_ref[:, sl])
            g = g_ref[:, sl].astype(F32)
            y_ref[:, sl] = (n * (g * _sigmoid(g))).astype(y_ref.dtype)

    row = pl.BlockSpec((tr, 1024), lambda i: (i, 0))
    return pl.pallas_call(
        body, name="ret_finish_fwd", grid=(L // tr,),
        in_specs=[row, row, pl.BlockSpec((tr, 1024), lambda i: (i, C_RG // 1024))],
        out_specs=row, out_shape=jax.ShapeDtypeStruct((L, 2048), BF16),
        compiler_params=_cparams(("parallel",)),
    )(of, ob, P)


def _ret_finish_bwd(of, ob, P, dY, L):
    tr = _pick(L, (512, 256, 128))

    def body(f_ref, b_ref, g_ref, dy_ref, do_ref, dg_ref):
        for h in range(RET_HEADS):
            sl = slice(h * RET_DV, (h + 1) * RET_DV)
            n, r = _norm_rows(f_ref[:, sl] + b_ref[:, sl])
            g = g_ref[:, sl].astype(F32)
            sg = _sigmoid(g)
            dy = dy_ref[:, sl].astype(F32)
            dg_ref[:, sl] = (dy * n * (sg * (1.0 + g * (1.0 - sg)))).astype(dg_ref.dtype)
            dn = dy * (g * sg)
            do_ref[:, sl] = (r * (dn - n * jnp.mean(dn * n, axis=-1, keepdims=True))).astype(do_ref.dtype)

    row = pl.BlockSpec((tr, 1024), lambda i: (i, 0))
    return pl.pallas_call(
        body, name="ret_finish_bwd", grid=(L // tr,),
        in_specs=[row, row, pl.BlockSpec((tr, 1024), lambda i: (i, C_RG // 1024)), row],
        out_specs=[row, row],
        out_shape=[jax.ShapeDtypeStruct((L, 1024), BF16), jax.ShapeDtypeStruct((L, 1024), BF16)],
        compiler_params=_cparams(("parallel",)),
    )(of, ob, P, dY)


def _ret_bwd(P, rope, tabs, stf, stb, dO, L, comm=()):
    n = L // CHUNK
    nb = n // RET_SUB

    def body(qf, kf, vf, rf, gf, sf, qb, kb, vb, rb, gb, sb, t_ref,
             dqf, dkf, dvf, dqb, dkb, dvb, ds0_ref, dlg_ref, ds):
        s = pl.program_id(0)

        @pl.when(s == 0)
        def _():
            ds[...] = jnp.zeros_like(ds)
            dlg_ref[...] = jnp.zeros_like(dlg_ref)

        for rnd in range(RET_SUB):
            units, pairs = [], []
            for d, (q_ref, k_ref, v_ref, r_ref, g_ref, s_ref, dq_ref, dk_ref, dv_ref) in enumerate(
                    ((qf, kf, vf, rf, gf, sf, dqf, dkf, dvf), (qb, kb, vb, rb, gb, sb, dqb, dkb, dvb))):
                j = RET_SUB - 1 - rnd if d == 0 else rnd
                rows = slice(j * CHUNK, (j + 1) * CHUNK)
                cos, sin = r_ref[0, rows, :], r_ref[1, rows, :]
                for p in range(RET_HEADS // 2):
                    qp = _rope(q_ref[rows, p * LANES:(p + 1) * LANES].astype(F32), cos, sin, 32)
                    kp = _rope(k_ref[rows, p * LANES:(p + 1) * LANES].astype(F32), cos, sin, 32) * K_SCALE
                    pair = dict(p=p, rows=rows, cos=cos, sin=sin, dq_ref=dq_ref, dk_ref=dk_ref, us=[])
                    pairs.append(pair)
                    for a in range(2):
                        h = 2 * p + a
                        r = d * RET_HEADS + h
                        hm = _half_mask(qp.shape, a)
                        zero = jnp.zeros((RET_DK, RET_DV), F32)
                        sp = s_ref[j, h]
                        u = dict(r=r, h=h, rows=rows, dv_ref=dv_ref,
                                 qh=jnp.where(hm, qp, 0.0), kh=jnp.where(hm, kp, 0.0),
                                 vh=v_ref[rows, h * RET_DV:(h + 1) * RET_DV],
                                 gh=g_ref[rows, h * RET_DV:(h + 1) * RET_DV],
                                 sp=jnp.concatenate([sp, zero] if a == 0 else [zero, sp], axis=0),
                                 dsn=ds[r])
                        u["qhb"], u["khb"] = u["qh"].astype(BF16), u["kh"].astype(BF16)
                        units.append(u)
                        pair["us"].append(u)
            for u in units:
                u["am"] = _dot(u["qhb"], u["khb"], NT)
                u["dar"] = _dot(u["gh"], u["vh"], NT)
                u["xq"] = _dot(u["gh"], u["sp"].astype(BF16), NT)
                u["yk"] = _dot(u["vh"], u["dsn"].astype(BF16), NT)
            for u in units:
                r = u["r"]
                dm = t_ref[r, 0]
                u["da"] = (u["dar"] * dm).astype(BF16)
                u["amd"] = (u["am"] * dm).astype(BF16)
                part = (jnp.sum(u["am"] * u["dar"] * t_ref[r, 1]) + jnp.sum(u["qh"] * t_ref[r, 3] * u["xq"])
                        + jnp.sum(u["kh"] * t_ref[r, 5] * u["yk"])
                        + float(CHUNK) * jnp.sum(t_ref[r, 6] * u["dsn"] * u["sp"]))
                dlg_ref[r:r + 1, :] += jnp.zeros((1, LANES), F32) + part
            for u in units:
                r, h = u["r"], u["h"]
                u["dq"] = _dot(u["da"], u["khb"], NN) + u["xq"] * t_ref[r, 2]
                u["dk"] = _dot(u["da"], u["qhb"], TN) + u["yk"] * t_ref[r, 4]
                u["dv_ref"][u["rows"], h * RET_DV:(h + 1) * RET_DV] = (
                    _dot(u["amd"], u["gh"], TN)
                    + _dot((u["kh"] * t_ref[r, 4]).astype(BF16), u["dsn"].astype(BF16), NN)
                ).astype(u["dv_ref"].dtype)
                ds[r] = t_ref[r, 6] * u["dsn"] + _dot((u["qh"] * t_ref[r, 2]).astype(BF16), u["gh"], TN)
            for pair in pairs:
                sl = slice(pair["p"] * LANES, (pair["p"] + 1) * LANES)
                u0, u1 = pair["us"]
                pair["dq_ref"][pair["rows"], sl] = _rope_t(
                    u0["dq"] + u1["dq"], pair["cos"], pair["sin"], 32).astype(BF16)
                pair["dk_ref"][pair["rows"], sl] = _rope_t(
                    (u0["dk"] + u1["dk"]) * K_SCALE, pair["cos"], pair["sin"], 32).astype(BF16)

        @pl.when(s == nb - 1)
        def _():
            ds0_ref[...] = ds[...]

    fw = lambda s: nb - 1 - s
    bw = lambda s: s
    RB = RET_SUB * CHUNK

    def specs(cm):
        return [pl.BlockSpec((RB, 512), lambda s: (cm(s), C_RQ // 512)),
                pl.BlockSpec((RB, 512), lambda s: (cm(s), C_RK // 512)),
                pl.BlockSpec((RB, 1024), lambda s: (cm(s), C_RV // 1024)),
                pl.BlockSpec((2, RB, LANES), lambda s: (0, cm(s), 0)),
                pl.BlockSpec((RB, 1024), lambda s: (cm(s), 0)),
                pl.BlockSpec((RET_SUB, RET_HEADS, RET_DK, RET_DV), lambda s: (cm(s), 0, 0, 0))]

    def ospecs(cm):
        return [pl.BlockSpec((RB, 512), lambda s: (cm(s), 0)), pl.BlockSpec((RB, 512), lambda s: (cm(s), 0)),
                pl.BlockSpec((RB, 1024), lambda s: (cm(s), 0))]

    oshape = [jax.ShapeDtypeStruct((L, 512), BF16), jax.ShapeDtypeStruct((L, 512), BF16),
              jax.ShapeDtypeStruct((L, 1024), BF16)]
    full = lambda shp: pl.BlockSpec(shp, lambda s: (0,) * len(shp))
    return _call(
        body, "ret_bwd", (nb,),
        specs(fw) + specs(bw) + [full((2 * RET_HEADS, N_TAB, CHUNK, CHUNK))],
        ospecs(fw) + ospecs(bw) + [full((2 * RET_HEADS, LANES, RET_DV)), full((2 * RET_HEADS, LANES))],
        oshape + oshape + [jax.ShapeDtypeStruct((2 * RET_HEADS, LANES, RET_DV), F32),
                           jax.ShapeDtypeStruct((2 * RET_HEADS, LANES), F32)],
        [pltpu.VMEM((2 * RET_HEADS, LANES, RET_DV), F32)],
        ("arbitrary",), (P, P, P, rope, dO, stf, P, P, P, rope, dO, stb, tabs), comm)


def _ret_ctx_bwd(P, ctab, ds0, dlg, rdb, L, Lc):
    cb = L // Lc

    def body(k_ref, v_ref, c_ref, ds_ref, dlg_ref, rd_ref, dk_ref, dv_ref, drd_ref):
        for p in range(RET_HEADS // 2):
            kp = k_ref[:, p * LANES:(p + 1) * LANES].astype(F32) * K_SCALE
            dkp = jnp.zeros((Lc, LANES), F32)
            for a in range(2):
                h = 2 * p + a
                kh = jnp.where(_half_mask(kp.shape, a), kp, 0.0)
                vh = v_ref[:, h * RET_DV:(h + 1) * RET_DV]
                dvh = jnp.zeros((Lc, RET_DV), F32)
                for d in range(2):
                    r = d * RET_HEADS + h
                    dsb = ds_ref[r].astype(BF16)
                    cw, cwc = c_ref[r, 0], c_ref[r, 1]
                    y = _dot(vh, dsb, NT)
                    dkp += y * cw
                    dvh += _dot((kh * cw).astype(BF16), dsb, NN)
                    lg = -jnp.exp(rd_ref[r])
                    drd_ref[r:r + 1, :] = (dlg_ref[r:r + 1, :] + jnp.sum(kh * cwc * y)) * lg
                dv_ref[:, h * RET_DV:(h + 1) * RET_DV] = dvh
            dk_ref[:, p * LANES:(p + 1) * LANES] = dkp * K_SCALE

    full = lambda shp: pl.BlockSpec(shp, lambda i: (0,) * len(shp))
    return pl.pallas_call(
        body, name="ret_ctx_bwd", grid=(1,),
        in_specs=[pl.BlockSpec((Lc, 512), lambda i: (cb, C_RK // 512)),
                  pl.BlockSpec((Lc, 1024), lambda i: (cb, C_RV // 1024)),
                  full((2 * RET_HEADS, 2, Lc, LANES)), full((2 * RET_HEADS, LANES, RET_DV)),
                  full((2 * RET_HEADS, LANES)), full((2 * RET_HEADS, 1, LANES))],
        out_specs=[full((Lc, 512)), full((Lc, 1024)), full((2 * RET_HEADS, LANES))],
        out_shape=[jax.ShapeDtypeStruct((Lc, 512), F32), jax.ShapeDtypeStruct((Lc, 1024), F32),
                   jax.ShapeDtypeStruct((2 * RET_HEADS, LANES), F32)],
        compiler_params=_cparams(("arbitrary",)),
    )(P, P, ctab, ds0, dlg, rdb)


BLK = 128
N_LOC = 3 * BLK


def _att_inputs(P, rope, L, Lc):
    n = L // BLK
    cb = L // Lc
    prev = lambda i: jnp.maximum(i - 1, 0)
    nxt = lambda i: jnp.minimum(i + 1, n - 1)
    specs = [pl.BlockSpec((BLK, 1024), lambda i: (i, C_AQ // 1024))]
    args = [P]
    for col in (C_AK // 256, C_AV // 256):
        for rm in (prev, lambda i: i, nxt):
            specs.append(pl.BlockSpec((BLK, 256), functools.partial(lambda i, rm, col: (rm(i), col), rm=rm, col=col)))
            args.append(P)
        specs.append(pl.BlockSpec((Lc, 256), functools.partial(lambda i, col: (cb, col), col=col)))
        args.append(P)
    for rm in (prev, lambda i: i, nxt):
        specs.append(pl.BlockSpec((2, BLK, LANES), functools.partial(lambda i, rm: (0, rm(i), 0), rm=rm)))
        args.append(rope)
    return specs, args


def _att_prep(i, n, refs, Lc):
    q_ref, kp_ref, kc_ref, kn_ref, kx_ref, vp_ref, vc_ref, vn_ref, vx_ref, rp_ref, rc_ref, rn_ref = refs
    cos = jnp.concatenate([rp_ref[0], rc_ref[0], rn_ref[0]], axis=0)
    sin = jnp.concatenate([rp_ref[1], rc_ref[1], rn_ref[1]], axis=0)
    kd, vd = [], []
    for t in range(ATT_KV // 2):
        sl = slice(t * LANES, (t + 1) * LANES)
        kl = jnp.concatenate([kp_ref[:, sl], kc_ref[:, sl], kn_ref[:, sl]], axis=0).astype(F32)
        kl = _rope(kl, cos, sin, 16)
        ka = jnp.concatenate([kl, kx_ref[:, sl].astype(F32)], axis=0)
        va = jnp.concatenate([vp_ref[:, sl], vc_ref[:, sl], vn_ref[:, sl], vx_ref[:, sl]], axis=0).astype(F32)
        kr, vr = pltpu.roll(ka, 64, 1), pltpu.roll(va, 64, 1)
        for b in range(2):
            hm = _half_mask(ka.shape, b)
            kd.append(jnp.where(hm, ka, kr).astype(BF16))
            vd.append(jnp.where(hm, va, vr).astype(BF16))
    nk = N_LOC + Lc
    rr = lax.broadcasted_iota(jnp.int32, (BLK, nk), 0)
    ss = lax.broadcasted_iota(jnp.int32, (BLK, nk), 1)
    lo = jnp.where(i == 0, BLK, 0)
    hi = jnp.where(i == n - 1, 2 * BLK, N_LOC)
    valid = (ss >= N_LOC) | ((ss >= rr) & (ss <= rr + 2 * BLK) & (ss >= lo) & (ss < hi))
    bias = jnp.where(valid, 0.0, NEG)
    return kd, vd, jnp.concatenate([bias] * 4, axis=0), rc_ref[0], rc_ref[1]


LOG2E = 1.4426950408889634
LN2 = 0.6931471805599453
Q_SCALE = A_SCALE * LOG2E


def _stack4(ref, g, f=None):
    parts = []
    for jp in range(2):
        t = ref[:, (2 * g + jp) * LANES:(2 * g + jp + 1) * LANES].astype(F32)
        if f is not None:
            t = f(t)
        for a in range(2):
            parts.append(jnp.where(_half_mask(t.shape, a), t, 0.0))
    return jnp.concatenate(parts, axis=0)


def _unstack4(x4, jp):
    r0 = 2 * jp * BLK
    lo = x4[r0:r0 + BLK]
    hi = x4[r0 + BLK:r0 + 2 * BLK]
    return jnp.where(_half_mask(lo.shape, 0), lo, hi)


def _softmax_parts(s, bias4, sink_ref, g):
    sink_col = LOG2E * jnp.concatenate(
        [jnp.zeros((BLK, 1), F32) + sink_ref[4 * g + r:4 * g + r + 1, 0:1] for r in range(4)], axis=0)
    s = s + bias4
    m = jnp.maximum(jnp.max(s, axis=-1, keepdims=True), sink_col)
    e = jnp.exp2(s - m)
    es = jnp.exp2(sink_col - m)
    return e, es, jnp.sum(e, axis=-1, keepdims=True) + es


def _att_fwd(P, rope, sinkb, Y, L, Lc, comm=()):
    n = L // BLK
    specs, args = _att_inputs(P, rope, L, Lc)

    def body(*refs):
        sink_ref, o_ref = refs[12], refs[14]
        i = pl.program_id(0)
        kd, vd, bias4, cq, sq = _att_prep(i, n, refs[:12], Lc)
        for g in range(ATT_KV):
            q4 = _stack4(refs[0], g, lambda t: _rope(t, cq, sq, 16) * Q_SCALE).astype(BF16)
            e, _, l = _softmax_parts(_dot(q4, kd[g], NT), bias4, sink_ref, g)
            o4 = _dot(e.astype(BF16), vd[g], NN) * (1.0 / l)
            for jp in range(2):
                c0 = (2 * g + jp) * LANES
                o_ref[:, c0:c0 + LANES] = _unstack4(o4, jp).astype(o_ref.dtype)

    return _call(
        body, "att_fwd", (n,),
        specs + [pl.BlockSpec((ATT_HEADS, LANES), lambda i: (0, 0)), pl.BlockSpec(memory_space=pl.ANY)],
        [pl.BlockSpec((BLK, 1024), lambda i: (i, 1))], [jax.ShapeDtypeStruct((L, 2048), BF16)], [],
        ("parallel",), (*args, sinkb, Y), comm, aliases={13: 0})


def _att_bwd(P, rope, sinkb, Y, dY, L, Lc, comm=()):
    n = L // BLK
    specs, args = _att_inputs(P, rope, L, Lc)
    nk = N_LOC + Lc

    def body(*refs):
        sink_ref, y_ref, dy_ref = refs[12], refs[13], refs[14]
        dq_ref, dkl_ref, dvl_ref, dkx_ref, dvx_ref, dsk_ref = refs[15:21]
        i = pl.program_id(0)

        @pl.when(i == 0)
        def _():
            dkx_ref[...] = jnp.zeros_like(dkx_ref)
            dvx_ref[...] = jnp.zeros_like(dvx_ref)
            dsk_ref[...] = jnp.zeros_like(dsk_ref)

        kd, vd, bias4, cq, sq = _att_prep(i, n, refs[:12], Lc)
        for t in range(ATT_KV // 2):
            dk_halves, dv_halves = [], []
            for b in range(2):
                g = 2 * t + b
                q4 = _stack4(refs[0], g, lambda x: _rope(x, cq, sq, 16) * Q_SCALE).astype(BF16)
                do4 = _stack4(dy_ref, g)
                delta = jnp.sum(do4 * _stack4(y_ref, g), axis=-1, keepdims=True)
                do4b = do4.astype(BF16)
                e, es, l = _softmax_parts(_dot(q4, kd[g], NT), bias4, sink_ref, g)
                inv = 1.0 / l
                p = e * inv
                dsc = (p * (_dot(do4b, vd[g], NT) - delta)).astype(BF16)
                dsr = es * inv * delta
                for r in range(4):
                    h = 4 * g + r
                    dsk_ref[h:h + 1, :] += jnp.zeros((1, LANES), F32) - jnp.sum(dsr[r * BLK:(r + 1) * BLK])
                dq4 = _dot(dsc, kd[g], NN) * A_SCALE
                for jp in range(2):
                    c0 = (2 * g + jp) * LANES
                    dq_ref[:, c0:c0 + LANES] = _rope_t(_unstack4(dq4, jp), cq, sq, 16).astype(dq_ref.dtype)
                dkd = _dot(q4, dsc, TN) * LN2
                dvd = _dot(do4b, p.astype(BF16), TN)
                dk_halves.append(dkd[:ATT_DH] + dkd[ATT_DH:])
                dv_halves.append(dvd[:ATT_DH] + dvd[ATT_DH:])
            dk_t = jnp.concatenate(dk_halves, axis=0).T
            dv_t = jnp.concatenate(dv_halves, axis=0).T
            sl = slice(t * LANES, (t + 1) * LANES)
            dkl_ref[0, :, sl] = dk_t[:N_LOC]
            dvl_ref[0, :, sl] = dv_t[:N_LOC]
            dkx_ref[:, sl] += dk_t[N_LOC:]
            dvx_ref[:, sl] += dv_t[N_LOC:]

    row = pl.BlockSpec((BLK, 1024), lambda i: (i, 0))
    loc = pl.BlockSpec((1, N_LOC, 256), lambda i: (i, 0, 0))
    cx = pl.BlockSpec((Lc, 256), lambda i: (0, 0))
    return _call(
        body, "att_bwd", (n,),
        specs + [pl.BlockSpec((ATT_HEADS, LANES), lambda i: (0, 0))] + [pl.BlockSpec((BLK, 1024), lambda i: (i, 1))] * 2,
        [row, loc, loc, cx, cx, pl.BlockSpec((ATT_HEADS, LANES), lambda i: (0, 0))],
        [jax.ShapeDtypeStruct((L, 1024), BF16), jax.ShapeDtypeStruct((n, N_LOC, 256), F32),
         jax.ShapeDtypeStruct((n, N_LOC, 256), F32), jax.ShapeDtypeStruct((Lc, 256), F32),
         jax.ShapeDtypeStruct((Lc, 256), F32), jax.ShapeDtypeStruct((ATT_HEADS, LANES), F32)], [],
        ("arbitrary",), (*args, sinkb, Y, dY), comm)


def _assemble_dp(L, Lc, dqf, dqb, dkf, dkb, dvf, dvb, drg, daq, dkl, dvl, rope_att, dck, dcv, dkx, dvx):
    n = L // BLK
    nc = Lc // BLK

    def body(dqf_r, dqb_r, dkf_r, dkb_r, dvf_r, dvb_r, drg_r, daq_r, kl0, kl1, kl2, vl0, vl1, vl2, rp_r,
             dck_r, dcv_r, dkx_r, dvx_r, o_ref):
        i = pl.program_id(0)

        @pl.when(i < n)
        def _():
            add = lambda a, b: (a[...].astype(F32) + b[...].astype(F32)).astype(o_ref.dtype)
            o_ref[:, C_RQ:C_RK] = add(dqf_r, dqb_r)
            o_ref[:, C_RK:C_RV] = add(dkf_r, dkb_r)
            o_ref[:, C_RV:C_RG] = add(dvf_r, dvb_r)
            o_ref[:, C_RG:C_AQ] = drg_r[...].astype(o_ref.dtype)
            o_ref[:, C_AQ:C_AK] = daq_r[...].astype(o_ref.dtype)
            w0 = jnp.where(i > 0, 1.0, 0.0)
            w2 = jnp.where(i < n - 1, 1.0, 0.0)
            dk = kl0[0] * w0 + kl1[0] + kl2[0] * w2
            dv = vl0[0] * w0 + vl1[0] + vl2[0] * w2
            for t in range(ATT_KV // 2):
                sl = slice(t * LANES, (t + 1) * LANES)
                o_ref[:, C_AK + t * LANES:C_AK + (t + 1) * LANES] = _rope_t(
                    dk[:, sl], rp_r[0], rp_r[1], 16).astype(o_ref.dtype)
            o_ref[:, C_AV:D_PROJ] = dv.astype(o_ref.dtype)

        @pl.when(i >= n)
        def _():
            o_ref[:, C_RQ:C_RK] = jnp.zeros((BLK, C_RK - C_RQ), o_ref.dtype)
            o_ref[:, C_RK:C_RV] = dck_r[...].astype(o_ref.dtype)
            o_ref[:, C_RV:C_RG] = dcv_r[...].astype(o_ref.dtype)
            o_ref[:, C_RG:C_AK] = jnp.zeros((BLK, C_AK - C_RG), o_ref.dtype)
            o_ref[:, C_AK:C_AV] = dkx_r[...].astype(o_ref.dtype)
            o_ref[:, C_AV:D_PROJ] = dvx_r[...].astype(o_ref.dtype)

    xm = lambda i: jnp.minimum(i, n - 1)
    cm = lambda i: jnp.clip(i - n, 0, nc - 1)
    r512 = pl.BlockSpec((BLK, 512), lambda i: (xm(i), 0))
    r1024 = pl.BlockSpec((BLK, 1024), lambda i: (xm(i), 0))
    part = lambda off: pl.BlockSpec((1, BLK, 256), lambda i: (jnp.clip(xm(i) + off, 0, n - 1), 1 - off, 0))
    return pl.pallas_call(
        body, name="assemble_dp", grid=(n + nc,),
        in_specs=[r512, r512, r512, r512, r1024, r1024, r1024, r1024,
                  part(-1), part(0), part(1), part(-1), part(0), part(1),
                  pl.BlockSpec((2, BLK, LANES), lambda i: (0, xm(i), 0)),
                  pl.BlockSpec((BLK, 512), lambda i: (cm(i), 0)), pl.BlockSpec((BLK, 1024), lambda i: (cm(i), 0)),
                  pl.BlockSpec((BLK, 256), lambda i: (cm(i), 0)), pl.BlockSpec((BLK, 256), lambda i: (cm(i), 0))],
        out_specs=pl.BlockSpec((BLK, D_PROJ), lambda i: (i, 0)),
        out_shape=jax.ShapeDtypeStruct((L + Lc, D_PROJ), BF16),
        compiler_params=_cparams(("parallel",)),
    )(dqf, dqb, dkf, dkb, dvf, dvb, drg, daq, dkl, dkl, dkl, dvl, dvl, dvl, rope_att, dck, dcv, dkx, dvx)


def _adam_math(w, g, m, v):
    m = ADAM_B1 * m + (1.0 - ADAM_B1) * g
    v = ADAM_B2 * v + (1.0 - ADAM_B2) * (g * g)
    m_hat = m / (1.0 - ADAM_B1 ** ADAM_STEP)
    v_hat = v / (1.0 - ADAM_B2 ** ADAM_STEP)
    delta = -ADAM_LR * (m_hat / (jnp.sqrt(v_hat) + ADAM_EPS) + ADAM_WD * w)
    return delta, m, v


def _adam(name, w, m, v, g=None, parts=None):
    R, C = w.shape
    tr = _pick(R, (256, 128, 64, 32, 16, 8))
    summed = parts is not None
    n_parts = parts.shape[0] if summed else 0

    def body(w_ref, m_ref, v_ref, g_ref, go_ref, d_ref, mo_ref, vo_ref):
        if summed:
            gv = g_ref[0].astype(F32)
            for j in range(1, n_parts):
                gv = gv + g_ref[j].astype(F32)
        else:
            gv = g_ref[...]
        d, mn, vn = _adam_math(w_ref[...], gv, m_ref[...], v_ref[...])
        go_ref[...] = gv
        d_ref[...] = d
        mo_ref[...] = mn
        vo_ref[...] = vn

    row = pl.BlockSpec((tr, C), lambda i: (i, 0))
    gspec = pl.BlockSpec((n_parts, tr, C), lambda i: (0, i, 0)) if summed else row
    return pl.pallas_call(
        body, name=name, grid=(R // tr,),
        in_specs=[row, row, row, gspec], out_specs=[row] * 4,
        out_shape=[jax.ShapeDtypeStruct((R, C), F32)] * 4,
        compiler_params=_cparams(("parallel",)),
    )(w, m, v, parts if summed else g)


def _rows_full(g):
    _, R, D = g.shape
    return g.reshape(N_DEV * R, D)


def _rows_slots(g):
    N, D = g.shape
    return g.reshape(N_DEV, N // N_DEV, D)


def _pad_rows(a, rows):
    return jnp.concatenate([a, jnp.zeros((rows - a.shape[0],) + a.shape[1:], a.dtype)], axis=0)


def kernel(x, c, ctx, c_ctx, w_mod, b_mod, norm_mix, norm_ffn, w_in, ret_decay, attn_sink, w_out, w_gate, w_up, w_down, norm_final, loss_target, m_c_ctx, m_w_mod, m_b_mod, m_norm_mix, m_norm_ffn, m_w_in, m_ret_decay, m_attn_sink, m_w_out, m_w_gate, m_w_up, m_w_down, m_norm_final, v_c_ctx, v_w_mod, v_b_mod, v_norm_mix, v_norm_ffn, v_w_in, v_ret_decay, v_attn_sink, v_w_out, v_w_gate, v_w_up, v_w_down, v_norm_final):
    L, D = x.shape[1], x.shape[2]
    Lc = ctx.shape[1]
    DF = w_gate.shape[2] * N_DEV
    C6 = w_mod.shape[2]
    me = _my_id()
    xs, cx, tgt = x[0], ctx[0], loss_target[0]

    ag_in = ("ag2", w_in[0].T.astype(BF16))
    ag_out, ag_gate = ("ag2", w_out[0].astype(BF16)), ("ag2", w_gate[0].T.astype(BF16))
    ag_up, ag_down = ("ag2", w_up[0].T.astype(BF16)), ("ag2", w_down[0].astype(BF16))

    cs = _allgather(c, "ag_c")[:, 0, :]
    s_in = _pad_rows(jnp.concatenate([cs, c_ctx[None, :]], axis=0), 16)
    b_l = lax.dynamic_slice_in_dim(b_mod, me * C6, C6, axis=1)
    mod_parts = _allgather(_mod_fwd(s_in, w_mod[0], b_l), "ag_mod")
    mod = _pad_rows(lax.dynamic_index_in_dim(mod_parts, me, axis=1, keepdims=False).reshape(6, D), 8)
    modc = _pad_rows(mod_parts[:, N_DEV, :].reshape(6, D), 8)
    mix_mod, ffn_mod = mod, jnp.roll(mod, -3, axis=0)
    gt_m, gt_f = mod[2:3], mod[5:6]

    rope_ret, rope_att = _rope_tables(L)
    rdb = jnp.broadcast_to(ret_decay[0].reshape(2 * RET_HEADS, 1, 1), (2 * RET_HEADS, 1, LANES))
    sinkb = jnp.broadcast_to(attn_sink[0].reshape(ATT_HEADS, 1), (ATT_HEADS, LANES))

    tm = _pick(L + Lc, (1408, 768, 512, 384, 256, 128))
    tmx = _pick(L, (1024, 512, 256, 128))

    (H,), (g_in,) = _modulate_fwd("mod_mix_fwd", xs, cx, norm_mix, mix_mod, modc, comm=[ag_in])
    W_inT = _rows_full(g_in)
    ident = lambda a, e: a
    tP, tD, tF = _pick(D_PROJ, (1152, 768, 512)), _pick(D, (2048, 1024, 512)), _pick(DF, (512, 256, 128))
    (P,), (g_gate,) = _matmul("mm_in", [(H, W_inT, 0)], 1, L + Lc, D_PROJ, D, "nt",
                              (tm, _pick(D_PROJ, (1536, 768, 512)), D), [], [BF16], ident,
                              comm=[ag_gate])
    W_gateT = _rows_full(g_gate)
    tabs, ctab = _ret_tables(rdb, Lc)
    s0 = _ret_ctx_state(P, ctab, L, Lc)
    (o_f, o_b, st_f, st_b), (g_out,) = _ret_fwd(P, rope_ret, tabs, s0, L, comm=[ag_out])
    W_out = _rows_full(g_out)
    Y_half = _ret_finish_fwd(o_f, o_b, P, L)
    (Y,), (g_up,) = _att_fwd(P, rope_att, sinkb, Y_half, L, Lc, comm=[ag_up])
    W_upT = _rows_full(g_up)
    KO = Y.shape[1]
    f_mix = _matmul("mm_out", [(Y, W_out, 0)], 1, L, D, KO, "nn", (tmx, tD, KO), [], [BF16], ident)[0]

    x1, H2 = _residual_modulate_fwd("mod_ffn_fwd", xs, f_mix, gt_m, norm_ffn, ffn_mod)

    def swiglu_epi(a, e):
        sg = _sigmoid(a[0])
        act = a[0] * sg
        return [act, a[1] * (sg * (1.0 + a[0] * (1.0 - sg))), act * a[1]]

    tm2 = tmx
    (act, up_dact, hmid), (g_down,) = _matmul("mm_gate_up", [(H2, W_gateT, 0), (H2, W_upT, 1)], 2, L, DF, D, "nt",
                                              (tm2, tF, D), [], [BF16, BF16, BF16], swiglu_epi, comm=[ag_down])
    W_down = _rows_full(g_down)
    f_ffn = _matmul("mm_down", [(hmid, W_down, 0)], 1, L, D, DF, "nn", (tm2, tD, _pick(DF, (1408, 512, 256, 128))),
                    [], [BF16], ident)[0]

    dx2, dFf, sums_l = _loss_head(x1, tgt, norm_final.reshape(1, D), f_ffn, gt_f)

    def dswiglu_epi(a, e):
        return [a[0] * e[0].astype(F32), a[0] * e[1].astype(F32)]

    dga, dup = _matmul("mm_d_down", [(dFf, W_down, 0)], 1, L, DF, D, "nt", (tm2, tF, D),
                       [(up_dact, "mn"), (act, "mn")], [BF16, BF16], dswiglu_epi)
    tkt, tkl = _pick(L, (512, 256, 128)), _pick(L, (1024, 512, 256, 128))
    dW_down =_matmul("mm_gw_down", [(hmid, dFf, 0)], 1, DF, D, L, "tn",
                      (_pick(DF, (1408, 512, 256, 128)), tD, tkt), [], [BF16], ident)[0]
    (dW_gateT, dW_upT), (p_down,) = _matmul("mm_gw_gate_up", [(dga, H2, 0), (dup, H2, 1)], 2, DF, D, L, "tn",
                                            (tF, tD, tkl), [], [BF16, BF16], ident,
                                            comm=[("a2a", _rows_slots(dW_down))])
    (dH2,), (p_gate,) = _matmul("mm_d_gate_up", [(dga, W_gateT, 0), (dup, W_upT, 0)], 1, L, D, DF, "nn",
                                (tm2, tD, tF), [], [BF16], ident, comm=[("a2a", _rows_slots(dW_gateT))])
    dx1, dFm, sums_f = _modulate_bwd("mod_ffn_bwd", x1, None, dH2, norm_ffn, ffn_mod, None, dx2, f_mix, gt_m)

    tO = _pick(KO, (2048, 1024, 512))
    dY = _matmul("mm_d_out", [(dFm, W_out, 0)], 1, L, KO, D, "nt", (tmx, tO, D), [], [BF16], ident)[0]
    dW_out = _matmul("mm_gw_out", [(Y, dFm, 0)], 1, KO, D, L, "tn",
                     (_pick(KO, (1024, 512)), tD, _pick(L, (2048, 1024, 512, 256, 128))), [], [BF16],
                     ident)[0]
    dO, drg = _ret_finish_bwd(o_f, o_b, P, dY, L)
    (dqf, dkf, dvf, dqb, dkb, dvb, ds0, dlg), (p_out,) = _ret_bwd(
        P, rope_ret, tabs, st_f, st_b, dO, L, comm=[("a2a", _rows_slots(dW_out))])
    dck, dcv, d_rd = _ret_ctx_bwd(P, ctab, ds0, dlg, rdb, L, Lc)
    (daq, dkl, dvl, dkx, dvx, d_sink), (p_up,) = _att_bwd(
        P, rope_att, sinkb, Y, dY, L, Lc, comm=[("a2a", _rows_slots(dW_upT))])
    dP = _assemble_dp(L, Lc, dqf, dqb, dkf, dkb, dvf, dvb, drg, daq, dkl, dvl, rope_att, dck, dcv, dkx, dvx)
    tkc = _pick(L + Lc, (768, 256, 128))
    dW_inT = _matmul("mm_gw_in", [(dP, H, 0)], 1, D_PROJ, D, L + Lc, "tn", (tP, tD, tkc), [], [BF16], ident)[0]
    (dH,), (p_in,) = _matmul("mm_d_in", [(dP, W_inT, 0)], 1, L + Lc, D, D_PROJ, "nn",
                             (tm, tD, _pick(D_PROJ, (768, 512, 256))), [], [BF16], ident,
                             comm=[("a2a", _rows_slots(dW_inT))])
    grad_x, sums_m = _modulate_bwd("mod_mix_bwd", xs, cx, dH, norm_mix, mix_mod, modc, dx1, None, None)

    zero = jnp.zeros((1, D), F32)
    dmod = jnp.concatenate([sums_m[0:1], sums_m[1:2], sums_f[6:7], sums_f[0:1], sums_f[1:2], sums_l[2:3]], axis=1)
    dmodc = jnp.concatenate([sums_m[3:4], sums_m[4:5], zero, zero, zero, zero], axis=1)
    dm_all = _allgather(jnp.concatenate([dmod, dmodc], axis=0), "ag_dmod")
    dm_cols = lax.dynamic_slice_in_dim(dm_all, me * C6, C6, axis=2)
    dm_in = jnp.concatenate([dm_cols[:, 0, :], dm_cols[:, 1, :]], axis=0)
    s_bwd = jnp.concatenate([cs, jnp.broadcast_to(c_ctx[None, :], (N_DEV, D))], axis=0)
    g_w_mod, dsil = _mod_bwd(s_bwd, dm_in, w_mod[0])

    lane_pad = lambda a: _pad_rows(a.reshape(-1, 1), LANES).reshape(1, LANES)
    pack = jnp.concatenate([dsil[0:1], sums_m[2:3], sums_f[2:3], sums_l[1:2],
                            lane_pad(d_rd[:, 0]), lane_pad(d_sink[:, 0]), sums_l[3:4, 0:LANES]], axis=1)
    packs = _allgather(pack, "ag_small")
    zl = jnp.zeros((1, LANES), F32)

    def pack_w(a_c, a_nm, a_nf, a_fin, a_rd, a_sk):
        return jnp.concatenate([a_c.reshape(1, D), a_nm, a_nf, a_fin.reshape(1, D), lane_pad(a_rd.reshape(-1)),
                                lane_pad(a_sk.reshape(-1)), zl], axis=1)

    sg, sd, sm, sv = _adam("adam_small", pack_w(c_ctx, norm_mix, norm_ffn, norm_final, ret_decay, attn_sink),
                           pack_w(m_c_ctx, m_norm_mix, m_norm_ffn, m_norm_final, m_ret_decay, m_attn_sink),
                           pack_w(v_c_ctx, v_norm_mix, v_norm_ffn, v_norm_final, v_ret_decay, v_attn_sink),
                           parts=packs)
    loss = sg[0, 4 * D + 2 * LANES]

    def unpack(a):
        return (a[0, 0:D], a[:, D:2 * D], a[:, 2 * D:3 * D], a[0, 3 * D:4 * D],
                a[0, 4 * D:4 * D + 2 * RET_HEADS].reshape(1, 2, RET_HEADS),
                a[:, 4 * D + LANES:4 * D + LANES + ATT_HEADS])

    bg, bd, bm, bv = _adam("adam_b_mod", b_mod, m_b_mod, v_b_mod, parts=dm_all.reshape(2 * N_DEV, 1, 6 * D))
    wg, wd, wm, wv = _adam("adam_w_mod", w_mod[0], m_w_mod[0], v_w_mod[0], g=g_w_mod)

    big = {}
    for nm, w, m, v, parts, transposed in (
            ("w_in", w_in, m_w_in, v_w_in, p_in, True), ("w_out", w_out, m_w_out, v_w_out, p_out, False),
            ("w_gate", w_gate, m_w_gate, v_w_gate, p_gate, True), ("w_up", w_up, m_w_up, v_w_up, p_up, True),
            ("w_down", w_down, m_w_down, v_w_down, p_down, False)):
        if transposed:
            res = [a.T for a in _adam("adam_" + nm, w[0].T, m[0].T, v[0].T, parts=parts)]
        else:
            res = _adam("adam_" + nm, w[0], m[0], v[0], parts=parts)
        big[nm] = [a[None] for a in res]

    g_s, d_s, m_s, v_s = unpack(sg), unpack(sd), unpack(sm), unpack(sv)

    def leaves(k, small, bmod, wmod):
        return (small[0], wmod[None], bmod, small[1], small[2], big["w_in"][k], small[4], small[5],
                big["w_out"][k], big["w_gate"][k], big["w_up"][k], big["w_down"][k], small[3])

    return (loss, grad_x[None], *leaves(0, g_s, bg, wg), *leaves(1, d_s, bd, wd),
            *leaves(2, m_s, bm, wm), *leaves(3, v_s, bv, wv))
```

```python
import functools

import jax
import jax.numpy as jnp
from jax import lax
from jax.experimental import pallas as pl
from jax.experimental.pallas import tpu as pltpu

F32 = jnp.float32
BF16 = jnp.bfloat16

N_DEV = 8
LANES = 128
RET_HEADS = 8
RET_DK = 64
RET_DV = 128
CHUNK = 128
ATT_HEADS = 16
ATT_KV = 4
ATT_DH = 64
GRID_W = 64
ROPE_BASE = 10000.0
EPS = 1e-6
NEG = -1e30
C_RQ, C_RK, C_RV, C_RG, C_AQ, C_AK, C_AV, D_PROJ = 0, 512, 1024, 2048, 3072, 4096, 4352, 4608
K_SCALE = RET_DK ** -0.5
A_SCALE = ATT_DH ** -0.5

ADAM_LR, ADAM_B1, ADAM_B2, ADAM_EPS, ADAM_WD, ADAM_STEP = 0.001, 0.9, 0.999, 1e-08, 0.01, 10

VMEM_BIG = 52 * 1024 * 1024

NN = (((1,), (0,)), ((), ()))
NT = (((1,), (1,)), ((), ()))
TN = (((0,), (0,)), ((), ()))


def _dot(a, b, dims):
    return lax.dot_general(a, b, dims, preferred_element_type=F32)


def _cparams(sem, vmem=VMEM_BIG):
    return pltpu.CompilerParams(dimension_semantics=sem, vmem_limit_bytes=vmem)


def _pick(dim, prefs):
    for p in prefs:
        if dim % p == 0:
            return p
    return dim


def _my_id():
    return lax.axis_index("x") * 4 + lax.axis_index("y") * 2 + lax.axis_index("c")


def _sigmoid(x):
    return 0.5 * jnp.tanh(0.5 * x) + 0.5


def _peers():
    mx, my, mc = lax.axis_index("x"), lax.axis_index("y"), lax.axis_index("c")
    out = []
    for k in range(1, N_DEV):
        kx, ky, kc = (k >> 2) & 1, (k >> 1) & 1, k & 1
        px = 1 - mx if kx else mx
        py = 1 - my if ky else my
        pc = 1 - mc if kc else mc
        out.append(((px, py, pc), px * 4 + py * 2 + pc))
    return out


def _exchange_copies(kind, x_ref, o_ref, ssem, rsem, lsem):
    me = _my_id()
    loc = pltpu.make_async_copy(x_ref if kind == "ag" else x_ref.at[me], o_ref.at[me], lsem)
    cps = []
    for k, (peer, pid) in enumerate(_peers()):
        cps.append(pltpu.make_async_remote_copy(
            src_ref=x_ref if kind == "ag" else x_ref.at[pid], dst_ref=o_ref.at[me],
            send_sem=ssem.at[k], recv_sem=rsem.at[k], device_id=peer, device_id_type=pl.DeviceIdType.MESH))
    return loc, cps


def _two_level_copies(x_ref, o_ref, ssem, rsem, lsem):
    mx, my, mc = lax.axis_index("x"), lax.axis_index("y"), lax.axis_index("c")
    me = mx * 4 + my * 2 + mc
    sibling = (mx, my, 1 - mc)
    chips = [(1 - mx, my), (mx, 1 - my), (1 - mx, 1 - my)]

    def copy(k, slot, to, src=None):
        return pltpu.make_async_remote_copy(
            src_ref=o_ref.at[slot] if src is None else src, dst_ref=o_ref.at[slot],
            send_sem=ssem.at[k], recv_sem=rsem.at[k], device_id=to, device_id_type=pl.DeviceIdType.MESH)

    loc = pltpu.make_async_copy(x_ref, o_ref.at[me], lsem)
    first = [copy(0, me, sibling, src=x_ref)]
    first += [copy(1 + j, me, (cx, cy, mc), src=x_ref) for j, (cx, cy) in enumerate(chips)]
    passed = [copy(4 + j, cx * 4 + cy * 2 + mc, sibling) for j, (cx, cy) in enumerate(chips)]
    return loc, first, passed


def _exchange_start(kind, x_ref, o_ref, ssem, rsem, lsem):
    if kind == "ag2":
        loc, first, _ = _two_level_copies(x_ref, o_ref, ssem, rsem, lsem)
        cps = first
    else:
        loc, cps = _exchange_copies(kind, x_ref, o_ref, ssem, rsem, lsem)
    loc.start()
    for cp in cps:
        cp.start()


def _exchange_pass_on(kind, x_ref, o_ref, ssem, rsem, lsem):
    if kind == "ag2":
        _, first, passed = _two_level_copies(x_ref, o_ref, ssem, rsem, lsem)
        for j in range(3):
            first[1 + j].wait_recv()
            passed[j].start()


def _exchange_wait(kind, x_ref, o_ref, ssem, rsem, lsem):
    if kind == "ag2":
        loc, first, passed = _two_level_copies(x_ref, o_ref, ssem, rsem, lsem)
        first[0].wait_recv()
        for cp in passed:
            cp.wait_recv()
        cps = first + passed
    else:
        loc, cps = _exchange_copies(kind, x_ref, o_ref, ssem, rsem, lsem)
        for cp in cps:
            cp.wait_recv()
    for cp in cps:
        cp.wait_send()
    loc.wait()


_EXCHANGE_SEMS = [pltpu.SemaphoreType.DMA((N_DEV - 1,)), pltpu.SemaphoreType.DMA((N_DEV - 1,)),
                  pltpu.SemaphoreType.DMA(())]


def _exchange_shape(kind, x):
    return jax.ShapeDtypeStruct(x.shape if kind == "a2a" else (N_DEV,) + x.shape, x.dtype)


def _exchange(kind, x, name):
    def body(x_ref, o_ref, ssem, rsem, lsem):
        _exchange_start(kind, x_ref, o_ref, ssem, rsem, lsem)
        _exchange_pass_on(kind, x_ref, o_ref, ssem, rsem, lsem)
        _exchange_wait(kind, x_ref, o_ref, ssem, rsem, lsem)

    return pl.pallas_call(
        body, name=name, out_shape=_exchange_shape(kind, x),
        in_specs=[pl.BlockSpec(memory_space=pl.ANY)], out_specs=pl.BlockSpec(memory_space=pl.ANY),
        scratch_shapes=list(_EXCHANGE_SEMS),
    )(x)


def _allgather(x, name):
    return _exchange("ag", x, name)


def _call(body, name, grid, in_specs, out_specs, out_shape, scratch_shapes, sem, args, comm=(), aliases=None,
          pass_on_at=0.75):
    in_specs, out_specs, out_shape = list(in_specs), list(out_specs), list(out_shape)
    scratch_shapes = list(scratch_shapes)
    aliases = aliases or {}
    if not comm:
        outs = pl.pallas_call(body, name=name, grid=grid, in_specs=in_specs, out_specs=out_specs, out_shape=out_shape,
                              scratch_shapes=scratch_shapes, input_output_aliases=aliases,
                              compiler_params=_cparams(sem))(*args)
        return list(outs), []
    n_in, n_out, n_scr, n_c = len(in_specs), len(out_specs), len(scratch_shapes), len(comm)
    hbm = pl.BlockSpec(memory_space=pl.ANY)

    def wrapped(*refs):
        ins, cins = refs[:n_in], refs[n_in:n_in + n_c]
        outs = refs[n_in + n_c:n_in + n_c + n_out]
        couts = refs[n_in + n_c + n_out:n_in + 2 * n_c + n_out]
        scr = refs[n_in + 2 * n_c + n_out:n_in + 2 * n_c + n_out + n_scr]
        sems = refs[n_in + 2 * n_c + n_out + n_scr:]
        step, total = pl.program_id(0), grid[0]
        for ax in range(1, len(grid)):
            step = step * grid[ax] + pl.program_id(ax)
            total *= grid[ax]

        @pl.when(step == 0)
        def _():
            for c, (kind, _) in enumerate(comm):
                _exchange_start(kind, cins[c], couts[c], *sems[3 * c:3 * c + 3])

        body(*ins, *outs, *scr)

        @pl.when(step == min(total - 1, int(total * pass_on_at)))
        def _():
            for c, (kind, _) in enumerate(comm):
                _exchange_pass_on(kind, cins[c], couts[c], *sems[3 * c:3 * c + 3])

        @pl.when(step == total - 1)
        def _():
            for c, (kind, _) in enumerate(comm):
                _exchange_wait(kind, cins[c], couts[c], *sems[3 * c:3 * c + 3])

    res = pl.pallas_call(
        wrapped, name=name, grid=grid,
        in_specs=in_specs + [hbm] * n_c, out_specs=out_specs + [hbm] * n_c,
        out_shape=out_shape + [_exchange_shape(kind, arr) for kind, arr in comm],
        scratch_shapes=scratch_shapes + list(_EXCHANGE_SEMS) * n_c, input_output_aliases=aliases,
        compiler_params=_cparams(("arbitrary",) * len(grid)),
    )(*args, *[arr for _, arr in comm])
    return list(res[:n_out]), list(res[n_out:])


def _matmul(name, pairs, n_acc, M, N, K, mode, tiles, extras, out_dtypes, epilogue, j_outer=False, comm=()):
    tm, tn, tk = tiles
    gm, gn, nk = M // tm, N // tn, K // tk
    assert gm * tm == M and gn * tn == N and nk * tk == K, (name, M, N, K, tiles)
    if j_outer:
        grid = (gn, gm, nk)
        ij = lambda g0, g1: (g1, g0)
    else:
        grid = (gm, gn, nk)
        ij = lambda g0, g1: (g0, g1)

    if mode in ("nn", "nt"):
        a_spec = pl.BlockSpec((tm, tk), lambda g0, g1, k: (ij(g0, g1)[0], k))
    else:
        a_spec = pl.BlockSpec((tk, tm), lambda g0, g1, k: (k, ij(g0, g1)[0]))
    if mode == "nt":
        b_spec = pl.BlockSpec((tn, tk), lambda g0, g1, k: (ij(g0, g1)[1], k))
    else:
        b_spec = pl.BlockSpec((tk, tn), lambda g0, g1, k: (k, ij(g0, g1)[1]))
    dims = {"nn": NN, "nt": NT, "tn": TN}[mode]
    mn_spec = pl.BlockSpec((tm, tn), lambda g0, g1, k: ij(g0, g1))
    n_spec = pl.BlockSpec((1, tn), lambda g0, g1, k: (0, ij(g0, g1)[1]))

    in_specs, args = [], []
    for a, b, _ in pairs:
        in_specs += [a_spec, b_spec]
        args += [a, b]
    for arr, kind in extras:
        in_specs.append(mn_spec if kind == "mn" else n_spec)
        args.append(arr)
    n_p, n_e, n_o = len(pairs), len(extras), len(out_dtypes)

    def body(*refs):
        ab = refs[:2 * n_p]
        ex = refs[2 * n_p:2 * n_p + n_e]
        outs = refs[2 * n_p + n_e:2 * n_p + n_e + n_o]
        accs = refs[2 * n_p + n_e + n_o:]
        k = pl.program_id(2)

        def partial_sums():
            sums = [None] * n_acc
            for p, (_, _, ai) in enumerate(pairs):
                d = _dot(ab[2 * p][...], ab[2 * p + 1][...], dims)
                sums[ai] = d if sums[ai] is None else sums[ai] + d
            return sums

        def finish(acc_vals):
            res = epilogue(acc_vals, [e[...] for e in ex])
            for o, r in zip(outs, res):
                o[...] = r.astype(o.dtype)

        def accumulate(first):
            w = _pick(tm if mode == "tn" else tn, (512, 384, 256))
            for c in range((tm if mode == "tn" else tn) // w):
                sl = slice(c * w, (c + 1) * w)
                sums = [None] * n_acc
                for p, (_, _, ai) in enumerate(pairs):
                    a_ref, b_ref = ab[2 * p], ab[2 * p + 1]
                    if mode == "tn":
                        d = _dot(a_ref[:, sl], b_ref[...], dims)
                    elif mode == "nn":
                        d = _dot(a_ref[...], b_ref[:, sl], dims)
                    else:
                        d = _dot(a_ref[...], b_ref[sl, :], dims)
                    sums[ai] = d if sums[ai] is None else sums[ai] + d
                idx = (sl, slice(None)) if mode == "tn" else (slice(None), sl)
                for ai, s in enumerate(sums):
                    if first:
                        accs[ai][idx] = s
                    else:
                        accs[ai][idx] += s

        if nk == 1:
            finish(partial_sums())
        else:
            pl.when(k == 0)(functools.partial(accumulate, True))
            pl.when(k > 0)(functools.partial(accumulate, False))

            @pl.when(k == nk - 1)
            def _():
                finish([a[...] for a in accs])

    outs, couts = _call(
        body, name, grid, in_specs, [mn_spec] * n_o,
        [jax.ShapeDtypeStruct((M, N), dt) for dt in out_dtypes],
        [pltpu.VMEM((tm, tn), F32) for _ in range(n_acc if nk > 1 else 0)],
        ("parallel", "parallel", "arbitrary"), args, comm)
    return (outs, couts) if comm else outs


def _rope_tables(L):
    t = jnp.arange(L, dtype=jnp.int32)
    f = jnp.arange(32, dtype=jnp.int32).astype(F32)
    ang = t.astype(F32)[:, None] * (ROPE_BASE ** (-f / 32.0))[None, :]
    cos, sin = jnp.cos(ang), jnp.sin(ang)
    ret = jnp.stack([jnp.tile(cos, (1, 4)), jnp.tile(jnp.concatenate([-sin, sin], axis=1), (1, 2))])
    f2 = jnp.arange(16, dtype=jnp.int32).astype(F32)
    inv2 = (ROPE_BASE ** (-f2 / 16.0))[None, :]
    ang_r = (t // GRID_W).astype(F32)[:, None] * inv2
    ang_c = (t % GRID_W).astype(F32)[:, None] * inv2
    cr, sr, cc, sc = jnp.cos(ang_r), jnp.sin(ang_r), jnp.cos(ang_c), jnp.sin(ang_c)
    att = jnp.stack([jnp.tile(jnp.concatenate([cr, cr, cc, cc], axis=1), (1, 2)),
                     jnp.tile(jnp.concatenate([-sr, sr, -sc, sc], axis=1), (1, 2))])
    return ret.astype(F32), att.astype(F32)


def _swap(x, sh):
    lane = lax.broadcasted_iota(jnp.int32, x.shape, 1)
    ra = pltpu.roll(x, LANES - sh, 1)
    rb = pltpu.roll(x, sh, 1)
    la = pltpu.roll(lane, LANES - sh, 1)
    partner = jnp.where((lane % (2 * sh)) < sh, lane + sh, lane - sh)
    return jnp.where(la == partner, ra, rb)


def _rope(x, cos, sin, sh):
    return x * cos + _swap(x, sh) * sin


def _rope_t(d, cos, sin, sh):
    return d * cos + _swap(d * sin, sh)


def _half_mask(shape, a):
    lane = lax.broadcasted_iota(jnp.int32, shape, 1)
    return (lane < 64) if a == 0 else (lane >= 64)


def _mod_fwd(s_in, w_l, b_l):
    D, C6 = w_l.shape
    tk = _pick(D, (512, 256, 128))
    nk = D // tk

    def body(s_ref, w_ref, b_ref, o_ref):
        k = pl.program_id(0)
        s = s_ref[...]
        s = s * _sigmoid(s)
        d = jnp.dot(s, w_ref[...], preferred_element_type=F32, precision=lax.Precision.HIGHEST)

        @pl.when(k == 0)
        def _():
            o_ref[...] = d + b_ref[...]

        @pl.when(k > 0)
        def _():
            o_ref[...] += d

    return pl.pallas_call(
        body, name="mod_fwd", grid=(nk,),
        in_specs=[pl.BlockSpec((16, tk), lambda k: (0, k)), pl.BlockSpec((tk, C6), lambda k: (k, 0)),
                  pl.BlockSpec((1, C6), lambda k: (0, 0))],
        out_specs=pl.BlockSpec((16, C6), lambda k: (0, 0)),
        out_shape=jax.ShapeDtypeStruct((16, C6), F32),
        compiler_params=_cparams(("arbitrary",)),
    )(s_in, w_l, b_l)


def _mod_bwd(s_in, dm, w_l):
    D, C6 = w_l.shape
    tk = _pick(D, (512, 256, 128))
    nk = D // tk

    def body(s_ref, dm_ref, w_ref, gw_ref, gc_ref):
        s = s_ref[...]
        sg = _sigmoid(s)
        act = s * sg
        dmv = dm_ref[...]
        gw_ref[...] = lax.dot_general(act, dmv, TN, preferred_element_type=F32, precision=lax.Precision.HIGHEST)
        ds = lax.dot_general(dmv, w_ref[...], NT, preferred_element_type=F32, precision=lax.Precision.HIGHEST)
        dsil = (sg * (1.0 + s * (1.0 - sg)))[8:9, :]
        gc_ref[...] = jnp.zeros((8, tk), F32) + jnp.sum(ds[8:16, :], axis=0, keepdims=True) * dsil

    return pl.pallas_call(
        body, name="mod_bwd", grid=(nk,),
        in_specs=[pl.BlockSpec((16, tk), lambda k: (0, k)), pl.BlockSpec((16, C6), lambda k: (0, 0)),
                  pl.BlockSpec((tk, C6), lambda k: (k, 0))],
        out_specs=[pl.BlockSpec((tk, C6), lambda k: (k, 0)), pl.BlockSpec((8, tk), lambda k: (0, k))],
        out_shape=[jax.ShapeDtypeStruct((D, C6), F32), jax.ShapeDtypeStruct((8, D), F32)],
        compiler_params=_cparams(("parallel",)),
    )(s_in, dm, w_l)


def _norm_rows(x):
    r = lax.rsqrt(jnp.mean(x * x, axis=-1, keepdims=True) + EPS)
    return x * r, r


def _modulate_fwd(name, x, ctx, g, mod, modc, comm=()):
    L, D = x.shape
    tr = ctx.shape[0]
    nx = L // tr

    def body(x_ref, c_ref, g_ref, m_ref, mc_ref, o_ref):
        i = pl.program_id(0)

        def run(src, m):
            n, _ = _norm_rows(src[...])
            o_ref[...] = (n * g_ref[...] * (1.0 + m[1:2, :]) + m[0:1, :]).astype(o_ref.dtype)

        @pl.when(i < nx)
        def _():
            run(x_ref, m_ref)

        @pl.when(i >= nx)
        def _():
            run(c_ref, mc_ref)

    row = pl.BlockSpec((tr, D), lambda i: (jnp.minimum(i, nx - 1), 0))
    vec = pl.BlockSpec((1, D), lambda i: (0, 0))
    mv = pl.BlockSpec((8, D), lambda i: (0, 0))
    return _call(
        body, name, (nx + 1,), [row, pl.BlockSpec((tr, D), lambda i: (0, 0)), vec, mv, mv],
        [pl.BlockSpec((tr, D), lambda i: (i, 0))], [jax.ShapeDtypeStruct((L + tr, D), BF16)], [],
        ("parallel",), (x, ctx, g, mod, modc), comm, pass_on_at=1.0)


def _residual_modulate_fwd(name, x, fbr, gate, g, mod):
    L, D = x.shape
    tr = _pick(L, (512, 256, 128))

    def body(x_ref, f_ref, gt_ref, g_ref, m_ref, x1_ref, o_ref):
        x1 = x_ref[...] + gt_ref[...] * f_ref[...].astype(F32)
        x1_ref[...] = x1
        n, _ = _norm_rows(x1)
        o_ref[...] = (n * g_ref[...] * (1.0 + m_ref[1:2, :]) + m_ref[0:1, :]).astype(o_ref.dtype)

    row = pl.BlockSpec((tr, D), lambda i: (i, 0))
    vec = pl.BlockSpec((1, D), lambda i: (0, 0))
    return pl.pallas_call(
        body, name=name, grid=(L // tr,),
        in_specs=[row, row, vec, vec, pl.BlockSpec((8, D), lambda i: (0, 0))],
        out_specs=[row, row],
        out_shape=[jax.ShapeDtypeStruct((L, D), F32), jax.ShapeDtypeStruct((L, D), BF16)],
        compiler_params=_cparams(("parallel",)),
    )(x, fbr, gate, g, mod)


def _modulate_bwd(name, x, ctx, dh, g, mod, modc, dres, fbr, gate):
    L, D = x.shape
    tr = ctx.shape[0] if ctx is not None else _pick(L, (256, 128))
    nx = L // tr
    nt = nx + (1 if ctx is not None else 0)
    has_f = fbr is not None

    def body(*refs):
        refs = list(refs)
        x_ref = refs.pop(0)
        c_ref = refs.pop(0) if ctx is not None else None
        dh_ref, g_ref, m_ref = refs.pop(0), refs.pop(0), refs.pop(0)
        mc_ref = refs.pop(0) if ctx is not None else None
        dr_ref = refs.pop(0)
        f_ref = refs.pop(0) if has_f else None
        gt_ref = refs.pop(0) if has_f else None
        dx_ref = refs.pop(0)
        df_ref = refs.pop(0) if has_f else None
        acc_ref = refs.pop(0)
        i = pl.program_id(0)

        @pl.when(i == 0)
        def _():
            acc_ref[...] = jnp.zeros_like(acc_ref)

        def sums(src, m, base, grow):
            n, r = _norm_rows(src[...])
            d = dh_ref[...].astype(F32)
            gg = g_ref[...]
            sc1 = 1.0 + m[1:2, :]
            acc_ref[base:base + 1, :] += jnp.sum(d, axis=0, keepdims=True)
            dn = d * n
            acc_ref[base + 1:base + 2, :] += jnp.sum(dn, axis=0, keepdims=True) * gg
            acc_ref[grow:grow + 1, :] += jnp.sum(dn, axis=0, keepdims=True) * sc1
            dnv = d * (gg * sc1)
            return r * (dnv - n * jnp.mean(dnv * n, axis=-1, keepdims=True))

        def x_rows():
            dx = sums(x_ref, m_ref, 0, 2) + dr_ref[...]
            dx_ref[...] = dx
            if has_f:
                acc_ref[6:7, :] += jnp.sum(dx * f_ref[...].astype(F32), axis=0, keepdims=True)
                df_ref[...] = (dx * gt_ref[...]).astype(df_ref.dtype)

        if ctx is None:
            x_rows()
        else:
            pl.when(i < nx)(x_rows)

            @pl.when(i >= nx)
            def _():
                sums(c_ref, mc_ref, 3, 2)

    row = pl.BlockSpec((tr, D), lambda i: (jnp.minimum(i, nx - 1), 0))
    vec = pl.BlockSpec((1, D), lambda i: (0, 0))
    mv = pl.BlockSpec((8, D), lambda i: (0, 0))
    in_specs, args = [row], [x]
    if ctx is not None:
        in_specs.append(pl.BlockSpec((tr, D), lambda i: (0, 0)))
        args.append(ctx)
    in_specs += [pl.BlockSpec((tr, D), lambda i: (i, 0)), vec, mv]
    args += [dh, g, mod]
    if ctx is not None:
        in_specs.append(mv)
        args.append(modc)
    in_specs.append(row)
    args.append(dres)
    out_specs = [row]
    out_shape = [jax.ShapeDtypeStruct((L, D), F32)]
    if has_f:
        in_specs += [row, vec]
        args += [fbr, gate]
        out_specs.append(row)
        out_shape.append(jax.ShapeDtypeStruct((L, D), BF16))
    out_specs.append(pl.BlockSpec((16, D), lambda i: (0, 0)))
    out_shape.append(jax.ShapeDtypeStruct((16, D), F32))
    return pl.pallas_call(
        body, name=name, grid=(nt,), in_specs=in_specs, out_specs=out_specs, out_shape=out_shape,
        compiler_params=_cparams(("arbitrary",)),
    )(*args)


def _loss_head(x1, tgt, nf, fbr, gate):
    L, D = x1.shape
    tr = _pick(L, (256, 128))

    def body(x_ref, t_ref, w_ref, f_ref, gt_ref, dx_ref, df_ref, acc_ref):
        i = pl.program_id(0)

        @pl.when(i == 0)
        def _():
            acc_ref[...] = jnp.zeros_like(acc_ref)

        n, r = _norm_rows(x_ref[...] + gt_ref[...] * f_ref[...].astype(F32))
        w = w_ref[...]
        e = n * w - t_ref[...]
        acc_ref[0:1, :] += jnp.sum(e * e, axis=0, keepdims=True) * (0.5 / D)
        dout = e * (1.0 / D)
        acc_ref[1:2, :] += jnp.sum(dout * n, axis=0, keepdims=True)
        dn = dout * w
        dx = r * (dn - n * jnp.mean(dn * n, axis=-1, keepdims=True))
        dx_ref[...] = dx
        acc_ref[2:3, :] += jnp.sum(dx * f_ref[...].astype(F32), axis=0, keepdims=True)
        df_ref[...] = (dx * gt_ref[...]).astype(df_ref.dtype)

        @pl.when(i == pl.num_programs(0) - 1)
        def _():
            acc_ref[3:4, :] = jnp.zeros((1, D), F32) + jnp.sum(acc_ref[0:1, :])

    row = pl.BlockSpec((tr, D), lambda i: (i, 0))
    vec = pl.BlockSpec((1, D), lambda i: (0, 0))
    return pl.pallas_call(
        body, name="loss_head", grid=(L // tr,),
        in_specs=[row, row, vec, row, vec],
        out_specs=[row, row, pl.BlockSpec((8, D), lambda i: (0, 0))],
        out_shape=[jax.ShapeDtypeStruct((L, D), F32), jax.ShapeDtypeStruct((L, D), BF16),
                   jax.ShapeDtypeStruct((8, D), F32)],
        compiler_params=_cparams(("arbitrary",)),
    )(x1, tgt, nf, fbr, gate)


RET_SUB = 4
N_TAB = 7


def _ret_tables(rdb, Lc):
    def body(rd_ref, t_ref, c_ref):
        d = pl.program_id(0) // RET_HEADS
        fwd = d == 0
        lg = -jnp.exp(rd_ref[0])
        i = lax.broadcasted_iota(jnp.int32, (CHUNK, CHUNK), 0).astype(F32)
        j = lax.broadcasted_iota(jnp.int32, (CHUNK, CHUNK), 1).astype(F32)
        rel = jnp.where(fwd, i - j, j - i)
        mask = (rel > 0.0) | ((rel == 0.0) & fwd)
        dm = jnp.where(mask, jnp.exp(lg * jnp.maximum(rel, 0.0)), 0.0)
        t_ref[0, 0] = dm
        t_ref[0, 1] = rel * dm
        qc = jnp.where(fwd, i + 1.0, CHUNK - i)
        qw = jnp.exp(lg * qc)
        t_ref[0, 2] = qw
        t_ref[0, 3] = qw * qc
        kc = jnp.where(fwd, CHUNK - 1.0 - i, i)
        kw = jnp.exp(lg * kc)
        t_ref[0, 4] = kw
        t_ref[0, 5] = kw * kc
        t_ref[0, 6] = jnp.exp(lg * float(CHUNK)) + jnp.zeros((CHUNK, CHUNK), F32)
        m = lax.broadcasted_iota(jnp.int32, (Lc, LANES), 0).astype(F32)
        cc = jnp.where(fwd, Lc - 1.0 - m, m)
        cw = jnp.exp(lg * cc)
        c_ref[0, 0] = cw
        c_ref[0, 1] = cw * cc

    return pl.pallas_call(
        body, name="ret_tables", grid=(2 * RET_HEADS,),
        in_specs=[pl.BlockSpec((1, 1, LANES), lambda r: (r, 0, 0))],
        out_specs=[pl.BlockSpec((1, N_TAB, CHUNK, CHUNK), lambda r: (r, 0, 0, 0)),
                   pl.BlockSpec((1, 2, Lc, LANES), lambda r: (r, 0, 0, 0))],
        out_shape=[jax.ShapeDtypeStruct((2 * RET_HEADS, N_TAB, CHUNK, CHUNK), F32),
                   jax.ShapeDtypeStruct((2 * RET_HEADS, 2, Lc, LANES), F32)],
        compiler_params=_cparams(("parallel",)),
    )(rdb)


def _ret_ctx_state(P, ctab, L, Lc):
    cb = L // Lc

    def body(k_ref, v_ref, c_ref, s_ref):
        for p in range(RET_HEADS // 2):
            kp = k_ref[:, p * LANES:(p + 1) * LANES].astype(F32) * K_SCALE
            for a in range(2):
                h = 2 * p + a
                kh = jnp.where(_half_mask(kp.shape, a), kp, 0.0)
                vh = v_ref[:, h * RET_DV:(h + 1) * RET_DV]
                for d in range(2):
                    kw = (kh * c_ref[d * RET_HEADS + h, 0]).astype(BF16)
                    s_ref[d * RET_HEADS + h] = _dot(kw, vh, TN)

    return pl.pallas_call(
        body, name="ret_ctx_state", grid=(1,),
        in_specs=[pl.BlockSpec((Lc, 512), lambda i: (cb, C_RK // 512)),
                  pl.BlockSpec((Lc, 1024), lambda i: (cb, C_RV // 1024)),
                  pl.BlockSpec((2 * RET_HEADS, 2, Lc, LANES), lambda i: (0, 0, 0, 0))],
        out_specs=pl.BlockSpec((2 * RET_HEADS, LANES, RET_DV), lambda i: (0, 0, 0)),
        out_shape=jax.ShapeDtypeStruct((2 * RET_HEADS, LANES, RET_DV), F32),
        compiler_params=_cparams(("arbitrary",)),
    )(P, P, ctab)


def _ret_fwd(P, rope, tabs, s0, L, comm=()):
    n = L // CHUNK
    nb = n // RET_SUB

    def body(qf, kf, vf, rf, qb, kb, vb, rb, t_ref, s0_ref, of_ref, ob_ref, stf_ref, stb_ref, st):
        s = pl.program_id(0)

        @pl.when(s == 0)
        def _():
            st[...] = s0_ref[...]

        for rnd in range(RET_SUB):
            units = []
            for d, (q_ref, k_ref, v_ref, r_ref, o_ref, so_ref) in enumerate(
                    ((qf, kf, vf, rf, of_ref, stf_ref), (qb, kb, vb, rb, ob_ref, stb_ref))):
                j = rnd if d == 0 else RET_SUB - 1 - rnd
                rows = slice(j * CHUNK, (j + 1) * CHUNK)
                cos, sin = r_ref[0, rows, :], r_ref[1, rows, :]
                for p in range(RET_HEADS // 2):
                    qp = _rope(q_ref[rows, p * LANES:(p + 1) * LANES].astype(F32), cos, sin, 32)
                    kp = _rope(k_ref[rows, p * LANES:(p + 1) * LANES].astype(F32), cos, sin, 32) * K_SCALE
                    for a in range(2):
                        h = 2 * p + a
                        hm = _half_mask(qp.shape, a)
                        units.append(dict(r=d * RET_HEADS + h, h=h, a=a, j=j, rows=rows, o_ref=o_ref, so_ref=so_ref,
                                          v_ref=v_ref, qh=jnp.where(hm, qp, 0.0), kh=jnp.where(hm, kp, 0.0)))
            for u in units:
                u["sc"] = _dot(u["qh"].astype(BF16), u["kh"].astype(BF16), NT)
            for u in units:
                r, h = u["r"], u["h"]
                sp = st[r]
                u["so_ref"][u["j"], h] = sp[u["a"] * RET_DK:(u["a"] + 1) * RET_DK, :]
                vh = u["v_ref"][u["rows"], h * RET_DV:(h + 1) * RET_DV]
                o = _dot((u["sc"] * t_ref[r, 0]).astype(BF16), vh, NN)
                o += _dot((u["qh"] * t_ref[r, 2]).astype(BF16), sp.astype(BF16), NN)
                u["o_ref"][u["rows"], h * RET_DV:(h + 1) * RET_DV] = o
            for u in units:
                r, h = u["r"], u["h"]
                vh = u["v_ref"][u["rows"], h * RET_DV:(h + 1) * RET_DV]
                st[r] = t_ref[r, 6] * st[r] + _dot((u["kh"] * t_ref[r, 4]).astype(BF16), vh, TN)

    fw = lambda s: s
    bw = lambda s: nb - 1 - s
    RB = RET_SUB * CHUNK

    def specs(cm):
        return [pl.BlockSpec((RB, 512), lambda s: (cm(s), C_RQ // 512)),
                pl.BlockSpec((RB, 512), lambda s: (cm(s), C_RK // 512)),
                pl.BlockSpec((RB, 1024), lambda s: (cm(s), C_RV // 1024)),
                pl.BlockSpec((2, RB, LANES), lambda s: (0, cm(s), 0))]

    full = lambda shp: pl.BlockSpec(shp, lambda s: (0,) * len(shp))
    return _call(
        body, "ret_fwd", (nb,),
        specs(fw) + specs(bw) + [full((2 * RET_HEADS, N_TAB, CHUNK, CHUNK)), full((2 * RET_HEADS, LANES, RET_DV))],
        [pl.BlockSpec((RB, 1024), lambda s: (fw(s), 0)),
         pl.BlockSpec((RB, 1024), lambda s: (bw(s), 0)),
         pl.BlockSpec((RET_SUB, RET_HEADS, RET_DK, RET_DV), lambda s: (fw(s), 0, 0, 0)),
         pl.BlockSpec((RET_SUB, RET_HEADS, RET_DK, RET_DV), lambda s: (bw(s), 0, 0, 0))],
        [jax.ShapeDtypeStruct((L, 1024), F32), jax.ShapeDtypeStruct((L, 1024), F32),
         jax.ShapeDtypeStruct((n, RET_HEADS, RET_DK, RET_DV), F32),
         jax.ShapeDtypeStruct((n, RET_HEADS, RET_DK, RET_DV), F32)],
        [pltpu.VMEM((2 * RET_HEADS, LANES, RET_DV), F32)],
        ("arbitrary",), (P, P, P, rope, P, P, P, rope, tabs, s0), comm)


def _ret_finish_fwd(of, ob, P, L):
    tr = _pick(L, (512, 256, 128))

    def body(f_ref, b_ref, g_ref, y_ref):
        for h in range(RET_HEADS):
            sl = slice(h * RET_DV, (h + 1) * RET_DV)
            n, _ = _norm_rows(f_ref[:, sl] + b_ref[:, sl])
            g = g_ref[:, sl].astype(F32)
            y_ref[:, sl] = (n * (g * _sigmoid(g))).astype(y_ref.dtype)

    row = pl.BlockSpec((tr, 1024), lambda i: (i, 0))
    return pl.pallas_call(
        body, name="ret_finish_fwd", grid=(L // tr,),
        in_specs=[row, row, pl.BlockSpec((tr, 1024), lambda i: (i, C_RG // 1024))],
        out_specs=row, out_shape=jax.ShapeDtypeStruct((L, 2048), BF16),
        compiler_params=_cparams(("parallel",)),
    )(of, ob, P)


def _ret_finish_bwd(of, ob, P, dY, L):
    tr = _pick(L, (512, 256, 128))

    def body(f_ref, b_ref, g_ref, dy_ref, do_ref, dg_ref):
        for h in range(RET_HEADS):
            sl = slice(h * RET_DV, (h + 1) * RET_DV)
            n, r = _norm_rows(f_ref[:, sl] + b_ref[:, sl])
            g = g_ref[:, sl].astype(F32)
            sg = _sigmoid(g)
            dy = dy_ref[:, sl].astype(F32)
            dg_ref[:, sl] = (dy * n * (sg * (1.0 + g * (1.0 - sg)))).astype(dg_ref.dtype)
            dn = dy * (g * sg)
            do_ref[:, sl] = (r * (dn - n * jnp.mean(dn * n, axis=-1, keepdims=True))).astype(do_ref.dtype)

    row = pl.BlockSpec((tr, 1024), lambda i: (i, 0))
    return pl.pallas_call(
        body, name="ret_finish_bwd", grid=(L // tr,),
        in_specs=[row, row, pl.BlockSpec((tr, 1024), lambda i: (i, C_RG // 1024)), row],
        out_specs=[row, row],
        out_shape=[jax.ShapeDtypeStruct((L, 1024), BF16), jax.ShapeDtypeStruct((L, 1024), BF16)],
        compiler_params=_cparams(("parallel",)),
    )(of, ob, P, dY)


def _ret_bwd(P, rope, tabs, stf, stb, dO, L, comm=()):
    n = L // CHUNK
    nb = n // RET_SUB

    def body(qf, kf, vf, rf, gf, sf, qb, kb, vb, rb, gb, sb, t_ref,
             dqf, dkf, dvf, dqb, dkb, dvb, ds0_ref, dlg_ref, ds):
        s = pl.program_id(0)

        @pl.when(s == 0)
        def _():
            ds[...] = jnp.zeros_like(ds)
            dlg_ref[...] = jnp.zeros_like(dlg_ref)

        for rnd in range(RET_SUB):
            units, pairs = [], []
            for d, (q_ref, k_ref, v_ref, r_ref, g_ref, s_ref, dq_ref, dk_ref, dv_ref) in enumerate(
                    ((qf, kf, vf, rf, gf, sf, dqf, dkf, dvf), (qb, kb, vb, rb, gb, sb, dqb, dkb, dvb))):
                j = RET_SUB - 1 - rnd if d == 0 else rnd
                rows = slice(j * CHUNK, (j + 1) * CHUNK)
                cos, sin = r_ref[0, rows, :], r_ref[1, rows, :]
                for p in range(RET_HEADS // 2):
                    qp = _rope(q_ref[rows, p * LANES:(p + 1) * LANES].astype(F32), cos, sin, 32)
                    kp = _rope(k_ref[rows, p * LANES:(p + 1) * LANES].astype(F32), cos, sin, 32) * K_SCALE
                    pair = dict(p=p, rows=rows, cos=cos, sin=sin, dq_ref=dq_ref, dk_ref=dk_ref, us=[])
                    pairs.append(pair)
                    for a in range(2):
                        h = 2 * p + a
                        r = d * RET_HEADS + h
                        hm = _half_mask(qp.shape, a)
                        zero = jnp.zeros((RET_DK, RET_DV), F32)
                        sp = s_ref[j, h]
                        u = dict(r=r, h=h, rows=rows, dv_ref=dv_ref,
                                 qh=jnp.where(hm, qp, 0.0), kh=jnp.where(hm, kp, 0.0),
                                 vh=v_ref[rows, h * RET_DV:(h + 1) * RET_DV],
                                 gh=g_ref[rows, h * RET_DV:(h + 1) * RET_DV],
                                 sp=jnp.concatenate([sp, zero] if a == 0 else [zero, sp], axis=0),
                                 dsn=ds[r])
                        u["qhb"], u["khb"] = u["qh"].astype(BF16), u["kh"].astype(BF16)
                        units.append(u)
                        pair["us"].append(u)
            for u in units:
                u["am"] = _dot(u["qhb"], u["khb"], NT)
                u["dar"] = _dot(u["gh"], u["vh"], NT)
                u["xq"] = _dot(u["gh"], u["sp"].astype(BF16), NT)
                u["yk"] = _dot(u["vh"], u["dsn"].astype(BF16), NT)
            for u in units:
                r = u["r"]
                dm = t_ref[r, 0]
                u["da"] = (u["dar"] * dm).astype(BF16)
                u["amd"] = (u["am"] * dm).astype(BF16)
                part = (jnp.sum(u["am"] * u["dar"] * t_ref[r, 1]) + jnp.sum(u["qh"] * t_ref[r, 3] * u["xq"])
                        + jnp.sum(u["kh"] * t_ref[r, 5] * u["yk"])
                        + float(CHUNK) * jnp.sum(t_ref[r, 6] * u["dsn"] * u["sp"]))
                dlg_ref[r:r + 1, :] += jnp.zeros((1, LANES), F32) + part
            for u in units:
                r, h = u["r"], u["h"]
                u["dq"] = _dot(u["da"], u["khb"], NN) + u["xq"] * t_ref[r, 2]
                u["dk"] = _dot(u["da"], u["qhb"], TN) + u["yk"] * t_ref[r, 4]
                u["dv_ref"][u["rows"], h * RET_DV:(h + 1) * RET_DV] = (
                    _dot(u["amd"], u["gh"], TN)
                    + _dot((u["kh"] * t_ref[r, 4]).astype(BF16), u["dsn"].astype(BF16), NN)
                ).astype(u["dv_ref"].dtype)
                ds[r] = t_ref[r, 6] * u["dsn"] + _dot((u["qh"] * t_ref[r, 2]).astype(BF16), u["gh"], TN)
            for pair in pairs:
                sl = slice(pair["p"] * LANES, (pair["p"] + 1) * LANES)
                u0, u1 = pair["us"]
                pair["dq_ref"][pair["rows"], sl] = _rope_t(
                    u0["dq"] + u1["dq"], pair["cos"], pair["sin"], 32).astype(BF16)
                pair["dk_ref"][pair["rows"], sl] = _rope_t(
                    (u0["dk"] + u1["dk"]) * K_SCALE, pair["cos"], pair["sin"], 32).astype(BF16)

        @pl.when(s == nb - 1)
        def _():
            ds0_ref[...] = ds[...]

    fw = lambda s: nb - 1 - s
    bw = lambda s: s
    RB = RET_SUB * CHUNK

    def specs(cm):
        return [pl.BlockSpec((RB, 512), lambda s: (cm(s), C_RQ // 512)),
                pl.BlockSpec((RB, 512), lambda s: (cm(s), C_RK // 512)),
                pl.BlockSpec((RB, 1024), lambda s: (cm(s), C_RV // 1024)),
                pl.BlockSpec((2, RB, LANES), lambda s: (0, cm(s), 0)),
                pl.BlockSpec((RB, 1024), lambda s: (cm(s), 0)),
                pl.BlockSpec((RET_SUB, RET_HEADS, RET_DK, RET_DV), lambda s: (cm(s), 0, 0, 0))]

    def ospecs(cm):
        return [pl.BlockSpec((RB, 512), lambda s: (cm(s), 0)), pl.BlockSpec((RB, 512), lambda s: (cm(s), 0)),
                pl.BlockSpec((RB, 1024), lambda s: (cm(s), 0))]

    oshape = [jax.ShapeDtypeStruct((L, 512), BF16), jax.ShapeDtypeStruct((L, 512), BF16),
              jax.ShapeDtypeStruct((L, 1024), BF16)]
    full = lambda shp: pl.BlockSpec(shp, lambda s: (0,) * len(shp))
    return _call(
        body, "ret_bwd", (nb,),
        specs(fw) + specs(bw) + [full((2 * RET_HEADS, N_TAB, CHUNK, CHUNK))],
        ospecs(fw) + ospecs(bw) + [full((2 * RET_HEADS, LANES, RET_DV)), full((2 * RET_HEADS, LANES))],
        oshape + oshape + [jax.ShapeDtypeStruct((2 * RET_HEADS, LANES, RET_DV), F32),
                           jax.ShapeDtypeStruct((2 * RET_HEADS, LANES), F32)],
        [pltpu.VMEM((2 * RET_HEADS, LANES, RET_DV), F32)],
        ("arbitrary",), (P, P, P, rope, dO, stf, P, P, P, rope, dO, stb, tabs), comm)


def _ret_ctx_bwd(P, ctab, ds0, dlg, rdb, L, Lc):
    cb = L // Lc

    def body(k_ref, v_ref, c_ref, ds_ref, dlg_ref, rd_ref, dk_ref, dv_ref, drd_ref):
        for p in range(RET_HEADS // 2):
            kp = k_ref[:, p * LANES:(p + 1) * LANES].astype(F32) * K_SCALE
            dkp = jnp.zeros((Lc, LANES), F32)
            for a in range(2):
                h = 2 * p + a
                kh = jnp.where(_half_mask(kp.shape, a), kp, 0.0)
                vh = v_ref[:, h * RET_DV:(h + 1) * RET_DV]
                dvh = jnp.zeros((Lc, RET_DV), F32)
                for d in range(2):
                    r = d * RET_HEADS + h
                    dsb = ds_ref[r].astype(BF16)
                    cw, cwc = c_ref[r, 0], c_ref[r, 1]
                    y = _dot(vh, dsb, NT)
                    dkp += y * cw
                    dvh += _dot((kh * cw).astype(BF16), dsb, NN)
                    lg = -jnp.exp(rd_ref[r])
                    drd_ref[r:r + 1, :] = (dlg_ref[r:r + 1, :] + jnp.sum(kh * cwc * y)) * lg
                dv_ref[:, h * RET_DV:(h + 1) * RET_DV] = dvh
            dk_ref[:, p * LANES:(p + 1) * LANES] = dkp * K_SCALE

    full = lambda shp: pl.BlockSpec(shp, lambda i: (0,) * len(shp))
    return pl.pallas_call(
        body, name="ret_ctx_bwd", grid=(1,),
        in_specs=[pl.BlockSpec((Lc, 512), lambda i: (cb, C_RK // 512)),
                  pl.BlockSpec((Lc, 1024), lambda i: (cb, C_RV // 1024)),
                  full((2 * RET_HEADS, 2, Lc, LANES)), full((2 * RET_HEADS, LANES, RET_DV)),
                  full((2 * RET_HEADS, LANES)), full((2 * RET_HEADS, 1, LANES))],
        out_specs=[full((Lc, 512)), full((Lc, 1024)), full((2 * RET_HEADS, LANES))],
        out_shape=[jax.ShapeDtypeStruct((Lc, 512), F32), jax.ShapeDtypeStruct((Lc, 1024), F32),
                   jax.ShapeDtypeStruct((2 * RET_HEADS, LANES), F32)],
        compiler_params=_cparams(("arbitrary",)),
    )(P, P, ctab, ds0, dlg, rdb)


BLK = 128
N_LOC = 3 * BLK


ATT_SUB = 4


def _att_inputs(P, rope, L, Lc):
    n = L // BLK
    cb = L // Lc
    prev = lambda i: jnp.maximum(ATT_SUB * i - 1, 0)
    nxt = lambda i: jnp.minimum(ATT_SUB * i + ATT_SUB, n - 1)
    specs = [pl.BlockSpec((ATT_SUB * BLK, 1024), lambda i: (i, C_AQ // 1024))]
    args = [P]
    for col in (C_AK // 256, C_AV // 256):
        specs += [pl.BlockSpec((BLK, 256), functools.partial(lambda i, col: (prev(i), col), col=col)),
                  pl.BlockSpec((ATT_SUB * BLK, 256), functools.partial(lambda i, col: (i, col), col=col)),
                  pl.BlockSpec((BLK, 256), functools.partial(lambda i, col: (nxt(i), col), col=col)),
                  pl.BlockSpec((Lc, 256), functools.partial(lambda i, col: (cb, col), col=col))]
        args += [P] * 4
    specs += [pl.BlockSpec((2, BLK, LANES), lambda i: (0, prev(i), 0)),
              pl.BlockSpec((2, ATT_SUB * BLK, LANES), lambda i: (0, i, 0)),
              pl.BlockSpec((2, BLK, LANES), lambda i: (0, nxt(i), 0))]
    args += [rope] * 3
    return specs, args


def _att_prep(i, n, refs, Lc):
    q_ref, kp_ref, kc_ref, kn_ref, kx_ref, vp_ref, vc_ref, vn_ref, vx_ref, rp_ref, rc_ref, rn_ref = refs
    cos = jnp.concatenate([rp_ref[0], rc_ref[0], rn_ref[0]], axis=0)
    sin = jnp.concatenate([rp_ref[1], rc_ref[1], rn_ref[1]], axis=0)

    def dup(x):
        xr = pltpu.roll(x, 64, 1)
        return [jnp.where(_half_mask(x.shape, b), x, xr).astype(BF16) for b in range(2)]

    kd = [[] for _ in range(ATT_SUB)]
    vd = [[] for _ in range(ATT_SUB)]
    for t in range(ATT_KV // 2):
        sl = slice(t * LANES, (t + 1) * LANES)
        kl = jnp.concatenate([kp_ref[:, sl], kc_ref[:, sl], kn_ref[:, sl]], axis=0).astype(F32)
        kl = dup(_rope(kl, cos, sin, 16))
        vl = dup(jnp.concatenate([vp_ref[:, sl], vc_ref[:, sl], vn_ref[:, sl]], axis=0).astype(F32))
        kx, vx = dup(kx_ref[:, sl].astype(F32)), dup(vx_ref[:, sl].astype(F32))
        for j in range(ATT_SUB):
            rows = slice(j * BLK, j * BLK + N_LOC)
            for b in range(2):
                kd[j].append(jnp.concatenate([kl[b][rows], kx[b]], axis=0))
                vd[j].append(jnp.concatenate([vl[b][rows], vx[b]], axis=0))
    nk = N_LOC + Lc
    rr = lax.broadcasted_iota(jnp.int32, (BLK, nk), 0)
    ss = lax.broadcasted_iota(jnp.int32, (BLK, nk), 1)
    band = (ss >= rr) & (ss <= rr + 2 * BLK)
    bias4, tabs = [], []
    for j in range(ATT_SUB):
        blk = ATT_SUB * i + j
        lo = jnp.where(blk == 0, BLK, 0)
        hi = jnp.where(blk == n - 1, 2 * BLK, N_LOC)
        bias = jnp.where((ss >= N_LOC) | (band & (ss >= lo) & (ss < hi)), 0.0, NEG)
        bias4.append(jnp.concatenate([bias] * 4, axis=0))
        tabs.append((rc_ref[0, j * BLK:(j + 1) * BLK, :], rc_ref[1, j * BLK:(j + 1) * BLK, :]))
    return kd, vd, bias4, tabs


LOG2E = 1.4426950408889634
LN2 = 0.6931471805599453
Q_SCALE = A_SCALE * LOG2E


def _stack4(ref, rows, g, f=None):
    parts = []
    for jp in range(2):
        t = ref[rows, (2 * g + jp) * LANES:(2 * g + jp + 1) * LANES].astype(F32)
        if f is not None:
            t = f(t)
        for a in range(2):
            parts.append(jnp.where(_half_mask(t.shape, a), t, 0.0))
    return jnp.concatenate(parts, axis=0)


def _unstack4(x4, jp):
    r0 = 2 * jp * BLK
    lo = x4[r0:r0 + BLK]
    hi = x4[r0 + BLK:r0 + 2 * BLK]
    return jnp.where(_half_mask(lo.shape, 0), lo, hi)


def _softmax_parts(s, bias4, sink_ref, g):
    sink_col = LOG2E * jnp.concatenate(
        [jnp.zeros((BLK, 1), F32) + sink_ref[4 * g + r:4 * g + r + 1, 0:1] for r in range(4)], axis=0)
    s = s + bias4
    m = jnp.maximum(jnp.max(s, axis=-1, keepdims=True), sink_col)
    e = jnp.exp2(s - m)
    es = jnp.exp2(sink_col - m)
    return e, es, jnp.sum(e, axis=-1, keepdims=True) + es


def _att_fwd(P, rope, sinkb, Y, L, Lc, comm=()):
    n = L // BLK
    specs, args = _att_inputs(P, rope, L, Lc)

    def body(*refs):
        sink_ref, o_ref = refs[12], refs[14]
        i = pl.program_id(0)
        kd, vd, bias4, tabs = _att_prep(i, n, refs[:12], Lc)
        for j in range(ATT_SUB):
            rows = slice(j * BLK, (j + 1) * BLK)
            cq, sq = tabs[j]
            for g in range(ATT_KV):
                q4 = _stack4(refs[0], rows, g, lambda t: _rope(t, cq, sq, 16) * Q_SCALE).astype(BF16)
                e, _, l = _softmax_parts(_dot(q4, kd[j][g], NT), bias4[j], sink_ref, g)
                o4 = _dot(e.astype(BF16), vd[j][g], NN) * (1.0 / l)
                for jp in range(2):
                    c0 = (2 * g + jp) * LANES
                    o_ref[rows, c0:c0 + LANES] = _unstack4(o4, jp).astype(o_ref.dtype)

    return _call(
        body, "att_fwd", (n // ATT_SUB,),
        specs + [pl.BlockSpec((ATT_HEADS, LANES), lambda i: (0, 0)), pl.BlockSpec(memory_space=pl.ANY)],
        [pl.BlockSpec((ATT_SUB * BLK, 1024), lambda i: (i, 1))], [jax.ShapeDtypeStruct((L, 2048), BF16)], [],
        ("parallel",), (*args, sinkb, Y), comm, aliases={13: 0})


def _att_bwd(P, rope, sinkb, Y, dY, L, Lc, comm=()):
    n = L // BLK
    specs, args = _att_inputs(P, rope, L, Lc)
    nk = N_LOC + Lc

    def body(*refs):
        sink_ref, y_ref, dy_ref = refs[12], refs[13], refs[14]
        dq_ref, dkl_ref, dvl_ref, dkx_ref, dvx_ref, dsk_ref = refs[15:21]
        i = pl.program_id(0)

        @pl.when(i == 0)
        def _():
            dkx_ref[...] = jnp.zeros_like(dkx_ref)
            dvx_ref[...] = jnp.zeros_like(dvx_ref)
            dsk_ref[...] = jnp.zeros_like(dsk_ref)

        kd, vd, bias4, tabs = _att_prep(i, n, refs[:12], Lc)
        for j in range(ATT_SUB):
            rows = slice(j * BLK, (j + 1) * BLK)
            cq, sq = tabs[j]
            for t in range(ATT_KV // 2):
                dk_halves, dv_halves = [], []
                for b in range(2):
                    g = 2 * t + b
                    q4 = _stack4(refs[0], rows, g, lambda x: _rope(x, cq, sq, 16) * Q_SCALE).astype(BF16)
                    do4 = _stack4(dy_ref, rows, g)
                    delta = jnp.sum(do4 * _stack4(y_ref, rows, g), axis=-1, keepdims=True)
                    do4b = do4.astype(BF16)
                    e, es, l = _softmax_parts(_dot(q4, kd[j][g], NT), bias4[j], sink_ref, g)
                    inv = 1.0 / l
                    p = e * inv
                    dsc = (p * (_dot(do4b, vd[j][g], NT) - delta)).astype(BF16)
                    dsr = es * inv * delta
                    for r in range(4):
                        h = 4 * g + r
                        dsk_ref[h:h + 1, :] += jnp.zeros((1, LANES), F32) - jnp.sum(dsr[r * BLK:(r + 1) * BLK])
                    dq4 = _dot(dsc, kd[j][g], NN) * A_SCALE
                    for jp in range(2):
                        c0 = (2 * g + jp) * LANES
                        dq_ref[rows, c0:c0 + LANES] = _rope_t(_unstack4(dq4, jp), cq, sq, 16).astype(dq_ref.dtype)
                    dkd = _dot(q4, dsc, TN) * LN2
                    dvd = _dot(do4b, p.astype(BF16), TN)
                    dk_halves.append(dkd[:ATT_DH] + dkd[ATT_DH:])
                    dv_halves.append(dvd[:ATT_DH] + dvd[ATT_DH:])
                dk_t = jnp.concatenate(dk_halves, axis=0).T
                dv_t = jnp.concatenate(dv_halves, axis=0).T
                sl = slice(t * LANES, (t + 1) * LANES)
                dkl_ref[j, :, sl] = dk_t[:N_LOC]
                dvl_ref[j, :, sl] = dv_t[:N_LOC]
                dkx_ref[:, sl] += dk_t[N_LOC:]
                dvx_ref[:, sl] += dv_t[N_LOC:]

    row = pl.BlockSpec((ATT_SUB * BLK, 1024), lambda i: (i, 0))
    loc = pl.BlockSpec((ATT_SUB, N_LOC, 256), lambda i: (i, 0, 0))
    cx = pl.BlockSpec((Lc, 256), lambda i: (0, 0))
    return _call(
        body, "att_bwd", (n // ATT_SUB,),
        specs + [pl.BlockSpec((ATT_HEADS, LANES), lambda i: (0, 0))]
        + [pl.BlockSpec((ATT_SUB * BLK, 1024), lambda i: (i, 1))] * 2,
        [row, loc, loc, cx, cx, pl.BlockSpec((ATT_HEADS, LANES), lambda i: (0, 0))],
        [jax.ShapeDtypeStruct((L, 1024), BF16), jax.ShapeDtypeStruct((n, N_LOC, 256), F32),
         jax.ShapeDtypeStruct((n, N_LOC, 256), F32), jax.ShapeDtypeStruct((Lc, 256), F32),
         jax.ShapeDtypeStruct((Lc, 256), F32), jax.ShapeDtypeStruct((ATT_HEADS, LANES), F32)], [],
        ("arbitrary",), (*args, sinkb, Y, dY), comm)


def _assemble_dp(L, Lc, dqf, dqb, dkf, dkb, dvf, dvb, drg, daq, dkl, dvl, rope_att, dck, dcv, dkx, dvx):
    n = L // BLK
    nc = Lc // BLK

    def body(dqf_r, dqb_r, dkf_r, dkb_r, dvf_r, dvb_r, drg_r, daq_r, kl0, kl1, kl2, vl0, vl1, vl2, rp_r,
             dck_r, dcv_r, dkx_r, dvx_r, o_ref):
        i = pl.program_id(0)

        @pl.when(i < n)
        def _():
            add = lambda a, b: (a[...].astype(F32) + b[...].astype(F32)).astype(o_ref.dtype)
            o_ref[:, C_RQ:C_RK] = add(dqf_r, dqb_r)
            o_ref[:, C_RK:C_RV] = add(dkf_r, dkb_r)
            o_ref[:, C_RV:C_RG] = add(dvf_r, dvb_r)
            o_ref[:, C_RG:C_AQ] = drg_r[...].astype(o_ref.dtype)
            o_ref[:, C_AQ:C_AK] = daq_r[...].astype(o_ref.dtype)
            w0 = jnp.where(i > 0, 1.0, 0.0)
            w2 = jnp.where(i < n - 1, 1.0, 0.0)
            dk = kl0[0] * w0 + kl1[0] + kl2[0] * w2
            dv = vl0[0] * w0 + vl1[0] + vl2[0] * w2
            for t in range(ATT_KV // 2):
                sl = slice(t * LANES, (t + 1) * LANES)
                o_ref[:, C_AK + t * LANES:C_AK + (t + 1) * LANES] = _rope_t(
                    dk[:, sl], rp_r[0], rp_r[1], 16).astype(o_ref.dtype)
            o_ref[:, C_AV:D_PROJ] = dv.astype(o_ref.dtype)

        @pl.when(i >= n)
        def _():
            o_ref[:, C_RQ:C_RK] = jnp.zeros((BLK, C_RK - C_RQ), o_ref.dtype)
            o_ref[:, C_RK:C_RV] = dck_r[...].astype(o_ref.dtype)
            o_ref[:, C_RV:C_RG] = dcv_r[...].astype(o_ref.dtype)
            o_ref[:, C_RG:C_AK] = jnp.zeros((BLK, C_AK - C_RG), o_ref.dtype)
            o_ref[:, C_AK:C_AV] = dkx_r[...].astype(o_ref.dtype)
            o_ref[:, C_AV:D_PROJ] = dvx_r[...].astype(o_ref.dtype)

    xm = lambda i: jnp.minimum(i, n - 1)
    cm = lambda i: jnp.clip(i - n, 0, nc - 1)
    r512 = pl.BlockSpec((BLK, 512), lambda i: (xm(i), 0))
    r1024 = pl.BlockSpec((BLK, 1024), lambda i: (xm(i), 0))
    part = lambda off: pl.BlockSpec((1, BLK, 256), lambda i: (jnp.clip(xm(i) + off, 0, n - 1), 1 - off, 0))
    return pl.pallas_call(
        body, name="assemble_dp", grid=(n + nc,),
        in_specs=[r512, r512, r512, r512, r1024, r1024, r1024, r1024,
                  part(-1), part(0), part(1), part(-1), part(0), part(1),
                  pl.BlockSpec((2, BLK, LANES), lambda i: (0, xm(i), 0)),
                  pl.BlockSpec((BLK, 512), lambda i: (cm(i), 0)), pl.BlockSpec((BLK, 1024), lambda i: (cm(i), 0)),
                  pl.BlockSpec((BLK, 256), lambda i: (cm(i), 0)), pl.BlockSpec((BLK, 256), lambda i: (cm(i), 0))],
        out_specs=pl.BlockSpec((BLK, D_PROJ), lambda i: (i, 0)),
        out_shape=jax.ShapeDtypeStruct((L + Lc, D_PROJ), BF16),
        compiler_params=_cparams(("parallel",)),
    )(dqf, dqb, dkf, dkb, dvf, dvb, drg, daq, dkl, dkl, dkl, dvl, dvl, dvl, rope_att, dck, dcv, dkx, dvx)


def _adam_math(w, g, m, v):
    m = ADAM_B1 * m + (1.0 - ADAM_B1) * g
    v = ADAM_B2 * v + (1.0 - ADAM_B2) * (g * g)
    m_hat = m / (1.0 - ADAM_B1 ** ADAM_STEP)
    v_hat = v / (1.0 - ADAM_B2 ** ADAM_STEP)
    delta = -ADAM_LR * (m_hat / (jnp.sqrt(v_hat) + ADAM_EPS) + ADAM_WD * w)
    return delta, m, v


def _adam(name, w, m, v, g=None, parts=None):
    R, C = w.shape
    tr = _pick(R, (256, 128, 64, 32, 16, 8))
    summed = parts is not None
    n_parts = parts.shape[0] if summed else 0

    def body(w_ref, m_ref, v_ref, g_ref, go_ref, d_ref, mo_ref, vo_ref):
        if summed:
            gv = g_ref[0].astype(F32)
            for j in range(1, n_parts):
                gv = gv + g_ref[j].astype(F32)
        else:
            gv = g_ref[...]
        d, mn, vn = _adam_math(w_ref[...], gv, m_ref[...], v_ref[...])
        go_ref[...] = gv
        d_ref[...] = d
        mo_ref[...] = mn
        vo_ref[...] = vn

    row = pl.BlockSpec((tr, C), lambda i: (i, 0))
    gspec = pl.BlockSpec((n_parts, tr, C), lambda i: (0, i, 0)) if summed else row
    return pl.pallas_call(
        body, name=name, grid=(R // tr,),
        in_specs=[row, row, row, gspec], out_specs=[row] * 4,
        out_shape=[jax.ShapeDtypeStruct((R, C), F32)] * 4,
        compiler_params=_cparams(("parallel",)),
    )(w, m, v, parts if summed else g)


def _rows_full(g):
    _, R, D = g.shape
    return g.reshape(N_DEV * R, D)


def _rows_slots(g):
    N, D = g.shape
    return g.reshape(N_DEV, N // N_DEV, D)


def _pad_rows(a, rows):
    return jnp.concatenate([a, jnp.zeros((rows - a.shape[0],) + a.shape[1:], a.dtype)], axis=0)


def kernel(x, c, ctx, c_ctx, w_mod, b_mod, norm_mix, norm_ffn, w_in, ret_decay, attn_sink, w_out, w_gate, w_up, w_down, norm_final, loss_target, m_c_ctx, m_w_mod, m_b_mod, m_norm_mix, m_norm_ffn, m_w_in, m_ret_decay, m_attn_sink, m_w_out, m_w_gate, m_w_up, m_w_down, m_norm_final, v_c_ctx, v_w_mod, v_b_mod, v_norm_mix, v_norm_ffn, v_w_in, v_ret_decay, v_attn_sink, v_w_out, v_w_gate, v_w_up, v_w_down, v_norm_final):
    L, D = x.shape[1], x.shape[2]
    Lc = ctx.shape[1]
    DF = w_gate.shape[2] * N_DEV
    C6 = w_mod.shape[2]
    me = _my_id()
    xs, cx, tgt = x[0], ctx[0], loss_target[0]

    ag_in = ("ag2", w_in[0].T.astype(BF16))
    ag_out, ag_gate = ("ag2", w_out[0].astype(BF16)), ("ag2", w_gate[0].T.astype(BF16))
    ag_up, ag_down = ("ag2", w_up[0].T.astype(BF16)), ("ag2", w_down[0].astype(BF16))

    cs = _allgather(c, "ag_c")[:, 0, :]
    s_in = _pad_rows(jnp.concatenate([cs, c_ctx[None, :]], axis=0), 16)
    b_l = lax.dynamic_slice_in_dim(b_mod, me * C6, C6, axis=1)
    mod_parts = _allgather(_mod_fwd(s_in, w_mod[0], b_l), "ag_mod")
    mod = _pad_rows(lax.dynamic_index_in_dim(mod_parts, me, axis=1, keepdims=False).reshape(6, D), 8)
    modc = _pad_rows(mod_parts[:, N_DEV, :].reshape(6, D), 8)
    mix_mod, ffn_mod = mod, jnp.roll(mod, -3, axis=0)
    gt_m, gt_f = mod[2:3], mod[5:6]

    rope_ret, rope_att = _rope_tables(L)
    rdb = jnp.broadcast_to(ret_decay[0].reshape(2 * RET_HEADS, 1, 1), (2 * RET_HEADS, 1, LANES))
    sinkb = jnp.broadcast_to(attn_sink[0].reshape(ATT_HEADS, 1), (ATT_HEADS, LANES))

    tm = _pick(L + Lc, (1408, 768, 512, 384, 256, 128))
    tmx = _pick(L, (1024, 512, 256, 128))

    (H,), (g_in,) = _modulate_fwd("mod_mix_fwd", xs, cx, norm_mix, mix_mod, modc, comm=[ag_in])
    W_inT = _rows_full(g_in)
    ident = lambda a, e: a
    tP, tD, tF = _pick(D_PROJ, (1152, 768, 512)), _pick(D, (2048, 1024, 512)), _pick(DF, (512, 256, 128))
    (P,), (g_gate,) = _matmul("mm_in", [(H, W_inT, 0)], 1, L + Lc, D_PROJ, D, "nt",
                              (tm, _pick(D_PROJ, (1536, 768, 512)), D), [], [BF16], ident,
                              comm=[ag_gate])
    W_gateT = _rows_full(g_gate)
    tabs, ctab = _ret_tables(rdb, Lc)
    s0 = _ret_ctx_state(P, ctab, L, Lc)
    (o_f, o_b, st_f, st_b), (g_out,) = _ret_fwd(P, rope_ret, tabs, s0, L, comm=[ag_out])
    W_out = _rows_full(g_out)
    Y_half = _ret_finish_fwd(o_f, o_b, P, L)
    (Y,), (g_up,) = _att_fwd(P, rope_att, sinkb, Y_half, L, Lc, comm=[ag_up])
    W_upT = _rows_full(g_up)
    KO = Y.shape[1]
    f_mix = _matmul("mm_out", [(Y, W_out, 0)], 1, L, D, KO, "nn", (tmx, tD, KO), [], [BF16], ident)[0]

    x1, H2 = _residual_modulate_fwd("mod_ffn_fwd", xs, f_mix, gt_m, norm_ffn, ffn_mod)

    def swiglu_epi(a, e):
        sg = _sigmoid(a[0])
        act = a[0] * sg
        return [act, a[1] * (sg * (1.0 + a[0] * (1.0 - sg))), act * a[1]]

    tm2 = tmx
    (act, up_dact, hmid), (g_down,) = _matmul("mm_gate_up", [(H2, W_gateT, 0), (H2, W_upT, 1)], 2, L, DF, D, "nt",
                                              (tm2, tF, D), [], [BF16, BF16, BF16], swiglu_epi, comm=[ag_down])
    W_down = _rows_full(g_down)
    f_ffn = _matmul("mm_down", [(hmid, W_down, 0)], 1, L, D, DF, "nn", (tm2, tD, _pick(DF, (1408, 512, 256, 128))),
                    [], [BF16], ident)[0]

    dx2, dFf, sums_l = _loss_head(x1, tgt, norm_final.reshape(1, D), f_ffn, gt_f)

    def dswiglu_epi(a, e):
        return [a[0] * e[0].astype(F32), a[0] * e[1].astype(F32)]

    dga, dup = _matmul("mm_d_down", [(dFf, W_down, 0)], 1, L, DF, D, "nt", (tm2, tF, D),
                       [(up_dact, "mn"), (act, "mn")], [BF16, BF16], dswiglu_epi)
    tkt, tkl = _pick(L, (512, 256, 128)), _pick(L, (1024, 512, 256, 128))
    dW_down =_matmul("mm_gw_down", [(hmid, dFf, 0)], 1, DF, D, L, "tn",
                      (_pick(DF, (1408, 512, 256, 128)), tD, tkt), [], [BF16], ident)[0]
    (dW_gateT, dW_upT), (p_down,) = _matmul("mm_gw_gate_up", [(dga, H2, 0), (dup, H2, 1)], 2, DF, D, L, "tn",
                                            (tF, tD, tkl), [], [BF16, BF16], ident,
                                            comm=[("a2a", _rows_slots(dW_down))])
    (dH2,), (p_gate,) = _matmul("mm_d_gate_up", [(dga, W_gateT, 0), (dup, W_upT, 0)], 1, L, D, DF, "nn",
                                (tm2, tD, tF), [], [BF16], ident, comm=[("a2a", _rows_slots(dW_gateT))])
    dx1, dFm, sums_f = _modulate_bwd("mod_ffn_bwd", x1, None, dH2, norm_ffn, ffn_mod, None, dx2, f_mix, gt_m)

    tO = _pick(KO, (2048, 1024, 512))
    dY = _matmul("mm_d_out", [(dFm, W_out, 0)], 1, L, KO, D, "nt", (tmx, tO, D), [], [BF16], ident)[0]
    dW_out = _matmul("mm_gw_out", [(Y, dFm, 0)], 1, KO, D, L, "tn",
                     (_pick(KO, (1024, 512)), tD, _pick(L, (2048, 1024, 512, 256, 128))), [], [BF16],
                     ident)[0]
    dO, drg = _ret_finish_bwd(o_f, o_b, P, dY, L)
    (dqf, dkf, dvf, dqb, dkb, dvb, ds0, dlg), (p_out,) = _ret_bwd(
        P, rope_ret, tabs, st_f, st_b, dO, L, comm=[("a2a", _rows_slots(dW_out))])
    dck, dcv, d_rd = _ret_ctx_bwd(P, ctab, ds0, dlg, rdb, L, Lc)
    (daq, dkl, dvl, dkx, dvx, d_sink), (p_up,) = _att_bwd(
        P, rope_att, sinkb, Y, dY, L, Lc, comm=[("a2a", _rows_slots(dW_upT))])
    dP = _assemble_dp(L, Lc, dqf, dqb, dkf, dkb, dvf, dvb, drg, daq, dkl, dvl, rope_att, dck, dcv, dkx, dvx)
    tkc = _pick(L + Lc, (768, 256, 128))
    dW_inT = _matmul("mm_gw_in", [(dP, H, 0)], 1, D_PROJ, D, L + Lc, "tn", (tP, tD, tkc), [], [BF16], ident)[0]
    (dH,), (p_in,) = _matmul("mm_d_in", [(dP, W_inT, 0)], 1, L + Lc, D, D_PROJ, "nn",
                             (tm, tD, _pick(D_PROJ, (768, 512, 256))), [], [BF16], ident,
                             comm=[("a2a", _rows_slots(dW_inT))])
    grad_x, sums_m = _modulate_bwd("mod_mix_bwd", xs, cx, dH, norm_mix, mix_mod, modc, dx1, None, None)

    zero = jnp.zeros((1, D), F32)
    dmod = jnp.concatenate([sums_m[0:1], sums_m[1:2], sums_f[6:7], sums_f[0:1], sums_f[1:2], sums_l[2:3]], axis=1)
    dmodc = jnp.concatenate([sums_m[3:4], sums_m[4:5], zero, zero, zero, zero], axis=1)
    dm_all = _allgather(jnp.concatenate([dmod, dmodc], axis=0), "ag_dmod")
    dm_cols = lax.dynamic_slice_in_dim(dm_all, me * C6, C6, axis=2)
    dm_in = jnp.concatenate([dm_cols[:, 0, :], dm_cols[:, 1, :]], axis=0)
    s_bwd = jnp.concatenate([cs, jnp.broadcast_to(c_ctx[None, :], (N_DEV, D))], axis=0)
    g_w_mod, dsil = _mod_bwd(s_bwd, dm_in, w_mod[0])

    lane_pad = lambda a: _pad_rows(a.reshape(-1, 1), LANES).reshape(1, LANES)
    pack = jnp.concatenate([dsil[0:1], sums_m[2:3], sums_f[2:3], sums_l[1:2],
                            lane_pad(d_rd[:, 0]), lane_pad(d_sink[:, 0]), sums_l[3:4, 0:LANES]], axis=1)
    packs = _allgather(pack, "ag_small")
    zl = jnp.zeros((1, LANES), F32)

    def pack_w(a_c, a_nm, a_nf, a_fin, a_rd, a_sk):
        return jnp.concatenate([a_c.reshape(1, D), a_nm, a_nf, a_fin.reshape(1, D), lane_pad(a_rd.reshape(-1)),
                                lane_pad(a_sk.reshape(-1)), zl], axis=1)

    sg, sd, sm, sv = _adam("adam_small", pack_w(c_ctx, norm_mix, norm_ffn, norm_final, ret_decay, attn_sink),
                           pack_w(m_c_ctx, m_norm_mix, m_norm_ffn, m_norm_final, m_ret_decay, m_attn_sink),
                           pack_w(v_c_ctx, v_norm_mix, v_norm_ffn, v_norm_final, v_ret_decay, v_attn_sink),
                           parts=packs)
    loss = sg[0, 4 * D + 2 * LANES]

    def unpack(a):
        return (a[0, 0:D], a[:, D:2 * D], a[:, 2 * D:3 * D], a[0, 3 * D:4 * D],
                a[0, 4 * D:4 * D + 2 * RET_HEADS].reshape(1, 2, RET_HEADS),
                a[:, 4 * D + LANES:4 * D + LANES + ATT_HEADS])

    bg, bd, bm, bv = _adam("adam_b_mod", b_mod, m_b_mod, v_b_mod, parts=dm_all.reshape(2 * N_DEV, 1, 6 * D))
    wg, wd, wm, wv = _adam("adam_w_mod", w_mod[0], m_w_mod[0], v_w_mod[0], g=g_w_mod)

    big = {}
    for nm, w, m, v, parts, transposed in (
            ("w_in", w_in, m_w_in, v_w_in, p_in, True), ("w_out", w_out, m_w_out, v_w_out, p_out, False),
            ("w_gate", w_gate, m_w_gate, v_w_gate, p_gate, True), ("w_up", w_up, m_w_up, v_w_up, p_up, True),
            ("w_down", w_down, m_w_down, v_w_down, p_down, False)):
        if transposed:
            res = [a.T for a in _adam("adam_" + nm, w[0].T, m[0].T, v[0].T, parts=parts)]
        else:
            res = _adam("adam_" + nm, w[0], m[0], v[0], parts=parts)
        big[nm] = [a[None] for a in res]

    g_s, d_s, m_s, v_s = unpack(sg), unpack(sd), unpack(sm), unpack(sv)

    def leaves(k, small, bmod, wmod):
        return (small[0], wmod[None], bmod, small[1], small[2], big["w_in"][k], small[4], small[5],
                big["w_out"][k], big["w_gate"][k], big["w_up"][k], big["w_down"][k], small[3])

    return (loss, grad_x[None], *leaves(0, g_s, bg, wg), *leaves(1, d_s, bd, wd),
            *leaves(2, m_s, bm, wm), *leaves(3, v_s, bv, wv))
```

```python
import functools

import jax
import jax.numpy as jnp
from jax import lax
from jax.experimental import pallas as pl
from jax.experimental.pallas import tpu as pltpu

F32 = jnp.float32
BF16 = jnp.bfloat16

N_DEV = 8
LANES = 128
RET_HEADS = 8
RET_DK = 64
RET_DV = 128
CHUNK = 128
ATT_HEADS = 16
ATT_KV = 4
ATT_DH = 64
GRID_W = 64
ROPE_BASE = 10000.0
EPS = 1e-6
NEG = -1e30
C_RQ, C_RK, C_RV, C_RG, C_AQ, C_AK, C_AV, D_PROJ = 0, 512, 1024, 2048, 3072, 4096, 4352, 4608
K_SCALE = RET_DK ** -0.5
A_SCALE = ATT_DH ** -0.5

ADAM_LR, ADAM_B1, ADAM_B2, ADAM_EPS, ADAM_WD, ADAM_STEP = 0.001, 0.9, 0.999, 1e-08, 0.01, 10

VMEM_BIG = 58 * 1024 * 1024

NN = (((1,), (0,)), ((), ()))
NT = (((1,), (1,)), ((), ()))
TN = (((0,), (0,)), ((), ()))


def _dot(a, b, dims):
    return lax.dot_general(a, b, dims, preferred_element_type=F32)


def _cparams(sem, vmem=VMEM_BIG):
    return pltpu.CompilerParams(dimension_semantics=sem, vmem_limit_bytes=vmem)


def _pick(dim, prefs):
    for p in prefs:
        if dim % p == 0:
            return p
    return dim


def _my_id():
    return lax.axis_index("x") * 4 + lax.axis_index("y") * 2 + lax.axis_index("c")


def _sigmoid(x):
    return 0.5 * jnp.tanh(0.5 * x) + 0.5


def _peers():
    mx, my, mc = lax.axis_index("x"), lax.axis_index("y"), lax.axis_index("c")
    out = []
    for k in range(1, N_DEV):
        kx, ky, kc = (k >> 2) & 1, (k >> 1) & 1, k & 1
        px = 1 - mx if kx else mx
        py = 1 - my if ky else my
        pc = 1 - mc if kc else mc
        out.append(((px, py, pc), px * 4 + py * 2 + pc))
    return out


def _exchange_copies(kind, x_ref, o_ref, ssem, rsem, lsem):
    me = _my_id()
    loc = pltpu.make_async_copy(x_ref if kind == "ag" else x_ref.at[me], o_ref.at[me], lsem)
    cps = []
    for k, (peer, pid) in enumerate(_peers()):
        cps.append(pltpu.make_async_remote_copy(
            src_ref=x_ref if kind == "ag" else x_ref.at[pid], dst_ref=o_ref.at[me],
            send_sem=ssem.at[k], recv_sem=rsem.at[k], device_id=peer, device_id_type=pl.DeviceIdType.MESH))
    return loc, cps


def _two_level_copies(x_ref, o_ref, ssem, rsem, lsem):
    mx, my, mc = lax.axis_index("x"), lax.axis_index("y"), lax.axis_index("c")
    me = mx * 4 + my * 2 + mc
    sibling = (mx, my, 1 - mc)
    chips = [(1 - mx, my), (mx, 1 - my), (1 - mx, 1 - my)]

    def copy(k, slot, to, src=None):
        return pltpu.make_async_remote_copy(
            src_ref=o_ref.at[slot] if src is None else src, dst_ref=o_ref.at[slot],
            send_sem=ssem.at[k], recv_sem=rsem.at[k], device_id=to, device_id_type=pl.DeviceIdType.MESH)

    loc = pltpu.make_async_copy(x_ref, o_ref.at[me], lsem)
    first = [copy(0, me, sibling, src=x_ref)]
    first += [copy(1 + j, me, (cx, cy, mc), src=x_ref) for j, (cx, cy) in enumerate(chips)]
    passed = [copy(4 + j, cx * 4 + cy * 2 + mc, sibling) for j, (cx, cy) in enumerate(chips)]
    return loc, first, passed


def _exchange_start(kind, x_ref, o_ref, ssem, rsem, lsem):
    if kind == "ag2":
        loc, first, _ = _two_level_copies(x_ref, o_ref, ssem, rsem, lsem)
        cps = first
    else:
        loc, cps = _exchange_copies(kind, x_ref, o_ref, ssem, rsem, lsem)
    loc.start()
    for cp in cps:
        cp.start()


def _exchange_pass_on(kind, x_ref, o_ref, ssem, rsem, lsem):
    if kind == "ag2":
        _, first, passed = _two_level_copies(x_ref, o_ref, ssem, rsem, lsem)
        for j in range(3):
            first[1 + j].wait_recv()
            passed[j].start()


def _exchange_wait(kind, x_ref, o_ref, ssem, rsem, lsem):
    if kind == "ag2":
        loc, first, passed = _two_level_copies(x_ref, o_ref, ssem, rsem, lsem)
        first[0].wait_recv()
        for cp in passed:
            cp.wait_recv()
        cps = first + passed
    else:
        loc, cps = _exchange_copies(kind, x_ref, o_ref, ssem, rsem, lsem)
        for cp in cps:
            cp.wait_recv()
    for cp in cps:
        cp.wait_send()
    loc.wait()


_EXCHANGE_SEMS = [pltpu.SemaphoreType.DMA((N_DEV - 1,)), pltpu.SemaphoreType.DMA((N_DEV - 1,)),
                  pltpu.SemaphoreType.DMA(())]


def _exchange_shape(kind, x):
    return jax.ShapeDtypeStruct(x.shape if kind == "a2a" else (N_DEV,) + x.shape, x.dtype)


def _exchange(kind, x, name):
    def body(x_ref, o_ref, ssem, rsem, lsem):
        _exchange_start(kind, x_ref, o_ref, ssem, rsem, lsem)
        _exchange_pass_on(kind, x_ref, o_ref, ssem, rsem, lsem)
        _exchange_wait(kind, x_ref, o_ref, ssem, rsem, lsem)

    return pl.pallas_call(
        body, name=name, out_shape=_exchange_shape(kind, x),
        in_specs=[pl.BlockSpec(memory_space=pl.ANY)], out_specs=pl.BlockSpec(memory_space=pl.ANY),
        scratch_shapes=list(_EXCHANGE_SEMS),
    )(x)


def _allgather(x, name):
    return _exchange("ag", x, name)


def _call(body, name, grid, in_specs, out_specs, out_shape, scratch_shapes, sem, args, comm=(), aliases=None,
          pass_on_at=0.75):
    in_specs, out_specs, out_shape = list(in_specs), list(out_specs), list(out_shape)
    scratch_shapes = list(scratch_shapes)
    aliases = aliases or {}
    if not comm:
        outs = pl.pallas_call(body, name=name, grid=grid, in_specs=in_specs, out_specs=out_specs, out_shape=out_shape,
                              scratch_shapes=scratch_shapes, input_output_aliases=aliases,
                              compiler_params=_cparams(sem))(*args)
        return list(outs), []
    n_in, n_out, n_scr, n_c = len(in_specs), len(out_specs), len(scratch_shapes), len(comm)
    hbm = pl.BlockSpec(memory_space=pl.ANY)

    def wrapped(*refs):
        ins, cins = refs[:n_in], refs[n_in:n_in + n_c]
        outs = refs[n_in + n_c:n_in + n_c + n_out]
        couts = refs[n_in + n_c + n_out:n_in + 2 * n_c + n_out]
        scr = refs[n_in + 2 * n_c + n_out:n_in + 2 * n_c + n_out + n_scr]
        sems = refs[n_in + 2 * n_c + n_out + n_scr:]
        step, total = pl.program_id(0), grid[0]
        for ax in range(1, len(grid)):
            step = step * grid[ax] + pl.program_id(ax)
            total *= grid[ax]

        @pl.when(step == 0)
        def _():
            for c, (kind, _) in enumerate(comm):
                _exchange_start(kind, cins[c], couts[c], *sems[3 * c:3 * c + 3])

        body(*ins, *outs, *scr)

        @pl.when(step == min(total - 1, int(total * pass_on_at)))
        def _():
            for c, (kind, _) in enumerate(comm):
                _exchange_pass_on(kind, cins[c], couts[c], *sems[3 * c:3 * c + 3])

        @pl.when(step == total - 1)
        def _():
            for c, (kind, _) in enumerate(comm):
                _exchange_wait(kind, cins[c], couts[c], *sems[3 * c:3 * c + 3])

    res = pl.pallas_call(
        wrapped, name=name, grid=grid,
        in_specs=in_specs + [hbm] * n_c, out_specs=out_specs + [hbm] * n_c,
        out_shape=out_shape + [_exchange_shape(kind, arr) for kind, arr in comm],
        scratch_shapes=scratch_shapes + list(_EXCHANGE_SEMS) * n_c, input_output_aliases=aliases,
        compiler_params=_cparams(("arbitrary",) * len(grid)),
    )(*args, *[arr for _, arr in comm])
    return list(res[:n_out]), list(res[n_out:])


def _matmul(name, pairs, n_acc, M, N, K, mode, tiles, extras, out_dtypes, epilogue, j_outer=False, comm=()):
    tm, tn, tk = tiles
    gm, gn, nk = M // tm, N // tn, K // tk
    assert gm * tm == M and gn * tn == N and nk * tk == K, (name, M, N, K, tiles)
    if j_outer:
        grid = (gn, gm, nk)
        ij = lambda g0, g1: (g1, g0)
    else:
        grid = (gm, gn, nk)
        ij = lambda g0, g1: (g0, g1)

    if mode in ("nn", "nt"):
        a_spec = pl.BlockSpec((tm, tk), lambda g0, g1, k: (ij(g0, g1)[0], k))
    else:
        a_spec = pl.BlockSpec((tk, tm), lambda g0, g1, k: (k, ij(g0, g1)[0]))
    if mode == "nt":
        b_spec = pl.BlockSpec((tn, tk), lambda g0, g1, k: (ij(g0, g1)[1], k))
    else:
        b_spec = pl.BlockSpec((tk, tn), lambda g0, g1, k: (k, ij(g0, g1)[1]))
    dims = {"nn": NN, "nt": NT, "tn": TN}[mode]
    mn_spec = pl.BlockSpec((tm, tn), lambda g0, g1, k: ij(g0, g1))
    n_spec = pl.BlockSpec((1, tn), lambda g0, g1, k: (0, ij(g0, g1)[1]))

    in_specs, args = [], []
    for a, b, _ in pairs:
        in_specs += [a_spec, b_spec]
        args += [a, b]
    for arr, kind in extras:
        in_specs.append(mn_spec if kind == "mn" else n_spec)
        args.append(arr)
    n_p, n_e, n_o = len(pairs), len(extras), len(out_dtypes)

    def body(*refs):
        ab = refs[:2 * n_p]
        ex = refs[2 * n_p:2 * n_p + n_e]
        outs = refs[2 * n_p + n_e:2 * n_p + n_e + n_o]
        accs = refs[2 * n_p + n_e + n_o:]
        k = pl.program_id(2)

        def single_step():
            rc = _pick(tm, (1024,)) if mode != "tn" else tm
            for c in range(tm // rc):
                rs = slice(c * rc, (c + 1) * rc)
                sums = [None] * n_acc
                for p, (_, _, ai) in enumerate(pairs):
                    a_ref, b_ref = ab[2 * p], ab[2 * p + 1]
                    d = _dot(a_ref[...] if mode == "tn" else a_ref[rs, :], b_ref[...], dims)
                    sums[ai] = d if sums[ai] is None else sums[ai] + d
                res = epilogue(sums, [e[rs, :] if e.shape[0] == tm else e[...] for e in ex])
                for o, r in zip(outs, res):
                    o[rs, :] = r.astype(o.dtype)

        def finish(acc_vals):
            res = epilogue(acc_vals, [e[...] for e in ex])
            for o, r in zip(outs, res):
                o[...] = r.astype(o.dtype)

        def accumulate(first):
            w = _pick(tm if mode == "tn" else tn, (512, 384, 256))
            for c in range((tm if mode == "tn" else tn) // w):
                sl = slice(c * w, (c + 1) * w)
                sums = [None] * n_acc
                for p, (_, _, ai) in enumerate(pairs):
                    a_ref, b_ref = ab[2 * p], ab[2 * p + 1]
                    if mode == "tn":
                        d = _dot(a_ref[:, sl], b_ref[...], dims)
                    elif mode == "nn":
                        d = _dot(a_ref[...], b_ref[:, sl], dims)
                    else:
                        d = _dot(a_ref[...], b_ref[sl, :], dims)
                    sums[ai] = d if sums[ai] is None else sums[ai] + d
                idx = (sl, slice(None)) if mode == "tn" else (slice(None), sl)
                for ai, s in enumerate(sums):
                    if first:
                        accs[ai][idx] = s
                    else:
                        accs[ai][idx] += s

        if nk == 1:
            single_step()
        else:
            pl.when(k == 0)(functools.partial(accumulate, True))
            pl.when(k > 0)(functools.partial(accumulate, False))

            @pl.when(k == nk - 1)
            def _():
                finish([a[...] for a in accs])

    outs, couts = _call(
        body, name, grid, in_specs, [mn_spec] * n_o,
        [jax.ShapeDtypeStruct((M, N), dt) for dt in out_dtypes],
        [pltpu.VMEM((tm, tn), F32) for _ in range(n_acc if nk > 1 else 0)],
        ("parallel", "parallel", "arbitrary"), args, comm)
    return (outs, couts) if comm else outs


def _rope_tables(L):
    t = jnp.arange(L, dtype=jnp.int32)
    f = jnp.arange(32, dtype=jnp.int32).astype(F32)
    ang = t.astype(F32)[:, None] * (ROPE_BASE ** (-f / 32.0))[None, :]
    cos, sin = jnp.cos(ang), jnp.sin(ang)
    ret = jnp.stack([jnp.tile(cos, (1, 4)), jnp.tile(jnp.concatenate([-sin, sin], axis=1), (1, 2))])
    f2 = jnp.arange(16, dtype=jnp.int32).astype(F32)
    inv2 = (ROPE_BASE ** (-f2 / 16.0))[None, :]
    ang_r = (t // GRID_W).astype(F32)[:, None] * inv2
    ang_c = (t % GRID_W).astype(F32)[:, None] * inv2
    cr, sr, cc, sc = jnp.cos(ang_r), jnp.sin(ang_r), jnp.cos(ang_c), jnp.sin(ang_c)
    att = jnp.stack([jnp.tile(jnp.concatenate([cr, cr, cc, cc], axis=1), (1, 2)),
                     jnp.tile(jnp.concatenate([-sr, sr, -sc, sc], axis=1), (1, 2))])
    return ret.astype(F32), att.astype(F32)


def _swap(x, sh):
    lane = lax.broadcasted_iota(jnp.int32, x.shape, 1)
    ra = pltpu.roll(x, LANES - sh, 1)
    rb = pltpu.roll(x, sh, 1)
    la = pltpu.roll(lane, LANES - sh, 1)
    partner = jnp.where((lane % (2 * sh)) < sh, lane + sh, lane - sh)
    return jnp.where(la == partner, ra, rb)


def _rope(x, cos, sin, sh):
    return x * cos + _swap(x, sh) * sin


def _rope_t(d, cos, sin, sh):
    return d * cos + _swap(d * sin, sh)


def _half_mask(shape, a):
    lane = lax.broadcasted_iota(jnp.int32, shape, 1)
    return (lane < 64) if a == 0 else (lane >= 64)


def _mod_fwd(s_in, w_l, b_l):
    D, C6 = w_l.shape
    tk = _pick(D, (512, 256, 128))
    nk = D // tk

    def body(s_ref, w_ref, b_ref, o_ref):
        k = pl.program_id(0)
        s = s_ref[...]
        s = s * _sigmoid(s)
        d = jnp.dot(s, w_ref[...], preferred_element_type=F32, precision=lax.Precision.HIGHEST)

        @pl.when(k == 0)
        def _():
            o_ref[...] = d + b_ref[...]

        @pl.when(k > 0)
        def _():
            o_ref[...] += d

    return pl.pallas_call(
        body, name="mod_fwd", grid=(nk,),
        in_specs=[pl.BlockSpec((16, tk), lambda k: (0, k)), pl.BlockSpec((tk, C6), lambda k: (k, 0)),
                  pl.BlockSpec((1, C6), lambda k: (0, 0))],
        out_specs=pl.BlockSpec((16, C6), lambda k: (0, 0)),
        out_shape=jax.ShapeDtypeStruct((16, C6), F32),
        compiler_params=_cparams(("arbitrary",)),
    )(s_in, w_l, b_l)


def _mod_bwd(s_in, dm, w_l):
    D, C6 = w_l.shape
    tk = _pick(D, (512, 256, 128))
    nk = D // tk

    def body(s_ref, dm_ref, w_ref, gw_ref, gc_ref):
        s = s_ref[...]
        sg = _sigmoid(s)
        act = s * sg
        dmv = dm_ref[...]
        gw_ref[...] = lax.dot_general(act, dmv, TN, preferred_element_type=F32, precision=lax.Precision.HIGHEST)
        ds = lax.dot_general(dmv, w_ref[...], NT, preferred_element_type=F32, precision=lax.Precision.HIGHEST)
        dsil = (sg * (1.0 + s * (1.0 - sg)))[8:9, :]
        gc_ref[...] = jnp.zeros((8, tk), F32) + jnp.sum(ds[8:16, :], axis=0, keepdims=True) * dsil

    return pl.pallas_call(
        body, name="mod_bwd", grid=(nk,),
        in_specs=[pl.BlockSpec((16, tk), lambda k: (0, k)), pl.BlockSpec((16, C6), lambda k: (0, 0)),
                  pl.BlockSpec((tk, C6), lambda k: (k, 0))],
        out_specs=[pl.BlockSpec((tk, C6), lambda k: (k, 0)), pl.BlockSpec((8, tk), lambda k: (0, k))],
        out_shape=[jax.ShapeDtypeStruct((D, C6), F32), jax.ShapeDtypeStruct((8, D), F32)],
        compiler_params=_cparams(("parallel",)),
    )(s_in, dm, w_l)


def _norm_rows(x):
    r = lax.rsqrt(jnp.mean(x * x, axis=-1, keepdims=True) + EPS)
    return x * r, r


def _modulate_fwd(name, x, ctx, g, mod, modc, comm=()):
    L, D = x.shape
    tr = ctx.shape[0]
    nx = L // tr

    def body(x_ref, c_ref, g_ref, m_ref, mc_ref, o_ref):
        i = pl.program_id(0)

        def run(src, m):
            n, _ = _norm_rows(src[...])
            o_ref[...] = (n * g_ref[...] * (1.0 + m[1:2, :]) + m[0:1, :]).astype(o_ref.dtype)

        @pl.when(i < nx)
        def _():
            run(x_ref, m_ref)

        @pl.when(i >= nx)
        def _():
            run(c_ref, mc_ref)

    row = pl.BlockSpec((tr, D), lambda i: (jnp.minimum(i, nx - 1), 0))
    vec = pl.BlockSpec((1, D), lambda i: (0, 0))
    mv = pl.BlockSpec((8, D), lambda i: (0, 0))
    return _call(
        body, name, (nx + 1,), [row, pl.BlockSpec((tr, D), lambda i: (0, 0)), vec, mv, mv],
        [pl.BlockSpec((tr, D), lambda i: (i, 0))], [jax.ShapeDtypeStruct((L + tr, D), BF16)], [],
        ("parallel",), (x, ctx, g, mod, modc), comm, pass_on_at=1.0)


def _residual_modulate_fwd(name, x, fbr, gate, g, mod):
    L, D = x.shape
    tr = _pick(L, (512, 256, 128))

    def body(x_ref, f_ref, gt_ref, g_ref, m_ref, x1_ref, o_ref):
        x1 = x_ref[...] + gt_ref[...] * f_ref[...].astype(F32)
        x1_ref[...] = x1
        n, _ = _norm_rows(x1)
        o_ref[...] = (n * g_ref[...] * (1.0 + m_ref[1:2, :]) + m_ref[0:1, :]).astype(o_ref.dtype)

    row = pl.BlockSpec((tr, D), lambda i: (i, 0))
    vec = pl.BlockSpec((1, D), lambda i: (0, 0))
    return pl.pallas_call(
        body, name=name, grid=(L // tr,),
        in_specs=[row, row, vec, vec, pl.BlockSpec((8, D), lambda i: (0, 0))],
        out_specs=[row, row],
        out_shape=[jax.ShapeDtypeStruct((L, D), F32), jax.ShapeDtypeStruct((L, D), BF16)],
        compiler_params=_cparams(("parallel",)),
    )(x, fbr, gate, g, mod)


def _modulate_bwd(name, x, ctx, dh, g, mod, modc, dres, fbr, gate):
    L, D = x.shape
    tr = ctx.shape[0] if ctx is not None else _pick(L, (256, 128))
    nx = L // tr
    nt = nx + (1 if ctx is not None else 0)
    has_f = fbr is not None

    def body(*refs):
        refs = list(refs)
        x_ref = refs.pop(0)
        c_ref = refs.pop(0) if ctx is not None else None
        dh_ref, g_ref, m_ref = refs.pop(0), refs.pop(0), refs.pop(0)
        mc_ref = refs.pop(0) if ctx is not None else None
        dr_ref = refs.pop(0)
        f_ref = refs.pop(0) if has_f else None
        gt_ref = refs.pop(0) if has_f else None
        dx_ref = refs.pop(0)
        df_ref = refs.pop(0) if has_f else None
        acc_ref = refs.pop(0)
        i = pl.program_id(0)

        @pl.when(i == 0)
        def _():
            acc_ref[...] = jnp.zeros_like(acc_ref)

        def sums(src, m, base, grow):
            n, r = _norm_rows(src[...])
            d = dh_ref[...].astype(F32)
            gg = g_ref[...]
            sc1 = 1.0 + m[1:2, :]
            acc_ref[base:base + 1, :] += jnp.sum(d, axis=0, keepdims=True)
            dn = d * n
            acc_ref[base + 1:base + 2, :] += jnp.sum(dn, axis=0, keepdims=True) * gg
            acc_ref[grow:grow + 1, :] += jnp.sum(dn, axis=0, keepdims=True) * sc1
            dnv = d * (gg * sc1)
            return r * (dnv - n * jnp.mean(dnv * n, axis=-1, keepdims=True))

        def x_rows():
            dx = sums(x_ref, m_ref, 0, 2) + dr_ref[...]
            dx_ref[...] = dx
            if has_f:
                acc_ref[6:7, :] += jnp.sum(dx * f_ref[...].astype(F32), axis=0, keepdims=True)
                df_ref[...] = (dx * gt_ref[...]).astype(df_ref.dtype)

        if ctx is None:
            x_rows()
        else:
            pl.when(i < nx)(x_rows)

            @pl.when(i >= nx)
            def _():
                sums(c_ref, mc_ref, 3, 2)

    row = pl.BlockSpec((tr, D), lambda i: (jnp.minimum(i, nx - 1), 0))
    vec = pl.BlockSpec((1, D), lambda i: (0, 0))
    mv = pl.BlockSpec((8, D), lambda i: (0, 0))
    in_specs, args = [row], [x]
    if ctx is not None:
        in_specs.append(pl.BlockSpec((tr, D), lambda i: (0, 0)))
        args.append(ctx)
    in_specs += [pl.BlockSpec((tr, D), lambda i: (i, 0)), vec, mv]
    args += [dh, g, mod]
    if ctx is not None:
        in_specs.append(mv)
        args.append(modc)
    in_specs.append(row)
    args.append(dres)
    out_specs = [row]
    out_shape = [jax.ShapeDtypeStruct((L, D), F32)]
    if has_f:
        in_specs += [row, vec]
        args += [fbr, gate]
        out_specs.append(row)
        out_shape.append(jax.ShapeDtypeStruct((L, D), BF16))
    out_specs.append(pl.BlockSpec((16, D), lambda i: (0, 0)))
    out_shape.append(jax.ShapeDtypeStruct((16, D), F32))
    return pl.pallas_call(
        body, name=name, grid=(nt,), in_specs=in_specs, out_specs=out_specs, out_shape=out_shape,
        compiler_params=_cparams(("arbitrary",)),
    )(*args)


def _loss_head(x1, tgt, nf, fbr, gate):
    L, D = x1.shape
    tr = _pick(L, (256, 128))

    def body(x_ref, t_ref, w_ref, f_ref, gt_ref, dx_ref, df_ref, acc_ref):
        i = pl.program_id(0)

        @pl.when(i == 0)
        def _():
            acc_ref[...] = jnp.zeros_like(acc_ref)

        n, r = _norm_rows(x_ref[...] + gt_ref[...] * f_ref[...].astype(F32))
        w = w_ref[...]
        e = n * w - t_ref[...]
        acc_ref[0:1, :] += jnp.sum(e * e, axis=0, keepdims=True) * (0.5 / D)
        dout = e * (1.0 / D)
        acc_ref[1:2, :] += jnp.sum(dout * n, axis=0, keepdims=True)
        dn = dout * w
        dx = r * (dn - n * jnp.mean(dn * n, axis=-1, keepdims=True))
        dx_ref[...] = dx
        acc_ref[2:3, :] += jnp.sum(dx * f_ref[...].astype(F32), axis=0, keepdims=True)
        df_ref[...] = (dx * gt_ref[...]).astype(df_ref.dtype)

        @pl.when(i == pl.num_programs(0) - 1)
        def _():
            acc_ref[3:4, :] = jnp.zeros((1, D), F32) + jnp.sum(acc_ref[0:1, :])

    row = pl.BlockSpec((tr, D), lambda i: (i, 0))
    vec = pl.BlockSpec((1, D), lambda i: (0, 0))
    return pl.pallas_call(
        body, name="loss_head", grid=(L // tr,),
        in_specs=[row, row, vec, row, vec],
        out_specs=[row, row, pl.BlockSpec((8, D), lambda i: (0, 0))],
        out_shape=[jax.ShapeDtypeStruct((L, D), F32), jax.ShapeDtypeStruct((L, D), BF16),
                   jax.ShapeDtypeStruct((8, D), F32)],
        compiler_params=_cparams(("arbitrary",)),
    )(x1, tgt, nf, fbr, gate)


RET_SUB = 4
N_TAB = 7


def _ret_tables(rdb, Lc):
    def body(rd_ref, t_ref, c_ref):
        d = pl.program_id(0) // RET_HEADS
        fwd = d == 0
        lg = -jnp.exp(rd_ref[0])
        i = lax.broadcasted_iota(jnp.int32, (CHUNK, CHUNK), 0).astype(F32)
        j = lax.broadcasted_iota(jnp.int32, (CHUNK, CHUNK), 1).astype(F32)
        rel = jnp.where(fwd, i - j, j - i)
        mask = (rel > 0.0) | ((rel == 0.0) & fwd)
        dm = jnp.where(mask, jnp.exp(lg * jnp.maximum(rel, 0.0)), 0.0)
        t_ref[0, 0] = dm
        t_ref[0, 1] = rel * dm
        qc = jnp.where(fwd, i + 1.0, CHUNK - i)
        qw = jnp.exp(lg * qc)
        t_ref[0, 2] = qw
        t_ref[0, 3] = qw * qc
        kc = jnp.where(fwd, CHUNK - 1.0 - i, i)
        kw = jnp.exp(lg * kc)
        t_ref[0, 4] = kw
        t_ref[0, 5] = kw * kc
        t_ref[0, 6] = jnp.exp(lg * float(CHUNK)) + jnp.zeros((CHUNK, CHUNK), F32)
        m = lax.broadcasted_iota(jnp.int32, (Lc, LANES), 0).astype(F32)
        cc = jnp.where(fwd, Lc - 1.0 - m, m)
        cw = jnp.exp(lg * cc)
        c_ref[0, 0] = cw
        c_ref[0, 1] = cw * cc

    return pl.pallas_call(
        body, name="ret_tables", grid=(2 * RET_HEADS,),
        in_specs=[pl.BlockSpec((1, 1, LANES), lambda r: (r, 0, 0))],
        out_specs=[pl.BlockSpec((1, N_TAB, CHUNK, CHUNK), lambda r: (r, 0, 0, 0)),
                   pl.BlockSpec((1, 2, Lc, LANES), lambda r: (r, 0, 0, 0))],
        out_shape=[jax.ShapeDtypeStruct((2 * RET_HEADS, N_TAB, CHUNK, CHUNK), F32),
                   jax.ShapeDtypeStruct((2 * RET_HEADS, 2, Lc, LANES), F32)],
        compiler_params=_cparams(("parallel",)),
    )(rdb)


def _ret_ctx_state(P, ctab, L, Lc):
    cb = L // Lc

    def body(k_ref, v_ref, c_ref, s_ref):
        for p in range(RET_HEADS // 2):
            kp = k_ref[:, p * LANES:(p + 1) * LANES].astype(F32) * K_SCALE
            for a in range(2):
                h = 2 * p + a
                kh = jnp.where(_half_mask(kp.shape, a), kp, 0.0)
                vh = v_ref[:, h * RET_DV:(h + 1) * RET_DV]
                for d in range(2):
                    kw = (kh * c_ref[d * RET_HEADS + h, 0]).astype(BF16)
                    s_ref[d * RET_HEADS + h] = _dot(kw, vh, TN)

    return pl.pallas_call(
        body, name="ret_ctx_state", grid=(1,),
        in_specs=[pl.BlockSpec((Lc, 512), lambda i: (cb, C_RK // 512)),
                  pl.BlockSpec((Lc, 1024), lambda i: (cb, C_RV // 1024)),
                  pl.BlockSpec((2 * RET_HEADS, 2, Lc, LANES), lambda i: (0, 0, 0, 0))],
        out_specs=pl.BlockSpec((2 * RET_HEADS, LANES, RET_DV), lambda i: (0, 0, 0)),
        out_shape=jax.ShapeDtypeStruct((2 * RET_HEADS, LANES, RET_DV), F32),
        compiler_params=_cparams(("arbitrary",)),
    )(P, P, ctab)


def _ret_fwd(P, rope, tabs, s0, L, comm=()):
    n = L // CHUNK
    nb = n // RET_SUB

    def body(qf, kf, vf, rf, qb, kb, vb, rb, t_ref, s0_ref, of_ref, ob_ref, stf_ref, stb_ref, st):
        s = pl.program_id(0)

        @pl.when(s == 0)
        def _():
            st[...] = s0_ref[...]

        for rnd in range(RET_SUB):
            units = []
            for d, (q_ref, k_ref, v_ref, r_ref, o_ref, so_ref) in enumerate(
                    ((qf, kf, vf, rf, of_ref, stf_ref), (qb, kb, vb, rb, ob_ref, stb_ref))):
                j = rnd if d == 0 else RET_SUB - 1 - rnd
                rows = slice(j * CHUNK, (j + 1) * CHUNK)
                cos, sin = r_ref[0, rows, :], r_ref[1, rows, :]
                for p in range(RET_HEADS // 2):
                    qp = _rope(q_ref[rows, p * LANES:(p + 1) * LANES].astype(F32), cos, sin, 32)
                    kp = _rope(k_ref[rows, p * LANES:(p + 1) * LANES].astype(F32), cos, sin, 32) * K_SCALE
                    for a in range(2):
                        h = 2 * p + a
                        hm = _half_mask(qp.shape, a)
                        units.append(dict(r=d * RET_HEADS + h, h=h, a=a, j=j, rows=rows, o_ref=o_ref, so_ref=so_ref,
                                          v_ref=v_ref, qh=jnp.where(hm, qp, 0.0), kh=jnp.where(hm, kp, 0.0)))
            for u in units:
                u["sc"] = _dot(u["qh"].astype(BF16), u["kh"].astype(BF16), NT)
            for u in units:
                r, h = u["r"], u["h"]
                sp = st[r]
                u["so_ref"][u["j"], h] = sp[u["a"] * RET_DK:(u["a"] + 1) * RET_DK, :]
                vh = u["v_ref"][u["rows"], h * RET_DV:(h + 1) * RET_DV]
                o = _dot((u["sc"] * t_ref[r, 0]).astype(BF16), vh, NN)
                o += _dot((u["qh"] * t_ref[r, 2]).astype(BF16), sp.astype(BF16), NN)
                u["o_ref"][u["rows"], h * RET_DV:(h + 1) * RET_DV] = o
            for u in units:
                r, h = u["r"], u["h"]
                vh = u["v_ref"][u["rows"], h * RET_DV:(h + 1) * RET_DV]
                st[r] = t_ref[r, 6] * st[r] + _dot((u["kh"] * t_ref[r, 4]).astype(BF16), vh, TN)

    fw = lambda s: s
    bw = lambda s: nb - 1 - s
    RB = RET_SUB * CHUNK

    def specs(cm):
        return [pl.BlockSpec((RB, 512), lambda s: (cm(s), C_RQ // 512)),
                pl.BlockSpec((RB, 512), lambda s: (cm(s), C_RK // 512)),
                pl.BlockSpec((RB, 1024), lambda s: (cm(s), C_RV // 1024)),
                pl.BlockSpec((2, RB, LANES), lambda s: (0, cm(s), 0))]

    full = lambda shp: pl.BlockSpec(shp, lambda s: (0,) * len(shp))
    return _call(
        body, "ret_fwd", (nb,),
        specs(fw) + specs(bw) + [full((2 * RET_HEADS, N_TAB, CHUNK, CHUNK)), full((2 * RET_HEADS, LANES, RET_DV))],
        [pl.BlockSpec((RB, 1024), lambda s: (fw(s), 0)),
         pl.BlockSpec((RB, 1024), lambda s: (bw(s), 0)),
         pl.BlockSpec((RET_SUB, RET_HEADS, RET_DK, RET_DV), lambda s: (fw(s), 0, 0, 0)),
         pl.BlockSpec((RET_SUB, RET_HEADS, RET_DK, RET_DV), lambda s: (bw(s), 0, 0, 0))],
        [jax.ShapeDtypeStruct((L, 1024), F32), jax.ShapeDtypeStruct((L, 1024), F32),
         jax.ShapeDtypeStruct((n, RET_HEADS, RET_DK, RET_DV), F32),
         jax.ShapeDtypeStruct((n, RET_HEADS, RET_DK, RET_DV), F32)],
        [pltpu.VMEM((2 * RET_HEADS, LANES, RET_DV), F32)],
        ("arbitrary",), (P, P, P, rope, P, P, P, rope, tabs, s0), comm)


def _ret_finish_fwd(of, ob, P, L):
    tr = _pick(L, (512, 256, 128))

    def body(f_ref, b_ref, g_ref, y_ref):
        for h in range(RET_HEADS):
            sl = slice(h * RET_DV, (h + 1) * RET_DV)
            n, _ = _norm_rows(f_ref[:, sl] + b_ref[:, sl])
            g = g_ref[:, sl].astype(F32)
            y_ref[:, sl] = (n * (g * _sigmoid(g))).astype(y_ref.dtype)

    row = pl.BlockSpec((tr, 1024), lambda i: (i, 0))
    return pl.pallas_call(
        body, name="ret_finish_fwd", grid=(L // tr,),
        in_specs=[row, row, pl.BlockSpec((tr, 1024), lambda i: (i, C_RG // 1024))],
        out_specs=row, out_shape=jax.ShapeDtypeStruct((L, 2048), BF16),
        compiler_params=_cparams(("parallel",)),
    )(of, ob, P)


def _ret_finish_bwd(of, ob, P, dY, L):
    tr = _pick(L, (512, 256, 128))

    def body(f_ref, b_ref, g_ref, dy_ref, do_ref, dg_ref):
        for h in range(RET_HEADS):
            sl = slice(h * RET_DV, (h + 1) * RET_DV)
            n, r = _norm_rows(f_ref[:, sl] + b_ref[:, sl])
            g = g_ref[:, sl].astype(F32)
            sg = _sigmoid(g)
            dy = dy_ref[:, sl].astype(F32)
            dg_ref[:, sl] = (dy * n * (sg * (1.0 + g * (1.0 - sg)))).astype(dg_ref.dtype)
            dn = dy * (g * sg)
            do_ref[:, sl] = (r * (dn - n * jnp.mean(dn * n, axis=-1, keepdims=True))).astype(do_ref.dtype)

    row = pl.BlockSpec((tr, 1024), lambda i: (i, 0))
    return pl.pallas_call(
        body, name="ret_finish_bwd", grid=(L // tr,),
        in_specs=[row, row, pl.BlockSpec((tr, 1024), lambda i: (i, C_RG // 1024)), row],
        out_specs=[row, row],
        out_shape=[jax.ShapeDtypeStruct((L, 1024), BF16), jax.ShapeDtypeStruct((L, 1024), BF16)],
        compiler_params=_cparams(("parallel",)),
    )(of, ob, P, dY)


def _ret_bwd(P, rope, tabs, stf, stb, dO, L, comm=()):
    n = L // CHUNK
    nb = n // RET_SUB

    def body(qf, kf, vf, rf, gf, sf, qb, kb, vb, rb, gb, sb, t_ref,
             dqf, dkf, dvf, dqb, dkb, dvb, ds0_ref, dlg_ref, ds):
        s = pl.program_id(0)

        @pl.when(s == 0)
        def _():
            ds[...] = jnp.zeros_like(ds)
            dlg_ref[...] = jnp.zeros_like(dlg_ref)

        for rnd in range(RET_SUB):
            units, pairs = [], []
            for d, (q_ref, k_ref, v_ref, r_ref, g_ref, s_ref, dq_ref, dk_ref, dv_ref) in enumerate(
                    ((qf, kf, vf, rf, gf, sf, dqf, dkf, dvf), (qb, kb, vb, rb, gb, sb, dqb, dkb, dvb))):
                j = RET_SUB - 1 - rnd if d == 0 else rnd
                rows = slice(j * CHUNK, (j + 1) * CHUNK)
                cos, sin = r_ref[0, rows, :], r_ref[1, rows, :]
                for p in range(RET_HEADS // 2):
                    qp = _rope(q_ref[rows, p * LANES:(p + 1) * LANES].astype(F32), cos, sin, 32)
                    kp = _rope(k_ref[rows, p * LANES:(p + 1) * LANES].astype(F32), cos, sin, 32) * K_SCALE
                    pair = dict(p=p, rows=rows, cos=cos, sin=sin, dq_ref=dq_ref, dk_ref=dk_ref, us=[])
                    pairs.append(pair)
                    for a in range(2):
                        h = 2 * p + a
                        r = d * RET_HEADS + h
                        hm = _half_mask(qp.shape, a)
                        zero = jnp.zeros((RET_DK, RET_DV), F32)
                        sp = s_ref[j, h]
                        u = dict(r=r, h=h, rows=rows, dv_ref=dv_ref,
                                 qh=jnp.where(hm, qp, 0.0), kh=jnp.where(hm, kp, 0.0),
                                 vh=v_ref[rows, h * RET_DV:(h + 1) * RET_DV],
                                 gh=g_ref[rows, h * RET_DV:(h + 1) * RET_DV],
                                 sp=jnp.concatenate([sp, zero] if a == 0 else [zero, sp], axis=0),
                                 dsn=ds[r])
                        u["qhb"], u["khb"] = u["qh"].astype(BF16), u["kh"].astype(BF16)
                        units.append(u)
                        pair["us"].append(u)
            for u in units:
                u["am"] = _dot(u["qhb"], u["khb"], NT)
                u["dar"] = _dot(u["gh"], u["vh"], NT)
                u["xq"] = _dot(u["gh"], u["sp"].astype(BF16), NT)
                u["yk"] = _dot(u["vh"], u["dsn"].astype(BF16), NT)
            for u in units:
                r = u["r"]
                dm = t_ref[r, 0]
                u["da"] = (u["dar"] * dm).astype(BF16)
                u["amd"] = (u["am"] * dm).astype(BF16)
                part = (jnp.sum(u["am"] * u["dar"] * t_ref[r, 1]) + jnp.sum(u["qh"] * t_ref[r, 3] * u["xq"])
                        + jnp.sum(u["kh"] * t_ref[r, 5] * u["yk"])
                        + float(CHUNK) * jnp.sum(t_ref[r, 6] * u["dsn"] * u["sp"]))
                dlg_ref[r:r + 1, :] += jnp.zeros((1, LANES), F32) + part
            for u in units:
                r, h = u["r"], u["h"]
                u["dq"] = _dot(u["da"], u["khb"], NN) + u["xq"] * t_ref[r, 2]
                u["dk"] = _dot(u["da"], u["qhb"], TN) + u["yk"] * t_ref[r, 4]
                u["dv_ref"][u["rows"], h * RET_DV:(h + 1) * RET_DV] = (
                    _dot(u["amd"], u["gh"], TN)
                    + _dot((u["kh"] * t_ref[r, 4]).astype(BF16), u["dsn"].astype(BF16), NN)
                ).astype(u["dv_ref"].dtype)
                ds[r] = t_ref[r, 6] * u["dsn"] + _dot((u["qh"] * t_ref[r, 2]).astype(BF16), u["gh"], TN)
            for pair in pairs:
                sl = slice(pair["p"] * LANES, (pair["p"] + 1) * LANES)
                u0, u1 = pair["us"]
                pair["dq_ref"][pair["rows"], sl] = _rope_t(
                    u0["dq"] + u1["dq"], pair["cos"], pair["sin"], 32).astype(BF16)
                pair["dk_ref"][pair["rows"], sl] = _rope_t(
                    (u0["dk"] + u1["dk"]) * K_SCALE, pair["cos"], pair["sin"], 32).astype(BF16)

        @pl.when(s == nb - 1)
        def _():
            ds0_ref[...] = ds[...]

    fw = lambda s: nb - 1 - s
    bw = lambda s: s
    RB = RET_SUB * CHUNK

    def specs(cm):
        return [pl.BlockSpec((RB, 512), lambda s: (cm(s), C_RQ // 512)),
                pl.BlockSpec((RB, 512), lambda s: (cm(s), C_RK // 512)),
                pl.BlockSpec((RB, 1024), lambda s: (cm(s), C_RV // 1024)),
                pl.BlockSpec((2, RB, LANES), lambda s: (0, cm(s), 0)),
                pl.BlockSpec((RB, 1024), lambda s: (cm(s), 0)),
                pl.BlockSpec((RET_SUB, RET_HEADS, RET_DK, RET_DV), lambda s: (cm(s), 0, 0, 0))]

    def ospecs(cm):
        return [pl.BlockSpec((RB, 512), lambda s: (cm(s), 0)), pl.BlockSpec((RB, 512), lambda s: (cm(s), 0)),
                pl.BlockSpec((RB, 1024), lambda s: (cm(s), 0))]

    oshape = [jax.ShapeDtypeStruct((L, 512), BF16), jax.ShapeDtypeStruct((L, 512), BF16),
              jax.ShapeDtypeStruct((L, 1024), BF16)]
    full = lambda shp: pl.BlockSpec(shp, lambda s: (0,) * len(shp))
    return _call(
        body, "ret_bwd", (nb,),
        specs(fw) + specs(bw) + [full((2 * RET_HEADS, N_TAB, CHUNK, CHUNK))],
        ospecs(fw) + ospecs(bw) + [full((2 * RET_HEADS, LANES, RET_DV)), full((2 * RET_HEADS, LANES))],
        oshape + oshape + [jax.ShapeDtypeStruct((2 * RET_HEADS, LANES, RET_DV), F32),
                           jax.ShapeDtypeStruct((2 * RET_HEADS, LANES), F32)],
        [pltpu.VMEM((2 * RET_HEADS, LANES, RET_DV), F32)],
        ("arbitrary",), (P, P, P, rope, dO, stf, P, P, P, rope, dO, stb, tabs), comm)


def _ret_ctx_bwd(P, ctab, ds0, dlg, rdb, L, Lc):
    cb = L // Lc

    def body(k_ref, v_ref, c_ref, ds_ref, dlg_ref, rd_ref, dk_ref, dv_ref, drd_ref):
        for p in range(RET_HEADS // 2):
            kp = k_ref[:, p * LANES:(p + 1) * LANES].astype(F32) * K_SCALE
            dkp = jnp.zeros((Lc, LANES), F32)
            for a in range(2):
                h = 2 * p + a
                kh = jnp.where(_half_mask(kp.shape, a), kp, 0.0)
                vh = v_ref[:, h * RET_DV:(h + 1) * RET_DV]
                dvh = jnp.zeros((Lc, RET_DV), F32)
                for d in range(2):
                    r = d * RET_HEADS + h
                    dsb = ds_ref[r].astype(BF16)
                    cw, cwc = c_ref[r, 0], c_ref[r, 1]
                    y = _dot(vh, dsb, NT)
                    dkp += y * cw
                    dvh += _dot((kh * cw).astype(BF16), dsb, NN)
                    lg = -jnp.exp(rd_ref[r])
                    drd_ref[r:r + 1, :] = (dlg_ref[r:r + 1, :] + jnp.sum(kh * cwc * y)) * lg
                dv_ref[:, h * RET_DV:(h + 1) * RET_DV] = dvh
            dk_ref[:, p * LANES:(p + 1) * LANES] = dkp * K_SCALE

    full = lambda shp: pl.BlockSpec(shp, lambda i: (0,) * len(shp))
    return pl.pallas_call(
        body, name="ret_ctx_bwd", grid=(1,),
        in_specs=[pl.BlockSpec((Lc, 512), lambda i: (cb, C_RK // 512)),
                  pl.BlockSpec((Lc, 1024), lambda i: (cb, C_RV // 1024)),
                  full((2 * RET_HEADS, 2, Lc, LANES)), full((2 * RET_HEADS, LANES, RET_DV)),
                  full((2 * RET_HEADS, LANES)), full((2 * RET_HEADS, 1, LANES))],
        out_specs=[full((Lc, 512)), full((Lc, 1024)), full((2 * RET_HEADS, LANES))],
        out_shape=[jax.ShapeDtypeStruct((Lc, 512), F32), jax.ShapeDtypeStruct((Lc, 1024), F32),
                   jax.ShapeDtypeStruct((2 * RET_HEADS, LANES), F32)],
        compiler_params=_cparams(("arbitrary",)),
    )(P, P, ctab, ds0, dlg, rdb)


BLK = 128
N_LOC = 3 * BLK


ATT_SUB = 4


def _att_inputs(P, rope, L, Lc):
    n = L // BLK
    cb = L // Lc
    prev = lambda i: jnp.maximum(ATT_SUB * i - 1, 0)
    nxt = lambda i: jnp.minimum(ATT_SUB * i + ATT_SUB, n - 1)
    specs = [pl.BlockSpec((ATT_SUB * BLK, 1024), lambda i: (i, C_AQ // 1024))]
    args = [P]
    for col in (C_AK // 256, C_AV // 256):
        specs += [pl.BlockSpec((BLK, 256), functools.partial(lambda i, col: (prev(i), col), col=col)),
                  pl.BlockSpec((ATT_SUB * BLK, 256), functools.partial(lambda i, col: (i, col), col=col)),
                  pl.BlockSpec((BLK, 256), functools.partial(lambda i, col: (nxt(i), col), col=col)),
                  pl.BlockSpec((Lc, 256), functools.partial(lambda i, col: (cb, col), col=col))]
        args += [P] * 4
    specs += [pl.BlockSpec((2, BLK, LANES), lambda i: (0, prev(i), 0)),
              pl.BlockSpec((2, ATT_SUB * BLK, LANES), lambda i: (0, i, 0)),
              pl.BlockSpec((2, BLK, LANES), lambda i: (0, nxt(i), 0))]
    args += [rope] * 3
    return specs, args


def _att_prep(i, n, refs, Lc):
    q_ref, kp_ref, kc_ref, kn_ref, kx_ref, vp_ref, vc_ref, vn_ref, vx_ref, rp_ref, rc_ref, rn_ref = refs
    cos = jnp.concatenate([rp_ref[0], rc_ref[0], rn_ref[0]], axis=0)
    sin = jnp.concatenate([rp_ref[1], rc_ref[1], rn_ref[1]], axis=0)

    def dup(x):
        xr = pltpu.roll(x, 64, 1)
        return [jnp.where(_half_mask(x.shape, b), x, xr).astype(BF16) for b in range(2)]

    kd = [[] for _ in range(ATT_SUB)]
    vd = [[] for _ in range(ATT_SUB)]
    for t in range(ATT_KV // 2):
        sl = slice(t * LANES, (t + 1) * LANES)
        kl = jnp.concatenate([kp_ref[:, sl], kc_ref[:, sl], kn_ref[:, sl]], axis=0).astype(F32)
        kl = dup(_rope(kl, cos, sin, 16))
        vl = dup(jnp.concatenate([vp_ref[:, sl], vc_ref[:, sl], vn_ref[:, sl]], axis=0).astype(F32))
        kx, vx = dup(kx_ref[:, sl].astype(F32)), dup(vx_ref[:, sl].astype(F32))
        for j in range(ATT_SUB):
            rows = slice(j * BLK, j * BLK + N_LOC)
            for b in range(2):
                kd[j].append(jnp.concatenate([kl[b][rows], kx[b]], axis=0))
                vd[j].append(jnp.concatenate([vl[b][rows], vx[b]], axis=0))
    nk = N_LOC + Lc
    rr = lax.broadcasted_iota(jnp.int32, (BLK, nk), 0)
    ss = lax.broadcasted_iota(jnp.int32, (BLK, nk), 1)
    band = (ss >= rr) & (ss <= rr + 2 * BLK)
    bias4, tabs = [], []
    for j in range(ATT_SUB):
        blk = ATT_SUB * i + j
        lo = jnp.where(blk == 0, BLK, 0)
        hi = jnp.where(blk == n - 1, 2 * BLK, N_LOC)
        bias = jnp.where((ss >= N_LOC) | (band & (ss >= lo) & (ss < hi)), 0.0, NEG)
        bias4.append(jnp.concatenate([bias] * 4, axis=0))
        tabs.append((rc_ref[0, j * BLK:(j + 1) * BLK, :], rc_ref[1, j * BLK:(j + 1) * BLK, :]))
    return kd, vd, bias4, tabs


LOG2E = 1.4426950408889634
LN2 = 0.6931471805599453
Q_SCALE = A_SCALE * LOG2E


def _stack4(ref, rows, g, f=None):
    parts = []
    for jp in range(2):
        t = ref[rows, (2 * g + jp) * LANES:(2 * g + jp + 1) * LANES].astype(F32)
        if f is not None:
            t = f(t)
        for a in range(2):
            parts.append(jnp.where(_half_mask(t.shape, a), t, 0.0))
    return jnp.concatenate(parts, axis=0)


def _unstack4(x4, jp):
    r0 = 2 * jp * BLK
    lo = x4[r0:r0 + BLK]
    hi = x4[r0 + BLK:r0 + 2 * BLK]
    return jnp.where(_half_mask(lo.shape, 0), lo, hi)


def _softmax_parts(s, bias4, sink_ref, g):
    sink_col = LOG2E * jnp.concatenate(
        [jnp.zeros((BLK, 1), F32) + sink_ref[4 * g + r:4 * g + r + 1, 0:1] for r in range(4)], axis=0)
    s = s + bias4
    m = jnp.maximum(jnp.max(s, axis=-1, keepdims=True), sink_col)
    e = jnp.exp2(s - m)
    es = jnp.exp2(sink_col - m)
    return e, es, jnp.sum(e, axis=-1, keepdims=True) + es


def _att_fwd(P, rope, sinkb, Y, L, Lc, comm=()):
    n = L // BLK
    specs, args = _att_inputs(P, rope, L, Lc)

    def body(*refs):
        sink_ref, o_ref = refs[12], refs[14]
        i = pl.program_id(0)
        kd, vd, bias4, tabs = _att_prep(i, n, refs[:12], Lc)
        for j in range(ATT_SUB):
            rows = slice(j * BLK, (j + 1) * BLK)
            cq, sq = tabs[j]
            for g in range(ATT_KV):
                q4 = _stack4(refs[0], rows, g, lambda t: _rope(t, cq, sq, 16) * Q_SCALE).astype(BF16)
                e, _, l = _softmax_parts(_dot(q4, kd[j][g], NT), bias4[j], sink_ref, g)
                o4 = _dot(e.astype(BF16), vd[j][g], NN) * (1.0 / l)
                for jp in range(2):
                    c0 = (2 * g + jp) * LANES
                    o_ref[rows, c0:c0 + LANES] = _unstack4(o4, jp).astype(o_ref.dtype)

    return _call(
        body, "att_fwd", (n // ATT_SUB,),
        specs + [pl.BlockSpec((ATT_HEADS, LANES), lambda i: (0, 0)), pl.BlockSpec(memory_space=pl.ANY)],
        [pl.BlockSpec((ATT_SUB * BLK, 1024), lambda i: (i, 1))], [jax.ShapeDtypeStruct((L, 2048), BF16)], [],
        ("parallel",), (*args, sinkb, Y), comm, aliases={13: 0})


def _att_bwd(P, rope, sinkb, Y, dY, L, Lc, comm=()):
    n = L // BLK
    specs, args = _att_inputs(P, rope, L, Lc)
    nk = N_LOC + Lc

    def body(*refs):
        sink_ref, y_ref, dy_ref = refs[12], refs[13], refs[14]
        dq_ref, dkl_ref, dvl_ref, dkx_ref, dvx_ref, dsk_ref = refs[15:21]
        i = pl.program_id(0)

        @pl.when(i == 0)
        def _():
            dkx_ref[...] = jnp.zeros_like(dkx_ref)
            dvx_ref[...] = jnp.zeros_like(dvx_ref)
            dsk_ref[...] = jnp.zeros_like(dsk_ref)

        kd, vd, bias4, tabs = _att_prep(i, n, refs[:12], Lc)
        for j in range(ATT_SUB):
            rows = slice(j * BLK, (j + 1) * BLK)
            cq, sq = tabs[j]
            for t in range(ATT_KV // 2):
                dk_halves, dv_halves = [], []
                for b in range(2):
                    g = 2 * t + b
                    q4 = _stack4(refs[0], rows, g, lambda x: _rope(x, cq, sq, 16) * Q_SCALE).astype(BF16)
                    do4 = _stack4(dy_ref, rows, g)
                    delta = jnp.sum(do4 * _stack4(y_ref, rows, g), axis=-1, keepdims=True)
                    do4b = do4.astype(BF16)
                    e, es, l = _softmax_parts(_dot(q4, kd[j][g], NT), bias4[j], sink_ref, g)
                    inv = 1.0 / l
                    p = e * inv
                    dsc = (p * (_dot(do4b, vd[j][g], NT) - delta)).astype(BF16)
                    dsr = es * inv * delta
                    for r in range(4):
                        h = 4 * g + r
                        dsk_ref[h:h + 1, :] += jnp.zeros((1, LANES), F32) - jnp.sum(dsr[r * BLK:(r + 1) * BLK])
                    dq4 = _dot(dsc, kd[j][g], NN) * A_SCALE
                    for jp in range(2):
                        c0 = (2 * g + jp) * LANES
                        dq_ref[rows, c0:c0 + LANES] = _rope_t(_unstack4(dq4, jp), cq, sq, 16).astype(dq_ref.dtype)
                    dkd = _dot(q4, dsc, TN) * LN2
                    dvd = _dot(do4b, p.astype(BF16), TN)
                    dk_halves.append(dkd[:ATT_DH] + dkd[ATT_DH:])
                    dv_halves.append(dvd[:ATT_DH] + dvd[ATT_DH:])
                dk_t = jnp.concatenate(dk_halves, axis=0).T
                dv_t = jnp.concatenate(dv_halves, axis=0).T
                sl = slice(t * LANES, (t + 1) * LANES)
                dkl_ref[j, :, sl] = dk_t[:N_LOC]
                dvl_ref[j, :, sl] = dv_t[:N_LOC]
                dkx_ref[:, sl] += dk_t[N_LOC:]
                dvx_ref[:, sl] += dv_t[N_LOC:]

    row = pl.BlockSpec((ATT_SUB * BLK, 1024), lambda i: (i, 0))
    loc = pl.BlockSpec((ATT_SUB, N_LOC, 256), lambda i: (i, 0, 0))
    cx = pl.BlockSpec((Lc, 256), lambda i: (0, 0))
    return _call(
        body, "att_bwd", (n // ATT_SUB,),
        specs + [pl.BlockSpec((ATT_HEADS, LANES), lambda i: (0, 0))]
        + [pl.BlockSpec((ATT_SUB * BLK, 1024), lambda i: (i, 1))] * 2,
        [row, loc, loc, cx, cx, pl.BlockSpec((ATT_HEADS, LANES), lambda i: (0, 0))],
        [jax.ShapeDtypeStruct((L, 1024), BF16), jax.ShapeDtypeStruct((n, N_LOC, 256), F32),
         jax.ShapeDtypeStruct((n, N_LOC, 256), F32), jax.ShapeDtypeStruct((Lc, 256), F32),
         jax.ShapeDtypeStruct((Lc, 256), F32), jax.ShapeDtypeStruct((ATT_HEADS, LANES), F32)], [],
        ("arbitrary",), (*args, sinkb, Y, dY), comm)


def _assemble_dp(L, Lc, dqf, dqb, dkf, dkb, dvf, dvb, drg, daq, dkl, dvl, rope_att, dck, dcv, dkx, dvx):
    n = L // BLK
    nc = Lc // BLK

    def body(dqf_r, dqb_r, dkf_r, dkb_r, dvf_r, dvb_r, drg_r, daq_r, kl0, kl1, kl2, vl0, vl1, vl2, rp_r,
             dck_r, dcv_r, dkx_r, dvx_r, o_ref):
        i = pl.program_id(0)

        @pl.when(i < n)
        def _():
            add = lambda a, b: (a[...].astype(F32) + b[...].astype(F32)).astype(o_ref.dtype)
            o_ref[:, C_RQ:C_RK] = add(dqf_r, dqb_r)
            o_ref[:, C_RK:C_RV] = add(dkf_r, dkb_r)
            o_ref[:, C_RV:C_RG] = add(dvf_r, dvb_r)
            o_ref[:, C_RG:C_AQ] = drg_r[...].astype(o_ref.dtype)
            o_ref[:, C_AQ:C_AK] = daq_r[...].astype(o_ref.dtype)
            w0 = jnp.where(i > 0, 1.0, 0.0)
            w2 = jnp.where(i < n - 1, 1.0, 0.0)
            dk = kl0[0] * w0 + kl1[0] + kl2[0] * w2
            dv = vl0[0] * w0 + vl1[0] + vl2[0] * w2
            for t in range(ATT_KV // 2):
                sl = slice(t * LANES, (t + 1) * LANES)
                o_ref[:, C_AK + t * LANES:C_AK + (t + 1) * LANES] = _rope_t(
                    dk[:, sl], rp_r[0], rp_r[1], 16).astype(o_ref.dtype)
            o_ref[:, C_AV:D_PROJ] = dv.astype(o_ref.dtype)

        @pl.when(i >= n)
        def _():
            o_ref[:, C_RQ:C_RK] = jnp.zeros((BLK, C_RK - C_RQ), o_ref.dtype)
            o_ref[:, C_RK:C_RV] = dck_r[...].astype(o_ref.dtype)
            o_ref[:, C_RV:C_RG] = dcv_r[...].astype(o_ref.dtype)
            o_ref[:, C_RG:C_AK] = jnp.zeros((BLK, C_AK - C_RG), o_ref.dtype)
            o_ref[:, C_AK:C_AV] = dkx_r[...].astype(o_ref.dtype)
            o_ref[:, C_AV:D_PROJ] = dvx_r[...].astype(o_ref.dtype)

    xm = lambda i: jnp.minimum(i, n - 1)
    cm = lambda i: jnp.clip(i - n, 0, nc - 1)
    r512 = pl.BlockSpec((BLK, 512), lambda i: (xm(i), 0))
    r1024 = pl.BlockSpec((BLK, 1024), lambda i: (xm(i), 0))
    part = lambda off: pl.BlockSpec((1, BLK, 256), lambda i: (jnp.clip(xm(i) + off, 0, n - 1), 1 - off, 0))
    return pl.pallas_call(
        body, name="assemble_dp", grid=(n + nc,),
        in_specs=[r512, r512, r512, r512, r1024, r1024, r1024, r1024,
                  part(-1), part(0), part(1), part(-1), part(0), part(1),
                  pl.BlockSpec((2, BLK, LANES), lambda i: (0, xm(i), 0)),
                  pl.BlockSpec((BLK, 512), lambda i: (cm(i), 0)), pl.BlockSpec((BLK, 1024), lambda i: (cm(i), 0)),
                  pl.BlockSpec((BLK, 256), lambda i: (cm(i), 0)), pl.BlockSpec((BLK, 256), lambda i: (cm(i), 0))],
        out_specs=pl.BlockSpec((BLK, D_PROJ), lambda i: (i, 0)),
        out_shape=jax.ShapeDtypeStruct((L + Lc, D_PROJ), BF16),
        compiler_params=_cparams(("parallel",)),
    )(dqf, dqb, dkf, dkb, dvf, dvb, drg, daq, dkl, dkl, dkl, dvl, dvl, dvl, rope_att, dck, dcv, dkx, dvx)


def _adam_math(w, g, m, v):
    m = ADAM_B1 * m + (1.0 - ADAM_B1) * g
    v = ADAM_B2 * v + (1.0 - ADAM_B2) * (g * g)
    m_hat = m / (1.0 - ADAM_B1 ** ADAM_STEP)
    v_hat = v / (1.0 - ADAM_B2 ** ADAM_STEP)
    delta = -ADAM_LR * (m_hat / (jnp.sqrt(v_hat) + ADAM_EPS) + ADAM_WD * w)
    return delta, m, v


def _adam(name, w, m, v, g=None, parts=None):
    R, C = w.shape
    tr = _pick(R, (256, 128, 64, 32, 16, 8))
    summed = parts is not None
    n_parts = parts.shape[0] if summed else 0

    def body(w_ref, m_ref, v_ref, g_ref, go_ref, d_ref, mo_ref, vo_ref):
        if summed:
            gv = g_ref[0].astype(F32)
            for j in range(1, n_parts):
                gv = gv + g_ref[j].astype(F32)
        else:
            gv = g_ref[...]
        d, mn, vn = _adam_math(w_ref[...], gv, m_ref[...], v_ref[...])
        go_ref[...] = gv
        d_ref[...] = d
        mo_ref[...] = mn
        vo_ref[...] = vn

    row = pl.BlockSpec((tr, C), lambda i: (i, 0))
    gspec = pl.BlockSpec((n_parts, tr, C), lambda i: (0, i, 0)) if summed else row
    return pl.pallas_call(
        body, name=name, grid=(R // tr,),
        in_specs=[row, row, row, gspec], out_specs=[row] * 4,
        out_shape=[jax.ShapeDtypeStruct((R, C), F32)] * 4,
        compiler_params=_cparams(("parallel",)),
    )(w, m, v, parts if summed else g)


def _rows_full(g):
    _, R, D = g.shape
    return g.reshape(N_DEV * R, D)


def _rows_slots(g):
    N, D = g.shape
    return g.reshape(N_DEV, N // N_DEV, D)


def _pad_rows(a, rows):
    return jnp.concatenate([a, jnp.zeros((rows - a.shape[0],) + a.shape[1:], a.dtype)], axis=0)


def kernel(x, c, ctx, c_ctx, w_mod, b_mod, norm_mix, norm_ffn, w_in, ret_decay, attn_sink, w_out, w_gate, w_up, w_down, norm_final, loss_target, m_c_ctx, m_w_mod, m_b_mod, m_norm_mix, m_norm_ffn, m_w_in, m_ret_decay, m_attn_sink, m_w_out, m_w_gate, m_w_up, m_w_down, m_norm_final, v_c_ctx, v_w_mod, v_b_mod, v_norm_mix, v_norm_ffn, v_w_in, v_ret_decay, v_attn_sink, v_w_out, v_w_gate, v_w_up, v_w_down, v_norm_final):
    L, D = x.shape[1], x.shape[2]
    Lc = ctx.shape[1]
    DF = w_gate.shape[2] * N_DEV
    C6 = w_mod.shape[2]
    me = _my_id()
    xs, cx, tgt = x[0], ctx[0], loss_target[0]

    ag_in = ("ag2", w_in[0].T.astype(BF16))
    ag_out, ag_gate = ("ag2", w_out[0].astype(BF16)), ("ag2", w_gate[0].T.astype(BF16))
    ag_up, ag_down = ("ag2", w_up[0].T.astype(BF16)), ("ag2", w_down[0].astype(BF16))

    cs = _allgather(c, "ag_c")[:, 0, :]
    s_in = _pad_rows(jnp.concatenate([cs, c_ctx[None, :]], axis=0), 16)
    b_l = lax.dynamic_slice_in_dim(b_mod, me * C6, C6, axis=1)
    mod_parts = _allgather(_mod_fwd(s_in, w_mod[0], b_l), "ag_mod")
    mod = _pad_rows(lax.dynamic_index_in_dim(mod_parts, me, axis=1, keepdims=False).reshape(6, D), 8)
    modc = _pad_rows(mod_parts[:, N_DEV, :].reshape(6, D), 8)
    mix_mod, ffn_mod = mod, jnp.roll(mod, -3, axis=0)
    gt_m, gt_f = mod[2:3], mod[5:6]

    rope_ret, rope_att = _rope_tables(L)
    rdb = jnp.broadcast_to(ret_decay[0].reshape(2 * RET_HEADS, 1, 1), (2 * RET_HEADS, 1, LANES))
    sinkb = jnp.broadcast_to(attn_sink[0].reshape(ATT_HEADS, 1), (ATT_HEADS, LANES))

    tm = _pick(L + Lc, (1408, 768, 512, 384, 256, 128))
    tmx = _pick(L, (1024, 512, 256, 128))

    (H,), (g_in,) = _modulate_fwd("mod_mix_fwd", xs, cx, norm_mix, mix_mod, modc, comm=[ag_in])
    W_inT = _rows_full(g_in)
    ident = lambda a, e: a
    tP, tD, tF = _pick(D_PROJ, (1152, 768, 512)), _pick(D, (2048, 1024, 512)), _pick(DF, (512, 256, 128))
    (P,), (g_gate,) = _matmul("mm_in", [(H, W_inT, 0)], 1, L + Lc, D_PROJ, D, "nt",
                              (tm, _pick(D_PROJ, (1536, 768, 512)), D), [], [BF16], ident,
                              comm=[ag_gate])
    W_gateT = _rows_full(g_gate)
    tabs, ctab = _ret_tables(rdb, Lc)
    s0 = _ret_ctx_state(P, ctab, L, Lc)
    (o_f, o_b, st_f, st_b), (g_out,) = _ret_fwd(P, rope_ret, tabs, s0, L, comm=[ag_out])
    W_out = _rows_full(g_out)
    Y_half = _ret_finish_fwd(o_f, o_b, P, L)
    (Y,), (g_up,) = _att_fwd(P, rope_att, sinkb, Y_half, L, Lc, comm=[ag_up])
    W_upT = _rows_full(g_up)
    KO = Y.shape[1]
    f_mix = _matmul("mm_out", [(Y, W_out, 0)], 1, L, D, KO, "nn", (tmx, tD, KO), [], [BF16], ident)[0]

    x1, H2 = _residual_modulate_fwd("mod_ffn_fwd", xs, f_mix, gt_m, norm_ffn, ffn_mod)

    def swiglu_epi(a, e):
        sg = _sigmoid(a[0])
        act = a[0] * sg
        return [act, a[1] * (sg * (1.0 + a[0] * (1.0 - sg))), act * a[1]]

    tm2 = tmx
    (act, up_dact, hmid), (g_down,) = _matmul("mm_gate_up", [(H2, W_gateT, 0), (H2, W_upT, 1)], 2, L, DF, D, "nt",
                                              (_pick(L, (2048, 1024, 512, 256, 128)), tF, D), [],
                                              [BF16, BF16, BF16], swiglu_epi, comm=[ag_down])
    W_down = _rows_full(g_down)
    f_ffn = _matmul("mm_down", [(hmid, W_down, 0)], 1, L, D, DF, "nn",
                    (tm2, _pick(D, (1024, 512)), _pick(DF, (2816, 512, 256, 128))), [], [BF16], ident)[0]

    dx2, dFf, sums_l = _loss_head(x1, tgt, norm_final.reshape(1, D), f_ffn, gt_f)

    def dswiglu_epi(a, e):
        return [a[0] * e[0].astype(F32), a[0] * e[1].astype(F32)]

    dga, dup = _matmul("mm_d_down", [(dFf, W_down, 0)], 1, L, DF, D, "nt", (_pick(L, (2048, 1024, 512, 256, 128)), tF, D),
                       [(up_dact, "mn"), (act, "mn")], [BF16, BF16], dswiglu_epi)
    tkt, tkl = _pick(L, (512, 256, 128)), _pick(L, (1024, 512, 256, 128))
    dW_down =_matmul("mm_gw_down", [(hmid, dFf, 0)], 1, DF, D, L, "tn",
                      (_pick(DF, (1408, 512, 256, 128)), tD, tkt), [], [BF16], ident)[0]
    (dW_gateT, dW_upT), (p_down,) = _matmul("mm_gw_gate_up", [(dga, H2, 0), (dup, H2, 1)], 2, DF, D, L, "tn",
                                            (tF, tD, _pick(L, (2048, 1024, 512, 256, 128))), [], [BF16, BF16], ident,
                                            comm=[("a2a", _rows_slots(dW_down))])
    (dH2,), (p_gate,) = _matmul("mm_d_gate_up", [(dga, W_gateT, 0), (dup, W_upT, 0)], 1, L, D, DF, "nn",
                                (_pick(L, (2048, 1024, 512, 256, 128)), tD, tF), [], [BF16], ident,
                                comm=[("a2a", _rows_slots(dW_gateT))])
    dx1, dFm, sums_f = _modulate_bwd("mod_ffn_bwd", x1, None, dH2, norm_ffn, ffn_mod, None, dx2, f_mix, gt_m)

    tO = _pick(KO, (2048, 1024, 512))
    dY = _matmul("mm_d_out", [(dFm, W_out, 0)], 1, L, KO, D, "nt", (tmx, tO, D), [], [BF16], ident)[0]
    dW_out = _matmul("mm_gw_out", [(Y, dFm, 0)], 1, KO, D, L, "tn",
                     (_pick(KO, (1024, 512)), tD, _pick(L, (2048, 1024, 512, 256, 128))), [], [BF16],
                     ident)[0]
    dO, drg = _ret_finish_bwd(o_f, o_b, P, dY, L)
    (dqf, dkf, dvf, dqb, dkb, dvb, ds0, dlg), (p_out,) = _ret_bwd(
        P, rope_ret, tabs, st_f, st_b, dO, L, comm=[("a2a", _rows_slots(dW_out))])
    dck, dcv, d_rd = _ret_ctx_bwd(P, ctab, ds0, dlg, rdb, L, Lc)
    (daq, dkl, dvl, dkx, dvx, d_sink), (p_up,) = _att_bwd(
        P, rope_att, sinkb, Y, dY, L, Lc, comm=[("a2a", _rows_slots(dW_upT))])
    dP = _assemble_dp(L, Lc, dqf, dqb, dkf, dkb, dvf, dvb, drg, daq, dkl, dvl, rope_att, dck, dcv, dkx, dvx)
    tkc = _pick(L + Lc, (768, 256, 128))
    dW_inT = _matmul("mm_gw_in", [(dP, H, 0)], 1, D_PROJ, D, L + Lc, "tn", (tP, tD, tkc), [], [BF16], ident)[0]
    (dH,), (p_in,) = _matmul("mm_d_in", [(dP, W_inT, 0)], 1, L + Lc, D, D_PROJ, "nn",
                             (tm, tD, _pick(D_PROJ, (768, 512, 256))), [], [BF16], ident,
                             comm=[("a2a", _rows_slots(dW_inT))])
    grad_x, sums_m = _modulate_bwd("mod_mix_bwd", xs, cx, dH, norm_mix, mix_mod, modc, dx1, None, None)

    zero = jnp.zeros((1, D), F32)
    dmod = jnp.concatenate([sums_m[0:1], sums_m[1:2], sums_f[6:7], sums_f[0:1], sums_f[1:2], sums_l[2:3]], axis=1)
    dmodc = jnp.concatenate([sums_m[3:4], sums_m[4:5], zero, zero, zero, zero], axis=1)
    dm_all = _allgather(jnp.concatenate([dmod, dmodc], axis=0), "ag_dmod")
    dm_cols = lax.dynamic_slice_in_dim(dm_all, me * C6, C6, axis=2)
    dm_in = jnp.concatenate([dm_cols[:, 0, :], dm_cols[:, 1, :]], axis=0)
    s_bwd = jnp.concatenate([cs, jnp.broadcast_to(c_ctx[None, :], (N_DEV, D))], axis=0)
    g_w_mod, dsil = _mod_bwd(s_bwd, dm_in, w_mod[0])

    lane_pad = lambda a: _pad_rows(a.reshape(-1, 1), LANES).reshape(1, LANES)
    pack = jnp.concatenate([dsil[0:1], sums_m[2:3], sums_f[2:3], sums_l[1:2],
                            lane_pad(d_rd[:, 0]), lane_pad(d_sink[:, 0]), sums_l[3:4, 0:LANES]], axis=1)
    packs = _allgather(pack, "ag_small")
    zl = jnp.zeros((1, LANES), F32)

    def pack_w(a_c, a_nm, a_nf, a_fin, a_rd, a_sk):
        return jnp.concatenate([a_c.reshape(1, D), a_nm, a_nf, a_fin.reshape(1, D), lane_pad(a_rd.reshape(-1)),
                                lane_pad(a_sk.reshape(-1)), zl], axis=1)

    sg, sd, sm, sv = _adam("adam_small", pack_w(c_ctx, norm_mix, norm_ffn, norm_final, ret_decay, attn_sink),
                           pack_w(m_c_ctx, m_norm_mix, m_norm_ffn, m_norm_final, m_ret_decay, m_attn_sink),
                           pack_w(v_c_ctx, v_norm_mix, v_norm_ffn, v_norm_final, v_ret_decay, v_attn_sink),
                           parts=packs)
    loss = sg[0, 4 * D + 2 * LANES]

    def unpack(a):
        return (a[0, 0:D], a[:, D:2 * D], a[:, 2 * D:3 * D], a[0, 3 * D:4 * D],
                a[0, 4 * D:4 * D + 2 * RET_HEADS].reshape(1, 2, RET_HEADS),
                a[:, 4 * D + LANES:4 * D + LANES + ATT_HEADS])

    bg, bd, bm, bv = _adam("adam_b_mod", b_mod, m_b_mod, v_b_mod, parts=dm_all.reshape(2 * N_DEV, 1, 6 * D))
    wg, wd, wm, wv = _adam("adam_w_mod", w_mod[0], m_w_mod[0], v_w_mod[0], g=g_w_mod)

    big = {}
    for nm, w, m, v, parts, transposed in (
            ("w_in", w_in, m_w_in, v_w_in, p_in, True), ("w_out", w_out, m_w_out, v_w_out, p_out, False),
            ("w_gate", w_gate, m_w_gate, v_w_gate, p_gate, True), ("w_up", w_up, m_w_up, v_w_up, p_up, True),
            ("w_down", w_down, m_w_down, v_w_down, p_down, False)):
        if transposed:
            res = [a.T for a in _adam("adam_" + nm, w[0].T, m[0].T, v[0].T, parts=parts)]
        else:
            res = _adam("adam_" + nm, w[0], m[0], v[0], parts=parts)
        big[nm] = [a[None] for a in res]

    g_s, d_s, m_s, v_s = unpack(sg), unpack(sd), unpack(sm), unpack(sv)

    def leaves(k, small, bmod, wmod):
        return (small[0], wmod[None], bmod, small[1], small[2], big["w_in"][k], small[4], small[5],
                big["w_out"][k], big["w_gate"][k], big["w_up"][k], big["w_down"][k], small[3])

    return (loss, grad_x[None], *leaves(0, g_s, bg, wg), *leaves(1, d_s, bd, wd),
            *leaves(2, m_s, bm, wm), *leaves(3, v_s, bv, wv))
```

```python
import functools

import jax
import jax.numpy as jnp
from jax import lax
from jax.experimental import pallas as pl
from jax.experimental.pallas import tpu as pltpu

F32 = jnp.float32
BF16 = jnp.bfloat16

N_DEV = 8
LANES = 128
RET_HEADS = 8
RET_DK = 64
RET_DV = 128
CHUNK = 128
ATT_HEADS = 16
ATT_KV = 4
ATT_DH = 64
GRID_W = 64
ROPE_BASE = 10000.0
EPS = 1e-6
NEG = -1e30
C_RQ, C_RK, C_RV, C_RG, C_AQ, C_AK, C_AV, D_PROJ = 0, 512, 1024, 2048, 3072, 4096, 4352, 4608
K_SCALE = RET_DK ** -0.5
A_SCALE = ATT_DH ** -0.5

ADAM_LR, ADAM_B1, ADAM_B2, ADAM_EPS, ADAM_WD, ADAM_STEP = 0.001, 0.9, 0.999, 1e-08, 0.01, 10

VMEM_BIG = 58 * 1024 * 1024

NN = (((1,), (0,)), ((), ()))
NT = (((1,), (1,)), ((), ()))
TN = (((0,), (0,)), ((), ()))


def _dot(a, b, dims):
    return lax.dot_general(a, b, dims, preferred_element_type=F32)


def _cparams(sem, vmem=VMEM_BIG):
    return pltpu.CompilerParams(dimension_semantics=sem, vmem_limit_bytes=vmem)


def _pick(dim, prefs):
    for p in prefs:
        if dim % p == 0:
            return p
    return dim


def _my_id():
    return lax.axis_index("x") * 4 + lax.axis_index("y") * 2 + lax.axis_index("c")


def _sigmoid(x):
    return 0.5 * jnp.tanh(0.5 * x) + 0.5


def _peers():
    mx, my, mc = lax.axis_index("x"), lax.axis_index("y"), lax.axis_index("c")
    out = []
    for k in range(1, N_DEV):
        kx, ky, kc = (k >> 2) & 1, (k >> 1) & 1, k & 1
        px = 1 - mx if kx else mx
        py = 1 - my if ky else my
        pc = 1 - mc if kc else mc
        out.append(((px, py, pc), px * 4 + py * 2 + pc))
    return out


def _exchange_copies(kind, x_ref, o_ref, ssem, rsem, lsem):
    me = _my_id()
    loc = pltpu.make_async_copy(x_ref if kind == "ag" else x_ref.at[me], o_ref.at[me], lsem)
    cps = []
    for k, (peer, pid) in enumerate(_peers()):
        cps.append(pltpu.make_async_remote_copy(
            src_ref=x_ref if kind == "ag" else x_ref.at[pid], dst_ref=o_ref.at[me],
            send_sem=ssem.at[k], recv_sem=rsem.at[k], device_id=peer, device_id_type=pl.DeviceIdType.MESH))
    return loc, cps


def _two_level_copies(x_ref, o_ref, ssem, rsem, lsem):
    mx, my, mc = lax.axis_index("x"), lax.axis_index("y"), lax.axis_index("c")
    me = mx * 4 + my * 2 + mc
    sibling = (mx, my, 1 - mc)
    chips = [(1 - mx, my), (mx, 1 - my), (1 - mx, 1 - my)]

    def copy(k, slot, to, src=None):
        return pltpu.make_async_remote_copy(
            src_ref=o_ref.at[slot] if src is None else src, dst_ref=o_ref.at[slot],
            send_sem=ssem.at[k], recv_sem=rsem.at[k], device_id=to, device_id_type=pl.DeviceIdType.MESH)

    loc = pltpu.make_async_copy(x_ref, o_ref.at[me], lsem)
    first = [copy(0, me, sibling, src=x_ref)]
    first += [copy(1 + j, me, (cx, cy, mc), src=x_ref) for j, (cx, cy) in enumerate(chips)]
    passed = [copy(4 + j, cx * 4 + cy * 2 + mc, sibling) for j, (cx, cy) in enumerate(chips)]
    return loc, first, passed


def _exchange_start(kind, x_ref, o_ref, ssem, rsem, lsem):
    if kind == "ag2":
        loc, first, _ = _two_level_copies(x_ref, o_ref, ssem, rsem, lsem)
        cps = first
    else:
        loc, cps = _exchange_copies(kind, x_ref, o_ref, ssem, rsem, lsem)
    loc.start()
    for cp in cps:
        cp.start()


def _exchange_pass_on(kind, x_ref, o_ref, ssem, rsem, lsem):
    if kind == "ag2":
        _, first, passed = _two_level_copies(x_ref, o_ref, ssem, rsem, lsem)
        for j in range(3):
            first[1 + j].wait_recv()
            passed[j].start()


def _exchange_wait(kind, x_ref, o_ref, ssem, rsem, lsem):
    if kind == "ag2":
        loc, first, passed = _two_level_copies(x_ref, o_ref, ssem, rsem, lsem)
        first[0].wait_recv()
        for cp in passed:
            cp.wait_recv()
        cps = first + passed
    else:
        loc, cps = _exchange_copies(kind, x_ref, o_ref, ssem, rsem, lsem)
        for cp in cps:
            cp.wait_recv()
    for cp in cps:
        cp.wait_send()
    loc.wait()


_EXCHANGE_SEMS = [pltpu.SemaphoreType.DMA((N_DEV - 1,)), pltpu.SemaphoreType.DMA((N_DEV - 1,)),
                  pltpu.SemaphoreType.DMA(())]


def _exchange_shape(kind, x):
    return jax.ShapeDtypeStruct(x.shape if kind == "a2a" else (N_DEV,) + x.shape, x.dtype)


def _exchange(kind, x, name):
    def body(x_ref, o_ref, ssem, rsem, lsem):
        _exchange_start(kind, x_ref, o_ref, ssem, rsem, lsem)
        _exchange_pass_on(kind, x_ref, o_ref, ssem, rsem, lsem)
        _exchange_wait(kind, x_ref, o_ref, ssem, rsem, lsem)

    return pl.pallas_call(
        body, name=name, out_shape=_exchange_shape(kind, x),
        in_specs=[pl.BlockSpec(memory_space=pl.ANY)], out_specs=pl.BlockSpec(memory_space=pl.ANY),
        scratch_shapes=list(_EXCHANGE_SEMS),
    )(x)


def _allgather(x, name):
    return _exchange("ag", x, name)


def _call(body, name, grid, in_specs, out_specs, out_shape, scratch_shapes, sem, args, comm=(), aliases=None,
          pass_on_at=0.75):
    in_specs, out_specs, out_shape = list(in_specs), list(out_specs), list(out_shape)
    scratch_shapes = list(scratch_shapes)
    aliases = aliases or {}
    if not comm:
        outs = pl.pallas_call(body, name=name, grid=grid, in_specs=in_specs, out_specs=out_specs, out_shape=out_shape,
                              scratch_shapes=scratch_shapes, input_output_aliases=aliases,
                              compiler_params=_cparams(sem))(*args)
        return list(outs), []
    n_in, n_out, n_scr, n_c = len(in_specs), len(out_specs), len(scratch_shapes), len(comm)
    hbm = pl.BlockSpec(memory_space=pl.ANY)

    def wrapped(*refs):
        ins, cins = refs[:n_in], refs[n_in:n_in + n_c]
        outs = refs[n_in + n_c:n_in + n_c + n_out]
        couts = refs[n_in + n_c + n_out:n_in + 2 * n_c + n_out]
        scr = refs[n_in + 2 * n_c + n_out:n_in + 2 * n_c + n_out + n_scr]
        sems = refs[n_in + 2 * n_c + n_out + n_scr:]
        step, total = pl.program_id(0), grid[0]
        for ax in range(1, len(grid)):
            step = step * grid[ax] + pl.program_id(ax)
            total *= grid[ax]

        @pl.when(step == 0)
        def _():
            for c, (kind, _) in enumerate(comm):
                _exchange_start(kind, cins[c], couts[c], *sems[3 * c:3 * c + 3])

        body(*ins, *outs, *scr)

        @pl.when(step == min(total - 1, int(total * pass_on_at)))
        def _():
            for c, (kind, _) in enumerate(comm):
                _exchange_pass_on(kind, cins[c], couts[c], *sems[3 * c:3 * c + 3])

        @pl.when(step == total - 1)
        def _():
            for c, (kind, _) in enumerate(comm):
                _exchange_wait(kind, cins[c], couts[c], *sems[3 * c:3 * c + 3])

    res = pl.pallas_call(
        wrapped, name=name, grid=grid,
        in_specs=in_specs + [hbm] * n_c, out_specs=out_specs + [hbm] * n_c,
        out_shape=out_shape + [_exchange_shape(kind, arr) for kind, arr in comm],
        scratch_shapes=scratch_shapes + list(_EXCHANGE_SEMS) * n_c, input_output_aliases=aliases,
        compiler_params=_cparams(("arbitrary",) * len(grid)),
    )(*args, *[arr for _, arr in comm])
    return list(res[:n_out]), list(res[n_out:])


def _matmul(name, pairs, n_acc, M, N, K, mode, tiles, extras, out_dtypes, epilogue, j_outer=False, comm=()):
    tm, tn, tk = tiles
    gm, gn, nk = M // tm, N // tn, K // tk
    assert gm * tm == M and gn * tn == N and nk * tk == K, (name, M, N, K, tiles)
    if j_outer:
        grid = (gn, gm, nk)
        ij = lambda g0, g1: (g1, g0)
    else:
        grid = (gm, gn, nk)
        ij = lambda g0, g1: (g0, g1)

    if mode in ("nn", "nt"):
        a_spec = pl.BlockSpec((tm, tk), lambda g0, g1, k: (ij(g0, g1)[0], k))
    else:
        a_spec = pl.BlockSpec((tk, tm), lambda g0, g1, k: (k, ij(g0, g1)[0]))
    if mode == "nt":
        b_spec = pl.BlockSpec((tn, tk), lambda g0, g1, k: (ij(g0, g1)[1], k))
    else:
        b_spec = pl.BlockSpec((tk, tn), lambda g0, g1, k: (k, ij(g0, g1)[1]))
    dims = {"nn": NN, "nt": NT, "tn": TN}[mode]
    mn_spec = pl.BlockSpec((tm, tn), lambda g0, g1, k: ij(g0, g1))
    n_spec = pl.BlockSpec((1, tn), lambda g0, g1, k: (0, ij(g0, g1)[1]))

    in_specs, args = [], []
    for a, b, _ in pairs:
        in_specs += [a_spec, b_spec]
        args += [a, b]
    for arr, kind in extras:
        in_specs.append(mn_spec if kind == "mn" else n_spec)
        args.append(arr)
    n_p, n_e, n_o = len(pairs), len(extras), len(out_dtypes)

    def body(*refs):
        ab = refs[:2 * n_p]
        ex = refs[2 * n_p:2 * n_p + n_e]
        outs = refs[2 * n_p + n_e:2 * n_p + n_e + n_o]
        accs = refs[2 * n_p + n_e + n_o:]
        k = pl.program_id(2)

        def single_step():
            rc = _pick(tm, (1024,)) if mode != "tn" else tm
            for c in range(tm // rc):
                rs = slice(c * rc, (c + 1) * rc)
                sums = [None] * n_acc
                for p, (_, _, ai) in enumerate(pairs):
                    a_ref, b_ref = ab[2 * p], ab[2 * p + 1]
                    d = _dot(a_ref[...] if mode == "tn" else a_ref[rs, :], b_ref[...], dims)
                    sums[ai] = d if sums[ai] is None else sums[ai] + d
                res = epilogue(sums, [e[rs, :] if e.shape[0] == tm else e[...] for e in ex])
                for o, r in zip(outs, res):
                    o[rs, :] = r.astype(o.dtype)

        def finish(acc_vals):
            res = epilogue(acc_vals, [e[...] for e in ex])
            for o, r in zip(outs, res):
                o[...] = r.astype(o.dtype)

        def accumulate(first):
            w = _pick(tm if mode == "tn" else tn, (512, 384, 256))
            for c in range((tm if mode == "tn" else tn) // w):
                sl = slice(c * w, (c + 1) * w)
                sums = [None] * n_acc
                for p, (_, _, ai) in enumerate(pairs):
                    a_ref, b_ref = ab[2 * p], ab[2 * p + 1]
                    if mode == "tn":
                        d = _dot(a_ref[:, sl], b_ref[...], dims)
                    elif mode == "nn":
                        d = _dot(a_ref[...], b_ref[:, sl], dims)
                    else:
                        d = _dot(a_ref[...], b_ref[sl, :], dims)
                    sums[ai] = d if sums[ai] is None else sums[ai] + d
                idx = (sl, slice(None)) if mode == "tn" else (slice(None), sl)
                for ai, s in enumerate(sums):
                    if first:
                        accs[ai][idx] = s
                    else:
                        accs[ai][idx] += s

        if nk == 1:
            single_step()
        else:
            pl.when(k == 0)(functools.partial(accumulate, True))
            pl.when(k > 0)(functools.partial(accumulate, False))

            @pl.when(k == nk - 1)
            def _():
                finish([a[...] for a in accs])

    outs, couts = _call(
        body, name, grid, in_specs, [mn_spec] * n_o,
        [jax.ShapeDtypeStruct((M, N), dt) for dt in out_dtypes],
        [pltpu.VMEM((tm, tn), F32) for _ in range(n_acc if nk > 1 else 0)],
        ("parallel", "parallel", "arbitrary"), args, comm)
    return (outs, couts) if comm else outs


def _rope_tables(L):
    t = jnp.arange(L, dtype=jnp.int32)
    f = jnp.arange(32, dtype=jnp.int32).astype(F32)
    ang = t.astype(F32)[:, None] * (ROPE_BASE ** (-f / 32.0))[None, :]
    cos, sin = jnp.cos(ang), jnp.sin(ang)
    ret = jnp.stack([jnp.tile(cos, (1, 4)), jnp.tile(jnp.concatenate([-sin, sin], axis=1), (1, 2))])
    f2 = jnp.arange(16, dtype=jnp.int32).astype(F32)
    inv2 = (ROPE_BASE ** (-f2 / 16.0))[None, :]
    ang_r = (t // GRID_W).astype(F32)[:, None] * inv2
    ang_c = (t % GRID_W).astype(F32)[:, None] * inv2
    cr, sr, cc, sc = jnp.cos(ang_r), jnp.sin(ang_r), jnp.cos(ang_c), jnp.sin(ang_c)
    att = jnp.stack([jnp.tile(jnp.concatenate([cr, cr, cc, cc], axis=1), (1, 2)),
                     jnp.tile(jnp.concatenate([-sr, sr, -sc, sc], axis=1), (1, 2))])
    return ret.astype(F32), att.astype(F32)


def _swap(x, sh):
    lane = lax.broadcasted_iota(jnp.int32, x.shape, 1)
    ra = pltpu.roll(x, LANES - sh, 1)
    rb = pltpu.roll(x, sh, 1)
    la = pltpu.roll(lane, LANES - sh, 1)
    partner = jnp.where((lane % (2 * sh)) < sh, lane + sh, lane - sh)
    return jnp.where(la == partner, ra, rb)


def _rope(x, cos, sin, sh):
    return x * cos + _swap(x, sh) * sin


def _rope_t(d, cos, sin, sh):
    return d * cos + _swap(d * sin, sh)


def _half_mask(shape, a):
    lane = lax.broadcasted_iota(jnp.int32, shape, 1)
    return (lane < 64) if a == 0 else (lane >= 64)


def _mod_fwd(s_in, w_l, b_l):
    D, C6 = w_l.shape
    tk = _pick(D, (512, 256, 128))
    nk = D // tk

    def body(s_ref, w_ref, b_ref, o_ref):
        k = pl.program_id(0)
        s = s_ref[...]
        s = s * _sigmoid(s)
        d = jnp.dot(s, w_ref[...], preferred_element_type=F32, precision=lax.Precision.HIGHEST)

        @pl.when(k == 0)
        def _():
            o_ref[...] = d + b_ref[...]

        @pl.when(k > 0)
        def _():
            o_ref[...] += d

    return pl.pallas_call(
        body, name="mod_fwd", grid=(nk,),
        in_specs=[pl.BlockSpec((16, tk), lambda k: (0, k)), pl.BlockSpec((tk, C6), lambda k: (k, 0)),
                  pl.BlockSpec((1, C6), lambda k: (0, 0))],
        out_specs=pl.BlockSpec((16, C6), lambda k: (0, 0)),
        out_shape=jax.ShapeDtypeStruct((16, C6), F32),
        compiler_params=_cparams(("arbitrary",)),
    )(s_in, w_l, b_l)


def _mod_bwd(s_in, dm, w_l):
    D, C6 = w_l.shape
    tk = _pick(D, (512, 256, 128))
    nk = D // tk

    def body(s_ref, dm_ref, w_ref, gw_ref, gc_ref):
        s = s_ref[...]
        sg = _sigmoid(s)
        act = s * sg
        dmv = dm_ref[...]
        gw_ref[...] = lax.dot_general(act, dmv, TN, preferred_element_type=F32, precision=lax.Precision.HIGHEST)
        ds = lax.dot_general(dmv, w_ref[...], NT, preferred_element_type=F32, precision=lax.Precision.HIGHEST)
        dsil = (sg * (1.0 + s * (1.0 - sg)))[8:9, :]
        gc_ref[...] = jnp.zeros((8, tk), F32) + jnp.sum(ds[8:16, :], axis=0, keepdims=True) * dsil

    return pl.pallas_call(
        body, name="mod_bwd", grid=(nk,),
        in_specs=[pl.BlockSpec((16, tk), lambda k: (0, k)), pl.BlockSpec((16, C6), lambda k: (0, 0)),
                  pl.BlockSpec((tk, C6), lambda k: (k, 0))],
        out_specs=[pl.BlockSpec((tk, C6), lambda k: (k, 0)), pl.BlockSpec((8, tk), lambda k: (0, k))],
        out_shape=[jax.ShapeDtypeStruct((D, C6), F32), jax.ShapeDtypeStruct((8, D), F32)],
        compiler_params=_cparams(("parallel",)),
    )(s_in, dm, w_l)


def _norm_rows(x):
    r = lax.rsqrt(jnp.mean(x * x, axis=-1, keepdims=True) + EPS)
    return x * r, r


def _modulate_fwd(name, x, ctx, g, mod, modc, comm=()):
    L, D = x.shape
    tr = ctx.shape[0]
    nx = L // tr

    def body(x_ref, c_ref, g_ref, m_ref, mc_ref, o_ref):
        i = pl.program_id(0)

        def run(src, m):
            n, _ = _norm_rows(src[...])
            o_ref[...] = (n * g_ref[...] * (1.0 + m[1:2, :]) + m[0:1, :]).astype(o_ref.dtype)

        @pl.when(i < nx)
        def _():
            run(x_ref, m_ref)

        @pl.when(i >= nx)
        def _():
            run(c_ref, mc_ref)

    row = pl.BlockSpec((tr, D), lambda i: (jnp.minimum(i, nx - 1), 0))
    vec = pl.BlockSpec((1, D), lambda i: (0, 0))
    mv = pl.BlockSpec((8, D), lambda i: (0, 0))
    return _call(
        body, name, (nx + 1,), [row, pl.BlockSpec((tr, D), lambda i: (0, 0)), vec, mv, mv],
        [pl.BlockSpec((tr, D), lambda i: (i, 0))], [jax.ShapeDtypeStruct((L + tr, D), BF16)], [],
        ("parallel",), (x, ctx, g, mod, modc), comm, pass_on_at=1.0)


def _residual_modulate_fwd(name, x, fbr, gate, g, mod):
    L, D = x.shape
    tr = _pick(L, (512, 256, 128))

    def body(x_ref, f_ref, gt_ref, g_ref, m_ref, x1_ref, o_ref):
        x1 = x_ref[...] + gt_ref[...] * f_ref[...].astype(F32)
        x1_ref[...] = x1
        n, _ = _norm_rows(x1)
        o_ref[...] = (n * g_ref[...] * (1.0 + m_ref[1:2, :]) + m_ref[0:1, :]).astype(o_ref.dtype)

    row = pl.BlockSpec((tr, D), lambda i: (i, 0))
    vec = pl.BlockSpec((1, D), lambda i: (0, 0))
    return pl.pallas_call(
        body, name=name, grid=(L // tr,),
        in_specs=[row, row, vec, vec, pl.BlockSpec((8, D), lambda i: (0, 0))],
        out_specs=[row, row],
        out_shape=[jax.ShapeDtypeStruct((L, D), F32), jax.ShapeDtypeStruct((L, D), BF16)],
        compiler_params=_cparams(("parallel",)),
    )(x, fbr, gate, g, mod)


def _modulate_bwd(name, x, ctx, dh, g, mod, modc, dres, fbr, gate):
    L, D = x.shape
    tr = ctx.shape[0] if ctx is not None else _pick(L, (512, 256, 128))
    nx = L // tr
    nt = nx + (1 if ctx is not None else 0)
    has_f = fbr is not None

    def body(*refs):
        refs = list(refs)
        x_ref = refs.pop(0)
        c_ref = refs.pop(0) if ctx is not None else None
        dh_ref, g_ref, m_ref = refs.pop(0), refs.pop(0), refs.pop(0)
        mc_ref = refs.pop(0) if ctx is not None else None
        dr_ref = refs.pop(0)
        f_ref = refs.pop(0) if has_f else None
        gt_ref = refs.pop(0) if has_f else None
        dx_ref = refs.pop(0)
        df_ref = refs.pop(0) if has_f else None
        acc_ref = refs.pop(0)
        i = pl.program_id(0)

        @pl.when(i == 0)
        def _():
            acc_ref[...] = jnp.zeros_like(acc_ref)

        def sums(src, m, base, grow):
            n, r = _norm_rows(src[...])
            d = dh_ref[...].astype(F32)
            gg = g_ref[...]
            sc1 = 1.0 + m[1:2, :]
            acc_ref[base:base + 1, :] += jnp.sum(d, axis=0, keepdims=True)
            dn = d * n
            acc_ref[base + 1:base + 2, :] += jnp.sum(dn, axis=0, keepdims=True) * gg
            acc_ref[grow:grow + 1, :] += jnp.sum(dn, axis=0, keepdims=True) * sc1
            dnv = d * (gg * sc1)
            return r * (dnv - n * jnp.mean(dnv * n, axis=-1, keepdims=True))

        def x_rows():
            dx = sums(x_ref, m_ref, 0, 2) + dr_ref[...]
            dx_ref[...] = dx
            if has_f:
                acc_ref[6:7, :] += jnp.sum(dx * f_ref[...].astype(F32), axis=0, keepdims=True)
                df_ref[...] = (dx * gt_ref[...]).astype(df_ref.dtype)

        if ctx is None:
            x_rows()
        else:
            pl.when(i < nx)(x_rows)

            @pl.when(i >= nx)
            def _():
                sums(c_ref, mc_ref, 3, 2)

    row = pl.BlockSpec((tr, D), lambda i: (jnp.minimum(i, nx - 1), 0))
    vec = pl.BlockSpec((1, D), lambda i: (0, 0))
    mv = pl.BlockSpec((8, D), lambda i: (0, 0))
    in_specs, args = [row], [x]
    if ctx is not None:
        in_specs.append(pl.BlockSpec((tr, D), lambda i: (0, 0)))
        args.append(ctx)
    in_specs += [pl.BlockSpec((tr, D), lambda i: (i, 0)), vec, mv]
    args += [dh, g, mod]
    if ctx is not None:
        in_specs.append(mv)
        args.append(modc)
    in_specs.append(row)
    args.append(dres)
    out_specs = [row]
    out_shape = [jax.ShapeDtypeStruct((L, D), F32)]
    if has_f:
        in_specs += [row, vec]
        args += [fbr, gate]
        out_specs.append(row)
        out_shape.append(jax.ShapeDtypeStruct((L, D), BF16))
    out_specs.append(pl.BlockSpec((16, D), lambda i: (0, 0)))
    out_shape.append(jax.ShapeDtypeStruct((16, D), F32))
    return pl.pallas_call(
        body, name=name, grid=(nt,), in_specs=in_specs, out_specs=out_specs, out_shape=out_shape,
        compiler_params=_cparams(("arbitrary",)),
    )(*args)


def _loss_head(x1, tgt, nf, fbr, gate):
    L, D = x1.shape
    tr = _pick(L, (512, 256, 128))

    def body(x_ref, t_ref, w_ref, f_ref, gt_ref, dx_ref, df_ref, acc_ref):
        i = pl.program_id(0)

        @pl.when(i == 0)
        def _():
            acc_ref[...] = jnp.zeros_like(acc_ref)

        n, r = _norm_rows(x_ref[...] + gt_ref[...] * f_ref[...].astype(F32))
        w = w_ref[...]
        e = n * w - t_ref[...]
        acc_ref[0:1, :] += jnp.sum(e * e, axis=0, keepdims=True) * (0.5 / D)
        dout = e * (1.0 / D)
        acc_ref[1:2, :] += jnp.sum(dout * n, axis=0, keepdims=True)
        dn = dout * w
        dx = r * (dn - n * jnp.mean(dn * n, axis=-1, keepdims=True))
        dx_ref[...] = dx
        acc_ref[2:3, :] += jnp.sum(dx * f_ref[...].astype(F32), axis=0, keepdims=True)
        df_ref[...] = (dx * gt_ref[...]).astype(df_ref.dtype)

        @pl.when(i == pl.num_programs(0) - 1)
        def _():
            acc_ref[3:4, :] = jnp.zeros((1, D), F32) + jnp.sum(acc_ref[0:1, :])

    row = pl.BlockSpec((tr, D), lambda i: (i, 0))
    vec = pl.BlockSpec((1, D), lambda i: (0, 0))
    return pl.pallas_call(
        body, name="loss_head", grid=(L // tr,),
        in_specs=[row, row, vec, row, vec],
        out_specs=[row, row, pl.BlockSpec((8, D), lambda i: (0, 0))],
        out_shape=[jax.ShapeDtypeStruct((L, D), F32), jax.ShapeDtypeStruct((L, D), BF16),
                   jax.ShapeDtypeStruct((8, D), F32)],
        compiler_params=_cparams(("arbitrary",)),
    )(x1, tgt, nf, fbr, gate)


RET_SUB = 4
N_TAB = 7


def _ret_tables(rdb, Lc):
    def body(rd_ref, t_ref, c_ref):
        d = pl.program_id(0) // RET_HEADS
        fwd = d == 0
        lg = -jnp.exp(rd_ref[0])
        i = lax.broadcasted_iota(jnp.int32, (CHUNK, CHUNK), 0).astype(F32)
        j = lax.broadcasted_iota(jnp.int32, (CHUNK, CHUNK), 1).astype(F32)
        rel = jnp.where(fwd, i - j, j - i)
        mask = (rel > 0.0) | ((rel == 0.0) & fwd)
        dm = jnp.where(mask, jnp.exp(lg * jnp.maximum(rel, 0.0)), 0.0)
        t_ref[0, 0] = dm
        t_ref[0, 1] = rel * dm
        qc = jnp.where(fwd, i + 1.0, CHUNK - i)
        qw = jnp.exp(lg * qc)
        t_ref[0, 2] = qw
        t_ref[0, 3] = qw * qc
        kc = jnp.where(fwd, CHUNK - 1.0 - i, i)
        kw = jnp.exp(lg * kc)
        t_ref[0, 4] = kw
        t_ref[0, 5] = kw * kc
        t_ref[0, 6] = jnp.exp(lg * float(CHUNK)) + jnp.zeros((CHUNK, CHUNK), F32)
        m = lax.broadcasted_iota(jnp.int32, (Lc, LANES), 0).astype(F32)
        cc = jnp.where(fwd, Lc - 1.0 - m, m)
        cw = jnp.exp(lg * cc)
        c_ref[0, 0] = cw
        c_ref[0, 1] = cw * cc

    return pl.pallas_call(
        body, name="ret_tables", grid=(2 * RET_HEADS,),
        in_specs=[pl.BlockSpec((1, 1, LANES), lambda r: (r, 0, 0))],
        out_specs=[pl.BlockSpec((1, N_TAB, CHUNK, CHUNK), lambda r: (r, 0, 0, 0)),
                   pl.BlockSpec((1, 2, Lc, LANES), lambda r: (r, 0, 0, 0))],
        out_shape=[jax.ShapeDtypeStruct((2 * RET_HEADS, N_TAB, CHUNK, CHUNK), F32),
                   jax.ShapeDtypeStruct((2 * RET_HEADS, 2, Lc, LANES), F32)],
        compiler_params=_cparams(("parallel",)),
    )(rdb)


def _ret_ctx_state(P, ctab, L, Lc):
    cb = L // Lc

    def body(k_ref, v_ref, c_ref, s_ref):
        for p in range(RET_HEADS // 2):
            kp = k_ref[:, p * LANES:(p + 1) * LANES].astype(F32) * K_SCALE
            for a in range(2):
                h = 2 * p + a
                kh = jnp.where(_half_mask(kp.shape, a), kp, 0.0)
                vh = v_ref[:, h * RET_DV:(h + 1) * RET_DV]
                for d in range(2):
                    kw = (kh * c_ref[d * RET_HEADS + h, 0]).astype(BF16)
                    s_ref[d * RET_HEADS + h] = _dot(kw, vh, TN)

    return pl.pallas_call(
        body, name="ret_ctx_state", grid=(1,),
        in_specs=[pl.BlockSpec((Lc, 512), lambda i: (cb, C_RK // 512)),
                  pl.BlockSpec((Lc, 1024), lambda i: (cb, C_RV // 1024)),
                  pl.BlockSpec((2 * RET_HEADS, 2, Lc, LANES), lambda i: (0, 0, 0, 0))],
        out_specs=pl.BlockSpec((2 * RET_HEADS, LANES, RET_DV), lambda i: (0, 0, 0)),
        out_shape=jax.ShapeDtypeStruct((2 * RET_HEADS, LANES, RET_DV), F32),
        compiler_params=_cparams(("arbitrary",)),
    )(P, P, ctab)


def _ret_fwd(P, rope, tabs, s0, L, comm=()):
    n = L // CHUNK
    nb = n // RET_SUB

    def body(qf, kf, vf, rf, qb, kb, vb, rb, t_ref, s0_ref, of_ref, ob_ref, stf_ref, stb_ref, st):
        s = pl.program_id(0)

        @pl.when(s == 0)
        def _():
            st[...] = s0_ref[...]

        for rnd in range(RET_SUB):
            units = []
            for d, (q_ref, k_ref, v_ref, r_ref, o_ref, so_ref) in enumerate(
                    ((qf, kf, vf, rf, of_ref, stf_ref), (qb, kb, vb, rb, ob_ref, stb_ref))):
                j = rnd if d == 0 else RET_SUB - 1 - rnd
                rows = slice(j * CHUNK, (j + 1) * CHUNK)
                cos, sin = r_ref[0, rows, :], r_ref[1, rows, :]
                for p in range(RET_HEADS // 2):
                    qp = _rope(q_ref[rows, p * LANES:(p + 1) * LANES].astype(F32), cos, sin, 32)
                    kp = _rope(k_ref[rows, p * LANES:(p + 1) * LANES].astype(F32), cos, sin, 32) * K_SCALE
                    for a in range(2):
                        h = 2 * p + a
                        hm = _half_mask(qp.shape, a)
                        units.append(dict(r=d * RET_HEADS + h, h=h, a=a, j=j, rows=rows, o_ref=o_ref, so_ref=so_ref,
                                          v_ref=v_ref, qh=jnp.where(hm, qp, 0.0), kh=jnp.where(hm, kp, 0.0)))
            for u in units:
                u["sc"] = _dot(u["qh"].astype(BF16), u["kh"].astype(BF16), NT)
            for u in units:
                r, h = u["r"], u["h"]
                sp = st[r]
                u["so_ref"][u["j"], h] = sp[u["a"] * RET_DK:(u["a"] + 1) * RET_DK, :]
                vh = u["v_ref"][u["rows"], h * RET_DV:(h + 1) * RET_DV]
                o = _dot((u["sc"] * t_ref[r, 0]).astype(BF16), vh, NN)
                o += _dot((u["qh"] * t_ref[r, 2]).astype(BF16), sp.astype(BF16), NN)
                u["o_ref"][u["rows"], h * RET_DV:(h + 1) * RET_DV] = o
            for u in units:
                r, h = u["r"], u["h"]
                vh = u["v_ref"][u["rows"], h * RET_DV:(h + 1) * RET_DV]
                st[r] = t_ref[r, 6] * st[r] + _dot((u["kh"] * t_ref[r, 4]).astype(BF16), vh, TN)

    fw = lambda s: s
    bw = lambda s: nb - 1 - s
    RB = RET_SUB * CHUNK

    def specs(cm):
        return [pl.BlockSpec((RB, 512), lambda s: (cm(s), C_RQ // 512)),
                pl.BlockSpec((RB, 512), lambda s: (cm(s), C_RK // 512)),
                pl.BlockSpec((RB, 1024), lambda s: (cm(s), C_RV // 1024)),
                pl.BlockSpec((2, RB, LANES), lambda s: (0, cm(s), 0))]

    full = lambda shp: pl.BlockSpec(shp, lambda s: (0,) * len(shp))
    return _call(
        body, "ret_fwd", (nb,),
        specs(fw) + specs(bw) + [full((2 * RET_HEADS, N_TAB, CHUNK, CHUNK)), full((2 * RET_HEADS, LANES, RET_DV))],
        [pl.BlockSpec((RB, 1024), lambda s: (fw(s), 0)),
         pl.BlockSpec((RB, 1024), lambda s: (bw(s), 0)),
         pl.BlockSpec((RET_SUB, RET_HEADS, RET_DK, RET_DV), lambda s: (fw(s), 0, 0, 0)),
         pl.BlockSpec((RET_SUB, RET_HEADS, RET_DK, RET_DV), lambda s: (bw(s), 0, 0, 0))],
        [jax.ShapeDtypeStruct((L, 1024), F32), jax.ShapeDtypeStruct((L, 1024), F32),
         jax.ShapeDtypeStruct((n, RET_HEADS, RET_DK, RET_DV), F32),
         jax.ShapeDtypeStruct((n, RET_HEADS, RET_DK, RET_DV), F32)],
        [pltpu.VMEM((2 * RET_HEADS, LANES, RET_DV), F32)],
        ("arbitrary",), (P, P, P, rope, P, P, P, rope, tabs, s0), comm)


def _ret_finish_fwd(of, ob, P, L):
    tr = _pick(L, (512, 256, 128))

    def body(f_ref, b_ref, g_ref, y_ref):
        for h in range(RET_HEADS):
            sl = slice(h * RET_DV, (h + 1) * RET_DV)
            n, _ = _norm_rows(f_ref[:, sl] + b_ref[:, sl])
            g = g_ref[:, sl].astype(F32)
            y_ref[:, sl] = (n * (g * _sigmoid(g))).astype(y_ref.dtype)

    row = pl.BlockSpec((tr, 1024), lambda i: (i, 0))
    return pl.pallas_call(
        body, name="ret_finish_fwd", grid=(L // tr,),
        in_specs=[row, row, pl.BlockSpec((tr, 1024), lambda i: (i, C_RG // 1024))],
        out_specs=row, out_shape=jax.ShapeDtypeStruct((L, 2048), BF16),
        compiler_params=_cparams(("parallel",)),
    )(of, ob, P)


def _ret_finish_bwd(of, ob, P, dY, L):
    tr = _pick(L, (512, 256, 128))

    def body(f_ref, b_ref, g_ref, dy_ref, do_ref, dg_ref):
        for h in range(RET_HEADS):
            sl = slice(h * RET_DV, (h + 1) * RET_DV)
            n, r = _norm_rows(f_ref[:, sl] + b_ref[:, sl])
            g = g_ref[:, sl].astype(F32)
            sg = _sigmoid(g)
            dy = dy_ref[:, sl].astype(F32)
            dg_ref[:, sl] = (dy * n * (sg * (1.0 + g * (1.0 - sg)))).astype(dg_ref.dtype)
            dn = dy * (g * sg)
            do_ref[:, sl] = (r * (dn - n * jnp.mean(dn * n, axis=-1, keepdims=True))).astype(do_ref.dtype)

    row = pl.BlockSpec((tr, 1024), lambda i: (i, 0))
    return pl.pallas_call(
        body, name="ret_finish_bwd", grid=(L // tr,),
        in_specs=[row, row, pl.BlockSpec((tr, 1024), lambda i: (i, C_RG // 1024)), row],
        out_specs=[row, row],
        out_shape=[jax.ShapeDtypeStruct((L, 1024), BF16), jax.ShapeDtypeStruct((L, 1024), BF16)],
        compiler_params=_cparams(("parallel",)),
    )(of, ob, P, dY)


def _ret_bwd(P, rope, tabs, stf, stb, dO, L, comm=()):
    n = L // CHUNK
    nb = n // RET_SUB

    def body(qf, kf, vf, rf, gf, sf, qb, kb, vb, rb, gb, sb, t_ref,
             dqf, dkf, dvf, dqb, dkb, dvb, ds0_ref, dlg_ref, ds):
        s = pl.program_id(0)

        @pl.when(s == 0)
        def _():
            ds[...] = jnp.zeros_like(ds)
            dlg_ref[...] = jnp.zeros_like(dlg_ref)

        for rnd in range(RET_SUB):
            units, pairs = [], []
            for d, (q_ref, k_ref, v_ref, r_ref, g_ref, s_ref, dq_ref, dk_ref, dv_ref) in enumerate(
                    ((qf, kf, vf, rf, gf, sf, dqf, dkf, dvf), (qb, kb, vb, rb, gb, sb, dqb, dkb, dvb))):
                j = RET_SUB - 1 - rnd if d == 0 else rnd
                rows = slice(j * CHUNK, (j + 1) * CHUNK)
                cos, sin = r_ref[0, rows, :], r_ref[1, rows, :]
                for p in range(RET_HEADS // 2):
                    qp = _rope(q_ref[rows, p * LANES:(p + 1) * LANES].astype(F32), cos, sin, 32)
                    kp = _rope(k_ref[rows, p * LANES:(p + 1) * LANES].astype(F32), cos, sin, 32) * K_SCALE
                    pair = dict(p=p, rows=rows, cos=cos, sin=sin, dq_ref=dq_ref, dk_ref=dk_ref, us=[])
                    pairs.append(pair)
                    for a in range(2):
                        h = 2 * p + a
                        r = d * RET_HEADS + h
                        hm = _half_mask(qp.shape, a)
                        zero = jnp.zeros((RET_DK, RET_DV), F32)
                        sp = s_ref[j, h]
                        u = dict(r=r, h=h, rows=rows, dv_ref=dv_ref,
                                 qh=jnp.where(hm, qp, 0.0), kh=jnp.where(hm, kp, 0.0),
                                 vh=v_ref[rows, h * RET_DV:(h + 1) * RET_DV],
                                 gh=g_ref[rows, h * RET_DV:(h + 1) * RET_DV],
                                 sp=jnp.concatenate([sp, zero] if a == 0 else [zero, sp], axis=0),
                                 dsn=ds[r])
                        u["qhb"], u["khb"] = u["qh"].astype(BF16), u["kh"].astype(BF16)
                        units.append(u)
                        pair["us"].append(u)
            for u in units:
                u["am"] = _dot(u["qhb"], u["khb"], NT)
                u["dar"] = _dot(u["gh"], u["vh"], NT)
                u["xq"] = _dot(u["gh"], u["sp"].astype(BF16), NT)
                u["yk"] = _dot(u["vh"], u["dsn"].astype(BF16), NT)
            for u in units:
                r = u["r"]
                dm = t_ref[r, 0]
                u["da"] = (u["dar"] * dm).astype(BF16)
                u["amd"] = (u["am"] * dm).astype(BF16)
                part = (jnp.sum(u["am"] * u["dar"] * t_ref[r, 1]) + jnp.sum(u["qh"] * t_ref[r, 3] * u["xq"])
                        + jnp.sum(u["kh"] * t_ref[r, 5] * u["yk"])
                        + float(CHUNK) * jnp.sum(t_ref[r, 6] * u["dsn"] * u["sp"]))
                dlg_ref[r:r + 1, :] += jnp.zeros((1, LANES), F32) + part
            for u in units:
                r, h = u["r"], u["h"]
                u["dq"] = _dot(u["da"], u["khb"], NN) + u["xq"] * t_ref[r, 2]
                u["dk"] = _dot(u["da"], u["qhb"], TN) + u["yk"] * t_ref[r, 4]
                u["dv_ref"][u["rows"], h * RET_DV:(h + 1) * RET_DV] = (
                    _dot(u["amd"], u["gh"], TN)
                    + _dot((u["kh"] * t_ref[r, 4]).astype(BF16), u["dsn"].astype(BF16), NN)
                ).astype(u["dv_ref"].dtype)
                ds[r] = t_ref[r, 6] * u["dsn"] + _dot((u["qh"] * t_ref[r, 2]).astype(BF16), u["gh"], TN)
            for pair in pairs:
                sl = slice(pair["p"] * LANES, (pair["p"] + 1) * LANES)
                u0, u1 = pair["us"]
                pair["dq_ref"][pair["rows"], sl] = _rope_t(
                    u0["dq"] + u1["dq"], pair["cos"], pair["sin"], 32).astype(BF16)
                pair["dk_ref"][pair["rows"], sl] = _rope_t(
                    (u0["dk"] + u1["dk"]) * K_SCALE, pair["cos"], pair["sin"], 32).astype(BF16)

        @pl.when(s == nb - 1)
        def _():
            ds0_ref[...] = ds[...]

    fw = lambda s: nb - 1 - s
    bw = lambda s: s
    RB = RET_SUB * CHUNK

    def specs(cm):
        return [pl.BlockSpec((RB, 512), lambda s: (cm(s), C_RQ // 512)),
                pl.BlockSpec((RB, 512), lambda s: (cm(s), C_RK // 512)),
                pl.BlockSpec((RB, 1024), lambda s: (cm(s), C_RV // 1024)),
                pl.BlockSpec((2, RB, LANES), lambda s: (0, cm(s), 0)),
                pl.BlockSpec((RB, 1024), lambda s: (cm(s), 0)),
                pl.BlockSpec((RET_SUB, RET_HEADS, RET_DK, RET_DV), lambda s: (cm(s), 0, 0, 0))]

    def ospecs(cm):
        return [pl.BlockSpec((RB, 512), lambda s: (cm(s), 0)), pl.BlockSpec((RB, 512), lambda s: (cm(s), 0)),
                pl.BlockSpec((RB, 1024), lambda s: (cm(s), 0))]

    oshape = [jax.ShapeDtypeStruct((L, 512), BF16), jax.ShapeDtypeStruct((L, 512), BF16),
              jax.ShapeDtypeStruct((L, 1024), BF16)]
    full = lambda shp: pl.BlockSpec(shp, lambda s: (0,) * len(shp))
    return _call(
        body, "ret_bwd", (nb,),
        specs(fw) + specs(bw) + [full((2 * RET_HEADS, N_TAB, CHUNK, CHUNK))],
        ospecs(fw) + ospecs(bw) + [full((2 * RET_HEADS, LANES, RET_DV)), full((2 * RET_HEADS, LANES))],
        oshape + oshape + [jax.ShapeDtypeStruct((2 * RET_HEADS, LANES, RET_DV), F32),
                           jax.ShapeDtypeStruct((2 * RET_HEADS, LANES), F32)],
        [pltpu.VMEM((2 * RET_HEADS, LANES, RET_DV), F32)],
        ("arbitrary",), (P, P, P, rope, dO, stf, P, P, P, rope, dO, stb, tabs), comm)


def _ret_ctx_bwd(P, ctab, ds0, dlg, rdb, L, Lc):
    cb = L // Lc

    def body(k_ref, v_ref, c_ref, ds_ref, dlg_ref, rd_ref, dk_ref, dv_ref, drd_ref):
        for p in range(RET_HEADS // 2):
            kp = k_ref[:, p * LANES:(p + 1) * LANES].astype(F32) * K_SCALE
            dkp = jnp.zeros((Lc, LANES), F32)
            for a in range(2):
                h = 2 * p + a
                kh = jnp.where(_half_mask(kp.shape, a), kp, 0.0)
                vh = v_ref[:, h * RET_DV:(h + 1) * RET_DV]
                dvh = jnp.zeros((Lc, RET_DV), F32)
                for d in range(2):
                    r = d * RET_HEADS + h
                    dsb = ds_ref[r].astype(BF16)
                    cw, cwc = c_ref[r, 0], c_ref[r, 1]
                    y = _dot(vh, dsb, NT)
                    dkp += y * cw
                    dvh += _dot((kh * cw).astype(BF16), dsb, NN)
                    lg = -jnp.exp(rd_ref[r])
                    drd_ref[r:r + 1, :] = (dlg_ref[r:r + 1, :] + jnp.sum(kh * cwc * y)) * lg
                dv_ref[:, h * RET_DV:(h + 1) * RET_DV] = dvh
            dk_ref[:, p * LANES:(p + 1) * LANES] = dkp * K_SCALE

    full = lambda shp: pl.BlockSpec(shp, lambda i: (0,) * len(shp))
    return pl.pallas_call(
        body, name="ret_ctx_bwd", grid=(1,),
        in_specs=[pl.BlockSpec((Lc, 512), lambda i: (cb, C_RK // 512)),
                  pl.BlockSpec((Lc, 1024), lambda i: (cb, C_RV // 1024)),
                  full((2 * RET_HEADS, 2, Lc, LANES)), full((2 * RET_HEADS, LANES, RET_DV)),
                  full((2 * RET_HEADS, LANES)), full((2 * RET_HEADS, 1, LANES))],
        out_specs=[full((Lc, 512)), full((Lc, 1024)), full((2 * RET_HEADS, LANES))],
        out_shape=[jax.ShapeDtypeStruct((Lc, 512), F32), jax.ShapeDtypeStruct((Lc, 1024), F32),
                   jax.ShapeDtypeStruct((2 * RET_HEADS, LANES), F32)],
        compiler_params=_cparams(("arbitrary",)),
    )(P, P, ctab, ds0, dlg, rdb)


BLK = 128
N_LOC = 3 * BLK


ATT_SUB = 4


def _att_inputs(P, rope, L, Lc):
    n = L // BLK
    cb = L // Lc
    prev = lambda i: jnp.maximum(ATT_SUB * i - 1, 0)
    nxt = lambda i: jnp.minimum(ATT_SUB * i + ATT_SUB, n - 1)
    specs = [pl.BlockSpec((ATT_SUB * BLK, 1024), lambda i: (i, C_AQ // 1024))]
    args = [P]
    for col in (C_AK // 256, C_AV // 256):
        specs += [pl.BlockSpec((BLK, 256), functools.partial(lambda i, col: (prev(i), col), col=col)),
                  pl.BlockSpec((ATT_SUB * BLK, 256), functools.partial(lambda i, col: (i, col), col=col)),
                  pl.BlockSpec((BLK, 256), functools.partial(lambda i, col: (nxt(i), col), col=col)),
                  pl.BlockSpec((Lc, 256), functools.partial(lambda i, col: (cb, col), col=col))]
        args += [P] * 4
    specs += [pl.BlockSpec((2, BLK, LANES), lambda i: (0, prev(i), 0)),
              pl.BlockSpec((2, ATT_SUB * BLK, LANES), lambda i: (0, i, 0)),
              pl.BlockSpec((2, BLK, LANES), lambda i: (0, nxt(i), 0))]
    args += [rope] * 3
    return specs, args


def _att_prep(i, n, refs, Lc):
    q_ref, kp_ref, kc_ref, kn_ref, kx_ref, vp_ref, vc_ref, vn_ref, vx_ref, rp_ref, rc_ref, rn_ref = refs
    cos = jnp.concatenate([rp_ref[0], rc_ref[0], rn_ref[0]], axis=0)
    sin = jnp.concatenate([rp_ref[1], rc_ref[1], rn_ref[1]], axis=0)

    def dup(x):
        xr = pltpu.roll(x, 64, 1)
        return [jnp.where(_half_mask(x.shape, b), x, xr).astype(BF16) for b in range(2)]

    kd = [[] for _ in range(ATT_SUB)]
    vd = [[] for _ in range(ATT_SUB)]
    for t in range(ATT_KV // 2):
        sl = slice(t * LANES, (t + 1) * LANES)
        kl = jnp.concatenate([kp_ref[:, sl], kc_ref[:, sl], kn_ref[:, sl]], axis=0).astype(F32)
        kl = dup(_rope(kl, cos, sin, 16))
        vl = dup(jnp.concatenate([vp_ref[:, sl], vc_ref[:, sl], vn_ref[:, sl]], axis=0).astype(F32))
        kx, vx = dup(kx_ref[:, sl].astype(F32)), dup(vx_ref[:, sl].astype(F32))
        for j in range(ATT_SUB):
            rows = slice(j * BLK, j * BLK + N_LOC)
            for b in range(2):
                kd[j].append(jnp.concatenate([kl[b][rows], kx[b]], axis=0))
                vd[j].append(jnp.concatenate([vl[b][rows], vx[b]], axis=0))
    nk = N_LOC + Lc
    rr = lax.broadcasted_iota(jnp.int32, (BLK, nk), 0)
    ss = lax.broadcasted_iota(jnp.int32, (BLK, nk), 1)
    band = (ss >= rr) & (ss <= rr + 2 * BLK)
    bias4, tabs = [], []
    for j in range(ATT_SUB):
        blk = ATT_SUB * i + j
        lo = jnp.where(blk == 0, BLK, 0)
        hi = jnp.where(blk == n - 1, 2 * BLK, N_LOC)
        bias = jnp.where((ss >= N_LOC) | (band & (ss >= lo) & (ss < hi)), 0.0, NEG)
        bias4.append(jnp.concatenate([bias] * 4, axis=0))
        tabs.append((rc_ref[0, j * BLK:(j + 1) * BLK, :], rc_ref[1, j * BLK:(j + 1) * BLK, :]))
    return kd, vd, bias4, tabs


LOG2E = 1.4426950408889634
LN2 = 0.6931471805599453
Q_SCALE = A_SCALE * LOG2E


def _stack4(ref, rows, g, f=None):
    parts = []
    for jp in range(2):
        t = ref[rows, (2 * g + jp) * LANES:(2 * g + jp + 1) * LANES].astype(F32)
        if f is not None:
            t = f(t)
        for a in range(2):
            parts.append(jnp.where(_half_mask(t.shape, a), t, 0.0))
    return jnp.concatenate(parts, axis=0)


def _unstack4(x4, jp):
    r0 = 2 * jp * BLK
    lo = x4[r0:r0 + BLK]
    hi = x4[r0 + BLK:r0 + 2 * BLK]
    return jnp.where(_half_mask(lo.shape, 0), lo, hi)


def _softmax_parts(s, bias4, sink_ref, g):
    sink_col = LOG2E * jnp.concatenate(
        [jnp.zeros((BLK, 1), F32) + sink_ref[4 * g + r:4 * g + r + 1, 0:1] for r in range(4)], axis=0)
    s = s + bias4
    m = jnp.maximum(jnp.max(s, axis=-1, keepdims=True), sink_col)
    e = jnp.exp2(s - m)
    es = jnp.exp2(sink_col - m)
    return e, es, jnp.sum(e, axis=-1, keepdims=True) + es


def _att_fwd(P, rope, sinkb, Y, L, Lc, comm=()):
    n = L // BLK
    specs, args = _att_inputs(P, rope, L, Lc)

    def body(*refs):
        sink_ref, o_ref = refs[12], refs[14]
        i = pl.program_id(0)
        kd, vd, bias4, tabs = _att_prep(i, n, refs[:12], Lc)
        for j in range(ATT_SUB):
            rows = slice(j * BLK, (j + 1) * BLK)
            cq, sq = tabs[j]
            for g in range(ATT_KV):
                q4 = _stack4(refs[0], rows, g, lambda t: _rope(t, cq, sq, 16) * Q_SCALE).astype(BF16)
                e, _, l = _softmax_parts(_dot(q4, kd[j][g], NT), bias4[j], sink_ref, g)
                o4 = _dot(e.astype(BF16), vd[j][g], NN) * (1.0 / l)
                for jp in range(2):
                    c0 = (2 * g + jp) * LANES
                    o_ref[rows, c0:c0 + LANES] = _unstack4(o4, jp).astype(o_ref.dtype)

    return _call(
        body, "att_fwd", (n // ATT_SUB,),
        specs + [pl.BlockSpec((ATT_HEADS, LANES), lambda i: (0, 0)), pl.BlockSpec(memory_space=pl.ANY)],
        [pl.BlockSpec((ATT_SUB * BLK, 1024), lambda i: (i, 1))], [jax.ShapeDtypeStruct((L, 2048), BF16)], [],
        ("parallel",), (*args, sinkb, Y), comm, aliases={13: 0})


def _att_bwd(P, rope, sinkb, Y, dY, L, Lc, comm=()):
    n = L // BLK
    specs, args = _att_inputs(P, rope, L, Lc)
    nk = N_LOC + Lc

    def body(*refs):
        sink_ref, y_ref, dy_ref = refs[12], refs[13], refs[14]
        dq_ref, dkl_ref, dvl_ref, dkx_ref, dvx_ref, dsk_ref = refs[15:21]
        i = pl.program_id(0)

        @pl.when(i == 0)
        def _():
            dkx_ref[...] = jnp.zeros_like(dkx_ref)
            dvx_ref[...] = jnp.zeros_like(dvx_ref)
            dsk_ref[...] = jnp.zeros_like(dsk_ref)

        kd, vd, bias4, tabs = _att_prep(i, n, refs[:12], Lc)
        for j in range(ATT_SUB):
            rows = slice(j * BLK, (j + 1) * BLK)
            cq, sq = tabs[j]
            for t in range(ATT_KV // 2):
                dk_halves, dv_halves = [], []
                for b in range(2):
                    g = 2 * t + b
                    q4 = _stack4(refs[0], rows, g, lambda x: _rope(x, cq, sq, 16) * Q_SCALE).astype(BF16)
                    do4 = _stack4(dy_ref, rows, g)
                    delta = jnp.sum(do4 * _stack4(y_ref, rows, g), axis=-1, keepdims=True)
                    do4b = do4.astype(BF16)
                    e, es, l = _softmax_parts(_dot(q4, kd[j][g], NT), bias4[j], sink_ref, g)
                    inv = 1.0 / l
                    p = e * inv
                    dsc = (p * (_dot(do4b, vd[j][g], NT) - delta)).astype(BF16)
                    dsr = es * inv * delta
                    for r in range(4):
                        h = 4 * g + r
                        dsk_ref[h:h + 1, :] += jnp.zeros((1, LANES), F32) - jnp.sum(dsr[r * BLK:(r + 1) * BLK])
                    dq4 = _dot(dsc, kd[j][g], NN) * A_SCALE
                    for jp in range(2):
                        c0 = (2 * g + jp) * LANES
                        dq_ref[rows, c0:c0 + LANES] = _rope_t(_unstack4(dq4, jp), cq, sq, 16).astype(dq_ref.dtype)
                    dkd = _dot(q4, dsc, TN) * LN2
                    dvd = _dot(do4b, p.astype(BF16), TN)
                    dk_halves.append(dkd[:ATT_DH] + dkd[ATT_DH:])
                    dv_halves.append(dvd[:ATT_DH] + dvd[ATT_DH:])
                dk_t = jnp.concatenate(dk_halves, axis=0).T
                dv_t = jnp.concatenate(dv_halves, axis=0).T
                sl = slice(t * LANES, (t + 1) * LANES)
                dkl_ref[j, :, sl] = dk_t[:N_LOC]
                dvl_ref[j, :, sl] = dv_t[:N_LOC]
                dkx_ref[:, sl] += dk_t[N_LOC:]
                dvx_ref[:, sl] += dv_t[N_LOC:]

    row = pl.BlockSpec((ATT_SUB * BLK, 1024), lambda i: (i, 0))
    loc = pl.BlockSpec((ATT_SUB, N_LOC, 256), lambda i: (i, 0, 0))
    cx = pl.BlockSpec((Lc, 256), lambda i: (0, 0))
    return _call(
        body, "att_bwd", (n // ATT_SUB,),
        specs + [pl.BlockSpec((ATT_HEADS, LANES), lambda i: (0, 0))]
        + [pl.BlockSpec((ATT_SUB * BLK, 1024), lambda i: (i, 1))] * 2,
        [row, loc, loc, cx, cx, pl.BlockSpec((ATT_HEADS, LANES), lambda i: (0, 0))],
        [jax.ShapeDtypeStruct((L, 1024), BF16), jax.ShapeDtypeStruct((n, N_LOC, 256), F32),
         jax.ShapeDtypeStruct((n, N_LOC, 256), F32), jax.ShapeDtypeStruct((Lc, 256), F32),
         jax.ShapeDtypeStruct((Lc, 256), F32), jax.ShapeDtypeStruct((ATT_HEADS, LANES), F32)], [],
        ("arbitrary",), (*args, sinkb, Y, dY), comm)


def _assemble_dp(L, Lc, dqf, dqb, dkf, dkb, dvf, dvb, drg, daq, dkl, dvl, rope_att, dck, dcv, dkx, dvx):
    n = L // BLK
    nc = Lc // BLK

    def body(dqf_r, dqb_r, dkf_r, dkb_r, dvf_r, dvb_r, drg_r, daq_r, kl0, kl1, kl2, vl0, vl1, vl2, rp_r,
             dck_r, dcv_r, dkx_r, dvx_r, o_ref):
        i = pl.program_id(0)

        @pl.when(i < n)
        def _():
            add = lambda a, b: (a[...].astype(F32) + b[...].astype(F32)).astype(o_ref.dtype)
            o_ref[:, C_RQ:C_RK] = add(dqf_r, dqb_r)
            o_ref[:, C_RK:C_RV] = add(dkf_r, dkb_r)
            o_ref[:, C_RV:C_RG] = add(dvf_r, dvb_r)
            o_ref[:, C_RG:C_AQ] = drg_r[...].astype(o_ref.dtype)
            o_ref[:, C_AQ:C_AK] = daq_r[...].astype(o_ref.dtype)
            w0 = jnp.where(i > 0, 1.0, 0.0)
            w2 = jnp.where(i < n - 1, 1.0, 0.0)
            dk = kl0[0] * w0 + kl1[0] + kl2[0] * w2
            dv = vl0[0] * w0 + vl1[0] + vl2[0] * w2
            for t in range(ATT_KV // 2):
                sl = slice(t * LANES, (t + 1) * LANES)
                o_ref[:, C_AK + t * LANES:C_AK + (t + 1) * LANES] = _rope_t(
                    dk[:, sl], rp_r[0], rp_r[1], 16).astype(o_ref.dtype)
            o_ref[:, C_AV:D_PROJ] = dv.astype(o_ref.dtype)

        @pl.when(i >= n)
        def _():
            o_ref[:, C_RQ:C_RK] = jnp.zeros((BLK, C_RK - C_RQ), o_ref.dtype)
            o_ref[:, C_RK:C_RV] = dck_r[...].astype(o_ref.dtype)
            o_ref[:, C_RV:C_RG] = dcv_r[...].astype(o_ref.dtype)
            o_ref[:, C_RG:C_AK] = jnp.zeros((BLK, C_AK - C_RG), o_ref.dtype)
            o_ref[:, C_AK:C_AV] = dkx_r[...].astype(o_ref.dtype)
            o_ref[:, C_AV:D_PROJ] = dvx_r[...].astype(o_ref.dtype)

    xm = lambda i: jnp.minimum(i, n - 1)
    cm = lambda i: jnp.clip(i - n, 0, nc - 1)
    r512 = pl.BlockSpec((BLK, 512), lambda i: (xm(i), 0))
    r1024 = pl.BlockSpec((BLK, 1024), lambda i: (xm(i), 0))
    part = lambda off: pl.BlockSpec((1, BLK, 256), lambda i: (jnp.clip(xm(i) + off, 0, n - 1), 1 - off, 0))
    return pl.pallas_call(
        body, name="assemble_dp", grid=(n + nc,),
        in_specs=[r512, r512, r512, r512, r1024, r1024, r1024, r1024,
                  part(-1), part(0), part(1), part(-1), part(0), part(1),
                  pl.BlockSpec((2, BLK, LANES), lambda i: (0, xm(i), 0)),
                  pl.BlockSpec((BLK, 512), lambda i: (cm(i), 0)), pl.BlockSpec((BLK, 1024), lambda i: (cm(i), 0)),
                  pl.BlockSpec((BLK, 256), lambda i: (cm(i), 0)), pl.BlockSpec((BLK, 256), lambda i: (cm(i), 0))],
        out_specs=pl.BlockSpec((BLK, D_PROJ), lambda i: (i, 0)),
        out_shape=jax.ShapeDtypeStruct((L + Lc, D_PROJ), BF16),
        compiler_params=_cparams(("parallel",)),
    )(dqf, dqb, dkf, dkb, dvf, dvb, drg, daq, dkl, dkl, dkl, dvl, dvl, dvl, rope_att, dck, dcv, dkx, dvx)


def _adam_math(w, g, m, v):
    m = ADAM_B1 * m + (1.0 - ADAM_B1) * g
    v = ADAM_B2 * v + (1.0 - ADAM_B2) * (g * g)
    m_hat = m / (1.0 - ADAM_B1 ** ADAM_STEP)
    v_hat = v / (1.0 - ADAM_B2 ** ADAM_STEP)
    delta = -ADAM_LR * (m_hat / (jnp.sqrt(v_hat) + ADAM_EPS) + ADAM_WD * w)
    return delta, m, v


def _adam(name, w, m, v, g=None, parts=None):
    R, C = w.shape
    tr = _pick(R, (256, 128, 64, 32, 16, 8))
    summed = parts is not None
    n_parts = parts.shape[0] if summed else 0

    def body(w_ref, m_ref, v_ref, g_ref, go_ref, d_ref, mo_ref, vo_ref):
        if summed:
            gv = g_ref[0].astype(F32)
            for j in range(1, n_parts):
                gv = gv + g_ref[j].astype(F32)
        else:
            gv = g_ref[...]
        d, mn, vn = _adam_math(w_ref[...], gv, m_ref[...], v_ref[...])
        go_ref[...] = gv
        d_ref[...] = d
        mo_ref[...] = mn
        vo_ref[...] = vn

    row = pl.BlockSpec((tr, C), lambda i: (i, 0))
    gspec = pl.BlockSpec((n_parts, tr, C), lambda i: (0, i, 0)) if summed else row
    return pl.pallas_call(
        body, name=name, grid=(R // tr,),
        in_specs=[row, row, row, gspec], out_specs=[row] * 4,
        out_shape=[jax.ShapeDtypeStruct((R, C), F32)] * 4,
        compiler_params=_cparams(("parallel",)),
    )(w, m, v, parts if summed else g)


def _rows_full(g):
    _, R, D = g.shape
    return g.reshape(N_DEV * R, D)


def _rows_slots(g):
    N, D = g.shape
    return g.reshape(N_DEV, N // N_DEV, D)


def _pad_rows(a, rows):
    return jnp.concatenate([a, jnp.zeros((rows - a.shape[0],) + a.shape[1:], a.dtype)], axis=0)


def kernel(x, c, ctx, c_ctx, w_mod, b_mod, norm_mix, norm_ffn, w_in, ret_decay, attn_sink, w_out, w_gate, w_up, w_down, norm_final, loss_target, m_c_ctx, m_w_mod, m_b_mod, m_norm_mix, m_norm_ffn, m_w_in, m_ret_decay, m_attn_sink, m_w_out, m_w_gate, m_w_up, m_w_down, m_norm_final, v_c_ctx, v_w_mod, v_b_mod, v_norm_mix, v_norm_ffn, v_w_in, v_ret_decay, v_attn_sink, v_w_out, v_w_gate, v_w_up, v_w_down, v_norm_final):
    L, D = x.shape[1], x.shape[2]
    Lc = ctx.shape[1]
    DF = w_gate.shape[2] * N_DEV
    C6 = w_mod.shape[2]
    me = _my_id()
    xs, cx, tgt = x[0], ctx[0], loss_target[0]

    ag_in = ("ag2", w_in[0].T.astype(BF16))
    ag_out, ag_gate = ("ag2", w_out[0].astype(BF16)), ("ag2", w_gate[0].T.astype(BF16))
    ag_up, ag_down = ("ag2", w_up[0].T.astype(BF16)), ("ag2", w_down[0].astype(BF16))

    cs = _allgather(c, "ag_c")[:, 0, :]
    s_in = _pad_rows(jnp.concatenate([cs, c_ctx[None, :]], axis=0), 16)
    b_l = lax.dynamic_slice_in_dim(b_mod, me * C6, C6, axis=1)
    mod_parts = _allgather(_mod_fwd(s_in, w_mod[0], b_l), "ag_mod")
    mod = _pad_rows(lax.dynamic_index_in_dim(mod_parts, me, axis=1, keepdims=False).reshape(6, D), 8)
    modc = _pad_rows(mod_parts[:, N_DEV, :].reshape(6, D), 8)
    mix_mod, ffn_mod = mod, jnp.roll(mod, -3, axis=0)
    gt_m, gt_f = mod[2:3], mod[5:6]

    rope_ret, rope_att = _rope_tables(L)
    rdb = jnp.broadcast_to(ret_decay[0].reshape(2 * RET_HEADS, 1, 1), (2 * RET_HEADS, 1, LANES))
    sinkb = jnp.broadcast_to(attn_sink[0].reshape(ATT_HEADS, 1), (ATT_HEADS, LANES))

    tm = _pick(L + Lc, (1408, 768, 512, 384, 256, 128))
    tmx = _pick(L, (1024, 512, 256, 128))

    (H,), (g_in,) = _modulate_fwd("mod_mix_fwd", xs, cx, norm_mix, mix_mod, modc, comm=[ag_in])
    W_inT = _rows_full(g_in)
    ident = lambda a, e: a
    tP, tD, tF = _pick(D_PROJ, (1152, 768, 512)), _pick(D, (2048, 1024, 512)), _pick(DF, (512, 256, 128))
    (P,), (g_gate,) = _matmul("mm_in", [(H, W_inT, 0)], 1, L + Lc, D_PROJ, D, "nt",
                              (tm, _pick(D_PROJ, (1536, 768, 512)), D), [], [BF16], ident,
                              comm=[ag_gate])
    W_gateT = _rows_full(g_gate)
    tabs, ctab = _ret_tables(rdb, Lc)
    s0 = _ret_ctx_state(P, ctab, L, Lc)
    (o_f, o_b, st_f, st_b), (g_out,) = _ret_fwd(P, rope_ret, tabs, s0, L, comm=[ag_out])
    W_out = _rows_full(g_out)
    Y_half = _ret_finish_fwd(o_f, o_b, P, L)
    (Y,), (g_up,) = _att_fwd(P, rope_att, sinkb, Y_half, L, Lc, comm=[ag_up])
    W_upT = _rows_full(g_up)
    KO = Y.shape[1]
    f_mix = _matmul("mm_out", [(Y, W_out, 0)], 1, L, D, KO, "nn", (tmx, tD, KO), [], [BF16], ident)[0]

    x1, H2 = _residual_modulate_fwd("mod_ffn_fwd", xs, f_mix, gt_m, norm_ffn, ffn_mod)

    def swiglu_epi(a, e):
        sg = _sigmoid(a[0])
        act = a[0] * sg
        return [act, a[1] * (sg * (1.0 + a[0] * (1.0 - sg))), act * a[1]]

    tm2 = tmx
    (act, up_dact, hmid), (g_down,) = _matmul("mm_gate_up", [(H2, W_gateT, 0), (H2, W_upT, 1)], 2, L, DF, D, "nt",
                                              (_pick(L, (2048, 1024, 512, 256, 128)), tF, D), [],
                                              [BF16, BF16, BF16], swiglu_epi, comm=[ag_down])
    W_down = _rows_full(g_down)
    f_ffn = _matmul("mm_down", [(hmid, W_down, 0)], 1, L, D, DF, "nn",
                    (tm2, _pick(D, (1024, 512)), _pick(DF, (2816, 512, 256, 128))), [], [BF16], ident)[0]

    dx2, dFf, sums_l = _loss_head(x1, tgt, norm_final.reshape(1, D), f_ffn, gt_f)

    def dswiglu_epi(a, e):
        return [a[0] * e[0].astype(F32), a[0] * e[1].astype(F32)]

    dga, dup = _matmul("mm_d_down", [(dFf, W_down, 0)], 1, L, DF, D, "nt", (_pick(L, (2048, 1024, 512, 256, 128)), tF, D),
                       [(up_dact, "mn"), (act, "mn")], [BF16, BF16], dswiglu_epi)
    tkt, tkl = _pick(L, (512, 256, 128)), _pick(L, (1024, 512, 256, 128))
    dW_down = _matmul("mm_gw_down", [(hmid, dFf, 0)], 1, DF, D, L, "tn",
                      (_pick(DF, (1408, 512, 256, 128)), tD, tkl), [], [BF16], ident)[0]
    (dW_gateT, dW_upT), (p_down,) = _matmul("mm_gw_gate_up", [(dga, H2, 0), (dup, H2, 1)], 2, DF, D, L, "tn",
                                            (tF, tD, _pick(L, (2048, 1024, 512, 256, 128))), [], [BF16, BF16], ident,
                                            comm=[("a2a", _rows_slots(dW_down))])
    (dH2,), (p_gate,) = _matmul("mm_d_gate_up", [(dga, W_gateT, 0), (dup, W_upT, 0)], 1, L, D, DF, "nn",
                                (_pick(L, (2048, 1024, 512, 256, 128)), tD, tF), [], [BF16], ident,
                                comm=[("a2a", _rows_slots(dW_gateT))])
    dx1, dFm, sums_f = _modulate_bwd("mod_ffn_bwd", x1, None, dH2, norm_ffn, ffn_mod, None, dx2, f_mix, gt_m)

    tO = _pick(KO, (2048, 1024, 512))
    dY = _matmul("mm_d_out", [(dFm, W_out, 0)], 1, L, KO, D, "nt", (tmx, tO, D), [], [BF16], ident)[0]
    dW_out = _matmul("mm_gw_out", [(Y, dFm, 0)], 1, KO, D, L, "tn",
                     (_pick(KO, (1024, 512)), tD, _pick(L, (2048, 1024, 512, 256, 128))), [], [BF16],
                     ident)[0]
    dO, drg = _ret_finish_bwd(o_f, o_b, P, dY, L)
    (dqf, dkf, dvf, dqb, dkb, dvb, ds0, dlg), (p_out,) = _ret_bwd(
        P, rope_ret, tabs, st_f, st_b, dO, L, comm=[("a2a", _rows_slots(dW_out))])
    dck, dcv, d_rd = _ret_ctx_bwd(P, ctab, ds0, dlg, rdb, L, Lc)
    (daq, dkl, dvl, dkx, dvx, d_sink), (p_up,) = _att_bwd(
        P, rope_att, sinkb, Y, dY, L, Lc, comm=[("a2a", _rows_slots(dW_upT))])
    dP = _assemble_dp(L, Lc, dqf, dqb, dkf, dkb, dvf, dvb, drg, daq, dkl, dvl, rope_att, dck, dcv, dkx, dvx)
    tkc = _pick(L + Lc, (768, 256, 128))
    dW_inT = _matmul("mm_gw_in", [(dP, H, 0)], 1, D_PROJ, D, L + Lc, "tn", (tP, tD, tkc), [], [BF16], ident)[0]
    (dH,), (p_in,) = _matmul("mm_d_in", [(dP, W_inT, 0)], 1, L + Lc, D, D_PROJ, "nn",
                             (tm, tD, _pick(D_PROJ, (768, 512, 256))), [], [BF16], ident,
                             comm=[("a2a", _rows_slots(dW_inT))])
    grad_x, sums_m = _modulate_bwd("mod_mix_bwd", xs, cx, dH, norm_mix, mix_mod, modc, dx1, None, None)

    zero = jnp.zeros((1, D), F32)
    dmod = jnp.concatenate([sums_m[0:1], sums_m[1:2], sums_f[6:7], sums_f[0:1], sums_f[1:2], sums_l[2:3]], axis=1)
    dmodc = jnp.concatenate([sums_m[3:4], sums_m[4:5], zero, zero, zero, zero], axis=1)
    dm_all = _allgather(jnp.concatenate([dmod, dmodc], axis=0), "ag_dmod")
    dm_cols = lax.dynamic_slice_in_dim(dm_all, me * C6, C6, axis=2)
    dm_in = jnp.concatenate([dm_cols[:, 0, :], dm_cols[:, 1, :]], axis=0)
    s_bwd = jnp.concatenate([cs, jnp.broadcast_to(c_ctx[None, :], (N_DEV, D))], axis=0)
    g_w_mod, dsil = _mod_bwd(s_bwd, dm_in, w_mod[0])

    lane_pad = lambda a: _pad_rows(a.reshape(-1, 1), LANES).reshape(1, LANES)
    pack = jnp.concatenate([dsil[0:1], sums_m[2:3], sums_f[2:3], sums_l[1:2],
                            lane_pad(d_rd[:, 0]), lane_pad(d_sink[:, 0]), sums_l[3:4, 0:LANES]], axis=1)
    packs = _allgather(pack, "ag_small")
    zl = jnp.zeros((1, LANES), F32)

    def pack_w(a_c, a_nm, a_nf, a_fin, a_rd, a_sk):
        return jnp.concatenate([a_c.reshape(1, D), a_nm, a_nf, a_fin.reshape(1, D), lane_pad(a_rd.reshape(-1)),
                                lane_pad(a_sk.reshape(-1)), zl], axis=1)

    sg, sd, sm, sv = _adam("adam_small", pack_w(c_ctx, norm_mix, norm_ffn, norm_final, ret_decay, attn_sink),
                           pack_w(m_c_ctx, m_norm_mix, m_norm_ffn, m_norm_final, m_ret_decay, m_attn_sink),
                           pack_w(v_c_ctx, v_norm_mix, v_norm_ffn, v_norm_final, v_ret_decay, v_attn_sink),
                           parts=packs)
    loss = sg[0, 4 * D + 2 * LANES]

    def unpack(a):
        return (a[0, 0:D], a[:, D:2 * D], a[:, 2 * D:3 * D], a[0, 3 * D:4 * D],
                a[0, 4 * D:4 * D + 2 * RET_HEADS].reshape(1, 2, RET_HEADS),
                a[:, 4 * D + LANES:4 * D + LANES + ATT_HEADS])

    bg, bd, bm, bv = _adam("adam_b_mod", b_mod, m_b_mod, v_b_mod, parts=dm_all.reshape(2 * N_DEV, 1, 6 * D))
    wg, wd, wm, wv = _adam("adam_w_mod", w_mod[0], m_w_mod[0], v_w_mod[0], g=g_w_mod)

    big = {}
    for nm, w, m, v, parts, transposed in (
            ("w_in", w_in, m_w_in, v_w_in, p_in, True), ("w_out", w_out, m_w_out, v_w_out, p_out, False),
            ("w_gate", w_gate, m_w_gate, v_w_gate, p_gate, True), ("w_up", w_up, m_w_up, v_w_up, p_up, True),
            ("w_down", w_down, m_w_down, v_w_down, p_down, False)):
        if transposed:
            res = [a.T for a in _adam("adam_" + nm, w[0].T, m[0].T, v[0].T, parts=parts)]
        else:
            res = _adam("adam_" + nm, w[0], m[0], v[0], parts=parts)
        big[nm] = [a[None] for a in res]

    g_s, d_s, m_s, v_s = unpack(sg), unpack(sd), unpack(sm), unpack(sv)

    def leaves(k, small, bmod, wmod):
        return (small[0], wmod[None], bmod, small[1], small[2], big["w_in"][k], small[4], small[5],
                big["w_out"][k], big["w_gate"][k], big["w_up"][k], big["w_down"][k], small[3])

    return (loss, grad_x[None], *leaves(0, g_s, bg, wg), *leaves(1, d_s, bd, wd),
            *leaves(2, m_s, bm, wm), *leaves(3, v_s, bv, wv))
```

```python
import functools

import jax
import jax.numpy as jnp
from jax import lax
from jax.experimental import pallas as pl
from jax.experimental.pallas import tpu as pltpu

F32 = jnp.float32
BF16 = jnp.bfloat16

N_DEV = 8
LANES = 128
RET_HEADS = 8
RET_DK = 64
RET_DV = 128
CHUNK = 128
ATT_HEADS = 16
ATT_KV = 4
ATT_DH = 64
GRID_W = 64
ROPE_BASE = 10000.0
EPS = 1e-6
NEG = -1e30
C_RQ, C_RK, C_RV, C_RG, C_AQ, C_AK, C_AV, D_PROJ = 0, 512, 1024, 2048, 3072, 4096, 4352, 4608
K_SCALE = RET_DK ** -0.5
A_SCALE = ATT_DH ** -0.5

ADAM_LR, ADAM_B1, ADAM_B2, ADAM_EPS, ADAM_WD, ADAM_STEP = 0.001, 0.9, 0.999, 1e-08, 0.01, 10

VMEM_BIG = 58 * 1024 * 1024

NN = (((1,), (0,)), ((), ()))
NT = (((1,), (1,)), ((), ()))
TN = (((0,), (0,)), ((), ()))


def _dot(a, b, dims):
    return lax.dot_general(a, b, dims, preferred_element_type=F32)


def _cparams(sem, vmem=VMEM_BIG):
    return pltpu.CompilerParams(dimension_semantics=sem, vmem_limit_bytes=vmem)


def _pick(dim, prefs):
    for p in prefs:
        if dim % p == 0:
            return p
    return dim


def _my_id():
    return lax.axis_index("x") * 4 + lax.axis_index("y") * 2 + lax.axis_index("c")


def _sigmoid(x):
    return 0.5 * jnp.tanh(0.5 * x) + 0.5


def _peers():
    mx, my, mc = lax.axis_index("x"), lax.axis_index("y"), lax.axis_index("c")
    out = []
    for k in range(1, N_DEV):
        kx, ky, kc = (k >> 2) & 1, (k >> 1) & 1, k & 1
        px = 1 - mx if kx else mx
        py = 1 - my if ky else my
        pc = 1 - mc if kc else mc
        out.append(((px, py, pc), px * 4 + py * 2 + pc))
    return out


def _exchange_copies(kind, x_ref, o_ref, ssem, rsem, lsem):
    me = _my_id()
    loc = pltpu.make_async_copy(x_ref if kind == "ag" else x_ref.at[me], o_ref.at[me], lsem)
    cps = []
    for k, (peer, pid) in enumerate(_peers()):
        cps.append(pltpu.make_async_remote_copy(
            src_ref=x_ref if kind == "ag" else x_ref.at[pid], dst_ref=o_ref.at[me],
            send_sem=ssem.at[k], recv_sem=rsem.at[k], device_id=peer, device_id_type=pl.DeviceIdType.MESH))
    return loc, cps


def _two_level_copies(x_ref, o_ref, ssem, rsem, lsem):
    mx, my, mc = lax.axis_index("x"), lax.axis_index("y"), lax.axis_index("c")
    me = mx * 4 + my * 2 + mc
    sibling = (mx, my, 1 - mc)
    chips = [(1 - mx, my), (mx, 1 - my), (1 - mx, 1 - my)]

    def copy(k, slot, to, src=None):
        return pltpu.make_async_remote_copy(
            src_ref=o_ref.at[slot] if src is None else src, dst_ref=o_ref.at[slot],
            send_sem=ssem.at[k], recv_sem=rsem.at[k], device_id=to, device_id_type=pl.DeviceIdType.MESH)

    loc = pltpu.make_async_copy(x_ref, o_ref.at[me], lsem)
    first = [copy(0, me, sibling, src=x_ref)]
    first += [copy(1 + j, me, (cx, cy, mc), src=x_ref) for j, (cx, cy) in enumerate(chips)]
    passed = [copy(4 + j, cx * 4 + cy * 2 + mc, sibling) for j, (cx, cy) in enumerate(chips)]
    return loc, first, passed


def _exchange_start(kind, x_ref, o_ref, ssem, rsem, lsem):
    if kind == "ag2":
        loc, first, _ = _two_level_copies(x_ref, o_ref, ssem, rsem, lsem)
        cps = first
    else:
        loc, cps = _exchange_copies(kind, x_ref, o_ref, ssem, rsem, lsem)
    loc.start()
    for cp in cps:
        cp.start()


def _exchange_pass_on(kind, x_ref, o_ref, ssem, rsem, lsem):
    if kind == "ag2":
        _, first, passed = _two_level_copies(x_ref, o_ref, ssem, rsem, lsem)
        for j in range(3):
            first[1 + j].wait_recv()
            passed[j].start()


def _exchange_wait(kind, x_ref, o_ref, ssem, rsem, lsem):
    if kind == "ag2":
        loc, first, passed = _two_level_copies(x_ref, o_ref, ssem, rsem, lsem)
        first[0].wait_recv()
        for cp in passed:
            cp.wait_recv()
        cps = first + passed
    else:
        loc, cps = _exchange_copies(kind, x_ref, o_ref, ssem, rsem, lsem)
        for cp in cps:
            cp.wait_recv()
    for cp in cps:
        cp.wait_send()
    loc.wait()


_EXCHANGE_SEMS = [pltpu.SemaphoreType.DMA((N_DEV - 1,)), pltpu.SemaphoreType.DMA((N_DEV - 1,)),
                  pltpu.SemaphoreType.DMA(())]


def _exchange_shape(kind, x):
    return jax.ShapeDtypeStruct(x.shape if kind == "a2a" else (N_DEV,) + x.shape, x.dtype)


def _exchange(kind, x, name):
    def body(x_ref, o_ref, ssem, rsem, lsem):
        _exchange_start(kind, x_ref, o_ref, ssem, rsem, lsem)
        _exchange_pass_on(kind, x_ref, o_ref, ssem, rsem, lsem)
        _exchange_wait(kind, x_ref, o_ref, ssem, rsem, lsem)

    return pl.pallas_call(
        body, name=name, out_shape=_exchange_shape(kind, x),
        in_specs=[pl.BlockSpec(memory_space=pl.ANY)], out_specs=pl.BlockSpec(memory_space=pl.ANY),
        scratch_shapes=list(_EXCHANGE_SEMS),
    )(x)


def _allgather(x, name):
    return _exchange("ag", x, name)


def _call(body, name, grid, in_specs, out_specs, out_shape, scratch_shapes, sem, args, comm=(), aliases=None,
          pass_on_at=0.75):
    in_specs, out_specs, out_shape = list(in_specs), list(out_specs), list(out_shape)
    scratch_shapes = list(scratch_shapes)
    aliases = aliases or {}
    if not comm:
        outs = pl.pallas_call(body, name=name, grid=grid, in_specs=in_specs, out_specs=out_specs, out_shape=out_shape,
                              scratch_shapes=scratch_shapes, input_output_aliases=aliases,
                              compiler_params=_cparams(sem))(*args)
        return list(outs), []
    n_in, n_out, n_scr, n_c = len(in_specs), len(out_specs), len(scratch_shapes), len(comm)
    hbm = pl.BlockSpec(memory_space=pl.ANY)

    def wrapped(*refs):
        ins, cins = refs[:n_in], refs[n_in:n_in + n_c]
        outs = refs[n_in + n_c:n_in + n_c + n_out]
        couts = refs[n_in + n_c + n_out:n_in + 2 * n_c + n_out]
        scr = refs[n_in + 2 * n_c + n_out:n_in + 2 * n_c + n_out + n_scr]
        sems = refs[n_in + 2 * n_c + n_out + n_scr:]
        step, total = pl.program_id(0), grid[0]
        for ax in range(1, len(grid)):
            step = step * grid[ax] + pl.program_id(ax)
            total *= grid[ax]

        @pl.when(step == 0)
        def _():
            for c, (kind, _) in enumerate(comm):
                _exchange_start(kind, cins[c], couts[c], *sems[3 * c:3 * c + 3])

        body(*ins, *outs, *scr)

        @pl.when(step == min(total - 1, int(total * pass_on_at)))
        def _():
            for c, (kind, _) in enumerate(comm):
                _exchange_pass_on(kind, cins[c], couts[c], *sems[3 * c:3 * c + 3])

        @pl.when(step == total - 1)
        def _():
            for c, (kind, _) in enumerate(comm):
                _exchange_wait(kind, cins[c], couts[c], *sems[3 * c:3 * c + 3])

    res = pl.pallas_call(
        wrapped, name=name, grid=grid,
        in_specs=in_specs + [hbm] * n_c, out_specs=out_specs + [hbm] * n_c,
        out_shape=out_shape + [_exchange_shape(kind, arr) for kind, arr in comm],
        scratch_shapes=scratch_shapes + list(_EXCHANGE_SEMS) * n_c, input_output_aliases=aliases,
        compiler_params=_cparams(("arbitrary",) * len(grid)),
    )(*args, *[arr for _, arr in comm])
    return list(res[:n_out]), list(res[n_out:])


def _matmul(name, pairs, n_acc, M, N, K, mode, tiles, extras, out_dtypes, epilogue, j_outer=False, comm=()):
    tm, tn, tk = tiles
    gm, gn, nk = M // tm, N // tn, K // tk
    assert gm * tm == M and gn * tn == N and nk * tk == K, (name, M, N, K, tiles)
    if j_outer:
        grid = (gn, gm, nk)
        ij = lambda g0, g1: (g1, g0)
    else:
        grid = (gm, gn, nk)
        ij = lambda g0, g1: (g0, g1)

    if mode in ("nn", "nt"):
        a_spec = pl.BlockSpec((tm, tk), lambda g0, g1, k: (ij(g0, g1)[0], k))
    else:
        a_spec = pl.BlockSpec((tk, tm), lambda g0, g1, k: (k, ij(g0, g1)[0]))
    if mode == "nt":
        b_spec = pl.BlockSpec((tn, tk), lambda g0, g1, k: (ij(g0, g1)[1], k))
    else:
        b_spec = pl.BlockSpec((tk, tn), lambda g0, g1, k: (k, ij(g0, g1)[1]))
    dims = {"nn": NN, "nt": NT, "tn": TN}[mode]
    mn_spec = pl.BlockSpec((tm, tn), lambda g0, g1, k: ij(g0, g1))
    n_spec = pl.BlockSpec((1, tn), lambda g0, g1, k: (0, ij(g0, g1)[1]))

    in_specs, args = [], []
    for a, b, _ in pairs:
        in_specs += [a_spec, b_spec]
        args += [a, b]
    for arr, kind in extras:
        in_specs.append(mn_spec if kind == "mn" else n_spec)
        args.append(arr)
    n_p, n_e, n_o = len(pairs), len(extras), len(out_dtypes)

    def body(*refs):
        ab = refs[:2 * n_p]
        ex = refs[2 * n_p:2 * n_p + n_e]
        outs = refs[2 * n_p + n_e:2 * n_p + n_e + n_o]
        accs = refs[2 * n_p + n_e + n_o:]
        k = pl.program_id(2)

        def single_step():
            rc = _pick(tm, (1024,)) if mode != "tn" else tm
            for c in range(tm // rc):
                rs = slice(c * rc, (c + 1) * rc)
                sums = [None] * n_acc
                for p, (_, _, ai) in enumerate(pairs):
                    a_ref, b_ref = ab[2 * p], ab[2 * p + 1]
                    d = _dot(a_ref[...] if mode == "tn" else a_ref[rs, :], b_ref[...], dims)
                    sums[ai] = d if sums[ai] is None else sums[ai] + d
                res = epilogue(sums, [e[rs, :] if e.shape[0] == tm else e[...] for e in ex])
                for o, r in zip(outs, res):
                    o[rs, :] = r.astype(o.dtype)

        def finish(acc_vals):
            res = epilogue(acc_vals, [e[...] for e in ex])
            for o, r in zip(outs, res):
                o[...] = r.astype(o.dtype)

        def accumulate(first):
            w = _pick(tm if mode == "tn" else tn, (512, 384, 256))
            for c in range((tm if mode == "tn" else tn) // w):
                sl = slice(c * w, (c + 1) * w)
                sums = [None] * n_acc
                for p, (_, _, ai) in enumerate(pairs):
                    a_ref, b_ref = ab[2 * p], ab[2 * p + 1]
                    if mode == "tn":
                        d = _dot(a_ref[:, sl], b_ref[...], dims)
                    elif mode == "nn":
                        d = _dot(a_ref[...], b_ref[:, sl], dims)
                    else:
                        d = _dot(a_ref[...], b_ref[sl, :], dims)
                    sums[ai] = d if sums[ai] is None else sums[ai] + d
                idx = (sl, slice(None)) if mode == "tn" else (slice(None), sl)
                for ai, s in enumerate(sums):
                    if first:
                        accs[ai][idx] = s
                    else:
                        accs[ai][idx] += s

        if nk == 1:
            single_step()
        else:
            pl.when(k == 0)(functools.partial(accumulate, True))
            pl.when(k > 0)(functools.partial(accumulate, False))

            @pl.when(k == nk - 1)
            def _():
                finish([a[...] for a in accs])

    outs, couts = _call(
        body, name, grid, in_specs, [mn_spec] * n_o,
        [jax.ShapeDtypeStruct((M, N), dt) for dt in out_dtypes],
        [pltpu.VMEM((tm, tn), F32) for _ in range(n_acc if nk > 1 else 0)],
        ("parallel", "parallel", "arbitrary"), args, comm)
    return (outs, couts) if comm else outs


def _rope_tables(L):
    t = jnp.arange(L, dtype=jnp.int32)
    f = jnp.arange(32, dtype=jnp.int32).astype(F32)
    ang = t.astype(F32)[:, None] * (ROPE_BASE ** (-f / 32.0))[None, :]
    cos, sin = jnp.cos(ang), jnp.sin(ang)
    ret = jnp.stack([jnp.tile(cos, (1, 4)), jnp.tile(jnp.concatenate([-sin, sin], axis=1), (1, 2))])
    f2 = jnp.arange(16, dtype=jnp.int32).astype(F32)
    inv2 = (ROPE_BASE ** (-f2 / 16.0))[None, :]
    ang_r = (t // GRID_W).astype(F32)[:, None] * inv2
    ang_c = (t % GRID_W).astype(F32)[:, None] * inv2
    cr, sr, cc, sc = jnp.cos(ang_r), jnp.sin(ang_r), jnp.cos(ang_c), jnp.sin(ang_c)
    att = jnp.stack([jnp.tile(jnp.concatenate([cr, cr, cc, cc], axis=1), (1, 2)),
                     jnp.tile(jnp.concatenate([-sr, sr, -sc, sc], axis=1), (1, 2))])
    return ret.astype(F32), att.astype(F32)


def _swap(x, sh):
    lane = lax.broadcasted_iota(jnp.int32, x.shape, 1)
    ra = pltpu.roll(x, LANES - sh, 1)
    rb = pltpu.roll(x, sh, 1)
    la = pltpu.roll(lane, LANES - sh, 1)
    partner = jnp.where((lane % (2 * sh)) < sh, lane + sh, lane - sh)
    return jnp.where(la == partner, ra, rb)


def _rope(x, cos, sin, sh):
    return x * cos + _swap(x, sh) * sin


def _rope_t(d, cos, sin, sh):
    return d * cos + _swap(d * sin, sh)


def _half_mask(shape, a):
    lane = lax.broadcasted_iota(jnp.int32, shape, 1)
    return (lane < 64) if a == 0 else (lane >= 64)


def _mod_fwd(s_in, w_l, b_l):
    D, C6 = w_l.shape
    tk = _pick(D, (512, 256, 128))
    nk = D // tk

    def body(s_ref, w_ref, b_ref, o_ref):
        k = pl.program_id(0)
        s = s_ref[...]
        s = s * _sigmoid(s)
        d = jnp.dot(s, w_ref[...], preferred_element_type=F32, precision=lax.Precision.HIGHEST)

        @pl.when(k == 0)
        def _():
            o_ref[...] = d + b_ref[...]

        @pl.when(k > 0)
        def _():
            o_ref[...] += d

    return pl.pallas_call(
        body, name="mod_fwd", grid=(nk,),
        in_specs=[pl.BlockSpec((16, tk), lambda k: (0, k)), pl.BlockSpec((tk, C6), lambda k: (k, 0)),
                  pl.BlockSpec((1, C6), lambda k: (0, 0))],
        out_specs=pl.BlockSpec((16, C6), lambda k: (0, 0)),
        out_shape=jax.ShapeDtypeStruct((16, C6), F32),
        compiler_params=_cparams(("arbitrary",)),
    )(s_in, w_l, b_l)


def _mod_bwd(s_in, dm, w_l):
    D, C6 = w_l.shape
    tk = _pick(D, (512, 256, 128))
    nk = D // tk

    def body(s_ref, dm_ref, w_ref, gw_ref, gc_ref):
        s = s_ref[...]
        sg = _sigmoid(s)
        act = s * sg
        dmv = dm_ref[...]
        gw_ref[...] = lax.dot_general(act, dmv, TN, preferred_element_type=F32, precision=lax.Precision.HIGHEST)
        ds = lax.dot_general(dmv, w_ref[...], NT, preferred_element_type=F32, precision=lax.Precision.HIGHEST)
        dsil = (sg * (1.0 + s * (1.0 - sg)))[8:9, :]
        gc_ref[...] = jnp.zeros((8, tk), F32) + jnp.sum(ds[8:16, :], axis=0, keepdims=True) * dsil

    return pl.pallas_call(
        body, name="mod_bwd", grid=(nk,),
        in_specs=[pl.BlockSpec((16, tk), lambda k: (0, k)), pl.BlockSpec((16, C6), lambda k: (0, 0)),
                  pl.BlockSpec((tk, C6), lambda k: (k, 0))],
        out_specs=[pl.BlockSpec((tk, C6), lambda k: (k, 0)), pl.BlockSpec((8, tk), lambda k: (0, k))],
        out_shape=[jax.ShapeDtypeStruct((D, C6), F32), jax.ShapeDtypeStruct((8, D), F32)],
        compiler_params=_cparams(("parallel",)),
    )(s_in, dm, w_l)


def _norm_rows(x):
    r = lax.rsqrt(jnp.mean(x * x, axis=-1, keepdims=True) + EPS)
    return x * r, r


def _modulate_fwd(name, x, ctx, g, mod, modc, comm=()):
    L, D = x.shape
    tr = ctx.shape[0]
    nx = L // tr

    def body(x_ref, c_ref, g_ref, m_ref, mc_ref, o_ref):
        i = pl.program_id(0)

        def run(src, m):
            n, _ = _norm_rows(src[...])
            o_ref[...] = (n * g_ref[...] * (1.0 + m[1:2, :]) + m[0:1, :]).astype(o_ref.dtype)

        @pl.when(i < nx)
        def _():
            run(x_ref, m_ref)

        @pl.when(i >= nx)
        def _():
            run(c_ref, mc_ref)

    row = pl.BlockSpec((tr, D), lambda i: (jnp.minimum(i, nx - 1), 0))
    vec = pl.BlockSpec((1, D), lambda i: (0, 0))
    mv = pl.BlockSpec((8, D), lambda i: (0, 0))
    return _call(
        body, name, (nx + 1,), [row, pl.BlockSpec((tr, D), lambda i: (0, 0)), vec, mv, mv],
        [pl.BlockSpec((tr, D), lambda i: (i, 0))], [jax.ShapeDtypeStruct((L + tr, D), BF16)], [],
        ("parallel",), (x, ctx, g, mod, modc), comm, pass_on_at=1.0)


def _residual_modulate_fwd(name, x, fbr, gate, g, mod):
    L, D = x.shape
    tr = _pick(L, (512, 256, 128))

    def body(x_ref, f_ref, gt_ref, g_ref, m_ref, x1_ref, o_ref):
        x1 = x_ref[...] + gt_ref[...] * f_ref[...].astype(F32)
        x1_ref[...] = x1
        n, _ = _norm_rows(x1)
        o_ref[...] = (n * g_ref[...] * (1.0 + m_ref[1:2, :]) + m_ref[0:1, :]).astype(o_ref.dtype)

    row = pl.BlockSpec((tr, D), lambda i: (i, 0))
    vec = pl.BlockSpec((1, D), lambda i: (0, 0))
    return pl.pallas_call(
        body, name=name, grid=(L // tr,),
        in_specs=[row, row, vec, vec, pl.BlockSpec((8, D), lambda i: (0, 0))],
        out_specs=[row, row],
        out_shape=[jax.ShapeDtypeStruct((L, D), F32), jax.ShapeDtypeStruct((L, D), BF16)],
        compiler_params=_cparams(("parallel",)),
    )(x, fbr, gate, g, mod)


def _modulate_bwd(name, x, ctx, dh, g, mod, modc, dres, fbr, gate):
    L, D = x.shape
    tr = ctx.shape[0] if ctx is not None else _pick(L, (512, 256, 128))
    nx = L // tr
    nt = nx + (1 if ctx is not None else 0)
    has_f = fbr is not None

    def body(*refs):
        refs = list(refs)
        x_ref = refs.pop(0)
        c_ref = refs.pop(0) if ctx is not None else None
        dh_ref, g_ref, m_ref = refs.pop(0), refs.pop(0), refs.pop(0)
        mc_ref = refs.pop(0) if ctx is not None else None
        dr_ref = refs.pop(0)
        f_ref = refs.pop(0) if has_f else None
        gt_ref = refs.pop(0) if has_f else None
        dx_ref = refs.pop(0)
        df_ref = refs.pop(0) if has_f else None
        acc_ref = refs.pop(0)
        i = pl.program_id(0)

        @pl.when(i == 0)
        def _():
            acc_ref[...] = jnp.zeros_like(acc_ref)

        def sums(src, m, base, grow):
            n, r = _norm_rows(src[...])
            d = dh_ref[...].astype(F32)
            gg = g_ref[...]
            sc1 = 1.0 + m[1:2, :]
            acc_ref[base:base + 1, :] += jnp.sum(d, axis=0, keepdims=True)
            dn = d * n
            acc_ref[base + 1:base + 2, :] += jnp.sum(dn, axis=0, keepdims=True) * gg
            acc_ref[grow:grow + 1, :] += jnp.sum(dn, axis=0, keepdims=True) * sc1
            dnv = d * (gg * sc1)
            return r * (dnv - n * jnp.mean(dnv * n, axis=-1, keepdims=True))

        def x_rows():
            dx = sums(x_ref, m_ref, 0, 2) + dr_ref[...]
            dx_ref[...] = dx
            if has_f:
                acc_ref[6:7, :] += jnp.sum(dx * f_ref[...].astype(F32), axis=0, keepdims=True)
                df_ref[...] = (dx * gt_ref[...]).astype(df_ref.dtype)

        if ctx is None:
            x_rows()
        else:
            pl.when(i < nx)(x_rows)

            @pl.when(i >= nx)
            def _():
                sums(c_ref, mc_ref, 3, 2)

    row = pl.BlockSpec((tr, D), lambda i: (jnp.minimum(i, nx - 1), 0))
    vec = pl.BlockSpec((1, D), lambda i: (0, 0))
    mv = pl.BlockSpec((8, D), lambda i: (0, 0))
    in_specs, args = [row], [x]
    if ctx is not None:
        in_specs.append(pl.BlockSpec((tr, D), lambda i: (0, 0)))
        args.append(ctx)
    in_specs += [pl.BlockSpec((tr, D), lambda i: (i, 0)), vec, mv]
    args += [dh, g, mod]
    if ctx is not None:
        in_specs.append(mv)
        args.append(modc)
    in_specs.append(row)
    args.append(dres)
    out_specs = [row]
    out_shape = [jax.ShapeDtypeStruct((L, D), F32)]
    if has_f:
        in_specs += [row, vec]
        args += [fbr, gate]
        out_specs.append(row)
        out_shape.append(jax.ShapeDtypeStruct((L, D), BF16))
    out_specs.append(pl.BlockSpec((16, D), lambda i: (0, 0)))
    out_shape.append(jax.ShapeDtypeStruct((16, D), F32))
    return pl.pallas_call(
        body, name=name, grid=(nt,), in_specs=in_specs, out_specs=out_specs, out_shape=out_shape,
        compiler_params=_cparams(("arbitrary",)),
    )(*args)


def _loss_head(x1, tgt, nf, fbr, gate):
    L, D = x1.shape
    tr = _pick(L, (512, 256, 128))

    def body(x_ref, t_ref, w_ref, f_ref, gt_ref, dx_ref, df_ref, acc_ref):
        i = pl.program_id(0)

        @pl.when(i == 0)
        def _():
            acc_ref[...] = jnp.zeros_like(acc_ref)

        n, r = _norm_rows(x_ref[...] + gt_ref[...] * f_ref[...].astype(F32))
        w = w_ref[...]
        e = n * w - t_ref[...]
        acc_ref[0:1, :] += jnp.sum(e * e, axis=0, keepdims=True) * (0.5 / D)
        dout = e * (1.0 / D)
        acc_ref[1:2, :] += jnp.sum(dout * n, axis=0, keepdims=True)
        dn = dout * w
        dx = r * (dn - n * jnp.mean(dn * n, axis=-1, keepdims=True))
        dx_ref[...] = dx
        acc_ref[2:3, :] += jnp.sum(dx * f_ref[...].astype(F32), axis=0, keepdims=True)
        df_ref[...] = (dx * gt_ref[...]).astype(df_ref.dtype)

        @pl.when(i == pl.num_programs(0) - 1)
        def _():
            acc_ref[3:4, :] = jnp.zeros((1, D), F32) + jnp.sum(acc_ref[0:1, :])

    row = pl.BlockSpec((tr, D), lambda i: (i, 0))
    vec = pl.BlockSpec((1, D), lambda i: (0, 0))
    return pl.pallas_call(
        body, name="loss_head", grid=(L // tr,),
        in_specs=[row, row, vec, row, vec],
        out_specs=[row, row, pl.BlockSpec((8, D), lambda i: (0, 0))],
        out_shape=[jax.ShapeDtypeStruct((L, D), F32), jax.ShapeDtypeStruct((L, D), BF16),
                   jax.ShapeDtypeStruct((8, D), F32)],
        compiler_params=_cparams(("arbitrary",)),
    )(x1, tgt, nf, fbr, gate)


RET_SUB = 4
N_TAB = 7


def _ret_tables(rdb, Lc):
    def body(rd_ref, t_ref, c_ref):
        d = pl.program_id(0) // RET_HEADS
        fwd = d == 0
        lg = -jnp.exp(rd_ref[0])
        i = lax.broadcasted_iota(jnp.int32, (CHUNK, CHUNK), 0).astype(F32)
        j = lax.broadcasted_iota(jnp.int32, (CHUNK, CHUNK), 1).astype(F32)
        rel = jnp.where(fwd, i - j, j - i)
        mask = (rel > 0.0) | ((rel == 0.0) & fwd)
        dm = jnp.where(mask, jnp.exp(lg * jnp.maximum(rel, 0.0)), 0.0)
        t_ref[0, 0] = dm
        t_ref[0, 1] = rel * dm
        qc = jnp.where(fwd, i + 1.0, CHUNK - i)
        qw = jnp.exp(lg * qc)
        t_ref[0, 2] = qw
        t_ref[0, 3] = qw * qc
        kc = jnp.where(fwd, CHUNK - 1.0 - i, i)
        kw = jnp.exp(lg * kc)
        t_ref[0, 4] = kw
        t_ref[0, 5] = kw * kc
        t_ref[0, 6] = jnp.exp(lg * float(CHUNK)) + jnp.zeros((CHUNK, CHUNK), F32)
        m = lax.broadcasted_iota(jnp.int32, (Lc, LANES), 0).astype(F32)
        cc = jnp.where(fwd, Lc - 1.0 - m, m)
        cw = jnp.exp(lg * cc)
        c_ref[0, 0] = cw
        c_ref[0, 1] = cw * cc

    return pl.pallas_call(
        body, name="ret_tables", grid=(2 * RET_HEADS,),
        in_specs=[pl.BlockSpec((1, 1, LANES), lambda r: (r, 0, 0))],
        out_specs=[pl.BlockSpec((1, N_TAB, CHUNK, CHUNK), lambda r: (r, 0, 0, 0)),
                   pl.BlockSpec((1, 2, Lc, LANES), lambda r: (r, 0, 0, 0))],
        out_shape=[jax.ShapeDtypeStruct((2 * RET_HEADS, N_TAB, CHUNK, CHUNK), F32),
                   jax.ShapeDtypeStruct((2 * RET_HEADS, 2, Lc, LANES), F32)],
        compiler_params=_cparams(("parallel",)),
    )(rdb)


def _ret_ctx_state(P, ctab, L, Lc):
    cb = L // Lc

    def body(k_ref, v_ref, c_ref, s_ref):
        for p in range(RET_HEADS // 2):
            kp = k_ref[:, p * LANES:(p + 1) * LANES].astype(F32) * K_SCALE
            for a in range(2):
                h = 2 * p + a
                kh = jnp.where(_half_mask(kp.shape, a), kp, 0.0)
                vh = v_ref[:, h * RET_DV:(h + 1) * RET_DV]
                for d in range(2):
                    kw = (kh * c_ref[d * RET_HEADS + h, 0]).astype(BF16)
                    s_ref[d * RET_HEADS + h] = _dot(kw, vh, TN)

    return pl.pallas_call(
        body, name="ret_ctx_state", grid=(1,),
        in_specs=[pl.BlockSpec((Lc, 512), lambda i: (cb, C_RK // 512)),
                  pl.BlockSpec((Lc, 1024), lambda i: (cb, C_RV // 1024)),
                  pl.BlockSpec((2 * RET_HEADS, 2, Lc, LANES), lambda i: (0, 0, 0, 0))],
        out_specs=pl.BlockSpec((2 * RET_HEADS, LANES, RET_DV), lambda i: (0, 0, 0)),
        out_shape=jax.ShapeDtypeStruct((2 * RET_HEADS, LANES, RET_DV), F32),
        compiler_params=_cparams(("arbitrary",)),
    )(P, P, ctab)


def _ret_fwd(P, rope, tabs, s0, L, comm=()):
    n = L // CHUNK
    nb = n // RET_SUB

    def body(qf, kf, vf, rf, qb, kb, vb, rb, t_ref, s0_ref, of_ref, ob_ref, stf_ref, stb_ref, st):
        s = pl.program_id(0)

        @pl.when(s == 0)
        def _():
            st[...] = s0_ref[...]

        for rnd in range(RET_SUB):
            units = []
            for d, (q_ref, k_ref, v_ref, r_ref, o_ref, so_ref) in enumerate(
                    ((qf, kf, vf, rf, of_ref, stf_ref), (qb, kb, vb, rb, ob_ref, stb_ref))):
                j = rnd if d == 0 else RET_SUB - 1 - rnd
                rows = slice(j * CHUNK, (j + 1) * CHUNK)
                cos, sin = r_ref[0, rows, :], r_ref[1, rows, :]
                for p in range(RET_HEADS // 2):
                    qp = _rope(q_ref[rows, p * LANES:(p + 1) * LANES].astype(F32), cos, sin, 32)
                    kp = _rope(k_ref[rows, p * LANES:(p + 1) * LANES].astype(F32), cos, sin, 32) * K_SCALE
                    for a in range(2):
                        h = 2 * p + a
                        hm = _half_mask(qp.shape, a)
                        units.append(dict(r=d * RET_HEADS + h, h=h, a=a, j=j, rows=rows, o_ref=o_ref, so_ref=so_ref,
                                          v_ref=v_ref, qh=jnp.where(hm, qp, 0.0), kh=jnp.where(hm, kp, 0.0)))
            for u in units:
                u["sc"] = _dot(u["qh"].astype(BF16), u["kh"].astype(BF16), NT)
            for u in units:
                r, h = u["r"], u["h"]
                sp = st[r]
                u["so_ref"][u["j"], h] = sp[u["a"] * RET_DK:(u["a"] + 1) * RET_DK, :]
                vh = u["v_ref"][u["rows"], h * RET_DV:(h + 1) * RET_DV]
                o = _dot((u["sc"] * t_ref[r, 0]).astype(BF16), vh, NN)
                o += _dot((u["qh"] * t_ref[r, 2]).astype(BF16), sp.astype(BF16), NN)
                u["o_ref"][u["rows"], h * RET_DV:(h + 1) * RET_DV] = o
            for u in units:
                r, h = u["r"], u["h"]
                vh = u["v_ref"][u["rows"], h * RET_DV:(h + 1) * RET_DV]
                st[r] = t_ref[r, 6] * st[r] + _dot((u["kh"] * t_ref[r, 4]).astype(BF16), vh, TN)

    fw = lambda s: s
    bw = lambda s: nb - 1 - s
    RB = RET_SUB * CHUNK

    def specs(cm):
        return [pl.BlockSpec((RB, 512), lambda s: (cm(s), C_RQ // 512)),
                pl.BlockSpec((RB, 512), lambda s: (cm(s), C_RK // 512)),
                pl.BlockSpec((RB, 1024), lambda s: (cm(s), C_RV // 1024)),
                pl.BlockSpec((2, RB, LANES), lambda s: (0, cm(s), 0))]

    full = lambda shp: pl.BlockSpec(shp, lambda s: (0,) * len(shp))
    return _call(
        body, "ret_fwd", (nb,),
        specs(fw) + specs(bw) + [full((2 * RET_HEADS, N_TAB, CHUNK, CHUNK)), full((2 * RET_HEADS, LANES, RET_DV))],
        [pl.BlockSpec((RB, 1024), lambda s: (fw(s), 0)),
         pl.BlockSpec((RB, 1024), lambda s: (bw(s), 0)),
         pl.BlockSpec((RET_SUB, RET_HEADS, RET_DK, RET_DV), lambda s: (fw(s), 0, 0, 0)),
         pl.BlockSpec((RET_SUB, RET_HEADS, RET_DK, RET_DV), lambda s: (bw(s), 0, 0, 0))],
        [jax.ShapeDtypeStruct((L, 1024), F32), jax.ShapeDtypeStruct((L, 1024), F32),
         jax.ShapeDtypeStruct((n, RET_HEADS, RET_DK, RET_DV), F32),
         jax.ShapeDtypeStruct((n, RET_HEADS, RET_DK, RET_DV), F32)],
        [pltpu.VMEM((2 * RET_HEADS, LANES, RET_DV), F32)],
        ("arbitrary",), (P, P, P, rope, P, P, P, rope, tabs, s0), comm)


def _ret_finish_fwd(of, ob, P, L):
    tr = _pick(L, (512, 256, 128))

    def body(f_ref, b_ref, g_ref, y_ref):
        for h in range(RET_HEADS):
            sl = slice(h * RET_DV, (h + 1) * RET_DV)
            n, _ = _norm_rows(f_ref[:, sl] + b_ref[:, sl])
            g = g_ref[:, sl].astype(F32)
            y_ref[:, sl] = (n * (g * _sigmoid(g))).astype(y_ref.dtype)

    row = pl.BlockSpec((tr, 1024), lambda i: (i, 0))
    return pl.pallas_call(
        body, name="ret_finish_fwd", grid=(L // tr,),
        in_specs=[row, row, pl.BlockSpec((tr, 1024), lambda i: (i, C_RG // 1024))],
        out_specs=row, out_shape=jax.ShapeDtypeStruct((L, 2048), BF16),
        compiler_params=_cparams(("parallel",)),
    )(of, ob, P)


def _ret_finish_bwd(of, ob, P, dY, L):
    tr = _pick(L, (512, 256, 128))

    def body(f_ref, b_ref, g_ref, dy_ref, do_ref, dg_ref):
        for h in range(RET_HEADS):
            sl = slice(h * RET_DV, (h + 1) * RET_DV)
            n, r = _norm_rows(f_ref[:, sl] + b_ref[:, sl])
            g = g_ref[:, sl].astype(F32)
            sg = _sigmoid(g)
            dy = dy_ref[:, sl].astype(F32)
            dg_ref[:, sl] = (dy * n * (sg * (1.0 + g * (1.0 - sg)))).astype(dg_ref.dtype)
            dn = dy * (g * sg)
            do_ref[:, sl] = (r * (dn - n * jnp.mean(dn * n, axis=-1, keepdims=True))).astype(do_ref.dtype)

    row = pl.BlockSpec((tr, 1024), lambda i: (i, 0))
    return pl.pallas_call(
        body, name="ret_finish_bwd", grid=(L // tr,),
        in_specs=[row, row, pl.BlockSpec((tr, 1024), lambda i: (i, C_RG // 1024)), row],
        out_specs=[row, row],
        out_shape=[jax.ShapeDtypeStruct((L, 1024), BF16), jax.ShapeDtypeStruct((L, 1024), BF16)],
        compiler_params=_cparams(("parallel",)),
    )(of, ob, P, dY)


def _ret_bwd(P, rope, tabs, stf, stb, dO, L, comm=()):
    n = L // CHUNK
    nb = n // RET_SUB

    def body(qf, kf, vf, rf, gf, sf, qb, kb, vb, rb, gb, sb, t_ref,
             dqf, dkf, dvf, dqb, dkb, dvb, ds0_ref, dlg_ref, ds):
        s = pl.program_id(0)

        @pl.when(s == 0)
        def _():
            ds[...] = jnp.zeros_like(ds)
            dlg_ref[...] = jnp.zeros_like(dlg_ref)

        for rnd in range(RET_SUB):
            units, pairs = [], []
            for d, (q_ref, k_ref, v_ref, r_ref, g_ref, s_ref, dq_ref, dk_ref, dv_ref) in enumerate(
                    ((qf, kf, vf, rf, gf, sf, dqf, dkf, dvf), (qb, kb, vb, rb, gb, sb, dqb, dkb, dvb))):
                j = RET_SUB - 1 - rnd if d == 0 else rnd
                rows = slice(j * CHUNK, (j + 1) * CHUNK)
                cos, sin = r_ref[0, rows, :], r_ref[1, rows, :]
                for p in range(RET_HEADS // 2):
                    qp = _rope(q_ref[rows, p * LANES:(p + 1) * LANES].astype(F32), cos, sin, 32)
                    kp = _rope(k_ref[rows, p * LANES:(p + 1) * LANES].astype(F32), cos, sin, 32) * K_SCALE
                    pair = dict(p=p, rows=rows, cos=cos, sin=sin, dq_ref=dq_ref, dk_ref=dk_ref, us=[])
                    pairs.append(pair)
                    for a in range(2):
                        h = 2 * p + a
                        r = d * RET_HEADS + h
                        hm = _half_mask(qp.shape, a)
                        zero = jnp.zeros((RET_DK, RET_DV), F32)
                        sp = s_ref[j, h]
                        u = dict(r=r, h=h, rows=rows, dv_ref=dv_ref,
                                 qh=jnp.where(hm, qp, 0.0), kh=jnp.where(hm, kp, 0.0),
                                 vh=v_ref[rows, h * RET_DV:(h + 1) * RET_DV],
                                 gh=g_ref[rows, h * RET_DV:(h + 1) * RET_DV],
                                 sp=jnp.concatenate([sp, zero] if a == 0 else [zero, sp], axis=0),
                                 dsn=ds[r])
                        u["qhb"], u["khb"] = u["qh"].astype(BF16), u["kh"].astype(BF16)
                        units.append(u)
                        pair["us"].append(u)
            for u in units:
                u["am"] = _dot(u["qhb"], u["khb"], NT)
                u["dar"] = _dot(u["gh"], u["vh"], NT)
                u["xq"] = _dot(u["gh"], u["sp"].astype(BF16), NT)
                u["yk"] = _dot(u["vh"], u["dsn"].astype(BF16), NT)
            for u in units:
                r = u["r"]
                dm = t_ref[r, 0]
                u["da"] = (u["dar"] * dm).astype(BF16)
                u["amd"] = (u["am"] * dm).astype(BF16)
                part = (jnp.sum(u["am"] * u["dar"] * t_ref[r, 1]) + jnp.sum(u["qh"] * t_ref[r, 3] * u["xq"])
                        + jnp.sum(u["kh"] * t_ref[r, 5] * u["yk"])
                        + float(CHUNK) * jnp.sum(t_ref[r, 6] * u["dsn"] * u["sp"]))
                dlg_ref[r:r + 1, :] += jnp.zeros((1, LANES), F32) + part
            for u in units:
                r, h = u["r"], u["h"]
                u["dq"] = _dot(u["da"], u["khb"], NN) + u["xq"] * t_ref[r, 2]
                u["dk"] = _dot(u["da"], u["qhb"], TN) + u["yk"] * t_ref[r, 4]
                u["dv_ref"][u["rows"], h * RET_DV:(h + 1) * RET_DV] = (
                    _dot(u["amd"], u["gh"], TN)
                    + _dot((u["kh"] * t_ref[r, 4]).astype(BF16), u["dsn"].astype(BF16), NN)
                ).astype(u["dv_ref"].dtype)
                ds[r] = t_ref[r, 6] * u["dsn"] + _dot((u["qh"] * t_ref[r, 2]).astype(BF16), u["gh"], TN)
            for pair in pairs:
                sl = slice(pair["p"] * LANES, (pair["p"] + 1) * LANES)
                u0, u1 = pair["us"]
                pair["dq_ref"][pair["rows"], sl] = _rope_t(
                    u0["dq"] + u1["dq"], pair["cos"], pair["sin"], 32).astype(BF16)
                pair["dk_ref"][pair["rows"], sl] = _rope_t(
                    (u0["dk"] + u1["dk"]) * K_SCALE, pair["cos"], pair["sin"], 32).astype(BF16)

        @pl.when(s == nb - 1)
        def _():
            ds0_ref[...] = ds[...]

    fw = lambda s: nb - 1 - s
    bw = lambda s: s
    RB = RET_SUB * CHUNK

    def specs(cm):
        return [pl.BlockSpec((RB, 512), lambda s: (cm(s), C_RQ // 512)),
                pl.BlockSpec((RB, 512), lambda s: (cm(s), C_RK // 512)),
                pl.BlockSpec((RB, 1024), lambda s: (cm(s), C_RV // 1024)),
                pl.BlockSpec((2, RB, LANES), lambda s: (0, cm(s), 0)),
                pl.BlockSpec((RB, 1024), lambda s: (cm(s), 0)),
                pl.BlockSpec((RET_SUB, RET_HEADS, RET_DK, RET_DV), lambda s: (cm(s), 0, 0, 0))]

    def ospecs(cm):
        return [pl.BlockSpec((RB, 512), lambda s: (cm(s), 0)), pl.BlockSpec((RB, 512), lambda s: (cm(s), 0)),
                pl.BlockSpec((RB, 1024), lambda s: (cm(s), 0))]

    oshape = [jax.ShapeDtypeStruct((L, 512), BF16), jax.ShapeDtypeStruct((L, 512), BF16),
              jax.ShapeDtypeStruct((L, 1024), BF16)]
    full = lambda shp: pl.BlockSpec(shp, lambda s: (0,) * len(shp))
    return _call(
        body, "ret_bwd", (nb,),
        specs(fw) + specs(bw) + [full((2 * RET_HEADS, N_TAB, CHUNK, CHUNK))],
        ospecs(fw) + ospecs(bw) + [full((2 * RET_HEADS, LANES, RET_DV)), full((2 * RET_HEADS, LANES))],
        oshape + oshape + [jax.ShapeDtypeStruct((2 * RET_HEADS, LANES, RET_DV), F32),
                           jax.ShapeDtypeStruct((2 * RET_HEADS, LANES), F32)],
        [pltpu.VMEM((2 * RET_HEADS, LANES, RET_DV), F32)],
        ("arbitrary",), (P, P, P, rope, dO, stf, P, P, P, rope, dO, stb, tabs), comm)


def _ret_ctx_bwd(P, ctab, ds0, dlg, rdb, L, Lc):
    cb = L // Lc

    def body(k_ref, v_ref, c_ref, ds_ref, dlg_ref, rd_ref, dk_ref, dv_ref, drd_ref):
        for p in range(RET_HEADS // 2):
            kp = k_ref[:, p * LANES:(p + 1) * LANES].astype(F32) * K_SCALE
            dkp = jnp.zeros((Lc, LANES), F32)
            for a in range(2):
                h = 2 * p + a
                kh = jnp.where(_half_mask(kp.shape, a), kp, 0.0)
                vh = v_ref[:, h * RET_DV:(h + 1) * RET_DV]
                dvh = jnp.zeros((Lc, RET_DV), F32)
                for d in range(2):
                    r = d * RET_HEADS + h
                    dsb = ds_ref[r].astype(BF16)
                    cw, cwc = c_ref[r, 0], c_ref[r, 1]
                    y = _dot(vh, dsb, NT)
                    dkp += y * cw
                    dvh += _dot((kh * cw).astype(BF16), dsb, NN)
                    lg = -jnp.exp(rd_ref[r])
                    drd_ref[r:r + 1, :] = (dlg_ref[r:r + 1, :] + jnp.sum(kh * cwc * y)) * lg
                dv_ref[:, h * RET_DV:(h + 1) * RET_DV] = dvh
            dk_ref[:, p * LANES:(p + 1) * LANES] = dkp * K_SCALE

    full = lambda shp: pl.BlockSpec(shp, lambda i: (0,) * len(shp))
    return pl.pallas_call(
        body, name="ret_ctx_bwd", grid=(1,),
        in_specs=[pl.BlockSpec((Lc, 512), lambda i: (cb, C_RK // 512)),
                  pl.BlockSpec((Lc, 1024), lambda i: (cb, C_RV // 1024)),
                  full((2 * RET_HEADS, 2, Lc, LANES)), full((2 * RET_HEADS, LANES, RET_DV)),
                  full((2 * RET_HEADS, LANES)), full((2 * RET_HEADS, 1, LANES))],
        out_specs=[full((Lc, 512)), full((Lc, 1024)), full((2 * RET_HEADS, LANES))],
        out_shape=[jax.ShapeDtypeStruct((Lc, 512), F32), jax.ShapeDtypeStruct((Lc, 1024), F32),
                   jax.ShapeDtypeStruct((2 * RET_HEADS, LANES), F32)],
        compiler_params=_cparams(("arbitrary",)),
    )(P, P, ctab, ds0, dlg, rdb)


BLK = 128
N_LOC = 3 * BLK


ATT_SUB = 4


def _att_inputs(P, rope, L, Lc):
    n = L // BLK
    cb = L // Lc
    prev = lambda i: jnp.maximum(ATT_SUB * i - 1, 0)
    nxt = lambda i: jnp.minimum(ATT_SUB * i + ATT_SUB, n - 1)
    specs = [pl.BlockSpec((ATT_SUB * BLK, 1024), lambda i: (i, C_AQ // 1024))]
    args = [P]
    for col in (C_AK // 256, C_AV // 256):
        specs += [pl.BlockSpec((BLK, 256), functools.partial(lambda i, col: (prev(i), col), col=col)),
                  pl.BlockSpec((ATT_SUB * BLK, 256), functools.partial(lambda i, col: (i, col), col=col)),
                  pl.BlockSpec((BLK, 256), functools.partial(lambda i, col: (nxt(i), col), col=col)),
                  pl.BlockSpec((Lc, 256), functools.partial(lambda i, col: (cb, col), col=col))]
        args += [P] * 4
    specs += [pl.BlockSpec((2, BLK, LANES), lambda i: (0, prev(i), 0)),
              pl.BlockSpec((2, ATT_SUB * BLK, LANES), lambda i: (0, i, 0)),
              pl.BlockSpec((2, BLK, LANES), lambda i: (0, nxt(i), 0))]
    args += [rope] * 3
    return specs, args


def _att_prep(i, n, refs, Lc):
    q_ref, kp_ref, kc_ref, kn_ref, kx_ref, vp_ref, vc_ref, vn_ref, vx_ref, rp_ref, rc_ref, rn_ref = refs
    cos = jnp.concatenate([rp_ref[0], rc_ref[0], rn_ref[0]], axis=0)
    sin = jnp.concatenate([rp_ref[1], rc_ref[1], rn_ref[1]], axis=0)

    def dup(x):
        xr = pltpu.roll(x, 64, 1)
        return [jnp.where(_half_mask(x.shape, b), x, xr).astype(BF16) for b in range(2)]

    kd = [[] for _ in range(ATT_SUB)]
    vd = [[] for _ in range(ATT_SUB)]
    for t in range(ATT_KV // 2):
        sl = slice(t * LANES, (t + 1) * LANES)
        kl = jnp.concatenate([kp_ref[:, sl], kc_ref[:, sl], kn_ref[:, sl]], axis=0).astype(F32)
        kl = dup(_rope(kl, cos, sin, 16))
        vl = dup(jnp.concatenate([vp_ref[:, sl], vc_ref[:, sl], vn_ref[:, sl]], axis=0).astype(F32))
        kx, vx = dup(kx_ref[:, sl].astype(F32)), dup(vx_ref[:, sl].astype(F32))
        for j in range(ATT_SUB):
            rows = slice(j * BLK, j * BLK + N_LOC)
            for b in range(2):
                kd[j].append(jnp.concatenate([kl[b][rows], kx[b]], axis=0))
                vd[j].append(jnp.concatenate([vl[b][rows], vx[b]], axis=0))
    nk = N_LOC + Lc
    rr = lax.broadcasted_iota(jnp.int32, (BLK, nk), 0)
    ss = lax.broadcasted_iota(jnp.int32, (BLK, nk), 1)
    band = (ss >= rr) & (ss <= rr + 2 * BLK)
    bias4, tabs = [], []
    for j in range(ATT_SUB):
        blk = ATT_SUB * i + j
        lo = jnp.where(blk == 0, BLK, 0)
        hi = jnp.where(blk == n - 1, 2 * BLK, N_LOC)
        bias = jnp.where((ss >= N_LOC) | (band & (ss >= lo) & (ss < hi)), 0.0, NEG)
        bias4.append(jnp.concatenate([bias] * 4, axis=0))
        tabs.append((rc_ref[0, j * BLK:(j + 1) * BLK, :], rc_ref[1, j * BLK:(j + 1) * BLK, :]))
    return kd, vd, bias4, tabs


LOG2E = 1.4426950408889634
LN2 = 0.6931471805599453
Q_SCALE = A_SCALE * LOG2E


def _stack4(ref, rows, g, f=None):
    parts = []
    for jp in range(2):
        t = ref[rows, (2 * g + jp) * LANES:(2 * g + jp + 1) * LANES].astype(F32)
        if f is not None:
            t = f(t)
        for a in range(2):
            parts.append(jnp.where(_half_mask(t.shape, a), t, 0.0))
    return jnp.concatenate(parts, axis=0)


def _unstack4(x4, jp):
    r0 = 2 * jp * BLK
    lo = x4[r0:r0 + BLK]
    hi = x4[r0 + BLK:r0 + 2 * BLK]
    return jnp.where(_half_mask(lo.shape, 0), lo, hi)


def _softmax_parts(s, bias4, sink_ref, g):
    sink_col = LOG2E * jnp.concatenate(
        [jnp.zeros((BLK, 1), F32) + sink_ref[4 * g + r:4 * g + r + 1, 0:1] for r in range(4)], axis=0)
    s = s + bias4
    m = jnp.maximum(jnp.max(s, axis=-1, keepdims=True), sink_col)
    e = jnp.exp2(s - m)
    es = jnp.exp2(sink_col - m)
    return e, es, jnp.sum(e, axis=-1, keepdims=True) + es


def _att_fwd(P, rope, sinkb, Y, L, Lc, comm=()):
    n = L // BLK
    specs, args = _att_inputs(P, rope, L, Lc)

    def body(*refs):
        sink_ref, o_ref = refs[12], refs[14]
        i = pl.program_id(0)
        kd, vd, bias4, tabs = _att_prep(i, n, refs[:12], Lc)
        for j in range(ATT_SUB):
            rows = slice(j * BLK, (j + 1) * BLK)
            cq, sq = tabs[j]
            for g in range(ATT_KV):
                q4 = _stack4(refs[0], rows, g, lambda t: _rope(t, cq, sq, 16) * Q_SCALE).astype(BF16)
                e, _, l = _softmax_parts(_dot(q4, kd[j][g], NT), bias4[j], sink_ref, g)
                o4 = _dot(e.astype(BF16), vd[j][g], NN) * (1.0 / l)
                for jp in range(2):
                    c0 = (2 * g + jp) * LANES
                    o_ref[rows, c0:c0 + LANES] = _unstack4(o4, jp).astype(o_ref.dtype)

    return _call(
        body, "att_fwd", (n // ATT_SUB,),
        specs + [pl.BlockSpec((ATT_HEADS, LANES), lambda i: (0, 0)), pl.BlockSpec(memory_space=pl.ANY)],
        [pl.BlockSpec((ATT_SUB * BLK, 1024), lambda i: (i, 1))], [jax.ShapeDtypeStruct((L, 2048), BF16)], [],
        ("parallel",), (*args, sinkb, Y), comm, aliases={13: 0})


def _att_bwd(P, rope, sinkb, Y, dY, L, Lc, comm=()):
    n = L // BLK
    specs, args = _att_inputs(P, rope, L, Lc)
    nk = N_LOC + Lc

    def body(*refs):
        sink_ref, y_ref, dy_ref = refs[12], refs[13], refs[14]
        dq_ref, dkl_ref, dvl_ref, dkx_ref, dvx_ref, dsk_ref = refs[15:21]
        i = pl.program_id(0)

        @pl.when(i == 0)
        def _():
            dkx_ref[...] = jnp.zeros_like(dkx_ref)
            dvx_ref[...] = jnp.zeros_like(dvx_ref)
            dsk_ref[...] = jnp.zeros_like(dsk_ref)

        kd, vd, bias4, tabs = _att_prep(i, n, refs[:12], Lc)
        for j in range(ATT_SUB):
            rows = slice(j * BLK, (j + 1) * BLK)
            cq, sq = tabs[j]
            for t in range(ATT_KV // 2):
                dk_halves, dv_halves = [], []
                for b in range(2):
                    g = 2 * t + b
                    q4 = _stack4(refs[0], rows, g, lambda x: _rope(x, cq, sq, 16) * Q_SCALE).astype(BF16)
                    do4 = _stack4(dy_ref, rows, g)
                    delta = jnp.sum(do4 * _stack4(y_ref, rows, g), axis=-1, keepdims=True)
                    do4b = do4.astype(BF16)
                    e, es, l = _softmax_parts(_dot(q4, kd[j][g], NT), bias4[j], sink_ref, g)
                    inv = 1.0 / l
                    p = e * inv
                    dsc = (p * (_dot(do4b, vd[j][g], NT) - delta)).astype(BF16)
                    dsr = es * inv * delta
                    for r in range(4):
                        h = 4 * g + r
                        dsk_ref[h:h + 1, :] += jnp.zeros((1, LANES), F32) - jnp.sum(dsr[r * BLK:(r + 1) * BLK])
                    dq4 = _dot(dsc, kd[j][g], NN) * A_SCALE
                    for jp in range(2):
                        c0 = (2 * g + jp) * LANES
                        dq_ref[rows, c0:c0 + LANES] = _rope_t(_unstack4(dq4, jp), cq, sq, 16).astype(dq_ref.dtype)
                    dkd = _dot(q4, dsc, TN) * LN2
                    dvd = _dot(do4b, p.astype(BF16), TN)
                    dk_halves.append(dkd[:ATT_DH] + dkd[ATT_DH:])
                    dv_halves.append(dvd[:ATT_DH] + dvd[ATT_DH:])
                dk_t = jnp.concatenate(dk_halves, axis=0).T
                dv_t = jnp.concatenate(dv_halves, axis=0).T
                sl = slice(t * LANES, (t + 1) * LANES)
                dkl_ref[j, :, sl] = dk_t[:N_LOC]
                dvl_ref[j, :, sl] = dv_t[:N_LOC]
                dkx_ref[:, sl] += dk_t[N_LOC:]
                dvx_ref[:, sl] += dv_t[N_LOC:]

    row = pl.BlockSpec((ATT_SUB * BLK, 1024), lambda i: (i, 0))
    loc = pl.BlockSpec((ATT_SUB, N_LOC, 256), lambda i: (i, 0, 0))
    cx = pl.BlockSpec((Lc, 256), lambda i: (0, 0))
    return _call(
        body, "att_bwd", (n // ATT_SUB,),
        specs + [pl.BlockSpec((ATT_HEADS, LANES), lambda i: (0, 0))]
        + [pl.BlockSpec((ATT_SUB * BLK, 1024), lambda i: (i, 1))] * 2,
        [row, loc, loc, cx, cx, pl.BlockSpec((ATT_HEADS, LANES), lambda i: (0, 0))],
        [jax.ShapeDtypeStruct((L, 1024), BF16), jax.ShapeDtypeStruct((n, N_LOC, 256), F32),
         jax.ShapeDtypeStruct((n, N_LOC, 256), F32), jax.ShapeDtypeStruct((Lc, 256), F32),
         jax.ShapeDtypeStruct((Lc, 256), F32), jax.ShapeDtypeStruct((ATT_HEADS, LANES), F32)], [],
        ("arbitrary",), (*args, sinkb, Y, dY), comm)


def _assemble_dp(L, Lc, dqf, dqb, dkf, dkb, dvf, dvb, drg, daq, dkl, dvl, rope_att, dck, dcv, dkx, dvx):
    n = L // BLK
    nc = Lc // BLK

    def body(dqf_r, dqb_r, dkf_r, dkb_r, dvf_r, dvb_r, drg_r, daq_r, kl0, kl1, kl2, vl0, vl1, vl2, rp_r,
             dck_r, dcv_r, dkx_r, dvx_r, o_ref):
        i = pl.program_id(0)

        @pl.when(i < n)
        def _():
            add = lambda a, b: (a[...].astype(F32) + b[...].astype(F32)).astype(o_ref.dtype)
            o_ref[:, C_RQ:C_RK] = add(dqf_r, dqb_r)
            o_ref[:, C_RK:C_RV] = add(dkf_r, dkb_r)
            o_ref[:, C_RV:C_RG] = add(dvf_r, dvb_r)
            o_ref[:, C_RG:C_AQ] = drg_r[...].astype(o_ref.dtype)
            o_ref[:, C_AQ:C_AK] = daq_r[...].astype(o_ref.dtype)
            w0 = jnp.where(i > 0, 1.0, 0.0)
            w2 = jnp.where(i < n - 1, 1.0, 0.0)
            dk = kl0[0] * w0 + kl1[0] + kl2[0] * w2
            dv = vl0[0] * w0 + vl1[0] + vl2[0] * w2
            for t in range(ATT_KV // 2):
                sl = slice(t * LANES, (t + 1) * LANES)
                o_ref[:, C_AK + t * LANES:C_AK + (t + 1) * LANES] = _rope_t(
                    dk[:, sl], rp_r[0], rp_r[1], 16).astype(o_ref.dtype)
            o_ref[:, C_AV:D_PROJ] = dv.astype(o_ref.dtype)

        @pl.when(i >= n)
        def _():
            o_ref[:, C_RQ:C_RK] = jnp.zeros((BLK, C_RK - C_RQ), o_ref.dtype)
            o_ref[:, C_RK:C_RV] = dck_r[...].astype(o_ref.dtype)
            o_ref[:, C_RV:C_RG] = dcv_r[...].astype(o_ref.dtype)
            o_ref[:, C_RG:C_AK] = jnp.zeros((BLK, C_AK - C_RG), o_ref.dtype)
            o_ref[:, C_AK:C_AV] = dkx_r[...].astype(o_ref.dtype)
            o_ref[:, C_AV:D_PROJ] = dvx_r[...].astype(o_ref.dtype)

    xm = lambda i: jnp.minimum(i, n - 1)
    cm = lambda i: jnp.clip(i - n, 0, nc - 1)
    r512 = pl.BlockSpec((BLK, 512), lambda i: (xm(i), 0))
    r1024 = pl.BlockSpec((BLK, 1024), lambda i: (xm(i), 0))
    part = lambda off: pl.BlockSpec((1, BLK, 256), lambda i: (jnp.clip(xm(i) + off, 0, n - 1), 1 - off, 0))
    return pl.pallas_call(
        body, name="assemble_dp", grid=(n + nc,),
        in_specs=[r512, r512, r512, r512, r1024, r1024, r1024, r1024,
                  part(-1), part(0), part(1), part(-1), part(0), part(1),
                  pl.BlockSpec((2, BLK, LANES), lambda i: (0, xm(i), 0)),
                  pl.BlockSpec((BLK, 512), lambda i: (cm(i), 0)), pl.BlockSpec((BLK, 1024), lambda i: (cm(i), 0)),
                  pl.BlockSpec((BLK, 256), lambda i: (cm(i), 0)), pl.BlockSpec((BLK, 256), lambda i: (cm(i), 0))],
        out_specs=pl.BlockSpec((BLK, D_PROJ), lambda i: (i, 0)),
        out_shape=jax.ShapeDtypeStruct((L + Lc, D_PROJ), BF16),
        compiler_params=_cparams(("parallel",)),
    )(dqf, dqb, dkf, dkb, dvf, dvb, drg, daq, dkl, dkl, dkl, dvl, dvl, dvl, rope_att, dck, dcv, dkx, dvx)


def _adam_math(w, g, m, v):
    m = ADAM_B1 * m + (1.0 - ADAM_B1) * g
    v = ADAM_B2 * v + (1.0 - ADAM_B2) * (g * g)
    m_hat = m / (1.0 - ADAM_B1 ** ADAM_STEP)
    v_hat = v / (1.0 - ADAM_B2 ** ADAM_STEP)
    delta = -ADAM_LR * (m_hat / (jnp.sqrt(v_hat) + ADAM_EPS) + ADAM_WD * w)
    return delta, m, v


def _adam(name, w, m, v, g=None, parts=None):
    R, C = w.shape
    tr = _pick(R, (256, 192, 176, 128, 64, 32, 16, 8))
    summed = parts is not None
    n_parts = parts.shape[0] if summed else 0

    def body(w_ref, m_ref, v_ref, g_ref, go_ref, d_ref, mo_ref, vo_ref):
        if summed:
            gv = g_ref[0].astype(F32)
            for j in range(1, n_parts):
                gv = gv + g_ref[j].astype(F32)
        else:
            gv = g_ref[...]
        d, mn, vn = _adam_math(w_ref[...], gv, m_ref[...], v_ref[...])
        go_ref[...] = gv
        d_ref[...] = d
        mo_ref[...] = mn
        vo_ref[...] = vn

    row = pl.BlockSpec((tr, C), lambda i: (i, 0))
    gspec = pl.BlockSpec((n_parts, tr, C), lambda i: (0, i, 0)) if summed else row
    return pl.pallas_call(
        body, name=name, grid=(R // tr,),
        in_specs=[row, row, row, gspec], out_specs=[row] * 4,
        out_shape=[jax.ShapeDtypeStruct((R, C), F32)] * 4,
        compiler_params=_cparams(("parallel",)),
    )(w, m, v, parts if summed else g)


def _rows_full(g):
    _, R, D = g.shape
    return g.reshape(N_DEV * R, D)


def _rows_slots(g):
    N, D = g.shape
    return g.reshape(N_DEV, N // N_DEV, D)


def _pad_rows(a, rows):
    return jnp.concatenate([a, jnp.zeros((rows - a.shape[0],) + a.shape[1:], a.dtype)], axis=0)


def kernel(x, c, ctx, c_ctx, w_mod, b_mod, norm_mix, norm_ffn, w_in, ret_decay, attn_sink, w_out, w_gate, w_up, w_down, norm_final, loss_target, m_c_ctx, m_w_mod, m_b_mod, m_norm_mix, m_norm_ffn, m_w_in, m_ret_decay, m_attn_sink, m_w_out, m_w_gate, m_w_up, m_w_down, m_norm_final, v_c_ctx, v_w_mod, v_b_mod, v_norm_mix, v_norm_ffn, v_w_in, v_ret_decay, v_attn_sink, v_w_out, v_w_gate, v_w_up, v_w_down, v_norm_final):
    L, D = x.shape[1], x.shape[2]
    Lc = ctx.shape[1]
    DF = w_gate.shape[2] * N_DEV
    C6 = w_mod.shape[2]
    me = _my_id()
    xs, cx, tgt = x[0], ctx[0], loss_target[0]

    ag_in = ("ag2", w_in[0].T.astype(BF16))
    ag_out, ag_gate = ("ag2", w_out[0].astype(BF16)), ("ag2", w_gate[0].T.astype(BF16))
    ag_up, ag_down = ("ag2", w_up[0].T.astype(BF16)), ("ag2", w_down[0].astype(BF16))

    cs = _allgather(c, "ag_c")[:, 0, :]
    s_in = _pad_rows(jnp.concatenate([cs, c_ctx[None, :]], axis=0), 16)
    b_l = lax.dynamic_slice_in_dim(b_mod, me * C6, C6, axis=1)
    mod_parts = _allgather(_mod_fwd(s_in, w_mod[0], b_l), "ag_mod")
    mod = _pad_rows(lax.dynamic_index_in_dim(mod_parts, me, axis=1, keepdims=False).reshape(6, D), 8)
    modc = _pad_rows(mod_parts[:, N_DEV, :].reshape(6, D), 8)
    mix_mod, ffn_mod = mod, jnp.roll(mod, -3, axis=0)
    gt_m, gt_f = mod[2:3], mod[5:6]

    rope_ret, rope_att = _rope_tables(L)
    rdb = jnp.broadcast_to(ret_decay[0].reshape(2 * RET_HEADS, 1, 1), (2 * RET_HEADS, 1, LANES))
    sinkb = jnp.broadcast_to(attn_sink[0].reshape(ATT_HEADS, 1), (ATT_HEADS, LANES))

    tm = _pick(L + Lc, (1408, 768, 512, 384, 256, 128))
    tmx = _pick(L, (1024, 512, 256, 128))

    (H,), (g_in,) = _modulate_fwd("mod_mix_fwd", xs, cx, norm_mix, mix_mod, modc, comm=[ag_in])
    W_inT = _rows_full(g_in)
    ident = lambda a, e: a
    tP, tD, tF = _pick(D_PROJ, (1152, 768, 512)), _pick(D, (2048, 1024, 512)), _pick(DF, (512, 256, 128))
    (P,), (g_gate,) = _matmul("mm_in", [(H, W_inT, 0)], 1, L + Lc, D_PROJ, D, "nt",
                              (tm, _pick(D_PROJ, (1536, 768, 512)), D), [], [BF16], ident,
                              comm=[ag_gate])
    W_gateT = _rows_full(g_gate)
    tabs, ctab = _ret_tables(rdb, Lc)
    s0 = _ret_ctx_state(P, ctab, L, Lc)
    (o_f, o_b, st_f, st_b), (g_out,) = _ret_fwd(P, rope_ret, tabs, s0, L, comm=[ag_out])
    W_out = _rows_full(g_out)
    Y_half = _ret_finish_fwd(o_f, o_b, P, L)
    (Y,), (g_up,) = _att_fwd(P, rope_att, sinkb, Y_half, L, Lc, comm=[ag_up])
    W_upT = _rows_full(g_up)
    KO = Y.shape[1]
    f_mix = _matmul("mm_out", [(Y, W_out, 0)], 1, L, D, KO, "nn", (tmx, tD, KO), [], [BF16], ident)[0]

    x1, H2 = _residual_modulate_fwd("mod_ffn_fwd", xs, f_mix, gt_m, norm_ffn, ffn_mod)

    def swiglu_epi(a, e):
        sg = _sigmoid(a[0])
        act = a[0] * sg
        return [act, a[1] * (sg * (1.0 + a[0] * (1.0 - sg))), act * a[1]]

    tm2 = tmx
    (act, up_dact, hmid), (g_down,) = _matmul("mm_gate_up", [(H2, W_gateT, 0), (H2, W_upT, 1)], 2, L, DF, D, "nt",
                                              (_pick(L, (2048, 1024, 512, 256, 128)), tF, D), [],
                                              [BF16, BF16, BF16], swiglu_epi, comm=[ag_down])
    W_down = _rows_full(g_down)
    f_ffn = _matmul("mm_down", [(hmid, W_down, 0)], 1, L, D, DF, "nn",
                    (tm2, _pick(D, (1024, 512)), _pick(DF, (2816, 512, 256, 128))), [], [BF16], ident)[0]

    dx2, dFf, sums_l = _loss_head(x1, tgt, norm_final.reshape(1, D), f_ffn, gt_f)

    def dswiglu_epi(a, e):
        return [a[0] * e[0].astype(F32), a[0] * e[1].astype(F32)]

    dga, dup = _matmul("mm_d_down", [(dFf, W_down, 0)], 1, L, DF, D, "nt", (_pick(L, (2048, 1024, 512, 256, 128)), tF, D),
                       [(up_dact, "mn"), (act, "mn")], [BF16, BF16], dswiglu_epi)
    tkt, tkl = _pick(L, (512, 256, 128)), _pick(L, (1024, 512, 256, 128))
    dW_down = _matmul("mm_gw_down", [(hmid, dFf, 0)], 1, DF, D, L, "tn",
                      (_pick(DF, (1408, 512, 256, 128)), tD, tkl), [], [BF16], ident)[0]
    (dW_gateT, dW_upT), (p_down,) = _matmul("mm_gw_gate_up", [(dga, H2, 0), (dup, H2, 1)], 2, DF, D, L, "tn",
                                            (tF, tD, _pick(L, (2048, 1024, 512, 256, 128))), [], [BF16, BF16], ident,
                                            comm=[("a2a", _rows_slots(dW_down))])
    (dH2,), (p_gate,) = _matmul("mm_d_gate_up", [(dga, W_gateT, 0), (dup, W_upT, 0)], 1, L, D, DF, "nn",
                                (_pick(L, (2048, 1024, 512, 256, 128)), tD, tF), [], [BF16], ident,
                                comm=[("a2a", _rows_slots(dW_gateT))])
    dx1, dFm, sums_f = _modulate_bwd("mod_ffn_bwd", x1, None, dH2, norm_ffn, ffn_mod, None, dx2, f_mix, gt_m)

    tO = _pick(KO, (2048, 1024, 512))
    dY = _matmul("mm_d_out", [(dFm, W_out, 0)], 1, L, KO, D, "nt", (tmx, tO, D), [], [BF16], ident)[0]
    dW_out = _matmul("mm_gw_out", [(Y, dFm, 0)], 1, KO, D, L, "tn",
                     (_pick(KO, (1024, 512)), tD, _pick(L, (2048, 1024, 512, 256, 128))), [], [BF16],
                     ident)[0]
    dO, drg = _ret_finish_bwd(o_f, o_b, P, dY, L)
    (dqf, dkf, dvf, dqb, dkb, dvb, ds0, dlg), (p_out,) = _ret_bwd(
        P, rope_ret, tabs, st_f, st_b, dO, L, comm=[("a2a", _rows_slots(dW_out))])
    dck, dcv, d_rd = _ret_ctx_bwd(P, ctab, ds0, dlg, rdb, L, Lc)
    (daq, dkl, dvl, dkx, dvx, d_sink), (p_up,) = _att_bwd(
        P, rope_att, sinkb, Y, dY, L, Lc, comm=[("a2a", _rows_slots(dW_upT))])
    dP = _assemble_dp(L, Lc, dqf, dqb, dkf, dkb, dvf, dvb, drg, daq, dkl, dvl, rope_att, dck, dcv, dkx, dvx)
    tkc = _pick(L + Lc, (768, 256, 128))
    dW_inT = _matmul("mm_gw_in", [(dP, H, 0)], 1, D_PROJ, D, L + Lc, "tn",
                     (_pick(D_PROJ, (2304, 1152, 768, 512)), tD, tkc), [], [BF16], ident)[0]
    (dH,), (p_in,) = _matmul("mm_d_in", [(dP, W_inT, 0)], 1, L + Lc, D, D_PROJ, "nn",
                             (tm, tD, _pick(D_PROJ, (768, 512, 256))), [], [BF16], ident,
                             comm=[("a2a", _rows_slots(dW_inT))])
    grad_x, sums_m = _modulate_bwd("mod_mix_bwd", xs, cx, dH, norm_mix, mix_mod, modc, dx1, None, None)

    zero = jnp.zeros((1, D), F32)
    dmod = jnp.concatenate([sums_m[0:1], sums_m[1:2], sums_f[6:7], sums_f[0:1], sums_f[1:2], sums_l[2:3]], axis=1)
    dmodc = jnp.concatenate([sums_m[3:4], sums_m[4:5], zero, zero, zero, zero], axis=1)
    dm_all = _allgather(jnp.concatenate([dmod, dmodc], axis=0), "ag_dmod")
    dm_cols = lax.dynamic_slice_in_dim(dm_all, me * C6, C6, axis=2)
    dm_in = jnp.concatenate([dm_cols[:, 0, :], dm_cols[:, 1, :]], axis=0)
    s_bwd = jnp.concatenate([cs, jnp.broadcast_to(c_ctx[None, :], (N_DEV, D))], axis=0)
    g_w_mod, dsil = _mod_bwd(s_bwd, dm_in, w_mod[0])

    lane_pad = lambda a: _pad_rows(a.reshape(-1, 1), LANES).reshape(1, LANES)
    pack = jnp.concatenate([dsil[0:1], sums_m[2:3], sums_f[2:3], sums_l[1:2],
                            lane_pad(d_rd[:, 0]), lane_pad(d_sink[:, 0]), sums_l[3:4, 0:LANES]], axis=1)
    packs = _allgather(pack, "ag_small")
    zl = jnp.zeros((1, LANES), F32)

    def pack_w(a_c, a_nm, a_nf, a_fin, a_rd, a_sk):
        return jnp.concatenate([a_c.reshape(1, D), a_nm, a_nf, a_fin.reshape(1, D), lane_pad(a_rd.reshape(-1)),
                                lane_pad(a_sk.reshape(-1)), zl], axis=1)

    sg, sd, sm, sv = _adam("adam_small", pack_w(c_ctx, norm_mix, norm_ffn, norm_final, ret_decay, attn_sink),
                           pack_w(m_c_ctx, m_norm_mix, m_norm_ffn, m_norm_final, m_ret_decay, m_attn_sink),
                           pack_w(v_c_ctx, v_norm_mix, v_norm_ffn, v_norm_final, v_ret_decay, v_attn_sink),
                           parts=packs)
    loss = sg[0, 4 * D + 2 * LANES]

    def unpack(a):
        return (a[0, 0:D], a[:, D:2 * D], a[:, 2 * D:3 * D], a[0, 3 * D:4 * D],
                a[0, 4 * D:4 * D + 2 * RET_HEADS].reshape(1, 2, RET_HEADS),
                a[:, 4 * D + LANES:4 * D + LANES + ATT_HEADS])

    bg, bd, bm, bv = _adam("adam_b_mod", b_mod, m_b_mod, v_b_mod, parts=dm_all.reshape(2 * N_DEV, 1, 6 * D))
    wg, wd, wm, wv = _adam("adam_w_mod", w_mod[0], m_w_mod[0], v_w_mod[0], g=g_w_mod)

    big = {}
    for nm, w, m, v, parts, transposed in (
            ("w_in", w_in, m_w_in, v_w_in, p_in, True), ("w_out", w_out, m_w_out, v_w_out, p_out, False),
            ("w_gate", w_gate, m_w_gate, v_w_gate, p_gate, True), ("w_up", w_up, m_w_up, v_w_up, p_up, True),
            ("w_down", w_down, m_w_down, v_w_down, p_down, False)):
        if transposed:
            res = [a.T for a in _adam("adam_" + nm, w[0].T, m[0].T, v[0].T, parts=parts)]
        else:
            res = _adam("adam_" + nm, w[0], m[0], v[0], parts=parts)
        big[nm] = [a[None] for a in res]

    g_s, d_s, m_s, v_s = unpack(sg), unpack(sd), unpack(sm), unpack(sv)

    def leaves(k, small, bmod, wmod):
        return (small[0], wmod[None], bmod, small[1], small[2], big["w_in"][k], small[4], small[5],
                big["w_out"][k], big["w_gate"][k], big["w_up"][k], big["w_down"][k], small[3])

    return (loss, grad_x[None], *leaves(0, g_s, bg, wg), *leaves(1, d_s, bd, wd),
            *leaves(2, m_s, bm, wm), *leaves(3, v_s, bv, wv))
```

```python
import functools

import jax
import jax.numpy as jnp
from jax import lax
from jax.experimental import pallas as pl
from jax.experimental.pallas import tpu as pltpu

F32 = jnp.float32
BF16 = jnp.bfloat16

N_DEV = 8
LANES = 128
RET_HEADS = 8
RET_DK = 64
RET_DV = 128
CHUNK = 128
ATT_HEADS = 16
ATT_KV = 4
ATT_DH = 64
GRID_W = 64
ROPE_BASE = 10000.0
EPS = 1e-6
NEG = -1e30
C_RQ, C_RK, C_RV, C_RG, C_AQ, C_AK, C_AV, D_PROJ = 0, 512, 1024, 2048, 3072, 4096, 4352, 4608
K_SCALE = RET_DK ** -0.5
A_SCALE = ATT_DH ** -0.5

ADAM_LR, ADAM_B1, ADAM_B2, ADAM_EPS, ADAM_WD, ADAM_STEP = 0.001, 0.9, 0.999, 1e-08, 0.01, 10

VMEM_BIG = 58 * 1024 * 1024

NN = (((1,), (0,)), ((), ()))
NT = (((1,), (1,)), ((), ()))
TN = (((0,), (0,)), ((), ()))


def _dot(a, b, dims):
    return lax.dot_general(a, b, dims, preferred_element_type=F32)


def _cparams(sem, vmem=VMEM_BIG):
    return pltpu.CompilerParams(dimension_semantics=sem, vmem_limit_bytes=vmem)


def _pick(dim, prefs):
    for p in prefs:
        if dim % p == 0:
            return p
    return dim


def _my_id():
    return lax.axis_index("x") * 4 + lax.axis_index("y") * 2 + lax.axis_index("c")


def _sigmoid(x):
    return 0.5 * jnp.tanh(0.5 * x) + 0.5


def _peers():
    mx, my, mc = lax.axis_index("x"), lax.axis_index("y"), lax.axis_index("c")
    out = []
    for k in range(1, N_DEV):
        kx, ky, kc = (k >> 2) & 1, (k >> 1) & 1, k & 1
        px = 1 - mx if kx else mx
        py = 1 - my if ky else my
        pc = 1 - mc if kc else mc
        out.append(((px, py, pc), px * 4 + py * 2 + pc))
    return out


def _exchange_copies(kind, x_ref, o_ref, ssem, rsem, lsem):
    me = _my_id()
    loc = pltpu.make_async_copy(x_ref if kind == "ag" else x_ref.at[me], o_ref.at[me], lsem)
    cps = []
    for k, (peer, pid) in enumerate(_peers()):
        cps.append(pltpu.make_async_remote_copy(
            src_ref=x_ref if kind == "ag" else x_ref.at[pid], dst_ref=o_ref.at[me],
            send_sem=ssem.at[k], recv_sem=rsem.at[k], device_id=peer, device_id_type=pl.DeviceIdType.MESH))
    return loc, cps


def _two_level_copies(x_ref, o_ref, ssem, rsem, lsem):
    mx, my, mc = lax.axis_index("x"), lax.axis_index("y"), lax.axis_index("c")
    me = mx * 4 + my * 2 + mc
    sibling = (mx, my, 1 - mc)
    chips = [(1 - mx, my), (mx, 1 - my), (1 - mx, 1 - my)]

    def copy(k, slot, to, src=None):
        return pltpu.make_async_remote_copy(
            src_ref=o_ref.at[slot] if src is None else src, dst_ref=o_ref.at[slot],
            send_sem=ssem.at[k], recv_sem=rsem.at[k], device_id=to, device_id_type=pl.DeviceIdType.MESH)

    loc = pltpu.make_async_copy(x_ref, o_ref.at[me], lsem)
    first = [copy(0, me, sibling, src=x_ref)]
    first += [copy(1 + j, me, (cx, cy, mc), src=x_ref) for j, (cx, cy) in enumerate(chips)]
    passed = [copy(4 + j, cx * 4 + cy * 2 + mc, sibling) for j, (cx, cy) in enumerate(chips)]
    return loc, first, passed


def _exchange_start(kind, x_ref, o_ref, ssem, rsem, lsem):
    if kind == "ag2":
        loc, first, _ = _two_level_copies(x_ref, o_ref, ssem, rsem, lsem)
        cps = first
    else:
        loc, cps = _exchange_copies(kind, x_ref, o_ref, ssem, rsem, lsem)
    loc.start()
    for cp in cps:
        cp.start()


def _exchange_pass_on(kind, x_ref, o_ref, ssem, rsem, lsem):
    if kind == "ag2":
        _, first, passed = _two_level_copies(x_ref, o_ref, ssem, rsem, lsem)
        for j in range(3):
            first[1 + j].wait_recv()
            passed[j].start()


def _exchange_wait(kind, x_ref, o_ref, ssem, rsem, lsem):
    if kind == "ag2":
        loc, first, passed = _two_level_copies(x_ref, o_ref, ssem, rsem, lsem)
        first[0].wait_recv()
        for cp in passed:
            cp.wait_recv()
        cps = first + passed
    else:
        loc, cps = _exchange_copies(kind, x_ref, o_ref, ssem, rsem, lsem)
        for cp in cps:
            cp.wait_recv()
    for cp in cps:
        cp.wait_send()
    loc.wait()


_EXCHANGE_SEMS = [pltpu.SemaphoreType.DMA((N_DEV - 1,)), pltpu.SemaphoreType.DMA((N_DEV - 1,)),
                  pltpu.SemaphoreType.DMA(())]


def _exchange_shape(kind, x):
    return jax.ShapeDtypeStruct(x.shape if kind == "a2a" else (N_DEV,) + x.shape, x.dtype)


def _exchange(kind, x, name):
    def body(x_ref, o_ref, ssem, rsem, lsem):
        _exchange_start(kind, x_ref, o_ref, ssem, rsem, lsem)
        _exchange_pass_on(kind, x_ref, o_ref, ssem, rsem, lsem)
        _exchange_wait(kind, x_ref, o_ref, ssem, rsem, lsem)

    return pl.pallas_call(
        body, name=name, out_shape=_exchange_shape(kind, x),
        in_specs=[pl.BlockSpec(memory_space=pl.ANY)], out_specs=pl.BlockSpec(memory_space=pl.ANY),
        scratch_shapes=list(_EXCHANGE_SEMS),
    )(x)


def _allgather(x, name):
    return _exchange("ag", x, name)


def _call(body, name, grid, in_specs, out_specs, out_shape, scratch_shapes, sem, args, comm=(), aliases=None,
          pass_on_at=0.75):
    in_specs, out_specs, out_shape = list(in_specs), list(out_specs), list(out_shape)
    scratch_shapes = list(scratch_shapes)
    aliases = aliases or {}
    if not comm:
        outs = pl.pallas_call(body, name=name, grid=grid, in_specs=in_specs, out_specs=out_specs, out_shape=out_shape,
                              scratch_shapes=scratch_shapes, input_output_aliases=aliases,
                              compiler_params=_cparams(sem))(*args)
        return list(outs), []
    n_in, n_out, n_scr, n_c = len(in_specs), len(out_specs), len(scratch_shapes), len(comm)
    hbm = pl.BlockSpec(memory_space=pl.ANY)

    def wrapped(*refs):
        ins, cins = refs[:n_in], refs[n_in:n_in + n_c]
        outs = refs[n_in + n_c:n_in + n_c + n_out]
        couts = refs[n_in + n_c + n_out:n_in + 2 * n_c + n_out]
        scr = refs[n_in + 2 * n_c + n_out:n_in + 2 * n_c + n_out + n_scr]
        sems = refs[n_in + 2 * n_c + n_out + n_scr:]
        step, total = pl.program_id(0), grid[0]
        for ax in range(1, len(grid)):
            step = step * grid[ax] + pl.program_id(ax)
            total *= grid[ax]

        @pl.when(step == 0)
        def _():
            for c, (kind, _) in enumerate(comm):
                _exchange_start(kind, cins[c], couts[c], *sems[3 * c:3 * c + 3])

        body(*ins, *outs, *scr)

        @pl.when(step == min(total - 1, int(total * pass_on_at)))
        def _():
            for c, (kind, _) in enumerate(comm):
                _exchange_pass_on(kind, cins[c], couts[c], *sems[3 * c:3 * c + 3])

        @pl.when(step == total - 1)
        def _():
            for c, (kind, _) in enumerate(comm):
                _exchange_wait(kind, cins[c], couts[c], *sems[3 * c:3 * c + 3])

    res = pl.pallas_call(
        wrapped, name=name, grid=grid,
        in_specs=in_specs + [hbm] * n_c, out_specs=out_specs + [hbm] * n_c,
        out_shape=out_shape + [_exchange_shape(kind, arr) for kind, arr in comm],
        scratch_shapes=scratch_shapes + list(_EXCHANGE_SEMS) * n_c, input_output_aliases=aliases,
        compiler_params=_cparams(("arbitrary",) * len(grid)),
    )(*args, *[arr for _, arr in comm])
    return list(res[:n_out]), list(res[n_out:])


def _matmul(name, pairs, n_acc, M, N, K, mode, tiles, extras, out_dtypes, epilogue, j_outer=False, comm=()):
    tm, tn, tk = tiles
    gm, gn, nk = M // tm, N // tn, K // tk
    assert gm * tm == M and gn * tn == N and nk * tk == K, (name, M, N, K, tiles)
    if j_outer:
        grid = (gn, gm, nk)
        ij = lambda g0, g1: (g1, g0)
    else:
        grid = (gm, gn, nk)
        ij = lambda g0, g1: (g0, g1)

    if mode in ("nn", "nt"):
        a_spec = pl.BlockSpec((tm, tk), lambda g0, g1, k: (ij(g0, g1)[0], k))
    else:
        a_spec = pl.BlockSpec((tk, tm), lambda g0, g1, k: (k, ij(g0, g1)[0]))
    if mode == "nt":
        b_spec = pl.BlockSpec((tn, tk), lambda g0, g1, k: (ij(g0, g1)[1], k))
    else:
        b_spec = pl.BlockSpec((tk, tn), lambda g0, g1, k: (k, ij(g0, g1)[1]))
    dims = {"nn": NN, "nt": NT, "tn": TN}[mode]
    mn_spec = pl.BlockSpec((tm, tn), lambda g0, g1, k: ij(g0, g1))
    n_spec = pl.BlockSpec((1, tn), lambda g0, g1, k: (0, ij(g0, g1)[1]))

    in_specs, args = [], []
    for a, b, _ in pairs:
        in_specs += [a_spec, b_spec]
        args += [a, b]
    for arr, kind in extras:
        in_specs.append(mn_spec if kind == "mn" else n_spec)
        args.append(arr)
    n_p, n_e, n_o = len(pairs), len(extras), len(out_dtypes)

    def body(*refs):
        ab = refs[:2 * n_p]
        ex = refs[2 * n_p:2 * n_p + n_e]
        outs = refs[2 * n_p + n_e:2 * n_p + n_e + n_o]
        accs = refs[2 * n_p + n_e + n_o:]
        k = pl.program_id(2)

        def single_step():
            rc = _pick(tm, (1024,)) if mode != "tn" else tm
            for c in range(tm // rc):
                rs = slice(c * rc, (c + 1) * rc)
                sums = [None] * n_acc
                for p, (_, _, ai) in enumerate(pairs):
                    a_ref, b_ref = ab[2 * p], ab[2 * p + 1]
                    d = _dot(a_ref[...] if mode == "tn" else a_ref[rs, :], b_ref[...], dims)
                    sums[ai] = d if sums[ai] is None else sums[ai] + d
                res = epilogue(sums, [e[rs, :] if e.shape[0] == tm else e[...] for e in ex])
                for o, r in zip(outs, res):
                    o[rs, :] = r.astype(o.dtype)

        def finish(acc_vals):
            res = epilogue(acc_vals, [e[...] for e in ex])
            for o, r in zip(outs, res):
                o[...] = r.astype(o.dtype)

        def accumulate(first):
            w = _pick(tm if mode == "tn" else tn, (512, 384, 256))
            for c in range((tm if mode == "tn" else tn) // w):
                sl = slice(c * w, (c + 1) * w)
                sums = [None] * n_acc
                for p, (_, _, ai) in enumerate(pairs):
                    a_ref, b_ref = ab[2 * p], ab[2 * p + 1]
                    if mode == "tn":
                        d = _dot(a_ref[:, sl], b_ref[...], dims)
                    elif mode == "nn":
                        d = _dot(a_ref[...], b_ref[:, sl], dims)
                    else:
                        d = _dot(a_ref[...], b_ref[sl, :], dims)
                    sums[ai] = d if sums[ai] is None else sums[ai] + d
                idx = (sl, slice(None)) if mode == "tn" else (slice(None), sl)
                for ai, s in enumerate(sums):
                    if first:
                        accs[ai][idx] = s
                    else:
                        accs[ai][idx] += s

        if nk == 1:
            single_step()
        else:
            pl.when(k == 0)(functools.partial(accumulate, True))
            pl.when(k > 0)(functools.partial(accumulate, False))

            @pl.when(k == nk - 1)
            def _():
                finish([a[...] for a in accs])

    outs, couts = _call(
        body, name, grid, in_specs, [mn_spec] * n_o,
        [jax.ShapeDtypeStruct((M, N), dt) for dt in out_dtypes],
        [pltpu.VMEM((tm, tn), F32) for _ in range(n_acc if nk > 1 else 0)],
        ("parallel", "parallel", "arbitrary"), args, comm)
    return (outs, couts) if comm else outs


def _rope_tables(L):
    t = jnp.arange(L, dtype=jnp.int32)
    f = jnp.arange(32, dtype=jnp.int32).astype(F32)
    ang = t.astype(F32)[:, None] * (ROPE_BASE ** (-f / 32.0))[None, :]
    cos, sin = jnp.cos(ang), jnp.sin(ang)
    ret = jnp.stack([jnp.tile(cos, (1, 4)), jnp.tile(jnp.concatenate([-sin, sin], axis=1), (1, 2))])
    f2 = jnp.arange(16, dtype=jnp.int32).astype(F32)
    inv2 = (ROPE_BASE ** (-f2 / 16.0))[None, :]
    ang_r = (t // GRID_W).astype(F32)[:, None] * inv2
    ang_c = (t % GRID_W).astype(F32)[:, None] * inv2
    cr, sr, cc, sc = jnp.cos(ang_r), jnp.sin(ang_r), jnp.cos(ang_c), jnp.sin(ang_c)
    att = jnp.stack([jnp.tile(jnp.concatenate([cr, cr, cc, cc], axis=1), (1, 2)),
                     jnp.tile(jnp.concatenate([-sr, sr, -sc, sc], axis=1), (1, 2))])
    return ret.astype(F32), att.astype(F32)


def _swap(x, sh):
    lane = lax.broadcasted_iota(jnp.int32, x.shape, 1)
    ra = pltpu.roll(x, LANES - sh, 1)
    rb = pltpu.roll(x, sh, 1)
    la = pltpu.roll(lane, LANES - sh, 1)
    partner = jnp.where((lane % (2 * sh)) < sh, lane + sh, lane - sh)
    return jnp.where(la == partner, ra, rb)


def _rope(x, cos, sin, sh):
    return x * cos + _swap(x, sh) * sin


def _rope_t(d, cos, sin, sh):
    return d * cos + _swap(d * sin, sh)


def _half_mask(shape, a):
    lane = lax.broadcasted_iota(jnp.int32, shape, 1)
    return (lane < 64) if a == 0 else (lane >= 64)


def _mod_fwd(s_in, w_l, b_l):
    D, C6 = w_l.shape
    tk = _pick(D, (512, 256, 128))
    nk = D // tk

    def body(s_ref, w_ref, b_ref, o_ref):
        k = pl.program_id(0)
        s = s_ref[...]
        s = s * _sigmoid(s)
        d = jnp.dot(s, w_ref[...], preferred_element_type=F32, precision=lax.Precision.HIGHEST)

        @pl.when(k == 0)
        def _():
            o_ref[...] = d + b_ref[...]

        @pl.when(k > 0)
        def _():
            o_ref[...] += d

    return pl.pallas_call(
        body, name="mod_fwd", grid=(nk,),
        in_specs=[pl.BlockSpec((16, tk), lambda k: (0, k)), pl.BlockSpec((tk, C6), lambda k: (k, 0)),
                  pl.BlockSpec((1, C6), lambda k: (0, 0))],
        out_specs=pl.BlockSpec((16, C6), lambda k: (0, 0)),
        out_shape=jax.ShapeDtypeStruct((16, C6), F32),
        compiler_params=_cparams(("arbitrary",)),
    )(s_in, w_l, b_l)


def _mod_bwd(s_in, dm, w_l):
    D, C6 = w_l.shape
    tk = _pick(D, (512, 256, 128))
    nk = D // tk

    def body(s_ref, dm_ref, w_ref, gw_ref, gc_ref):
        s = s_ref[...]
        sg = _sigmoid(s)
        act = s * sg
        dmv = dm_ref[...]
        gw_ref[...] = lax.dot_general(act, dmv, TN, preferred_element_type=F32, precision=lax.Precision.HIGHEST)
        ds = lax.dot_general(dmv, w_ref[...], NT, preferred_element_type=F32, precision=lax.Precision.HIGHEST)
        dsil = (sg * (1.0 + s * (1.0 - sg)))[8:9, :]
        gc_ref[...] = jnp.zeros((8, tk), F32) + jnp.sum(ds[8:16, :], axis=0, keepdims=True) * dsil

    return pl.pallas_call(
        body, name="mod_bwd", grid=(nk,),
        in_specs=[pl.BlockSpec((16, tk), lambda k: (0, k)), pl.BlockSpec((16, C6), lambda k: (0, 0)),
                  pl.BlockSpec((tk, C6), lambda k: (k, 0))],
        out_specs=[pl.BlockSpec((tk, C6), lambda k: (k, 0)), pl.BlockSpec((8, tk), lambda k: (0, k))],
        out_shape=[jax.ShapeDtypeStruct((D, C6), F32), jax.ShapeDtypeStruct((8, D), F32)],
        compiler_params=_cparams(("parallel",)),
    )(s_in, dm, w_l)


def _norm_rows(x):
    r = lax.rsqrt(jnp.mean(x * x, axis=-1, keepdims=True) + EPS)
    return x * r, r


def _modulate_fwd(name, x, ctx, g, mod, modc, comm=()):
    L, D = x.shape
    tr = ctx.shape[0]
    nx = L // tr

    def body(x_ref, c_ref, g_ref, m_ref, mc_ref, o_ref):
        i = pl.program_id(0)

        def run(src, m):
            n, _ = _norm_rows(src[...])
            o_ref[...] = (n * g_ref[...] * (1.0 + m[1:2, :]) + m[0:1, :]).astype(o_ref.dtype)

        @pl.when(i < nx)
        def _():
            run(x_ref, m_ref)

        @pl.when(i >= nx)
        def _():
            run(c_ref, mc_ref)

    row = pl.BlockSpec((tr, D), lambda i: (jnp.minimum(i, nx - 1), 0))
    vec = pl.BlockSpec((1, D), lambda i: (0, 0))
    mv = pl.BlockSpec((8, D), lambda i: (0, 0))
    return _call(
        body, name, (nx + 1,), [row, pl.BlockSpec((tr, D), lambda i: (0, 0)), vec, mv, mv],
        [pl.BlockSpec((tr, D), lambda i: (i, 0))], [jax.ShapeDtypeStruct((L + tr, D), BF16)], [],
        ("parallel",), (x, ctx, g, mod, modc), comm, pass_on_at=1.0)


def _residual_modulate_fwd(name, x, fbr, gate, g, mod):
    L, D = x.shape
    tr = _pick(L, (512, 256, 128))

    def body(x_ref, f_ref, gt_ref, g_ref, m_ref, x1_ref, o_ref):
        x1 = x_ref[...] + gt_ref[...] * f_ref[...].astype(F32)
        x1_ref[...] = x1
        n, _ = _norm_rows(x1)
        o_ref[...] = (n * g_ref[...] * (1.0 + m_ref[1:2, :]) + m_ref[0:1, :]).astype(o_ref.dtype)

    row = pl.BlockSpec((tr, D), lambda i: (i, 0))
    vec = pl.BlockSpec((1, D), lambda i: (0, 0))
    return pl.pallas_call(
        body, name=name, grid=(L // tr,),
        in_specs=[row, row, vec, vec, pl.BlockSpec((8, D), lambda i: (0, 0))],
        out_specs=[row, row],
        out_shape=[jax.ShapeDtypeStruct((L, D), F32), jax.ShapeDtypeStruct((L, D), BF16)],
        compiler_params=_cparams(("parallel",)),
    )(x, fbr, gate, g, mod)


def _modulate_bwd(name, x, ctx, dh, g, mod, modc, dres, fbr, gate):
    L, D = x.shape
    tr = ctx.shape[0] if ctx is not None else _pick(L, (512, 256, 128))
    nx = L // tr
    nt = nx + (1 if ctx is not None else 0)
    has_f = fbr is not None

    def body(*refs):
        refs = list(refs)
        x_ref = refs.pop(0)
        c_ref = refs.pop(0) if ctx is not None else None
        dh_ref, g_ref, m_ref = refs.pop(0), refs.pop(0), refs.pop(0)
        mc_ref = refs.pop(0) if ctx is not None else None
        dr_ref = refs.pop(0)
        f_ref = refs.pop(0) if has_f else None
        gt_ref = refs.pop(0) if has_f else None
        dx_ref = refs.pop(0)
        df_ref = refs.pop(0) if has_f else None
        acc_ref = refs.pop(0)
        i = pl.program_id(0)

        @pl.when(i == 0)
        def _():
            acc_ref[...] = jnp.zeros_like(acc_ref)

        def sums(src, m, base, grow):
            n, r = _norm_rows(src[...])
            d = dh_ref[...].astype(F32)
            gg = g_ref[...]
            sc1 = 1.0 + m[1:2, :]
            acc_ref[base:base + 1, :] += jnp.sum(d, axis=0, keepdims=True)
            dn = d * n
            acc_ref[base + 1:base + 2, :] += jnp.sum(dn, axis=0, keepdims=True) * gg
            acc_ref[grow:grow + 1, :] += jnp.sum(dn, axis=0, keepdims=True) * sc1
            dnv = d * (gg * sc1)
            return r * (dnv - n * jnp.mean(dnv * n, axis=-1, keepdims=True))

        def x_rows():
            dx = sums(x_ref, m_ref, 0, 2) + dr_ref[...]
            dx_ref[...] = dx
            if has_f:
                acc_ref[6:7, :] += jnp.sum(dx * f_ref[...].astype(F32), axis=0, keepdims=True)
                df_ref[...] = (dx * gt_ref[...]).astype(df_ref.dtype)

        if ctx is None:
            x_rows()
        else:
            pl.when(i < nx)(x_rows)

            @pl.when(i >= nx)
            def _():
                sums(c_ref, mc_ref, 3, 2)

    row = pl.BlockSpec((tr, D), lambda i: (jnp.minimum(i, nx - 1), 0))
    vec = pl.BlockSpec((1, D), lambda i: (0, 0))
    mv = pl.BlockSpec((8, D), lambda i: (0, 0))
    in_specs, args = [row], [x]
    if ctx is not None:
        in_specs.append(pl.BlockSpec((tr, D), lambda i: (0, 0)))
        args.append(ctx)
    in_specs += [pl.BlockSpec((tr, D), lambda i: (i, 0)), vec, mv]
    args += [dh, g, mod]
    if ctx is not None:
        in_specs.append(mv)
        args.append(modc)
    in_specs.append(row)
    args.append(dres)
    out_specs = [row]
    out_shape = [jax.ShapeDtypeStruct((L, D), F32)]
    if has_f:
        in_specs += [row, vec]
        args += [fbr, gate]
        out_specs.append(row)
        out_shape.append(jax.ShapeDtypeStruct((L, D), BF16))
    out_specs.append(pl.BlockSpec((16, D), lambda i: (0, 0)))
    out_shape.append(jax.ShapeDtypeStruct((16, D), F32))
    return pl.pallas_call(
        body, name=name, grid=(nt,), in_specs=in_specs, out_specs=out_specs, out_shape=out_shape,
        compiler_params=_cparams(("arbitrary",)),
    )(*args)


def _loss_head(x1, tgt, nf, fbr, gate):
    L, D = x1.shape
    tr = _pick(L, (512, 256, 128))

    def body(x_ref, t_ref, w_ref, f_ref, gt_ref, dx_ref, df_ref, acc_ref):
        i = pl.program_id(0)

        @pl.when(i == 0)
        def _():
            acc_ref[...] = jnp.zeros_like(acc_ref)

        n, r = _norm_rows(x_ref[...] + gt_ref[...] * f_ref[...].astype(F32))
        w = w_ref[...]
        e = n * w - t_ref[...]
        acc_ref[0:1, :] += jnp.sum(e * e, axis=0, keepdims=True) * (0.5 / D)
        dout = e * (1.0 / D)
        acc_ref[1:2, :] += jnp.sum(dout * n, axis=0, keepdims=True)
        dn = dout * w
        dx = r * (dn - n * jnp.mean(dn * n, axis=-1, keepdims=True))
        dx_ref[...] = dx
        acc_ref[2:3, :] += jnp.sum(dx * f_ref[...].astype(F32), axis=0, keepdims=True)
        df_ref[...] = (dx * gt_ref[...]).astype(df_ref.dtype)

        @pl.when(i == pl.num_programs(0) - 1)
        def _():
            acc_ref[3:4, :] = jnp.zeros((1, D), F32) + jnp.sum(acc_ref[0:1, :])

    row = pl.BlockSpec((tr, D), lambda i: (i, 0))
    vec = pl.BlockSpec((1, D), lambda i: (0, 0))
    return pl.pallas_call(
        body, name="loss_head", grid=(L // tr,),
        in_specs=[row, row, vec, row, vec],
        out_specs=[row, row, pl.BlockSpec((8, D), lambda i: (0, 0))],
        out_shape=[jax.ShapeDtypeStruct((L, D), F32), jax.ShapeDtypeStruct((L, D), BF16),
                   jax.ShapeDtypeStruct((8, D), F32)],
        compiler_params=_cparams(("arbitrary",)),
    )(x1, tgt, nf, fbr, gate)


RET_SUB = 4
N_TAB = 7


def _ret_tables(rdb, Lc):
    def body(rd_ref, t_ref, c_ref):
        d = pl.program_id(0) // RET_HEADS
        fwd = d == 0
        lg = -jnp.exp(rd_ref[0])
        i = lax.broadcasted_iota(jnp.int32, (CHUNK, CHUNK), 0).astype(F32)
        j = lax.broadcasted_iota(jnp.int32, (CHUNK, CHUNK), 1).astype(F32)
        rel = jnp.where(fwd, i - j, j - i)
        mask = (rel > 0.0) | ((rel == 0.0) & fwd)
        dm = jnp.where(mask, jnp.exp(lg * jnp.maximum(rel, 0.0)), 0.0)
        t_ref[0, 0] = dm
        t_ref[0, 1] = rel * dm
        qc = jnp.where(fwd, i + 1.0, CHUNK - i)
        qw = jnp.exp(lg * qc)
        t_ref[0, 2] = qw
        t_ref[0, 3] = qw * qc
        kc = jnp.where(fwd, CHUNK - 1.0 - i, i)
        kw = jnp.exp(lg * kc)
        t_ref[0, 4] = kw
        t_ref[0, 5] = kw * kc
        t_ref[0, 6] = jnp.exp(lg * float(CHUNK)) + jnp.zeros((CHUNK, CHUNK), F32)
        m = lax.broadcasted_iota(jnp.int32, (Lc, LANES), 0).astype(F32)
        cc = jnp.where(fwd, Lc - 1.0 - m, m)
        cw = jnp.exp(lg * cc)
        c_ref[0, 0] = cw
        c_ref[0, 1] = cw * cc

    return pl.pallas_call(
        body, name="ret_tables", grid=(2 * RET_HEADS,),
        in_specs=[pl.BlockSpec((1, 1, LANES), lambda r: (r, 0, 0))],
        out_specs=[pl.BlockSpec((1, N_TAB, CHUNK, CHUNK), lambda r: (r, 0, 0, 0)),
                   pl.BlockSpec((1, 2, Lc, LANES), lambda r: (r, 0, 0, 0))],
        out_shape=[jax.ShapeDtypeStruct((2 * RET_HEADS, N_TAB, CHUNK, CHUNK), F32),
                   jax.ShapeDtypeStruct((2 * RET_HEADS, 2, Lc, LANES), F32)],
        compiler_params=_cparams(("parallel",)),
    )(rdb)


def _ret_ctx_state(P, ctab, L, Lc):
    cb = L // Lc

    def body(k_ref, v_ref, c_ref, s_ref):
        for p in range(RET_HEADS // 2):
            kp = k_ref[:, p * LANES:(p + 1) * LANES].astype(F32) * K_SCALE
            for a in range(2):
                h = 2 * p + a
                kh = jnp.where(_half_mask(kp.shape, a), kp, 0.0)
                vh = v_ref[:, h * RET_DV:(h + 1) * RET_DV]
                for d in range(2):
                    kw = (kh * c_ref[d * RET_HEADS + h, 0]).astype(BF16)
                    s_ref[d * RET_HEADS + h] = _dot(kw, vh, TN)

    return pl.pallas_call(
        body, name="ret_ctx_state", grid=(1,),
        in_specs=[pl.BlockSpec((Lc, 512), lambda i: (cb, C_RK // 512)),
                  pl.BlockSpec((Lc, 1024), lambda i: (cb, C_RV // 1024)),
                  pl.BlockSpec((2 * RET_HEADS, 2, Lc, LANES), lambda i: (0, 0, 0, 0))],
        out_specs=pl.BlockSpec((2 * RET_HEADS, LANES, RET_DV), lambda i: (0, 0, 0)),
        out_shape=jax.ShapeDtypeStruct((2 * RET_HEADS, LANES, RET_DV), F32),
        compiler_params=_cparams(("arbitrary",)),
    )(P, P, ctab)


def _ret_fwd(P, rope, tabs, s0, L, comm=()):
    n = L // CHUNK
    nb = n // RET_SUB

    def body(qf, kf, vf, rf, qb, kb, vb, rb, t_ref, s0_ref, of_ref, ob_ref, stf_ref, stb_ref, st):
        s = pl.program_id(0)

        @pl.when(s == 0)
        def _():
            st[...] = s0_ref[...]

        for rnd in range(RET_SUB):
            units = []
            for d, (q_ref, k_ref, v_ref, r_ref, o_ref, so_ref) in enumerate(
                    ((qf, kf, vf, rf, of_ref, stf_ref), (qb, kb, vb, rb, ob_ref, stb_ref))):
                j = rnd if d == 0 else RET_SUB - 1 - rnd
                rows = slice(j * CHUNK, (j + 1) * CHUNK)
                cos, sin = r_ref[0, rows, :], r_ref[1, rows, :]
                for p in range(RET_HEADS // 2):
                    qp = _rope(q_ref[rows, p * LANES:(p + 1) * LANES].astype(F32), cos, sin, 32)
                    kp = _rope(k_ref[rows, p * LANES:(p + 1) * LANES].astype(F32), cos, sin, 32) * K_SCALE
                    for a in range(2):
                        h = 2 * p + a
                        hm = _half_mask(qp.shape, a)
                        units.append(dict(r=d * RET_HEADS + h, h=h, a=a, j=j, rows=rows, o_ref=o_ref, so_ref=so_ref,
                                          v_ref=v_ref, qh=jnp.where(hm, qp, 0.0), kh=jnp.where(hm, kp, 0.0)))
            for u in units:
                u["sc"] = _dot(u["qh"].astype(BF16), u["kh"].astype(BF16), NT)
            for u in units:
                r, h = u["r"], u["h"]
                sp = st[r]
                u["so_ref"][u["j"], h] = sp[u["a"] * RET_DK:(u["a"] + 1) * RET_DK, :]
                vh = u["v_ref"][u["rows"], h * RET_DV:(h + 1) * RET_DV]
                o = _dot((u["sc"] * t_ref[r, 0]).astype(BF16), vh, NN)
                o += _dot((u["qh"] * t_ref[r, 2]).astype(BF16), sp.astype(BF16), NN)
                u["o_ref"][u["rows"], h * RET_DV:(h + 1) * RET_DV] = o
            for u in units:
                r, h = u["r"], u["h"]
                vh = u["v_ref"][u["rows"], h * RET_DV:(h + 1) * RET_DV]
                st[r] = t_ref[r, 6] * st[r] + _dot((u["kh"] * t_ref[r, 4]).astype(BF16), vh, TN)

    fw = lambda s: s
    bw = lambda s: nb - 1 - s
    RB = RET_SUB * CHUNK

    def specs(cm):
        return [pl.BlockSpec((RB, 512), lambda s: (cm(s), C_RQ // 512)),
                pl.BlockSpec((RB, 512), lambda s: (cm(s), C_RK // 512)),
                pl.BlockSpec((RB, 1024), lambda s: (cm(s), C_RV // 1024)),
                pl.BlockSpec((2, RB, LANES), lambda s: (0, cm(s), 0))]

    full = lambda shp: pl.BlockSpec(shp, lambda s: (0,) * len(shp))
    return _call(
        body, "ret_fwd", (nb,),
        specs(fw) + specs(bw) + [full((2 * RET_HEADS, N_TAB, CHUNK, CHUNK)), full((2 * RET_HEADS, LANES, RET_DV))],
        [pl.BlockSpec((RB, 1024), lambda s: (fw(s), 0)),
         pl.BlockSpec((RB, 1024), lambda s: (bw(s), 0)),
         pl.BlockSpec((RET_SUB, RET_HEADS, RET_DK, RET_DV), lambda s: (fw(s), 0, 0, 0)),
         pl.BlockSpec((RET_SUB, RET_HEADS, RET_DK, RET_DV), lambda s: (bw(s), 0, 0, 0))],
        [jax.ShapeDtypeStruct((L, 1024), F32), jax.ShapeDtypeStruct((L, 1024), F32),
         jax.ShapeDtypeStruct((n, RET_HEADS, RET_DK, RET_DV), F32),
         jax.ShapeDtypeStruct((n, RET_HEADS, RET_DK, RET_DV), F32)],
        [pltpu.VMEM((2 * RET_HEADS, LANES, RET_DV), F32)],
        ("arbitrary",), (P, P, P, rope, P, P, P, rope, tabs, s0), comm)


def _ret_finish_fwd(of, ob, P, L):
    tr = _pick(L, (1024, 512, 256, 128))

    def body(f_ref, b_ref, g_ref, y_ref):
        for h in range(RET_HEADS):
            sl = slice(h * RET_DV, (h + 1) * RET_DV)
            n, _ = _norm_rows(f_ref[:, sl] + b_ref[:, sl])
            g = g_ref[:, sl].astype(F32)
            y_ref[:, sl] = (n * (g * _sigmoid(g))).astype(y_ref.dtype)

    row = pl.BlockSpec((tr, 1024), lambda i: (i, 0))
    return pl.pallas_call(
        body, name="ret_finish_fwd", grid=(L // tr,),
        in_specs=[row, row, pl.BlockSpec((tr, 1024), lambda i: (i, C_RG // 1024))],
        out_specs=row, out_shape=jax.ShapeDtypeStruct((L, 2048), BF16),
        compiler_params=_cparams(("parallel",)),
    )(of, ob, P)


def _ret_finish_bwd(of, ob, P, dY, L):
    tr = _pick(L, (1024, 512, 256, 128))

    def body(f_ref, b_ref, g_ref, dy_ref, do_ref, dg_ref):
        for h in range(RET_HEADS):
            sl = slice(h * RET_DV, (h + 1) * RET_DV)
            n, r = _norm_rows(f_ref[:, sl] + b_ref[:, sl])
            g = g_ref[:, sl].astype(F32)
            sg = _sigmoid(g)
            dy = dy_ref[:, sl].astype(F32)
            dg_ref[:, sl] = (dy * n * (sg * (1.0 + g * (1.0 - sg)))).astype(dg_ref.dtype)
            dn = dy * (g * sg)
            do_ref[:, sl] = (r * (dn - n * jnp.mean(dn * n, axis=-1, keepdims=True))).astype(do_ref.dtype)

    row = pl.BlockSpec((tr, 1024), lambda i: (i, 0))
    return pl.pallas_call(
        body, name="ret_finish_bwd", grid=(L // tr,),
        in_specs=[row, row, pl.BlockSpec((tr, 1024), lambda i: (i, C_RG // 1024)), row],
        out_specs=[row, row],
        out_shape=[jax.ShapeDtypeStruct((L, 1024), BF16), jax.ShapeDtypeStruct((L, 1024), BF16)],
        compiler_params=_cparams(("parallel",)),
    )(of, ob, P, dY)


def _ret_bwd(P, rope, tabs, stf, stb, dO, L, comm=()):
    n = L // CHUNK
    nb = n // RET_SUB

    def body(qf, kf, vf, rf, gf, sf, qb, kb, vb, rb, gb, sb, t_ref,
             dqf, dkf, dvf, dqb, dkb, dvb, ds0_ref, dlg_ref, ds):
        s = pl.program_id(0)

        @pl.when(s == 0)
        def _():
            ds[...] = jnp.zeros_like(ds)
            dlg_ref[...] = jnp.zeros_like(dlg_ref)

        for rnd in range(RET_SUB):
            units, pairs = [], []
            for d, (q_ref, k_ref, v_ref, r_ref, g_ref, s_ref, dq_ref, dk_ref, dv_ref) in enumerate(
                    ((qf, kf, vf, rf, gf, sf, dqf, dkf, dvf), (qb, kb, vb, rb, gb, sb, dqb, dkb, dvb))):
                j = RET_SUB - 1 - rnd if d == 0 else rnd
                rows = slice(j * CHUNK, (j + 1) * CHUNK)
                cos, sin = r_ref[0, rows, :], r_ref[1, rows, :]
                for p in range(RET_HEADS // 2):
                    qp = _rope(q_ref[rows, p * LANES:(p + 1) * LANES].astype(F32), cos, sin, 32)
                    kp = _rope(k_ref[rows, p * LANES:(p + 1) * LANES].astype(F32), cos, sin, 32) * K_SCALE
                    pair = dict(p=p, rows=rows, cos=cos, sin=sin, dq_ref=dq_ref, dk_ref=dk_ref, us=[])
                    pairs.append(pair)
                    for a in range(2):
                        h = 2 * p + a
                        r = d * RET_HEADS + h
                        hm = _half_mask(qp.shape, a)
                        zero = jnp.zeros((RET_DK, RET_DV), F32)
                        sp = s_ref[j, h]
                        u = dict(r=r, h=h, rows=rows, dv_ref=dv_ref,
                                 qh=jnp.where(hm, qp, 0.0), kh=jnp.where(hm, kp, 0.0),
                                 vh=v_ref[rows, h * RET_DV:(h + 1) * RET_DV],
                                 gh=g_ref[rows, h * RET_DV:(h + 1) * RET_DV],
                                 sp=jnp.concatenate([sp, zero] if a == 0 else [zero, sp], axis=0),
                                 dsn=ds[r])
                        u["qhb"], u["khb"] = u["qh"].astype(BF16), u["kh"].astype(BF16)
                        units.append(u)
                        pair["us"].append(u)
            for u in units:
                u["am"] = _dot(u["qhb"], u["khb"], NT)
                u["dar"] = _dot(u["gh"], u["vh"], NT)
                u["xq"] = _dot(u["gh"], u["sp"].astype(BF16), NT)
                u["yk"] = _dot(u["vh"], u["dsn"].astype(BF16), NT)
            for u in units:
                r = u["r"]
                dm = t_ref[r, 0]
                u["da"] = (u["dar"] * dm).astype(BF16)
                u["amd"] = (u["am"] * dm).astype(BF16)
                part = (jnp.sum(u["am"] * u["dar"] * t_ref[r, 1]) + jnp.sum(u["qh"] * t_ref[r, 3] * u["xq"])
                        + jnp.sum(u["kh"] * t_ref[r, 5] * u["yk"])
                        + float(CHUNK) * jnp.sum(t_ref[r, 6] * u["dsn"] * u["sp"]))
                dlg_ref[r:r + 1, :] += jnp.zeros((1, LANES), F32) + part
            for u in units:
                r, h = u["r"], u["h"]
                u["dq"] = _dot(u["da"], u["khb"], NN) + u["xq"] * t_ref[r, 2]
                u["dk"] = _dot(u["da"], u["qhb"], TN) + u["yk"] * t_ref[r, 4]
                u["dv_ref"][u["rows"], h * RET_DV:(h + 1) * RET_DV] = (
                    _dot(u["amd"], u["gh"], TN)
                    + _dot((u["kh"] * t_ref[r, 4]).astype(BF16), u["dsn"].astype(BF16), NN)
                ).astype(u["dv_ref"].dtype)
                ds[r] = t_ref[r, 6] * u["dsn"] + _dot((u["qh"] * t_ref[r, 2]).astype(BF16), u["gh"], TN)
            for pair in pairs:
                sl = slice(pair["p"] * LANES, (pair["p"] + 1) * LANES)
                u0, u1 = pair["us"]
                pair["dq_ref"][pair["rows"], sl] = _rope_t(
                    u0["dq"] + u1["dq"], pair["cos"], pair["sin"], 32).astype(BF16)
                pair["dk_ref"][pair["rows"], sl] = _rope_t(
                    (u0["dk"] + u1["dk"]) * K_SCALE, pair["cos"], pair["sin"], 32).astype(BF16)

        @pl.when(s == nb - 1)
        def _():
            ds0_ref[...] = ds[...]

    fw = lambda s: nb - 1 - s
    bw = lambda s: s
    RB = RET_SUB * CHUNK

    def specs(cm):
        return [pl.BlockSpec((RB, 512), lambda s: (cm(s), C_RQ // 512)),
                pl.BlockSpec((RB, 512), lambda s: (cm(s), C_RK // 512)),
                pl.BlockSpec((RB, 1024), lambda s: (cm(s), C_RV // 1024)),
                pl.BlockSpec((2, RB, LANES), lambda s: (0, cm(s), 0)),
                pl.BlockSpec((RB, 1024), lambda s: (cm(s), 0)),
                pl.BlockSpec((RET_SUB, RET_HEADS, RET_DK, RET_DV), lambda s: (cm(s), 0, 0, 0))]

    def ospecs(cm):
        return [pl.BlockSpec((RB, 512), lambda s: (cm(s), 0)), pl.BlockSpec((RB, 512), lambda s: (cm(s), 0)),
                pl.BlockSpec((RB, 1024), lambda s: (cm(s), 0))]

    oshape = [jax.ShapeDtypeStruct((L, 512), BF16), jax.ShapeDtypeStruct((L, 512), BF16),
              jax.ShapeDtypeStruct((L, 1024), BF16)]
    full = lambda shp: pl.BlockSpec(shp, lambda s: (0,) * len(shp))
    return _call(
        body, "ret_bwd", (nb,),
        specs(fw) + specs(bw) + [full((2 * RET_HEADS, N_TAB, CHUNK, CHUNK))],
        ospecs(fw) + ospecs(bw) + [full((2 * RET_HEADS, LANES, RET_DV)), full((2 * RET_HEADS, LANES))],
        oshape + oshape + [jax.ShapeDtypeStruct((2 * RET_HEADS, LANES, RET_DV), F32),
                           jax.ShapeDtypeStruct((2 * RET_HEADS, LANES), F32)],
        [pltpu.VMEM((2 * RET_HEADS, LANES, RET_DV), F32)],
        ("arbitrary",), (P, P, P, rope, dO, stf, P, P, P, rope, dO, stb, tabs), comm)


def _ret_ctx_bwd(P, ctab, ds0, dlg, rdb, L, Lc):
    cb = L // Lc

    def body(k_ref, v_ref, c_ref, ds_ref, dlg_ref, rd_ref, dk_ref, dv_ref, drd_ref):
        for p in range(RET_HEADS // 2):
            kp = k_ref[:, p * LANES:(p + 1) * LANES].astype(F32) * K_SCALE
            dkp = jnp.zeros((Lc, LANES), F32)
            for a in range(2):
                h = 2 * p + a
                kh = jnp.where(_half_mask(kp.shape, a), kp, 0.0)
                vh = v_ref[:, h * RET_DV:(h + 1) * RET_DV]
                dvh = jnp.zeros((Lc, RET_DV), F32)
                for d in range(2):
                    r = d * RET_HEADS + h
                    dsb = ds_ref[r].astype(BF16)
                    cw, cwc = c_ref[r, 0], c_ref[r, 1]
                    y = _dot(vh, dsb, NT)
                    dkp += y * cw
                    dvh += _dot((kh * cw).astype(BF16), dsb, NN)
                    lg = -jnp.exp(rd_ref[r])
                    drd_ref[r:r + 1, :] = (dlg_ref[r:r + 1, :] + jnp.sum(kh * cwc * y)) * lg
                dv_ref[:, h * RET_DV:(h + 1) * RET_DV] = dvh
            dk_ref[:, p * LANES:(p + 1) * LANES] = dkp * K_SCALE

    full = lambda shp: pl.BlockSpec(shp, lambda i: (0,) * len(shp))
    return pl.pallas_call(
        body, name="ret_ctx_bwd", grid=(1,),
        in_specs=[pl.BlockSpec((Lc, 512), lambda i: (cb, C_RK // 512)),
                  pl.BlockSpec((Lc, 1024), lambda i: (cb, C_RV // 1024)),
                  full((2 * RET_HEADS, 2, Lc, LANES)), full((2 * RET_HEADS, LANES, RET_DV)),
                  full((2 * RET_HEADS, LANES)), full((2 * RET_HEADS, 1, LANES))],
        out_specs=[full((Lc, 512)), full((Lc, 1024)), full((2 * RET_HEADS, LANES))],
        out_shape=[jax.ShapeDtypeStruct((Lc, 512), F32), jax.ShapeDtypeStruct((Lc, 1024), F32),
                   jax.ShapeDtypeStruct((2 * RET_HEADS, LANES), F32)],
        compiler_params=_cparams(("arbitrary",)),
    )(P, P, ctab, ds0, dlg, rdb)


BLK = 128
N_LOC = 3 * BLK


ATT_SUB = 4


def _att_inputs(P, rope, L, Lc):
    n = L // BLK
    cb = L // Lc
    prev = lambda i: jnp.maximum(ATT_SUB * i - 1, 0)
    nxt = lambda i: jnp.minimum(ATT_SUB * i + ATT_SUB, n - 1)
    specs = [pl.BlockSpec((ATT_SUB * BLK, 1024), lambda i: (i, C_AQ // 1024))]
    args = [P]
    for col in (C_AK // 256, C_AV // 256):
        specs += [pl.BlockSpec((BLK, 256), functools.partial(lambda i, col: (prev(i), col), col=col)),
                  pl.BlockSpec((ATT_SUB * BLK, 256), functools.partial(lambda i, col: (i, col), col=col)),
                  pl.BlockSpec((BLK, 256), functools.partial(lambda i, col: (nxt(i), col), col=col)),
                  pl.BlockSpec((Lc, 256), functools.partial(lambda i, col: (cb, col), col=col))]
        args += [P] * 4
    specs += [pl.BlockSpec((2, BLK, LANES), lambda i: (0, prev(i), 0)),
              pl.BlockSpec((2, ATT_SUB * BLK, LANES), lambda i: (0, i, 0)),
              pl.BlockSpec((2, BLK, LANES), lambda i: (0, nxt(i), 0))]
    args += [rope] * 3
    return specs, args


def _att_prep(i, n, refs, Lc):
    q_ref, kp_ref, kc_ref, kn_ref, kx_ref, vp_ref, vc_ref, vn_ref, vx_ref, rp_ref, rc_ref, rn_ref = refs
    cos = jnp.concatenate([rp_ref[0], rc_ref[0], rn_ref[0]], axis=0)
    sin = jnp.concatenate([rp_ref[1], rc_ref[1], rn_ref[1]], axis=0)

    def dup(x):
        xr = pltpu.roll(x, 64, 1)
        return [jnp.where(_half_mask(x.shape, b), x, xr).astype(BF16) for b in range(2)]

    kd = [[] for _ in range(ATT_SUB)]
    vd = [[] for _ in range(ATT_SUB)]
    for t in range(ATT_KV // 2):
        sl = slice(t * LANES, (t + 1) * LANES)
        kl = jnp.concatenate([kp_ref[:, sl], kc_ref[:, sl], kn_ref[:, sl]], axis=0).astype(F32)
        kl = dup(_rope(kl, cos, sin, 16))
        vl = dup(jnp.concatenate([vp_ref[:, sl], vc_ref[:, sl], vn_ref[:, sl]], axis=0).astype(F32))
        kx, vx = dup(kx_ref[:, sl].astype(F32)), dup(vx_ref[:, sl].astype(F32))
        for j in range(ATT_SUB):
            rows = slice(j * BLK, j * BLK + N_LOC)
            for b in range(2):
                kd[j].append(jnp.concatenate([kl[b][rows], kx[b]], axis=0))
                vd[j].append(jnp.concatenate([vl[b][rows], vx[b]], axis=0))
    nk = N_LOC + Lc
    rr = lax.broadcasted_iota(jnp.int32, (BLK, nk), 0)
    ss = lax.broadcasted_iota(jnp.int32, (BLK, nk), 1)
    band = (ss >= rr) & (ss <= rr + 2 * BLK)
    bias4, tabs = [], []
    for j in range(ATT_SUB):
        blk = ATT_SUB * i + j
        lo = jnp.where(blk == 0, BLK, 0)
        hi = jnp.where(blk == n - 1, 2 * BLK, N_LOC)
        bias = jnp.where((ss >= N_LOC) | (band & (ss >= lo) & (ss < hi)), 0.0, NEG)
        bias4.append(jnp.concatenate([bias] * 4, axis=0))
        tabs.append((rc_ref[0, j * BLK:(j + 1) * BLK, :], rc_ref[1, j * BLK:(j + 1) * BLK, :]))
    return kd, vd, bias4, tabs


LOG2E = 1.4426950408889634
LN2 = 0.6931471805599453
Q_SCALE = A_SCALE * LOG2E


def _stack4(ref, rows, g, f=None):
    parts = []
    for jp in range(2):
        t = ref[rows, (2 * g + jp) * LANES:(2 * g + jp + 1) * LANES].astype(F32)
        if f is not None:
            t = f(t)
        for a in range(2):
            parts.append(jnp.where(_half_mask(t.shape, a), t, 0.0))
    return jnp.concatenate(parts, axis=0)


def _unstack4(x4, jp):
    r0 = 2 * jp * BLK
    lo = x4[r0:r0 + BLK]
    hi = x4[r0 + BLK:r0 + 2 * BLK]
    return jnp.where(_half_mask(lo.shape, 0), lo, hi)


def _softmax_parts(s, bias4, sink_ref, g):
    sink_col = LOG2E * jnp.concatenate(
        [jnp.zeros((BLK, 1), F32) + sink_ref[4 * g + r:4 * g + r + 1, 0:1] for r in range(4)], axis=0)
    s = s + bias4
    m = jnp.maximum(jnp.max(s, axis=-1, keepdims=True), sink_col)
    e = jnp.exp2(s - m)
    es = jnp.exp2(sink_col - m)
    return e, es, jnp.sum(e, axis=-1, keepdims=True) + es


def _att_fwd(P, rope, sinkb, Y, L, Lc, comm=()):
    n = L // BLK
    specs, args = _att_inputs(P, rope, L, Lc)

    def body(*refs):
        sink_ref, o_ref = refs[12], refs[14]
        i = pl.program_id(0)
        kd, vd, bias4, tabs = _att_prep(i, n, refs[:12], Lc)
        for j in range(ATT_SUB):
            rows = slice(j * BLK, (j + 1) * BLK)
            cq, sq = tabs[j]
            for g in range(ATT_KV):
                q4 = _stack4(refs[0], rows, g, lambda t: _rope(t, cq, sq, 16) * Q_SCALE).astype(BF16)
                e, _, l = _softmax_parts(_dot(q4, kd[j][g], NT), bias4[j], sink_ref, g)
                o4 = _dot(e.astype(BF16), vd[j][g], NN) * (1.0 / l)
                for jp in range(2):
                    c0 = (2 * g + jp) * LANES
                    o_ref[rows, c0:c0 + LANES] = _unstack4(o4, jp).astype(o_ref.dtype)

    return _call(
        body, "att_fwd", (n // ATT_SUB,),
        specs + [pl.BlockSpec((ATT_HEADS, LANES), lambda i: (0, 0)), pl.BlockSpec(memory_space=pl.ANY)],
        [pl.BlockSpec((ATT_SUB * BLK, 1024), lambda i: (i, 1))], [jax.ShapeDtypeStruct((L, 2048), BF16)], [],
        ("parallel",), (*args, sinkb, Y), comm, aliases={13: 0})


def _att_bwd(P, rope, sinkb, Y, dY, L, Lc, comm=()):
    n = L // BLK
    specs, args = _att_inputs(P, rope, L, Lc)
    nk = N_LOC + Lc

    def body(*refs):
        sink_ref, y_ref, dy_ref = refs[12], refs[13], refs[14]
        dq_ref, dkl_ref, dvl_ref, dkx_ref, dvx_ref, dsk_ref = refs[15:21]
        i = pl.program_id(0)

        @pl.when(i == 0)
        def _():
            dkx_ref[...] = jnp.zeros_like(dkx_ref)
            dvx_ref[...] = jnp.zeros_like(dvx_ref)
            dsk_ref[...] = jnp.zeros_like(dsk_ref)

        kd, vd, bias4, tabs = _att_prep(i, n, refs[:12], Lc)
        for j in range(ATT_SUB):
            rows = slice(j * BLK, (j + 1) * BLK)
            cq, sq = tabs[j]
            for t in range(ATT_KV // 2):
                dk_halves, dv_halves = [], []
                for b in range(2):
                    g = 2 * t + b
                    q4 = _stack4(refs[0], rows, g, lambda x: _rope(x, cq, sq, 16) * Q_SCALE).astype(BF16)
                    do4 = _stack4(dy_ref, rows, g)
                    delta = jnp.sum(do4 * _stack4(y_ref, rows, g), axis=-1, keepdims=True)
                    do4b = do4.astype(BF16)
                    e, es, l = _softmax_parts(_dot(q4, kd[j][g], NT), bias4[j], sink_ref, g)
                    inv = 1.0 / l
                    p = e * inv
                    dsc = (p * (_dot(do4b, vd[j][g], NT) - delta)).astype(BF16)
                    dsr = es * inv * delta
                    for r in range(4):
                        h = 4 * g + r
                        dsk_ref[h:h + 1, :] += jnp.zeros((1, LANES), F32) - jnp.sum(dsr[r * BLK:(r + 1) * BLK])
                    dq4 = _dot(dsc, kd[j][g], NN) * A_SCALE
                    for jp in range(2):
                        c0 = (2 * g + jp) * LANES
                        dq_ref[rows, c0:c0 + LANES] = _rope_t(_unstack4(dq4, jp), cq, sq, 16).astype(dq_ref.dtype)
                    dkd = _dot(q4, dsc, TN) * LN2
                    dvd = _dot(do4b, p.astype(BF16), TN)
                    dk_halves.append(dkd[:ATT_DH] + dkd[ATT_DH:])
                    dv_halves.append(dvd[:ATT_DH] + dvd[ATT_DH:])
                dk_t = jnp.concatenate(dk_halves, axis=0).T
                dv_t = jnp.concatenate(dv_halves, axis=0).T
                sl = slice(t * LANES, (t + 1) * LANES)
                dkl_ref[j, :, sl] = dk_t[:N_LOC]
                dvl_ref[j, :, sl] = dv_t[:N_LOC]
                dkx_ref[:, sl] += dk_t[N_LOC:]
                dvx_ref[:, sl] += dv_t[N_LOC:]

    row = pl.BlockSpec((ATT_SUB * BLK, 1024), lambda i: (i, 0))
    loc = pl.BlockSpec((ATT_SUB, N_LOC, 256), lambda i: (i, 0, 0))
    cx = pl.BlockSpec((Lc, 256), lambda i: (0, 0))
    return _call(
        body, "att_bwd", (n // ATT_SUB,),
        specs + [pl.BlockSpec((ATT_HEADS, LANES), lambda i: (0, 0))]
        + [pl.BlockSpec((ATT_SUB * BLK, 1024), lambda i: (i, 1))] * 2,
        [row, loc, loc, cx, cx, pl.BlockSpec((ATT_HEADS, LANES), lambda i: (0, 0))],
        [jax.ShapeDtypeStruct((L, 1024), BF16), jax.ShapeDtypeStruct((n, N_LOC, 256), F32),
         jax.ShapeDtypeStruct((n, N_LOC, 256), F32), jax.ShapeDtypeStruct((Lc, 256), F32),
         jax.ShapeDtypeStruct((Lc, 256), F32), jax.ShapeDtypeStruct((ATT_HEADS, LANES), F32)], [],
        ("arbitrary",), (*args, sinkb, Y, dY), comm)


def _assemble_dp(L, Lc, dqf, dqb, dkf, dkb, dvf, dvb, drg, daq, dkl, dvl, rope_att, dck, dcv, dkx, dvx):
    n = L // BLK
    nc = Lc // BLK

    def body(dqf_r, dqb_r, dkf_r, dkb_r, dvf_r, dvb_r, drg_r, daq_r, kl0, kl1, kl2, vl0, vl1, vl2, rp_r,
             dck_r, dcv_r, dkx_r, dvx_r, o_ref):
        i = pl.program_id(0)

        @pl.when(i < n)
        def _():
            add = lambda a, b: (a[...].astype(F32) + b[...].astype(F32)).astype(o_ref.dtype)
            o_ref[:, C_RQ:C_RK] = add(dqf_r, dqb_r)
            o_ref[:, C_RK:C_RV] = add(dkf_r, dkb_r)
            o_ref[:, C_RV:C_RG] = add(dvf_r, dvb_r)
            o_ref[:, C_RG:C_AQ] = drg_r[...].astype(o_ref.dtype)
            o_ref[:, C_AQ:C_AK] = daq_r[...].astype(o_ref.dtype)
            w0 = jnp.where(i > 0, 1.0, 0.0)
            w2 = jnp.where(i < n - 1, 1.0, 0.0)
            dk = kl0[0] * w0 + kl1[0] + kl2[0] * w2
            dv = vl0[0] * w0 + vl1[0] + vl2[0] * w2
            for t in range(ATT_KV // 2):
                sl = slice(t * LANES, (t + 1) * LANES)
                o_ref[:, C_AK + t * LANES:C_AK + (t + 1) * LANES] = _rope_t(
                    dk[:, sl], rp_r[0], rp_r[1], 16).astype(o_ref.dtype)
            o_ref[:, C_AV:D_PROJ] = dv.astype(o_ref.dtype)

        @pl.when(i >= n)
        def _():
            o_ref[:, C_RQ:C_RK] = jnp.zeros((BLK, C_RK - C_RQ), o_ref.dtype)
            o_ref[:, C_RK:C_RV] = dck_r[...].astype(o_ref.dtype)
            o_ref[:, C_RV:C_RG] = dcv_r[...].astype(o_ref.dtype)
            o_ref[:, C_RG:C_AK] = jnp.zeros((BLK, C_AK - C_RG), o_ref.dtype)
            o_ref[:, C_AK:C_AV] = dkx_r[...].astype(o_ref.dtype)
            o_ref[:, C_AV:D_PROJ] = dvx_r[...].astype(o_ref.dtype)

    xm = lambda i: jnp.minimum(i, n - 1)
    cm = lambda i: jnp.clip(i - n, 0, nc - 1)
    r512 = pl.BlockSpec((BLK, 512), lambda i: (xm(i), 0))
    r1024 = pl.BlockSpec((BLK, 1024), lambda i: (xm(i), 0))
    part = lambda off: pl.BlockSpec((1, BLK, 256), lambda i: (jnp.clip(xm(i) + off, 0, n - 1), 1 - off, 0))
    return pl.pallas_call(
        body, name="assemble_dp", grid=(n + nc,),
        in_specs=[r512, r512, r512, r512, r1024, r1024, r1024, r1024,
                  part(-1), part(0), part(1), part(-1), part(0), part(1),
                  pl.BlockSpec((2, BLK, LANES), lambda i: (0, xm(i), 0)),
                  pl.BlockSpec((BLK, 512), lambda i: (cm(i), 0)), pl.BlockSpec((BLK, 1024), lambda i: (cm(i), 0)),
                  pl.BlockSpec((BLK, 256), lambda i: (cm(i), 0)), pl.BlockSpec((BLK, 256), lambda i: (cm(i), 0))],
        out_specs=pl.BlockSpec((BLK, D_PROJ), lambda i: (i, 0)),
        out_shape=jax.ShapeDtypeStruct((L + Lc, D_PROJ), BF16),
        compiler_params=_cparams(("parallel",)),
    )(dqf, dqb, dkf, dkb, dvf, dvb, drg, daq, dkl, dkl, dkl, dvl, dvl, dvl, rope_att, dck, dcv, dkx, dvx)


def _adam_math(w, g, m, v):
    m = ADAM_B1 * m + (1.0 - ADAM_B1) * g
    v = ADAM_B2 * v + (1.0 - ADAM_B2) * (g * g)
    m_hat = m / (1.0 - ADAM_B1 ** ADAM_STEP)
    v_hat = v / (1.0 - ADAM_B2 ** ADAM_STEP)
    delta = -ADAM_LR * (m_hat / (jnp.sqrt(v_hat) + ADAM_EPS) + ADAM_WD * w)
    return delta, m, v


def _adam(name, w, m, v, g=None, parts=None):
    R, C = w.shape
    tr = _pick(R, (256, 192, 176, 128, 64, 32, 16, 8))
    summed = parts is not None
    n_parts = parts.shape[0] if summed else 0

    def body(w_ref, m_ref, v_ref, g_ref, go_ref, d_ref, mo_ref, vo_ref):
        if summed:
            gv = g_ref[0].astype(F32)
            for j in range(1, n_parts):
                gv = gv + g_ref[j].astype(F32)
        else:
            gv = g_ref[...]
        d, mn, vn = _adam_math(w_ref[...], gv, m_ref[...], v_ref[...])
        go_ref[...] = gv
        d_ref[...] = d
        mo_ref[...] = mn
        vo_ref[...] = vn

    row = pl.BlockSpec((tr, C), lambda i: (i, 0))
    gspec = pl.BlockSpec((n_parts, tr, C), lambda i: (0, i, 0)) if summed else row
    return pl.pallas_call(
        body, name=name, grid=(R // tr,),
        in_specs=[row, row, row, gspec], out_specs=[row] * 4,
        out_shape=[jax.ShapeDtypeStruct((R, C), F32)] * 4,
        compiler_params=_cparams(("parallel",)),
    )(w, m, v, parts if summed else g)


def _rows_full(g):
    _, R, D = g.shape
    return g.reshape(N_DEV * R, D)


def _rows_slots(g):
    N, D = g.shape
    return g.reshape(N_DEV, N // N_DEV, D)


def _pad_rows(a, rows):
    return jnp.concatenate([a, jnp.zeros((rows - a.shape[0],) + a.shape[1:], a.dtype)], axis=0)


def kernel(x, c, ctx, c_ctx, w_mod, b_mod, norm_mix, norm_ffn, w_in, ret_decay, attn_sink, w_out, w_gate, w_up, w_down, norm_final, loss_target, m_c_ctx, m_w_mod, m_b_mod, m_norm_mix, m_norm_ffn, m_w_in, m_ret_decay, m_attn_sink, m_w_out, m_w_gate, m_w_up, m_w_down, m_norm_final, v_c_ctx, v_w_mod, v_b_mod, v_norm_mix, v_norm_ffn, v_w_in, v_ret_decay, v_attn_sink, v_w_out, v_w_gate, v_w_up, v_w_down, v_norm_final):
    L, D = x.shape[1], x.shape[2]
    Lc = ctx.shape[1]
    DF = w_gate.shape[2] * N_DEV
    C6 = w_mod.shape[2]
    me = _my_id()
    xs, cx, tgt = x[0], ctx[0], loss_target[0]

    ag_in = ("ag2", w_in[0].T.astype(BF16))
    ag_out, ag_gate = ("ag2", w_out[0].astype(BF16)), ("ag2", w_gate[0].T.astype(BF16))
    ag_up, ag_down = ("ag2", w_up[0].T.astype(BF16)), ("ag2", w_down[0].astype(BF16))

    cs = _allgather(c, "ag_c")[:, 0, :]
    s_in = _pad_rows(jnp.concatenate([cs, c_ctx[None, :]], axis=0), 16)
    b_l = lax.dynamic_slice_in_dim(b_mod, me * C6, C6, axis=1)
    mod_parts = _allgather(_mod_fwd(s_in, w_mod[0], b_l), "ag_mod")
    mod = _pad_rows(lax.dynamic_index_in_dim(mod_parts, me, axis=1, keepdims=False).reshape(6, D), 8)
    modc = _pad_rows(mod_parts[:, N_DEV, :].reshape(6, D), 8)
    mix_mod, ffn_mod = mod, jnp.roll(mod, -3, axis=0)
    gt_m, gt_f = mod[2:3], mod[5:6]

    rope_ret, rope_att = _rope_tables(L)
    rdb = jnp.broadcast_to(ret_decay[0].reshape(2 * RET_HEADS, 1, 1), (2 * RET_HEADS, 1, LANES))
    sinkb = jnp.broadcast_to(attn_sink[0].reshape(ATT_HEADS, 1), (ATT_HEADS, LANES))

    tm = _pick(L + Lc, (1408, 768, 512, 384, 256, 128))
    tmx = _pick(L, (1024, 512, 256, 128))

    (H,), (g_in,) = _modulate_fwd("mod_mix_fwd", xs, cx, norm_mix, mix_mod, modc, comm=[ag_in])
    W_inT = _rows_full(g_in)
    ident = lambda a, e: a
    tP, tD, tF = _pick(D_PROJ, (1152, 768, 512)), _pick(D, (2048, 1024, 512)), _pick(DF, (512, 256, 128))
    (P,), (g_gate,) = _matmul("mm_in", [(H, W_inT, 0)], 1, L + Lc, D_PROJ, D, "nt",
                              (tm, _pick(D_PROJ, (1536, 768, 512)), D), [], [BF16], ident,
                              comm=[ag_gate])
    W_gateT = _rows_full(g_gate)
    tabs, ctab = _ret_tables(rdb, Lc)
    s0 = _ret_ctx_state(P, ctab, L, Lc)
    (o_f, o_b, st_f, st_b), (g_out,) = _ret_fwd(P, rope_ret, tabs, s0, L, comm=[ag_out])
    W_out = _rows_full(g_out)
    Y_half = _ret_finish_fwd(o_f, o_b, P, L)
    (Y,), (g_up,) = _att_fwd(P, rope_att, sinkb, Y_half, L, Lc, comm=[ag_up])
    W_upT = _rows_full(g_up)
    KO = Y.shape[1]
    f_mix = _matmul("mm_out", [(Y, W_out, 0)], 1, L, D, KO, "nn", (tmx, tD, KO), [], [BF16], ident)[0]

    x1, H2 = _residual_modulate_fwd("mod_ffn_fwd", xs, f_mix, gt_m, norm_ffn, ffn_mod)

    def swiglu_epi(a, e):
        sg = _sigmoid(a[0])
        act = a[0] * sg
        return [act, a[1] * (sg * (1.0 + a[0] * (1.0 - sg))), act * a[1]]

    tm2 = tmx
    (act, up_dact, hmid), (g_down,) = _matmul("mm_gate_up", [(H2, W_gateT, 0), (H2, W_upT, 1)], 2, L, DF, D, "nt",
                                              (_pick(L, (2048, 1024, 512, 256, 128)), tF, D), [],
                                              [BF16, BF16, BF16], swiglu_epi, comm=[ag_down])
    W_down = _rows_full(g_down)
    f_ffn = _matmul("mm_down", [(hmid, W_down, 0)], 1, L, D, DF, "nn",
                    (tm2, _pick(D, (1024, 512)), _pick(DF, (2816, 512, 256, 128))), [], [BF16], ident)[0]

    dx2, dFf, sums_l = _loss_head(x1, tgt, norm_final.reshape(1, D), f_ffn, gt_f)

    def dswiglu_epi(a, e):
        return [a[0] * e[0].astype(F32), a[0] * e[1].astype(F32)]

    dga, dup = _matmul("mm_d_down", [(dFf, W_down, 0)], 1, L, DF, D, "nt", (_pick(L, (2048, 1024, 512, 256, 128)), tF, D),
                       [(up_dact, "mn"), (act, "mn")], [BF16, BF16], dswiglu_epi)
    tkt, tkl = _pick(L, (512, 256, 128)), _pick(L, (1024, 512, 256, 128))
    dW_down = _matmul("mm_gw_down", [(hmid, dFf, 0)], 1, DF, D, L, "tn",
                      (_pick(DF, (1408, 512, 256, 128)), tD, tkl), [], [BF16], ident)[0]
    (dW_gateT, dW_upT), (p_down,) = _matmul("mm_gw_gate_up", [(dga, H2, 0), (dup, H2, 1)], 2, DF, D, L, "tn",
                                            (tF, tD, _pick(L, (2048, 1024, 512, 256, 128))), [], [BF16, BF16], ident,
                                            comm=[("a2a", _rows_slots(dW_down))])
    (dH2,), (p_gate,) = _matmul("mm_d_gate_up", [(dga, W_gateT, 0), (dup, W_upT, 0)], 1, L, D, DF, "nn",
                                (_pick(L, (2048, 1024, 512, 256, 128)), tD, tF), [], [BF16], ident,
                                comm=[("a2a", _rows_slots(dW_gateT))])
    dx1, dFm, sums_f = _modulate_bwd("mod_ffn_bwd", x1, None, dH2, norm_ffn, ffn_mod, None, dx2, f_mix, gt_m)

    tO = _pick(KO, (2048, 1024, 512))
    dY = _matmul("mm_d_out", [(dFm, W_out, 0)], 1, L, KO, D, "nt", (tmx, tO, D), [], [BF16], ident)[0]
    dW_out = _matmul("mm_gw_out", [(Y, dFm, 0)], 1, KO, D, L, "tn",
                     (_pick(KO, (1024, 512)), tD, _pick(L, (2048, 1024, 512, 256, 128))), [], [BF16],
                     ident)[0]
    dO, drg = _ret_finish_bwd(o_f, o_b, P, dY, L)
    (dqf, dkf, dvf, dqb, dkb, dvb, ds0, dlg), (p_out,) = _ret_bwd(
        P, rope_ret, tabs, st_f, st_b, dO, L, comm=[("a2a", _rows_slots(dW_out))])
    dck, dcv, d_rd = _ret_ctx_bwd(P, ctab, ds0, dlg, rdb, L, Lc)
    (daq, dkl, dvl, dkx, dvx, d_sink), (p_up,) = _att_bwd(
        P, rope_att, sinkb, Y, dY, L, Lc, comm=[("a2a", _rows_slots(dW_upT))])
    dP = _assemble_dp(L, Lc, dqf, dqb, dkf, dkb, dvf, dvb, drg, daq, dkl, dvl, rope_att, dck, dcv, dkx, dvx)
    tkc = _pick(L + Lc, (768, 256, 128))
    dW_inT = _matmul("mm_gw_in", [(dP, H, 0)], 1, D_PROJ, D, L + Lc, "tn",
                     (_pick(D_PROJ, (2304, 1152, 768, 512)), tD, tkc), [], [BF16], ident)[0]
    (dH,), (p_in,) = _matmul("mm_d_in", [(dP, W_inT, 0)], 1, L + Lc, D, D_PROJ, "nn",
                             (tm, tD, _pick(D_PROJ, (768, 512, 256))), [], [BF16], ident,
                             comm=[("a2a", _rows_slots(dW_inT))])
    grad_x, sums_m = _modulate_bwd("mod_mix_bwd", xs, cx, dH, norm_mix, mix_mod, modc, dx1, None, None)

    zero = jnp.zeros((1, D), F32)
    dmod = jnp.concatenate([sums_m[0:1], sums_m[1:2], sums_f[6:7], sums_f[0:1], sums_f[1:2], sums_l[2:3]], axis=1)
    dmodc = jnp.concatenate([sums_m[3:4], sums_m[4:5], zero, zero, zero, zero], axis=1)
    dm_all = _allgather(jnp.concatenate([dmod, dmodc], axis=0), "ag_dmod")
    dm_cols = lax.dynamic_slice_in_dim(dm_all, me * C6, C6, axis=2)
    dm_in = jnp.concatenate([dm_cols[:, 0, :], dm_cols[:, 1, :]], axis=0)
    s_bwd = jnp.concatenate([cs, jnp.broadcast_to(c_ctx[None, :], (N_DEV, D))], axis=0)
    g_w_mod, dsil = _mod_bwd(s_bwd, dm_in, w_mod[0])

    lane_pad = lambda a: _pad_rows(a.reshape(-1, 1), LANES).reshape(1, LANES)
    pack = jnp.concatenate([dsil[0:1], sums_m[2:3], sums_f[2:3], sums_l[1:2],
                            lane_pad(d_rd[:, 0]), lane_pad(d_sink[:, 0]), sums_l[3:4, 0:LANES]], axis=1)
    packs = _allgather(pack, "ag_small")
    zl = jnp.zeros((1, LANES), F32)

    def pack_w(a_c, a_nm, a_nf, a_fin, a_rd, a_sk):
        return jnp.concatenate([a_c.reshape(1, D), a_nm, a_nf, a_fin.reshape(1, D), lane_pad(a_rd.reshape(-1)),
                                lane_pad(a_sk.reshape(-1)), zl], axis=1)

    sg, sd, sm, sv = _adam("adam_small", pack_w(c_ctx, norm_mix, norm_ffn, norm_final, ret_decay, attn_sink),
                           pack_w(m_c_ctx, m_norm_mix, m_norm_ffn, m_norm_final, m_ret_decay, m_attn_sink),
                           pack_w(v_c_ctx, v_norm_mix, v_norm_ffn, v_norm_final, v_ret_decay, v_attn_sink),
                           parts=packs)
    loss = sg[0, 4 * D + 2 * LANES]

    def unpack(a):
        return (a[0, 0:D], a[:, D:2 * D], a[:, 2 * D:3 * D], a[0, 3 * D:4 * D],
                a[0, 4 * D:4 * D + 2 * RET_HEADS].reshape(1, 2, RET_HEADS),
                a[:, 4 * D + LANES:4 * D + LANES + ATT_HEADS])

    bg, bd, bm, bv = _adam("adam_b_mod", b_mod, m_b_mod, v_b_mod, parts=dm_all.reshape(2 * N_DEV, 1, 6 * D))
    wg, wd, wm, wv = _adam("adam_w_mod", w_mod[0], m_w_mod[0], v_w_mod[0], g=g_w_mod)

    big = {}
    for nm, w, m, v, parts, transposed in (
            ("w_in", w_in, m_w_in, v_w_in, p_in, True), ("w_out", w_out, m_w_out, v_w_out, p_out, False),
            ("w_gate", w_gate, m_w_gate, v_w_gate, p_gate, True), ("w_up", w_up, m_w_up, v_w_up, p_up, True),
            ("w_down", w_down, m_w_down, v_w_down, p_down, False)):
        if transposed:
            res = [a.T for a in _adam("adam_" + nm, w[0].T, m[0].T, v[0].T, parts=parts)]
        else:
            res = _adam("adam_" + nm, w[0], m[0], v[0], parts=parts)
        big[nm] = [a[None] for a in res]

    g_s, d_s, m_s, v_s = unpack(sg), unpack(sd), unpack(sm), unpack(sv)

    def leaves(k, small, bmod, wmod):
        return (small[0], wmod[None], bmod, small[1], small[2], big["w_in"][k], small[4], small[5],
                big["w_out"][k], big["w_gate"][k], big["w_up"][k], big["w_down"][k], small[3])

    return (loss, grad_x[None], *leaves(0, g_s, bg, wg), *leaves(1, d_s, bd, wd),
            *leaves(2, m_s, bm, wm), *leaves(3, v_s, bv, wv))
```

```python
import functools

import jax
import jax.numpy as jnp
from jax import lax
from jax.experimental import pallas as pl
from jax.experimental.pallas import tpu as pltpu

F32 = jnp.float32
BF16 = jnp.bfloat16

N_DEV = 8
LANES = 128
RET_HEADS = 8
RET_DK = 64
RET_DV = 128
CHUNK = 128
ATT_HEADS = 16
ATT_KV = 4
ATT_DH = 64
GRID_W = 64
ROPE_BASE = 10000.0
EPS = 1e-6
NEG = -1e30
C_RQ, C_RK, C_RV, C_RG, C_AQ, C_AK, C_AV, D_PROJ = 0, 512, 1024, 2048, 3072, 4096, 4352, 4608
K_SCALE = RET_DK ** -0.5
A_SCALE = ATT_DH ** -0.5

ADAM_LR, ADAM_B1, ADAM_B2, ADAM_EPS, ADAM_WD, ADAM_STEP = 0.001, 0.9, 0.999, 1e-08, 0.01, 10

VMEM_BIG = 58 * 1024 * 1024

NN = (((1,), (0,)), ((), ()))
NT = (((1,), (1,)), ((), ()))
TN = (((0,), (0,)), ((), ()))


def _dot(a, b, dims):
    return lax.dot_general(a, b, dims, preferred_element_type=F32)


def _cparams(sem, vmem=VMEM_BIG):
    return pltpu.CompilerParams(dimension_semantics=sem, vmem_limit_bytes=vmem)


def _pick(dim, prefs):
    for p in prefs:
        if dim % p == 0:
            return p
    return dim


def _my_id():
    return lax.axis_index("x") * 4 + lax.axis_index("y") * 2 + lax.axis_index("c")


def _sigmoid(x):
    return 0.5 * jnp.tanh(0.5 * x) + 0.5


def _peers():
    mx, my, mc = lax.axis_index("x"), lax.axis_index("y"), lax.axis_index("c")
    out = []
    for k in range(1, N_DEV):
        kx, ky, kc = (k >> 2) & 1, (k >> 1) & 1, k & 1
        px = 1 - mx if kx else mx
        py = 1 - my if ky else my
        pc = 1 - mc if kc else mc
        out.append(((px, py, pc), px * 4 + py * 2 + pc))
    return out


def _exchange_copies(kind, x_ref, o_ref, ssem, rsem, lsem):
    me = _my_id()
    loc = pltpu.make_async_copy(x_ref if kind == "ag" else x_ref.at[me], o_ref.at[me], lsem)
    cps = []
    for k, (peer, pid) in enumerate(_peers()):
        cps.append(pltpu.make_async_remote_copy(
            src_ref=x_ref if kind == "ag" else x_ref.at[pid], dst_ref=o_ref.at[me],
            send_sem=ssem.at[k], recv_sem=rsem.at[k], device_id=peer, device_id_type=pl.DeviceIdType.MESH))
    return loc, cps


def _two_level_copies(x_ref, o_ref, ssem, rsem, lsem):
    mx, my, mc = lax.axis_index("x"), lax.axis_index("y"), lax.axis_index("c")
    me = mx * 4 + my * 2 + mc
    sibling = (mx, my, 1 - mc)
    chips = [(1 - mx, my), (mx, 1 - my), (1 - mx, 1 - my)]

    def copy(k, slot, to, src=None):
        return pltpu.make_async_remote_copy(
            src_ref=o_ref.at[slot] if src is None else src, dst_ref=o_ref.at[slot],
            send_sem=ssem.at[k], recv_sem=rsem.at[k], device_id=to, device_id_type=pl.DeviceIdType.MESH)

    loc = pltpu.make_async_copy(x_ref, o_ref.at[me], lsem)
    first = [copy(0, me, sibling, src=x_ref)]
    first += [copy(1 + j, me, (cx, cy, mc), src=x_ref) for j, (cx, cy) in enumerate(chips)]
    passed = [copy(4 + j, cx * 4 + cy * 2 + mc, sibling) for j, (cx, cy) in enumerate(chips)]
    return loc, first, passed


def _exchange_start(kind, x_ref, o_ref, ssem, rsem, lsem):
    if kind == "ag2":
        loc, first, _ = _two_level_copies(x_ref, o_ref, ssem, rsem, lsem)
        cps = first
    else:
        loc, cps = _exchange_copies(kind, x_ref, o_ref, ssem, rsem, lsem)
    loc.start()
    for cp in cps:
        cp.start()


def _exchange_pass_on(kind, x_ref, o_ref, ssem, rsem, lsem):
    if kind == "ag2":
        _, first, passed = _two_level_copies(x_ref, o_ref, ssem, rsem, lsem)
        for j in range(3):
            first[1 + j].wait_recv()
            passed[j].start()


def _exchange_wait(kind, x_ref, o_ref, ssem, rsem, lsem):
    if kind == "ag2":
        loc, first, passed = _two_level_copies(x_ref, o_ref, ssem, rsem, lsem)
        first[0].wait_recv()
        for cp in passed:
            cp.wait_recv()
        cps = first + passed
    else:
        loc, cps = _exchange_copies(kind, x_ref, o_ref, ssem, rsem, lsem)
        for cp in cps:
            cp.wait_recv()
    for cp in cps:
        cp.wait_send()
    loc.wait()


_EXCHANGE_SEMS = [pltpu.SemaphoreType.DMA((N_DEV - 1,)), pltpu.SemaphoreType.DMA((N_DEV - 1,)),
                  pltpu.SemaphoreType.DMA(())]


def _exchange_shape(kind, x):
    return jax.ShapeDtypeStruct(x.shape if kind == "a2a" else (N_DEV,) + x.shape, x.dtype)


def _exchange(kind, x, name):
    def body(x_ref, o_ref, ssem, rsem, lsem):
        _exchange_start(kind, x_ref, o_ref, ssem, rsem, lsem)
        _exchange_pass_on(kind, x_ref, o_ref, ssem, rsem, lsem)
        _exchange_wait(kind, x_ref, o_ref, ssem, rsem, lsem)

    return pl.pallas_call(
        body, name=name, out_shape=_exchange_shape(kind, x),
        in_specs=[pl.BlockSpec(memory_space=pl.ANY)], out_specs=pl.BlockSpec(memory_space=pl.ANY),
        scratch_shapes=list(_EXCHANGE_SEMS),
    )(x)


def _allgather(x, name):
    return _exchange("ag", x, name)


def _call(body, name, grid, in_specs, out_specs, out_shape, scratch_shapes, sem, args, comm=(), aliases=None,
          pass_on_at=0.75):
    in_specs, out_specs, out_shape = list(in_specs), list(out_specs), list(out_shape)
    scratch_shapes = list(scratch_shapes)
    aliases = aliases or {}
    if not comm:
        outs = pl.pallas_call(body, name=name, grid=grid, in_specs=in_specs, out_specs=out_specs, out_shape=out_shape,
                              scratch_shapes=scratch_shapes, input_output_aliases=aliases,
                              compiler_params=_cparams(sem))(*args)
        return list(outs), []
    n_in, n_out, n_scr, n_c = len(in_specs), len(out_specs), len(scratch_shapes), len(comm)
    hbm = pl.BlockSpec(memory_space=pl.ANY)

    def wrapped(*refs):
        ins, cins = refs[:n_in], refs[n_in:n_in + n_c]
        outs = refs[n_in + n_c:n_in + n_c + n_out]
        couts = refs[n_in + n_c + n_out:n_in + 2 * n_c + n_out]
        scr = refs[n_in + 2 * n_c + n_out:n_in + 2 * n_c + n_out + n_scr]
        sems = refs[n_in + 2 * n_c + n_out + n_scr:]
        step, total = pl.program_id(0), grid[0]
        for ax in range(1, len(grid)):
            step = step * grid[ax] + pl.program_id(ax)
            total *= grid[ax]

        @pl.when(step == 0)
        def _():
            for c, (kind, _) in enumerate(comm):
                _exchange_start(kind, cins[c], couts[c], *sems[3 * c:3 * c + 3])

        body(*ins, *outs, *scr)

        @pl.when(step == min(total - 1, int(total * pass_on_at)))
        def _():
            for c, (kind, _) in enumerate(comm):
                _exchange_pass_on(kind, cins[c], couts[c], *sems[3 * c:3 * c + 3])

        @pl.when(step == total - 1)
        def _():
            for c, (kind, _) in enumerate(comm):
                _exchange_wait(kind, cins[c], couts[c], *sems[3 * c:3 * c + 3])

    res = pl.pallas_call(
        wrapped, name=name, grid=grid,
        in_specs=in_specs + [hbm] * n_c, out_specs=out_specs + [hbm] * n_c,
        out_shape=out_shape + [_exchange_shape(kind, arr) for kind, arr in comm],
        scratch_shapes=scratch_shapes + list(_EXCHANGE_SEMS) * n_c, input_output_aliases=aliases,
        compiler_params=_cparams(("arbitrary",) * len(grid)),
    )(*args, *[arr for _, arr in comm])
    return list(res[:n_out]), list(res[n_out:])


def _matmul(name, pairs, n_acc, M, N, K, mode, tiles, extras, out_dtypes, epilogue, j_outer=False, comm=(),
            pass_on_at=0.75):
    tm, tn, tk = tiles
    gm, gn, nk = M // tm, N // tn, K // tk
    assert gm * tm == M and gn * tn == N and nk * tk == K, (name, M, N, K, tiles)
    if j_outer:
        grid = (gn, gm, nk)
        ij = lambda g0, g1: (g1, g0)
    else:
        grid = (gm, gn, nk)
        ij = lambda g0, g1: (g0, g1)

    if mode in ("nn", "nt"):
        a_spec = pl.BlockSpec((tm, tk), lambda g0, g1, k: (ij(g0, g1)[0], k))
    else:
        a_spec = pl.BlockSpec((tk, tm), lambda g0, g1, k: (k, ij(g0, g1)[0]))
    if mode == "nt":
        b_spec = pl.BlockSpec((tn, tk), lambda g0, g1, k: (ij(g0, g1)[1], k))
    else:
        b_spec = pl.BlockSpec((tk, tn), lambda g0, g1, k: (k, ij(g0, g1)[1]))
    dims = {"nn": NN, "nt": NT, "tn": TN}[mode]
    mn_spec = pl.BlockSpec((tm, tn), lambda g0, g1, k: ij(g0, g1))
    n_spec = pl.BlockSpec((1, tn), lambda g0, g1, k: (0, ij(g0, g1)[1]))

    in_specs, args = [], []
    for a, b, _ in pairs:
        in_specs += [a_spec, b_spec]
        args += [a, b]
    for arr, kind in extras:
        in_specs.append(mn_spec if kind == "mn" else n_spec)
        args.append(arr)
    n_p, n_e, n_o = len(pairs), len(extras), len(out_dtypes)

    def body(*refs):
        ab = refs[:2 * n_p]
        ex = refs[2 * n_p:2 * n_p + n_e]
        outs = refs[2 * n_p + n_e:2 * n_p + n_e + n_o]
        accs = refs[2 * n_p + n_e + n_o:]
        k = pl.program_id(2)

        def single_step():
            rc = _pick(tm, (1024,)) if mode != "tn" else tm
            for c in range(tm // rc):
                rs = slice(c * rc, (c + 1) * rc)
                sums = [None] * n_acc
                for p, (_, _, ai) in enumerate(pairs):
                    a_ref, b_ref = ab[2 * p], ab[2 * p + 1]
                    d = _dot(a_ref[...] if mode == "tn" else a_ref[rs, :], b_ref[...], dims)
                    sums[ai] = d if sums[ai] is None else sums[ai] + d
                res = epilogue(sums, [e[rs, :] if e.shape[0] == tm else e[...] for e in ex])
                for o, r in zip(outs, res):
                    o[rs, :] = r.astype(o.dtype)

        def finish(acc_vals):
            res = epilogue(acc_vals, [e[...] for e in ex])
            for o, r in zip(outs, res):
                o[...] = r.astype(o.dtype)

        def accumulate(first):
            w = _pick(tm if mode == "tn" else tn, (512, 384, 256))
            for c in range((tm if mode == "tn" else tn) // w):
                sl = slice(c * w, (c + 1) * w)
                sums = [None] * n_acc
                for p, (_, _, ai) in enumerate(pairs):
                    a_ref, b_ref = ab[2 * p], ab[2 * p + 1]
                    if mode == "tn":
                        d = _dot(a_ref[:, sl], b_ref[...], dims)
                    elif mode == "nn":
                        d = _dot(a_ref[...], b_ref[:, sl], dims)
                    else:
                        d = _dot(a_ref[...], b_ref[sl, :], dims)
                    sums[ai] = d if sums[ai] is None else sums[ai] + d
                idx = (sl, slice(None)) if mode == "tn" else (slice(None), sl)
                for ai, s in enumerate(sums):
                    if first:
                        accs[ai][idx] = s
                    else:
                        accs[ai][idx] += s

        if nk == 1:
            single_step()
        else:
            pl.when(k == 0)(functools.partial(accumulate, True))
            pl.when(k > 0)(functools.partial(accumulate, False))

            @pl.when(k == nk - 1)
            def _():
                finish([a[...] for a in accs])

    outs, couts = _call(
        body, name, grid, in_specs, [mn_spec] * n_o,
        [jax.ShapeDtypeStruct((M, N), dt) for dt in out_dtypes],
        [pltpu.VMEM((tm, tn), F32) for _ in range(n_acc if nk > 1 else 0)],
        ("parallel", "parallel", "arbitrary"), args, comm, pass_on_at=pass_on_at)
    return (outs, couts) if comm else outs


def _rope_tables(L):
    t = jnp.arange(L, dtype=jnp.int32)
    f = jnp.arange(32, dtype=jnp.int32).astype(F32)
    ang = t.astype(F32)[:, None] * (ROPE_BASE ** (-f / 32.0))[None, :]
    cos, sin = jnp.cos(ang), jnp.sin(ang)
    ret = jnp.stack([jnp.tile(cos, (1, 4)), jnp.tile(jnp.concatenate([-sin, sin], axis=1), (1, 2))])
    f2 = jnp.arange(16, dtype=jnp.int32).astype(F32)
    inv2 = (ROPE_BASE ** (-f2 / 16.0))[None, :]
    ang_r = (t // GRID_W).astype(F32)[:, None] * inv2
    ang_c = (t % GRID_W).astype(F32)[:, None] * inv2
    cr, sr, cc, sc = jnp.cos(ang_r), jnp.sin(ang_r), jnp.cos(ang_c), jnp.sin(ang_c)
    att = jnp.stack([jnp.tile(jnp.concatenate([cr, cr, cc, cc], axis=1), (1, 2)),
                     jnp.tile(jnp.concatenate([-sr, sr, -sc, sc], axis=1), (1, 2))])
    return ret.astype(F32), att.astype(F32)


def _swap(x, sh):
    lane = lax.broadcasted_iota(jnp.int32, x.shape, 1)
    ra = pltpu.roll(x, LANES - sh, 1)
    rb = pltpu.roll(x, sh, 1)
    la = pltpu.roll(lane, LANES - sh, 1)
    partner = jnp.where((lane % (2 * sh)) < sh, lane + sh, lane - sh)
    return jnp.where(la == partner, ra, rb)


def _rope(x, cos, sin, sh):
    return x * cos + _swap(x, sh) * sin


def _rope_t(d, cos, sin, sh):
    return d * cos + _swap(d * sin, sh)


def _half_mask(shape, a):
    lane = lax.broadcasted_iota(jnp.int32, shape, 1)
    return (lane < 64) if a == 0 else (lane >= 64)


def _mod_fwd(s_in, w_l, b_l):
    D, C6 = w_l.shape
    tk = _pick(D, (512, 256, 128))
    nk = D // tk

    def body(s_ref, w_ref, b_ref, o_ref):
        k = pl.program_id(0)
        s = s_ref[...]
        s = s * _sigmoid(s)
        d = jnp.dot(s, w_ref[...], preferred_element_type=F32, precision=lax.Precision.HIGHEST)

        @pl.when(k == 0)
        def _():
            o_ref[...] = d + b_ref[...]

        @pl.when(k > 0)
        def _():
            o_ref[...] += d

    return pl.pallas_call(
        body, name="mod_fwd", grid=(nk,),
        in_specs=[pl.BlockSpec((16, tk), lambda k: (0, k)), pl.BlockSpec((tk, C6), lambda k: (k, 0)),
                  pl.BlockSpec((1, C6), lambda k: (0, 0))],
        out_specs=pl.BlockSpec((16, C6), lambda k: (0, 0)),
        out_shape=jax.ShapeDtypeStruct((16, C6), F32),
        compiler_params=_cparams(("arbitrary",)),
    )(s_in, w_l, b_l)


def _mod_bwd(s_in, dm, w_l):
    D, C6 = w_l.shape
    tk = _pick(D, (512, 256, 128))
    nk = D // tk

    def body(s_ref, dm_ref, w_ref, gw_ref, gc_ref):
        s = s_ref[...]
        sg = _sigmoid(s)
        act = s * sg
        dmv = dm_ref[...]
        gw_ref[...] = lax.dot_general(act, dmv, TN, preferred_element_type=F32, precision=lax.Precision.HIGHEST)
        ds = lax.dot_general(dmv, w_ref[...], NT, preferred_element_type=F32, precision=lax.Precision.HIGHEST)
        dsil = (sg * (1.0 + s * (1.0 - sg)))[8:9, :]
        gc_ref[...] = jnp.zeros((8, tk), F32) + jnp.sum(ds[8:16, :], axis=0, keepdims=True) * dsil

    return pl.pallas_call(
        body, name="mod_bwd", grid=(nk,),
        in_specs=[pl.BlockSpec((16, tk), lambda k: (0, k)), pl.BlockSpec((16, C6), lambda k: (0, 0)),
                  pl.BlockSpec((tk, C6), lambda k: (k, 0))],
        out_specs=[pl.BlockSpec((tk, C6), lambda k: (k, 0)), pl.BlockSpec((8, tk), lambda k: (0, k))],
        out_shape=[jax.ShapeDtypeStruct((D, C6), F32), jax.ShapeDtypeStruct((8, D), F32)],
        compiler_params=_cparams(("parallel",)),
    )(s_in, dm, w_l)


def _resident(shape):
    return pl.BlockSpec(shape, lambda *_: (0,) * len(shape), pipeline_mode=pl.Buffered(1))


def _norm_rows(x):
    r = lax.rsqrt(jnp.mean(x * x, axis=-1, keepdims=True) + EPS)
    return x * r, r


def _modulate_fwd(name, x, ctx, g, mod, modc, comm=()):
    L, D = x.shape
    tr = ctx.shape[0]
    nx = L // tr

    def body(x_ref, c_ref, g_ref, m_ref, mc_ref, o_ref):
        i = pl.program_id(0)

        def run(src, m):
            n, _ = _norm_rows(src[...])
            o_ref[...] = (n * g_ref[...] * (1.0 + m[1:2, :]) + m[0:1, :]).astype(o_ref.dtype)

        @pl.when(i < nx)
        def _():
            run(x_ref, m_ref)

        @pl.when(i >= nx)
        def _():
            run(c_ref, mc_ref)

    row = pl.BlockSpec((tr, D), lambda i: (jnp.minimum(i, nx - 1), 0))
    vec = pl.BlockSpec((1, D), lambda i: (0, 0))
    mv = pl.BlockSpec((8, D), lambda i: (0, 0))
    return _call(
        body, name, (nx + 1,), [row, pl.BlockSpec((tr, D), lambda i: (0, 0)), vec, mv, mv],
        [pl.BlockSpec((tr, D), lambda i: (i, 0))], [jax.ShapeDtypeStruct((L + tr, D), BF16)], [],
        ("parallel",), (x, ctx, g, mod, modc), comm, pass_on_at=1.0)


def _residual_modulate_fwd(name, x, fbr, gate, g, mod):
    L, D = x.shape
    tr = _pick(L, (512, 256, 128))

    def body(x_ref, f_ref, gt_ref, g_ref, m_ref, x1_ref, o_ref):
        x1 = x_ref[...] + gt_ref[...] * f_ref[...].astype(F32)
        x1_ref[...] = x1
        n, _ = _norm_rows(x1)
        o_ref[...] = (n * g_ref[...] * (1.0 + m_ref[1:2, :]) + m_ref[0:1, :]).astype(o_ref.dtype)

    row = pl.BlockSpec((tr, D), lambda i: (i, 0))
    vec = pl.BlockSpec((1, D), lambda i: (0, 0))
    return pl.pallas_call(
        body, name=name, grid=(L // tr,),
        in_specs=[row, row, vec, vec, pl.BlockSpec((8, D), lambda i: (0, 0))],
        out_specs=[row, row],
        out_shape=[jax.ShapeDtypeStruct((L, D), F32), jax.ShapeDtypeStruct((L, D), BF16)],
        compiler_params=_cparams(("parallel",)),
    )(x, fbr, gate, g, mod)


def _modulate_bwd(name, x, ctx, dh, g, mod, modc, dres, fbr, gate):
    L, D = x.shape
    tr = ctx.shape[0] if ctx is not None else _pick(L, (512, 256, 128))
    nx = L // tr
    nt = nx + (1 if ctx is not None else 0)
    has_f = fbr is not None

    def body(*refs):
        refs = list(refs)
        x_ref = refs.pop(0)
        c_ref = refs.pop(0) if ctx is not None else None
        dh_ref, g_ref, m_ref = refs.pop(0), refs.pop(0), refs.pop(0)
        mc_ref = refs.pop(0) if ctx is not None else None
        dr_ref = refs.pop(0)
        f_ref = refs.pop(0) if has_f else None
        gt_ref = refs.pop(0) if has_f else None
        dx_ref = refs.pop(0)
        df_ref = refs.pop(0) if has_f else None
        acc_ref = refs.pop(0)
        i = pl.program_id(0)

        @pl.when(i == 0)
        def _():
            acc_ref[...] = jnp.zeros_like(acc_ref)

        def sums(src, m, base, grow):
            n, r = _norm_rows(src[...])
            d = dh_ref[...].astype(F32)
            gg = g_ref[...]
            sc1 = 1.0 + m[1:2, :]
            acc_ref[base:base + 1, :] += jnp.sum(d, axis=0, keepdims=True)
            dn = d * n
            acc_ref[base + 1:base + 2, :] += jnp.sum(dn, axis=0, keepdims=True) * gg
            acc_ref[grow:grow + 1, :] += jnp.sum(dn, axis=0, keepdims=True) * sc1
            dnv = d * (gg * sc1)
            return r * (dnv - n * jnp.mean(dnv * n, axis=-1, keepdims=True))

        def x_rows():
            dx = sums(x_ref, m_ref, 0, 2) + dr_ref[...]
            dx_ref[...] = dx
            if has_f:
                acc_ref[6:7, :] += jnp.sum(dx * f_ref[...].astype(F32), axis=0, keepdims=True)
                df_ref[...] = (dx * gt_ref[...]).astype(df_ref.dtype)

        if ctx is None:
            x_rows()
        else:
            pl.when(i < nx)(x_rows)

            @pl.when(i >= nx)
            def _():
                sums(c_ref, mc_ref, 3, 2)

    row = pl.BlockSpec((tr, D), lambda i: (jnp.minimum(i, nx - 1), 0))
    vec = pl.BlockSpec((1, D), lambda i: (0, 0))
    mv = pl.BlockSpec((8, D), lambda i: (0, 0))
    in_specs, args = [row], [x]
    if ctx is not None:
        in_specs.append(pl.BlockSpec((tr, D), lambda i: (0, 0)))
        args.append(ctx)
    in_specs += [pl.BlockSpec((tr, D), lambda i: (i, 0)), vec, mv]
    args += [dh, g, mod]
    if ctx is not None:
        in_specs.append(mv)
        args.append(modc)
    in_specs.append(row)
    args.append(dres)
    out_specs = [row]
    out_shape = [jax.ShapeDtypeStruct((L, D), F32)]
    if has_f:
        in_specs += [row, vec]
        args += [fbr, gate]
        out_specs.append(row)
        out_shape.append(jax.ShapeDtypeStruct((L, D), BF16))
    out_specs.append(pl.BlockSpec((16, D), lambda i: (0, 0)))
    out_shape.append(jax.ShapeDtypeStruct((16, D), F32))
    return pl.pallas_call(
        body, name=name, grid=(nt,), in_specs=in_specs, out_specs=out_specs, out_shape=out_shape,
        compiler_params=_cparams(("arbitrary",)),
    )(*args)


def _loss_head(x1, tgt, nf, fbr, gate):
    L, D = x1.shape
    tr = _pick(L, (512, 256, 128))

    def body(x_ref, t_ref, w_ref, f_ref, gt_ref, dx_ref, df_ref, acc_ref):
        i = pl.program_id(0)

        @pl.when(i == 0)
        def _():
            acc_ref[...] = jnp.zeros_like(acc_ref)

        n, r = _norm_rows(x_ref[...] + gt_ref[...] * f_ref[...].astype(F32))
        w = w_ref[...]
        e = n * w - t_ref[...]
        acc_ref[0:1, :] += jnp.sum(e * e, axis=0, keepdims=True) * (0.5 / D)
        dout = e * (1.0 / D)
        acc_ref[1:2, :] += jnp.sum(dout * n, axis=0, keepdims=True)
        dn = dout * w
        dx = r * (dn - n * jnp.mean(dn * n, axis=-1, keepdims=True))
        dx_ref[...] = dx
        acc_ref[2:3, :] += jnp.sum(dx * f_ref[...].astype(F32), axis=0, keepdims=True)
        df_ref[...] = (dx * gt_ref[...]).astype(df_ref.dtype)

        @pl.when(i == pl.num_programs(0) - 1)
        def _():
            acc_ref[3:4, :] = jnp.zeros((1, D), F32) + jnp.sum(acc_ref[0:1, :])

    row = pl.BlockSpec((tr, D), lambda i: (i, 0))
    vec = pl.BlockSpec((1, D), lambda i: (0, 0))
    return pl.pallas_call(
        body, name="loss_head", grid=(L // tr,),
        in_specs=[row, row, vec, row, vec],
        out_specs=[row, row, pl.BlockSpec((8, D), lambda i: (0, 0))],
        out_shape=[jax.ShapeDtypeStruct((L, D), F32), jax.ShapeDtypeStruct((L, D), BF16),
                   jax.ShapeDtypeStruct((8, D), F32)],
        compiler_params=_cparams(("arbitrary",)),
    )(x1, tgt, nf, fbr, gate)


RET_SUB = 4
N_TAB = 7


def _ret_tables(rdb, Lc):
    def body(rd_ref, t_ref, c_ref):
        d = pl.program_id(0) // RET_HEADS
        fwd = d == 0
        lg = -jnp.exp(rd_ref[0])
        i = lax.broadcasted_iota(jnp.int32, (CHUNK, CHUNK), 0).astype(F32)
        j = lax.broadcasted_iota(jnp.int32, (CHUNK, CHUNK), 1).astype(F32)
        rel = jnp.where(fwd, i - j, j - i)
        mask = (rel > 0.0) | ((rel == 0.0) & fwd)
        dm = jnp.where(mask, jnp.exp(lg * jnp.maximum(rel, 0.0)), 0.0)
        t_ref[0, 0] = dm
        t_ref[0, 1] = rel * dm
        qc = jnp.where(fwd, i + 1.0, CHUNK - i)
        qw = jnp.exp(lg * qc)
        t_ref[0, 2] = qw
        t_ref[0, 3] = qw * qc
        kc = jnp.where(fwd, CHUNK - 1.0 - i, i)
        kw = jnp.exp(lg * kc)
        t_ref[0, 4] = kw
        t_ref[0, 5] = kw * kc
        t_ref[0, 6] = jnp.exp(lg * float(CHUNK)) + jnp.zeros((CHUNK, CHUNK), F32)
        m = lax.broadcasted_iota(jnp.int32, (Lc, LANES), 0).astype(F32)
        cc = jnp.where(fwd, Lc - 1.0 - m, m)
        cw = jnp.exp(lg * cc)
        c_ref[0, 0] = cw
        c_ref[0, 1] = cw * cc

    return pl.pallas_call(
        body, name="ret_tables", grid=(2 * RET_HEADS,),
        in_specs=[pl.BlockSpec((1, 1, LANES), lambda r: (r, 0, 0))],
        out_specs=[pl.BlockSpec((1, N_TAB, CHUNK, CHUNK), lambda r: (r, 0, 0, 0)),
                   pl.BlockSpec((1, 2, Lc, LANES), lambda r: (r, 0, 0, 0))],
        out_shape=[jax.ShapeDtypeStruct((2 * RET_HEADS, N_TAB, CHUNK, CHUNK), F32),
                   jax.ShapeDtypeStruct((2 * RET_HEADS, 2, Lc, LANES), F32)],
        compiler_params=_cparams(("parallel",)),
    )(rdb)


def _ret_ctx_state(P, ctab, L, Lc):
    cb = L // Lc

    def body(k_ref, v_ref, c_ref, s_ref):
        for p in range(RET_HEADS // 2):
            kp = k_ref[:, p * LANES:(p + 1) * LANES].astype(F32) * K_SCALE
            for a in range(2):
                h = 2 * p + a
                kh = jnp.where(_half_mask(kp.shape, a), kp, 0.0)
                vh = v_ref[:, h * RET_DV:(h + 1) * RET_DV]
                for d in range(2):
                    kw = (kh * c_ref[d * RET_HEADS + h, 0]).astype(BF16)
                    s_ref[d * RET_HEADS + h] = _dot(kw, vh, TN)

    return pl.pallas_call(
        body, name="ret_ctx_state", grid=(1,),
        in_specs=[pl.BlockSpec((Lc, 512), lambda i: (cb, C_RK // 512)),
                  pl.BlockSpec((Lc, 1024), lambda i: (cb, C_RV // 1024)),
                  pl.BlockSpec((2 * RET_HEADS, 2, Lc, LANES), lambda i: (0, 0, 0, 0))],
        out_specs=pl.BlockSpec((2 * RET_HEADS, LANES, RET_DV), lambda i: (0, 0, 0)),
        out_shape=jax.ShapeDtypeStruct((2 * RET_HEADS, LANES, RET_DV), F32),
        compiler_params=_cparams(("arbitrary",)),
    )(P, P, ctab)


def _ret_fwd(P, rope, tabs, s0, L, comm=()):
    n = L // CHUNK
    nb = n // RET_SUB

    def body(qf, kf, vf, rf, qb, kb, vb, rb, t_ref, s0_ref, of_ref, ob_ref, stf_ref, stb_ref, st):
        s = pl.program_id(0)

        @pl.when(s == 0)
        def _():
            st[...] = s0_ref[...]

        for rnd in range(RET_SUB):
            units = []
            for d, (q_ref, k_ref, v_ref, r_ref, o_ref, so_ref) in enumerate(
                    ((qf, kf, vf, rf, of_ref, stf_ref), (qb, kb, vb, rb, ob_ref, stb_ref))):
                j = rnd if d == 0 else RET_SUB - 1 - rnd
                rows = slice(j * CHUNK, (j + 1) * CHUNK)
                cos, sin = r_ref[0, rows, :], r_ref[1, rows, :]
                for p in range(RET_HEADS // 2):
                    qp = _rope(q_ref[rows, p * LANES:(p + 1) * LANES].astype(F32), cos, sin, 32)
                    kp = _rope(k_ref[rows, p * LANES:(p + 1) * LANES].astype(F32), cos, sin, 32) * K_SCALE
                    for a in range(2):
                        h = 2 * p + a
                        hm = _half_mask(qp.shape, a)
                        units.append(dict(r=d * RET_HEADS + h, h=h, a=a, j=j, rows=rows, o_ref=o_ref, so_ref=so_ref,
                                          v_ref=v_ref, qh=jnp.where(hm, qp, 0.0), kh=jnp.where(hm, kp, 0.0)))
            for u in units:
                u["sc"] = _dot(u["qh"].astype(BF16), u["kh"].astype(BF16), NT)
            for u in units:
                r, h = u["r"], u["h"]
                sp = st[r]
                u["so_ref"][u["j"], h] = sp[u["a"] * RET_DK:(u["a"] + 1) * RET_DK, :]
                vh = u["v_ref"][u["rows"], h * RET_DV:(h + 1) * RET_DV]
                o = _dot((u["sc"] * t_ref[r, 0]).astype(BF16), vh, NN)
                o += _dot((u["qh"] * t_ref[r, 2]).astype(BF16), sp.astype(BF16), NN)
                u["o_ref"][u["rows"], h * RET_DV:(h + 1) * RET_DV] = o
            for u in units:
                r, h = u["r"], u["h"]
                vh = u["v_ref"][u["rows"], h * RET_DV:(h + 1) * RET_DV]
                st[r] = t_ref[r, 6] * st[r] + _dot((u["kh"] * t_ref[r, 4]).astype(BF16), vh, TN)

    fw = lambda s: s
    bw = lambda s: nb - 1 - s
    RB = RET_SUB * CHUNK

    def specs(cm):
        return [pl.BlockSpec((RB, 512), lambda s: (cm(s), C_RQ // 512)),
                pl.BlockSpec((RB, 512), lambda s: (cm(s), C_RK // 512)),
                pl.BlockSpec((RB, 1024), lambda s: (cm(s), C_RV // 1024)),
                pl.BlockSpec((2, RB, LANES), lambda s: (0, cm(s), 0))]

    full = lambda shp: pl.BlockSpec(shp, lambda s: (0,) * len(shp))
    return _call(
        body, "ret_fwd", (nb,),
        specs(fw) + specs(bw) + [_resident((2 * RET_HEADS, N_TAB, CHUNK, CHUNK)),
                                 _resident((2 * RET_HEADS, LANES, RET_DV))],
        [pl.BlockSpec((RB, 1024), lambda s: (fw(s), 0)),
         pl.BlockSpec((RB, 1024), lambda s: (bw(s), 0)),
         pl.BlockSpec((RET_SUB, RET_HEADS, RET_DK, RET_DV), lambda s: (fw(s), 0, 0, 0)),
         pl.BlockSpec((RET_SUB, RET_HEADS, RET_DK, RET_DV), lambda s: (bw(s), 0, 0, 0))],
        [jax.ShapeDtypeStruct((L, 1024), F32), jax.ShapeDtypeStruct((L, 1024), F32),
         jax.ShapeDtypeStruct((n, RET_HEADS, RET_DK, RET_DV), F32),
         jax.ShapeDtypeStruct((n, RET_HEADS, RET_DK, RET_DV), F32)],
        [pltpu.VMEM((2 * RET_HEADS, LANES, RET_DV), F32)],
        ("arbitrary",), (P, P, P, rope, P, P, P, rope, tabs, s0), comm)


def _ret_finish_fwd(of, ob, P, L):
    tr = _pick(L, (1024, 512, 256, 128))

    def body(f_ref, b_ref, g_ref, y_ref):
        for h in range(RET_HEADS):
            sl = slice(h * RET_DV, (h + 1) * RET_DV)
            n, _ = _norm_rows(f_ref[:, sl] + b_ref[:, sl])
            g = g_ref[:, sl].astype(F32)
            y_ref[:, sl] = (n * (g * _sigmoid(g))).astype(y_ref.dtype)

    row = pl.BlockSpec((tr, 1024), lambda i: (i, 0))
    return pl.pallas_call(
        body, name="ret_finish_fwd", grid=(L // tr,),
        in_specs=[row, row, pl.BlockSpec((tr, 1024), lambda i: (i, C_RG // 1024))],
        out_specs=row, out_shape=jax.ShapeDtypeStruct((L, 2048), BF16),
        compiler_params=_cparams(("parallel",)),
    )(of, ob, P)


def _ret_finish_bwd(of, ob, P, dY, L):
    tr = _pick(L, (1024, 512, 256, 128))

    def body(f_ref, b_ref, g_ref, dy_ref, do_ref, dg_ref):
        for h in range(RET_HEADS):
            sl = slice(h * RET_DV, (h + 1) * RET_DV)
            n, r = _norm_rows(f_ref[:, sl] + b_ref[:, sl])
            g = g_ref[:, sl].astype(F32)
            sg = _sigmoid(g)
            dy = dy_ref[:, sl].astype(F32)
            dg_ref[:, sl] = (dy * n * (sg * (1.0 + g * (1.0 - sg)))).astype(dg_ref.dtype)
            dn = dy * (g * sg)
            do_ref[:, sl] = (r * (dn - n * jnp.mean(dn * n, axis=-1, keepdims=True))).astype(do_ref.dtype)

    row = pl.BlockSpec((tr, 1024), lambda i: (i, 0))
    return pl.pallas_call(
        body, name="ret_finish_bwd", grid=(L // tr,),
        in_specs=[row, row, pl.BlockSpec((tr, 1024), lambda i: (i, C_RG // 1024)), row],
        out_specs=[row, row],
        out_shape=[jax.ShapeDtypeStruct((L, 1024), BF16), jax.ShapeDtypeStruct((L, 1024), BF16)],
        compiler_params=_cparams(("parallel",)),
    )(of, ob, P, dY)


def _ret_bwd(P, rope, tabs, stf, stb, dO, L, comm=()):
    n = L // CHUNK
    nb = n // RET_SUB

    def body(qf, kf, vf, rf, gf, sf, qb, kb, vb, rb, gb, sb, t_ref,
             dqf, dkf, dvf, dqb, dkb, dvb, ds0_ref, dlg_ref, ds):
        s = pl.program_id(0)

        @pl.when(s == 0)
        def _():
            ds[...] = jnp.zeros_like(ds)
            dlg_ref[...] = jnp.zeros_like(dlg_ref)

        for rnd in range(RET_SUB):
            units, pairs = [], []
            for d, (q_ref, k_ref, v_ref, r_ref, g_ref, s_ref, dq_ref, dk_ref, dv_ref) in enumerate(
                    ((qf, kf, vf, rf, gf, sf, dqf, dkf, dvf), (qb, kb, vb, rb, gb, sb, dqb, dkb, dvb))):
                j = RET_SUB - 1 - rnd if d == 0 else rnd
                rows = slice(j * CHUNK, (j + 1) * CHUNK)
                cos, sin = r_ref[0, rows, :], r_ref[1, rows, :]
                for p in range(RET_HEADS // 2):
                    qp = _rope(q_ref[rows, p * LANES:(p + 1) * LANES].astype(F32), cos, sin, 32)
                    kp = _rope(k_ref[rows, p * LANES:(p + 1) * LANES].astype(F32), cos, sin, 32) * K_SCALE
                    pair = dict(p=p, rows=rows, cos=cos, sin=sin, dq_ref=dq_ref, dk_ref=dk_ref, us=[])
                    pairs.append(pair)
                    for a in range(2):
                        h = 2 * p + a
                        r = d * RET_HEADS + h
                        hm = _half_mask(qp.shape, a)
                        zero = jnp.zeros((RET_DK, RET_DV), F32)
                        sp = s_ref[j, h]
                        u = dict(r=r, h=h, rows=rows, dv_ref=dv_ref,
                                 qh=jnp.where(hm, qp, 0.0), kh=jnp.where(hm, kp, 0.0),
                                 vh=v_ref[rows, h * RET_DV:(h + 1) * RET_DV],
                                 gh=g_ref[rows, h * RET_DV:(h + 1) * RET_DV],
                                 sp=jnp.concatenate([sp, zero] if a == 0 else [zero, sp], axis=0),
                                 dsn=ds[r])
                        u["qhb"], u["khb"] = u["qh"].astype(BF16), u["kh"].astype(BF16)
                        units.append(u)
                        pair["us"].append(u)
            for u in units:
                u["am"] = _dot(u["qhb"], u["khb"], NT)
                u["dar"] = _dot(u["gh"], u["vh"], NT)
                u["xq"] = _dot(u["gh"], u["sp"].astype(BF16), NT)
                u["yk"] = _dot(u["vh"], u["dsn"].astype(BF16), NT)
            for u in units:
                r = u["r"]
                dm = t_ref[r, 0]
                u["da"] = (u["dar"] * dm).astype(BF16)
                u["amd"] = (u["am"] * dm).astype(BF16)
                part = (jnp.sum(u["am"] * u["dar"] * t_ref[r, 1]) + jnp.sum(u["qh"] * t_ref[r, 3] * u["xq"])
                        + jnp.sum(u["kh"] * t_ref[r, 5] * u["yk"])
                        + float(CHUNK) * jnp.sum(t_ref[r, 6] * u["dsn"] * u["sp"]))
                dlg_ref[r:r + 1, :] += jnp.zeros((1, LANES), F32) + part
            for u in units:
                r, h = u["r"], u["h"]
                u["dq"] = _dot(u["da"], u["khb"], NN) + u["xq"] * t_ref[r, 2]
                u["dk"] = _dot(u["da"], u["qhb"], TN) + u["yk"] * t_ref[r, 4]
                u["dv_ref"][u["rows"], h * RET_DV:(h + 1) * RET_DV] = (
                    _dot(u["amd"], u["gh"], TN)
                    + _dot((u["kh"] * t_ref[r, 4]).astype(BF16), u["dsn"].astype(BF16), NN)
                ).astype(u["dv_ref"].dtype)
                ds[r] = t_ref[r, 6] * u["dsn"] + _dot((u["qh"] * t_ref[r, 2]).astype(BF16), u["gh"], TN)
            for pair in pairs:
                sl = slice(pair["p"] * LANES, (pair["p"] + 1) * LANES)
                u0, u1 = pair["us"]
                pair["dq_ref"][pair["rows"], sl] = _rope_t(
                    u0["dq"] + u1["dq"], pair["cos"], pair["sin"], 32).astype(BF16)
                pair["dk_ref"][pair["rows"], sl] = _rope_t(
                    (u0["dk"] + u1["dk"]) * K_SCALE, pair["cos"], pair["sin"], 32).astype(BF16)

        @pl.when(s == nb - 1)
        def _():
            ds0_ref[...] = ds[...]

    fw = lambda s: nb - 1 - s
    bw = lambda s: s
    RB = RET_SUB * CHUNK

    def specs(cm):
        return [pl.BlockSpec((RB, 512), lambda s: (cm(s), C_RQ // 512)),
                pl.BlockSpec((RB, 512), lambda s: (cm(s), C_RK // 512)),
                pl.BlockSpec((RB, 1024), lambda s: (cm(s), C_RV // 1024)),
                pl.BlockSpec((2, RB, LANES), lambda s: (0, cm(s), 0)),
                pl.BlockSpec((RB, 1024), lambda s: (cm(s), 0)),
                pl.BlockSpec((RET_SUB, RET_HEADS, RET_DK, RET_DV), lambda s: (cm(s), 0, 0, 0))]

    def ospecs(cm):
        return [pl.BlockSpec((RB, 512), lambda s: (cm(s), 0)), pl.BlockSpec((RB, 512), lambda s: (cm(s), 0)),
                pl.BlockSpec((RB, 1024), lambda s: (cm(s), 0))]

    oshape = [jax.ShapeDtypeStruct((L, 512), BF16), jax.ShapeDtypeStruct((L, 512), BF16),
              jax.ShapeDtypeStruct((L, 1024), BF16)]
    full = lambda shp: pl.BlockSpec(shp, lambda s: (0,) * len(shp))
    return _call(
        body, "ret_bwd", (nb,),
        specs(fw) + specs(bw) + [_resident((2 * RET_HEADS, N_TAB, CHUNK, CHUNK))],
        ospecs(fw) + ospecs(bw) + [full((2 * RET_HEADS, LANES, RET_DV)), full((2 * RET_HEADS, LANES))],
        oshape + oshape + [jax.ShapeDtypeStruct((2 * RET_HEADS, LANES, RET_DV), F32),
                           jax.ShapeDtypeStruct((2 * RET_HEADS, LANES), F32)],
        [pltpu.VMEM((2 * RET_HEADS, LANES, RET_DV), F32)],
        ("arbitrary",), (P, P, P, rope, dO, stf, P, P, P, rope, dO, stb, tabs), comm)


def _ret_ctx_bwd(P, ctab, ds0, dlg, rdb, L, Lc):
    cb = L // Lc

    def body(k_ref, v_ref, c_ref, ds_ref, dlg_ref, rd_ref, dk_ref, dv_ref, drd_ref):
        for p in range(RET_HEADS // 2):
            kp = k_ref[:, p * LANES:(p + 1) * LANES].astype(F32) * K_SCALE
            dkp = jnp.zeros((Lc, LANES), F32)
            for a in range(2):
                h = 2 * p + a
                kh = jnp.where(_half_mask(kp.shape, a), kp, 0.0)
                vh = v_ref[:, h * RET_DV:(h + 1) * RET_DV]
                dvh = jnp.zeros((Lc, RET_DV), F32)
                for d in range(2):
                    r = d * RET_HEADS + h
                    dsb = ds_ref[r].astype(BF16)
                    cw, cwc = c_ref[r, 0], c_ref[r, 1]
                    y = _dot(vh, dsb, NT)
                    dkp += y * cw
                    dvh += _dot((kh * cw).astype(BF16), dsb, NN)
                    lg = -jnp.exp(rd_ref[r])
                    drd_ref[r:r + 1, :] = (dlg_ref[r:r + 1, :] + jnp.sum(kh * cwc * y)) * lg
                dv_ref[:, h * RET_DV:(h + 1) * RET_DV] = dvh
            dk_ref[:, p * LANES:(p + 1) * LANES] = dkp * K_SCALE

    full = lambda shp: pl.BlockSpec(shp, lambda i: (0,) * len(shp))
    return pl.pallas_call(
        body, name="ret_ctx_bwd", grid=(1,),
        in_specs=[pl.BlockSpec((Lc, 512), lambda i: (cb, C_RK // 512)),
                  pl.BlockSpec((Lc, 1024), lambda i: (cb, C_RV // 1024)),
                  full((2 * RET_HEADS, 2, Lc, LANES)), full((2 * RET_HEADS, LANES, RET_DV)),
                  full((2 * RET_HEADS, LANES)), full((2 * RET_HEADS, 1, LANES))],
        out_specs=[full((Lc, 512)), full((Lc, 1024)), full((2 * RET_HEADS, LANES))],
        out_shape=[jax.ShapeDtypeStruct((Lc, 512), F32), jax.ShapeDtypeStruct((Lc, 1024), F32),
                   jax.ShapeDtypeStruct((2 * RET_HEADS, LANES), F32)],
        compiler_params=_cparams(("arbitrary",)),
    )(P, P, ctab, ds0, dlg, rdb)


BLK = 128
N_LOC = 3 * BLK


ATT_SUB = 4


def _att_inputs(P, rope, L, Lc):
    n = L // BLK
    cb = L // Lc
    prev = lambda i: jnp.maximum(ATT_SUB * i - 1, 0)
    nxt = lambda i: jnp.minimum(ATT_SUB * i + ATT_SUB, n - 1)
    specs = [pl.BlockSpec((ATT_SUB * BLK, 1024), lambda i: (i, C_AQ // 1024))]
    args = [P]
    for col in (C_AK // 256, C_AV // 256):
        specs += [pl.BlockSpec((BLK, 256), functools.partial(lambda i, col: (prev(i), col), col=col)),
                  pl.BlockSpec((ATT_SUB * BLK, 256), functools.partial(lambda i, col: (i, col), col=col)),
                  pl.BlockSpec((BLK, 256), functools.partial(lambda i, col: (nxt(i), col), col=col)),
                  pl.BlockSpec((Lc, 256), functools.partial(lambda i, col: (cb, col), col=col))]
        args += [P] * 4
    specs += [pl.BlockSpec((2, BLK, LANES), lambda i: (0, prev(i), 0)),
              pl.BlockSpec((2, ATT_SUB * BLK, LANES), lambda i: (0, i, 0)),
              pl.BlockSpec((2, BLK, LANES), lambda i: (0, nxt(i), 0))]
    args += [rope] * 3
    return specs, args


def _att_prep(i, n, refs, Lc):
    q_ref, kp_ref, kc_ref, kn_ref, kx_ref, vp_ref, vc_ref, vn_ref, vx_ref, rp_ref, rc_ref, rn_ref = refs
    cos = jnp.concatenate([rp_ref[0], rc_ref[0], rn_ref[0]], axis=0)
    sin = jnp.concatenate([rp_ref[1], rc_ref[1], rn_ref[1]], axis=0)

    def dup(x):
        xr = pltpu.roll(x, 64, 1)
        return [jnp.where(_half_mask(x.shape, b), x, xr).astype(BF16) for b in range(2)]

    kd = [[] for _ in range(ATT_SUB)]
    vd = [[] for _ in range(ATT_SUB)]
    for t in range(ATT_KV // 2):
        sl = slice(t * LANES, (t + 1) * LANES)
        kl = jnp.concatenate([kp_ref[:, sl], kc_ref[:, sl], kn_ref[:, sl]], axis=0).astype(F32)
        kl = dup(_rope(kl, cos, sin, 16))
        vl = dup(jnp.concatenate([vp_ref[:, sl], vc_ref[:, sl], vn_ref[:, sl]], axis=0).astype(F32))
        kx, vx = dup(kx_ref[:, sl].astype(F32)), dup(vx_ref[:, sl].astype(F32))
        for j in range(ATT_SUB):
            rows = slice(j * BLK, j * BLK + N_LOC)
            for b in range(2):
                kd[j].append(jnp.concatenate([kl[b][rows], kx[b]], axis=0))
                vd[j].append(jnp.concatenate([vl[b][rows], vx[b]], axis=0))
    nk = N_LOC + Lc
    rr = lax.broadcasted_iota(jnp.int32, (BLK, nk), 0)
    ss = lax.broadcasted_iota(jnp.int32, (BLK, nk), 1)
    band = (ss >= rr) & (ss <= rr + 2 * BLK)
    bias4, tabs = [], []
    for j in range(ATT_SUB):
        blk = ATT_SUB * i + j
        lo = jnp.where(blk == 0, BLK, 0)
        hi = jnp.where(blk == n - 1, 2 * BLK, N_LOC)
        bias = jnp.where((ss >= N_LOC) | (band & (ss >= lo) & (ss < hi)), 0.0, NEG)
        bias4.append(jnp.concatenate([bias] * 4, axis=0))
        tabs.append((rc_ref[0, j * BLK:(j + 1) * BLK, :], rc_ref[1, j * BLK:(j + 1) * BLK, :]))
    return kd, vd, bias4, tabs


LOG2E = 1.4426950408889634
LN2 = 0.6931471805599453
Q_SCALE = A_SCALE * LOG2E


def _stack4(ref, rows, g, f=None):
    parts = []
    for jp in range(2):
        t = ref[rows, (2 * g + jp) * LANES:(2 * g + jp + 1) * LANES].astype(F32)
        if f is not None:
            t = f(t)
        for a in range(2):
            parts.append(jnp.where(_half_mask(t.shape, a), t, 0.0))
    return jnp.concatenate(parts, axis=0)


def _unstack4(x4, jp):
    r0 = 2 * jp * BLK
    lo = x4[r0:r0 + BLK]
    hi = x4[r0 + BLK:r0 + 2 * BLK]
    return jnp.where(_half_mask(lo.shape, 0), lo, hi)


def _softmax_parts(s, bias4, sink_ref, g):
    sink_col = LOG2E * jnp.concatenate(
        [jnp.zeros((BLK, 1), F32) + sink_ref[4 * g + r:4 * g + r + 1, 0:1] for r in range(4)], axis=0)
    s = s + bias4
    m = jnp.maximum(jnp.max(s, axis=-1, keepdims=True), sink_col)
    e = jnp.exp2(s - m)
    es = jnp.exp2(sink_col - m)
    return e, es, jnp.sum(e, axis=-1, keepdims=True) + es


def _att_fwd(P, rope, sinkb, Y, L, Lc, comm=()):
    n = L // BLK
    specs, args = _att_inputs(P, rope, L, Lc)

    def body(*refs):
        sink_ref, o_ref = refs[12], refs[14]
        i = pl.program_id(0)
        kd, vd, bias4, tabs = _att_prep(i, n, refs[:12], Lc)
        for j in range(ATT_SUB):
            rows = slice(j * BLK, (j + 1) * BLK)
            cq, sq = tabs[j]
            for g in range(ATT_KV):
                q4 = _stack4(refs[0], rows, g, lambda t: _rope(t, cq, sq, 16) * Q_SCALE).astype(BF16)
                e, _, l = _softmax_parts(_dot(q4, kd[j][g], NT), bias4[j], sink_ref, g)
                o4 = _dot(e.astype(BF16), vd[j][g], NN) * (1.0 / l)
                for jp in range(2):
                    c0 = (2 * g + jp) * LANES
                    o_ref[rows, c0:c0 + LANES] = _unstack4(o4, jp).astype(o_ref.dtype)

    return _call(
        body, "att_fwd", (n // ATT_SUB,),
        specs + [pl.BlockSpec((ATT_HEADS, LANES), lambda i: (0, 0)), pl.BlockSpec(memory_space=pl.ANY)],
        [pl.BlockSpec((ATT_SUB * BLK, 1024), lambda i: (i, 1))], [jax.ShapeDtypeStruct((L, 2048), BF16)], [],
        ("parallel",), (*args, sinkb, Y), comm, aliases={13: 0})


def _att_bwd(P, rope, sinkb, Y, dY, L, Lc, comm=()):
    n = L // BLK
    specs, args = _att_inputs(P, rope, L, Lc)
    nk = N_LOC + Lc

    def body(*refs):
        sink_ref, y_ref, dy_ref = refs[12], refs[13], refs[14]
        dq_ref, dkl_ref, dvl_ref, dkx_ref, dvx_ref, dsk_ref = refs[15:21]
        i = pl.program_id(0)

        @pl.when(i == 0)
        def _():
            dkx_ref[...] = jnp.zeros_like(dkx_ref)
            dvx_ref[...] = jnp.zeros_like(dvx_ref)
            dsk_ref[...] = jnp.zeros_like(dsk_ref)

        kd, vd, bias4, tabs = _att_prep(i, n, refs[:12], Lc)
        for j in range(ATT_SUB):
            rows = slice(j * BLK, (j + 1) * BLK)
            cq, sq = tabs[j]
            for t in range(ATT_KV // 2):
                dk_halves, dv_halves = [], []
                for b in range(2):
                    g = 2 * t + b
                    q4 = _stack4(refs[0], rows, g, lambda x: _rope(x, cq, sq, 16) * Q_SCALE).astype(BF16)
                    do4 = _stack4(dy_ref, rows, g)
                    delta = jnp.sum(do4 * _stack4(y_ref, rows, g), axis=-1, keepdims=True)
                    do4b = do4.astype(BF16)
                    e, es, l = _softmax_parts(_dot(q4, kd[j][g], NT), bias4[j], sink_ref, g)
                    inv = 1.0 / l
                    p = e * inv
                    dsc = (p * (_dot(do4b, vd[j][g], NT) - delta)).astype(BF16)
                    dsr = es * inv * delta
                    for r in range(4):
                        h = 4 * g + r
                        dsk_ref[h:h + 1, :] += jnp.zeros((1, LANES), F32) - jnp.sum(dsr[r * BLK:(r + 1) * BLK])
                    dq4 = _dot(dsc, kd[j][g], NN) * A_SCALE
                    for jp in range(2):
                        c0 = (2 * g + jp) * LANES
                        dq_ref[rows, c0:c0 + LANES] = _rope_t(_unstack4(dq4, jp), cq, sq, 16).astype(dq_ref.dtype)
                    dkd = _dot(q4, dsc, TN) * LN2
                    dvd = _dot(do4b, p.astype(BF16), TN)
                    dk_halves.append(dkd[:ATT_DH] + dkd[ATT_DH:])
                    dv_halves.append(dvd[:ATT_DH] + dvd[ATT_DH:])
                dk_t = jnp.concatenate(dk_halves, axis=0).T
                dv_t = jnp.concatenate(dv_halves, axis=0).T
                sl = slice(t * LANES, (t + 1) * LANES)
                dkl_ref[j, :, sl] = dk_t[:N_LOC]
                dvl_ref[j, :, sl] = dv_t[:N_LOC]
                dkx_ref[:, sl] += dk_t[N_LOC:]
                dvx_ref[:, sl] += dv_t[N_LOC:]

    row = pl.BlockSpec((ATT_SUB * BLK, 1024), lambda i: (i, 0))
    loc = pl.BlockSpec((ATT_SUB, N_LOC, 256), lambda i: (i, 0, 0))
    cx = pl.BlockSpec((Lc, 256), lambda i: (0, 0))
    return _call(
        body, "att_bwd", (n // ATT_SUB,),
        specs + [pl.BlockSpec((ATT_HEADS, LANES), lambda i: (0, 0))]
        + [pl.BlockSpec((ATT_SUB * BLK, 1024), lambda i: (i, 1))] * 2,
        [row, loc, loc, cx, cx, pl.BlockSpec((ATT_HEADS, LANES), lambda i: (0, 0))],
        [jax.ShapeDtypeStruct((L, 1024), BF16), jax.ShapeDtypeStruct((n, N_LOC, 256), F32),
         jax.ShapeDtypeStruct((n, N_LOC, 256), F32), jax.ShapeDtypeStruct((Lc, 256), F32),
         jax.ShapeDtypeStruct((Lc, 256), F32), jax.ShapeDtypeStruct((ATT_HEADS, LANES), F32)], [],
        ("arbitrary",), (*args, sinkb, Y, dY), comm)


def _assemble_dp(L, Lc, dqf, dqb, dkf, dkb, dvf, dvb, drg, daq, dkl, dvl, rope_att, dck, dcv, dkx, dvx):
    n = L // BLK
    nc = Lc // BLK

    def body(dqf_r, dqb_r, dkf_r, dkb_r, dvf_r, dvb_r, drg_r, daq_r, kl0, kl1, kl2, vl0, vl1, vl2, rp_r,
             dck_r, dcv_r, dkx_r, dvx_r, o_ref):
        i = pl.program_id(0)

        @pl.when(i < n)
        def _():
            add = lambda a, b: (a[...].astype(F32) + b[...].astype(F32)).astype(o_ref.dtype)
            o_ref[:, C_RQ:C_RK] = add(dqf_r, dqb_r)
            o_ref[:, C_RK:C_RV] = add(dkf_r, dkb_r)
            o_ref[:, C_RV:C_RG] = add(dvf_r, dvb_r)
            o_ref[:, C_RG:C_AQ] = drg_r[...].astype(o_ref.dtype)
            o_ref[:, C_AQ:C_AK] = daq_r[...].astype(o_ref.dtype)
            w0 = jnp.where(i > 0, 1.0, 0.0)
            w2 = jnp.where(i < n - 1, 1.0, 0.0)
            dk = kl0[0] * w0 + kl1[0] + kl2[0] * w2
            dv = vl0[0] * w0 + vl1[0] + vl2[0] * w2
            for t in range(ATT_KV // 2):
                sl = slice(t * LANES, (t + 1) * LANES)
                o_ref[:, C_AK + t * LANES:C_AK + (t + 1) * LANES] = _rope_t(
                    dk[:, sl], rp_r[0], rp_r[1], 16).astype(o_ref.dtype)
            o_ref[:, C_AV:D_PROJ] = dv.astype(o_ref.dtype)

        @pl.when(i >= n)
        def _():
            o_ref[:, C_RQ:C_RK] = jnp.zeros((BLK, C_RK - C_RQ), o_ref.dtype)
            o_ref[:, C_RK:C_RV] = dck_r[...].astype(o_ref.dtype)
            o_ref[:, C_RV:C_RG] = dcv_r[...].astype(o_ref.dtype)
            o_ref[:, C_RG:C_AK] = jnp.zeros((BLK, C_AK - C_RG), o_ref.dtype)
            o_ref[:, C_AK:C_AV] = dkx_r[...].astype(o_ref.dtype)
            o_ref[:, C_AV:D_PROJ] = dvx_r[...].astype(o_ref.dtype)

    xm = lambda i: jnp.minimum(i, n - 1)
    cm = lambda i: jnp.clip(i - n, 0, nc - 1)
    r512 = pl.BlockSpec((BLK, 512), lambda i: (xm(i), 0))
    r1024 = pl.BlockSpec((BLK, 1024), lambda i: (xm(i), 0))
    part = lambda off: pl.BlockSpec((1, BLK, 256), lambda i: (jnp.clip(xm(i) + off, 0, n - 1), 1 - off, 0))
    return pl.pallas_call(
        body, name="assemble_dp", grid=(n + nc,),
        in_specs=[r512, r512, r512, r512, r1024, r1024, r1024, r1024,
                  part(-1), part(0), part(1), part(-1), part(0), part(1),
                  pl.BlockSpec((2, BLK, LANES), lambda i: (0, xm(i), 0)),
                  pl.BlockSpec((BLK, 512), lambda i: (cm(i), 0)), pl.BlockSpec((BLK, 1024), lambda i: (cm(i), 0)),
                  pl.BlockSpec((BLK, 256), lambda i: (cm(i), 0)), pl.BlockSpec((BLK, 256), lambda i: (cm(i), 0))],
        out_specs=pl.BlockSpec((BLK, D_PROJ), lambda i: (i, 0)),
        out_shape=jax.ShapeDtypeStruct((L + Lc, D_PROJ), BF16),
        compiler_params=_cparams(("parallel",)),
    )(dqf, dqb, dkf, dkb, dvf, dvb, drg, daq, dkl, dkl, dkl, dvl, dvl, dvl, rope_att, dck, dcv, dkx, dvx)


def _adam_math(w, g, m, v):
    m = ADAM_B1 * m + (1.0 - ADAM_B1) * g
    v = ADAM_B2 * v + (1.0 - ADAM_B2) * (g * g)
    m_hat = m / (1.0 - ADAM_B1 ** ADAM_STEP)
    v_hat = v / (1.0 - ADAM_B2 ** ADAM_STEP)
    delta = -ADAM_LR * (m_hat / (jnp.sqrt(v_hat) + ADAM_EPS) + ADAM_WD * w)
    return delta, m, v


def _adam(name, w, m, v, g=None, parts=None):
    R, C = w.shape
    tr = _pick(R, (256, 192, 176, 128, 64, 32, 16, 8))
    summed = parts is not None
    n_parts = parts.shape[0] if summed else 0

    def body(w_ref, m_ref, v_ref, g_ref, go_ref, d_ref, mo_ref, vo_ref):
        if summed:
            gv = g_ref[0].astype(F32)
            for j in range(1, n_parts):
                gv = gv + g_ref[j].astype(F32)
        else:
            gv = g_ref[...]
        d, mn, vn = _adam_math(w_ref[...], gv, m_ref[...], v_ref[...])
        go_ref[...] = gv
        d_ref[...] = d
        mo_ref[...] = mn
        vo_ref[...] = vn

    row = pl.BlockSpec((tr, C), lambda i: (i, 0))
    gspec = pl.BlockSpec((n_parts, tr, C), lambda i: (0, i, 0)) if summed else row
    return pl.pallas_call(
        body, name=name, grid=(R // tr,),
        in_specs=[row, row, row, gspec], out_specs=[row] * 4,
        out_shape=[jax.ShapeDtypeStruct((R, C), F32)] * 4,
        compiler_params=_cparams(("parallel",)),
    )(w, m, v, parts if summed else g)


def _rows_full(g):
    _, R, D = g.shape
    return g.reshape(N_DEV * R, D)


def _rows_slots(g):
    N, D = g.shape
    return g.reshape(N_DEV, N // N_DEV, D)


def _pad_rows(a, rows):
    return jnp.concatenate([a, jnp.zeros((rows - a.shape[0],) + a.shape[1:], a.dtype)], axis=0)


def kernel(x, c, ctx, c_ctx, w_mod, b_mod, norm_mix, norm_ffn, w_in, ret_decay, attn_sink, w_out, w_gate, w_up, w_down, norm_final, loss_target, m_c_ctx, m_w_mod, m_b_mod, m_norm_mix, m_norm_ffn, m_w_in, m_ret_decay, m_attn_sink, m_w_out, m_w_gate, m_w_up, m_w_down, m_norm_final, v_c_ctx, v_w_mod, v_b_mod, v_norm_mix, v_norm_ffn, v_w_in, v_ret_decay, v_attn_sink, v_w_out, v_w_gate, v_w_up, v_w_down, v_norm_final):
    L, D = x.shape[1], x.shape[2]
    Lc = ctx.shape[1]
    DF = w_gate.shape[2] * N_DEV
    C6 = w_mod.shape[2]
    me = _my_id()
    xs, cx, tgt = x[0], ctx[0], loss_target[0]

    ag_in = ("ag2", w_in[0].T.astype(BF16))
    ag_out, ag_gate = ("ag2", w_out[0].astype(BF16)), ("ag2", w_gate[0].T.astype(BF16))
    ag_up, ag_down = ("ag2", w_up[0].T.astype(BF16)), ("ag2", w_down[0].astype(BF16))

    cs = _allgather(c, "ag_c")[:, 0, :]
    s_in = _pad_rows(jnp.concatenate([cs, c_ctx[None, :]], axis=0), 16)
    b_l = lax.dynamic_slice_in_dim(b_mod, me * C6, C6, axis=1)
    mod_parts = _allgather(_mod_fwd(s_in, w_mod[0], b_l), "ag_mod")
    mod = _pad_rows(lax.dynamic_index_in_dim(mod_parts, me, axis=1, keepdims=False).reshape(6, D), 8)
    modc = _pad_rows(mod_parts[:, N_DEV, :].reshape(6, D), 8)
    mix_mod, ffn_mod = mod, jnp.roll(mod, -3, axis=0)
    gt_m, gt_f = mod[2:3], mod[5:6]

    rope_ret, rope_att = _rope_tables(L)
    rdb = jnp.broadcast_to(ret_decay[0].reshape(2 * RET_HEADS, 1, 1), (2 * RET_HEADS, 1, LANES))
    sinkb = jnp.broadcast_to(attn_sink[0].reshape(ATT_HEADS, 1), (ATT_HEADS, LANES))

    tm = _pick(L + Lc, (1408, 768, 512, 384, 256, 128))
    tmx = _pick(L, (1024, 512, 256, 128))

    (H,), (g_in,) = _modulate_fwd("mod_mix_fwd", xs, cx, norm_mix, mix_mod, modc, comm=[ag_in])
    W_inT = _rows_full(g_in)
    ident = lambda a, e: a
    tP, tD, tF = _pick(D_PROJ, (1152, 768, 512)), _pick(D, (2048, 1024, 512)), _pick(DF, (512, 256, 128))
    (P,), (g_gate,) = _matmul("mm_in", [(H, W_inT, 0)], 1, L + Lc, D_PROJ, D, "nt",
                              (tm, _pick(D_PROJ, (1536, 768, 512)), D), [], [BF16], ident,
                              comm=[ag_gate], pass_on_at=0.85)
    W_gateT = _rows_full(g_gate)
    tabs, ctab = _ret_tables(rdb, Lc)
    s0 = _ret_ctx_state(P, ctab, L, Lc)
    (o_f, o_b, st_f, st_b), (g_out,) = _ret_fwd(P, rope_ret, tabs, s0, L, comm=[ag_out])
    W_out = _rows_full(g_out)
    Y_half = _ret_finish_fwd(o_f, o_b, P, L)
    (Y,), (g_up,) = _att_fwd(P, rope_att, sinkb, Y_half, L, Lc, comm=[ag_up])
    W_upT = _rows_full(g_up)
    KO = Y.shape[1]
    f_mix = _matmul("mm_out", [(Y, W_out, 0)], 1, L, D, KO, "nn", (tmx, tD, KO), [], [BF16], ident)[0]

    x1, H2 = _residual_modulate_fwd("mod_ffn_fwd", xs, f_mix, gt_m, norm_ffn, ffn_mod)

    def swiglu_epi(a, e):
        sg = _sigmoid(a[0])
        act = a[0] * sg
        return [act, a[1] * (sg * (1.0 + a[0] * (1.0 - sg))), act * a[1]]

    tm2 = tmx
    (act, up_dact, hmid), (g_down,) = _matmul("mm_gate_up", [(H2, W_gateT, 0), (H2, W_upT, 1)], 2, L, DF, D, "nt",
                                              (_pick(L, (2048, 1024, 512, 256, 128)), tF, D), [],
                                              [BF16, BF16, BF16], swiglu_epi, comm=[ag_down])
    W_down = _rows_full(g_down)
    f_ffn = _matmul("mm_down", [(hmid, W_down, 0)], 1, L, D, DF, "nn",
                    (tm2, _pick(D, (1024, 512)), _pick(DF, (2816, 512, 256, 128))), [], [BF16], ident)[0]

    dx2, dFf, sums_l = _loss_head(x1, tgt, norm_final.reshape(1, D), f_ffn, gt_f)

    def dswiglu_epi(a, e):
        return [a[0] * e[0].astype(F32), a[0] * e[1].astype(F32)]

    dga, dup = _matmul("mm_d_down", [(dFf, W_down, 0)], 1, L, DF, D, "nt", (_pick(L, (2048, 1024, 512, 256, 128)), tF, D),
                       [(up_dact, "mn"), (act, "mn")], [BF16, BF16], dswiglu_epi)
    tkt, tkl = _pick(L, (512, 256, 128)), _pick(L, (1024, 512, 256, 128))
    dW_down = _matmul("mm_gw_down", [(hmid, dFf, 0)], 1, DF, D, L, "tn",
                      (_pick(DF, (1408, 512, 256, 128)), tD, tkl), [], [BF16], ident)[0]
    (dW_gateT, dW_upT), (p_down,) = _matmul("mm_gw_gate_up", [(dga, H2, 0), (dup, H2, 1)], 2, DF, D, L, "tn",
                                            (tF, tD, _pick(L, (2048, 1024, 512, 256, 128))), [], [BF16, BF16], ident,
                                            comm=[("a2a", _rows_slots(dW_down))])
    (dH2,), (p_gate,) = _matmul("mm_d_gate_up", [(dga, W_gateT, 0), (dup, W_upT, 0)], 1, L, D, DF, "nn",
                                (_pick(L, (2048, 1024, 512, 256, 128)), tD, tF), [], [BF16], ident,
                                comm=[("a2a", _rows_slots(dW_gateT))])
    dx1, dFm, sums_f = _modulate_bwd("mod_ffn_bwd", x1, None, dH2, norm_ffn, ffn_mod, None, dx2, f_mix, gt_m)

    tO = _pick(KO, (2048, 1024, 512))
    dY = _matmul("mm_d_out", [(dFm, W_out, 0)], 1, L, KO, D, "nt", (tmx, tO, D), [], [BF16], ident)[0]
    dW_out = _matmul("mm_gw_out", [(Y, dFm, 0)], 1, KO, D, L, "tn",
                     (_pick(KO, (1024, 512)), tD, _pick(L, (2048, 1024, 512, 256, 128))), [], [BF16],
                     ident)[0]
    dO, drg = _ret_finish_bwd(o_f, o_b, P, dY, L)
    (dqf, dkf, dvf, dqb, dkb, dvb, ds0, dlg), (p_out,) = _ret_bwd(
        P, rope_ret, tabs, st_f, st_b, dO, L, comm=[("a2a", _rows_slots(dW_out))])
    dck, dcv, d_rd = _ret_ctx_bwd(P, ctab, ds0, dlg, rdb, L, Lc)
    (daq, dkl, dvl, dkx, dvx, d_sink), (p_up,) = _att_bwd(
        P, rope_att, sinkb, Y, dY, L, Lc, comm=[("a2a", _rows_slots(dW_upT))])
    dP = _assemble_dp(L, Lc, dqf, dqb, dkf, dkb, dvf, dvb, drg, daq, dkl, dvl, rope_att, dck, dcv, dkx, dvx)
    tkc = _pick(L + Lc, (768, 256, 128))
    dW_inT = _matmul("mm_gw_in", [(dP, H, 0)], 1, D_PROJ, D, L + Lc, "tn",
                     (_pick(D_PROJ, (2304, 1152, 768, 512)), tD, tkc), [], [BF16], ident)[0]
    (dH,), (p_in,) = _matmul("mm_d_in", [(dP, W_inT, 0)], 1, L + Lc, D, D_PROJ, "nn",
                             (tm, tD, _pick(D_PROJ, (768, 512, 256))), [], [BF16], ident,
                             comm=[("a2a", _rows_slots(dW_inT))])
    grad_x, sums_m = _modulate_bwd("mod_mix_bwd", xs, cx, dH, norm_mix, mix_mod, modc, dx1, None, None)

    zero = jnp.zeros((1, D), F32)
    dmod = jnp.concatenate([sums_m[0:1], sums_m[1:2], sums_f[6:7], sums_f[0:1], sums_f[1:2], sums_l[2:3]], axis=1)
    dmodc = jnp.concatenate([sums_m[3:4], sums_m[4:5], zero, zero, zero, zero], axis=1)
    dm_all = _allgather(jnp.concatenate([dmod, dmodc], axis=0), "ag_dmod")
    dm_cols = lax.dynamic_slice_in_dim(dm_all, me * C6, C6, axis=2)
    dm_in = jnp.concatenate([dm_cols[:, 0, :], dm_cols[:, 1, :]], axis=0)
    s_bwd = jnp.concatenate([cs, jnp.broadcast_to(c_ctx[None, :], (N_DEV, D))], axis=0)
    g_w_mod, dsil = _mod_bwd(s_bwd, dm_in, w_mod[0])

    lane_pad = lambda a: _pad_rows(a.reshape(-1, 1), LANES).reshape(1, LANES)
    pack = jnp.concatenate([dsil[0:1], sums_m[2:3], sums_f[2:3], sums_l[1:2],
                            lane_pad(d_rd[:, 0]), lane_pad(d_sink[:, 0]), sums_l[3:4, 0:LANES]], axis=1)
    packs = _allgather(pack, "ag_small")
    zl = jnp.zeros((1, LANES), F32)

    def pack_w(a_c, a_nm, a_nf, a_fin, a_rd, a_sk):
        return jnp.concatenate([a_c.reshape(1, D), a_nm, a_nf, a_fin.reshape(1, D), lane_pad(a_rd.reshape(-1)),
                                lane_pad(a_sk.reshape(-1)), zl], axis=1)

    sg, sd, sm, sv = _adam("adam_small", pack_w(c_ctx, norm_mix, norm_ffn, norm_final, ret_decay, attn_sink),
                           pack_w(m_c_ctx, m_norm_mix, m_norm_ffn, m_norm_final, m_ret_decay, m_attn_sink),
                           pack_w(v_c_ctx, v_norm_mix, v_norm_ffn, v_norm_final, v_ret_decay, v_attn_sink),
                           parts=packs)
    loss = sg[0, 4 * D + 2 * LANES]

    def unpack(a):
        return (a[0, 0:D], a[:, D:2 * D], a[:, 2 * D:3 * D], a[0, 3 * D:4 * D],
                a[0, 4 * D:4 * D + 2 * RET_HEADS].reshape(1, 2, RET_HEADS),
                a[:, 4 * D + LANES:4 * D + LANES + ATT_HEADS])

    bg, bd, bm, bv = _adam("adam_b_mod", b_mod, m_b_mod, v_b_mod, parts=dm_all.reshape(2 * N_DEV, 1, 6 * D))
    wg, wd, wm, wv = _adam("adam_w_mod", w_mod[0], m_w_mod[0], v_w_mod[0], g=g_w_mod)

    big = {}
    for nm, w, m, v, parts, transposed in (
            ("w_in", w_in, m_w_in, v_w_in, p_in, True), ("w_out", w_out, m_w_out, v_w_out, p_out, False),
            ("w_gate", w_gate, m_w_gate, v_w_gate, p_gate, True), ("w_up", w_up, m_w_up, v_w_up, p_up, True),
            ("w_down", w_down, m_w_down, v_w_down, p_down, False)):
        if transposed:
            res = [a.T for a in _adam("adam_" + nm, w[0].T, m[0].T, v[0].T, parts=parts)]
        else:
            res = _adam("adam_" + nm, w[0], m[0], v[0], parts=parts)
        big[nm] = [a[None] for a in res]

    g_s, d_s, m_s, v_s = unpack(sg), unpack(sd), unpack(sm), unpack(sv)

    def leaves(k, small, bmod, wmod):
        return (small[0], wmod[None], bmod, small[1], small[2], big["w_in"][k], small[4], small[5],
                big["w_out"][k], big["w_gate"][k], big["w_up"][k], big["w_down"][k], small[3])

    return (loss, grad_x[None], *leaves(0, g_s, bg, wg), *leaves(1, d_s, bd, wd),
            *leaves(2, m_s, bm, wm), *leaves(3, v_s, bv, wv))
```

```python
import functools

import jax
import jax.numpy as jnp
from jax import lax
from jax.experimental import pallas as pl
from jax.experimental.pallas import tpu as pltpu

F32 = jnp.float32
BF16 = jnp.bfloat16

N_DEV = 8
LANES = 128
RET_HEADS = 8
RET_DK = 64
RET_DV = 128
CHUNK = 128
ATT_HEADS = 16
ATT_KV = 4
ATT_DH = 64
GRID_W = 64
ROPE_BASE = 10000.0
EPS = 1e-6
NEG = -1e30
C_RQ, C_RK, C_RV, C_RG, C_AQ, C_AK, C_AV, D_PROJ = 0, 512, 1024, 2048, 3072, 4096, 4352, 4608
K_SCALE = RET_DK ** -0.5
A_SCALE = ATT_DH ** -0.5

ADAM_LR, ADAM_B1, ADAM_B2, ADAM_EPS, ADAM_WD, ADAM_STEP = 0.001, 0.9, 0.999, 1e-08, 0.01, 10

VMEM_BIG = 58 * 1024 * 1024

NN = (((1,), (0,)), ((), ()))
NT = (((1,), (1,)), ((), ()))
TN = (((0,), (0,)), ((), ()))


def _dot(a, b, dims):
    return lax.dot_general(a, b, dims, preferred_element_type=F32)


def _cparams(sem, vmem=VMEM_BIG):
    return pltpu.CompilerParams(dimension_semantics=sem, vmem_limit_bytes=vmem)


def _pick(dim, prefs):
    for p in prefs:
        if dim % p == 0:
            return p
    return dim


def _my_id():
    return lax.axis_index("x") * 4 + lax.axis_index("y") * 2 + lax.axis_index("c")


def _sigmoid(x):
    return 0.5 * jnp.tanh(0.5 * x) + 0.5


def _peers():
    mx, my, mc = lax.axis_index("x"), lax.axis_index("y"), lax.axis_index("c")
    out = []
    for k in range(1, N_DEV):
        kx, ky, kc = (k >> 2) & 1, (k >> 1) & 1, k & 1
        px = 1 - mx if kx else mx
        py = 1 - my if ky else my
        pc = 1 - mc if kc else mc
        out.append(((px, py, pc), px * 4 + py * 2 + pc))
    return out


def _exchange_copies(kind, x_ref, o_ref, ssem, rsem, lsem):
    me = _my_id()
    loc = pltpu.make_async_copy(x_ref if kind == "ag" else x_ref.at[me], o_ref.at[me], lsem)
    cps = []
    for k, (peer, pid) in enumerate(_peers()):
        cps.append(pltpu.make_async_remote_copy(
            src_ref=x_ref if kind == "ag" else x_ref.at[pid], dst_ref=o_ref.at[me],
            send_sem=ssem.at[k], recv_sem=rsem.at[k], device_id=peer, device_id_type=pl.DeviceIdType.MESH))
    return loc, cps


def _two_level_copies(x_ref, o_ref, ssem, rsem, lsem):
    mx, my, mc = lax.axis_index("x"), lax.axis_index("y"), lax.axis_index("c")
    me = mx * 4 + my * 2 + mc
    sibling = (mx, my, 1 - mc)
    chips = [(1 - mx, my), (mx, 1 - my), (1 - mx, 1 - my)]

    def copy(k, slot, to, src=None):
        return pltpu.make_async_remote_copy(
            src_ref=o_ref.at[slot] if src is None else src, dst_ref=o_ref.at[slot],
            send_sem=ssem.at[k], recv_sem=rsem.at[k], device_id=to, device_id_type=pl.DeviceIdType.MESH)

    loc = pltpu.make_async_copy(x_ref, o_ref.at[me], lsem)
    first = [copy(0, me, sibling, src=x_ref)]
    first += [copy(1 + j, me, (cx, cy, mc), src=x_ref) for j, (cx, cy) in enumerate(chips)]
    passed = [copy(4 + j, cx * 4 + cy * 2 + mc, sibling) for j, (cx, cy) in enumerate(chips)]
    return loc, first, passed


def _exchange_start(kind, x_ref, o_ref, ssem, rsem, lsem):
    if kind == "ag2":
        loc, first, _ = _two_level_copies(x_ref, o_ref, ssem, rsem, lsem)
        cps = first
    else:
        loc, cps = _exchange_copies(kind, x_ref, o_ref, ssem, rsem, lsem)
    loc.start()
    for cp in cps:
        cp.start()


def _exchange_pass_on(kind, x_ref, o_ref, ssem, rsem, lsem):
    if kind == "ag2":
        _, first, passed = _two_level_copies(x_ref, o_ref, ssem, rsem, lsem)
        for j in range(3):
            first[1 + j].wait_recv()
            passed[j].start()


def _exchange_wait(kind, x_ref, o_ref, ssem, rsem, lsem):
    if kind == "ag2":
        loc, first, passed = _two_level_copies(x_ref, o_ref, ssem, rsem, lsem)
        first[0].wait_recv()
        for cp in passed:
            cp.wait_recv()
        cps = first + passed
    else:
        loc, cps = _exchange_copies(kind, x_ref, o_ref, ssem, rsem, lsem)
        for cp in cps:
            cp.wait_recv()
    for cp in cps:
        cp.wait_send()
    loc.wait()


_EXCHANGE_SEMS = [pltpu.SemaphoreType.DMA((N_DEV - 1,)), pltpu.SemaphoreType.DMA((N_DEV - 1,)),
                  pltpu.SemaphoreType.DMA(())]


def _exchange_shape(kind, x):
    return jax.ShapeDtypeStruct(x.shape if kind == "a2a" else (N_DEV,) + x.shape, x.dtype)


def _exchange(kind, x, name):
    def body(x_ref, o_ref, ssem, rsem, lsem):
        _exchange_start(kind, x_ref, o_ref, ssem, rsem, lsem)
        _exchange_pass_on(kind, x_ref, o_ref, ssem, rsem, lsem)
        _exchange_wait(kind, x_ref, o_ref, ssem, rsem, lsem)

    return pl.pallas_call(
        body, name=name, out_shape=_exchange_shape(kind, x),
        in_specs=[pl.BlockSpec(memory_space=pl.ANY)], out_specs=pl.BlockSpec(memory_space=pl.ANY),
        scratch_shapes=list(_EXCHANGE_SEMS),
    )(x)


def _allgather(x, name):
    return _exchange("ag", x, name)


def _call(body, name, grid, in_specs, out_specs, out_shape, scratch_shapes, sem, args, comm=(), aliases=None,
          pass_on_at=0.75):
    in_specs, out_specs, out_shape = list(in_specs), list(out_specs), list(out_shape)
    scratch_shapes = list(scratch_shapes)
    aliases = aliases or {}
    if not comm:
        outs = pl.pallas_call(body, name=name, grid=grid, in_specs=in_specs, out_specs=out_specs, out_shape=out_shape,
                              scratch_shapes=scratch_shapes, input_output_aliases=aliases,
                              compiler_params=_cparams(sem))(*args)
        return list(outs), []
    n_in, n_out, n_scr, n_c = len(in_specs), len(out_specs), len(scratch_shapes), len(comm)
    hbm = pl.BlockSpec(memory_space=pl.ANY)

    def wrapped(*refs):
        ins, cins = refs[:n_in], refs[n_in:n_in + n_c]
        outs = refs[n_in + n_c:n_in + n_c + n_out]
        couts = refs[n_in + n_c + n_out:n_in + 2 * n_c + n_out]
        scr = refs[n_in + 2 * n_c + n_out:n_in + 2 * n_c + n_out + n_scr]
        sems = refs[n_in + 2 * n_c + n_out + n_scr:]
        step, total = pl.program_id(0), grid[0]
        for ax in range(1, len(grid)):
            step = step * grid[ax] + pl.program_id(ax)
            total *= grid[ax]

        @pl.when(step == 0)
        def _():
            for c, (kind, _) in enumerate(comm):
                _exchange_start(kind, cins[c], couts[c], *sems[3 * c:3 * c + 3])

        body(*ins, *outs, *scr)

        @pl.when(step == min(total - 1, int(total * pass_on_at)))
        def _():
            for c, (kind, _) in enumerate(comm):
                _exchange_pass_on(kind, cins[c], couts[c], *sems[3 * c:3 * c + 3])

        @pl.when(step == total - 1)
        def _():
            for c, (kind, _) in enumerate(comm):
                _exchange_wait(kind, cins[c], couts[c], *sems[3 * c:3 * c + 3])

    res = pl.pallas_call(
        wrapped, name=name, grid=grid,
        in_specs=in_specs + [hbm] * n_c, out_specs=out_specs + [hbm] * n_c,
        out_shape=out_shape + [_exchange_shape(kind, arr) for kind, arr in comm],
        scratch_shapes=scratch_shapes + list(_EXCHANGE_SEMS) * n_c, input_output_aliases=aliases,
        compiler_params=_cparams(("arbitrary",) * len(grid)),
    )(*args, *[arr for _, arr in comm])
    return list(res[:n_out]), list(res[n_out:])


def _matmul(name, pairs, n_acc, M, N, K, mode, tiles, extras, out_dtypes, epilogue, j_outer=False, comm=(),
            pass_on_at=0.75):
    tm, tn, tk = tiles
    gm, gn, nk = M // tm, N // tn, K // tk
    assert gm * tm == M and gn * tn == N and nk * tk == K, (name, M, N, K, tiles)
    if j_outer:
        grid = (gn, gm, nk)
        ij = lambda g0, g1: (g1, g0)
    else:
        grid = (gm, gn, nk)
        ij = lambda g0, g1: (g0, g1)

    if mode in ("nn", "nt"):
        a_spec = pl.BlockSpec((tm, tk), lambda g0, g1, k: (ij(g0, g1)[0], k))
    else:
        a_spec = pl.BlockSpec((tk, tm), lambda g0, g1, k: (k, ij(g0, g1)[0]))
    if mode == "nt":
        b_spec = pl.BlockSpec((tn, tk), lambda g0, g1, k: (ij(g0, g1)[1], k))
    else:
        b_spec = pl.BlockSpec((tk, tn), lambda g0, g1, k: (k, ij(g0, g1)[1]))
    dims = {"nn": NN, "nt": NT, "tn": TN}[mode]
    mn_spec = pl.BlockSpec((tm, tn), lambda g0, g1, k: ij(g0, g1))
    n_spec = pl.BlockSpec((1, tn), lambda g0, g1, k: (0, ij(g0, g1)[1]))

    in_specs, args = [], []
    for a, b, _ in pairs:
        in_specs += [a_spec, b_spec]
        args += [a, b]
    for arr, kind in extras:
        in_specs.append(mn_spec if kind == "mn" else n_spec)
        args.append(arr)
    n_p, n_e, n_o = len(pairs), len(extras), len(out_dtypes)

    def body(*refs):
        ab = refs[:2 * n_p]
        ex = refs[2 * n_p:2 * n_p + n_e]
        outs = refs[2 * n_p + n_e:2 * n_p + n_e + n_o]
        accs = refs[2 * n_p + n_e + n_o:]
        k = pl.program_id(2)

        def single_step():
            rc = _pick(tm, (1024,)) if mode != "tn" else tm
            for c in range(tm // rc):
                rs = slice(c * rc, (c + 1) * rc)
                sums = [None] * n_acc
                for p, (_, _, ai) in enumerate(pairs):
                    a_ref, b_ref = ab[2 * p], ab[2 * p + 1]
                    d = _dot(a_ref[...] if mode == "tn" else a_ref[rs, :], b_ref[...], dims)
                    sums[ai] = d if sums[ai] is None else sums[ai] + d
                res = epilogue(sums, [e[rs, :] if e.shape[0] == tm else e[...] for e in ex])
                for o, r in zip(outs, res):
                    o[rs, :] = r.astype(o.dtype)

        def finish(acc_vals):
            res = epilogue(acc_vals, [e[...] for e in ex])
            for o, r in zip(outs, res):
                o[...] = r.astype(o.dtype)

        def accumulate(first):
            w = _pick(tm if mode == "tn" else tn, (512, 384, 256))
            for c in range((tm if mode == "tn" else tn) // w):
                sl = slice(c * w, (c + 1) * w)
                sums = [None] * n_acc
                for p, (_, _, ai) in enumerate(pairs):
                    a_ref, b_ref = ab[2 * p], ab[2 * p + 1]
                    if mode == "tn":
                        d = _dot(a_ref[:, sl], b_ref[...], dims)
                    elif mode == "nn":
                        d = _dot(a_ref[...], b_ref[:, sl], dims)
                    else:
                        d = _dot(a_ref[...], b_ref[sl, :], dims)
                    sums[ai] = d if sums[ai] is None else sums[ai] + d
                idx = (sl, slice(None)) if mode == "tn" else (slice(None), sl)
                for ai, s in enumerate(sums):
                    if first:
                        accs[ai][idx] = s
                    else:
                        accs[ai][idx] += s

        if nk == 1:
            single_step()
        else:
            pl.when(k == 0)(functools.partial(accumulate, True))
            pl.when(k > 0)(functools.partial(accumulate, False))

            @pl.when(k == nk - 1)
            def _():
                finish([a[...] for a in accs])

    outs, couts = _call(
        body, name, grid, in_specs, [mn_spec] * n_o,
        [jax.ShapeDtypeStruct((M, N), dt) for dt in out_dtypes],
        [pltpu.VMEM((tm, tn), F32) for _ in range(n_acc if nk > 1 else 0)],
        ("parallel", "parallel", "arbitrary"), args, comm, pass_on_at=pass_on_at)
    return (outs, couts) if comm else outs


def _rope_tables(L):
    t = jnp.arange(L, dtype=jnp.int32)
    f = jnp.arange(32, dtype=jnp.int32).astype(F32)
    ang = t.astype(F32)[:, None] * (ROPE_BASE ** (-f / 32.0))[None, :]
    cos, sin = jnp.cos(ang), jnp.sin(ang)
    ret = jnp.stack([jnp.tile(cos, (1, 4)), jnp.tile(jnp.concatenate([-sin, sin], axis=1), (1, 2))])
    f2 = jnp.arange(16, dtype=jnp.int32).astype(F32)
    inv2 = (ROPE_BASE ** (-f2 / 16.0))[None, :]
    ang_r = (t // GRID_W).astype(F32)[:, None] * inv2
    ang_c = (t % GRID_W).astype(F32)[:, None] * inv2
    cr, sr, cc, sc = jnp.cos(ang_r), jnp.sin(ang_r), jnp.cos(ang_c), jnp.sin(ang_c)
    att = jnp.stack([jnp.tile(jnp.concatenate([cr, cr, cc, cc], axis=1), (1, 2)),
                     jnp.tile(jnp.concatenate([-sr, sr, -sc, sc], axis=1), (1, 2))])
    return ret.astype(F32), att.astype(F32)


def _swap(x, sh):
    lane = lax.broadcasted_iota(jnp.int32, x.shape, 1)
    ra = pltpu.roll(x, LANES - sh, 1)
    rb = pltpu.roll(x, sh, 1)
    la = pltpu.roll(lane, LANES - sh, 1)
    partner = jnp.where((lane % (2 * sh)) < sh, lane + sh, lane - sh)
    return jnp.where(la == partner, ra, rb)


def _rope(x, cos, sin, sh):
    return x * cos + _swap(x, sh) * sin


def _rope_t(d, cos, sin, sh):
    return d * cos + _swap(d * sin, sh)


def _half_mask(shape, a):
    lane = lax.broadcasted_iota(jnp.int32, shape, 1)
    return (lane < 64) if a == 0 else (lane >= 64)


def _mod_fwd(s_in, w_l, b_l):
    D, C6 = w_l.shape
    tk = _pick(D, (512, 256, 128))
    nk = D // tk

    def body(s_ref, w_ref, b_ref, o_ref):
        k = pl.program_id(0)
        s = s_ref[...]
        s = s * _sigmoid(s)
        d = jnp.dot(s, w_ref[...], preferred_element_type=F32, precision=lax.Precision.HIGHEST)

        @pl.when(k == 0)
        def _():
            o_ref[...] = d + b_ref[...]

        @pl.when(k > 0)
        def _():
            o_ref[...] += d

    return pl.pallas_call(
        body, name="mod_fwd", grid=(nk,),
        in_specs=[pl.BlockSpec((16, tk), lambda k: (0, k)), pl.BlockSpec((tk, C6), lambda k: (k, 0)),
                  pl.BlockSpec((1, C6), lambda k: (0, 0))],
        out_specs=pl.BlockSpec((16, C6), lambda k: (0, 0)),
        out_shape=jax.ShapeDtypeStruct((16, C6), F32),
        compiler_params=_cparams(("arbitrary",)),
    )(s_in, w_l, b_l)


def _mod_bwd(s_in, dm, w_l):
    D, C6 = w_l.shape
    tk = _pick(D, (512, 256, 128))
    nk = D // tk

    def body(s_ref, dm_ref, w_ref, gw_ref, gc_ref):
        s = s_ref[...]
        sg = _sigmoid(s)
        act = s * sg
        dmv = dm_ref[...]
        gw_ref[...] = lax.dot_general(act, dmv, TN, preferred_element_type=F32, precision=lax.Precision.HIGHEST)
        ds = lax.dot_general(dmv, w_ref[...], NT, preferred_element_type=F32, precision=lax.Precision.HIGHEST)
        dsil = (sg * (1.0 + s * (1.0 - sg)))[8:9, :]
        gc_ref[...] = jnp.zeros((8, tk), F32) + jnp.sum(ds[8:16, :], axis=0, keepdims=True) * dsil

    return pl.pallas_call(
        body, name="mod_bwd", grid=(nk,),
        in_specs=[pl.BlockSpec((16, tk), lambda k: (0, k)), pl.BlockSpec((16, C6), lambda k: (0, 0)),
                  pl.BlockSpec((tk, C6), lambda k: (k, 0))],
        out_specs=[pl.BlockSpec((tk, C6), lambda k: (k, 0)), pl.BlockSpec((8, tk), lambda k: (0, k))],
        out_shape=[jax.ShapeDtypeStruct((D, C6), F32), jax.ShapeDtypeStruct((8, D), F32)],
        compiler_params=_cparams(("parallel",)),
    )(s_in, dm, w_l)


def _resident(shape):
    return pl.BlockSpec(shape, lambda *_: (0,) * len(shape), pipeline_mode=pl.Buffered(1))


def _norm_rows(x):
    r = lax.rsqrt(jnp.mean(x * x, axis=-1, keepdims=True) + EPS)
    return x * r, r


def _modulate_fwd(name, x, ctx, g, mod, modc, comm=()):
    L, D = x.shape
    tr = ctx.shape[0]
    nx = L // tr

    def body(x_ref, c_ref, g_ref, m_ref, mc_ref, o_ref):
        i = pl.program_id(0)

        def run(src, m):
            n, _ = _norm_rows(src[...])
            o_ref[...] = (n * g_ref[...] * (1.0 + m[1:2, :]) + m[0:1, :]).astype(o_ref.dtype)

        @pl.when(i < nx)
        def _():
            run(x_ref, m_ref)

        @pl.when(i >= nx)
        def _():
            run(c_ref, mc_ref)

    row = pl.BlockSpec((tr, D), lambda i: (jnp.minimum(i, nx - 1), 0))
    vec = pl.BlockSpec((1, D), lambda i: (0, 0))
    mv = pl.BlockSpec((8, D), lambda i: (0, 0))
    return _call(
        body, name, (nx + 1,), [row, pl.BlockSpec((tr, D), lambda i: (0, 0)), vec, mv, mv],
        [pl.BlockSpec((tr, D), lambda i: (i, 0))], [jax.ShapeDtypeStruct((L + tr, D), BF16)], [],
        ("parallel",), (x, ctx, g, mod, modc), comm, pass_on_at=1.0)


def _residual_modulate_fwd(name, x, fbr, gate, g, mod):
    L, D = x.shape
    tr = _pick(L, (512, 256, 128))

    def body(x_ref, f_ref, gt_ref, g_ref, m_ref, x1_ref, o_ref):
        x1 = x_ref[...] + gt_ref[...] * f_ref[...].astype(F32)
        x1_ref[...] = x1
        n, _ = _norm_rows(x1)
        o_ref[...] = (n * g_ref[...] * (1.0 + m_ref[1:2, :]) + m_ref[0:1, :]).astype(o_ref.dtype)

    row = pl.BlockSpec((tr, D), lambda i: (i, 0))
    vec = pl.BlockSpec((1, D), lambda i: (0, 0))
    return pl.pallas_call(
        body, name=name, grid=(L // tr,),
        in_specs=[row, row, vec, vec, pl.BlockSpec((8, D), lambda i: (0, 0))],
        out_specs=[row, row],
        out_shape=[jax.ShapeDtypeStruct((L, D), F32), jax.ShapeDtypeStruct((L, D), BF16)],
        compiler_params=_cparams(("parallel",)),
    )(x, fbr, gate, g, mod)


def _modulate_bwd(name, x, ctx, dh, g, mod, modc, dres, fbr, gate):
    L, D = x.shape
    tr = ctx.shape[0] if ctx is not None else _pick(L, (512, 256, 128))
    nx = L // tr
    nt = nx + (1 if ctx is not None else 0)
    has_f = fbr is not None

    def body(*refs):
        refs = list(refs)
        x_ref = refs.pop(0)
        c_ref = refs.pop(0) if ctx is not None else None
        dh_ref, g_ref, m_ref = refs.pop(0), refs.pop(0), refs.pop(0)
        mc_ref = refs.pop(0) if ctx is not None else None
        dr_ref = refs.pop(0)
        f_ref = refs.pop(0) if has_f else None
        gt_ref = refs.pop(0) if has_f else None
        dx_ref = refs.pop(0)
        df_ref = refs.pop(0) if has_f else None
        acc_ref = refs.pop(0)
        i = pl.program_id(0)

        @pl.when(i == 0)
        def _():
            acc_ref[...] = jnp.zeros_like(acc_ref)

        def sums(src, m, base, grow):
            n, r = _norm_rows(src[...])
            d = dh_ref[...].astype(F32)
            gg = g_ref[...]
            sc1 = 1.0 + m[1:2, :]
            acc_ref[base:base + 1, :] += jnp.sum(d, axis=0, keepdims=True)
            dn = d * n
            acc_ref[base + 1:base + 2, :] += jnp.sum(dn, axis=0, keepdims=True) * gg
            acc_ref[grow:grow + 1, :] += jnp.sum(dn, axis=0, keepdims=True) * sc1
            dnv = d * (gg * sc1)
            return r * (dnv - n * jnp.mean(dnv * n, axis=-1, keepdims=True))

        def x_rows():
            dx = sums(x_ref, m_ref, 0, 2) + dr_ref[...]
            dx_ref[...] = dx
            if has_f:
                acc_ref[6:7, :] += jnp.sum(dx * f_ref[...].astype(F32), axis=0, keepdims=True)
                df_ref[...] = (dx * gt_ref[...]).astype(df_ref.dtype)

        if ctx is None:
            x_rows()
        else:
            pl.when(i < nx)(x_rows)

            @pl.when(i >= nx)
            def _():
                sums(c_ref, mc_ref, 3, 2)

    row = pl.BlockSpec((tr, D), lambda i: (jnp.minimum(i, nx - 1), 0))
    vec = pl.BlockSpec((1, D), lambda i: (0, 0))
    mv = pl.BlockSpec((8, D), lambda i: (0, 0))
    in_specs, args = [row], [x]
    if ctx is not None:
        in_specs.append(pl.BlockSpec((tr, D), lambda i: (0, 0)))
        args.append(ctx)
    in_specs += [pl.BlockSpec((tr, D), lambda i: (i, 0)), vec, mv]
    args += [dh, g, mod]
    if ctx is not None:
        in_specs.append(mv)
        args.append(modc)
    in_specs.append(row)
    args.append(dres)
    out_specs = [row]
    out_shape = [jax.ShapeDtypeStruct((L, D), F32)]
    if has_f:
        in_specs += [row, vec]
        args += [fbr, gate]
        out_specs.append(row)
        out_shape.append(jax.ShapeDtypeStruct((L, D), BF16))
    out_specs.append(pl.BlockSpec((16, D), lambda i: (0, 0)))
    out_shape.append(jax.ShapeDtypeStruct((16, D), F32))
    return pl.pallas_call(
        body, name=name, grid=(nt,), in_specs=in_specs, out_specs=out_specs, out_shape=out_shape,
        compiler_params=_cparams(("arbitrary",)),
    )(*args)


def _loss_head(x1, tgt, nf, fbr, gate):
    L, D = x1.shape
    tr = _pick(L, (512, 256, 128))

    def body(x_ref, t_ref, w_ref, f_ref, gt_ref, dx_ref, df_ref, acc_ref):
        i = pl.program_id(0)

        @pl.when(i == 0)
        def _():
            acc_ref[...] = jnp.zeros_like(acc_ref)

        n, r = _norm_rows(x_ref[...] + gt_ref[...] * f_ref[...].astype(F32))
        w = w_ref[...]
        e = n * w - t_ref[...]
        acc_ref[0:1, :] += jnp.sum(e * e, axis=0, keepdims=True) * (0.5 / D)
        dout = e * (1.0 / D)
        acc_ref[1:2, :] += jnp.sum(dout * n, axis=0, keepdims=True)
        dn = dout * w
        dx = r * (dn - n * jnp.mean(dn * n, axis=-1, keepdims=True))
        dx_ref[...] = dx
        acc_ref[2:3, :] += jnp.sum(dx * f_ref[...].astype(F32), axis=0, keepdims=True)
        df_ref[...] = (dx * gt_ref[...]).astype(df_ref.dtype)

        @pl.when(i == pl.num_programs(0) - 1)
        def _():
            acc_ref[3:4, :] = jnp.zeros((1, D), F32) + jnp.sum(acc_ref[0:1, :])

    row = pl.BlockSpec((tr, D), lambda i: (i, 0))
    vec = pl.BlockSpec((1, D), lambda i: (0, 0))
    return pl.pallas_call(
        body, name="loss_head", grid=(L // tr,),
        in_specs=[row, row, vec, row, vec],
        out_specs=[row, row, pl.BlockSpec((8, D), lambda i: (0, 0))],
        out_shape=[jax.ShapeDtypeStruct((L, D), F32), jax.ShapeDtypeStruct((L, D), BF16),
                   jax.ShapeDtypeStruct((8, D), F32)],
        compiler_params=_cparams(("arbitrary",)),
    )(x1, tgt, nf, fbr, gate)


RET_SUB = 4
N_TAB = 7


def _ret_tables(rdb, Lc):
    def body(rd_ref, t_ref, c_ref):
        d = pl.program_id(0) // RET_HEADS
        fwd = d == 0
        lg = -jnp.exp(rd_ref[0])
        i = lax.broadcasted_iota(jnp.int32, (CHUNK, CHUNK), 0).astype(F32)
        j = lax.broadcasted_iota(jnp.int32, (CHUNK, CHUNK), 1).astype(F32)
        rel = jnp.where(fwd, i - j, j - i)
        mask = (rel > 0.0) | ((rel == 0.0) & fwd)
        dm = jnp.where(mask, jnp.exp(lg * jnp.maximum(rel, 0.0)), 0.0)
        t_ref[0, 0] = dm
        t_ref[0, 1] = rel * dm
        qc = jnp.where(fwd, i + 1.0, CHUNK - i)
        qw = jnp.exp(lg * qc)
        t_ref[0, 2] = qw
        t_ref[0, 3] = qw * qc
        kc = jnp.where(fwd, CHUNK - 1.0 - i, i)
        kw = jnp.exp(lg * kc)
        t_ref[0, 4] = kw
        t_ref[0, 5] = kw * kc
        t_ref[0, 6] = jnp.exp(lg * float(CHUNK)) + jnp.zeros((CHUNK, CHUNK), F32)
        m = lax.broadcasted_iota(jnp.int32, (Lc, LANES), 0).astype(F32)
        cc = jnp.where(fwd, Lc - 1.0 - m, m)
        cw = jnp.exp(lg * cc)
        c_ref[0, 0] = cw
        c_ref[0, 1] = cw * cc

    return pl.pallas_call(
        body, name="ret_tables", grid=(2 * RET_HEADS,),
        in_specs=[pl.BlockSpec((1, 1, LANES), lambda r: (r, 0, 0))],
        out_specs=[pl.BlockSpec((1, N_TAB, CHUNK, CHUNK), lambda r: (r, 0, 0, 0)),
                   pl.BlockSpec((1, 2, Lc, LANES), lambda r: (r, 0, 0, 0))],
        out_shape=[jax.ShapeDtypeStruct((2 * RET_HEADS, N_TAB, CHUNK, CHUNK), F32),
                   jax.ShapeDtypeStruct((2 * RET_HEADS, 2, Lc, LANES), F32)],
        compiler_params=_cparams(("parallel",)),
    )(rdb)


def _ret_ctx_state(P, ctab, L, Lc):
    cb = L // Lc

    def body(k_ref, v_ref, c_ref, s_ref):
        for p in range(RET_HEADS // 2):
            kp = k_ref[:, p * LANES:(p + 1) * LANES].astype(F32) * K_SCALE
            for a in range(2):
                h = 2 * p + a
                kh = jnp.where(_half_mask(kp.shape, a), kp, 0.0)
                vh = v_ref[:, h * RET_DV:(h + 1) * RET_DV]
                for d in range(2):
                    kw = (kh * c_ref[d * RET_HEADS + h, 0]).astype(BF16)
                    s_ref[d * RET_HEADS + h] = _dot(kw, vh, TN)

    return pl.pallas_call(
        body, name="ret_ctx_state", grid=(1,),
        in_specs=[pl.BlockSpec((Lc, 512), lambda i: (cb, C_RK // 512)),
                  pl.BlockSpec((Lc, 1024), lambda i: (cb, C_RV // 1024)),
                  pl.BlockSpec((2 * RET_HEADS, 2, Lc, LANES), lambda i: (0, 0, 0, 0))],
        out_specs=pl.BlockSpec((2 * RET_HEADS, LANES, RET_DV), lambda i: (0, 0, 0)),
        out_shape=jax.ShapeDtypeStruct((2 * RET_HEADS, LANES, RET_DV), F32),
        compiler_params=_cparams(("arbitrary",)),
    )(P, P, ctab)


def _ret_fwd(P, rope, tabs, s0, L, comm=()):
    n = L // CHUNK
    nb = n // RET_SUB

    def body(qf, kf, vf, rf, qb, kb, vb, rb, t_ref, s0_ref, of_ref, ob_ref, stf_ref, stb_ref, st):
        s = pl.program_id(0)

        @pl.when(s == 0)
        def _():
            st[...] = s0_ref[...]

        for rnd in range(RET_SUB):
            units = []
            for d, (q_ref, k_ref, v_ref, r_ref, o_ref, so_ref) in enumerate(
                    ((qf, kf, vf, rf, of_ref, stf_ref), (qb, kb, vb, rb, ob_ref, stb_ref))):
                j = rnd if d == 0 else RET_SUB - 1 - rnd
                rows = slice(j * CHUNK, (j + 1) * CHUNK)
                cos, sin = r_ref[0, rows, :], r_ref[1, rows, :]
                for p in range(RET_HEADS // 2):
                    qp = _rope(q_ref[rows, p * LANES:(p + 1) * LANES].astype(F32), cos, sin, 32)
                    kp = _rope(k_ref[rows, p * LANES:(p + 1) * LANES].astype(F32), cos, sin, 32) * K_SCALE
                    for a in range(2):
                        h = 2 * p + a
                        hm = _half_mask(qp.shape, a)
                        units.append(dict(r=d * RET_HEADS + h, h=h, a=a, j=j, rows=rows, o_ref=o_ref, so_ref=so_ref,
                                          v_ref=v_ref, qh=jnp.where(hm, qp, 0.0), kh=jnp.where(hm, kp, 0.0)))
            for u in units:
                u["sc"] = _dot(u["qh"].astype(BF16), u["kh"].astype(BF16), NT)
            for u in units:
                r, h = u["r"], u["h"]
                sp = st[r]
                u["so_ref"][u["j"], h] = sp[u["a"] * RET_DK:(u["a"] + 1) * RET_DK, :]
                vh = u["v_ref"][u["rows"], h * RET_DV:(h + 1) * RET_DV]
                o = _dot((u["sc"] * t_ref[r, 0]).astype(BF16), vh, NN)
                o += _dot((u["qh"] * t_ref[r, 2]).astype(BF16), sp.astype(BF16), NN)
                u["o_ref"][u["rows"], h * RET_DV:(h + 1) * RET_DV] = o
            for u in units:
                r, h = u["r"], u["h"]
                vh = u["v_ref"][u["rows"], h * RET_DV:(h + 1) * RET_DV]
                st[r] = t_ref[r, 6] * st[r] + _dot((u["kh"] * t_ref[r, 4]).astype(BF16), vh, TN)

    fw = lambda s: s
    bw = lambda s: nb - 1 - s
    RB = RET_SUB * CHUNK

    def specs(cm):
        return [pl.BlockSpec((RB, 512), lambda s: (cm(s), C_RQ // 512)),
                pl.BlockSpec((RB, 512), lambda s: (cm(s), C_RK // 512)),
                pl.BlockSpec((RB, 1024), lambda s: (cm(s), C_RV // 1024)),
                pl.BlockSpec((2, RB, LANES), lambda s: (0, cm(s), 0))]

    full = lambda shp: pl.BlockSpec(shp, lambda s: (0,) * len(shp))
    return _call(
        body, "ret_fwd", (nb,),
        specs(fw) + specs(bw) + [_resident((2 * RET_HEADS, N_TAB, CHUNK, CHUNK)),
                                 _resident((2 * RET_HEADS, LANES, RET_DV))],
        [pl.BlockSpec((RB, 1024), lambda s: (fw(s), 0)),
         pl.BlockSpec((RB, 1024), lambda s: (bw(s), 0)),
         pl.BlockSpec((RET_SUB, RET_HEADS, RET_DK, RET_DV), lambda s: (fw(s), 0, 0, 0)),
         pl.BlockSpec((RET_SUB, RET_HEADS, RET_DK, RET_DV), lambda s: (bw(s), 0, 0, 0))],
        [jax.ShapeDtypeStruct((L, 1024), F32), jax.ShapeDtypeStruct((L, 1024), F32),
         jax.ShapeDtypeStruct((n, RET_HEADS, RET_DK, RET_DV), F32),
         jax.ShapeDtypeStruct((n, RET_HEADS, RET_DK, RET_DV), F32)],
        [pltpu.VMEM((2 * RET_HEADS, LANES, RET_DV), F32)],
        ("arbitrary",), (P, P, P, rope, P, P, P, rope, tabs, s0), comm)


def _ret_finish_fwd(of, ob, P, L):
    tr = _pick(L, (1024, 512, 256, 128))

    def body(f_ref, b_ref, g_ref, y_ref):
        for h in range(RET_HEADS):
            sl = slice(h * RET_DV, (h + 1) * RET_DV)
            n, _ = _norm_rows(f_ref[:, sl] + b_ref[:, sl])
            g = g_ref[:, sl].astype(F32)
            y_ref[:, sl] = (n * (g * _sigmoid(g))).astype(y_ref.dtype)

    row = pl.BlockSpec((tr, 1024), lambda i: (i, 0))
    return pl.pallas_call(
        body, name="ret_finish_fwd", grid=(L // tr,),
        in_specs=[row, row, pl.BlockSpec((tr, 1024), lambda i: (i, C_RG // 1024))],
        out_specs=row, out_shape=jax.ShapeDtypeStruct((L, 2048), BF16),
        compiler_params=_cparams(("parallel",)),
    )(of, ob, P)


def _ret_finish_bwd(of, ob, P, dY, L):
    tr = _pick(L, (1024, 512, 256, 128))

    def body(f_ref, b_ref, g_ref, dy_ref, do_ref, dg_ref):
        for h in range(RET_HEADS):
            sl = slice(h * RET_DV, (h + 1) * RET_DV)
            n, r = _norm_rows(f_ref[:, sl] + b_ref[:, sl])
            g = g_ref[:, sl].astype(F32)
            sg = _sigmoid(g)
            dy = dy_ref[:, sl].astype(F32)
            dg_ref[:, sl] = (dy * n * (sg * (1.0 + g * (1.0 - sg)))).astype(dg_ref.dtype)
            dn = dy * (g * sg)
            do_ref[:, sl] = (r * (dn - n * jnp.mean(dn * n, axis=-1, keepdims=True))).astype(do_ref.dtype)

    row = pl.BlockSpec((tr, 1024), lambda i: (i, 0))
    return pl.pallas_call(
        body, name="ret_finish_bwd", grid=(L // tr,),
        in_specs=[row, row, pl.BlockSpec((tr, 1024), lambda i: (i, C_RG // 1024)), row],
        out_specs=[row, row],
        out_shape=[jax.ShapeDtypeStruct((L, 1024), BF16), jax.ShapeDtypeStruct((L, 1024), BF16)],
        compiler_params=_cparams(("parallel",)),
    )(of, ob, P, dY)


def _ret_bwd(P, rope, tabs, stf, stb, dO, L, comm=()):
    n = L // CHUNK
    nb = n // RET_SUB

    def body(qf, kf, vf, rf, gf, sf, qb, kb, vb, rb, gb, sb, t_ref,
             dqf, dkf, dvf, dqb, dkb, dvb, ds0_ref, dlg_ref, ds):
        s = pl.program_id(0)

        @pl.when(s == 0)
        def _():
            ds[...] = jnp.zeros_like(ds)
            dlg_ref[...] = jnp.zeros_like(dlg_ref)

        for rnd in range(RET_SUB):
            units, pairs = [], []
            for d, (q_ref, k_ref, v_ref, r_ref, g_ref, s_ref, dq_ref, dk_ref, dv_ref) in enumerate(
                    ((qf, kf, vf, rf, gf, sf, dqf, dkf, dvf), (qb, kb, vb, rb, gb, sb, dqb, dkb, dvb))):
                j = RET_SUB - 1 - rnd if d == 0 else rnd
                rows = slice(j * CHUNK, (j + 1) * CHUNK)
                cos, sin = r_ref[0, rows, :], r_ref[1, rows, :]
                for p in range(RET_HEADS // 2):
                    qp = _rope(q_ref[rows, p * LANES:(p + 1) * LANES].astype(F32), cos, sin, 32)
                    kp = _rope(k_ref[rows, p * LANES:(p + 1) * LANES].astype(F32), cos, sin, 32) * K_SCALE
                    pair = dict(p=p, rows=rows, cos=cos, sin=sin, dq_ref=dq_ref, dk_ref=dk_ref, us=[])
                    pairs.append(pair)
                    for a in range(2):
                        h = 2 * p + a
                        r = d * RET_HEADS + h
                        hm = _half_mask(qp.shape, a)
                        zero = jnp.zeros((RET_DK, RET_DV), F32)
                        sp = s_ref[j, h]
                        u = dict(r=r, h=h, rows=rows, dv_ref=dv_ref,
                                 qh=jnp.where(hm, qp, 0.0), kh=jnp.where(hm, kp, 0.0),
                                 vh=v_ref[rows, h * RET_DV:(h + 1) * RET_DV],
                                 gh=g_ref[rows, h * RET_DV:(h + 1) * RET_DV],
                                 sp=jnp.concatenate([sp, zero] if a == 0 else [zero, sp], axis=0),
                                 dsn=ds[r])
                        u["qhb"], u["khb"] = u["qh"].astype(BF16), u["kh"].astype(BF16)
                        units.append(u)
                        pair["us"].append(u)
            for u in units:
                u["am"] = _dot(u["qhb"], u["khb"], NT)
                u["dar"] = _dot(u["gh"], u["vh"], NT)
                u["xq"] = _dot(u["gh"], u["sp"].astype(BF16), NT)
                u["yk"] = _dot(u["vh"], u["dsn"].astype(BF16), NT)
            for u in units:
                r = u["r"]
                dm = t_ref[r, 0]
                u["da"] = (u["dar"] * dm).astype(BF16)
                u["amd"] = (u["am"] * dm).astype(BF16)
                part = (jnp.sum(u["am"] * u["dar"] * t_ref[r, 1]) + jnp.sum(u["qh"] * t_ref[r, 3] * u["xq"])
                        + jnp.sum(u["kh"] * t_ref[r, 5] * u["yk"])
                        + float(CHUNK) * jnp.sum(t_ref[r, 6] * u["dsn"] * u["sp"]))
                dlg_ref[r:r + 1, :] += jnp.zeros((1, LANES), F32) + part
            for u in units:
                r, h = u["r"], u["h"]
                u["dq"] = _dot(u["da"], u["khb"], NN) + u["xq"] * t_ref[r, 2]
                u["dk"] = _dot(u["da"], u["qhb"], TN) + u["yk"] * t_ref[r, 4]
                u["dv_ref"][u["rows"], h * RET_DV:(h + 1) * RET_DV] = (
                    _dot(u["amd"], u["gh"], TN)
                    + _dot((u["kh"] * t_ref[r, 4]).astype(BF16), u["dsn"].astype(BF16), NN)
                ).astype(u["dv_ref"].dtype)
                ds[r] = t_ref[r, 6] * u["dsn"] + _dot((u["qh"] * t_ref[r, 2]).astype(BF16), u["gh"], TN)
            for pair in pairs:
                sl = slice(pair["p"] * LANES, (pair["p"] + 1) * LANES)
                u0, u1 = pair["us"]
                pair["dq_ref"][pair["rows"], sl] = _rope_t(
                    u0["dq"] + u1["dq"], pair["cos"], pair["sin"], 32).astype(BF16)
                pair["dk_ref"][pair["rows"], sl] = _rope_t(
                    (u0["dk"] + u1["dk"]) * K_SCALE, pair["cos"], pair["sin"], 32).astype(BF16)

        @pl.when(s == nb - 1)
        def _():
            ds0_ref[...] = ds[...]

    fw = lambda s: nb - 1 - s
    bw = lambda s: s
    RB = RET_SUB * CHUNK

    def specs(cm):
        return [pl.BlockSpec((RB, 512), lambda s: (cm(s), C_RQ // 512)),
                pl.BlockSpec((RB, 512), lambda s: (cm(s), C_RK // 512)),
                pl.BlockSpec((RB, 1024), lambda s: (cm(s), C_RV // 1024)),
                pl.BlockSpec((2, RB, LANES), lambda s: (0, cm(s), 0)),
                pl.BlockSpec((RB, 1024), lambda s: (cm(s), 0)),
                pl.BlockSpec((RET_SUB, RET_HEADS, RET_DK, RET_DV), lambda s: (cm(s), 0, 0, 0))]

    def ospecs(cm):
        return [pl.BlockSpec((RB, 512), lambda s: (cm(s), 0)), pl.BlockSpec((RB, 512), lambda s: (cm(s), 0)),
                pl.BlockSpec((RB, 1024), lambda s: (cm(s), 0))]

    oshape = [jax.ShapeDtypeStruct((L, 512), BF16), jax.ShapeDtypeStruct((L, 512), BF16),
              jax.ShapeDtypeStruct((L, 1024), BF16)]
    full = lambda shp: pl.BlockSpec(shp, lambda s: (0,) * len(shp))
    return _call(
        body, "ret_bwd", (nb,),
        specs(fw) + specs(bw) + [_resident((2 * RET_HEADS, N_TAB, CHUNK, CHUNK))],
        ospecs(fw) + ospecs(bw) + [full((2 * RET_HEADS, LANES, RET_DV)), full((2 * RET_HEADS, LANES))],
        oshape + oshape + [jax.ShapeDtypeStruct((2 * RET_HEADS, LANES, RET_DV), F32),
                           jax.ShapeDtypeStruct((2 * RET_HEADS, LANES), F32)],
        [pltpu.VMEM((2 * RET_HEADS, LANES, RET_DV), F32)],
        ("arbitrary",), (P, P, P, rope, dO, stf, P, P, P, rope, dO, stb, tabs), comm)


def _ret_ctx_bwd(P, ctab, ds0, dlg, rdb, L, Lc):
    cb = L // Lc

    def body(k_ref, v_ref, c_ref, ds_ref, dlg_ref, rd_ref, dk_ref, dv_ref, drd_ref):
        for p in range(RET_HEADS // 2):
            kp = k_ref[:, p * LANES:(p + 1) * LANES].astype(F32) * K_SCALE
            dkp = jnp.zeros((Lc, LANES), F32)
            for a in range(2):
                h = 2 * p + a
                kh = jnp.where(_half_mask(kp.shape, a), kp, 0.0)
                vh = v_ref[:, h * RET_DV:(h + 1) * RET_DV]
                dvh = jnp.zeros((Lc, RET_DV), F32)
                for d in range(2):
                    r = d * RET_HEADS + h
                    dsb = ds_ref[r].astype(BF16)
                    cw, cwc = c_ref[r, 0], c_ref[r, 1]
                    y = _dot(vh, dsb, NT)
                    dkp += y * cw
                    dvh += _dot((kh * cw).astype(BF16), dsb, NN)
                    lg = -jnp.exp(rd_ref[r])
                    drd_ref[r:r + 1, :] = (dlg_ref[r:r + 1, :] + jnp.sum(kh * cwc * y)) * lg
                dv_ref[:, h * RET_DV:(h + 1) * RET_DV] = dvh
            dk_ref[:, p * LANES:(p + 1) * LANES] = dkp * K_SCALE

    full = lambda shp: pl.BlockSpec(shp, lambda i: (0,) * len(shp))
    return pl.pallas_call(
        body, name="ret_ctx_bwd", grid=(1,),
        in_specs=[pl.BlockSpec((Lc, 512), lambda i: (cb, C_RK // 512)),
                  pl.BlockSpec((Lc, 1024), lambda i: (cb, C_RV // 1024)),
                  full((2 * RET_HEADS, 2, Lc, LANES)), full((2 * RET_HEADS, LANES, RET_DV)),
                  full((2 * RET_HEADS, LANES)), full((2 * RET_HEADS, 1, LANES))],
        out_specs=[full((Lc, 512)), full((Lc, 1024)), full((2 * RET_HEADS, LANES))],
        out_shape=[jax.ShapeDtypeStruct((Lc, 512), F32), jax.ShapeDtypeStruct((Lc, 1024), F32),
                   jax.ShapeDtypeStruct((2 * RET_HEADS, LANES), F32)],
        compiler_params=_cparams(("arbitrary",)),
    )(P, P, ctab, ds0, dlg, rdb)


BLK = 128
N_LOC = 3 * BLK


ATT_SUB_FWD = 4
ATT_SUB_BWD = 8


def _att_inputs(P, rope, L, Lc, sub):
    n = L // BLK
    cb = L // Lc
    prev = lambda i: jnp.maximum(sub * i - 1, 0)
    nxt = lambda i: jnp.minimum(sub * i + sub, n - 1)
    specs = [pl.BlockSpec((sub * BLK, 1024), lambda i: (i, C_AQ // 1024))]
    args = [P]
    for col in (C_AK // 256, C_AV // 256):
        specs += [pl.BlockSpec((BLK, 256), functools.partial(lambda i, col: (prev(i), col), col=col)),
                  pl.BlockSpec((sub * BLK, 256), functools.partial(lambda i, col: (i, col), col=col)),
                  pl.BlockSpec((BLK, 256), functools.partial(lambda i, col: (nxt(i), col), col=col)),
                  pl.BlockSpec((Lc, 256), functools.partial(lambda i, col: (cb, col), col=col))]
        args += [P] * 4
    specs += [pl.BlockSpec((2, BLK, LANES), lambda i: (0, prev(i), 0)),
              pl.BlockSpec((2, sub * BLK, LANES), lambda i: (0, i, 0)),
              pl.BlockSpec((2, BLK, LANES), lambda i: (0, nxt(i), 0))]
    args += [rope] * 3
    return specs, args


def _att_prep(i, n, refs, Lc, sub):
    q_ref, kp_ref, kc_ref, kn_ref, kx_ref, vp_ref, vc_ref, vn_ref, vx_ref, rp_ref, rc_ref, rn_ref = refs
    cos = jnp.concatenate([rp_ref[0], rc_ref[0], rn_ref[0]], axis=0)
    sin = jnp.concatenate([rp_ref[1], rc_ref[1], rn_ref[1]], axis=0)

    def dup(x):
        xr = pltpu.roll(x, 64, 1)
        return [jnp.where(_half_mask(x.shape, b), x, xr).astype(BF16) for b in range(2)]

    kd = [[] for _ in range(sub)]
    vd = [[] for _ in range(sub)]
    for t in range(ATT_KV // 2):
        sl = slice(t * LANES, (t + 1) * LANES)
        kl = jnp.concatenate([kp_ref[:, sl], kc_ref[:, sl], kn_ref[:, sl]], axis=0).astype(F32)
        kl = dup(_rope(kl, cos, sin, 16))
        vl = dup(jnp.concatenate([vp_ref[:, sl], vc_ref[:, sl], vn_ref[:, sl]], axis=0).astype(F32))
        kx, vx = dup(kx_ref[:, sl].astype(F32)), dup(vx_ref[:, sl].astype(F32))
        for j in range(sub):
            rows = slice(j * BLK, j * BLK + N_LOC)
            for b in range(2):
                kd[j].append(jnp.concatenate([kl[b][rows], kx[b]], axis=0))
                vd[j].append(jnp.concatenate([vl[b][rows], vx[b]], axis=0))
    nk = N_LOC + Lc
    rr = lax.broadcasted_iota(jnp.int32, (BLK, nk), 0)
    ss = lax.broadcasted_iota(jnp.int32, (BLK, nk), 1)
    band = (ss >= rr) & (ss <= rr + 2 * BLK)
    bias4, tabs = [], []
    for j in range(sub):
        blk = sub * i + j
        lo = jnp.where(blk == 0, BLK, 0)
        hi = jnp.where(blk == n - 1, 2 * BLK, N_LOC)
        bias = jnp.where((ss >= N_LOC) | (band & (ss >= lo) & (ss < hi)), 0.0, NEG)
        bias4.append(jnp.concatenate([bias] * 4, axis=0))
        tabs.append((rc_ref[0, j * BLK:(j + 1) * BLK, :], rc_ref[1, j * BLK:(j + 1) * BLK, :]))
    return kd, vd, bias4, tabs


LOG2E = 1.4426950408889634
LN2 = 0.6931471805599453
Q_SCALE = A_SCALE * LOG2E


def _stack4(ref, rows, g, f=None):
    parts = []
    for jp in range(2):
        t = ref[rows, (2 * g + jp) * LANES:(2 * g + jp + 1) * LANES].astype(F32)
        if f is not None:
            t = f(t)
        for a in range(2):
            parts.append(jnp.where(_half_mask(t.shape, a), t, 0.0))
    return jnp.concatenate(parts, axis=0)


def _unstack4(x4, jp):
    r0 = 2 * jp * BLK
    lo = x4[r0:r0 + BLK]
    hi = x4[r0 + BLK:r0 + 2 * BLK]
    return jnp.where(_half_mask(lo.shape, 0), lo, hi)


def _softmax_parts(s, bias4, sink_ref, g):
    sink_col = LOG2E * jnp.concatenate(
        [jnp.zeros((BLK, 1), F32) + sink_ref[4 * g + r:4 * g + r + 1, 0:1] for r in range(4)], axis=0)
    s = s + bias4
    m = jnp.maximum(jnp.max(s, axis=-1, keepdims=True), sink_col)
    e = jnp.exp2(s - m)
    es = jnp.exp2(sink_col - m)
    return e, es, jnp.sum(e, axis=-1, keepdims=True) + es


def _att_fwd(P, rope, sinkb, Y, L, Lc, comm=()):
    n = L // BLK
    sub = _pick(n, (ATT_SUB_FWD, 4, 2, 1))
    specs, args = _att_inputs(P, rope, L, Lc, sub)

    def body(*refs):
        sink_ref, o_ref = refs[12], refs[14]
        i = pl.program_id(0)
        kd, vd, bias4, tabs = _att_prep(i, n, refs[:12], Lc, sub)
        for j in range(sub):
            rows = slice(j * BLK, (j + 1) * BLK)
            cq, sq = tabs[j]
            for g in range(ATT_KV):
                q4 = _stack4(refs[0], rows, g, lambda t: _rope(t, cq, sq, 16) * Q_SCALE).astype(BF16)
                e, _, l = _softmax_parts(_dot(q4, kd[j][g], NT), bias4[j], sink_ref, g)
                o4 = _dot(e.astype(BF16), vd[j][g], NN) * (1.0 / l)
                for jp in range(2):
                    c0 = (2 * g + jp) * LANES
                    o_ref[rows, c0:c0 + LANES] = _unstack4(o4, jp).astype(o_ref.dtype)

    return _call(
        body, "att_fwd", (n // sub,),
        specs + [pl.BlockSpec((ATT_HEADS, LANES), lambda i: (0, 0)), pl.BlockSpec(memory_space=pl.ANY)],
        [pl.BlockSpec((sub * BLK, 1024), lambda i: (i, 1))], [jax.ShapeDtypeStruct((L, 2048), BF16)], [],
        ("parallel",), (*args, sinkb, Y), comm, aliases={13: 0})


def _att_bwd(P, rope, sinkb, Y, dY, L, Lc, comm=()):
    n = L // BLK
    sub = _pick(n, (ATT_SUB_BWD, 4, 2, 1))
    specs, args = _att_inputs(P, rope, L, Lc, sub)
    nk = N_LOC + Lc

    def body(*refs):
        sink_ref, y_ref, dy_ref = refs[12], refs[13], refs[14]
        dq_ref, dkl_ref, dvl_ref, dkx_ref, dvx_ref, dsk_ref = refs[15:21]
        i = pl.program_id(0)

        @pl.when(i == 0)
        def _():
            dkx_ref[...] = jnp.zeros_like(dkx_ref)
            dvx_ref[...] = jnp.zeros_like(dvx_ref)
            dsk_ref[...] = jnp.zeros_like(dsk_ref)

        kd, vd, bias4, tabs = _att_prep(i, n, refs[:12], Lc, sub)
        for j in range(sub):
            rows = slice(j * BLK, (j + 1) * BLK)
            cq, sq = tabs[j]
            for t in range(ATT_KV // 2):
                dk_halves, dv_halves = [], []
                for b in range(2):
                    g = 2 * t + b
                    q4 = _stack4(refs[0], rows, g, lambda x: _rope(x, cq, sq, 16) * Q_SCALE).astype(BF16)
                    do4 = _stack4(dy_ref, rows, g)
                    delta = jnp.sum(do4 * _stack4(y_ref, rows, g), axis=-1, keepdims=True)
                    do4b = do4.astype(BF16)
                    e, es, l = _softmax_parts(_dot(q4, kd[j][g], NT), bias4[j], sink_ref, g)
                    inv = 1.0 / l
                    p = e * inv
                    dsc = (p * (_dot(do4b, vd[j][g], NT) - delta)).astype(BF16)
                    dsr = es * inv * delta
                    for r in range(4):
                        h = 4 * g + r
                        dsk_ref[h:h + 1, :] += jnp.zeros((1, LANES), F32) - jnp.sum(dsr[r * BLK:(r + 1) * BLK])
                    dq4 = _dot(dsc, kd[j][g], NN) * A_SCALE
                    for jp in range(2):
                        c0 = (2 * g + jp) * LANES
                        dq_ref[rows, c0:c0 + LANES] = _rope_t(_unstack4(dq4, jp), cq, sq, 16).astype(dq_ref.dtype)
                    dkd = _dot(q4, dsc, TN) * LN2
                    dvd = _dot(do4b, p.astype(BF16), TN)
                    dk_halves.append(dkd[:ATT_DH] + dkd[ATT_DH:])
                    dv_halves.append(dvd[:ATT_DH] + dvd[ATT_DH:])
                dk_t = jnp.concatenate(dk_halves, axis=0).T
                dv_t = jnp.concatenate(dv_halves, axis=0).T
                sl = slice(t * LANES, (t + 1) * LANES)
                dkl_ref[j, :, sl] = dk_t[:N_LOC]
                dvl_ref[j, :, sl] = dv_t[:N_LOC]
                dkx_ref[:, sl] += dk_t[N_LOC:]
                dvx_ref[:, sl] += dv_t[N_LOC:]

    row = pl.BlockSpec((sub * BLK, 1024), lambda i: (i, 0))
    loc = pl.BlockSpec((sub, N_LOC, 256), lambda i: (i, 0, 0))
    cx = pl.BlockSpec((Lc, 256), lambda i: (0, 0))
    return _call(
        body, "att_bwd", (n // sub,),
        specs + [pl.BlockSpec((ATT_HEADS, LANES), lambda i: (0, 0))]
        + [pl.BlockSpec((sub * BLK, 1024), lambda i: (i, 1))] * 2,
        [row, loc, loc, cx, cx, pl.BlockSpec((ATT_HEADS, LANES), lambda i: (0, 0))],
        [jax.ShapeDtypeStruct((L, 1024), BF16), jax.ShapeDtypeStruct((n, N_LOC, 256), F32),
         jax.ShapeDtypeStruct((n, N_LOC, 256), F32), jax.ShapeDtypeStruct((Lc, 256), F32),
         jax.ShapeDtypeStruct((Lc, 256), F32), jax.ShapeDtypeStruct((ATT_HEADS, LANES), F32)], [],
        ("arbitrary",), (*args, sinkb, Y, dY), comm)


def _assemble_dp(L, Lc, dqf, dqb, dkf, dkb, dvf, dvb, drg, daq, dkl, dvl, rope_att, dck, dcv, dkx, dvx):
    n = L // BLK
    nc = Lc // BLK

    def body(dqf_r, dqb_r, dkf_r, dkb_r, dvf_r, dvb_r, drg_r, daq_r, kl0, kl1, kl2, vl0, vl1, vl2, rp_r,
             dck_r, dcv_r, dkx_r, dvx_r, o_ref):
        i = pl.program_id(0)

        @pl.when(i < n)
        def _():
            add = lambda a, b: (a[...].astype(F32) + b[...].astype(F32)).astype(o_ref.dtype)
            o_ref[:, C_RQ:C_RK] = add(dqf_r, dqb_r)
            o_ref[:, C_RK:C_RV] = add(dkf_r, dkb_r)
            o_ref[:, C_RV:C_RG] = add(dvf_r, dvb_r)
            o_ref[:, C_RG:C_AQ] = drg_r[...].astype(o_ref.dtype)
            o_ref[:, C_AQ:C_AK] = daq_r[...].astype(o_ref.dtype)
            w0 = jnp.where(i > 0, 1.0, 0.0)
            w2 = jnp.where(i < n - 1, 1.0, 0.0)
            dk = kl0[0] * w0 + kl1[0] + kl2[0] * w2
            dv = vl0[0] * w0 + vl1[0] + vl2[0] * w2
            for t in range(ATT_KV // 2):
                sl = slice(t * LANES, (t + 1) * LANES)
                o_ref[:, C_AK + t * LANES:C_AK + (t + 1) * LANES] = _rope_t(
                    dk[:, sl], rp_r[0], rp_r[1], 16).astype(o_ref.dtype)
            o_ref[:, C_AV:D_PROJ] = dv.astype(o_ref.dtype)

        @pl.when(i >= n)
        def _():
            o_ref[:, C_RQ:C_RK] = jnp.zeros((BLK, C_RK - C_RQ), o_ref.dtype)
            o_ref[:, C_RK:C_RV] = dck_r[...].astype(o_ref.dtype)
            o_ref[:, C_RV:C_RG] = dcv_r[...].astype(o_ref.dtype)
            o_ref[:, C_RG:C_AK] = jnp.zeros((BLK, C_AK - C_RG), o_ref.dtype)
            o_ref[:, C_AK:C_AV] = dkx_r[...].astype(o_ref.dtype)
            o_ref[:, C_AV:D_PROJ] = dvx_r[...].astype(o_ref.dtype)

    xm = lambda i: jnp.minimum(i, n - 1)
    cm = lambda i: jnp.clip(i - n, 0, nc - 1)
    r512 = pl.BlockSpec((BLK, 512), lambda i: (xm(i), 0))
    r1024 = pl.BlockSpec((BLK, 1024), lambda i: (xm(i), 0))
    part = lambda off: pl.BlockSpec((1, BLK, 256), lambda i: (jnp.clip(xm(i) + off, 0, n - 1), 1 - off, 0))
    return pl.pallas_call(
        body, name="assemble_dp", grid=(n + nc,),
        in_specs=[r512, r512, r512, r512, r1024, r1024, r1024, r1024,
                  part(-1), part(0), part(1), part(-1), part(0), part(1),
                  pl.BlockSpec((2, BLK, LANES), lambda i: (0, xm(i), 0)),
                  pl.BlockSpec((BLK, 512), lambda i: (cm(i), 0)), pl.BlockSpec((BLK, 1024), lambda i: (cm(i), 0)),
                  pl.BlockSpec((BLK, 256), lambda i: (cm(i), 0)), pl.BlockSpec((BLK, 256), lambda i: (cm(i), 0))],
        out_specs=pl.BlockSpec((BLK, D_PROJ), lambda i: (i, 0)),
        out_shape=jax.ShapeDtypeStruct((L + Lc, D_PROJ), BF16),
        compiler_params=_cparams(("parallel",)),
    )(dqf, dqb, dkf, dkb, dvf, dvb, drg, daq, dkl, dkl, dkl, dvl, dvl, dvl, rope_att, dck, dcv, dkx, dvx)


def _adam_math(w, g, m, v):
    m = ADAM_B1 * m + (1.0 - ADAM_B1) * g
    v = ADAM_B2 * v + (1.0 - ADAM_B2) * (g * g)
    m_hat = m / (1.0 - ADAM_B1 ** ADAM_STEP)
    v_hat = v / (1.0 - ADAM_B2 ** ADAM_STEP)
    delta = -ADAM_LR * (m_hat / (jnp.sqrt(v_hat) + ADAM_EPS) + ADAM_WD * w)
    return delta, m, v


def _adam(name, w, m, v, g=None, parts=None):
    R, C = w.shape
    tr = _pick(R, (256, 192, 176, 128, 64, 32, 16, 8))
    summed = parts is not None
    n_parts = parts.shape[0] if summed else 0

    def body(w_ref, m_ref, v_ref, g_ref, go_ref, d_ref, mo_ref, vo_ref):
        if summed:
            gv = g_ref[0].astype(F32)
            for j in range(1, n_parts):
                gv = gv + g_ref[j].astype(F32)
        else:
            gv = g_ref[...]
        d, mn, vn = _adam_math(w_ref[...], gv, m_ref[...], v_ref[...])
        go_ref[...] = gv
        d_ref[...] = d
        mo_ref[...] = mn
        vo_ref[...] = vn

    row = pl.BlockSpec((tr, C), lambda i: (i, 0))
    gspec = pl.BlockSpec((n_parts, tr, C), lambda i: (0, i, 0)) if summed else row
    return pl.pallas_call(
        body, name=name, grid=(R // tr,),
        in_specs=[row, row, row, gspec], out_specs=[row] * 4,
        out_shape=[jax.ShapeDtypeStruct((R, C), F32)] * 4,
        compiler_params=_cparams(("parallel",)),
    )(w, m, v, parts if summed else g)


def _rows_full(g):
    _, R, D = g.shape
    return g.reshape(N_DEV * R, D)


def _rows_slots(g):
    N, D = g.shape
    return g.reshape(N_DEV, N // N_DEV, D)


def _pad_rows(a, rows):
    return jnp.concatenate([a, jnp.zeros((rows - a.shape[0],) + a.shape[1:], a.dtype)], axis=0)


def kernel(x, c, ctx, c_ctx, w_mod, b_mod, norm_mix, norm_ffn, w_in, ret_decay, attn_sink, w_out, w_gate, w_up, w_down, norm_final, loss_target, m_c_ctx, m_w_mod, m_b_mod, m_norm_mix, m_norm_ffn, m_w_in, m_ret_decay, m_attn_sink, m_w_out, m_w_gate, m_w_up, m_w_down, m_norm_final, v_c_ctx, v_w_mod, v_b_mod, v_norm_mix, v_norm_ffn, v_w_in, v_ret_decay, v_attn_sink, v_w_out, v_w_gate, v_w_up, v_w_down, v_norm_final):
    L, D = x.shape[1], x.shape[2]
    Lc = ctx.shape[1]
    DF = w_gate.shape[2] * N_DEV
    C6 = w_mod.shape[2]
    me = _my_id()
    xs, cx, tgt = x[0], ctx[0], loss_target[0]

    ag_in = ("ag2", w_in[0].T.astype(BF16))
    ag_out, ag_gate = ("ag2", w_out[0].astype(BF16)), ("ag2", w_gate[0].T.astype(BF16))
    ag_up, ag_down = ("ag2", w_up[0].T.astype(BF16)), ("ag2", w_down[0].astype(BF16))

    cs = _allgather(c, "ag_c")[:, 0, :]
    s_in = _pad_rows(jnp.concatenate([cs, c_ctx[None, :]], axis=0), 16)
    b_l = lax.dynamic_slice_in_dim(b_mod, me * C6, C6, axis=1)
    mod_parts = _allgather(_mod_fwd(s_in, w_mod[0], b_l), "ag_mod")
    mod = _pad_rows(lax.dynamic_index_in_dim(mod_parts, me, axis=1, keepdims=False).reshape(6, D), 8)
    modc = _pad_rows(mod_parts[:, N_DEV, :].reshape(6, D), 8)
    mix_mod, ffn_mod = mod, jnp.roll(mod, -3, axis=0)
    gt_m, gt_f = mod[2:3], mod[5:6]

    rope_ret, rope_att = _rope_tables(L)
    rdb = jnp.broadcast_to(ret_decay[0].reshape(2 * RET_HEADS, 1, 1), (2 * RET_HEADS, 1, LANES))
    sinkb = jnp.broadcast_to(attn_sink[0].reshape(ATT_HEADS, 1), (ATT_HEADS, LANES))

    tm = _pick(L + Lc, (1408, 768, 512, 384, 256, 128))
    tmx = _pick(L, (1024, 512, 256, 128))

    (H,), (g_in,) = _modulate_fwd("mod_mix_fwd", xs, cx, norm_mix, mix_mod, modc, comm=[ag_in])
    W_inT = _rows_full(g_in)
    ident = lambda a, e: a
    tP, tD, tF = _pick(D_PROJ, (1152, 768, 512)), _pick(D, (2048, 1024, 512)), _pick(DF, (512, 256, 128))
    (P,), (g_gate,) = _matmul("mm_in", [(H, W_inT, 0)], 1, L + Lc, D_PROJ, D, "nt",
                              (tm, _pick(D_PROJ, (1536, 768, 512)), D), [], [BF16], ident,
                              comm=[ag_gate], pass_on_at=0.85)
    W_gateT = _rows_full(g_gate)
    tabs, ctab = _ret_tables(rdb, Lc)
    s0 = _ret_ctx_state(P, ctab, L, Lc)
    (o_f, o_b, st_f, st_b), (g_out,) = _ret_fwd(P, rope_ret, tabs, s0, L, comm=[ag_out])
    W_out = _rows_full(g_out)
    Y_half = _ret_finish_fwd(o_f, o_b, P, L)
    (Y,), (g_up,) = _att_fwd(P, rope_att, sinkb, Y_half, L, Lc, comm=[ag_up])
    W_upT = _rows_full(g_up)
    KO = Y.shape[1]
    f_mix = _matmul("mm_out", [(Y, W_out, 0)], 1, L, D, KO, "nn", (tmx, tD, KO), [], [BF16], ident)[0]

    x1, H2 = _residual_modulate_fwd("mod_ffn_fwd", xs, f_mix, gt_m, norm_ffn, ffn_mod)

    def swiglu_epi(a, e):
        sg = _sigmoid(a[0])
        act = a[0] * sg
        return [act, a[1] * (sg * (1.0 + a[0] * (1.0 - sg))), act * a[1]]

    tm2 = tmx
    (act, up_dact, hmid), (g_down,) = _matmul("mm_gate_up", [(H2, W_gateT, 0), (H2, W_upT, 1)], 2, L, DF, D, "nt",
                                              (_pick(L, (2048, 1024, 512, 256, 128)), tF, D), [],
                                              [BF16, BF16, BF16], swiglu_epi, comm=[ag_down])
    W_down = _rows_full(g_down)
    f_ffn = _matmul("mm_down", [(hmid, W_down, 0)], 1, L, D, DF, "nn",
                    (tm2, _pick(D, (1024, 512)), _pick(DF, (2816, 512, 256, 128))), [], [BF16], ident)[0]

    dx2, dFf, sums_l = _loss_head(x1, tgt, norm_final.reshape(1, D), f_ffn, gt_f)

    def dswiglu_epi(a, e):
        return [a[0] * e[0].astype(F32), a[0] * e[1].astype(F32)]

    dga, dup = _matmul("mm_d_down", [(dFf, W_down, 0)], 1, L, DF, D, "nt", (_pick(L, (2048, 1024, 512, 256, 128)), tF, D),
                       [(up_dact, "mn"), (act, "mn")], [BF16, BF16], dswiglu_epi)
    tkt, tkl = _pick(L, (512, 256, 128)), _pick(L, (1024, 512, 256, 128))
    dW_down = _matmul("mm_gw_down", [(hmid, dFf, 0)], 1, DF, D, L, "tn",
                      (_pick(DF, (1408, 512, 256, 128)), tD, tkl), [], [BF16], ident)[0]
    (dW_gateT, dW_upT), (p_down,) = _matmul("mm_gw_gate_up", [(dga, H2, 0), (dup, H2, 1)], 2, DF, D, L, "tn",
                                            (tF, tD, _pick(L, (2048, 1024, 512, 256, 128))), [], [BF16, BF16], ident,
                                            comm=[("a2a", _rows_slots(dW_down))])
    (dH2,), (p_gate,) = _matmul("mm_d_gate_up", [(dga, W_gateT, 0), (dup, W_upT, 0)], 1, L, D, DF, "nn",
                                (_pick(L, (2048, 1024, 512, 256, 128)), tD, tF), [], [BF16], ident,
                                comm=[("a2a", _rows_slots(dW_gateT))])
    dx1, dFm, sums_f = _modulate_bwd("mod_ffn_bwd", x1, None, dH2, norm_ffn, ffn_mod, None, dx2, f_mix, gt_m)

    tO = _pick(KO, (2048, 1024, 512))
    dY = _matmul("mm_d_out", [(dFm, W_out, 0)], 1, L, KO, D, "nt", (tmx, tO, D), [], [BF16], ident)[0]
    dW_out = _matmul("mm_gw_out", [(Y, dFm, 0)], 1, KO, D, L, "tn",
                     (_pick(KO, (1024, 512)), tD, _pick(L, (2048, 1024, 512, 256, 128))), [], [BF16],
                     ident)[0]
    dO, drg = _ret_finish_bwd(o_f, o_b, P, dY, L)
    (dqf, dkf, dvf, dqb, dkb, dvb, ds0, dlg), (p_out,) = _ret_bwd(
        P, rope_ret, tabs, st_f, st_b, dO, L, comm=[("a2a", _rows_slots(dW_out))])
    dck, dcv, d_rd = _ret_ctx_bwd(P, ctab, ds0, dlg, rdb, L, Lc)
    (daq, dkl, dvl, dkx, dvx, d_sink), (p_up,) = _att_bwd(
        P, rope_att, sinkb, Y, dY, L, Lc, comm=[("a2a", _rows_slots(dW_upT))])
    dP = _assemble_dp(L, Lc, dqf, dqb, dkf, dkb, dvf, dvb, drg, daq, dkl, dvl, rope_att, dck, dcv, dkx, dvx)
    tkc = _pick(L + Lc, (768, 256, 128))
    dW_inT = _matmul("mm_gw_in", [(dP, H, 0)], 1, D_PROJ, D, L + Lc, "tn",
                     (_pick(D_PROJ, (2304, 1152, 768, 512)), tD, tkc), [], [BF16], ident)[0]
    (dH,), (p_in,) = _matmul("mm_d_in", [(dP, W_inT, 0)], 1, L + Lc, D, D_PROJ, "nn",
                             (tm, tD, _pick(D_PROJ, (768, 512, 256))), [], [BF16], ident,
                             comm=[("a2a", _rows_slots(dW_inT))])
    grad_x, sums_m = _modulate_bwd("mod_mix_bwd", xs, cx, dH, norm_mix, mix_mod, modc, dx1, None, None)

    zero = jnp.zeros((1, D), F32)
    dmod = jnp.concatenate([sums_m[0:1], sums_m[1:2], sums_f[6:7], sums_f[0:1], sums_f[1:2], sums_l[2:3]], axis=1)
    dmodc = jnp.concatenate([sums_m[3:4], sums_m[4:5], zero, zero, zero, zero], axis=1)
    dm_all = _allgather(jnp.concatenate([dmod, dmodc], axis=0), "ag_dmod")
    dm_cols = lax.dynamic_slice_in_dim(dm_all, me * C6, C6, axis=2)
    dm_in = jnp.concatenate([dm_cols[:, 0, :], dm_cols[:, 1, :]], axis=0)
    s_bwd = jnp.concatenate([cs, jnp.broadcast_to(c_ctx[None, :], (N_DEV, D))], axis=0)
    g_w_mod, dsil = _mod_bwd(s_bwd, dm_in, w_mod[0])

    lane_pad = lambda a: _pad_rows(a.reshape(-1, 1), LANES).reshape(1, LANES)
    pack = jnp.concatenate([dsil[0:1], sums_m[2:3], sums_f[2:3], sums_l[1:2],
                            lane_pad(d_rd[:, 0]), lane_pad(d_sink[:, 0]), sums_l[3:4, 0:LANES]], axis=1)
    packs = _allgather(pack, "ag_small")
    zl = jnp.zeros((1, LANES), F32)

    def pack_w(a_c, a_nm, a_nf, a_fin, a_rd, a_sk):
        return jnp.concatenate([a_c.reshape(1, D), a_nm, a_nf, a_fin.reshape(1, D), lane_pad(a_rd.reshape(-1)),
                                lane_pad(a_sk.reshape(-1)), zl], axis=1)

    sg, sd, sm, sv = _adam("adam_small", pack_w(c_ctx, norm_mix, norm_ffn, norm_final, ret_decay, attn_sink),
                           pack_w(m_c_ctx, m_norm_mix, m_norm_ffn, m_norm_final, m_ret_decay, m_attn_sink),
                           pack_w(v_c_ctx, v_norm_mix, v_norm_ffn, v_norm_final, v_ret_decay, v_attn_sink),
                           parts=packs)
    loss = sg[0, 4 * D + 2 * LANES]

    def unpack(a):
        return (a[0, 0:D], a[:, D:2 * D], a[:, 2 * D:3 * D], a[0, 3 * D:4 * D],
                a[0, 4 * D:4 * D + 2 * RET_HEADS].reshape(1, 2, RET_HEADS),
                a[:, 4 * D + LANES:4 * D + LANES + ATT_HEADS])

    bg, bd, bm, bv = _adam("adam_b_mod", b_mod, m_b_mod, v_b_mod, parts=dm_all.reshape(2 * N_DEV, 1, 6 * D))
    wg, wd, wm, wv = _adam("adam_w_mod", w_mod[0], m_w_mod[0], v_w_mod[0], g=g_w_mod)

    big = {}
    for nm, w, m, v, parts, transposed in (
            ("w_in", w_in, m_w_in, v_w_in, p_in, True), ("w_out", w_out, m_w_out, v_w_out, p_out, False),
            ("w_gate", w_gate, m_w_gate, v_w_gate, p_gate, True), ("w_up", w_up, m_w_up, v_w_up, p_up, True),
            ("w_down", w_down, m_w_down, v_w_down, p_down, False)):
        if transposed:
            res = [a.T for a in _adam("adam_" + nm, w[0].T, m[0].T, v[0].T, parts=parts)]
        else:
            res = _adam("adam_" + nm, w[0], m[0], v[0], parts=parts)
        big[nm] = [a[None] for a in res]

    g_s, d_s, m_s, v_s = unpack(sg), unpack(sd), unpack(sm), unpack(sv)

    def leaves(k, small, bmod, wmod):
        return (small[0], wmod[None], bmod, small[1], small[2], big["w_in"][k], small[4], small[5],
                big["w_out"][k], big["w_gate"][k], big["w_up"][k], big["w_down"][k], small[3])

    return (loss, grad_x[None], *leaves(0, g_s, bg, wg), *leaves(1, d_s, bd, wd),
            *leaves(2, m_s, bm, wm), *leaves(3, v_s, bv, wv))
```
